```python
import math
import jax
import jax.numpy as jnp
from jax import lax
import numpy as np

D_MODEL = 2048
BATCH = 8
SEQ = 4096
DEPTH = 2

GRID_W = 64
CTX_LEN = 256
N_EVEN = (DEPTH + 1) // 2
N_ODD = DEPTH // 2
N_MOD = 6
SSM_WIDTH = D_MODEL // 2
SSM_GROUP = 16
SSM_GROUPS = SSM_WIDTH // SSM_GROUP
SSM_STATE = 64
N_DIR = 2
DT_MIN = 1e-3
DT_MAX = 1e-1
NA_HEAD_DIM = 128
NA_WIDTH = D_MODEL // 2
NA_HEADS = NA_WIDTH // NA_HEAD_DIM
NA_WIN_R = 8
NA_WIN_C = 16
IN_WIDTH = SSM_WIDTH + 3 * NA_WIDTH
MIX_OUT = SSM_WIDTH + NA_WIDTH
CONV_WIDTH = D_MODEL
CONV_K = 31
FFN_DIM = 5632
FFN_CONV_K = 3
EPS = 1e-6

kernel_name = 'hybrid_s5_natten_conformer_dit'


def rms_norm(x, g):
    xf = x.astype(jnp.float32)
    y = xf * lax.rsqrt(jnp.mean(jnp.square(xf), axis=-1, keepdims=True) + EPS)
    return (y * g.astype(jnp.float32)).astype(x.dtype)


def layer_norm(x, g, b):
    xf = x.astype(jnp.float32)
    mu = jnp.mean(xf, axis=-1, keepdims=True)
    var = jnp.mean(jnp.square(xf - mu), axis=-1, keepdims=True)
    y = (xf - mu) * lax.rsqrt(var + EPS)
    return (y * g.astype(jnp.float32) + b.astype(jnp.float32)).astype(x.dtype)


def modulate(h, shift, scale):
    return h * (1.0 + scale) + shift


def dwconv1d(x, w, b):
    k = w.shape[0]
    pad = (k - 1) // 2
    y = lax.conv_general_dilated(
        x, w[:, None, :].astype(x.dtype), window_strides=(1,), padding=[(pad, pad)],
        dimension_numbers=('NWC', 'WIO', 'NWC'), feature_group_count=x.shape[-1])
    return y + b.astype(x.dtype)


def s5_discretize(lam_re, lam_im, log_dt, b_re, b_im):
    lam_re = lam_re.astype(jnp.float32)
    lam_im = lam_im.astype(jnp.float32)
    b_re = b_re.astype(jnp.float32)
    b_im = b_im.astype(jnp.float32)
    dt = jnp.exp(log_dt.astype(jnp.float32))[:, None]
    mag = jnp.exp(lam_re * dt)
    a_re = mag * jnp.cos(lam_im * dt)
    a_im = mag * jnp.sin(lam_im * dt)
    den = jnp.square(lam_re) + jnp.square(lam_im)
    f_re = ((a_re - 1.0) * lam_re + a_im * lam_im) / den
    f_im = (a_im * lam_re - (a_re - 1.0) * lam_im) / den
    bb_re = f_re[..., None] * b_re - f_im[..., None] * b_im
    bb_im = f_re[..., None] * b_im + f_im[..., None] * b_re
    return a_re, a_im, bb_re, bb_im


def _complex_scan_op(left, right):
    a1r, a1i, h1r, h1i = left
    a2r, a2i, h2r, h2i = right
    return (a2r * a1r - a2i * a1i, a2r * a1i + a2i * a1r,
            a2r * h1r - a2i * h1i + h2r, a2r * h1i + a2i * h1r + h2i)


def s5_states(u, a_re, a_im, bb_re, bb_im, h0=None):
    x_re = jnp.einsum('blgc,gpc->blgp', u, bb_re)
    x_im = jnp.einsum('blgc,gpc->blgp', u, bb_im)
    if h0 is not None:
        h0_re, h0_im = h0
        x_re = x_re.at[:, 0].add(a_re * h0_re - a_im * h0_im)
        x_im = x_im.at[:, 0].add(a_re * h0_im + a_im * h0_re)
    seq_len = u.shape[1]
    ar = jnp.broadcast_to(a_re, (1, seq_len) + a_re.shape)
    ai = jnp.broadcast_to(a_im, (1, seq_len) + a_im.shape)
    _, _, h_re, h_im = lax.associative_scan(_complex_scan_op, (ar, ai, x_re, x_im), axis=1)
    return h_re, h_im


def s5_readout(h_re, h_im, c_re, c_im):
    return jnp.einsum('blgp,gcp->blgc', h_re, c_re) - jnp.einsum('blgp,gcp->blgc', h_im, c_im)


def _orient(t, d):
    return jnp.flip(t, axis=1) if d == 1 else t


def s5_mixer(u_lat, u_ctx, lam_re, lam_im, log_dt, b_re, b_im, c_re, c_im, d_skip, w_glu, ctx_out):
    dtype = u_lat.dtype
    bsz, seq_len, _ = u_lat.shape
    ctx_len = u_ctx.shape[1]
    ul = u_lat.astype(jnp.float32)
    uc = u_ctx.astype(jnp.float32)
    ul_g = ul.reshape(bsz, seq_len, SSM_GROUPS, SSM_GROUP)
    uc_g = uc.reshape(bsz, ctx_len, SSM_GROUPS, SSM_GROUP)
    d32 = d_skip.astype(jnp.float32)
    y_lat = d32 * ul
    y_ctx = d32 * uc if ctx_out else None
    for d in range(N_DIR):
        a_re, a_im, bb_re, bb_im = s5_discretize(lam_re[d], lam_im[d], log_dt[d], b_re[d], b_im[d])
        cr = c_re[d].astype(jnp.float32)
        ci = c_im[d].astype(jnp.float32)
        hc_re, hc_im = s5_states(_orient(uc_g, d), a_re, a_im, bb_re, bb_im)
        hl_re, hl_im = s5_states(_orient(ul_g, d), a_re, a_im, bb_re, bb_im,
                                 h0=(hc_re[:, -1], hc_im[:, -1]))
        y_lat = y_lat + _orient(s5_readout(hl_re, hl_im, cr, ci), d).reshape(bsz, seq_len, SSM_WIDTH)
        if ctx_out:
            y_ctx = y_ctx + _orient(s5_readout(hc_re, hc_im, cr, ci), d).reshape(bsz, ctx_len, SSM_WIDTH)
    w32 = w_glu.astype(jnp.float32)

    def glu(y):
        z = jax.nn.gelu(y)
        return (z * jax.nn.sigmoid(z @ w32)).astype(dtype)

    return glu(y_lat), (glu(y_ctx) if ctx_out else None)


def na_latent(q, k, v, k_ctx, v_ctx, rpb):
    bsz, seq_len, n_heads, head_dim = q.shape
    rows = seq_len // GRID_W
    wr = min(NA_WIN_R, rows)
    r = jnp.arange(rows)
    key_rows = jnp.clip(r - wr // 2, 0, rows - wr)[:, None] + jnp.arange(wr)[None, :]
    col = jnp.arange(GRID_W)
    col_start = jnp.clip(col - NA_WIN_C // 2, 0, GRID_W - NA_WIN_C)
    col_ok = (col[None, :] >= col_start[:, None]) & (col[None, :] < col_start[:, None] + NA_WIN_C)
    row_idx = (key_rows - r[:, None]) + (NA_WIN_R - 1)
    col_idx = jnp.clip(col[None, :] - col[:, None] + (NA_WIN_C - 1), 0, 2 * NA_WIN_C - 2)
    bias = rpb.astype(jnp.float32)[:, row_idx[:, None, :, None], col_idx[None, :, None, :]]

    qg = q.reshape(bsz, rows, GRID_W, n_heads, head_dim)
    kg = k.reshape(bsz, rows, GRID_W, n_heads, head_dim)[:, key_rows]
    vg = v.reshape(bsz, rows, GRID_W, n_heads, head_dim)[:, key_rows]
    scale = head_dim ** -0.5
    s_loc = jnp.einsum('brqhd,brikhd->bhrqik', qg, kg, preferred_element_type=jnp.float32) * scale + bias[None]
    s_loc = jnp.where(col_ok[:, None, :], s_loc, jnp.finfo(jnp.float32).min)
    n_loc = wr * GRID_W
    s_loc = s_loc.reshape(bsz, n_heads, rows, GRID_W, n_loc)
    s_ctx = jnp.einsum('brqhd,bchd->bhrqc', qg, k_ctx, preferred_element_type=jnp.float32) * scale
    p = jax.nn.softmax(jnp.concatenate([s_loc, s_ctx], axis=-1), axis=-1).astype(v.dtype)
    p_loc = p[..., :n_loc].reshape(bsz, n_heads, rows, GRID_W, wr, GRID_W)
    p_ctx = p[..., n_loc:]
    o = (jnp.einsum('bhrqik,brikhd->brqhd', p_loc, vg)
         + jnp.einsum('bhrqc,bchd->brqhd', p_ctx, v_ctx))
    return o.reshape(bsz, seq_len, n_heads * head_dim)


def ctx_attention(q, k, v):
    bsz, ctx_len, n_heads, head_dim = q.shape
    s = jnp.einsum('bqhd,bkhd->bhqk', q, k, preferred_element_type=jnp.float32) * head_dim ** -0.5
    p = jax.nn.softmax(s, axis=-1).astype(v.dtype)
    return jnp.einsum('bhqk,bkhd->bqhd', p, v).reshape(bsz, ctx_len, n_heads * head_dim)


def _heads(t):
    return t.reshape(t.shape[0], t.shape[1], NA_HEADS, NA_HEAD_DIM)


def hybrid_mixer(h, hc, w_in, lam_re, lam_im, log_dt, b_re, b_im, c_re, c_im, d_skip, w_glu, rpb, w_out, ctx_out):
    splits = [SSM_WIDTH, SSM_WIDTH + NA_WIDTH, SSM_WIDTH + 2 * NA_WIDTH]
    u, q, k, v = jnp.split(h @ w_in, splits, axis=-1)
    if ctx_out:
        uc, qc, kc, vc = jnp.split(hc @ w_in, splits, axis=-1)
    else:
        uc = hc @ w_in[:, :SSM_WIDTH]
        kc, vc = jnp.split(hc @ w_in[:, SSM_WIDTH + NA_WIDTH:], 2, axis=-1)
    y_s5, y_s5_c = s5_mixer(u, uc, lam_re, lam_im, log_dt, b_re, b_im, c_re, c_im, d_skip, w_glu, ctx_out)
    y_na = na_latent(_heads(q), _heads(k), _heads(v), _heads(kc), _heads(vc), rpb)
    y = jnp.concatenate([y_s5, y_na], axis=-1) @ w_out
    y_c = None
    if ctx_out:
        y_na_c = ctx_attention(_heads(qc), _heads(kc), _heads(vc))
        y_c = jnp.concatenate([y_s5_c, y_na_c], axis=-1) @ w_out
    return y, y_c


def conformer_conv(h, w_pw1, dw_w, dw_b, ln_g, ln_b, w_pw2):
    a, g = jnp.split(h @ w_pw1, 2, axis=-1)
    z = dwconv1d(a * jax.nn.sigmoid(g), dw_w, dw_b)
    z = jax.nn.silu(layer_norm(z, ln_g, ln_b))
    return z @ w_pw2


def conv_ffn(h, w_up, conv_w, conv_b, w_down):
    u, g = jnp.split(dwconv1d(h @ w_up, conv_w, conv_b), 2, axis=-1)
    return (jax.nn.silu(g) * u) @ w_down


def _fwd_setup_inputs(seed: int = 0) -> dict:
    key = jax.random.key(seed)
    ks = iter(jax.random.split(key, 48))

    def nrm(shape, std):
        return jax.random.normal(next(ks), shape, jnp.float32) * std

    D = D_MODEL
    G, P, Cg = SSM_GROUPS, SSM_STATE, SSM_GROUP
    n = jnp.arange(P, dtype=jnp.float32)
    return {
        'x': nrm((BATCH, SEQ, D), 1.0),
        'c': nrm((BATCH, D), 1.0),
        'ctx': nrm((BATCH, CTX_LEN, D), 1.0),
        'c_ctx': nrm((D,), 1.0),
        'w_mod': nrm((DEPTH, D, N_MOD * D), 0.5 * D ** -0.5),
        'b_mod': nrm((DEPTH, N_MOD * D), 0.02),
        'g_mix': 1.0 + nrm((DEPTH, D), 0.02),
        'g_ffn': 1.0 + nrm((DEPTH, D), 0.02),
        'w_in': nrm((N_EVEN, D, IN_WIDTH), D ** -0.5),
        'ssm_lam_re': -0.5 + nrm((N_EVEN, N_DIR, G, P), 0.01),
        'ssm_lam_im': math.pi * n + nrm((N_EVEN, N_DIR, G, P), 0.01),
        'ssm_log_dt': jax.random.uniform(next(ks), (N_EVEN, N_DIR, G), jnp.float32,
                                         math.log(DT_MIN), math.log(DT_MAX)),
        'ssm_b_re': nrm((N_EVEN, N_DIR, G, P, Cg), (2 * Cg) ** -0.5),
        'ssm_b_im': nrm((N_EVEN, N_DIR, G, P, Cg), (2 * Cg) ** -0.5),
        'ssm_c_re': nrm((N_EVEN, N_DIR, G, Cg, P), 0.5),
        'ssm_c_im': nrm((N_EVEN, N_DIR, G, Cg, P), 0.5),
        'ssm_d': nrm((N_EVEN, SSM_WIDTH), 1.0),
        'ssm_w_glu': nrm((N_EVEN, SSM_WIDTH, SSM_WIDTH), SSM_WIDTH ** -0.5),
        'na_rpb': nrm((N_EVEN, NA_HEADS, 2 * NA_WIN_R - 1, 2 * NA_WIN_C - 1), 0.1),
        'w_out': nrm((N_EVEN, MIX_OUT, D), MIX_OUT ** -0.5),
        'cv_w_pw1': nrm((N_ODD, D, 2 * CONV_WIDTH), D ** -0.5),
        'cv_dw_w': nrm((N_ODD, CONV_K, CONV_WIDTH), CONV_K ** -0.5),
        'cv_dw_b': nrm((N_ODD, CONV_WIDTH), 0.02),
        'cv_ln_g': 1.0 + nrm((N_ODD, CONV_WIDTH), 0.02),
        'cv_ln_b': nrm((N_ODD, CONV_WIDTH), 0.02),
        'cv_w_pw2': nrm((N_ODD, CONV_WIDTH, D), CONV_WIDTH ** -0.5),
        'ffn_w_up': nrm((DEPTH, D, 2 * FFN_DIM), D ** -0.5),
        'ffn_conv_w': nrm((DEPTH, FFN_CONV_K, 2 * FFN_DIM), FFN_CONV_K ** -0.5),
        'ffn_conv_b': nrm((DEPTH, 2 * FFN_DIM), 0.02),
        'ffn_w_down': nrm((DEPTH, FFN_DIM, D), FFN_DIM ** -0.5),
        'g_out': 1.0 + nrm((D,), 0.02),
    }


def _fwd_reference(x, c, ctx, c_ctx, w_mod, b_mod, g_mix, g_ffn, w_in, ssm_lam_re, ssm_lam_im, ssm_log_dt,
              ssm_b_re, ssm_b_im, ssm_c_re, ssm_c_im, ssm_d, ssm_w_glu, na_rpb, w_out,
              cv_w_pw1, cv_dw_w, cv_dw_b, cv_ln_g, cv_ln_b, cv_w_pw2,
              ffn_w_up, ffn_conv_w, ffn_conv_b, ffn_w_down, g_out):
    x_ctx = ctx
    s_lat = jax.nn.silu(c)[:, None, :]
    s_ctx = jax.nn.silu(c_ctx)[None, None, :]
    for i in range(DEPTH):
        reads_ctx = (i % 2 == 0)
        ctx_next = any(j % 2 == 0 for j in range(i + 1, DEPTH))
        mod = jnp.split(s_lat @ w_mod[i] + b_mod[i], N_MOD, axis=-1)
        h = modulate(rms_norm(x, g_mix[i]), mod[0], mod[1])
        if reads_ctx or ctx_next:
            mod_c = jnp.split(s_ctx @ w_mod[i] + b_mod[i], N_MOD, axis=-1)
            hc = modulate(rms_norm(x_ctx, g_mix[i]), mod_c[0], mod_c[1])
        if reads_ctx:
            e = i // 2
            y, yc = hybrid_mixer(h, hc, w_in[e], ssm_lam_re[e], ssm_lam_im[e], ssm_log_dt[e],
                                 ssm_b_re[e], ssm_b_im[e], ssm_c_re[e], ssm_c_im[e], ssm_d[e],
                                 ssm_w_glu[e], na_rpb[e], w_out[e], ctx_next)
        else:
            o = i // 2
            y = conformer_conv(h, cv_w_pw1[o], cv_dw_w[o], cv_dw_b[o], cv_ln_g[o], cv_ln_b[o], cv_w_pw2[o])
            yc = (conformer_conv(hc, cv_w_pw1[o], cv_dw_w[o], cv_dw_b[o], cv_ln_g[o], cv_ln_b[o], cv_w_pw2[o])
                  if ctx_next else None)
        x = x + mod[2] * y
        h = modulate(rms_norm(x, g_ffn[i]), mod[3], mod[4])
        x = x + mod[5] * conv_ffn(h, ffn_w_up[i], ffn_conv_w[i], ffn_conv_b[i], ffn_w_down[i])
        if ctx_next:
            x_ctx = x_ctx + mod_c[2] * yc
            hc = modulate(rms_norm(x_ctx, g_ffn[i]), mod_c[3], mod_c[4])
            x_ctx = x_ctx + mod_c[5] * conv_ffn(hc, ffn_w_up[i], ffn_conv_w[i], ffn_conv_b[i], ffn_w_down[i])
    return rms_norm(x, g_out)


import jax as _jax
import jax.numpy as _jnp

TWIN_FORMAT = 'train_step'
FWD_PARAMS = ['x', 'c', 'ctx', 'c_ctx', 'w_mod', 'b_mod', 'g_mix', 'g_ffn', 'w_in', 'ssm_lam_re', 'ssm_lam_im', 'ssm_log_dt', 'ssm_b_re', 'ssm_b_im', 'ssm_c_re', 'ssm_c_im', 'ssm_d', 'ssm_w_glu', 'na_rpb', 'w_out', 'cv_w_pw1', 'cv_dw_w', 'cv_dw_b', 'cv_ln_g', 'cv_ln_b', 'cv_w_pw2', 'ffn_w_up', 'ffn_conv_w', 'ffn_conv_b', 'ffn_w_down', 'g_out']
TWIN_WEIGHTS = ['c_ctx', 'w_mod', 'b_mod', 'g_mix', 'g_ffn', 'w_in', 'ssm_lam_re', 'ssm_lam_im', 'ssm_log_dt', 'ssm_b_re', 'ssm_b_im', 'ssm_c_re', 'ssm_c_im', 'ssm_d', 'ssm_w_glu', 'na_rpb', 'w_out', 'cv_w_pw1', 'cv_dw_w', 'cv_dw_b', 'cv_ln_g', 'cv_ln_b', 'cv_w_pw2', 'ffn_w_up', 'ffn_conv_w', 'ffn_conv_b', 'ffn_w_down', 'g_out']
TWIN_DIFF_INPUT = 'x'
TWIN_INPUTS = ['x', 'c', 'ctx', 'c_ctx', 'w_mod', 'b_mod', 'g_mix', 'g_ffn', 'w_in', 'ssm_lam_re', 'ssm_lam_im', 'ssm_log_dt', 'ssm_b_re', 'ssm_b_im', 'ssm_c_re', 'ssm_c_im', 'ssm_d', 'ssm_w_glu', 'na_rpb', 'w_out', 'cv_w_pw1', 'cv_dw_w', 'cv_dw_b', 'cv_ln_g', 'cv_ln_b', 'cv_w_pw2', 'ffn_w_up', 'ffn_conv_w', 'ffn_conv_b', 'ffn_w_down', 'g_out', 'loss_target', 'm_c_ctx', 'm_w_mod', 'm_b_mod', 'm_g_mix', 'm_g_ffn', 'm_w_in', 'm_ssm_lam_re', 'm_ssm_lam_im', 'm_ssm_log_dt', 'm_ssm_b_re', 'm_ssm_b_im', 'm_ssm_c_re', 'm_ssm_c_im', 'm_ssm_d', 'm_ssm_w_glu', 'm_na_rpb', 'm_w_out', 'm_cv_w_pw1', 'm_cv_dw_w', 'm_cv_dw_b', 'm_cv_ln_g', 'm_cv_ln_b', 'm_cv_w_pw2', 'm_ffn_w_up', 'm_ffn_conv_w', 'm_ffn_conv_b', 'm_ffn_w_down', 'm_g_out', 'v_c_ctx', 'v_w_mod', 'v_b_mod', 'v_g_mix', 'v_g_ffn', 'v_w_in', 'v_ssm_lam_re', 'v_ssm_lam_im', 'v_ssm_log_dt', 'v_ssm_b_re', 'v_ssm_b_im', 'v_ssm_c_re', 'v_ssm_c_im', 'v_ssm_d', 'v_ssm_w_glu', 'v_na_rpb', 'v_w_out', 'v_cv_w_pw1', 'v_cv_dw_w', 'v_cv_dw_b', 'v_cv_ln_g', 'v_cv_ln_b', 'v_cv_w_pw2', 'v_ffn_w_up', 'v_ffn_conv_w', 'v_ffn_conv_b', 'v_ffn_w_down', 'v_g_out']
TWIN_OUTPUTS = ['loss', 'grad_x', 'grad_c_ctx', 'grad_w_mod', 'grad_b_mod', 'grad_g_mix', 'grad_g_ffn', 'grad_w_in', 'grad_ssm_lam_re', 'grad_ssm_lam_im', 'grad_ssm_log_dt', 'grad_ssm_b_re', 'grad_ssm_b_im', 'grad_ssm_c_re', 'grad_ssm_c_im', 'grad_ssm_d', 'grad_ssm_w_glu', 'grad_na_rpb', 'grad_w_out', 'grad_cv_w_pw1', 'grad_cv_dw_w', 'grad_cv_dw_b', 'grad_cv_ln_g', 'grad_cv_ln_b', 'grad_cv_w_pw2', 'grad_ffn_w_up', 'grad_ffn_conv_w', 'grad_ffn_conv_b', 'grad_ffn_w_down', 'grad_g_out', 'delta_c_ctx', 'delta_w_mod', 'delta_b_mod', 'delta_g_mix', 'delta_g_ffn', 'delta_w_in', 'delta_ssm_lam_re', 'delta_ssm_lam_im', 'delta_ssm_log_dt', 'delta_ssm_b_re', 'delta_ssm_b_im', 'delta_ssm_c_re', 'delta_ssm_c_im', 'delta_ssm_d', 'delta_ssm_w_glu', 'delta_na_rpb', 'delta_w_out', 'delta_cv_w_pw1', 'delta_cv_dw_w', 'delta_cv_dw_b', 'delta_cv_ln_g', 'delta_cv_ln_b', 'delta_cv_w_pw2', 'delta_ffn_w_up', 'delta_ffn_conv_w', 'delta_ffn_conv_b', 'delta_ffn_w_down', 'delta_g_out', 'new_m_c_ctx', 'new_m_w_mod', 'new_m_b_mod', 'new_m_g_mix', 'new_m_g_ffn', 'new_m_w_in', 'new_m_ssm_lam_re', 'new_m_ssm_lam_im', 'new_m_ssm_log_dt', 'new_m_ssm_b_re', 'new_m_ssm_b_im', 'new_m_ssm_c_re', 'new_m_ssm_c_im', 'new_m_ssm_d', 'new_m_ssm_w_glu', 'new_m_na_rpb', 'new_m_w_out', 'new_m_cv_w_pw1', 'new_m_cv_dw_w', 'new_m_cv_dw_b', 'new_m_cv_ln_g', 'new_m_cv_ln_b', 'new_m_cv_w_pw2', 'new_m_ffn_w_up', 'new_m_ffn_conv_w', 'new_m_ffn_conv_b', 'new_m_ffn_w_down', 'new_m_g_out', 'new_v_c_ctx', 'new_v_w_mod', 'new_v_b_mod', 'new_v_g_mix', 'new_v_g_ffn', 'new_v_w_in', 'new_v_ssm_lam_re', 'new_v_ssm_lam_im', 'new_v_ssm_log_dt', 'new_v_ssm_b_re', 'new_v_ssm_b_im', 'new_v_ssm_c_re', 'new_v_ssm_c_im', 'new_v_ssm_d', 'new_v_ssm_w_glu', 'new_v_na_rpb', 'new_v_w_out', 'new_v_cv_w_pw1', 'new_v_cv_dw_w', 'new_v_cv_dw_b', 'new_v_cv_ln_g', 'new_v_cv_ln_b', 'new_v_cv_w_pw2', 'new_v_ffn_w_up', 'new_v_ffn_conv_w', 'new_v_ffn_conv_b', 'new_v_ffn_w_down', 'new_v_g_out']
TWIN_LEAF_KINDS = {'loss': 'loss', 'grad_x': 'grad_x', 'grad_c_ctx': 'grad_w', 'grad_w_mod': 'grad_w', 'grad_b_mod': 'grad_w', 'grad_g_mix': 'grad_w', 'grad_g_ffn': 'grad_w', 'grad_w_in': 'grad_w', 'grad_ssm_lam_re': 'grad_w', 'grad_ssm_lam_im': 'grad_w', 'grad_ssm_log_dt': 'grad_w', 'grad_ssm_b_re': 'grad_w', 'grad_ssm_b_im': 'grad_w', 'grad_ssm_c_re': 'grad_w', 'grad_ssm_c_im': 'grad_w', 'grad_ssm_d': 'grad_w', 'grad_ssm_w_glu': 'grad_w', 'grad_na_rpb': 'grad_w', 'grad_w_out': 'grad_w', 'grad_cv_w_pw1': 'grad_w', 'grad_cv_dw_w': 'grad_w', 'grad_cv_dw_b': 'grad_w', 'grad_cv_ln_g': 'grad_w', 'grad_cv_ln_b': 'grad_w', 'grad_cv_w_pw2': 'grad_w', 'grad_ffn_w_up': 'grad_w', 'grad_ffn_conv_w': 'grad_w', 'grad_ffn_conv_b': 'grad_w', 'grad_ffn_w_down': 'grad_w', 'grad_g_out': 'grad_w', 'delta_c_ctx': 'delta_w', 'delta_w_mod': 'delta_w', 'delta_b_mod': 'delta_w', 'delta_g_mix': 'delta_w', 'delta_g_ffn': 'delta_w', 'delta_w_in': 'delta_w', 'delta_ssm_lam_re': 'delta_w', 'delta_ssm_lam_im': 'delta_w', 'delta_ssm_log_dt': 'delta_w', 'delta_ssm_b_re': 'delta_w', 'delta_ssm_b_im': 'delta_w', 'delta_ssm_c_re': 'delta_w', 'delta_ssm_c_im': 'delta_w', 'delta_ssm_d': 'delta_w', 'delta_ssm_w_glu': 'delta_w', 'delta_na_rpb': 'delta_w', 'delta_w_out': 'delta_w', 'delta_cv_w_pw1': 'delta_w', 'delta_cv_dw_w': 'delta_w', 'delta_cv_dw_b': 'delta_w', 'delta_cv_ln_g': 'delta_w', 'delta_cv_ln_b': 'delta_w', 'delta_cv_w_pw2': 'delta_w', 'delta_ffn_w_up': 'delta_w', 'delta_ffn_conv_w': 'delta_w', 'delta_ffn_conv_b': 'delta_w', 'delta_ffn_w_down': 'delta_w', 'delta_g_out': 'delta_w', 'new_m_c_ctx': 'new_m', 'new_m_w_mod': 'new_m', 'new_m_b_mod': 'new_m', 'new_m_g_mix': 'new_m', 'new_m_g_ffn': 'new_m', 'new_m_w_in': 'new_m', 'new_m_ssm_lam_re': 'new_m', 'new_m_ssm_lam_im': 'new_m', 'new_m_ssm_log_dt': 'new_m', 'new_m_ssm_b_re': 'new_m', 'new_m_ssm_b_im': 'new_m', 'new_m_ssm_c_re': 'new_m', 'new_m_ssm_c_im': 'new_m', 'new_m_ssm_d': 'new_m', 'new_m_ssm_w_glu': 'new_m', 'new_m_na_rpb': 'new_m', 'new_m_w_out': 'new_m', 'new_m_cv_w_pw1': 'new_m', 'new_m_cv_dw_w': 'new_m', 'new_m_cv_dw_b': 'new_m', 'new_m_cv_ln_g': 'new_m', 'new_m_cv_ln_b': 'new_m', 'new_m_cv_w_pw2': 'new_m', 'new_m_ffn_w_up': 'new_m', 'new_m_ffn_conv_w': 'new_m', 'new_m_ffn_conv_b': 'new_m', 'new_m_ffn_w_down': 'new_m', 'new_m_g_out': 'new_m', 'new_v_c_ctx': 'new_v', 'new_v_w_mod': 'new_v', 'new_v_b_mod': 'new_v', 'new_v_g_mix': 'new_v', 'new_v_g_ffn': 'new_v', 'new_v_w_in': 'new_v', 'new_v_ssm_lam_re': 'new_v', 'new_v_ssm_lam_im': 'new_v', 'new_v_ssm_log_dt': 'new_v', 'new_v_ssm_b_re': 'new_v', 'new_v_ssm_b_im': 'new_v', 'new_v_ssm_c_re': 'new_v', 'new_v_ssm_c_im': 'new_v', 'new_v_ssm_d': 'new_v', 'new_v_ssm_w_glu': 'new_v', 'new_v_na_rpb': 'new_v', 'new_v_w_out': 'new_v', 'new_v_cv_w_pw1': 'new_v', 'new_v_cv_dw_w': 'new_v', 'new_v_cv_dw_b': 'new_v', 'new_v_cv_ln_g': 'new_v', 'new_v_cv_ln_b': 'new_v', 'new_v_cv_w_pw2': 'new_v', 'new_v_ffn_w_up': 'new_v', 'new_v_ffn_conv_w': 'new_v', 'new_v_ffn_conv_b': 'new_v', 'new_v_ffn_w_down': 'new_v', 'new_v_g_out': 'new_v'}


def _forward(args):
    return _fwd_reference(*[args[k] for k in FWD_PARAMS])


def _output_shape():
    def fwd():
        inp = _fwd_setup_inputs(0)
        return _fwd_reference(*[inp[k] for k in FWD_PARAMS])
    out = _jax.eval_shape(fwd)
    return out.shape, out.dtype

N_MICROBATCH = 1
ADAM_LR = 0.001
ADAM_B1 = 0.9
ADAM_B2 = 0.999
ADAM_EPS = 1e-08
ADAM_WD = 0.01
ADAM_STEP = 10
PER_EXAMPLE_BATCH_AXIS = {'x': 0, 'c': 0, 'ctx': 0, 'loss_target': 0}
SHARED_INPUTS = []
_WEIGHT_DTYPES = {'c_ctx': _jnp.float32, 'w_mod': _jnp.float32, 'b_mod': _jnp.float32, 'g_mix': _jnp.float32, 'g_ffn': _jnp.float32, 'w_in': _jnp.float32, 'ssm_lam_re': _jnp.float32, 'ssm_lam_im': _jnp.float32, 'ssm_log_dt': _jnp.float32, 'ssm_b_re': _jnp.float32, 'ssm_b_im': _jnp.float32, 'ssm_c_re': _jnp.float32, 'ssm_c_im': _jnp.float32, 'ssm_d': _jnp.float32, 'ssm_w_glu': _jnp.float32, 'na_rpb': _jnp.float32, 'w_out': _jnp.float32, 'cv_w_pw1': _jnp.float32, 'cv_dw_w': _jnp.float32, 'cv_dw_b': _jnp.float32, 'cv_ln_g': _jnp.float32, 'cv_ln_b': _jnp.float32, 'cv_w_pw2': _jnp.float32, 'ffn_w_up': _jnp.float32, 'ffn_conv_w': _jnp.float32, 'ffn_conv_b': _jnp.float32, 'ffn_w_down': _jnp.float32, 'g_out': _jnp.float32}
MOMENT_SCALE = {'c_ctx': 4.919061e-03, 'w_mod': 2.218145e-02, 'b_mod': 3.840725e-02, 'g_mix': 1.442291e-02, 'g_ffn': 2.669058e-02, 'w_in': 7.968726e-03, 'ssm_lam_re': 4.387887e-03, 'ssm_lam_im': 3.639048e-03, 'ssm_log_dt': 2.378714e+00, 'ssm_b_re': 2.596606e-03, 'ssm_b_im': 2.492582e-03, 'ssm_c_re': 9.026428e-04, 'ssm_c_im': 8.940405e-04, 'ssm_d': 1.171730e-02, 'ssm_w_glu': 4.400828e-03, 'na_rpb': 9.702864e-04, 'w_out': 1.010279e-02, 'cv_w_pw1': 1.238436e-02, 'cv_dw_w': 1.630396e-02, 'cv_dw_b': 2.743048e-02, 'cv_ln_g': 1.871718e-02, 'cv_ln_b': 1.590871e-02, 'cv_w_pw2': 1.565365e-02, 'ffn_w_up': 1.155453e-02, 'ffn_conv_w': 1.156359e-02, 'ffn_conv_b': 1.058824e-02, 'ffn_w_down': 1.884426e-02, 'g_out': 1.599909e+01}


def _to_microbatches(a, axis):
    t = _jnp.moveaxis(a, axis, 0)
    t = t.reshape((N_MICROBATCH, t.shape[0] // N_MICROBATCH) + t.shape[1:])
    return _jnp.moveaxis(t, 1, axis + 1)


def setup_inputs(seed: int = 0) -> dict:
    inp = _fwd_setup_inputs(seed)
    key = _jax.random.fold_in(_jax.random.key(seed), 7919)
    shape, _ = _output_shape()
    out = dict(inp)
    out["loss_target"] = _jax.random.normal(_jax.random.fold_in(key, 0), shape, _jnp.float32)
    for i, name in enumerate(TWIN_WEIGHTS):
        w = inp[name].astype(_jnp.float32)
        if MOMENT_SCALE is None:
            s = _jnp.sqrt(_jnp.mean(_jnp.square(w)) + 1e-30)
        else:
            s = MOMENT_SCALE[name]
        km, kv = _jax.random.split(_jax.random.fold_in(key, i + 1))
        out[name] = w
        out["m_" + name] = s * _jax.random.normal(km, w.shape, _jnp.float32)
        out["v_" + name] = (s * s) * _jax.random.uniform(kv, w.shape, _jnp.float32, 0.5, 1.5)
    if N_MICROBATCH > 1:
        for name, axis in PER_EXAMPLE_BATCH_AXIS.items():
            out[name] = _to_microbatches(out[name], axis)
    return {'x': out['x'], 'c': out['c'], 'ctx': out['ctx'], 'c_ctx': out['c_ctx'], 'w_mod': out['w_mod'], 'b_mod': out['b_mod'], 'g_mix': out['g_mix'], 'g_ffn': out['g_ffn'], 'w_in': out['w_in'], 'ssm_lam_re': out['ssm_lam_re'], 'ssm_lam_im': out['ssm_lam_im'], 'ssm_log_dt': out['ssm_log_dt'], 'ssm_b_re': out['ssm_b_re'], 'ssm_b_im': out['ssm_b_im'], 'ssm_c_re': out['ssm_c_re'], 'ssm_c_im': out['ssm_c_im'], 'ssm_d': out['ssm_d'], 'ssm_w_glu': out['ssm_w_glu'], 'na_rpb': out['na_rpb'], 'w_out': out['w_out'], 'cv_w_pw1': out['cv_w_pw1'], 'cv_dw_w': out['cv_dw_w'], 'cv_dw_b': out['cv_dw_b'], 'cv_ln_g': out['cv_ln_g'], 'cv_ln_b': out['cv_ln_b'], 'cv_w_pw2': out['cv_w_pw2'], 'ffn_w_up': out['ffn_w_up'], 'ffn_conv_w': out['ffn_conv_w'], 'ffn_conv_b': out['ffn_conv_b'], 'ffn_w_down': out['ffn_w_down'], 'g_out': out['g_out'], 'loss_target': out['loss_target'], 'm_c_ctx': out['m_c_ctx'], 'm_w_mod': out['m_w_mod'], 'm_b_mod': out['m_b_mod'], 'm_g_mix': out['m_g_mix'], 'm_g_ffn': out['m_g_ffn'], 'm_w_in': out['m_w_in'], 'm_ssm_lam_re': out['m_ssm_lam_re'], 'm_ssm_lam_im': out['m_ssm_lam_im'], 'm_ssm_log_dt': out['m_ssm_log_dt'], 'm_ssm_b_re': out['m_ssm_b_re'], 'm_ssm_b_im': out['m_ssm_b_im'], 'm_ssm_c_re': out['m_ssm_c_re'], 'm_ssm_c_im': out['m_ssm_c_im'], 'm_ssm_d': out['m_ssm_d'], 'm_ssm_w_glu': out['m_ssm_w_glu'], 'm_na_rpb': out['m_na_rpb'], 'm_w_out': out['m_w_out'], 'm_cv_w_pw1': out['m_cv_w_pw1'], 'm_cv_dw_w': out['m_cv_dw_w'], 'm_cv_dw_b': out['m_cv_dw_b'], 'm_cv_ln_g': out['m_cv_ln_g'], 'm_cv_ln_b': out['m_cv_ln_b'], 'm_cv_w_pw2': out['m_cv_w_pw2'], 'm_ffn_w_up': out['m_ffn_w_up'], 'm_ffn_conv_w': out['m_ffn_conv_w'], 'm_ffn_conv_b': out['m_ffn_conv_b'], 'm_ffn_w_down': out['m_ffn_w_down'], 'm_g_out': out['m_g_out'], 'v_c_ctx': out['v_c_ctx'], 'v_w_mod': out['v_w_mod'], 'v_b_mod': out['v_b_mod'], 'v_g_mix': out['v_g_mix'], 'v_g_ffn': out['v_g_ffn'], 'v_w_in': out['v_w_in'], 'v_ssm_lam_re': out['v_ssm_lam_re'], 'v_ssm_lam_im': out['v_ssm_lam_im'], 'v_ssm_log_dt': out['v_ssm_log_dt'], 'v_ssm_b_re': out['v_ssm_b_re'], 'v_ssm_b_im': out['v_ssm_b_im'], 'v_ssm_c_re': out['v_ssm_c_re'], 'v_ssm_c_im': out['v_ssm_c_im'], 'v_ssm_d': out['v_ssm_d'], 'v_ssm_w_glu': out['v_ssm_w_glu'], 'v_na_rpb': out['v_na_rpb'], 'v_w_out': out['v_w_out'], 'v_cv_w_pw1': out['v_cv_w_pw1'], 'v_cv_dw_w': out['v_cv_dw_w'], 'v_cv_dw_b': out['v_cv_dw_b'], 'v_cv_ln_g': out['v_cv_ln_g'], 'v_cv_ln_b': out['v_cv_ln_b'], 'v_cv_w_pw2': out['v_cv_w_pw2'], 'v_ffn_w_up': out['v_ffn_w_up'], 'v_ffn_conv_w': out['v_ffn_conv_w'], 'v_ffn_conv_b': out['v_ffn_conv_b'], 'v_ffn_w_down': out['v_ffn_w_down'], 'v_g_out': out['v_g_out']}


def _loss(weights, diff, rest, loss_target):
    with _jax.named_scope("forward"):
        args = {**rest, TWIN_DIFF_INPUT: diff, **{k: w.astype(_WEIGHT_DTYPES[k]) for k, w in weights.items()}}
        y = _forward(args)
    with _jax.named_scope("loss_head"):
        err = _jnp.square(y.astype(_jnp.float32) - loss_target)
        return 0.5 * _jnp.sum(_jnp.mean(err, axis=-1)) if err.ndim else 0.5 * err


def _adamw(w, g, m, v):
    m = ADAM_B1 * m + (1.0 - ADAM_B1) * g
    v = ADAM_B2 * v + (1.0 - ADAM_B2) * _jnp.square(g)
    m_hat = m / (1.0 - ADAM_B1 ** ADAM_STEP)
    v_hat = v / (1.0 - ADAM_B2 ** ADAM_STEP)
    delta = -ADAM_LR * (m_hat / (_jnp.sqrt(v_hat) + ADAM_EPS) + ADAM_WD * w)
    return delta, m, v


def reference(x, c, ctx, c_ctx, w_mod, b_mod, g_mix, g_ffn, w_in, ssm_lam_re, ssm_lam_im, ssm_log_dt, ssm_b_re, ssm_b_im, ssm_c_re, ssm_c_im, ssm_d, ssm_w_glu, na_rpb, w_out, cv_w_pw1, cv_dw_w, cv_dw_b, cv_ln_g, cv_ln_b, cv_w_pw2, ffn_w_up, ffn_conv_w, ffn_conv_b, ffn_w_down, g_out, loss_target, m_c_ctx, m_w_mod, m_b_mod, m_g_mix, m_g_ffn, m_w_in, m_ssm_lam_re, m_ssm_lam_im, m_ssm_log_dt, m_ssm_b_re, m_ssm_b_im, m_ssm_c_re, m_ssm_c_im, m_ssm_d, m_ssm_w_glu, m_na_rpb, m_w_out, m_cv_w_pw1, m_cv_dw_w, m_cv_dw_b, m_cv_ln_g, m_cv_ln_b, m_cv_w_pw2, m_ffn_w_up, m_ffn_conv_w, m_ffn_conv_b, m_ffn_w_down, m_g_out, v_c_ctx, v_w_mod, v_b_mod, v_g_mix, v_g_ffn, v_w_in, v_ssm_lam_re, v_ssm_lam_im, v_ssm_log_dt, v_ssm_b_re, v_ssm_b_im, v_ssm_c_re, v_ssm_c_im, v_ssm_d, v_ssm_w_glu, v_na_rpb, v_w_out, v_cv_w_pw1, v_cv_dw_w, v_cv_dw_b, v_cv_ln_g, v_cv_ln_b, v_cv_w_pw2, v_ffn_w_up, v_ffn_conv_w, v_ffn_conv_b, v_ffn_w_down, v_g_out):
    given = dict(x=x, c=c, ctx=ctx, c_ctx=c_ctx, w_mod=w_mod, b_mod=b_mod, g_mix=g_mix, g_ffn=g_ffn, w_in=w_in, ssm_lam_re=ssm_lam_re, ssm_lam_im=ssm_lam_im, ssm_log_dt=ssm_log_dt, ssm_b_re=ssm_b_re, ssm_b_im=ssm_b_im, ssm_c_re=ssm_c_re, ssm_c_im=ssm_c_im, ssm_d=ssm_d, ssm_w_glu=ssm_w_glu, na_rpb=na_rpb, w_out=w_out, cv_w_pw1=cv_w_pw1, cv_dw_w=cv_dw_w, cv_dw_b=cv_dw_b, cv_ln_g=cv_ln_g, cv_ln_b=cv_ln_b, cv_w_pw2=cv_w_pw2, ffn_w_up=ffn_w_up, ffn_conv_w=ffn_conv_w, ffn_conv_b=ffn_conv_b, ffn_w_down=ffn_w_down, g_out=g_out, loss_target=loss_target, m_c_ctx=m_c_ctx, m_w_mod=m_w_mod, m_b_mod=m_b_mod, m_g_mix=m_g_mix, m_g_ffn=m_g_ffn, m_w_in=m_w_in, m_ssm_lam_re=m_ssm_lam_re, m_ssm_lam_im=m_ssm_lam_im, m_ssm_log_dt=m_ssm_log_dt, m_ssm_b_re=m_ssm_b_re, m_ssm_b_im=m_ssm_b_im, m_ssm_c_re=m_ssm_c_re, m_ssm_c_im=m_ssm_c_im, m_ssm_d=m_ssm_d, m_ssm_w_glu=m_ssm_w_glu, m_na_rpb=m_na_rpb, m_w_out=m_w_out, m_cv_w_pw1=m_cv_w_pw1, m_cv_dw_w=m_cv_dw_w, m_cv_dw_b=m_cv_dw_b, m_cv_ln_g=m_cv_ln_g, m_cv_ln_b=m_cv_ln_b, m_cv_w_pw2=m_cv_w_pw2, m_ffn_w_up=m_ffn_w_up, m_ffn_conv_w=m_ffn_conv_w, m_ffn_conv_b=m_ffn_conv_b, m_ffn_w_down=m_ffn_w_down, m_g_out=m_g_out, v_c_ctx=v_c_ctx, v_w_mod=v_w_mod, v_b_mod=v_b_mod, v_g_mix=v_g_mix, v_g_ffn=v_g_ffn, v_w_in=v_w_in, v_ssm_lam_re=v_ssm_lam_re, v_ssm_lam_im=v_ssm_lam_im, v_ssm_log_dt=v_ssm_log_dt, v_ssm_b_re=v_ssm_b_re, v_ssm_b_im=v_ssm_b_im, v_ssm_c_re=v_ssm_c_re, v_ssm_c_im=v_ssm_c_im, v_ssm_d=v_ssm_d, v_ssm_w_glu=v_ssm_w_glu, v_na_rpb=v_na_rpb, v_w_out=v_w_out, v_cv_w_pw1=v_cv_w_pw1, v_cv_dw_w=v_cv_dw_w, v_cv_dw_b=v_cv_dw_b, v_cv_ln_g=v_cv_ln_g, v_cv_ln_b=v_cv_ln_b, v_cv_w_pw2=v_cv_w_pw2, v_ffn_w_up=v_ffn_w_up, v_ffn_conv_w=v_ffn_conv_w, v_ffn_conv_b=v_ffn_conv_b, v_ffn_w_down=v_ffn_w_down, v_g_out=v_g_out)
    weights = {n: given[n] for n in TWIN_WEIGHTS}
    shared = {n: given[n] for n in SHARED_INPUTS}
    per_example = {n: given[n] for n in ['x', 'c', 'ctx']}
    grad_fn = _jax.value_and_grad(_loss, argnums=(0, 1))

    def one_microbatch(ex, loss_target):
        ex = dict(ex)
        diff = ex.pop(TWIN_DIFF_INPUT)
        return grad_fn(weights, diff, {**shared, **ex}, loss_target)

    if N_MICROBATCH == 1:
        loss, (grad_w, grad_x) = one_microbatch(per_example, given["loss_target"])
    else:
        def body(carry, xs):
            loss_sum, grad_sum = carry
            l_k, (gw_k, gx_k) = one_microbatch(xs[0], xs[1])
            with _jax.named_scope("update"):
                return (loss_sum + l_k, _jax.tree.map(_jnp.add, grad_sum, gw_k)), gx_k

        init = (_jnp.zeros((), _jnp.float32), _jax.tree.map(_jnp.zeros_like, weights))
        (loss, grad_w), grad_x = _jax.lax.scan(body, init, (per_example, given["loss_target"]))
    with _jax.named_scope("update"):
        delta_w, new_m, new_v = {}, {}, {}
        for n in TWIN_WEIGHTS:
            delta_w[n], new_m[n], new_v[n] = _adamw(weights[n], grad_w[n], given["m_" + n], given["v_" + n])
    return (loss, grad_x, *[grad_w[n] for n in TWIN_WEIGHTS], *[delta_w[n] for n in TWIN_WEIGHTS],
            *[new_m[n] for n in TWIN_WEIGHTS], *[new_v[n] for n in TWIN_WEIGHTS])
```

```python
import functools
import math

import numpy as np
import jax
import jax.numpy as jnp
from jax import lax
from jax.experimental import pallas as pl
from jax.experimental.pallas import tpu as pltpu

F32, BF16 = jnp.float32, jnp.bfloat16
MESH = pl.DeviceIdType.MESH
V7X_VMEM_LIMIT = 56 << 20
LANE, SUB = 128, 8
N_CHIP, N_DEV = 4, 8

GRID_W = 64
N_MOD = 6
SSM_GROUP, SSM_STATE = 16, 64
NA_HEAD_DIM, NA_WIN_R, NA_WIN_C = 128, 8, 16
EPS = 1e-6
NEG = -1e30
ADAM_LR, ADAM_B1, ADAM_B2, ADAM_EPS, ADAM_WD, ADAM_STEP = 0.001, 0.9, 0.999, 1e-08, 0.01, 10
S5_STRIP = 512
S5_SEG = 8

NN = (((1,), (0,)), ((), ()))
NT = (((1,), (1,)), ((), ()))
TN = (((0,), (0,)), ((), ()))


def _params(*sem):
    return pltpu.CompilerParams(dimension_semantics=sem if sem else None, vmem_limit_bytes=V7X_VMEM_LIMIT)


def _pick(n, pref, mult=LANE):
    if n <= pref:
        return n
    best = None
    for t in range(mult, pref + 1, mult):
        if n % t == 0:
            best = t
    assert best is not None, (n, pref, mult)
    return best


def _sigmoid(x):
    return 1.0 / (1.0 + jnp.exp(-x))


def _mm(name, a, b, *, dims, grid, a_spec, b_spec, o_spec, out_shape, out_dtype, acc_shape, exact=False):
    nk = grid[2]

    def body(a_ref, b_ref, o_ref, *scratch):
        if exact:
            part = lax.dot_general(a_ref[...], b_ref[...], dims, preferred_element_type=F32,
                                   precision=lax.Precision.HIGHEST)
        else:
            part = lax.dot_general(a_ref[...].astype(BF16), b_ref[...].astype(BF16), dims,
                                   preferred_element_type=F32)
        if nk == 1:
            o_ref[...] = part.astype(o_ref.dtype)
        else:
            acc = scratch[0]
            kk = pl.program_id(2)

            @pl.when(kk == 0)
            def _():
                acc[...] = part

            @pl.when(kk > 0)
            def _():
                acc[...] += part

            @pl.when(kk == nk - 1)
            def _():
                o_ref[...] = acc[...].astype(o_ref.dtype)

    return pl.pallas_call(
        body, name=name, grid=grid, in_specs=[a_spec, b_spec], out_specs=o_spec,
        out_shape=jax.ShapeDtypeStruct(out_shape, out_dtype),
        scratch_shapes=[] if nk == 1 else [pltpu.VMEM(acc_shape, F32)],
        compiler_params=_params("parallel", "parallel", "arbitrary"),
    )(a, b)


def mm_nn_pieces(name, a, w, p0, n_p, out_dtype, halves=1, tm=512, tn=512):
    M, K = a.shape
    Nq = w.shape[2]
    tm, tn = _pick(M, tm, SUB), _pick(Nq, tn)
    tpp = Nq // tn
    pph = n_p // halves
    if halves == 1:
        o_spec = pl.BlockSpec((tm, tn), lambda i, j, k: (i, j))
        oshape = (M, n_p * Nq)
    else:
        o_spec = pl.BlockSpec((None, tm, tn), lambda i, j, k: ((j // tpp) // pph, i, ((j // tpp) % pph) * tpp + j % tpp))
        oshape = (halves, M, pph * Nq)
    return _mm(name, a, w, dims=NN, grid=(M // tm, n_p * tpp, 1),
               a_spec=pl.BlockSpec((tm, K), lambda i, j, k: (i, 0)),
               b_spec=pl.BlockSpec((None, K, tn), lambda i, j, k: (p0 + j // tpp, 0, j % tpp)),
               o_spec=o_spec, out_shape=oshape, out_dtype=out_dtype, acc_shape=(tm, tn))


def mm_nn(name, a, w, out_dtype, tm=512, tn=512):
    M, K = a.shape
    N = w.shape[1]
    tm, tn = _pick(M, tm, SUB), _pick(N, tn)
    return _mm(name, a, w, dims=NN, grid=(M // tm, N // tn, 1),
               a_spec=pl.BlockSpec((tm, K), lambda i, j, k: (i, 0)),
               b_spec=pl.BlockSpec((K, tn), lambda i, j, k: (0, j)),
               o_spec=pl.BlockSpec((tm, tn), lambda i, j, k: (i, j)),
               out_shape=(M, N), out_dtype=out_dtype, acc_shape=(tm, tn))


def mm_nt(name, dy, w, out_dtype, tm=512, tn=512, exact=False):
    M, N = dy.shape
    K = w.shape[0]
    tm, tn = _pick(M, tm, SUB), _pick(K, tn)
    return _mm(name, dy, w, dims=NT, grid=(M // tm, K // tn, 1),
               a_spec=pl.BlockSpec((tm, N), lambda i, j, k: (i, 0)),
               b_spec=pl.BlockSpec((tn, N), lambda i, j, k: (j, 0)),
               o_spec=pl.BlockSpec((tm, tn), lambda i, j, k: (i, j)),
               out_shape=(M, K), out_dtype=out_dtype, acc_shape=(tm, tn), exact=exact)


def mm_nt_pieces(name, dy, w, out_dtype, halves=1, tm=512, tn=512):
    P, K, Nq = w.shape
    M = dy.shape[-2]
    tm, tn = _pick(M, tm, SUB), _pick(K, tn)
    pph = P // halves
    if halves == 1:
        a_spec = pl.BlockSpec((tm, Nq), lambda i, j, k: (i, k))
    else:
        a_spec = pl.BlockSpec((None, tm, Nq), lambda i, j, k: (k // pph, i, k % pph))
    return _mm(name, dy, w, dims=NT, grid=(M // tm, K // tn, P),
               a_spec=a_spec,
               b_spec=pl.BlockSpec((None, tn, Nq), lambda i, j, k: (k, j, 0)),
               o_spec=pl.BlockSpec((tm, tn), lambda i, j, k: (i, j)),
               out_shape=(M, K), out_dtype=out_dtype, acc_shape=(tm, tn))


def mm_tn(name, a, dy, out_dtype, tm=512, tn=512):
    M, K = a.shape
    N = dy.shape[1]
    tm, tn = _pick(K, tm), _pick(N, tn)
    return _mm(name, a, dy, dims=TN, grid=(K // tm, N // tn, 1),
               a_spec=pl.BlockSpec((M, tm), lambda i, j, k: (0, i)),
               b_spec=pl.BlockSpec((M, tn), lambda i, j, k: (0, j)),
               o_spec=pl.BlockSpec((tm, tn), lambda i, j, k: (i, j)),
               out_shape=(K, N), out_dtype=out_dtype, acc_shape=(tm, tn))


def mm_tn_pieces(name, a, dy, n_p, out_dtype, halves=1, tm=512, tn=256):
    M, K = a.shape
    Nq = (dy.shape[-1] * halves) // n_p
    tm, tn = _pick(K, tm), _pick(Nq, tn)
    tpp = Nq // tn
    pph = n_p // halves
    if halves == 1:
        b_spec = pl.BlockSpec((M, tn), lambda i, j, k: (0, j))
    else:
        b_spec = pl.BlockSpec((None, M, tn), lambda i, j, k: ((j // tpp) // pph, 0, ((j // tpp) % pph) * tpp + j % tpp))
    return _mm(name, a, dy, dims=TN, grid=(K // tm, n_p * tpp, 1),
               a_spec=pl.BlockSpec((M, tm), lambda i, j, k: (0, i)),
               b_spec=b_spec,
               o_spec=pl.BlockSpec((None, tm, tn), lambda i, j, k: (j // tpp, i, j % tpp)),
               out_shape=(n_p, K, Nq), out_dtype=out_dtype, acc_shape=(tm, tn))


def _row_call(name, body, ins, in_kinds, outs, rows, tr, scratch=()):
    def spec(kind, shape):
        if isinstance(kind, pl.BlockSpec):
            return kind
        if kind == "row":
            return pl.BlockSpec((tr,) + tuple(shape[1:]), lambda i: (i,) + (0,) * (len(shape) - 1))
        return pl.BlockSpec(tuple(shape), lambda i: (0,) * len(shape))

    return pl.pallas_call(
        body, name=name, grid=(rows // tr,),
        in_specs=[spec(k, a.shape) for k, a in zip(in_kinds, ins)],
        out_specs=[spec(k, s) for k, s, _ in outs],
        out_shape=[jax.ShapeDtypeStruct(s, d) for _, s, d in outs],
        scratch_shapes=list(scratch),
        compiler_params=_params("arbitrary"),
    )(*ins)


def _acc(ref, val):
    @pl.when(pl.program_id(0) == 0)
    def _():
        ref[...] = val

    @pl.when(pl.program_id(0) > 0)
    def _():
        ref[...] += val


def norm_mod_fwd(name, x, w, b, tr=256):
    rows, d = x.shape
    tr = _pick(rows, tr, SUB)

    def body(x_ref, w_ref, b_ref, h_ref):
        xv = x_ref[...]
        r = lax.rsqrt(jnp.mean(xv * xv, axis=-1, keepdims=True) + EPS)
        h_ref[...] = (xv * r * w_ref[...] + b_ref[...]).astype(BF16)

    return _row_call(name, body, [x, w.reshape(1, d), b.reshape(1, d)], ["row", "vec", "vec"],
                     [("row", (rows, d), BF16)], rows, tr)[0]


def norm_mod_bwd(name, x, dh, w, dx_in, tr=256):
    rows, d = x.shape
    tr = _pick(rows, tr, SUB)

    def body(x_ref, dh_ref, w_ref, dxi_ref, dx_ref, cs1_ref, cs2_ref):
        xv = x_ref[...]
        r = lax.rsqrt(jnp.mean(xv * xv, axis=-1, keepdims=True) + EPS)
        xn = xv * r
        dhv = dh_ref[...].astype(F32)
        dxn = dhv * w_ref[...]
        dx_ref[...] = dxi_ref[...] + r * (dxn - xn * jnp.mean(dxn * xn, axis=-1, keepdims=True))
        _acc(cs1_ref, jnp.sum(dhv, axis=0, keepdims=True))
        _acc(cs2_ref, jnp.sum(dhv * xn, axis=0, keepdims=True))

    return _row_call(name, body, [x, dh, w.reshape(1, d), dx_in], ["row", "row", "vec", "row"],
                     [("row", (rows, d), F32), ("acc", (1, d), F32), ("acc", (1, d), F32)], rows, tr)


def gate_res_fwd(name, x, y, gate, tr=256):
    rows, d = x.shape
    tr = _pick(rows, tr, SUB)

    def body(x_ref, y_ref, g_ref, o_ref):
        o_ref[...] = x_ref[...] + g_ref[...] * y_ref[...].astype(F32)

    return _row_call(name, body, [x, y, gate.reshape(1, d)], ["row", "row", "vec"],
                     [("row", (rows, d), F32)], rows, tr)[0]


def gate_res_bwd(name, dx, y, gate, tr=256):
    rows, d = dx.shape
    tr = _pick(rows, tr, SUB)

    def body(dx_ref, y_ref, g_ref, dy_ref, dg_ref):
        dxv = dx_ref[...]
        dy_ref[...] = (g_ref[...] * dxv).astype(BF16)
        _acc(dg_ref, jnp.sum(dxv * y_ref[...].astype(F32), axis=0, keepdims=True))

    return _row_call(name, body, [dx, y, gate.reshape(1, d)], ["row", "row", "vec"],
                     [("row", (rows, d), BF16), ("acc", (1, d), F32)], rows, tr)


def loss_head(name, x, g, target, tr=256):
    rows, d = x.shape
    tr = _pick(rows, tr, SUB)

    def body(x_ref, g_ref, t_ref, dx_ref, dg_ref, loss_ref):
        xv = x_ref[...]
        r = lax.rsqrt(jnp.mean(xv * xv, axis=-1, keepdims=True) + EPS)
        xn = xv * r
        err = xn * g_ref[...] - t_ref[...]
        dy = err * (1.0 / d)
        dxn = dy * g_ref[...]
        dx_ref[...] = r * (dxn - xn * jnp.mean(dxn * xn, axis=-1, keepdims=True))
        _acc(dg_ref, jnp.sum(dy * xn, axis=0, keepdims=True))
        part = 0.5 * jnp.sum(jnp.sum(err * err, axis=-1, keepdims=True) * (1.0 / d), axis=0, keepdims=True)
        _acc(loss_ref, jnp.broadcast_to(part, (1, LANE)))

    return _row_call(name, body, [x, g.reshape(1, d), target], ["row", "vec", "row"],
                     [("row", (rows, d), F32), ("acc", (1, d), F32), ("acc", (1, LANE), F32)], rows, tr)


def fma3(name, a, b, c, dvec, out_dtype, tr=256):
    rows, d = a.shape
    tr = _pick(rows, tr, SUB)

    def body(a_ref, b_ref, c_ref, d_ref, o_ref):
        o_ref[...] = (d_ref[...] * a_ref[...] + b_ref[...] + c_ref[...]).astype(o_ref.dtype)

    return _row_call(name, body, [a, b, c, dvec.reshape(1, d)], ["row", "row", "row", "vec"],
                     [("row", (rows, d), out_dtype)], rows, tr)[0]


def add2(name, a, b, out_dtype, tr=512):
    shape = a.shape
    a2, b2 = a.reshape(-1, shape[-1]), b.reshape(-1, shape[-1])
    rows = a2.shape[0]
    tr = _pick(rows, tr, 16)

    def body(a_ref, b_ref, o_ref):
        o_ref[...] = (a_ref[...].astype(F32) + b_ref[...].astype(F32)).astype(o_ref.dtype)

    out = _row_call(name, body, [a2, b2], ["row", "row"], [("row", a2.shape, out_dtype)], rows, tr)[0]
    return out.reshape(shape)


def sum_lead(name, a, out_dtype, tr=512):
    n, rows, cols = a.shape
    tr = _pick(rows, tr, 16)

    def body(a_ref, o_ref):
        acc = a_ref[0].astype(F32)
        for s in range(1, n):
            acc = acc + a_ref[s].astype(F32)
        o_ref[...] = acc.astype(o_ref.dtype)

    return pl.pallas_call(
        body, name=name, grid=(rows // tr,),
        in_specs=[pl.BlockSpec((n, tr, cols), lambda i: (0, i, 0))],
        out_specs=pl.BlockSpec((tr, cols), lambda i: (i, 0)),
        out_shape=jax.ShapeDtypeStruct((rows, cols), out_dtype),
        compiler_params=_params("parallel"),
    )(a)


def adamw(name, w, g, m, v, tr=512):
    shape = w.shape
    n = int(np.prod(shape))
    cols = 1024 if n % 1024 == 0 else shape[-1]
    w2, g2, m2, v2 = (t.reshape(-1, cols) for t in (w, g, m, v))
    rows = w2.shape[0]
    tr = _pick(rows, tr, SUB)
    c1 = 1.0 - ADAM_B1 ** ADAM_STEP
    c2 = 1.0 - ADAM_B2 ** ADAM_STEP

    def body(w_ref, g_ref, m_ref, v_ref, d_ref, mo_ref, vo_ref):
        gv = g_ref[...]
        mn = ADAM_B1 * m_ref[...] + (1.0 - ADAM_B1) * gv
        vn = ADAM_B2 * v_ref[...] + (1.0 - ADAM_B2) * (gv * gv)
        mo_ref[...] = mn
        vo_ref[...] = vn
        d_ref[...] = -ADAM_LR * ((mn / c1) / (jnp.sqrt(vn / c2) + ADAM_EPS) + ADAM_WD * w_ref[...])

    outs = _row_call(name, body, [w2, g2, m2, v2], ["row"] * 4, [("row", w2.shape, F32)] * 3, rows, tr)
    return tuple(o.reshape(shape) for o in outs)


HALO = 16


def _halo_specs(lead, R, tn, n_rows, col_of):
    nb, nblk = R // HALO, n_rows // HALO

    def mk(rows, row_of):
        return pl.BlockSpec((lead, rows, tn), lambda *g: (0, row_of(g[-1]), col_of(g)))

    return (mk(HALO, lambda i: jnp.maximum(i * nb - 1, 0)), mk(R, lambda i: i),
            mk(HALO, lambda i: jnp.minimum((i + 1) * nb, nblk - 1)))


def _fill_halo(dst, i, last, R, prev, cur, nxt):
    nd = len(dst.shape)
    lead = (slice(None),) * (nd - 2)
    dst[lead + (slice(0, HALO), slice(None))] = jnp.where(i == 0, 0.0, prev)
    dst[lead + (slice(HALO, HALO + R), slice(None))] = cur
    dst[lead + (slice(HALO + R, HALO + R + HALO), slice(None))] = jnp.where(i == last, 0.0, nxt)


def ffn_mid_fwd(name, up3, cw, cb, R=256, tn=256):
    _, L, Fd = up3.shape
    R, tn = _pick(L, R, HALO), _pick(Fd, tn)
    nrow = L // R

    def body(p_ref, c_ref, n_ref, w_ref, b_ref, act_ref, s_ref):
        i = pl.program_id(1)
        _fill_halo(s_ref, i, nrow - 1, R, p_ref[...].astype(F32), c_ref[...].astype(F32), n_ref[...].astype(F32))
        cv = b_ref[...]
        for k in range(3):
            cv = cv + w_ref[:, k:k + 1, :] * s_ref[:, pl.ds(HALO - 1 + k, R), :]
        u, g = cv[0], cv[1]
        act_ref[...] = (u * g * _sigmoid(g)).astype(BF16)

    hs = _halo_specs(2, R, tn, L, lambda g: g[0])
    return pl.pallas_call(
        body, name=name, grid=(Fd // tn, nrow),
        in_specs=[*hs, pl.BlockSpec((2, 3, tn), lambda j, i: (0, 0, j)), pl.BlockSpec((2, 1, tn), lambda j, i: (0, 0, j))],
        out_specs=pl.BlockSpec((R, tn), lambda j, i: (i, j)),
        out_shape=jax.ShapeDtypeStruct((L, Fd), BF16),
        scratch_shapes=[pltpu.VMEM((2, R + 2 * HALO, tn), F32)],
        compiler_params=_params("parallel", "arbitrary"),
    )(up3, up3, up3, cw, cb)


def ffn_mid_bwd(name, up3, dact, cw, cb, R=256, tn=256):
    _, L, Fd = up3.shape
    R, tn = _pick(L, R, HALO), _pick(Fd, tn)
    nrow = L // R
    E = R + HALO

    def body(pu, cu, nu, pd, cd, nd, w_ref, b_ref, dup_ref, dcw_ref, dcb_ref, s_ref, d_ref, e_ref):
        i = pl.program_id(1)
        _fill_halo(s_ref, i, nrow - 1, R, pu[...].astype(F32), cu[...].astype(F32), nu[...].astype(F32))
        _fill_halo(d_ref, i, nrow - 1, R, pd[0].astype(F32), cd[0].astype(F32), nd[0].astype(F32))
        cv = b_ref[...]
        for k in range(3):
            cv = cv + w_ref[:, k:k + 1, :] * s_ref[:, pl.ds(HALO - 9 + k, E), :]
        da = d_ref[pl.ds(HALO - 8, E), :]
        u, g = cv[0], cv[1]
        sg = _sigmoid(g)
        e_ref[0] = da * g * sg
        e_ref[1] = da * u * sg * (1.0 + g * (1.0 - sg))
        dup = jnp.zeros((2, R, tn), F32)
        for k in range(3):
            dup = dup + w_ref[:, k:k + 1, :] * e_ref[:, pl.ds(9 - k, R), :]
        dup_ref[...] = dup.astype(BF16)
        dc = e_ref[:, pl.ds(8, R), :]

        @pl.when(i == 0)
        def _():
            dcw_ref[...] = jnp.zeros_like(dcw_ref)
            dcb_ref[...] = jnp.zeros_like(dcb_ref)

        dcb_ref[...] += jnp.sum(dc, axis=1, keepdims=True)
        for k in range(3):
            dcw_ref[:, k:k + 1, :] += jnp.sum(dc * s_ref[:, pl.ds(HALO - 1 + k, R), :], axis=1, keepdims=True)

    hu = _halo_specs(2, R, tn, L, lambda g: g[0])
    hd = _halo_specs(1, R, tn, L, lambda g: g[0])
    return pl.pallas_call(
        body, name=name, grid=(Fd // tn, nrow),
        in_specs=[*hu, *hd, pl.BlockSpec((2, 3, tn), lambda j, i: (0, 0, j)), pl.BlockSpec((2, 1, tn), lambda j, i: (0, 0, j))],
        out_specs=[pl.BlockSpec((2, R, tn), lambda j, i: (0, i, j)), pl.BlockSpec((2, 3, tn), lambda j, i: (0, 0, j)),
                   pl.BlockSpec((2, 1, tn), lambda j, i: (0, 0, j))],
        out_shape=[jax.ShapeDtypeStruct((2, L, Fd), BF16), jax.ShapeDtypeStruct((2, 3, Fd), F32),
                   jax.ShapeDtypeStruct((2, 1, Fd), F32)],
        scratch_shapes=[pltpu.VMEM((2, R + 2 * HALO, tn), F32), pltpu.VMEM((R + 2 * HALO, tn), F32),
                        pltpu.VMEM((2, E, tn), F32)],
        compiler_params=_params("parallel", "arbitrary"),
    )(up3, up3, up3, dact[None], dact[None], dact[None], cw, cb)


def _glu_z0(blk):
    return blk[0].astype(F32) * _sigmoid(blk[1].astype(F32))


def conf_mid_fwd(name, ag3, dw_w, dw_b, ln_g, ln_b, R=128, cb=256):
    _, L, C = ag3.shape
    K = dw_w.shape[0]
    pad = (K - 1) // 2
    assert pad <= HALO
    R, cb = _pick(L, R, HALO), _pick(C, cb)
    nrow = L // R

    def body(p_ref, c_ref, n_ref, w_ref, b_ref, g_ref, bb_ref, z1_ref, z3_ref, s_ref):
        i = pl.program_id(0)
        _fill_halo(s_ref, i, nrow - 1, R, _glu_z0(p_ref), _glu_z0(c_ref), _glu_z0(n_ref))
        for c0 in range(0, C, cb):
            acc = jnp.broadcast_to(b_ref[:, c0:c0 + cb], (R, cb))
            for k in range(K):
                acc = acc + w_ref[k:k + 1, c0:c0 + cb] * s_ref[pl.ds(HALO - pad + k, R), c0:c0 + cb]
            z1_ref[:, c0:c0 + cb] = acc
        z1 = z1_ref[...]
        zc = z1 - jnp.mean(z1, axis=-1, keepdims=True)
        zn = zc * lax.rsqrt(jnp.mean(zc * zc, axis=-1, keepdims=True) + EPS)
        z2 = zn * g_ref[...] + bb_ref[...]
        z3_ref[...] = (z2 * _sigmoid(z2)).astype(BF16)

    hs = _halo_specs(2, R, C, L, lambda g: 0)
    vec = pl.BlockSpec((1, C), lambda i: (0, 0))
    return pl.pallas_call(
        body, name=name, grid=(nrow,),
        in_specs=[*hs, pl.BlockSpec((K, C), lambda i: (0, 0)), vec, vec, vec],
        out_specs=[pl.BlockSpec((R, C), lambda i: (i, 0)), pl.BlockSpec((R, C), lambda i: (i, 0))],
        out_shape=[jax.ShapeDtypeStruct((L, C), F32), jax.ShapeDtypeStruct((L, C), BF16)],
        scratch_shapes=[pltpu.VMEM((R + 2 * HALO, C), F32)],
        compiler_params=_params("parallel"),
    )(ag3, ag3, ag3, dw_w, dw_b.reshape(1, C), ln_g.reshape(1, C), ln_b.reshape(1, C))


def conf_ln_bwd(name, z1, dz3, ln_g, ln_b, tr=256):
    rows, C = z1.shape
    tr = _pick(rows, tr, HALO)

    def body(z_ref, d_ref, g_ref, b_ref, dz_ref, dg_ref, db_ref):
        z1v = z_ref[...]
        zc = z1v - jnp.mean(z1v, axis=-1, keepdims=True)
        rs = lax.rsqrt(jnp.mean(zc * zc, axis=-1, keepdims=True) + EPS)
        zn = zc * rs
        z2 = zn * g_ref[...] + b_ref[...]
        sg = _sigmoid(z2)
        dz2 = d_ref[...].astype(F32) * sg * (1.0 + z2 * (1.0 - sg))
        _acc(dg_ref, jnp.sum(dz2 * zn, axis=0, keepdims=True))
        _acc(db_ref, jnp.sum(dz2, axis=0, keepdims=True))
        dzn = dz2 * g_ref[...]
        dz1 = rs * (dzn - jnp.mean(dzn, axis=-1, keepdims=True) - zn * jnp.mean(dzn * zn, axis=-1, keepdims=True))
        dz_ref[...] = dz1.astype(BF16)

    return _row_call(name, body, [z1, dz3, ln_g.reshape(1, C), ln_b.reshape(1, C)], ["row", "row", "vec", "vec"],
                     [("row", (rows, C), BF16), ("acc", (1, C), F32), ("acc", (1, C), F32)], rows, tr)


def conf_conv_bwd(name, ag3, dz1, dw_w, R=128, cb=256):
    _, L, C = ag3.shape
    K = dw_w.shape[0]
    pad = (K - 1) // 2
    R, cb = _pick(L, R, HALO), _pick(C, cb)
    nrow = L // R

    def body(pa, ca, na, pd, cd, nd, w_ref, dag_ref, dw_ref, db_ref, s_ref, d_ref, z_ref):
        i = pl.program_id(0)
        _fill_halo(s_ref, i, nrow - 1, R, _glu_z0(pa), _glu_z0(ca), _glu_z0(na))
        _fill_halo(d_ref, i, nrow - 1, R, pd[0].astype(F32), cd[0].astype(F32), nd[0].astype(F32))

        @pl.when(i == 0)
        def _():
            dw_ref[...] = jnp.zeros_like(dw_ref)
            db_ref[...] = jnp.zeros_like(db_ref)

        for c0 in range(0, C, cb):
            cs = slice(c0, c0 + cb)
            dcur = d_ref[pl.ds(HALO, R), cs]
            acc = jnp.zeros((R, cb), F32)
            for k in range(K):
                acc = acc + w_ref[k:k + 1, cs] * d_ref[pl.ds(HALO + pad - k, R), cs]
                dw_ref[k:k + 1, cs] += jnp.sum(dcur * s_ref[pl.ds(HALO - pad + k, R), cs], axis=0, keepdims=True)
            z_ref[:, cs] = acc
            db_ref[:, cs] += jnp.sum(dcur, axis=0, keepdims=True)
        dz0 = z_ref[...]
        a = ca[0].astype(F32)
        sg = _sigmoid(ca[1].astype(F32))
        dag_ref[0] = (dz0 * sg).astype(BF16)
        dag_ref[1] = (dz0 * a * sg * (1.0 - sg)).astype(BF16)

    ha = _halo_specs(2, R, C, L, lambda g: 0)
    hd = _halo_specs(1, R, C, L, lambda g: 0)
    return pl.pallas_call(
        body, name=name, grid=(nrow,),
        in_specs=[*ha, *hd, pl.BlockSpec((K, C), lambda i: (0, 0))],
        out_specs=[pl.BlockSpec((2, R, C), lambda i: (0, i, 0)), pl.BlockSpec((K, C), lambda i: (0, 0)),
                   pl.BlockSpec((1, C), lambda i: (0, 0))],
        out_shape=[jax.ShapeDtypeStruct((2, L, C), BF16), jax.ShapeDtypeStruct((K, C), F32),
                   jax.ShapeDtypeStruct((1, C), F32)],
        scratch_shapes=[pltpu.VMEM((R + 2 * HALO, C), F32), pltpu.VMEM((R + 2 * HALO, C), F32), pltpu.VMEM((R, C), F32)],
        compiler_params=_params("arbitrary"),
    )(ag3, ag3, ag3, dz1[None], dz1[None], dz1[None], dw_w)


_GELU_C = math.sqrt(2.0 / math.pi)


def _gelu(x):
    return 0.5 * x * (1.0 + jnp.tanh(_GELU_C * (x + 0.044715 * x * x * x)))


def _gelu_grad(x):
    t = jnp.tanh(_GELU_C * (x + 0.044715 * x * x * x))
    return 0.5 * (1.0 + t) + 0.5 * x * (1.0 - t * t) * _GELU_C * (1.0 + 3.0 * 0.044715 * x * x)


def glu_fwd(name, u, y0, y1, d, wg, tr=512):
    rows, W = u.shape
    tr = _pick(rows, tr, HALO)

    def body(u_ref, y0_ref, y1_ref, d_ref, w_ref, o_ref):
        z = _gelu(d_ref[...] * u_ref[...] + y0_ref[...] + y1_ref[...])
        zz = jnp.dot(z.astype(BF16), w_ref[...], preferred_element_type=F32)
        o_ref[...] = (z * _sigmoid(zz)).astype(BF16)

    return _row_call(name, body, [u, y0, y1, d.reshape(1, W), wg], ["row", "row", "row", "vec", "vec"],
                     [("row", (rows, W), BF16)], rows, tr)[0]


def glu_bwd(name, u, y0, y1, d, wg, dmix, tr=512):
    rows, W = u.shape
    tr = _pick(rows, tr, HALO)

    def body(u_ref, y0_ref, y1_ref, d_ref, w_ref, do_ref, dy_ref, z_ref, dzz_ref, dd_ref):
        uv = u_ref[...]
        y = d_ref[...] * uv + y0_ref[...] + y1_ref[...]
        z = _gelu(y)
        zz = jnp.dot(z.astype(BF16), w_ref[...], preferred_element_type=F32)
        sg = _sigmoid(zz)
        do = do_ref[...].astype(F32)
        dzz = (do * z * sg * (1.0 - sg)).astype(BF16)
        dz = do * sg + lax.dot_general(dzz, w_ref[...], NT, preferred_element_type=F32)
        dy = dz * _gelu_grad(y)
        dy_ref[...] = dy
        z_ref[...] = z.astype(BF16)
        dzz_ref[...] = dzz
        _acc(dd_ref, jnp.sum(dy * uv, axis=0, keepdims=True))

    do_spec = pl.BlockSpec((tr, W), lambda i: (i, 0))
    return _row_call(name, body, [u, y0, y1, d.reshape(1, W), wg, dmix], ["row", "row", "row", "vec", "vec", do_spec],
                     [("row", (rows, W), F32), ("row", (rows, W), BF16), ("row", (rows, W), BF16), ("acc", (1, W), F32)],
                     rows, tr)


NA_KEYS = NA_WIN_R * GRID_W


def na_bias(rpb):
    q = np.arange(GRID_W)
    cs = np.clip(q - NA_WIN_C // 2, 0, GRID_W - NA_WIN_C)
    ok = (q[None, :] >= cs[:, None]) & (q[None, :] < cs[:, None] + NA_WIN_C)
    cidx = np.clip(q[None, :] - q[:, None] + (NA_WIN_C - 1), 0, 2 * NA_WIN_C - 2)
    ridx = np.arange(NA_WIN_R)[None, :] - np.arange(NA_WIN_R)[:, None] + (NA_WIN_R - 1)
    b = rpb[:, ridx[:, :, None, None], cidx[None, None, :, :]]
    b = jnp.where(ok[None, None, None], b, NEG)
    return b.transpose(0, 1, 3, 2, 4).reshape(rpb.shape[0], NA_WIN_R, GRID_W, NA_KEYS)


def na_bias_fold_matrix():
    q = np.arange(GRID_W)
    cs = np.clip(q - NA_WIN_C // 2, 0, GRID_W - NA_WIN_C)
    ok = (q[None, :] >= cs[:, None]) & (q[None, :] < cs[:, None] + NA_WIN_C)
    cidx = np.clip(q[None, :] - q[:, None] + (NA_WIN_C - 1), 0, 2 * NA_WIN_C - 2)
    e = (cidx.reshape(-1)[None, :] == np.arange(LANE)[:, None]) & ok.reshape(-1)[None, :]
    return jnp.asarray(e, F32)


def _na_window(r, rows):
    kr0 = jnp.clip(r - NA_WIN_R // 2, 0, rows - NA_WIN_R)
    return pl.multiple_of(kr0 * GRID_W, GRID_W), r - kr0


def natten_fwd(name, qkv, kvc, bias):
    L = qkv.shape[0]
    NA = qkv.shape[1] // 3
    H, rows, Lc = NA // NA_HEAD_DIM, L // GRID_W, kvc.shape[0]
    scale = NA_HEAD_DIM ** -0.5
    hd = NA_HEAD_DIM

    def body(q_ref, k_ref, v_ref, kc_ref, vc_ref, b_ref, o_ref, lse_ref):
        st, off = _na_window(pl.program_id(1), rows)
        q = q_ref[...]
        s_loc = lax.dot_general(q, k_ref[pl.ds(st, NA_KEYS), :], NT, preferred_element_type=F32) * scale + b_ref[off]
        s_ctx = lax.dot_general(q, kc_ref[...], NT, preferred_element_type=F32) * scale
        m = jnp.maximum(jnp.max(s_loc, axis=-1, keepdims=True), jnp.max(s_ctx, axis=-1, keepdims=True))
        p_loc, p_ctx = jnp.exp(s_loc - m), jnp.exp(s_ctx - m)
        l = jnp.sum(p_loc, axis=-1, keepdims=True) + jnp.sum(p_ctx, axis=-1, keepdims=True)
        o = (jnp.dot(p_loc.astype(BF16), v_ref[pl.ds(st, NA_KEYS), :], preferred_element_type=F32)
             + jnp.dot(p_ctx.astype(BF16), vc_ref[...], preferred_element_type=F32))
        o_ref[...] = (o / l).astype(BF16)
        lse_ref[...] = m + jnp.log(l)

    return pl.pallas_call(
        body, name=name, grid=(H, rows),
        in_specs=[pl.BlockSpec((GRID_W, hd), lambda h, r: (r, h)),
                  pl.BlockSpec((L, hd), lambda h, r: (0, H + h)),
                  pl.BlockSpec((L, hd), lambda h, r: (0, 2 * H + h)),
                  pl.BlockSpec((Lc, hd), lambda h, r: (0, h)),
                  pl.BlockSpec((Lc, hd), lambda h, r: (0, H + h)),
                  pl.BlockSpec((None, NA_WIN_R, GRID_W, NA_KEYS), lambda h, r: (h, 0, 0, 0))],
        out_specs=[pl.BlockSpec((GRID_W, hd), lambda h, r: (r, h)),
                   pl.BlockSpec((None, GRID_W, 1), lambda h, r: (h, r, 0))],
        out_shape=[jax.ShapeDtypeStruct((L, NA), BF16), jax.ShapeDtypeStruct((H, L, 1), F32)],
        compiler_params=_params("parallel", "arbitrary"),
    )(qkv, qkv, qkv, kvc, kvc, bias)


def natten_bwd(name, qkv, kvc, bias, o, lse, dmix):
    L = qkv.shape[0]
    NA = qkv.shape[1] // 3
    H, rows, Lc = NA // NA_HEAD_DIM, L // GRID_W, kvc.shape[0]
    scale = NA_HEAD_DIM ** -0.5
    hd = NA_HEAD_DIM

    def body(q_ref, k_ref, v_ref, kc_ref, vc_ref, b_ref, o_ref, lse_ref, do_ref,
             dq_ref, dk_ref, dv_ref, dkc_ref, dvc_ref, db_ref):
        r = pl.program_id(1)
        st, off = _na_window(r, rows)

        @pl.when(r == 0)
        def _():
            for ref in (dk_ref, dv_ref, dkc_ref, dvc_ref, db_ref):
                ref[...] = jnp.zeros_like(ref)

        q, kl, vl, kc, vc = q_ref[...], k_ref[pl.ds(st, NA_KEYS), :], v_ref[pl.ds(st, NA_KEYS), :], kc_ref[...], vc_ref[...]
        do = do_ref[...]
        lse_v = lse_ref[...]
        p_loc = jnp.exp(lax.dot_general(q, kl, NT, preferred_element_type=F32) * scale + b_ref[off] - lse_v)
        p_ctx = jnp.exp(lax.dot_general(q, kc, NT, preferred_element_type=F32) * scale - lse_v)
        delta = jnp.sum(do.astype(F32) * o_ref[...].astype(F32), axis=-1, keepdims=True)
        ds_loc = p_loc * (lax.dot_general(do, vl, NT, preferred_element_type=F32) - delta)
        ds_ctx = p_ctx * (lax.dot_general(do, vc, NT, preferred_element_type=F32) - delta)
        dsl, dsc = ds_loc.astype(BF16), ds_ctx.astype(BF16)
        dq = jnp.dot(dsl, kl, preferred_element_type=F32) + jnp.dot(dsc, kc, preferred_element_type=F32)
        dq_ref[...] = (dq * scale).astype(BF16)
        dk_ref[pl.ds(st, NA_KEYS), :] += lax.dot_general(dsl, q, TN, preferred_element_type=F32) * scale
        dv_ref[pl.ds(st, NA_KEYS), :] += lax.dot_general(p_loc.astype(BF16), do, TN, preferred_element_type=F32)
        dkc_ref[...] += lax.dot_general(dsc, q, TN, preferred_element_type=F32) * scale
        dvc_ref[...] += lax.dot_general(p_ctx.astype(BF16), do, TN, preferred_element_type=F32)
        db_ref[off] += ds_loc

    tok = pl.BlockSpec((GRID_W, hd), lambda h, r: (r, h))
    return pl.pallas_call(
        body, name=name, grid=(H, rows),
        in_specs=[tok,
                  pl.BlockSpec((L, hd), lambda h, r: (0, H + h)),
                  pl.BlockSpec((L, hd), lambda h, r: (0, 2 * H + h)),
                  pl.BlockSpec((Lc, hd), lambda h, r: (0, h)),
                  pl.BlockSpec((Lc, hd), lambda h, r: (0, H + h)),
                  pl.BlockSpec((None, NA_WIN_R, GRID_W, NA_KEYS), lambda h, r: (h, 0, 0, 0)),
                  tok,
                  pl.BlockSpec((None, GRID_W, 1), lambda h, r: (h, r, 0)),
                  pl.BlockSpec((GRID_W, hd), lambda h, r: (r, H + h))],
        out_specs=[tok,
                   pl.BlockSpec((L, hd), lambda h, r: (0, h)),
                   pl.BlockSpec((L, hd), lambda h, r: (0, h)),
                   pl.BlockSpec((Lc, hd), lambda h, r: (0, h)),
                   pl.BlockSpec((Lc, hd), lambda h, r: (0, h)),
                   pl.BlockSpec((None, NA_WIN_R, GRID_W, NA_KEYS), lambda h, r: (h, 0, 0, 0))],
        out_shape=[jax.ShapeDtypeStruct((L, NA), BF16), jax.ShapeDtypeStruct((L, NA), F32), jax.ShapeDtypeStruct((L, NA), F32),
                   jax.ShapeDtypeStruct((Lc, NA), F32), jax.ShapeDtypeStruct((Lc, NA), F32),
                   jax.ShapeDtypeStruct(bias.shape, F32)],
        compiler_params=_params("parallel", "arbitrary"),
    )(qkv, qkv, qkv, kvc, kvc, bias, o, lse, dmix)


def _s5_dims(T, N):
    TC = T // S5_SEG
    assert T % (S5_SEG * SUB * 2) == 0 and N % S5_STRIP == 0
    return TC, TC // SUB, S5_SEG, N // S5_STRIP


def s5_scan(name, xin, mats, a, rev):
    _, T, W = xin.shape
    N = a.shape[-1]
    TC, NG, NCH, NS = _s5_dims(T, N)
    CW, SL = W // NS, S5_STRIP

    def ck(k):
        return NCH - 1 - k if rev else k

    def body(x_ref, m_ref, a_ref, h_ref, f_ref, carry):
        @pl.when(pl.program_id(2) == 0)
        def _():
            carry[...] = jnp.zeros_like(carry)

        xb = x_ref[...].astype(BF16)
        h_ref[0] = jnp.dot(xb, m_ref[0], preferred_element_type=F32)
        h_ref[1] = jnp.dot(xb, m_ref[1], preferred_element_type=F32)
        ar, ai = jnp.broadcast_to(a_ref[0], (SUB, SL)), jnp.broadcast_to(a_ref[1], (SUB, SL))

        def step(t, c):
            hr, hi = c
            row = pl.multiple_of((NG - 1 - t if rev else t) * SUB, SUB)
            nr = ar * hr - ai * hi + h_ref[0, pl.ds(row, SUB), :]
            ni = ar * hi + ai * hr + h_ref[1, pl.ds(row, SUB), :]
            h_ref[0, pl.ds(row, SUB), :] = nr
            h_ref[1, pl.ds(row, SUB), :] = ni
            return nr, ni

        hr, hi = lax.fori_loop(0, NG, step, (carry[0], carry[1]))
        carry[0], carry[1] = hr, hi
        f_ref[0], f_ref[1] = hr, hi

    return pl.pallas_call(
        body, name=name, grid=(2, NS, NCH),
        in_specs=[pl.BlockSpec((None, TC, CW), lambda d, j, k: (d, ck(k), j)),
                  pl.BlockSpec((None, 2, None, CW, SL), lambda d, j, k: (d, 0, j, 0, 0)),
                  pl.BlockSpec((None, 2, 1, SL), lambda d, j, k: (d, 0, 0, j))],
        out_specs=[pl.BlockSpec((None, 2, TC, SL), lambda d, j, k: (d, 0, ck(k), j)),
                   pl.BlockSpec((None, 2, SUB, SL), lambda d, j, k: (d, 0, 0, j))],
        out_shape=[jax.ShapeDtypeStruct((2, 2, T, N), F32), jax.ShapeDtypeStruct((2, 2, SUB, N), F32)],
        scratch_shapes=[pltpu.VMEM((2, SUB, SL), F32)],
        compiler_params=_params("parallel", "parallel", "arbitrary"),
    )(xin, mats, a)


def s5_fix(name, hloc, hin, a, mats, rev):
    _, _, T, N = hloc.shape
    TC, NG, NCH, NS = _s5_dims(T, N)
    SL = S5_STRIP
    CW = mats.shape[-1]

    def ck(k):
        return NCH - 1 - k if rev else k

    def body(h_ref, hin_ref, a_ref, m_ref, ho_ref, y_ref, g):
        @pl.when(pl.program_id(2) == 0)
        def _():
            g[...] = hin_ref[...]

        ar, ai = jnp.broadcast_to(a_ref[0], (SUB, SL)), jnp.broadcast_to(a_ref[1], (SUB, SL))

        def step(t, c):
            gr, gi = c
            row = pl.multiple_of((NG - 1 - t if rev else t) * SUB, SUB)
            nr = ar * gr - ai * gi
            ni = ar * gi + ai * gr
            ho_ref[0, pl.ds(row, SUB), :] = h_ref[0, pl.ds(row, SUB), :] + nr
            ho_ref[1, pl.ds(row, SUB), :] = h_ref[1, pl.ds(row, SUB), :] + ni
            return nr, ni

        gr, gi = lax.fori_loop(0, NG, step, (g[0], g[1]))
        g[0], g[1] = gr, gi
        y_ref[...] = (jnp.dot(ho_ref[0].astype(BF16), m_ref[0], preferred_element_type=F32)
                      + jnp.dot(ho_ref[1].astype(BF16), m_ref[1], preferred_element_type=F32))

    return pl.pallas_call(
        body, name=name, grid=(2, NS, NCH),
        in_specs=[pl.BlockSpec((None, 2, TC, SL), lambda d, j, k: (d, 0, ck(k), j)),
                  pl.BlockSpec((None, 2, SUB, SL), lambda d, j, k: (d, 0, 0, j)),
                  pl.BlockSpec((None, 2, 1, SL), lambda d, j, k: (d, 0, 0, j)),
                  pl.BlockSpec((None, 2, None, SL, CW), lambda d, j, k: (d, 0, j, 0, 0))],
        out_specs=[pl.BlockSpec((None, 2, TC, SL), lambda d, j, k: (d, 0, ck(k), j)),
                   pl.BlockSpec((None, TC, CW), lambda d, j, k: (d, ck(k), j))],
        out_shape=[jax.ShapeDtypeStruct((2, 2, T, N), F32), jax.ShapeDtypeStruct((2, T, NS * CW), F32)],
        scratch_shapes=[pltpu.VMEM((2, SUB, SL), F32)],
        compiler_params=_params("parallel", "parallel", "arbitrary"),
    )(hloc, hin, a, mats)


def s5_grads(name, g, h, u, dy):
    _, _, T, N = g.shape
    W = u.shape[-1]
    TC, NG, NCH, NS = _s5_dims(T, N)
    CW, SL = W // NS, S5_STRIP

    def body(g_ref, h_ref, hp_ref, hl_ref, u_ref, dy_ref, dm_ref, dc_ref, da_ref, hs):
        k = pl.program_id(2)
        sub = lax.broadcasted_iota(jnp.int32, (SUB, SL), 0)
        for z in range(2):
            wrapped = jnp.where(sub == 0, 0.0, pltpu.roll(hl_ref[z], 1, 0))
            hs[z, 0:SUB, :] = jnp.where(k == 0, wrapped, hp_ref[z])
            hs[z, SUB:TC, :] = h_ref[z, 0:TC - SUB, :]
        gr, gi, pr, pi = g_ref[0], g_ref[1], hs[0], hs[1]
        dar = jnp.sum((gr * pr + gi * pi).reshape(NG, SUB, SL), axis=0)
        dai = jnp.sum((gi * pr - gr * pi).reshape(NG, SUB, SL), axis=0)
        ub, dyb = u_ref[...].astype(BF16), dy_ref[...].astype(BF16)
        dm = [lax.dot_general(ub, g_ref[z].astype(BF16), TN, preferred_element_type=F32) for z in range(2)]
        dc = [lax.dot_general(dyb, h_ref[z].astype(BF16), TN, preferred_element_type=F32) for z in range(2)]

        @pl.when(k == 0)
        def _():
            da_ref[0], da_ref[1] = dar, dai
            for z in range(2):
                dm_ref[z], dc_ref[z] = dm[z], dc[z]

        @pl.when(k > 0)
        def _():
            da_ref[0] += dar
            da_ref[1] += dai
            for z in range(2):
                dm_ref[z] += dm[z]
                dc_ref[z] += dc[z]

    big = pl.BlockSpec((None, 2, TC, SL), lambda d, j, k: (d, 0, k, j))
    tok = pl.BlockSpec((None, TC, CW), lambda d, j, k: (d, k, j))
    mat = pl.BlockSpec((None, 2, None, CW, SL), lambda d, j, k: (d, 0, j, 0, 0))
    return pl.pallas_call(
        body, name=name, grid=(2, NS, NCH),
        in_specs=[big, big,
                  pl.BlockSpec((None, 2, SUB, SL), lambda d, j, k: (d, 0, jnp.maximum(k * NG - 1, 0), j)),
                  pl.BlockSpec((None, 2, SUB, SL), lambda d, j, k: (d, 0, T // SUB - 1, j)),
                  tok, tok],
        out_specs=[mat, mat, pl.BlockSpec((None, 2, SUB, SL), lambda d, j, k: (d, 0, 0, j))],
        out_shape=[jax.ShapeDtypeStruct((2, 2, NS, CW, SL), F32), jax.ShapeDtypeStruct((2, 2, NS, CW, SL), F32),
                   jax.ShapeDtypeStruct((2, 2, SUB, N), F32)],
        scratch_shapes=[pltpu.VMEM((2, TC, SL), F32)],
        compiler_params=_params("parallel", "parallel", "arbitrary"),
    )(g, h, h, h, u, dy)


def _interleave(seq):
    *lead, T, W = seq.shape
    n = len(lead)
    return seq.reshape(*lead, S5_SEG, T // S5_SEG, W).swapaxes(n, n + 1).reshape(*lead, T, W)


def _deinterleave(seq):
    *lead, T, W = seq.shape
    n = len(lead)
    return seq.reshape(*lead, T // S5_SEG, S5_SEG, W).swapaxes(n, n + 1).reshape(*lead, T, W)


def _s5_discretize(lam_re, lam_im, log_dt, b_re, b_im):
    dt = jnp.exp(log_dt)[..., None]
    mag = jnp.exp(lam_re * dt)
    a_re = mag * jnp.cos(lam_im * dt)
    a_im = mag * jnp.sin(lam_im * dt)
    den = jnp.square(lam_re) + jnp.square(lam_im)
    f_re = ((a_re - 1.0) * lam_re + a_im * lam_im) / den
    f_im = (a_im * lam_re - (a_re - 1.0) * lam_im) / den
    bb_re = f_re[..., None] * b_re - f_im[..., None] * b_im
    bb_im = f_re[..., None] * b_im + f_im[..., None] * b_re
    return a_re, a_im, bb_re, bb_im


_GPS = S5_STRIP // SSM_STATE


def _blockdiag(t):
    d2, G, P, Cg = t.shape
    t5 = t.reshape(d2, G // _GPS, _GPS, P, Cg).transpose(0, 1, 2, 4, 3)
    m = t5[:, :, :, :, None, :] * jnp.eye(_GPS, dtype=t.dtype)[None, None, :, None, :, None]
    return m.reshape(d2, G // _GPS, _GPS * Cg, _GPS * P)


def _blockdiag_extract(m, Cg, P):
    d2, NS = m.shape[:2]
    m6 = m.reshape(d2, NS, _GPS, Cg, _GPS, P)
    diag = jnp.stack([m6[:, :, i, :, i, :] for i in range(_GPS)], axis=2)
    return diag.transpose(0, 1, 2, 4, 3).reshape(d2, NS * _GPS, P, Cg)


def _cmul(a, b):
    return a[0] * b[0] - a[1] * b[1], a[0] * b[1] + a[1] * b[0]


def _cpow(a, n):
    out, base = None, a
    while n:
        if n & 1:
            out = base if out is None else _cmul(out, base)
        base = _cmul(base, base)
        n >>= 1
    return out


def _segment_carry(fin, apow, rev):
    fr, fi = fin[:, 0], fin[:, 1]
    zero = jnp.zeros_like(fr[:, 0:1])
    cr, ci = zero, zero
    outs = [None] * S5_SEG
    order = range(S5_SEG - 1, -1, -1) if rev else range(S5_SEG)
    for s in order:
        outs[s] = (cr, ci)
        pr, pi = _cmul(apow, (cr, ci))
        cr, ci = pr + fr[:, s:s + 1], pi + fi[:, s:s + 1]
    return jnp.stack([jnp.concatenate([o[0] for o in outs], axis=1), jnp.concatenate([o[1] for o in outs], axis=1)], axis=1)


def _coords():
    x, y, c = lax.axis_index("x"), lax.axis_index("y"), lax.axis_index("c")
    others = [(1 - x, y), (x, 1 - y), (1 - x, 1 - y)]
    return x, y, c, 2 * x + y, others


def _comm(name, ins, out_shapes, aliases, n_local, n_remote, plan):
    n_in, n_out = len(ins), len(out_shapes)

    def body(*refs):
        in_refs, out_refs = refs[:n_in], refs[n_in:n_in + n_out]
        send_sems, recv_sems, local_sems = refs[n_in + n_out:]
        x, y, c = lax.axis_index("x"), lax.axis_index("y"), lax.axis_index("c")
        locs, sends, lands = plan(in_refs, out_refs)
        assert len(locs) == n_local and len(sends) == n_remote and len(lands) == n_remote
        local = [pltpu.make_async_copy(s, d, local_sems.at[i]) for i, (s, d) in enumerate(locs)]
        for cp in local:
            cp.start()
        remote = [pltpu.make_async_remote_copy(src_ref=s, dst_ref=d, send_sem=send_sems.at[i], recv_sem=recv_sems.at[i],
                                               device_id=peer, device_id_type=MESH)
                  for i, (s, d, peer) in enumerate(sends)]
        for cp in remote:
            cp.start()
        for i, d in enumerate(lands):
            pltpu.make_async_remote_copy(src_ref=d, dst_ref=d, send_sem=send_sems.at[i], recv_sem=recv_sems.at[i],
                                         device_id=(x, y, c), device_id_type=MESH).wait_recv()
        for cp in remote:
            cp.wait_send()
        for cp in local:
            cp.wait()

    any_spec = pl.BlockSpec(memory_space=pl.ANY)
    return pl.pallas_call(
        body, name=name,
        in_specs=[any_spec] * n_in, out_specs=[any_spec] * n_out,
        out_shape=[jax.ShapeDtypeStruct(s, d) for s, d in out_shapes],
        input_output_aliases=aliases,
        scratch_shapes=[pltpu.SemaphoreType.DMA((n_remote,)), pltpu.SemaphoreType.DMA((n_remote,)),
                        pltpu.SemaphoreType.DMA((max(n_local, 1),))],
        compiler_params=pltpu.CompilerParams(has_side_effects=True),
    )(*ins)


def allgather_dev(name, v):
    M, Nc = v.shape

    def plan(in_refs, out_refs):
        (v_ref,), (o_ref,) = in_refs, out_refs
        x, y, c = lax.axis_index("x"), lax.axis_index("y"), lax.axis_index("c")

        def rows(px, py, pc):
            return o_ref.at[pl.ds((4 * px + 2 * py + pc) * M, M), :]

        peers = [(x ^ fx, y ^ fy, c ^ fc) for fx in (0, 1) for fy in (0, 1) for fc in (0, 1) if fx or fy or fc]
        return ([(v_ref, rows(x, y, c))],
                [(v_ref, rows(x, y, c), p) for p in peers],
                [rows(*p) for p in peers])

    return _comm(name, [v], [((N_DEV * M, Nc), v.dtype)], {}, 1, N_DEV - 1, plan)[0]


def allgather_chips_1(name, shards):
    def plan(in_refs, out_refs):
        x, y, c, chip, others = _coords()
        locs, sends, lands = [], [], []
        for s_ref, g_ref in zip(in_refs, out_refs):
            hr = s_ref.shape[0] // 2
            mine = pl.ds(c * hr, hr)
            locs.append((s_ref, g_ref.at[chip]))
            for qx, qy in others:
                sends.append((s_ref.at[mine], g_ref.at[chip, mine], (qx, qy, c)))
                lands.append(g_ref.at[2 * qx + qy, mine])
        return locs, sends, lands

    n = len(shards)
    return _comm(name, shards, [((N_CHIP,) + s.shape, s.dtype) for s in shards], {}, n, 3 * n, plan)


def allgather_chips_2(name, gathered):
    def plan(in_refs, out_refs):
        x, y, c, chip, others = _coords()
        sends, lands = [], []
        for g_ref in out_refs:
            hr = g_ref.shape[1] // 2
            for qx, qy in others:
                q = 2 * qx + qy
                sends.append((g_ref.at[q, pl.ds(c * hr, hr)], g_ref.at[q, pl.ds(c * hr, hr)], (x, y, 1 - c)))
                lands.append(g_ref.at[q, pl.ds((1 - c) * hr, hr)])
        return [], sends, lands

    n = len(gathered)
    return _comm(name, gathered, [(g.shape, g.dtype) for g in gathered], {i: i for i in range(n)}, 0, 3 * n, plan)


def reduce_1(name, grads):
    def plan(in_refs, out_refs):
        x, y, c, chip, others = _coords()
        n = len(in_refs)
        locs, sends, lands = [], [], []
        for g_ref, own_ref, got_ref in zip(in_refs, out_refs[:n], out_refs[n:]):
            hr = g_ref.shape[1] // 2
            locs.append((g_ref.at[:, pl.ds(c * hr, hr), :], own_ref))
            sends.append((g_ref.at[:, pl.ds((1 - c) * hr, hr), :], got_ref, (x, y, 1 - c)))
            lands.append(got_ref)
        return locs, sends, lands

    n = len(grads)
    half = [((g.shape[0], g.shape[1] // 2, g.shape[2]), g.dtype) for g in grads]
    outs = _comm(name, grads, half + half, {}, n, n, plan)
    return outs[:n], outs[n:]


def reduce_2(name, parts):
    def plan(in_refs, out_refs):
        x, y, c, chip, others = _coords()
        locs, sends, lands = [], [], []
        for t_ref, q_ref in zip(in_refs, out_refs):
            locs.append((t_ref.at[chip], q_ref.at[chip]))
            for qx, qy in others:
                sends.append((t_ref.at[2 * qx + qy], q_ref.at[chip], (qx, qy, c)))
                lands.append(q_ref.at[2 * qx + qy])
        return locs, sends, lands

    n = len(parts)
    return _comm(name, parts, [(p.shape, p.dtype) for p in parts], {}, n, 3 * n, plan)


def reduce_3(name, halves):
    def plan(in_refs, out_refs):
        x, y, c, chip, others = _coords()
        locs, sends, lands = [], [], []
        for f_ref, o_ref in zip(in_refs, out_refs):
            locs.append((f_ref, o_ref.at[c]))
            sends.append((f_ref, o_ref.at[c], (x, y, 1 - c)))
            lands.append(o_ref.at[1 - c])
        return locs, sends, lands

    n = len(halves)
    return _comm(name, halves, [((2,) + h.shape, h.dtype) for h in halves], {}, n, n, plan)


_WEIGHTS = ['c_ctx', 'w_mod', 'b_mod', 'g_mix', 'g_ffn', 'w_in', 'ssm_lam_re', 'ssm_lam_im', 'ssm_log_dt', 'ssm_b_re',
            'ssm_b_im', 'ssm_c_re', 'ssm_c_im', 'ssm_d', 'ssm_w_glu', 'na_rpb', 'w_out', 'cv_w_pw1', 'cv_dw_w', 'cv_dw_b',
            'cv_ln_g', 'cv_ln_b', 'cv_w_pw2', 'ffn_w_up', 'ffn_conv_w', 'ffn_conv_b', 'ffn_w_down', 'g_out']
_INPUTS = ['x', 'c', 'ctx'] + _WEIGHTS + ['loss_target'] + ['m_' + w for w in _WEIGHTS] + ['v_' + w for w in _WEIGHTS]
_GATHERED_SMALL = ['ffn_conv_w', 'cv_dw_w', 'cv_dw_b', 'cv_ln_g', 'cv_ln_b']


def _silu(v):
    return v * jax.nn.sigmoid(v)


def _pack(arrs, cols):
    flat = jnp.concatenate([a.reshape(-1).astype(F32) for a in arrs])
    n = flat.shape[0]
    unit = SUB * cols
    flat = jnp.pad(flat, (0, (-n) % unit))
    return flat.reshape(-1, cols)


def _unpack(buf, shapes):
    flat = buf.reshape(-1)
    out, o = [], 0
    for s in shapes:
        n = int(np.prod(s))
        out.append(flat[o:o + n].reshape(s))
        o += n
    return out


def _ffn_fwd(tag, xin, sh, sc, gt, g, wup, cw3, cb3, wdn):
    hf = norm_mod_fwd(tag + "_norm", xin, g * (1.0 + sc), sh)
    up3 = mm_nn_pieces(tag + "_up", hf, wup, 0, N_CHIP, BF16, halves=2)
    act = ffn_mid_fwd(tag + "_mid", up3, cw3, cb3)
    yf = mm_nn(tag + "_down", act, wdn, BF16)
    return gate_res_fwd(tag + "_res", xin, yf, gt), (xin, hf, up3, act, yf)


def _ffn_bwd(tag, dxo, saved, sc, gt, g, wup, cw3, cb3, wdn):
    xin, hf, up3, act, yf = saved
    dyf, dgt = gate_res_bwd(tag + "_res_b", dxo, yf, gt)
    dact = mm_nt(tag + "_down_bx", dyf, wdn, BF16)
    dwdn = mm_tn(tag + "_down_bw", act, dyf, BF16)
    dup3, dcw3, dcb3 = ffn_mid_bwd(tag + "_mid_b", up3, dact, cw3, cb3)
    dhf = mm_nt_pieces(tag + "_up_bx", dup3, wup, BF16, halves=2)
    dwup = mm_tn_pieces(tag + "_up_bw", hf, dup3, N_CHIP, BF16, halves=2)
    dxi, cs1, cs2 = norm_mod_bwd(tag + "_norm_b", xin, dhf, g * (1.0 + sc), dxo)
    return dxi, dict(dsh=cs1[0], dsc=cs2[0] * g, dgt=dgt[0], dg=cs2[0] * (1.0 + sc), dwup=dwup, dwdn=dwdn,
                     dcw=dcw3.transpose(1, 0, 2).reshape(3, -1), dcb=dcb3.reshape(-1))


def kernel(x, c, ctx, c_ctx, w_mod, b_mod, g_mix, g_ffn, w_in, ssm_lam_re, ssm_lam_im, ssm_log_dt, ssm_b_re, ssm_b_im, ssm_c_re, ssm_c_im, ssm_d, ssm_w_glu, na_rpb, w_out, cv_w_pw1, cv_dw_w, cv_dw_b, cv_ln_g, cv_ln_b, cv_w_pw2, ffn_w_up, ffn_conv_w, ffn_conv_b, ffn_w_down, g_out, loss_target, m_c_ctx, m_w_mod, m_b_mod, m_g_mix, m_g_ffn, m_w_in, m_ssm_lam_re, m_ssm_lam_im, m_ssm_log_dt, m_ssm_b_re, m_ssm_b_im, m_ssm_c_re, m_ssm_c_im, m_ssm_d, m_ssm_w_glu, m_na_rpb, m_w_out, m_cv_w_pw1, m_cv_dw_w, m_cv_dw_b, m_cv_ln_g, m_cv_ln_b, m_cv_w_pw2, m_ffn_w_up, m_ffn_conv_w, m_ffn_conv_b, m_ffn_w_down, m_g_out, v_c_ctx, v_w_mod, v_b_mod, v_g_mix, v_g_ffn, v_w_in, v_ssm_lam_re, v_ssm_lam_im, v_ssm_log_dt, v_ssm_b_re, v_ssm_b_im, v_ssm_c_re, v_ssm_c_im, v_ssm_d, v_ssm_w_glu, v_na_rpb, v_w_out, v_cv_w_pw1, v_cv_dw_w, v_cv_dw_b, v_cv_ln_g, v_cv_ln_b, v_cv_w_pw2, v_ffn_w_up, v_ffn_conv_w, v_ffn_conv_b, v_ffn_w_down, v_g_out):
    p = dict(locals())
    xi, yi, ci = lax.axis_index("x"), lax.axis_index("y"), lax.axis_index("c")
    me, chip = 4 * xi + 2 * yi + ci, 2 * xi + yi
    xs, cx, tgt = x[0], ctx[0], loss_target[0]
    L, D = xs.shape
    Lc = cx.shape[0]
    T = L + Lc
    W = D // 2
    Cq = w_mod.shape[2]

    shards = [w_in[0], ssm_w_glu[0], w_out[0], cv_w_pw1[0], cv_w_pw2[0],
              ffn_w_up[0], ffn_w_up[1], ffn_w_down[0], ffn_w_down[1]]
    gathered = allgather_chips_2("gather_w_2", allgather_chips_1("gather_w_1", [s.astype(BF16) for s in shards]))
    Win, Wglu, Wout, Wpw1, Wpw2, Wup0, Wup1, Wdn0, Wdn1 = gathered
    Wglu, Wout, Wpw2 = (t.reshape(-1, t.shape[-1]) for t in (Wglu, Wout, Wpw2))
    Wdn = [Wdn0.reshape(-1, D), Wdn1.reshape(-1, D)]
    Wup = [Wup0, Wup1]
    Fd = Wdn[0].shape[0]

    small_shapes = [p[n].shape for n in _GATHERED_SMALL]
    sm = allgather_dev("gather_small", _pack([p[n] for n in _GATHERED_SMALL], 1024))
    sm = sm.reshape(N_DEV, -1)[0::2]
    per_chip = [_unpack(sm[q], small_shapes) for q in range(N_CHIP)]
    conv_w_f, dw_w_f, dw_b_f, ln_g_f, ln_b_f = (jnp.concatenate([pc[i] for pc in per_chip], axis=-1)
                                                for i in range(len(_GATHERED_SMALL)))
    cw3 = [conv_w_f[l].reshape(3, 2, Fd).transpose(1, 0, 2) for l in range(2)]
    cb3 = [ffn_conv_b[l].reshape(2, 1, Fd) for l in range(2)]
    dw_w_f, dw_b_f, ln_g_f, ln_b_f = dw_w_f[0], dw_b_f[0], ln_g_f[0], ln_b_f[0]

    c_all = allgather_dev("gather_c", jnp.zeros((SUB, D), F32).at[0].set(c[0])).reshape(N_DEV, SUB, D)[:, 0]
    S16 = jnp.concatenate([_silu(c_all), _silu(c_ctx)[None], jnp.zeros((2 * SUB - N_DEV - 1, D), F32)])
    modp = mm_nn_pieces("mod_fwd", S16, w_mod, 0, 2, F32, tm=2 * SUB)
    modg = allgather_dev("gather_mod", modp).reshape(N_DEV, 2 * SUB, 2, Cq)[0::2]
    mod_full = modg.transpose(2, 1, 0, 3).reshape(2, 2 * SUB, N_CHIP * Cq) + b_mod[:, None, :]
    mod_me = lax.dynamic_index_in_dim(mod_full, me, axis=1, keepdims=False)
    mods = [[mod_me[l, i * D:(i + 1) * D] for i in range(N_MOD)] for l in range(2)]
    shc, scc = mod_full[0, N_DEV, :D], mod_full[0, N_DEV, D:2 * D]

    sh_m, sc_m, gt_m, sh_f, sc_f, gt_f = mods[0]
    h0 = norm_mod_fwd("l0_norm", xs, g_mix[0] * (1.0 + sc_m), sh_m)
    hc0 = norm_mod_fwd("l0_norm_c", cx, g_mix[0] * (1.0 + scc), shc)
    u = mm_nn_pieces("l0_in_u", h0, Win, 0, 1, F32)
    qkv = mm_nn_pieces("l0_in_qkv", h0, Win, 1, 3, BF16)
    uc = mm_nn_pieces("l0_in_uc", hc0, Win, 0, 1, F32)
    kvc = mm_nn_pieces("l0_in_kvc", hc0, Win, 2, 2, BF16)

    lam_re, lam_im, log_dt = ssm_lam_re[0], ssm_lam_im[0], ssm_log_dt[0]
    b_re, b_im, c_re, c_im = ssm_b_re[0], ssm_b_im[0], ssm_c_re[0], ssm_c_im[0]
    (a_re, a_im, bb_re, bb_im), disc_vjp = jax.vjp(_s5_discretize, lam_re, lam_im, log_dt, b_re, b_im)
    G, P, Cg = bb_re.shape[1:]
    N = G * P
    a_re, a_im = a_re.reshape(2, 1, N), a_im.reshape(2, 1, N)
    a_f, a_b = jnp.stack([a_re, a_im], axis=1), jnp.stack([a_re, -a_im], axis=1)
    Bblk = jnp.stack([_blockdiag(bb_re), _blockdiag(bb_im)], axis=1)
    Cblk = jnp.stack([_blockdiag(c_re.swapaxes(-1, -2)), -_blockdiag(c_im.swapaxes(-1, -2))], axis=1)
    apow = _cpow((a_re, a_im), T // S5_SEG)

    useq = _interleave(jnp.stack([jnp.concatenate([uc, u]), jnp.concatenate([uc[::-1], u[::-1]])]))
    hloc, fin = s5_scan("s5_scan", useq, Bblk.astype(BF16), a_f, rev=False)
    hst, yseq = s5_fix("s5_fix", hloc, _segment_carry(fin, apow, False), a_f, Cblk.swapaxes(-1, -2).astype(BF16), rev=False)
    ys = _deinterleave(yseq)
    y0, y1 = ys[0, Lc:], ys[1, Lc:][::-1]
    s5o = glu_fwd("s5_glu", u, y0, y1, ssm_d[0], Wglu)

    bias = na_bias(na_rpb[0])
    o_na, lse = natten_fwd("na_fwd", qkv, kvc, bias)
    mixcat = jnp.concatenate([s5o, o_na], axis=1)
    ymix = mm_nn("l0_out", mixcat, Wout, BF16)
    x1 = gate_res_fwd("l0_res", xs, ymix, gt_m)
    x2, ffn0 = _ffn_fwd("f0", x1, sh_f, sc_f, gt_f, g_ffn[0], Wup[0], cw3[0], cb3[0], Wdn[0])

    sh_v, sc_v, gt_v, sh_g, sc_g, gt_g = mods[1]
    hcv = norm_mod_fwd("l1_norm", x2, g_mix[1] * (1.0 + sc_v), sh_v)
    ag3 = mm_nn_pieces("l1_pw1", hcv, Wpw1, 0, N_CHIP, BF16, halves=2)
    z1, z3 = conf_mid_fwd("l1_mid", ag3, dw_w_f, dw_b_f, ln_g_f, ln_b_f)
    ycv = mm_nn("l1_pw2", z3, Wpw2, BF16)
    x3 = gate_res_fwd("l1_res", x2, ycv, gt_v)
    x4, ffn1 = _ffn_fwd("f1", x3, sh_g, sc_g, gt_g, g_ffn[1], Wup[1], cw3[1], cb3[1], Wdn[1])

    dx4, dg_out, loss_part = loss_head("loss", x4, g_out, tgt)
    loss = lax.psum(loss_part[0, 0], ("x", "y", "c"))

    dx3, gf1 = _ffn_bwd("f1", dx4, ffn1, sc_g, gt_g, g_ffn[1], Wup[1], cw3[1], cb3[1], Wdn[1])
    dycv, dgt_v = gate_res_bwd("l1_res_b", dx3, ycv, gt_v)
    dz3 = mm_nt("l1_pw2_bx", dycv, Wpw2, BF16)
    dWpw2 = mm_tn("l1_pw2_bw", z3, dycv, BF16)
    dz1, dln_g, dln_b = conf_ln_bwd("l1_ln_b", z1, dz3, ln_g_f, ln_b_f)
    dag3, ddw_w, ddw_b = conf_conv_bwd("l1_conv_b", ag3, dz1, dw_w_f)
    dhcv = mm_nt_pieces("l1_pw1_bx", dag3, Wpw1, BF16, halves=2)
    dWpw1 = mm_tn_pieces("l1_pw1_bw", hcv, dag3, N_CHIP, BF16, halves=2)
    dx2, cs1_v, cs2_v = norm_mod_bwd("l1_norm_b", x2, dhcv, g_mix[1] * (1.0 + sc_v), dx3)

    dx1, gf0 = _ffn_bwd("f0", dx2, ffn0, sc_f, gt_f, g_ffn[0], Wup[0], cw3[0], cb3[0], Wdn[0])
    dymix, dgt_m = gate_res_bwd("l0_res_b", dx1, ymix, gt_m)
    dmix = mm_nt("l0_out_bx", dymix, Wout, BF16)
    dWout = mm_tn("l0_out_bw", mixcat, dymix, BF16)
    dq, dk, dv, dkc, dvc, dbias = natten_bwd("na_bwd", qkv, kvc, bias, o_na, lse, dmix)
    dy, zg, dzz, dd_skip = glu_bwd("s5_glu_b", u, y0, y1, ssm_d[0], Wglu, dmix)
    dWglu = mm_tn("s5_glu_bw", zg, dzz, BF16)

    zc = jnp.zeros((Lc, W), F32)
    dyseq = _interleave(jnp.stack([jnp.concatenate([zc, dy]), jnp.concatenate([zc, dy[::-1]])]))
    gloc, gfin = s5_scan("s5_scan_b", dyseq, Cblk.astype(BF16), a_b, rev=True)
    apow_b = (apow[0], -apow[1])
    gst, duseq = s5_fix("s5_fix_b", gloc, _segment_carry(gfin, apow_b, True), a_b, Bblk.swapaxes(-1, -2).astype(BF16), rev=True)
    dBm, dCm, da8 = s5_grads("s5_grads", gst, hst, useq, dyseq)
    dus = _deinterleave(duseq)
    du = fma3("s5_du", dy, dus[0, Lc:], dus[1, Lc:][::-1], ssm_d[0], BF16)
    duc = dus[0, :Lc] + dus[1, :Lc][::-1]

    d_in = jnp.concatenate([du, dq, dk.astype(BF16), dv.astype(BF16)], axis=1)
    d_in_c = jnp.concatenate([duc.astype(BF16), jnp.zeros((Lc, W), BF16), dkc.astype(BF16), dvc.astype(BF16)], axis=1)
    dh0 = mm_nt_pieces("l0_in_bx", d_in, Win, BF16)
    dhc0 = mm_nt_pieces("l0_in_bxc", d_in_c, Win, BF16)
    dWin = mm_tn_pieces("l0_in_bw", jnp.concatenate([hc0, h0]), jnp.concatenate([d_in_c, d_in]), N_CHIP, BF16)
    dx0, cs1_m, cs2_m = norm_mod_bwd("l0_norm_b", xs, dh0, g_mix[0] * (1.0 + sc_m), dx1)
    _, cs1_c, cs2_c = norm_mod_bwd("l0_norm_bc", cx, dhc0, g_mix[0] * (1.0 + scc), jnp.zeros_like(cx))

    dmod0 = jnp.concatenate([cs1_m[0], cs2_m[0] * g_mix[0], dgt_m[0], gf0["dsh"], gf0["dsc"], gf0["dgt"]])
    dmod1 = jnp.concatenate([cs1_v[0], cs2_v[0] * g_mix[1], dgt_v[0], gf1["dsh"], gf1["dsc"], gf1["dgt"]])
    dmodc = jnp.concatenate([cs1_c[0], cs2_c[0] * g_mix[0], jnp.zeros((4 * D,), F32)])
    dm_rows = jnp.concatenate([jnp.stack([dmod0, dmod1, dmodc]), jnp.zeros((SUB - 3, N_MOD * D), F32)])
    dm_all = allgather_dev("gather_dmod", dm_rows).reshape(N_DEV, SUB, N_MOD * D)
    dm_sum = sum_lead("sum_dmod", dm_all, F32)
    pad7 = jnp.zeros((2 * SUB - N_DEV - 1, N_MOD * D), F32)
    dMod = [jnp.concatenate([dm_all[:, 0], dm_sum[2:3], pad7]), jnp.concatenate([dm_all[:, 1], jnp.zeros_like(dm_sum[2:3]), pad7])]
    dMod_cols = [lax.dynamic_slice_in_dim(m, chip * Cq, Cq, axis=1) for m in dMod]
    g_w_mod = jnp.stack([mm_tn("mod_bw%d" % l, S16, dMod_cols[l], F32) for l in range(2)])
    g_b_mod = jnp.stack([dm_sum[0] + dm_sum[2], dm_sum[1]])
    ds_part = mm_nt("mod_bx", dMod_cols[0], w_mod[0], F32, tm=2 * SUB)
    ds_all = allgather_dev("gather_dsc", jnp.zeros((SUB, D), F32).at[0].set(ds_part[N_DEV]))
    ds_c = sum_lead("sum_dsc", ds_all.reshape(N_DEV, SUB, D)[0::2], F32)[0]
    sg_c = jax.nn.sigmoid(c_ctx)
    g_c_ctx = ds_c * sg_c * (1.0 + c_ctx * (1.0 - sg_c))

    H = W // NA_HEAD_DIM
    db5 = dbias.reshape(H, NA_WIN_R, GRID_W, NA_WIN_R, GRID_W).transpose(0, 1, 3, 2, 4).reshape(H * NA_WIN_R * NA_WIN_R, GRID_W * GRID_W)
    dcol = mm_nt("na_bias_fold", db5, na_bias_fold_matrix(), F32, exact=True)
    dcol = dcol.reshape(H, NA_WIN_R, NA_WIN_R, LANE)[..., :2 * NA_WIN_C - 1]
    ridx = np.arange(NA_WIN_R)[None, :] - np.arange(NA_WIN_R)[:, None] + (NA_WIN_R - 1)
    rsel = jnp.asarray(ridx[:, :, None] == np.arange(2 * NA_WIN_R - 1)[None, None, :], F32)
    g_rpb_loc = jnp.einsum("hoic,oir->hrc", dcol, rsel)

    dbb = [_blockdiag_extract(dBm[:, z], Cg, P) for z in range(2)]
    dcc = [_blockdiag_extract(dCm[:, z], Cg, P).swapaxes(-1, -2) for z in range(2)]
    da = jnp.sum(da8, axis=2).reshape(2, 2, G, P)
    small = {
        "g_mix": jnp.stack([cs2_m[0] * (1.0 + sc_m) + cs2_c[0] * (1.0 + scc), cs2_v[0] * (1.0 + sc_v)]),
        "g_ffn": jnp.stack([gf0["dg"], gf1["dg"]]),
        "a_re": da[:, 0], "a_im": da[:, 1], "bb_re": dbb[0], "bb_im": dbb[1], "c_re": dcc[0], "c_im": -dcc[1],
        "ssm_d": dd_skip, "na_rpb": g_rpb_loc, "cv_dw_w": ddw_w, "cv_dw_b": ddw_b, "cv_ln_g": dln_g, "cv_ln_b": dln_b,
        "ffn_conv_w": jnp.stack([gf0["dcw"], gf1["dcw"]]), "ffn_conv_b": jnp.stack([gf0["dcb"], gf1["dcb"]]),
        "g_out": dg_out,
    }
    skeys = list(small)
    sbuf = _pack([small[k] for k in skeys], 1024)
    sall = allgather_dev("gather_small_g", sbuf).reshape(N_DEV, sbuf.shape[0], 1024)
    ssum = dict(zip(skeys, _unpack(sum_lead("sum_small_g", sall, F32), [small[k].shape for k in skeys])))
    g_lam_re, g_lam_im, g_log_dt, g_b_re, g_b_im = disc_vjp((ssum["a_re"], ssum["a_im"], ssum["bb_re"], ssum["bb_im"]))

    def my_cols(t):
        n = t.shape[-1] // N_CHIP
        return lax.dynamic_slice_in_dim(t, chip * n, n, axis=t.ndim - 1)

    big = [dWin, dWglu.reshape(N_CHIP, -1, W), dWout.reshape(N_CHIP, -1, D), dWpw1, dWpw2.reshape(N_CHIP, -1, D),
           gf0["dwup"], gf1["dwup"], gf0["dwdn"].reshape(N_CHIP, -1, D), gf1["dwdn"].reshape(N_CHIP, -1, D)]
    own, got = reduce_1("reduce_g_1", big)
    parts = [add2("reduce_add_%d" % i, a, b, BF16) for i, (a, b) in enumerate(zip(own, got))]
    slots = reduce_2("reduce_g_2", parts)
    halves = [sum_lead("reduce_sum_%d" % i, s.reshape(N_CHIP, s.shape[1], s.shape[2]), F32) for i, s in enumerate(slots)]
    full = [f.reshape(-1, f.shape[-1]) for f in reduce_3("reduce_g_3", halves)]
    gWin, gWglu, gWout, gWpw1, gWpw2, gWup0, gWup1, gWdn0, gWdn1 = full

    grads = {
        "c_ctx": g_c_ctx, "w_mod": g_w_mod, "b_mod": g_b_mod, "g_mix": ssum["g_mix"], "g_ffn": ssum["g_ffn"],
        "w_in": gWin[None], "ssm_lam_re": g_lam_re[None], "ssm_lam_im": g_lam_im[None], "ssm_log_dt": g_log_dt[None],
        "ssm_b_re": g_b_re[None], "ssm_b_im": g_b_im[None], "ssm_c_re": ssum["c_re"][None], "ssm_c_im": ssum["c_im"][None],
        "ssm_d": ssum["ssm_d"], "ssm_w_glu": gWglu[None], "na_rpb": ssum["na_rpb"][None], "w_out": gWout[None],
        "cv_w_pw1": gWpw1[None], "cv_dw_w": my_cols(ssum["cv_dw_w"])[None], "cv_dw_b": my_cols(ssum["cv_dw_b"]),
        "cv_ln_g": my_cols(ssum["cv_ln_g"]), "cv_ln_b": my_cols(ssum["cv_ln_b"]), "cv_w_pw2": gWpw2[None],
        "ffn_w_up": jnp.stack([gWup0, gWup1]), "ffn_conv_w": my_cols(ssum["ffn_conv_w"]), "ffn_conv_b": ssum["ffn_conv_b"],
        "ffn_w_down": jnp.stack([gWdn0, gWdn1]), "g_out": ssum["g_out"][0],
    }
    grads = {k: grads[k].reshape(p[k].shape) for k in _WEIGHTS}

    large = [k for k in _WEIGHTS if p[k].size >= (1 << 18)]
    tiny = [k for k in _WEIGHTS if k not in large]
    delta, new_m, new_v = {}, {}, {}
    for k in large:
        delta[k], new_m[k], new_v[k] = adamw("adamw_" + k, p[k], grads[k], p["m_" + k], p["v_" + k])
    packs = [_pack([src[pre + k] for k in tiny], 1024) for src, pre in ((p, ""), (grads, ""), (p, "m_"), (p, "v_"))]
    outs = adamw("adamw_small", *packs)
    shapes = [p[k].shape for k in tiny]
    for dst, buf in zip((delta, new_m, new_v), outs):
        dst.update(zip(tiny, _unpack(buf, shapes)))

    return (loss, dx0[None], *[grads[k] for k in _WEIGHTS], *[delta[k] for k in _WEIGHTS],
            *[new_m[k] for k in _WEIGHTS], *[new_v[k] for k in _WEIGHTS])
```

```python
import functools
import math

import numpy as np
import jax
import jax.numpy as jnp
from jax import lax
from jax.experimental import pallas as pl
from jax.experimental.pallas import tpu as pltpu

F32, BF16 = jnp.float32, jnp.bfloat16
MESH = pl.DeviceIdType.MESH
V7X_VMEM_LIMIT = 56 << 20
LANE, SUB = 128, 8
N_CHIP, N_DEV = 4, 8

GRID_W = 64
N_MOD = 6
SSM_GROUP, SSM_STATE = 16, 64
NA_HEAD_DIM, NA_WIN_R, NA_WIN_C = 128, 8, 16
EPS = 1e-6
NEG = -1e30
ADAM_LR, ADAM_B1, ADAM_B2, ADAM_EPS, ADAM_WD, ADAM_STEP = 0.001, 0.9, 0.999, 1e-08, 0.01, 10
S5_STRIP = 512
S5_SEG = 8

NN = (((1,), (0,)), ((), ()))
NT = (((1,), (1,)), ((), ()))
TN = (((0,), (0,)), ((), ()))


def _params(*sem):
    return pltpu.CompilerParams(dimension_semantics=sem if sem else None, vmem_limit_bytes=V7X_VMEM_LIMIT)


def _pick(n, pref, mult=LANE):
    if n <= pref:
        return n
    best = None
    for t in range(mult, pref + 1, mult):
        if n % t == 0:
            best = t
    assert best is not None, (n, pref, mult)
    return best


def _sigmoid(x):
    return 1.0 / (1.0 + jnp.exp(-x))


def _mm(name, a, b, *, dims, grid, a_spec, b_spec, o_spec, out_shape, out_dtype, acc_shape, exact=False):
    nk = grid[2]

    def body(a_ref, b_ref, o_ref, *scratch):
        if exact:
            part = lax.dot_general(a_ref[...], b_ref[...], dims, preferred_element_type=F32,
                                   precision=lax.Precision.HIGHEST)
        else:
            part = lax.dot_general(a_ref[...].astype(BF16), b_ref[...].astype(BF16), dims,
                                   preferred_element_type=F32)
        if nk == 1:
            o_ref[...] = part.astype(o_ref.dtype)
        else:
            acc = scratch[0]
            kk = pl.program_id(2)

            @pl.when(kk == 0)
            def _():
                acc[...] = part

            @pl.when(kk > 0)
            def _():
                acc[...] += part

            @pl.when(kk == nk - 1)
            def _():
                o_ref[...] = acc[...].astype(o_ref.dtype)

    return pl.pallas_call(
        body, name=name, grid=grid, in_specs=[a_spec, b_spec], out_specs=o_spec,
        out_shape=jax.ShapeDtypeStruct(out_shape, out_dtype),
        scratch_shapes=[] if nk == 1 else [pltpu.VMEM(acc_shape, F32)],
        compiler_params=_params("parallel", "parallel", "arbitrary"),
    )(a, b)


def mm_nn_pieces(name, a, w, p0, n_p, out_dtype, halves=1, tm=512, tn=512):
    M, K = a.shape
    Nq = w.shape[2]
    tm, tn = _pick(M, tm, SUB), _pick(Nq, tn)
    tpp = Nq // tn
    pph = n_p // halves
    if halves == 1:
        o_spec = pl.BlockSpec((tm, tn), lambda i, j, k: (i, j))
        oshape = (M, n_p * Nq)
    else:
        o_spec = pl.BlockSpec((None, tm, tn), lambda i, j, k: ((j // tpp) // pph, i, ((j // tpp) % pph) * tpp + j % tpp))
        oshape = (halves, M, pph * Nq)
    return _mm(name, a, w, dims=NN, grid=(M // tm, n_p * tpp, 1),
               a_spec=pl.BlockSpec((tm, K), lambda i, j, k: (i, 0)),
               b_spec=pl.BlockSpec((None, K, tn), lambda i, j, k: (p0 + j // tpp, 0, j % tpp)),
               o_spec=o_spec, out_shape=oshape, out_dtype=out_dtype, acc_shape=(tm, tn))


def mm_nn(name, a, w, out_dtype, tm=512, tn=512, exact=False):
    M, K = a.shape
    N = w.shape[1]
    tm, tn = _pick(M, tm, SUB), _pick(N, tn)
    return _mm(name, a, w, dims=NN, grid=(M // tm, N // tn, 1),
               a_spec=pl.BlockSpec((tm, K), lambda i, j, k: (i, 0)),
               b_spec=pl.BlockSpec((K, tn), lambda i, j, k: (0, j)),
               o_spec=pl.BlockSpec((tm, tn), lambda i, j, k: (i, j)),
               out_shape=(M, N), out_dtype=out_dtype, acc_shape=(tm, tn), exact=exact)


def mm_nt(name, dy, w, out_dtype, tm=512, tn=512, exact=False):
    M, N = dy.shape
    K = w.shape[0]
    tm, tn = _pick(M, tm, SUB), _pick(K, tn)
    return _mm(name, dy, w, dims=NT, grid=(M // tm, K // tn, 1),
               a_spec=pl.BlockSpec((tm, N), lambda i, j, k: (i, 0)),
               b_spec=pl.BlockSpec((tn, N), lambda i, j, k: (j, 0)),
               o_spec=pl.BlockSpec((tm, tn), lambda i, j, k: (i, j)),
               out_shape=(M, K), out_dtype=out_dtype, acc_shape=(tm, tn), exact=exact)


def mm_nt_pieces(name, dy, w, out_dtype, halves=1, tm=512, tn=512):
    P, K, Nq = w.shape
    M = dy.shape[-2]
    tm, tn = _pick(M, tm, SUB), _pick(K, tn)
    pph = P // halves
    if halves == 1:
        a_spec = pl.BlockSpec((tm, Nq), lambda i, j, k: (i, k))
    else:
        a_spec = pl.BlockSpec((None, tm, Nq), lambda i, j, k: (k // pph, i, k % pph))
    return _mm(name, dy, w, dims=NT, grid=(M // tm, K // tn, P),
               a_spec=a_spec,
               b_spec=pl.BlockSpec((None, tn, Nq), lambda i, j, k: (k, j, 0)),
               o_spec=pl.BlockSpec((tm, tn), lambda i, j, k: (i, j)),
               out_shape=(M, K), out_dtype=out_dtype, acc_shape=(tm, tn))


def mm_tn(name, a, dy, out_dtype, tm=512, tn=512):
    M, K = a.shape
    N = dy.shape[1]
    tm, tn = _pick(K, tm), _pick(N, tn)
    return _mm(name, a, dy, dims=TN, grid=(K // tm, N // tn, 1),
               a_spec=pl.BlockSpec((M, tm), lambda i, j, k: (0, i)),
               b_spec=pl.BlockSpec((M, tn), lambda i, j, k: (0, j)),
               o_spec=pl.BlockSpec((tm, tn), lambda i, j, k: (i, j)),
               out_shape=(K, N), out_dtype=out_dtype, acc_shape=(tm, tn))


def mm_tn_pieces(name, a, dy, n_p, out_dtype, halves=1, tm=512, tn=256):
    M, K = a.shape
    Nq = (dy.shape[-1] * halves) // n_p
    tm, tn = _pick(K, tm), _pick(Nq, tn)
    tpp = Nq // tn
    pph = n_p // halves
    if halves == 1:
        b_spec = pl.BlockSpec((M, tn), lambda i, j, k: (0, j))
    else:
        b_spec = pl.BlockSpec((None, M, tn), lambda i, j, k: ((j // tpp) // pph, 0, ((j // tpp) % pph) * tpp + j % tpp))
    return _mm(name, a, dy, dims=TN, grid=(K // tm, n_p * tpp, 1),
               a_spec=pl.BlockSpec((M, tm), lambda i, j, k: (0, i)),
               b_spec=b_spec,
               o_spec=pl.BlockSpec((None, tm, tn), lambda i, j, k: (j // tpp, i, j % tpp)),
               out_shape=(n_p, K, Nq), out_dtype=out_dtype, acc_shape=(tm, tn))


def _row_call(name, body, ins, in_kinds, outs, rows, tr, scratch=()):
    def spec(kind, shape):
        if isinstance(kind, pl.BlockSpec):
            return kind
        if kind == "row":
            return pl.BlockSpec((tr,) + tuple(shape[1:]), lambda i: (i,) + (0,) * (len(shape) - 1))
        return pl.BlockSpec(tuple(shape), lambda i: (0,) * len(shape))

    return pl.pallas_call(
        body, name=name, grid=(rows // tr,),
        in_specs=[spec(k, a.shape) for k, a in zip(in_kinds, ins)],
        out_specs=[spec(k, s) for k, s, _ in outs],
        out_shape=[jax.ShapeDtypeStruct(s, d) for _, s, d in outs],
        scratch_shapes=list(scratch),
        compiler_params=_params("arbitrary"),
    )(*ins)


def _acc(ref, val):
    @pl.when(pl.program_id(0) == 0)
    def _():
        ref[...] = val

    @pl.when(pl.program_id(0) > 0)
    def _():
        ref[...] += val


def norm_mod_fwd(name, x, w, b, tr=256):
    rows, d = x.shape
    tr = _pick(rows, tr, SUB)

    def body(x_ref, w_ref, b_ref, h_ref):
        xv = x_ref[...]
        r = lax.rsqrt(jnp.mean(xv * xv, axis=-1, keepdims=True) + EPS)
        h_ref[...] = (xv * r * w_ref[...] + b_ref[...]).astype(BF16)

    return _row_call(name, body, [x, w.reshape(1, d), b.reshape(1, d)], ["row", "vec", "vec"],
                     [("row", (rows, d), BF16)], rows, tr)[0]


def norm_mod_bwd(name, x, dh, w, dx_in, tr=256):
    rows, d = x.shape
    tr = _pick(rows, tr, SUB)

    def body(x_ref, dh_ref, w_ref, dxi_ref, dx_ref, cs1_ref, cs2_ref):
        xv = x_ref[...]
        r = lax.rsqrt(jnp.mean(xv * xv, axis=-1, keepdims=True) + EPS)
        xn = xv * r
        dhv = dh_ref[...].astype(F32)
        dxn = dhv * w_ref[...]
        dx_ref[...] = dxi_ref[...] + r * (dxn - xn * jnp.mean(dxn * xn, axis=-1, keepdims=True))
        _acc(cs1_ref, jnp.sum(dhv, axis=0, keepdims=True))
        _acc(cs2_ref, jnp.sum(dhv * xn, axis=0, keepdims=True))

    return _row_call(name, body, [x, dh, w.reshape(1, d), dx_in], ["row", "row", "vec", "row"],
                     [("row", (rows, d), F32), ("acc", (1, d), F32), ("acc", (1, d), F32)], rows, tr)


def gate_res_fwd(name, x, y, gate, tr=256):
    rows, d = x.shape
    tr = _pick(rows, tr, SUB)

    def body(x_ref, y_ref, g_ref, o_ref):
        o_ref[...] = x_ref[...] + g_ref[...] * y_ref[...].astype(F32)

    return _row_call(name, body, [x, y, gate.reshape(1, d)], ["row", "row", "vec"],
                     [("row", (rows, d), F32)], rows, tr)[0]


def gate_res_bwd(name, dx, y, gate, tr=256):
    rows, d = dx.shape
    tr = _pick(rows, tr, SUB)

    def body(dx_ref, y_ref, g_ref, dy_ref, dg_ref):
        dxv = dx_ref[...]
        dy_ref[...] = (g_ref[...] * dxv).astype(BF16)
        _acc(dg_ref, jnp.sum(dxv * y_ref[...].astype(F32), axis=0, keepdims=True))

    return _row_call(name, body, [dx, y, gate.reshape(1, d)], ["row", "row", "vec"],
                     [("row", (rows, d), BF16), ("acc", (1, d), F32)], rows, tr)


def loss_head(name, x, g, target, tr=256):
    rows, d = x.shape
    tr = _pick(rows, tr, SUB)

    def body(x_ref, g_ref, t_ref, dx_ref, dg_ref, loss_ref):
        xv = x_ref[...]
        r = lax.rsqrt(jnp.mean(xv * xv, axis=-1, keepdims=True) + EPS)
        xn = xv * r
        err = xn * g_ref[...] - t_ref[...]
        dy = err * (1.0 / d)
        dxn = dy * g_ref[...]
        dx_ref[...] = r * (dxn - xn * jnp.mean(dxn * xn, axis=-1, keepdims=True))
        _acc(dg_ref, jnp.sum(dy * xn, axis=0, keepdims=True))
        part = 0.5 * jnp.sum(jnp.sum(err * err, axis=-1, keepdims=True) * (1.0 / d), axis=0, keepdims=True)
        _acc(loss_ref, jnp.broadcast_to(part, (1, LANE)))

    return _row_call(name, body, [x, g.reshape(1, d), target], ["row", "vec", "row"],
                     [("row", (rows, d), F32), ("acc", (1, d), F32), ("acc", (1, LANE), F32)], rows, tr)


def fma3(name, a, b, c, dvec, out_dtype, tr=256):
    rows, d = a.shape
    tr = _pick(rows, tr, SUB)

    def body(a_ref, b_ref, c_ref, d_ref, o_ref):
        o_ref[...] = (d_ref[...] * a_ref[...] + b_ref[...] + c_ref[...]).astype(o_ref.dtype)

    return _row_call(name, body, [a, b, c, dvec.reshape(1, d)], ["row", "row", "row", "vec"],
                     [("row", (rows, d), out_dtype)], rows, tr)[0]


def add2(name, a, b, out_dtype, tr=512):
    shape = a.shape
    a2, b2 = a.reshape(-1, shape[-1]), b.reshape(-1, shape[-1])
    rows = a2.shape[0]
    tr = _pick(rows, tr, 16)

    def body(a_ref, b_ref, o_ref):
        o_ref[...] = (a_ref[...].astype(F32) + b_ref[...].astype(F32)).astype(o_ref.dtype)

    out = _row_call(name, body, [a2, b2], ["row", "row"], [("row", a2.shape, out_dtype)], rows, tr)[0]
    return out.reshape(shape)


def sum_lead(name, a, out_dtype, tr=512):
    n, rows, cols = a.shape
    tr = _pick(rows, tr, 16)

    def body(a_ref, o_ref):
        acc = a_ref[0].astype(F32)
        for s in range(1, n):
            acc = acc + a_ref[s].astype(F32)
        o_ref[...] = acc.astype(o_ref.dtype)

    return pl.pallas_call(
        body, name=name, grid=(rows // tr,),
        in_specs=[pl.BlockSpec((n, tr, cols), lambda i: (0, i, 0))],
        out_specs=pl.BlockSpec((tr, cols), lambda i: (i, 0)),
        out_shape=jax.ShapeDtypeStruct((rows, cols), out_dtype),
        compiler_params=_params("parallel"),
    )(a)


def adamw(name, w, g, m, v, tr=512):
    shape = w.shape
    cols = shape[-1]
    w2, g2, m2, v2 = (t.reshape(-1, cols) for t in (w, g, m, v))
    rows = w2.shape[0]
    tr, tc = _pick(rows, 256, SUB), _pick(cols, 1536)
    c1 = 1.0 - ADAM_B1 ** ADAM_STEP
    c2 = 1.0 - ADAM_B2 ** ADAM_STEP

    def body(w_ref, g_ref, m_ref, v_ref, d_ref, mo_ref, vo_ref):
        gv = g_ref[...]
        mn = ADAM_B1 * m_ref[...] + (1.0 - ADAM_B1) * gv
        vn = ADAM_B2 * v_ref[...] + (1.0 - ADAM_B2) * (gv * gv)
        mo_ref[...] = mn
        vo_ref[...] = vn
        d_ref[...] = -ADAM_LR * ((mn / c1) / (jnp.sqrt(vn / c2) + ADAM_EPS) + ADAM_WD * w_ref[...])

    blk = pl.BlockSpec((tr, tc), lambda i, j: (i, j))
    outs = pl.pallas_call(
        body, name=name, grid=(rows // tr, cols // tc), in_specs=[blk] * 4, out_specs=[blk] * 3,
        out_shape=[jax.ShapeDtypeStruct(w2.shape, F32)] * 3, compiler_params=_params("parallel", "parallel"),
    )(w2, g2, m2, v2)
    return tuple(o.reshape(shape) for o in outs)


def add_half(name, grad, got, c_idx, tr=256):
    Pn, R, C = grad.shape
    hr = R // 2
    tr = _pick(hr, tr, HALO)
    nb = hr // tr

    def body(c_ref, a_ref, b_ref, o_ref):
        o_ref[...] = (a_ref[...].astype(F32) + b_ref[...].astype(F32)).astype(o_ref.dtype)

    return pl.pallas_call(
        body, name=name,
        grid_spec=pltpu.PrefetchScalarGridSpec(
            num_scalar_prefetch=1, grid=(Pn, nb),
            in_specs=[pl.BlockSpec((None, tr, C), lambda q, i, c: (q, c[0] * nb + i, 0)),
                      pl.BlockSpec((None, tr, C), lambda q, i, c: (q, i, 0))],
            out_specs=pl.BlockSpec((None, tr, C), lambda q, i, c: (q, i, 0))),
        out_shape=jax.ShapeDtypeStruct((Pn, hr, C), BF16),
        compiler_params=_params("parallel", "parallel"),
    )(c_idx, grad, got)


def sum_slots(name, slots, mine, ids, tr=256):
    Pn, hr, C = slots.shape
    tr = _pick(hr, tr, HALO)

    def body(ids_ref, m_ref, s1_ref, s2_ref, s3_ref, o_ref):
        o_ref[...] = (m_ref[...].astype(F32) + s1_ref[...].astype(F32)) + (s2_ref[...].astype(F32) + s3_ref[...].astype(F32))

    def other(k):
        return pl.BlockSpec((None, tr, C), lambda i, ids: ((ids[0] + k) % Pn, i, 0))

    return pl.pallas_call(
        body, name=name,
        grid_spec=pltpu.PrefetchScalarGridSpec(
            num_scalar_prefetch=1, grid=(hr // tr,),
            in_specs=[pl.BlockSpec((None, tr, C), lambda i, ids: (ids[0], i, 0)), other(1), other(2), other(3)],
            out_specs=pl.BlockSpec((None, tr, C), lambda i, ids: (ids[1], i, 0))),
        out_shape=jax.ShapeDtypeStruct((2, hr, C), F32),
        compiler_params=_params("parallel"),
    )(ids, mine, slots, slots, slots)


HALO = 16


def _halo_specs(lead, R, tn, n_rows, col_of):
    nb, nblk = R // HALO, n_rows // HALO

    def mk(rows, row_of):
        return pl.BlockSpec((lead, rows, tn), lambda *g: (0, row_of(g[-1]), col_of(g)))

    return (mk(HALO, lambda i: jnp.maximum(i * nb - 1, 0)), mk(R, lambda i: i),
            mk(HALO, lambda i: jnp.minimum((i + 1) * nb, nblk - 1)))


def _fill_halo(dst, i, last, R, prev, cur, nxt):
    nd = len(dst.shape)
    lead = (slice(None),) * (nd - 2)
    dst[lead + (slice(0, HALO), slice(None))] = jnp.where(i == 0, 0.0, prev)
    dst[lead + (slice(HALO, HALO + R), slice(None))] = cur
    dst[lead + (slice(HALO + R, HALO + R + HALO), slice(None))] = jnp.where(i == last, 0.0, nxt)


def ffn_mid_fwd(name, up3, cw, cb, R=256, tn=256):
    _, L, Fd = up3.shape
    R, tn = _pick(L, R, HALO), _pick(Fd, tn)
    nrow = L // R

    def body(p_ref, c_ref, n_ref, w_ref, b_ref, act_ref, s_ref):
        i = pl.program_id(1)
        _fill_halo(s_ref, i, nrow - 1, R, p_ref[...].astype(F32), c_ref[...].astype(F32), n_ref[...].astype(F32))
        cv = b_ref[...]
        for k in range(3):
            cv = cv + w_ref[:, k:k + 1, :] * s_ref[:, pl.ds(HALO - 1 + k, R), :]
        u, g = cv[0], cv[1]
        act_ref[...] = (u * g * _sigmoid(g)).astype(BF16)

    hs = _halo_specs(2, R, tn, L, lambda g: g[0])
    return pl.pallas_call(
        body, name=name, grid=(Fd // tn, nrow),
        in_specs=[*hs, pl.BlockSpec((2, 3, tn), lambda j, i: (0, 0, j)), pl.BlockSpec((2, 1, tn), lambda j, i: (0, 0, j))],
        out_specs=pl.BlockSpec((R, tn), lambda j, i: (i, j)),
        out_shape=jax.ShapeDtypeStruct((L, Fd), BF16),
        scratch_shapes=[pltpu.VMEM((2, R + 2 * HALO, tn), F32)],
        compiler_params=_params("parallel", "arbitrary"),
    )(up3, up3, up3, cw, cb)


def ffn_mid_bwd(name, up3, dact, cw, cb, R=256, tn=256):
    _, L, Fd = up3.shape
    R, tn = _pick(L, R, HALO), _pick(Fd, tn)
    nrow = L // R
    E = R + HALO

    def body(pu, cu, nu, pd, cd, nd, w_ref, b_ref, dup_ref, dcw_ref, dcb_ref, s_ref, d_ref, e_ref):
        i = pl.program_id(1)
        _fill_halo(s_ref, i, nrow - 1, R, pu[...].astype(F32), cu[...].astype(F32), nu[...].astype(F32))
        _fill_halo(d_ref, i, nrow - 1, R, pd[0].astype(F32), cd[0].astype(F32), nd[0].astype(F32))
        cv = b_ref[...]
        for k in range(3):
            cv = cv + w_ref[:, k:k + 1, :] * s_ref[:, pl.ds(HALO - 9 + k, E), :]
        da = d_ref[pl.ds(HALO - 8, E), :]
        u, g = cv[0], cv[1]
        sg = _sigmoid(g)
        e_ref[0] = da * g * sg
        e_ref[1] = da * u * sg * (1.0 + g * (1.0 - sg))
        dup = jnp.zeros((2, R, tn), F32)
        for k in range(3):
            dup = dup + w_ref[:, k:k + 1, :] * e_ref[:, pl.ds(9 - k, R), :]
        dup_ref[...] = dup.astype(BF16)
        dc = e_ref[:, pl.ds(8, R), :]

        @pl.when(i == 0)
        def _():
            dcw_ref[...] = jnp.zeros_like(dcw_ref)
            dcb_ref[...] = jnp.zeros_like(dcb_ref)

        dcb_ref[...] += jnp.sum(dc, axis=1, keepdims=True)
        for k in range(3):
            dcw_ref[:, k:k + 1, :] += jnp.sum(dc * s_ref[:, pl.ds(HALO - 1 + k, R), :], axis=1, keepdims=True)

    hu = _halo_specs(2, R, tn, L, lambda g: g[0])
    hd = _halo_specs(1, R, tn, L, lambda g: g[0])
    return pl.pallas_call(
        body, name=name, grid=(Fd // tn, nrow),
        in_specs=[*hu, *hd, pl.BlockSpec((2, 3, tn), lambda j, i: (0, 0, j)), pl.BlockSpec((2, 1, tn), lambda j, i: (0, 0, j))],
        out_specs=[pl.BlockSpec((2, R, tn), lambda j, i: (0, i, j)), pl.BlockSpec((2, 3, tn), lambda j, i: (0, 0, j)),
                   pl.BlockSpec((2, 1, tn), lambda j, i: (0, 0, j))],
        out_shape=[jax.ShapeDtypeStruct((2, L, Fd), BF16), jax.ShapeDtypeStruct((2, 3, Fd), F32),
                   jax.ShapeDtypeStruct((2, 1, Fd), F32)],
        scratch_shapes=[pltpu.VMEM((2, R + 2 * HALO, tn), F32), pltpu.VMEM((R + 2 * HALO, tn), F32),
                        pltpu.VMEM((2, E, tn), F32)],
        compiler_params=_params("parallel", "arbitrary"),
    )(up3, up3, up3, dact[None], dact[None], dact[None], cw, cb)


def _glu_z0(blk):
    return blk[0].astype(F32) * _sigmoid(blk[1].astype(F32))


def conf_mid_fwd(name, ag3, dw_w, dw_b, ln_g, ln_b, R=128, cb=256):
    _, L, C = ag3.shape
    K = dw_w.shape[0]
    pad = (K - 1) // 2
    assert pad <= HALO
    R, cb = _pick(L, R, HALO), _pick(C, cb)
    nrow = L // R

    def body(p_ref, c_ref, n_ref, w_ref, b_ref, g_ref, bb_ref, z1_ref, z3_ref, s_ref):
        i = pl.program_id(0)
        _fill_halo(s_ref, i, nrow - 1, R, _glu_z0(p_ref), _glu_z0(c_ref), _glu_z0(n_ref))
        for c0 in range(0, C, cb):
            acc = jnp.broadcast_to(b_ref[:, c0:c0 + cb], (R, cb))
            for k in range(K):
                acc = acc + w_ref[k:k + 1, c0:c0 + cb] * s_ref[pl.ds(HALO - pad + k, R), c0:c0 + cb]
            z1_ref[:, c0:c0 + cb] = acc
        z1 = z1_ref[...]
        zc = z1 - jnp.mean(z1, axis=-1, keepdims=True)
        zn = zc * lax.rsqrt(jnp.mean(zc * zc, axis=-1, keepdims=True) + EPS)
        z2 = zn * g_ref[...] + bb_ref[...]
        z3_ref[...] = (z2 * _sigmoid(z2)).astype(BF16)

    hs = _halo_specs(2, R, C, L, lambda g: 0)
    vec = pl.BlockSpec((1, C), lambda i: (0, 0))
    return pl.pallas_call(
        body, name=name, grid=(nrow,),
        in_specs=[*hs, pl.BlockSpec((K, C), lambda i: (0, 0)), vec, vec, vec],
        out_specs=[pl.BlockSpec((R, C), lambda i: (i, 0)), pl.BlockSpec((R, C), lambda i: (i, 0))],
        out_shape=[jax.ShapeDtypeStruct((L, C), F32), jax.ShapeDtypeStruct((L, C), BF16)],
        scratch_shapes=[pltpu.VMEM((R + 2 * HALO, C), F32)],
        compiler_params=_params("parallel"),
    )(ag3, ag3, ag3, dw_w, dw_b.reshape(1, C), ln_g.reshape(1, C), ln_b.reshape(1, C))


def conf_ln_bwd(name, z1, dz3, ln_g, ln_b, tr=256):
    rows, C = z1.shape
    tr = _pick(rows, tr, HALO)

    def body(z_ref, d_ref, g_ref, b_ref, dz_ref, dg_ref, db_ref):
        z1v = z_ref[...]
        zc = z1v - jnp.mean(z1v, axis=-1, keepdims=True)
        rs = lax.rsqrt(jnp.mean(zc * zc, axis=-1, keepdims=True) + EPS)
        zn = zc * rs
        z2 = zn * g_ref[...] + b_ref[...]
        sg = _sigmoid(z2)
        dz2 = d_ref[...].astype(F32) * sg * (1.0 + z2 * (1.0 - sg))
        _acc(dg_ref, jnp.sum(dz2 * zn, axis=0, keepdims=True))
        _acc(db_ref, jnp.sum(dz2, axis=0, keepdims=True))
        dzn = dz2 * g_ref[...]
        dz1 = rs * (dzn - jnp.mean(dzn, axis=-1, keepdims=True) - zn * jnp.mean(dzn * zn, axis=-1, keepdims=True))
        dz_ref[...] = dz1.astype(BF16)

    return _row_call(name, body, [z1, dz3, ln_g.reshape(1, C), ln_b.reshape(1, C)], ["row", "row", "vec", "vec"],
                     [("row", (rows, C), BF16), ("acc", (1, C), F32), ("acc", (1, C), F32)], rows, tr)


def conf_conv_bwd(name, ag3, dz1, dw_w, R=128, cb=256):
    _, L, C = ag3.shape
    K = dw_w.shape[0]
    pad = (K - 1) // 2
    R, cb = _pick(L, R, HALO), _pick(C, cb)
    nrow = L // R

    def body(pa, ca, na, pd, cd, nd, w_ref, dag_ref, dw_ref, db_ref, s_ref, d_ref, z_ref):
        i = pl.program_id(0)
        _fill_halo(s_ref, i, nrow - 1, R, _glu_z0(pa), _glu_z0(ca), _glu_z0(na))
        _fill_halo(d_ref, i, nrow - 1, R, pd[0].astype(F32), cd[0].astype(F32), nd[0].astype(F32))

        @pl.when(i == 0)
        def _():
            dw_ref[...] = jnp.zeros_like(dw_ref)
            db_ref[...] = jnp.zeros_like(db_ref)

        for c0 in range(0, C, cb):
            cs = slice(c0, c0 + cb)
            dcur = d_ref[pl.ds(HALO, R), cs]
            acc = jnp.zeros((R, cb), F32)
            for k in range(K):
                acc = acc + w_ref[k:k + 1, cs] * d_ref[pl.ds(HALO + pad - k, R), cs]
                dw_ref[k:k + 1, cs] += jnp.sum(dcur * s_ref[pl.ds(HALO - pad + k, R), cs], axis=0, keepdims=True)
            z_ref[:, cs] = acc
            db_ref[:, cs] += jnp.sum(dcur, axis=0, keepdims=True)
        dz0 = z_ref[...]
        a = ca[0].astype(F32)
        sg = _sigmoid(ca[1].astype(F32))
        dag_ref[0] = (dz0 * sg).astype(BF16)
        dag_ref[1] = (dz0 * a * sg * (1.0 - sg)).astype(BF16)

    ha = _halo_specs(2, R, C, L, lambda g: 0)
    hd = _halo_specs(1, R, C, L, lambda g: 0)
    return pl.pallas_call(
        body, name=name, grid=(nrow,),
        in_specs=[*ha, *hd, pl.BlockSpec((K, C), lambda i: (0, 0))],
        out_specs=[pl.BlockSpec((2, R, C), lambda i: (0, i, 0)), pl.BlockSpec((K, C), lambda i: (0, 0)),
                   pl.BlockSpec((1, C), lambda i: (0, 0))],
        out_shape=[jax.ShapeDtypeStruct((2, L, C), BF16), jax.ShapeDtypeStruct((K, C), F32),
                   jax.ShapeDtypeStruct((1, C), F32)],
        scratch_shapes=[pltpu.VMEM((R + 2 * HALO, C), F32), pltpu.VMEM((R + 2 * HALO, C), F32), pltpu.VMEM((R, C), F32)],
        compiler_params=_params("arbitrary"),
    )(ag3, ag3, ag3, dz1[None], dz1[None], dz1[None], dw_w)


_GELU_C = math.sqrt(2.0 / math.pi)


def _gelu(x):
    return 0.5 * x * (1.0 + jnp.tanh(_GELU_C * (x + 0.044715 * x * x * x)))


def _gelu_grad(x):
    t = jnp.tanh(_GELU_C * (x + 0.044715 * x * x * x))
    return 0.5 * (1.0 + t) + 0.5 * x * (1.0 - t * t) * _GELU_C * (1.0 + 3.0 * 0.044715 * x * x)


def glu_fwd(name, u, y0, y1, d, wg, tr=512):
    rows, W = u.shape
    tr = _pick(rows, tr, HALO)

    def body(u_ref, y0_ref, y1_ref, d_ref, w_ref, o_ref):
        z = _gelu(d_ref[...] * u_ref[...] + y0_ref[...] + y1_ref[...])
        zz = jnp.dot(z.astype(BF16), w_ref[...], preferred_element_type=F32)
        o_ref[...] = (z * _sigmoid(zz)).astype(BF16)

    return _row_call(name, body, [u, y0, y1, d.reshape(1, W), wg], ["row", "row", "row", "vec", "vec"],
                     [("row", (rows, W), BF16)], rows, tr)[0]


def glu_bwd(name, u, y0, y1, d, wg, dmix, tr=512):
    rows, W = u.shape
    tr = _pick(rows, tr, HALO)

    def body(u_ref, y0_ref, y1_ref, d_ref, w_ref, do_ref, dy_ref, z_ref, dzz_ref, dd_ref):
        uv = u_ref[...]
        y = d_ref[...] * uv + y0_ref[...] + y1_ref[...]
        z = _gelu(y)
        zz = jnp.dot(z.astype(BF16), w_ref[...], preferred_element_type=F32)
        sg = _sigmoid(zz)
        do = do_ref[...].astype(F32)
        dzz = (do * z * sg * (1.0 - sg)).astype(BF16)
        dz = do * sg + lax.dot_general(dzz, w_ref[...], NT, preferred_element_type=F32)
        dy = dz * _gelu_grad(y)
        dy_ref[...] = dy
        z_ref[...] = z.astype(BF16)
        dzz_ref[...] = dzz
        _acc(dd_ref, jnp.sum(dy * uv, axis=0, keepdims=True))

    do_spec = pl.BlockSpec((tr, W), lambda i: (i, 0))
    return _row_call(name, body, [u, y0, y1, d.reshape(1, W), wg, dmix], ["row", "row", "row", "vec", "vec", do_spec],
                     [("row", (rows, W), F32), ("row", (rows, W), BF16), ("row", (rows, W), BF16), ("acc", (1, W), F32)],
                     rows, tr)


NA_KEYS = NA_WIN_R * GRID_W


def na_bias(rpb):
    H, nr, nc = rpb.shape
    e, ok = _na_col_select()
    rp = jnp.pad(rpb.reshape(H * nr, nc), ((0, (-H * nr) % SUB), (0, LANE - nc)))
    cols = mm_nn("na_bias_mm", rp, jnp.asarray(e, F32), F32, exact=True)[:H * nr].reshape(H, nr, GRID_W * GRID_W)
    cols = cols + jnp.asarray(np.where(ok, 0.0, NEG), F32)
    b = jnp.stack([cols[:, NA_WIN_R - 1 - o:2 * NA_WIN_R - 1 - o] for o in range(NA_WIN_R)], axis=1)
    b = b.reshape(H, NA_WIN_R, NA_WIN_R, GRID_W, GRID_W)
    return b.transpose(0, 1, 3, 2, 4).reshape(H, NA_WIN_R, GRID_W, NA_KEYS)


def _na_col_select():
    q = np.arange(GRID_W)
    cs = np.clip(q - NA_WIN_C // 2, 0, GRID_W - NA_WIN_C)
    ok = ((q[None, :] >= cs[:, None]) & (q[None, :] < cs[:, None] + NA_WIN_C)).reshape(-1)
    cidx = np.clip(q[None, :] - q[:, None] + (NA_WIN_C - 1), 0, 2 * NA_WIN_C - 2).reshape(-1)
    return (cidx[None, :] == np.arange(LANE)[:, None]) & ok[None, :], ok


def na_bias_fold_matrix():
    return jnp.asarray(_na_col_select()[0], F32)


def _na_window(r, rows):
    kr0 = jnp.clip(r - NA_WIN_R // 2, 0, rows - NA_WIN_R)
    return pl.multiple_of(kr0 * GRID_W, GRID_W), r - kr0


def natten_fwd(name, qkv, kvc, bias):
    L = qkv.shape[0]
    NA = qkv.shape[1] // 3
    H, rows, Lc = NA // NA_HEAD_DIM, L // GRID_W, kvc.shape[0]
    scale = NA_HEAD_DIM ** -0.5
    hd = NA_HEAD_DIM

    def body(q_ref, k_ref, v_ref, kc_ref, vc_ref, b_ref, o_ref, lse_ref):
        st, off = _na_window(pl.program_id(1), rows)
        q = q_ref[...]
        s_loc = lax.dot_general(q, k_ref[pl.ds(st, NA_KEYS), :], NT, preferred_element_type=F32) * scale + b_ref[off]
        s_ctx = lax.dot_general(q, kc_ref[...], NT, preferred_element_type=F32) * scale
        m = jnp.maximum(jnp.max(s_loc, axis=-1, keepdims=True), jnp.max(s_ctx, axis=-1, keepdims=True))
        p_loc, p_ctx = jnp.exp(s_loc - m), jnp.exp(s_ctx - m)
        l = jnp.sum(p_loc, axis=-1, keepdims=True) + jnp.sum(p_ctx, axis=-1, keepdims=True)
        o = (jnp.dot(p_loc.astype(BF16), v_ref[pl.ds(st, NA_KEYS), :], preferred_element_type=F32)
             + jnp.dot(p_ctx.astype(BF16), vc_ref[...], preferred_element_type=F32))
        o_ref[...] = (o / l).astype(BF16)
        lse_ref[...] = m + jnp.log(l)

    return pl.pallas_call(
        body, name=name, grid=(H, rows),
        in_specs=[pl.BlockSpec((GRID_W, hd), lambda h, r: (r, h)),
                  pl.BlockSpec((L, hd), lambda h, r: (0, H + h)),
                  pl.BlockSpec((L, hd), lambda h, r: (0, 2 * H + h)),
                  pl.BlockSpec((Lc, hd), lambda h, r: (0, h)),
                  pl.BlockSpec((Lc, hd), lambda h, r: (0, H + h)),
                  pl.BlockSpec((None, NA_WIN_R, GRID_W, NA_KEYS), lambda h, r: (h, 0, 0, 0))],
        out_specs=[pl.BlockSpec((GRID_W, hd), lambda h, r: (r, h)),
                   pl.BlockSpec((None, GRID_W, 1), lambda h, r: (h, r, 0))],
        out_shape=[jax.ShapeDtypeStruct((L, NA), BF16), jax.ShapeDtypeStruct((H, L, 1), F32)],
        compiler_params=_params("parallel", "arbitrary"),
    )(qkv, qkv, qkv, kvc, kvc, bias)


def natten_bwd(name, qkv, kvc, bias, o, lse, dmix):
    L = qkv.shape[0]
    NA = qkv.shape[1] // 3
    H, rows, Lc = NA // NA_HEAD_DIM, L // GRID_W, kvc.shape[0]
    scale = NA_HEAD_DIM ** -0.5
    hd = NA_HEAD_DIM

    def body(q_ref, k_ref, v_ref, kc_ref, vc_ref, b_ref, o_ref, lse_ref, do_ref,
             dq_ref, dk_ref, dv_ref, dkc_ref, dvc_ref, db_ref):
        r = pl.program_id(1)
        st, off = _na_window(r, rows)

        @pl.when(r == 0)
        def _():
            for ref in (dk_ref, dv_ref, dkc_ref, dvc_ref, db_ref):
                ref[...] = jnp.zeros_like(ref)

        q, kl, vl, kc, vc = q_ref[...], k_ref[pl.ds(st, NA_KEYS), :], v_ref[pl.ds(st, NA_KEYS), :], kc_ref[...], vc_ref[...]
        do = do_ref[...]
        lse_v = lse_ref[...]
        p_loc = jnp.exp(lax.dot_general(q, kl, NT, preferred_element_type=F32) * scale + b_ref[off] - lse_v)
        p_ctx = jnp.exp(lax.dot_general(q, kc, NT, preferred_element_type=F32) * scale - lse_v)
        delta = jnp.sum(do.astype(F32) * o_ref[...].astype(F32), axis=-1, keepdims=True)
        ds_loc = p_loc * (lax.dot_general(do, vl, NT, preferred_element_type=F32) - delta)
        ds_ctx = p_ctx * (lax.dot_general(do, vc, NT, preferred_element_type=F32) - delta)
        dsl, dsc = ds_loc.astype(BF16), ds_ctx.astype(BF16)
        dq = jnp.dot(dsl, kl, preferred_element_type=F32) + jnp.dot(dsc, kc, preferred_element_type=F32)
        dq_ref[...] = (dq * scale).astype(BF16)
        dk_ref[pl.ds(st, NA_KEYS), :] += lax.dot_general(dsl, q, TN, preferred_element_type=F32) * scale
        dv_ref[pl.ds(st, NA_KEYS), :] += lax.dot_general(p_loc.astype(BF16), do, TN, preferred_element_type=F32)
        dkc_ref[...] += lax.dot_general(dsc, q, TN, preferred_element_type=F32) * scale
        dvc_ref[...] += lax.dot_general(p_ctx.astype(BF16), do, TN, preferred_element_type=F32)
        db_ref[off] += ds_loc

    tok = pl.BlockSpec((GRID_W, hd), lambda h, r: (r, h))
    return pl.pallas_call(
        body, name=name, grid=(H, rows),
        in_specs=[tok,
                  pl.BlockSpec((L, hd), lambda h, r: (0, H + h)),
                  pl.BlockSpec((L, hd), lambda h, r: (0, 2 * H + h)),
                  pl.BlockSpec((Lc, hd), lambda h, r: (0, h)),
                  pl.BlockSpec((Lc, hd), lambda h, r: (0, H + h)),
                  pl.BlockSpec((None, NA_WIN_R, GRID_W, NA_KEYS), lambda h, r: (h, 0, 0, 0)),
                  tok,
                  pl.BlockSpec((None, GRID_W, 1), lambda h, r: (h, r, 0)),
                  pl.BlockSpec((GRID_W, hd), lambda h, r: (r, H + h))],
        out_specs=[tok,
                   pl.BlockSpec((L, hd), lambda h, r: (0, h)),
                   pl.BlockSpec((L, hd), lambda h, r: (0, h)),
                   pl.BlockSpec((Lc, hd), lambda h, r: (0, h)),
                   pl.BlockSpec((Lc, hd), lambda h, r: (0, h)),
                   pl.BlockSpec((None, NA_WIN_R, GRID_W, NA_KEYS), lambda h, r: (h, 0, 0, 0))],
        out_shape=[jax.ShapeDtypeStruct((L, NA), BF16), jax.ShapeDtypeStruct((L, NA), F32), jax.ShapeDtypeStruct((L, NA), F32),
                   jax.ShapeDtypeStruct((Lc, NA), F32), jax.ShapeDtypeStruct((Lc, NA), F32),
                   jax.ShapeDtypeStruct(bias.shape, F32)],
        compiler_params=_params("parallel", "arbitrary"),
    )(qkv, qkv, qkv, kvc, kvc, bias, o, lse, dmix)


def _s5_dims(T, N):
    TC = T // S5_SEG
    assert T % (S5_SEG * SUB * 2) == 0 and N % S5_STRIP == 0
    return TC, TC // SUB, S5_SEG, N // S5_STRIP


def _s5_backward(d, rev):
    return (d == 1) != rev


def s5_scan(name, xin, mats, a, rev):
    _, T, W = xin.shape
    N = a.shape[-1]
    TC, NG, NCH, NS = _s5_dims(T, N)
    CW, SL = W // NS, S5_STRIP

    def ck(d, k):
        return jnp.where(_s5_backward(d, rev), NCH - 1 - k, k)

    def body(x_ref, m_ref, a_ref, h_ref, f_ref, carry):
        @pl.when(pl.program_id(2) == 0)
        def _():
            carry[...] = jnp.zeros_like(carry)

        xb = x_ref[...].astype(BF16)
        h_ref[0] = jnp.dot(xb, m_ref[0], preferred_element_type=F32)
        h_ref[1] = jnp.dot(xb, m_ref[1], preferred_element_type=F32)
        ar, ai = jnp.broadcast_to(a_ref[0], (SUB, SL)), jnp.broadcast_to(a_ref[1], (SUB, SL))
        bw = _s5_backward(pl.program_id(0), rev)

        def step(t, c):
            hr, hi = c
            row = pl.multiple_of(jnp.where(bw, NG - 1 - t, t) * SUB, SUB)
            nr = ar * hr - ai * hi + h_ref[0, pl.ds(row, SUB), :]
            ni = ar * hi + ai * hr + h_ref[1, pl.ds(row, SUB), :]
            h_ref[0, pl.ds(row, SUB), :] = nr
            h_ref[1, pl.ds(row, SUB), :] = ni
            return nr, ni

        hr, hi = lax.fori_loop(0, NG, step, (carry[0], carry[1]))
        carry[0], carry[1] = hr, hi
        f_ref[0], f_ref[1] = hr, hi

    return pl.pallas_call(
        body, name=name, grid=(2, NS, NCH),
        in_specs=[pl.BlockSpec((None, TC, CW), lambda d, j, k: (d, ck(d, k), j)),
                  pl.BlockSpec((None, 2, None, CW, SL), lambda d, j, k: (d, 0, j, 0, 0)),
                  pl.BlockSpec((None, 2, 1, SL), lambda d, j, k: (d, 0, 0, j))],
        out_specs=[pl.BlockSpec((None, 2, TC, SL), lambda d, j, k: (d, 0, ck(d, k), j)),
                   pl.BlockSpec((None, 2, SUB, SL), lambda d, j, k: (d, 0, 0, j))],
        out_shape=[jax.ShapeDtypeStruct((2, 2, T, N), F32), jax.ShapeDtypeStruct((2, 2, SUB, N), F32)],
        scratch_shapes=[pltpu.VMEM((2, SUB, SL), F32)],
        compiler_params=_params("parallel", "parallel", "arbitrary"),
    )(xin, mats, a)


def s5_fix(name, hloc, hin, a, mats, rev):
    _, _, T, N = hloc.shape
    TC, NG, NCH, NS = _s5_dims(T, N)
    SL = S5_STRIP
    CW = mats.shape[-1]

    def ck(d, k):
        return jnp.where(_s5_backward(d, rev), NCH - 1 - k, k)

    def body(h_ref, hin_ref, a_ref, m_ref, ho_ref, y_ref, g):
        @pl.when(pl.program_id(2) == 0)
        def _():
            g[...] = hin_ref[...]

        ar, ai = jnp.broadcast_to(a_ref[0], (SUB, SL)), jnp.broadcast_to(a_ref[1], (SUB, SL))
        bw = _s5_backward(pl.program_id(0), rev)

        def step(t, c):
            gr, gi = c
            row = pl.multiple_of(jnp.where(bw, NG - 1 - t, t) * SUB, SUB)
            nr = ar * gr - ai * gi
            ni = ar * gi + ai * gr
            ho_ref[0, pl.ds(row, SUB), :] = h_ref[0, pl.ds(row, SUB), :] + nr
            ho_ref[1, pl.ds(row, SUB), :] = h_ref[1, pl.ds(row, SUB), :] + ni
            return nr, ni

        gr, gi = lax.fori_loop(0, NG, step, (g[0], g[1]))
        g[0], g[1] = gr, gi
        y_ref[...] = (jnp.dot(ho_ref[0].astype(BF16), m_ref[0], preferred_element_type=F32)
                      + jnp.dot(ho_ref[1].astype(BF16), m_ref[1], preferred_element_type=F32))

    return pl.pallas_call(
        body, name=name, grid=(2, NS, NCH),
        in_specs=[pl.BlockSpec((None, 2, TC, SL), lambda d, j, k: (d, 0, ck(d, k), j)),
                  pl.BlockSpec((None, 2, SUB, SL), lambda d, j, k: (d, 0, 0, j)),
                  pl.BlockSpec((None, 2, 1, SL), lambda d, j, k: (d, 0, 0, j)),
                  pl.BlockSpec((None, 2, None, SL, CW), lambda d, j, k: (d, 0, j, 0, 0))],
        out_specs=[pl.BlockSpec((None, 2, TC, SL), lambda d, j, k: (d, 0, ck(d, k), j)),
                   pl.BlockSpec((None, TC, CW), lambda d, j, k: (d, ck(d, k), j))],
        out_shape=[jax.ShapeDtypeStruct((2, 2, T, N), F32), jax.ShapeDtypeStruct((2, T, NS * CW), F32)],
        scratch_shapes=[pltpu.VMEM((2, SUB, SL), F32)],
        compiler_params=_params("parallel", "parallel", "arbitrary"),
    )(hloc, hin, a, mats)


def s5_grads(name, g, h, u, dy):
    _, _, T, N = g.shape
    W = u.shape[-1]
    TC, NG, NCH, NS = _s5_dims(T, N)
    CW, SL = W // NS, S5_STRIP

    def body(g_ref, h_ref, hp_ref, hl_ref, u_ref, dy_ref, dm_ref, dc_ref, da_ref, hs):
        k = pl.program_id(2)
        sub = lax.broadcasted_iota(jnp.int32, (SUB, SL), 0)

        @pl.when(pl.program_id(0) == 0)
        def _():
            for z in range(2):
                wrapped = jnp.where(sub == 0, 0.0, pltpu.roll(hl_ref[z], 1, 0))
                hs[z, 0:SUB, :] = jnp.where(k == 0, wrapped, hp_ref[z])
                hs[z, SUB:TC, :] = h_ref[z, 0:TC - SUB, :]

        @pl.when(pl.program_id(0) == 1)
        def _():
            for z in range(2):
                wrapped = jnp.where(sub == SUB - 1, 0.0, pltpu.roll(hl_ref[z], SUB - 1, 0))
                hs[z, TC - SUB:TC, :] = jnp.where(k == NCH - 1, wrapped, hp_ref[z])
                hs[z, 0:TC - SUB, :] = h_ref[z, SUB:TC, :]

        gr, gi, pr, pi = g_ref[0], g_ref[1], hs[0], hs[1]
        dar = jnp.sum((gr * pr + gi * pi).reshape(NG, SUB, SL), axis=0)
        dai = jnp.sum((gi * pr - gr * pi).reshape(NG, SUB, SL), axis=0)
        ub, dyb = u_ref[...].astype(BF16), dy_ref[...].astype(BF16)
        dm = [lax.dot_general(ub, g_ref[z].astype(BF16), TN, preferred_element_type=F32) for z in range(2)]
        dc = [lax.dot_general(dyb, h_ref[z].astype(BF16), TN, preferred_element_type=F32) for z in range(2)]

        @pl.when(k == 0)
        def _():
            da_ref[0], da_ref[1] = dar, dai
            for z in range(2):
                dm_ref[z], dc_ref[z] = dm[z], dc[z]

        @pl.when(k > 0)
        def _():
            da_ref[0] += dar
            da_ref[1] += dai
            for z in range(2):
                dm_ref[z] += dm[z]
                dc_ref[z] += dc[z]

    big = pl.BlockSpec((None, 2, TC, SL), lambda d, j, k: (d, 0, k, j))
    tok = pl.BlockSpec((None, TC, CW), lambda d, j, k: (d, k, j))
    mat = pl.BlockSpec((None, 2, None, CW, SL), lambda d, j, k: (d, 0, j, 0, 0))
    return pl.pallas_call(
        body, name=name, grid=(2, NS, NCH),
        in_specs=[big, big,
                  pl.BlockSpec((None, 2, SUB, SL), lambda d, j, k: (
                      d, 0, jnp.where(d == 0, jnp.maximum(k * NG - 1, 0), jnp.minimum((k + 1) * NG, T // SUB - 1)), j)),
                  pl.BlockSpec((None, 2, SUB, SL), lambda d, j, k: (d, 0, jnp.where(d == 0, T // SUB - 1, 0), j)),
                  tok, tok],
        out_specs=[mat, mat, pl.BlockSpec((None, 2, SUB, SL), lambda d, j, k: (d, 0, 0, j))],
        out_shape=[jax.ShapeDtypeStruct((2, 2, NS, CW, SL), F32), jax.ShapeDtypeStruct((2, 2, NS, CW, SL), F32),
                   jax.ShapeDtypeStruct((2, 2, SUB, N), F32)],
        scratch_shapes=[pltpu.VMEM((2, TC, SL), F32)],
        compiler_params=_params("parallel", "parallel", "arbitrary"),
    )(g, h, h, h, u, dy)


def _interleave(seq):
    *lead, T, W = seq.shape
    n = len(lead)
    return seq.reshape(*lead, S5_SEG, T // S5_SEG, W).swapaxes(n, n + 1).reshape(*lead, T, W)


def _deinterleave(seq):
    *lead, T, W = seq.shape
    n = len(lead)
    return seq.reshape(*lead, T // S5_SEG, S5_SEG, W).swapaxes(n, n + 1).reshape(*lead, T, W)


def _s5_discretize(lam_re, lam_im, log_dt, b_re, b_im):
    dt = jnp.exp(log_dt)[..., None]
    mag = jnp.exp(lam_re * dt)
    a_re = mag * jnp.cos(lam_im * dt)
    a_im = mag * jnp.sin(lam_im * dt)
    den = jnp.square(lam_re) + jnp.square(lam_im)
    f_re = ((a_re - 1.0) * lam_re + a_im * lam_im) / den
    f_im = (a_im * lam_re - (a_re - 1.0) * lam_im) / den
    bb_re = f_re[..., None] * b_re - f_im[..., None] * b_im
    bb_im = f_re[..., None] * b_im + f_im[..., None] * b_re
    return a_re, a_im, bb_re, bb_im


_GPS = S5_STRIP // SSM_STATE


def _blockdiag(t):
    d2, G, P, Cg = t.shape
    t5 = t.reshape(d2, G // _GPS, _GPS, P, Cg).transpose(0, 1, 2, 4, 3)
    m = t5[:, :, :, :, None, :] * jnp.eye(_GPS, dtype=t.dtype)[None, None, :, None, :, None]
    return m.reshape(d2, G // _GPS, _GPS * Cg, _GPS * P)


def _blockdiag_extract(m, Cg, P):
    d2, NS = m.shape[:2]
    m6 = m.reshape(d2, NS, _GPS, Cg, _GPS, P)
    diag = jnp.stack([m6[:, :, i, :, i, :] for i in range(_GPS)], axis=2)
    return diag.transpose(0, 1, 2, 4, 3).reshape(d2, NS * _GPS, P, Cg)


def _cmul(a, b):
    return a[0] * b[0] - a[1] * b[1], a[0] * b[1] + a[1] * b[0]


def _cpow(a, n):
    out, base = None, a
    while n:
        if n & 1:
            out = base if out is None else _cmul(out, base)
        base = _cmul(base, base)
        n >>= 1
    return out


def _segment_carry(fin, apow, rev):
    per_dir = []
    for d in range(2):
        fr, fi = fin[d, 0], fin[d, 1]
        ap = (apow[0][d], apow[1][d])
        cr = ci = jnp.zeros_like(fr[0:1])
        outs = [None] * S5_SEG
        backward = (d == 1) != rev
        for s in (range(S5_SEG - 1, -1, -1) if backward else range(S5_SEG)):
            outs[s] = (cr, ci)
            pr, pi = _cmul(ap, (cr, ci))
            cr, ci = pr + fr[s:s + 1], pi + fi[s:s + 1]
        per_dir.append(jnp.stack([jnp.concatenate([o[0] for o in outs]), jnp.concatenate([o[1] for o in outs])]))
    return jnp.stack(per_dir)


def _coords():
    x, y, c = lax.axis_index("x"), lax.axis_index("y"), lax.axis_index("c")
    others = [(1 - x, y), (x, 1 - y), (1 - x, 1 - y)]
    return x, y, c, 2 * x + y, others


def _comm(name, ins, out_shapes, aliases, n_local, n_remote, plan):
    n_in, n_out = len(ins), len(out_shapes)

    def body(*refs):
        in_refs, out_refs = refs[:n_in], refs[n_in:n_in + n_out]
        send_sems, recv_sems, local_sems = refs[n_in + n_out:]
        x, y, c = lax.axis_index("x"), lax.axis_index("y"), lax.axis_index("c")
        locs, sends, lands = plan(in_refs, out_refs)
        assert len(locs) == n_local and len(sends) == n_remote and len(lands) == n_remote
        local = [pltpu.make_async_copy(s, d, local_sems.at[i]) for i, (s, d) in enumerate(locs)]
        for cp in local:
            cp.start()
        remote = [pltpu.make_async_remote_copy(src_ref=s, dst_ref=d, send_sem=send_sems.at[i], recv_sem=recv_sems.at[i],
                                               device_id=peer, device_id_type=MESH)
                  for i, (s, d, peer) in enumerate(sends)]
        for cp in remote:
            cp.start()
        for i, d in enumerate(lands):
            pltpu.make_async_remote_copy(src_ref=d, dst_ref=d, send_sem=send_sems.at[i], recv_sem=recv_sems.at[i],
                                         device_id=(x, y, c), device_id_type=MESH).wait_recv()
        for cp in remote:
            cp.wait_send()
        for cp in local:
            cp.wait()

    any_spec = pl.BlockSpec(memory_space=pl.ANY)
    return pl.pallas_call(
        body, name=name,
        in_specs=[any_spec] * n_in, out_specs=[any_spec] * n_out,
        out_shape=[jax.ShapeDtypeStruct(s, d) for s, d in out_shapes],
        input_output_aliases=aliases,
        scratch_shapes=[pltpu.SemaphoreType.DMA((n_remote,)), pltpu.SemaphoreType.DMA((n_remote,)),
                        pltpu.SemaphoreType.DMA((max(n_local, 1),))],
        compiler_params=pltpu.CompilerParams(has_side_effects=True),
    )(*ins)


def allgather_dev(name, v):
    M, Nc = v.shape

    def plan(in_refs, out_refs):
        (v_ref,), (o_ref,) = in_refs, out_refs
        x, y, c = lax.axis_index("x"), lax.axis_index("y"), lax.axis_index("c")

        def rows(px, py, pc):
            return o_ref.at[pl.ds((4 * px + 2 * py + pc) * M, M), :]

        peers = [(x ^ fx, y ^ fy, c ^ fc) for fx in (0, 1) for fy in (0, 1) for fc in (0, 1) if fx or fy or fc]
        return ([(v_ref, rows(x, y, c))],
                [(v_ref, rows(x, y, c), p) for p in peers],
                [rows(*p) for p in peers])

    return _comm(name, [v], [((N_DEV * M, Nc), v.dtype)], {}, 1, N_DEV - 1, plan)[0]


def allgather_chips_1(name, shards):
    def plan(in_refs, out_refs):
        x, y, c, chip, others = _coords()
        sends, lands = [], []
        for s_ref, g_ref in zip(in_refs, out_refs):
            hr = s_ref.shape[0] // 2
            mine = pl.ds(c * hr, hr)
            for qx, qy in others:
                sends.append((s_ref.at[mine], g_ref.at[chip, mine], (qx, qy, c)))
                lands.append(g_ref.at[2 * qx + qy, mine])
        return [], sends, lands

    n = len(shards)
    return _comm(name, shards, [((N_CHIP,) + s.shape, s.dtype) for s in shards], {}, 0, 3 * n, plan)


def allgather_chips_2(name, gathered, shards):
    n = len(gathered)

    def plan(in_refs, out_refs):
        x, y, c, chip, others = _coords()
        sends, lands = [], []
        for s_ref, g_ref in zip(in_refs[n:], out_refs):
            hr = g_ref.shape[1] // 2
            for qx, qy in others:
                q = 2 * qx + qy
                sends.append((g_ref.at[q, pl.ds(c * hr, hr)], g_ref.at[q, pl.ds(c * hr, hr)], (x, y, 1 - c)))
                lands.append(g_ref.at[q, pl.ds((1 - c) * hr, hr)])
            sends.append((s_ref, g_ref.at[chip], (x, y, 1 - c)))
            lands.append(g_ref.at[chip])
        return [], sends, lands

    return _comm(name, list(gathered) + list(shards), [(g.shape, g.dtype) for g in gathered], {i: i for i in range(n)},
                 0, 4 * n, plan)


def reduce_1(name, grads):
    def plan(in_refs, out_refs):
        x, y, c, chip, others = _coords()
        sends, lands = [], []
        for g_ref, got_ref in zip(in_refs, out_refs):
            hr = g_ref.shape[1] // 2
            sends.append((g_ref.at[:, pl.ds((1 - c) * hr, hr), :], got_ref, (x, y, 1 - c)))
            lands.append(got_ref)
        return [], sends, lands

    n = len(grads)
    return _comm(name, grads, [((g.shape[0], g.shape[1] // 2, g.shape[2]), g.dtype) for g in grads], {}, 0, n, plan)


def reduce_2(name, parts):
    def plan(in_refs, out_refs):
        x, y, c, chip, others = _coords()
        sends, lands = [], []
        for t_ref, q_ref in zip(in_refs, out_refs):
            for qx, qy in others:
                sends.append((t_ref.at[2 * qx + qy], q_ref.at[chip], (qx, qy, c)))
                lands.append(q_ref.at[2 * qx + qy])
        return [], sends, lands

    n = len(parts)
    return _comm(name, parts, [(p.shape, p.dtype) for p in parts], {}, 0, 3 * n, plan)


def reduce_3(name, fulls):
    def plan(in_refs, out_refs):
        x, y, c, chip, others = _coords()
        sends, lands = [], []
        for o_ref in out_refs:
            sends.append((o_ref.at[c], o_ref.at[c], (x, y, 1 - c)))
            lands.append(o_ref.at[1 - c])
        return [], sends, lands

    n = len(fulls)
    return _comm(name, fulls, [(f.shape, f.dtype) for f in fulls], {i: i for i in range(n)}, 0, n, plan)


_WEIGHTS = ['c_ctx', 'w_mod', 'b_mod', 'g_mix', 'g_ffn', 'w_in', 'ssm_lam_re', 'ssm_lam_im', 'ssm_log_dt', 'ssm_b_re',
            'ssm_b_im', 'ssm_c_re', 'ssm_c_im', 'ssm_d', 'ssm_w_glu', 'na_rpb', 'w_out', 'cv_w_pw1', 'cv_dw_w', 'cv_dw_b',
            'cv_ln_g', 'cv_ln_b', 'cv_w_pw2', 'ffn_w_up', 'ffn_conv_w', 'ffn_conv_b', 'ffn_w_down', 'g_out']
_INPUTS = ['x', 'c', 'ctx'] + _WEIGHTS + ['loss_target'] + ['m_' + w for w in _WEIGHTS] + ['v_' + w for w in _WEIGHTS]
_GATHERED_SMALL = ['ffn_conv_w', 'cv_dw_w', 'cv_dw_b', 'cv_ln_g', 'cv_ln_b']


def _silu(v):
    return v * jax.nn.sigmoid(v)


def _pack(arrs, cols):
    flat = jnp.concatenate([a.reshape(-1).astype(F32) for a in arrs])
    n = flat.shape[0]
    unit = SUB * cols
    flat = jnp.pad(flat, (0, (-n) % unit))
    return flat.reshape(-1, cols)


def _unpack(buf, shapes):
    flat = buf.reshape(-1)
    out, o = [], 0
    for s in shapes:
        n = int(np.prod(s))
        out.append(flat[o:o + n].reshape(s))
        o += n
    return out


def _ffn_fwd(tag, xin, sh, sc, gt, g, wup, cw3, cb3, wdn):
    hf = norm_mod_fwd(tag + "_norm", xin, g * (1.0 + sc), sh)
    up3 = mm_nn_pieces(tag + "_up", hf, wup, 0, N_CHIP, BF16, halves=2)
    act = ffn_mid_fwd(tag + "_mid", up3, cw3, cb3)
    yf = mm_nn(tag + "_down", act, wdn, BF16)
    return gate_res_fwd(tag + "_res", xin, yf, gt), (xin, hf, up3, act, yf)


def _ffn_bwd(tag, dxo, saved, sc, gt, g, wup, cw3, cb3, wdn):
    xin, hf, up3, act, yf = saved
    dyf, dgt = gate_res_bwd(tag + "_res_b", dxo, yf, gt)
    dact = mm_nt(tag + "_down_bx", dyf, wdn, BF16)
    dwdn = mm_tn(tag + "_down_bw", act, dyf, BF16)
    dup3, dcw3, dcb3 = ffn_mid_bwd(tag + "_mid_b", up3, dact, cw3, cb3)
    dhf = mm_nt_pieces(tag + "_up_bx", dup3, wup, BF16, halves=2)
    dwup = mm_tn_pieces(tag + "_up_bw", hf, dup3, N_CHIP, BF16, halves=2)
    dxi, cs1, cs2 = norm_mod_bwd(tag + "_norm_b", xin, dhf, g * (1.0 + sc), dxo)
    return dxi, dict(dsh=cs1[0], dsc=cs2[0] * g, dgt=dgt[0], dg=cs2[0] * (1.0 + sc), dwup=dwup, dwdn=dwdn,
                     dcw=dcw3.transpose(1, 0, 2).reshape(3, -1), dcb=dcb3.reshape(-1))


def kernel(x, c, ctx, c_ctx, w_mod, b_mod, g_mix, g_ffn, w_in, ssm_lam_re, ssm_lam_im, ssm_log_dt, ssm_b_re, ssm_b_im, ssm_c_re, ssm_c_im, ssm_d, ssm_w_glu, na_rpb, w_out, cv_w_pw1, cv_dw_w, cv_dw_b, cv_ln_g, cv_ln_b, cv_w_pw2, ffn_w_up, ffn_conv_w, ffn_conv_b, ffn_w_down, g_out, loss_target, m_c_ctx, m_w_mod, m_b_mod, m_g_mix, m_g_ffn, m_w_in, m_ssm_lam_re, m_ssm_lam_im, m_ssm_log_dt, m_ssm_b_re, m_ssm_b_im, m_ssm_c_re, m_ssm_c_im, m_ssm_d, m_ssm_w_glu, m_na_rpb, m_w_out, m_cv_w_pw1, m_cv_dw_w, m_cv_dw_b, m_cv_ln_g, m_cv_ln_b, m_cv_w_pw2, m_ffn_w_up, m_ffn_conv_w, m_ffn_conv_b, m_ffn_w_down, m_g_out, v_c_ctx, v_w_mod, v_b_mod, v_g_mix, v_g_ffn, v_w_in, v_ssm_lam_re, v_ssm_lam_im, v_ssm_log_dt, v_ssm_b_re, v_ssm_b_im, v_ssm_c_re, v_ssm_c_im, v_ssm_d, v_ssm_w_glu, v_na_rpb, v_w_out, v_cv_w_pw1, v_cv_dw_w, v_cv_dw_b, v_cv_ln_g, v_cv_ln_b, v_cv_w_pw2, v_ffn_w_up, v_ffn_conv_w, v_ffn_conv_b, v_ffn_w_down, v_g_out):
    p = dict(locals())
    xi, yi, ci = lax.axis_index("x"), lax.axis_index("y"), lax.axis_index("c")
    me, chip = 4 * xi + 2 * yi + ci, 2 * xi + yi
    xs, cx, tgt = x[0], ctx[0], loss_target[0]
    L, D = xs.shape
    Lc = cx.shape[0]
    T = L + Lc
    W = D // 2
    Cq = w_mod.shape[2]

    shards = [w_in[0], ssm_w_glu[0], w_out[0], cv_w_pw1[0], cv_w_pw2[0],
              ffn_w_up[0], ffn_w_up[1], ffn_w_down[0], ffn_w_down[1]]
    shards = [s.astype(BF16) for s in shards]
    gathered = allgather_chips_2("gather_w_2", allgather_chips_1("gather_w_1", shards), shards)
    Win, Wglu, Wout, Wpw1, Wpw2, Wup0, Wup1, Wdn0, Wdn1 = gathered
    Wglu, Wout, Wpw2 = (t.reshape(-1, t.shape[-1]) for t in (Wglu, Wout, Wpw2))
    Wdn = [Wdn0.reshape(-1, D), Wdn1.reshape(-1, D)]
    Wup = [Wup0, Wup1]
    Fd = Wdn[0].shape[0]

    small_shapes = [p[n].shape for n in _GATHERED_SMALL]
    sm = allgather_dev("gather_small", _pack([p[n] for n in _GATHERED_SMALL], 1024))
    sm = sm.reshape(N_DEV, -1)[0::2]
    per_chip = [_unpack(sm[q], small_shapes) for q in range(N_CHIP)]
    conv_w_f, dw_w_f, dw_b_f, ln_g_f, ln_b_f = (jnp.concatenate([pc[i] for pc in per_chip], axis=-1)
                                                for i in range(len(_GATHERED_SMALL)))
    cw3 = [conv_w_f[l].reshape(3, 2, Fd).transpose(1, 0, 2) for l in range(2)]
    cb3 = [ffn_conv_b[l].reshape(2, 1, Fd) for l in range(2)]
    dw_w_f, dw_b_f, ln_g_f, ln_b_f = dw_w_f[0], dw_b_f[0], ln_g_f[0], ln_b_f[0]

    c_all = allgather_dev("gather_c", jnp.zeros((SUB, D), F32).at[0].set(c[0])).reshape(N_DEV, SUB, D)[:, 0]
    S16 = jnp.concatenate([_silu(c_all), _silu(c_ctx)[None], jnp.zeros((2 * SUB - N_DEV - 1, D), F32)])
    modp = mm_nn_pieces("mod_fwd", S16, w_mod, 0, 2, F32, tm=2 * SUB)
    modg = allgather_dev("gather_mod", modp).reshape(N_DEV, 2 * SUB, 2, Cq)[0::2]
    mod_full = modg.transpose(2, 1, 0, 3).reshape(2, 2 * SUB, N_CHIP * Cq) + b_mod[:, None, :]
    mod_me = lax.dynamic_index_in_dim(mod_full, me, axis=1, keepdims=False)
    mods = [[mod_me[l, i * D:(i + 1) * D] for i in range(N_MOD)] for l in range(2)]
    shc, scc = mod_full[0, N_DEV, :D], mod_full[0, N_DEV, D:2 * D]

    sh_m, sc_m, gt_m, sh_f, sc_f, gt_f = mods[0]
    h0 = norm_mod_fwd("l0_norm", xs, g_mix[0] * (1.0 + sc_m), sh_m)
    hc0 = norm_mod_fwd("l0_norm_c", cx, g_mix[0] * (1.0 + scc), shc)
    u = mm_nn_pieces("l0_in_u", h0, Win, 0, 1, F32)
    qkv = mm_nn_pieces("l0_in_qkv", h0, Win, 1, 3, BF16)
    uc = mm_nn_pieces("l0_in_uc", hc0, Win, 0, 1, F32)
    kvc = mm_nn_pieces("l0_in_kvc", hc0, Win, 2, 2, BF16)

    lam_re, lam_im, log_dt = ssm_lam_re[0], ssm_lam_im[0], ssm_log_dt[0]
    b_re, b_im, c_re, c_im = ssm_b_re[0], ssm_b_im[0], ssm_c_re[0], ssm_c_im[0]
    (a_re, a_im, bb_re, bb_im), disc_vjp = jax.vjp(_s5_discretize, lam_re, lam_im, log_dt, b_re, b_im)
    G, P, Cg = bb_re.shape[1:]
    N = G * P
    a_re, a_im = a_re.reshape(2, 1, N), a_im.reshape(2, 1, N)
    a_f, a_b = jnp.stack([a_re, a_im], axis=1), jnp.stack([a_re, -a_im], axis=1)
    Bblk = jnp.stack([_blockdiag(bb_re), _blockdiag(bb_im)], axis=1)
    Cblk = jnp.stack([_blockdiag(c_re.swapaxes(-1, -2)), -_blockdiag(c_im.swapaxes(-1, -2))], axis=1)
    apow = _cpow((a_re, a_im), T // S5_SEG)

    useq = _interleave(jnp.stack([jnp.concatenate([uc, u]), jnp.concatenate([u, uc])]))
    hloc, fin = s5_scan("s5_scan", useq, Bblk.astype(BF16), a_f, rev=False)
    hst, yseq = s5_fix("s5_fix", hloc, _segment_carry(fin, apow, False), a_f, Cblk.swapaxes(-1, -2).astype(BF16), rev=False)
    ys = _deinterleave(yseq)
    y0, y1 = ys[0, Lc:], ys[1, :L]
    s5o = glu_fwd("s5_glu", u, y0, y1, ssm_d[0], Wglu)

    bias = na_bias(na_rpb[0])
    o_na, lse = natten_fwd("na_fwd", qkv, kvc, bias)
    mixcat = jnp.concatenate([s5o, o_na], axis=1)
    ymix = mm_nn("l0_out", mixcat, Wout, BF16)
    x1 = gate_res_fwd("l0_res", xs, ymix, gt_m)
    x2, ffn0 = _ffn_fwd("f0", x1, sh_f, sc_f, gt_f, g_ffn[0], Wup[0], cw3[0], cb3[0], Wdn[0])

    sh_v, sc_v, gt_v, sh_g, sc_g, gt_g = mods[1]
    hcv = norm_mod_fwd("l1_norm", x2, g_mix[1] * (1.0 + sc_v), sh_v)
    ag3 = mm_nn_pieces("l1_pw1", hcv, Wpw1, 0, N_CHIP, BF16, halves=2)
    z1, z3 = conf_mid_fwd("l1_mid", ag3, dw_w_f, dw_b_f, ln_g_f, ln_b_f)
    ycv = mm_nn("l1_pw2", z3, Wpw2, BF16)
    x3 = gate_res_fwd("l1_res", x2, ycv, gt_v)
    x4, ffn1 = _ffn_fwd("f1", x3, sh_g, sc_g, gt_g, g_ffn[1], Wup[1], cw3[1], cb3[1], Wdn[1])

    dx4, dg_out, loss_part = loss_head("loss", x4, g_out, tgt)
    loss = lax.psum(loss_part[0, 0], ("x", "y", "c"))

    dx3, gf1 = _ffn_bwd("f1", dx4, ffn1, sc_g, gt_g, g_ffn[1], Wup[1], cw3[1], cb3[1], Wdn[1])
    dycv, dgt_v = gate_res_bwd("l1_res_b", dx3, ycv, gt_v)
    dz3 = mm_nt("l1_pw2_bx", dycv, Wpw2, BF16)
    dWpw2 = mm_tn("l1_pw2_bw", z3, dycv, BF16)
    dz1, dln_g, dln_b = conf_ln_bwd("l1_ln_b", z1, dz3, ln_g_f, ln_b_f)
    dag3, ddw_w, ddw_b = conf_conv_bwd("l1_conv_b", ag3, dz1, dw_w_f)
    dhcv = mm_nt_pieces("l1_pw1_bx", dag3, Wpw1, BF16, halves=2)
    dWpw1 = mm_tn_pieces("l1_pw1_bw", hcv, dag3, N_CHIP, BF16, halves=2)
    dx2, cs1_v, cs2_v = norm_mod_bwd("l1_norm_b", x2, dhcv, g_mix[1] * (1.0 + sc_v), dx3)

    dx1, gf0 = _ffn_bwd("f0", dx2, ffn0, sc_f, gt_f, g_ffn[0], Wup[0], cw3[0], cb3[0], Wdn[0])
    dymix, dgt_m = gate_res_bwd("l0_res_b", dx1, ymix, gt_m)
    dmix = mm_nt("l0_out_bx", dymix, Wout, BF16)
    dWout = mm_tn("l0_out_bw", mixcat, dymix, BF16)
    dq, dk, dv, dkc, dvc, dbias = natten_bwd("na_bwd", qkv, kvc, bias, o_na, lse, dmix)
    dy, zg, dzz, dd_skip = glu_bwd("s5_glu_b", u, y0, y1, ssm_d[0], Wglu, dmix)
    dWglu = mm_tn("s5_glu_bw", zg, dzz, BF16)

    zc = jnp.zeros((Lc, W), F32)
    dyseq = _interleave(jnp.stack([jnp.concatenate([zc, dy]), jnp.concatenate([dy, zc])]))
    gloc, gfin = s5_scan("s5_scan_b", dyseq, Cblk.astype(BF16), a_b, rev=True)
    apow_b = (apow[0], -apow[1])
    gst, duseq = s5_fix("s5_fix_b", gloc, _segment_carry(gfin, apow_b, True), a_b, Bblk.swapaxes(-1, -2).astype(BF16), rev=True)
    dBm, dCm, da8 = s5_grads("s5_grads", gst, hst, useq, dyseq)
    dus = _deinterleave(duseq)
    du = fma3("s5_du", dy, dus[0, Lc:], dus[1, :L], ssm_d[0], BF16)
    duc = dus[0, :Lc] + dus[1, L:]

    d_in = jnp.concatenate([du, dq, dk.astype(BF16), dv.astype(BF16)], axis=1)
    d_in_c = jnp.concatenate([duc.astype(BF16), jnp.zeros((Lc, W), BF16), dkc.astype(BF16), dvc.astype(BF16)], axis=1)
    dh0 = mm_nt_pieces("l0_in_bx", d_in, Win, BF16)
    dhc0 = mm_nt_pieces("l0_in_bxc", d_in_c, Win, BF16)
    dWin = mm_tn_pieces("l0_in_bw", jnp.concatenate([hc0, h0]), jnp.concatenate([d_in_c, d_in]), N_CHIP, BF16)
    dx0, cs1_m, cs2_m = norm_mod_bwd("l0_norm_b", xs, dh0, g_mix[0] * (1.0 + sc_m), dx1)
    _, cs1_c, cs2_c = norm_mod_bwd("l0_norm_bc", cx, dhc0, g_mix[0] * (1.0 + scc), jnp.zeros_like(cx))

    dmod0 = jnp.concatenate([cs1_m[0], cs2_m[0] * g_mix[0], dgt_m[0], gf0["dsh"], gf0["dsc"], gf0["dgt"]])
    dmod1 = jnp.concatenate([cs1_v[0], cs2_v[0] * g_mix[1], dgt_v[0], gf1["dsh"], gf1["dsc"], gf1["dgt"]])
    dmodc = jnp.concatenate([cs1_c[0], cs2_c[0] * g_mix[0], jnp.zeros((4 * D,), F32)])
    dm_rows = jnp.concatenate([jnp.stack([dmod0, dmod1, dmodc]), jnp.zeros((SUB - 3, N_MOD * D), F32)])
    dm_all = allgather_dev("gather_dmod", dm_rows).reshape(N_DEV, SUB, N_MOD * D)
    dm_sum = sum_lead("sum_dmod", dm_all, F32)
    pad7 = jnp.zeros((2 * SUB - N_DEV - 1, N_MOD * D), F32)
    dMod = [jnp.concatenate([dm_all[:, 0], dm_sum[2:3], pad7]), jnp.concatenate([dm_all[:, 1], jnp.zeros_like(dm_sum[2:3]), pad7])]
    dMod_cols = [lax.dynamic_slice_in_dim(m, chip * Cq, Cq, axis=1) for m in dMod]
    g_w_mod = jnp.stack([mm_tn("mod_bw%d" % l, S16, dMod_cols[l], F32) for l in range(2)])
    g_b_mod = jnp.stack([dm_sum[0] + dm_sum[2], dm_sum[1]])
    ds_part = mm_nt("mod_bx", dMod_cols[0], w_mod[0], F32, tm=2 * SUB)
    ds_all = allgather_dev("gather_dsc", jnp.zeros((SUB, D), F32).at[0].set(ds_part[N_DEV]))
    ds_c = sum_lead("sum_dsc", ds_all.reshape(N_DEV, SUB, D)[0::2], F32)[0]
    sg_c = jax.nn.sigmoid(c_ctx)
    g_c_ctx = ds_c * sg_c * (1.0 + c_ctx * (1.0 - sg_c))

    H = W // NA_HEAD_DIM
    db5 = dbias.reshape(H, NA_WIN_R, GRID_W, NA_WIN_R, GRID_W).transpose(0, 1, 3, 2, 4).reshape(H * NA_WIN_R * NA_WIN_R, GRID_W * GRID_W)
    dcol = mm_nt("na_bias_fold", db5, na_bias_fold_matrix(), F32, exact=True)
    dcol = dcol.reshape(H, NA_WIN_R, NA_WIN_R, LANE)[..., :2 * NA_WIN_C - 1]
    ridx = np.arange(NA_WIN_R)[None, :] - np.arange(NA_WIN_R)[:, None] + (NA_WIN_R - 1)
    rsel = jnp.asarray(ridx[:, :, None] == np.arange(2 * NA_WIN_R - 1)[None, None, :], F32)
    g_rpb_loc = jnp.einsum("hoic,oir->hrc", dcol, rsel)

    dbb = [_blockdiag_extract(dBm[:, z], Cg, P) for z in range(2)]
    dcc = [_blockdiag_extract(dCm[:, z], Cg, P).swapaxes(-1, -2) for z in range(2)]
    da = jnp.sum(da8, axis=2).reshape(2, 2, G, P)
    small = {
        "g_mix": jnp.stack([cs2_m[0] * (1.0 + sc_m) + cs2_c[0] * (1.0 + scc), cs2_v[0] * (1.0 + sc_v)]),
        "g_ffn": jnp.stack([gf0["dg"], gf1["dg"]]),
        "a_re": da[:, 0], "a_im": da[:, 1], "bb_re": dbb[0], "bb_im": dbb[1], "c_re": dcc[0], "c_im": -dcc[1],
        "ssm_d": dd_skip, "na_rpb": g_rpb_loc, "cv_dw_w": ddw_w, "cv_dw_b": ddw_b, "cv_ln_g": dln_g, "cv_ln_b": dln_b,
        "ffn_conv_w": jnp.stack([gf0["dcw"], gf1["dcw"]]), "ffn_conv_b": jnp.stack([gf0["dcb"], gf1["dcb"]]),
        "g_out": dg_out,
    }
    skeys = list(small)
    sbuf = _pack([small[k] for k in skeys], 1024)
    sall = allgather_dev("gather_small_g", sbuf).reshape(N_DEV, sbuf.shape[0], 1024)
    ssum = dict(zip(skeys, _unpack(sum_lead("sum_small_g", sall, F32), [small[k].shape for k in skeys])))
    g_lam_re, g_lam_im, g_log_dt, g_b_re, g_b_im = disc_vjp((ssum["a_re"], ssum["a_im"], ssum["bb_re"], ssum["bb_im"]))

    def my_cols(t):
        n = t.shape[-1] // N_CHIP
        return lax.dynamic_slice_in_dim(t, chip * n, n, axis=t.ndim - 1)

    big = [dWin, dWglu.reshape(N_CHIP, -1, W), dWout.reshape(N_CHIP, -1, D), dWpw1, dWpw2.reshape(N_CHIP, -1, D),
           gf0["dwup"], gf1["dwup"], gf0["dwdn"].reshape(N_CHIP, -1, D), gf1["dwdn"].reshape(N_CHIP, -1, D)]
    got = reduce_1("reduce_g_1", big)
    c_idx = jnp.reshape(ci, (1,)).astype(jnp.int32)
    ids = jnp.stack([chip, ci]).astype(jnp.int32)
    parts = [add_half("reduce_add_%d" % i, g, r, c_idx) for i, (g, r) in enumerate(zip(big, got))]
    slots = reduce_2("reduce_g_2", parts)
    fulls = [sum_slots("reduce_sum_%d" % i, s, t, ids) for i, (s, t) in enumerate(zip(slots, parts))]
    full = [f.reshape(-1, f.shape[-1]) for f in reduce_3("reduce_g_3", fulls)]
    gWin, gWglu, gWout, gWpw1, gWpw2, gWup0, gWup1, gWdn0, gWdn1 = full

    grads = {
        "c_ctx": g_c_ctx, "w_mod": g_w_mod, "b_mod": g_b_mod, "g_mix": ssum["g_mix"], "g_ffn": ssum["g_ffn"],
        "w_in": gWin[None], "ssm_lam_re": g_lam_re[None], "ssm_lam_im": g_lam_im[None], "ssm_log_dt": g_log_dt[None],
        "ssm_b_re": g_b_re[None], "ssm_b_im": g_b_im[None], "ssm_c_re": ssum["c_re"][None], "ssm_c_im": ssum["c_im"][None],
        "ssm_d": ssum["ssm_d"], "ssm_w_glu": gWglu[None], "na_rpb": ssum["na_rpb"][None], "w_out": gWout[None],
        "cv_w_pw1": gWpw1[None], "cv_dw_w": my_cols(ssum["cv_dw_w"])[None], "cv_dw_b": my_cols(ssum["cv_dw_b"]),
        "cv_ln_g": my_cols(ssum["cv_ln_g"]), "cv_ln_b": my_cols(ssum["cv_ln_b"]), "cv_w_pw2": gWpw2[None],
        "ffn_w_up": jnp.stack([gWup0, gWup1]), "ffn_conv_w": my_cols(ssum["ffn_conv_w"]), "ffn_conv_b": ssum["ffn_conv_b"],
        "ffn_w_down": jnp.stack([gWdn0, gWdn1]), "g_out": ssum["g_out"][0],
    }
    grads = {k: grads[k].reshape(p[k].shape) for k in _WEIGHTS}

    large = [k for k in _WEIGHTS if p[k].size >= (1 << 18)]
    tiny = [k for k in _WEIGHTS if k not in large]
    delta, new_m, new_v = {}, {}, {}
    for k in large:
        delta[k], new_m[k], new_v[k] = adamw("adamw_" + k, p[k], grads[k], p["m_" + k], p["v_" + k])
    packs = [_pack([src[pre + k] for k in tiny], 1024) for src, pre in ((p, ""), (grads, ""), (p, "m_"), (p, "v_"))]
    outs = adamw("adamw_small", *packs)
    shapes = [p[k].shape for k in tiny]
    for dst, buf in zip((delta, new_m, new_v), outs):
        dst.update(zip(tiny, _unpack(buf, shapes)))

    return (loss, dx0[None], *[grads[k] for k in _WEIGHTS], *[delta[k] for k in _WEIGHTS],
            *[new_m[k] for k in _WEIGHTS], *[new_v[k] for k in _WEIGHTS])
```

```python
import functools
import math

import numpy as np
import jax
import jax.numpy as jnp
from jax import lax
from jax.experimental import pallas as pl
from jax.experimental.pallas import tpu as pltpu

F32, BF16 = jnp.float32, jnp.bfloat16
MESH = pl.DeviceIdType.MESH
V7X_VMEM_LIMIT = 56 << 20
LANE, SUB = 128, 8
N_CHIP, N_DEV = 4, 8

GRID_W = 64
N_MOD = 6
SSM_GROUP, SSM_STATE = 16, 64
NA_HEAD_DIM, NA_WIN_R, NA_WIN_C = 128, 8, 16
EPS = 1e-6
NEG = -1e30
ADAM_LR, ADAM_B1, ADAM_B2, ADAM_EPS, ADAM_WD, ADAM_STEP = 0.001, 0.9, 0.999, 1e-08, 0.01, 10
S5_STRIP = 512
S5_SEG = 8

NN = (((1,), (0,)), ((), ()))
NT = (((1,), (1,)), ((), ()))
TN = (((0,), (0,)), ((), ()))


def _params(*sem, side_effects=False):
    return pltpu.CompilerParams(dimension_semantics=sem if sem else None, vmem_limit_bytes=V7X_VMEM_LIMIT,
                                has_side_effects=side_effects)


def _call(body, args, *, name, grid, in_specs, out_specs, out_shape, sem, scratch_shapes=(), comm=None):
    out_specs, out_shape, scratch_shapes = list(out_specs), list(out_shape), list(scratch_shapes)
    if comm is None:
        outs = pl.pallas_call(body, name=name, grid=grid, in_specs=list(in_specs), out_specs=out_specs, out_shape=out_shape,
                              scratch_shapes=scratch_shapes, compiler_params=_params(*sem))(*args)
        return list(outs), []
    c_args, c_shapes, c_alias, n_remote, plan = comm
    n_in, n_out, n_ci, n_co, n_sc = len(args), len(out_shape), len(c_args), len(c_shapes), len(scratch_shapes)

    def wrapped(*refs):
        ins, cins = refs[:n_in], refs[n_in:n_in + n_ci]
        o0 = n_in + n_ci
        outs, couts = refs[o0:o0 + n_out], refs[o0 + n_out:o0 + n_out + n_co]
        s0 = o0 + n_out + n_co
        scr, (send_sems, recv_sems) = refs[s0:s0 + n_sc], refs[s0 + n_sc:]
        pids = [pl.program_id(a) for a in range(len(grid))]
        first = functools.reduce(jnp.logical_and, [q == 0 for q in pids])
        last = functools.reduce(jnp.logical_and, [q == g - 1 for q, g in zip(pids, grid)])
        me = (lax.axis_index("x"), lax.axis_index("y"), lax.axis_index("c"))

        def copies():
            _, sends, lands = plan(cins, couts)
            assert len(sends) == n_remote and len(lands) == n_remote
            out = [pltpu.make_async_remote_copy(src_ref=s, dst_ref=d, send_sem=send_sems.at[i], recv_sem=recv_sems.at[i],
                                                device_id=peer, device_id_type=MESH) for i, (s, d, peer) in enumerate(sends)]
            arrivals = [pltpu.make_async_remote_copy(src_ref=d, dst_ref=d, send_sem=send_sems.at[i], recv_sem=recv_sems.at[i],
                                                     device_id=me, device_id_type=MESH) for i, d in enumerate(lands)]
            return out, arrivals

        @pl.when(first)
        def _():
            for cp in copies()[0]:
                cp.start()

        body(*ins, *outs, *scr)

        @pl.when(last)
        def _():
            out, arrivals = copies()
            for cp in arrivals:
                cp.wait_recv()
            for cp in out:
                cp.wait_send()

    any_spec = pl.BlockSpec(memory_space=pl.ANY)
    res = pl.pallas_call(
        wrapped, name=name, grid=grid,
        in_specs=[*in_specs, *[any_spec] * n_ci], out_specs=[*out_specs, *[any_spec] * n_co],
        out_shape=[*out_shape, *[jax.ShapeDtypeStruct(s, d) for s, d in c_shapes]],
        input_output_aliases={n_in + i: n_out + j for i, j in c_alias.items()},
        scratch_shapes=[*scratch_shapes, pltpu.SemaphoreType.DMA((n_remote,)), pltpu.SemaphoreType.DMA((n_remote,))],
        compiler_params=_params(*["arbitrary"] * len(grid), side_effects=True),
    )(*args, *c_args)
    return list(res[:n_out]), list(res[n_out:])


def _pick(n, pref, mult=LANE):
    if n <= pref:
        return n
    best = None
    for t in range(mult, pref + 1, mult):
        if n % t == 0:
            best = t
    assert best is not None, (n, pref, mult)
    return best


def _sigmoid(x):
    return 1.0 / (1.0 + jnp.exp(-x))


def _mm(name, a, b, *, dims, grid, a_spec, b_spec, o_spec, out_shape, out_dtype, acc_shape, exact=False, comm=None):
    nk = grid[2]

    def body(a_ref, b_ref, o_ref, *scratch):
        if exact:
            part = lax.dot_general(a_ref[...], b_ref[...], dims, preferred_element_type=F32,
                                   precision=lax.Precision.HIGHEST)
        else:
            part = lax.dot_general(a_ref[...].astype(BF16), b_ref[...].astype(BF16), dims,
                                   preferred_element_type=F32)
        if nk == 1:
            o_ref[...] = part.astype(o_ref.dtype)
        else:
            acc = scratch[0]
            kk = pl.program_id(2)

            @pl.when(kk == 0)
            def _():
                acc[...] = part

            @pl.when(kk > 0)
            def _():
                acc[...] += part

            @pl.when(kk == nk - 1)
            def _():
                o_ref[...] = acc[...].astype(o_ref.dtype)

    outs, couts = _call(body, [a, b], name=name, grid=grid, in_specs=[a_spec, b_spec], out_specs=[o_spec],
                        out_shape=[jax.ShapeDtypeStruct(out_shape, out_dtype)],
                        scratch_shapes=[] if nk == 1 else [pltpu.VMEM(acc_shape, F32)],
                        sem=("parallel", "parallel", "arbitrary"), comm=comm)
    return outs[0] if comm is None else (outs[0], couts)


MM_VMEM_BUDGET = 36 << 20


def _fit(M, N, cost, m_mult=SUB):
    best = None
    for tm in sorted({_pick(M, p, m_mult) for p in (2048, 1024, 512, 256, 128)}):
        for tn in sorted({_pick(N, p) for p in (1408, 1024, 512, 256, 128)}):
            if best is None or (cost(tm, tn) <= MM_VMEM_BUDGET and tm * tn > best[0] * best[1]):
                best = (tm, tn)
    return best


def _sz(t):
    return jnp.dtype(t).itemsize


def mm_nn_pieces(name, a, w, p0, n_p, out_dtype, halves=1, comm=None):
    M, K = a.shape
    Nq = w.shape[2]
    tm, tn = _fit(M, Nq, lambda m, n: 2 * (m * K * _sz(a.dtype) + K * n * _sz(w.dtype) + m * n * _sz(out_dtype)))
    tpp = Nq // tn
    pph = n_p // halves
    if halves == 1:
        o_spec = pl.BlockSpec((tm, tn), lambda i, j, k: (i, j))
        oshape = (M, n_p * Nq)
    else:
        o_spec = pl.BlockSpec((None, tm, tn), lambda i, j, k: ((j // tpp) // pph, i, ((j // tpp) % pph) * tpp + j % tpp))
        oshape = (halves, M, pph * Nq)
    return _mm(name, a, w, dims=NN, grid=(M // tm, n_p * tpp, 1),
               a_spec=pl.BlockSpec((tm, K), lambda i, j, k: (i, 0)),
               b_spec=pl.BlockSpec((None, K, tn), lambda i, j, k: (p0 + j // tpp, 0, j % tpp)),
               o_spec=o_spec, out_shape=oshape, out_dtype=out_dtype, acc_shape=(tm, tn), comm=comm)


def mm_nn(name, a, w, out_dtype, exact=False):
    M, K = a.shape
    N = w.shape[1]
    tm, tn = _fit(M, N, lambda m, n: 2 * (m * K * _sz(a.dtype) + K * n * _sz(w.dtype) + m * n * _sz(out_dtype)))
    return _mm(name, a, w, dims=NN, grid=(M // tm, N // tn, 1),
               a_spec=pl.BlockSpec((tm, K), lambda i, j, k: (i, 0)),
               b_spec=pl.BlockSpec((K, tn), lambda i, j, k: (0, j)),
               o_spec=pl.BlockSpec((tm, tn), lambda i, j, k: (i, j)),
               out_shape=(M, N), out_dtype=out_dtype, acc_shape=(tm, tn), exact=exact)


def mm_nt(name, dy, w, out_dtype, exact=False):
    M, N = dy.shape
    K = w.shape[0]
    tm, tn = _fit(M, K, lambda m, n: 2 * (m * N * _sz(dy.dtype) + n * N * _sz(w.dtype) + m * n * _sz(out_dtype)))
    return _mm(name, dy, w, dims=NT, grid=(M // tm, K // tn, 1),
               a_spec=pl.BlockSpec((tm, N), lambda i, j, k: (i, 0)),
               b_spec=pl.BlockSpec((tn, N), lambda i, j, k: (j, 0)),
               o_spec=pl.BlockSpec((tm, tn), lambda i, j, k: (i, j)),
               out_shape=(M, K), out_dtype=out_dtype, acc_shape=(tm, tn), exact=exact)


def mm_nt_pieces(name, dy, w, out_dtype, halves=1):
    P, K, Nq = w.shape
    M = dy.shape[-2]
    tm, tn = _fit(M, K, lambda m, n: 2 * (m * Nq * _sz(dy.dtype) + n * Nq * _sz(w.dtype) + m * n * _sz(out_dtype)) + 4 * m * n)
    pph = P // halves
    if halves == 1:
        a_spec = pl.BlockSpec((tm, Nq), lambda i, j, k: (i, k))
    else:
        a_spec = pl.BlockSpec((None, tm, Nq), lambda i, j, k: (k // pph, i, k % pph))
    return _mm(name, dy, w, dims=NT, grid=(M // tm, K // tn, P),
               a_spec=a_spec,
               b_spec=pl.BlockSpec((None, tn, Nq), lambda i, j, k: (k, j, 0)),
               o_spec=pl.BlockSpec((tm, tn), lambda i, j, k: (i, j)),
               out_shape=(M, K), out_dtype=out_dtype, acc_shape=(tm, tn))


def mm_tn(name, a, dy, out_dtype):
    M, K = a.shape
    N = dy.shape[1]
    tm, tn = _fit(K, N, lambda m, n: 2 * (M * m * _sz(a.dtype) + M * n * _sz(dy.dtype) + m * n * _sz(out_dtype)), LANE)
    return _mm(name, a, dy, dims=TN, grid=(K // tm, N // tn, 1),
               a_spec=pl.BlockSpec((M, tm), lambda i, j, k: (0, i)),
               b_spec=pl.BlockSpec((M, tn), lambda i, j, k: (0, j)),
               o_spec=pl.BlockSpec((tm, tn), lambda i, j, k: (i, j)),
               out_shape=(K, N), out_dtype=out_dtype, acc_shape=(tm, tn))


def mm_tn_pieces(name, a, dy, n_p, out_dtype, halves=1):
    M, K = a.shape
    Nq = (dy.shape[-1] * halves) // n_p
    tm, tn = _fit(K, Nq, lambda m, n: 2 * (M * m * _sz(a.dtype) + M * n * _sz(dy.dtype) + m * n * _sz(out_dtype)), LANE)
    tpp = Nq // tn
    pph = n_p // halves
    if halves == 1:
        b_spec = pl.BlockSpec((M, tn), lambda i, j, k: (0, j))
    else:
        b_spec = pl.BlockSpec((None, M, tn), lambda i, j, k: ((j // tpp) // pph, 0, ((j // tpp) % pph) * tpp + j % tpp))
    return _mm(name, a, dy, dims=TN, grid=(K // tm, n_p * tpp, 1),
               a_spec=pl.BlockSpec((M, tm), lambda i, j, k: (0, i)),
               b_spec=b_spec,
               o_spec=pl.BlockSpec((None, tm, tn), lambda i, j, k: (j // tpp, i, j % tpp)),
               out_shape=(n_p, K, Nq), out_dtype=out_dtype, acc_shape=(tm, tn))


def _row_call(name, body, ins, in_kinds, outs, rows, tr, scratch=()):
    def spec(kind, shape):
        if isinstance(kind, pl.BlockSpec):
            return kind
        if kind == "row":
            return pl.BlockSpec((tr,) + tuple(shape[1:]), lambda i: (i,) + (0,) * (len(shape) - 1))
        return pl.BlockSpec(tuple(shape), lambda i: (0,) * len(shape))

    return pl.pallas_call(
        body, name=name, grid=(rows // tr,),
        in_specs=[spec(k, a.shape) for k, a in zip(in_kinds, ins)],
        out_specs=[spec(k, s) for k, s, _ in outs],
        out_shape=[jax.ShapeDtypeStruct(s, d) for _, s, d in outs],
        scratch_shapes=list(scratch),
        compiler_params=_params("arbitrary"),
    )(*ins)


def _acc(ref, val):
    @pl.when(pl.program_id(0) == 0)
    def _():
        ref[...] = val

    @pl.when(pl.program_id(0) > 0)
    def _():
        ref[...] += val


def norm_mod_fwd(name, x, w, b, tr=256):
    rows, d = x.shape
    tr = _pick(rows, tr, SUB)

    def body(x_ref, w_ref, b_ref, h_ref):
        xv = x_ref[...]
        r = lax.rsqrt(jnp.mean(xv * xv, axis=-1, keepdims=True) + EPS)
        h_ref[...] = (xv * r * w_ref[...] + b_ref[...]).astype(BF16)

    return _row_call(name, body, [x, w.reshape(1, d), b.reshape(1, d)], ["row", "vec", "vec"],
                     [("row", (rows, d), BF16)], rows, tr)[0]


def norm_mod_bwd(name, x, dh, w, dx_in, tr=256):
    rows, d = x.shape
    tr = _pick(rows, tr, SUB)

    def body(x_ref, dh_ref, w_ref, dxi_ref, dx_ref, cs1_ref, cs2_ref):
        xv = x_ref[...]
        r = lax.rsqrt(jnp.mean(xv * xv, axis=-1, keepdims=True) + EPS)
        xn = xv * r
        dhv = dh_ref[...].astype(F32)
        dxn = dhv * w_ref[...]
        dx_ref[...] = dxi_ref[...] + r * (dxn - xn * jnp.mean(dxn * xn, axis=-1, keepdims=True))
        _acc(cs1_ref, jnp.sum(dhv, axis=0, keepdims=True))
        _acc(cs2_ref, jnp.sum(dhv * xn, axis=0, keepdims=True))

    return _row_call(name, body, [x, dh, w.reshape(1, d), dx_in], ["row", "row", "vec", "row"],
                     [("row", (rows, d), F32), ("acc", (1, d), F32), ("acc", (1, d), F32)], rows, tr)


def gate_res_fwd(name, x, y, gate, tr=256):
    rows, d = x.shape
    tr = _pick(rows, tr, SUB)

    def body(x_ref, y_ref, g_ref, o_ref):
        o_ref[...] = x_ref[...] + g_ref[...] * y_ref[...].astype(F32)

    return _row_call(name, body, [x, y, gate.reshape(1, d)], ["row", "row", "vec"],
                     [("row", (rows, d), F32)], rows, tr)[0]


def gate_res_bwd(name, dx, y, gate, tr=256):
    rows, d = dx.shape
    tr = _pick(rows, tr, SUB)

    def body(dx_ref, y_ref, g_ref, dy_ref, dg_ref):
        dxv = dx_ref[...]
        dy_ref[...] = (g_ref[...] * dxv).astype(BF16)
        _acc(dg_ref, jnp.sum(dxv * y_ref[...].astype(F32), axis=0, keepdims=True))

    return _row_call(name, body, [dx, y, gate.reshape(1, d)], ["row", "row", "vec"],
                     [("row", (rows, d), BF16), ("acc", (1, d), F32)], rows, tr)


def loss_head(name, x, g, target, tr=256):
    rows, d = x.shape
    tr = _pick(rows, tr, SUB)

    def body(x_ref, g_ref, t_ref, dx_ref, dg_ref, loss_ref):
        xv = x_ref[...]
        r = lax.rsqrt(jnp.mean(xv * xv, axis=-1, keepdims=True) + EPS)
        xn = xv * r
        err = xn * g_ref[...] - t_ref[...]
        dy = err * (1.0 / d)
        dxn = dy * g_ref[...]
        dx_ref[...] = r * (dxn - xn * jnp.mean(dxn * xn, axis=-1, keepdims=True))
        _acc(dg_ref, jnp.sum(dy * xn, axis=0, keepdims=True))
        part = 0.5 * jnp.sum(jnp.sum(err * err, axis=-1, keepdims=True) * (1.0 / d), axis=0, keepdims=True)
        _acc(loss_ref, jnp.broadcast_to(part, (1, LANE)))

    return _row_call(name, body, [x, g.reshape(1, d), target], ["row", "vec", "row"],
                     [("row", (rows, d), F32), ("acc", (1, d), F32), ("acc", (1, LANE), F32)], rows, tr)


def fma3(name, a, b, c, dvec, out_dtype, tr=256):
    rows, d = a.shape
    tr = _pick(rows, tr, SUB)

    def body(a_ref, b_ref, c_ref, d_ref, o_ref):
        o_ref[...] = (d_ref[...] * a_ref[...] + b_ref[...] + c_ref[...]).astype(o_ref.dtype)

    return _row_call(name, body, [a, b, c, dvec.reshape(1, d)], ["row", "row", "row", "vec"],
                     [("row", (rows, d), out_dtype)], rows, tr)[0]


def add2(name, a, b, out_dtype, tr=512):
    shape = a.shape
    a2, b2 = a.reshape(-1, shape[-1]), b.reshape(-1, shape[-1])
    rows = a2.shape[0]
    tr = _pick(rows, tr, 16)

    def body(a_ref, b_ref, o_ref):
        o_ref[...] = (a_ref[...].astype(F32) + b_ref[...].astype(F32)).astype(o_ref.dtype)

    out = _row_call(name, body, [a2, b2], ["row", "row"], [("row", a2.shape, out_dtype)], rows, tr)[0]
    return out.reshape(shape)


def sum_lead(name, a, out_dtype, tr=512):
    n, rows, cols = a.shape
    tr = _pick(rows, tr, 16)

    def body(a_ref, o_ref):
        acc = a_ref[0].astype(F32)
        for s in range(1, n):
            acc = acc + a_ref[s].astype(F32)
        o_ref[...] = acc.astype(o_ref.dtype)

    return pl.pallas_call(
        body, name=name, grid=(rows // tr,),
        in_specs=[pl.BlockSpec((n, tr, cols), lambda i: (0, i, 0))],
        out_specs=pl.BlockSpec((tr, cols), lambda i: (i, 0)),
        out_shape=jax.ShapeDtypeStruct((rows, cols), out_dtype),
        compiler_params=_params("parallel"),
    )(a)


def adamw(name, w, g, m, v, tr=512):
    shape = w.shape
    cols = shape[-1]
    w2, g2, m2, v2 = (t.reshape(-1, cols) for t in (w, g, m, v))
    rows = w2.shape[0]
    tr, tc = _pick(rows, 256, SUB), _pick(cols, 1536)
    c1 = 1.0 - ADAM_B1 ** ADAM_STEP
    c2 = 1.0 - ADAM_B2 ** ADAM_STEP

    def body(w_ref, g_ref, m_ref, v_ref, d_ref, mo_ref, vo_ref):
        gv = g_ref[...]
        mn = ADAM_B1 * m_ref[...] + (1.0 - ADAM_B1) * gv
        vn = ADAM_B2 * v_ref[...] + (1.0 - ADAM_B2) * (gv * gv)
        mo_ref[...] = mn
        vo_ref[...] = vn
        d_ref[...] = -ADAM_LR * ((mn / c1) / (jnp.sqrt(vn / c2) + ADAM_EPS) + ADAM_WD * w_ref[...])

    blk = pl.BlockSpec((tr, tc), lambda i, j: (i, j))
    outs = pl.pallas_call(
        body, name=name, grid=(rows // tr, cols // tc), in_specs=[blk] * 4, out_specs=[blk] * 3,
        out_shape=[jax.ShapeDtypeStruct(w2.shape, F32)] * 3, compiler_params=_params("parallel", "parallel"),
    )(w2, g2, m2, v2)
    return tuple(o.reshape(shape) for o in outs)


def add_half(name, grad, got, c_idx, tr=256):
    Pn, R, C = grad.shape
    hr = R // 2
    tr = _pick(hr, tr, HALO)
    nb = hr // tr

    def body(c_ref, a_ref, b_ref, o_ref):
        o_ref[...] = (a_ref[...].astype(F32) + b_ref[...].astype(F32)).astype(o_ref.dtype)

    return pl.pallas_call(
        body, name=name,
        grid_spec=pltpu.PrefetchScalarGridSpec(
            num_scalar_prefetch=1, grid=(Pn, nb),
            in_specs=[pl.BlockSpec((None, tr, C), lambda q, i, c: (q, c[0] * nb + i, 0)),
                      pl.BlockSpec((None, tr, C), lambda q, i, c: (q, i, 0))],
            out_specs=pl.BlockSpec((None, tr, C), lambda q, i, c: (q, i, 0))),
        out_shape=jax.ShapeDtypeStruct((Pn, hr, C), BF16),
        compiler_params=_params("parallel", "parallel"),
    )(c_idx, grad, got)


def sum_slots(name, slots, mine, ids, tr=256):
    Pn, hr, C = slots.shape
    tr = _pick(hr, tr, HALO)

    def body(ids_ref, m_ref, s1_ref, s2_ref, s3_ref, o_ref):
        o_ref[...] = (m_ref[...].astype(F32) + s1_ref[...].astype(F32)) + (s2_ref[...].astype(F32) + s3_ref[...].astype(F32))

    def other(k):
        return pl.BlockSpec((None, tr, C), lambda i, ids: ((ids[0] + k) % Pn, i, 0))

    return pl.pallas_call(
        body, name=name,
        grid_spec=pltpu.PrefetchScalarGridSpec(
            num_scalar_prefetch=1, grid=(hr // tr,),
            in_specs=[pl.BlockSpec((None, tr, C), lambda i, ids: (ids[0], i, 0)), other(1), other(2), other(3)],
            out_specs=pl.BlockSpec((None, tr, C), lambda i, ids: (ids[1], i, 0))),
        out_shape=jax.ShapeDtypeStruct((2, hr, C), F32),
        compiler_params=_params("parallel"),
    )(ids, mine, slots, slots, slots)


HALO = 16


def _halo_specs(lead, R, tn, n_rows, col_of):
    nb, nblk = R // HALO, n_rows // HALO

    def mk(rows, row_of):
        return pl.BlockSpec((lead, rows, tn), lambda *g: (0, row_of(g[-1]), col_of(g)))

    return (mk(HALO, lambda i: jnp.maximum(i * nb - 1, 0)), mk(R, lambda i: i),
            mk(HALO, lambda i: jnp.minimum((i + 1) * nb, nblk - 1)))


def _fill_halo(dst, i, last, R, prev, cur, nxt):
    nd = len(dst.shape)
    lead = (slice(None),) * (nd - 2)
    dst[lead + (slice(0, HALO), slice(None))] = jnp.where(i == 0, 0.0, prev)
    dst[lead + (slice(HALO, HALO + R), slice(None))] = cur
    dst[lead + (slice(HALO + R, HALO + R + HALO), slice(None))] = jnp.where(i == last, 0.0, nxt)


def _shift_mats(n):
    i = np.arange(n)
    return jnp.asarray(np.stack([i[:, None] - 1 == i[None, :], i[:, None] + 1 == i[None, :]]), BF16)


def _shifted(s_ref, xb):
    return (jnp.dot(s_ref[0], xb, preferred_element_type=F32), jnp.dot(s_ref[1], xb, preferred_element_type=F32))


def ffn_mid_fwd(name, up3, cw, cb, R=256, tn=512, comm=None):
    _, L, Fd = up3.shape
    R, tn = _pick(L, R, HALO), _pick(Fd, tn)
    nrow = L // R

    def body(p_ref, c_ref, n_ref, w_ref, b_ref, s_ref, act_ref):
        i = pl.program_id(1)
        row = lax.broadcasted_iota(jnp.int32, (R, tn), 0)
        cv = []
        for z in range(2):
            xb = c_ref[z]
            before = jnp.where(i == 0, 0.0, p_ref[z].astype(F32)[HALO - 1:HALO])
            after = jnp.where(i == nrow - 1, 0.0, n_ref[z].astype(F32)[0:1])
            dn, up = _shifted(s_ref, xb)
            dn = jnp.where(row == 0, before, dn)
            up = jnp.where(row == R - 1, after, up)
            cv.append(b_ref[z] + w_ref[z, 0:1, :] * dn + w_ref[z, 1:2, :] * xb.astype(F32) + w_ref[z, 2:3, :] * up)
        u, g = cv
        act_ref[...] = (u * g * _sigmoid(g)).astype(BF16)

    hs = _halo_specs(2, R, tn, L, lambda g: g[0])
    outs, couts = _call(
        body, [up3, up3, up3, cw, cb, _shift_mats(R)], name=name, grid=(Fd // tn, nrow),
        in_specs=[*hs, pl.BlockSpec((2, 3, tn), lambda j, i: (0, 0, j)), pl.BlockSpec((2, 1, tn), lambda j, i: (0, 0, j)),
                  pl.BlockSpec((2, R, R), lambda j, i: (0, 0, 0))],
        out_specs=[pl.BlockSpec((R, tn), lambda j, i: (i, j))],
        out_shape=[jax.ShapeDtypeStruct((L, Fd), BF16)], sem=("parallel", "arbitrary"), comm=comm)
    return outs[0] if comm is None else (outs[0], couts)


def ffn_mid_bwd(name, up3, dact, cw, cb, R=256, tn=512, comm=None):
    _, L, Fd = up3.shape
    R, tn = _pick(L, R, HALO), _pick(Fd, tn)
    nrow = L // R
    E = R + 2 * HALO
    inner = slice(HALO, HALO + R)

    def body(pu, cu, nu, pd, cd, nd, w_ref, b_ref, s_ref, dup_ref, dcw_ref, dcb_ref, xs, ds):
        i = pl.program_id(1)
        _fill_halo(xs, i, nrow - 1, R, pu[...], cu[...], nu[...])
        _fill_halo(ds, i, nrow - 1, R, pd[0], cd[0], nd[0])
        da = ds[...].astype(F32)
        cv, taps = [], []
        for z in range(2):
            xb = xs[z]
            dn, up = _shifted(s_ref, xb)
            taps.append((dn, xb.astype(F32), up))
            cv.append(b_ref[z] + w_ref[z, 0:1, :] * dn + w_ref[z, 1:2, :] * taps[z][1] + w_ref[z, 2:3, :] * up)
        u, g = cv
        sg = _sigmoid(g)
        dcs = (da * g * sg, da * u * sg * (1.0 + g * (1.0 - sg)))

        @pl.when(i == 0)
        def _():
            dcw_ref[...] = jnp.zeros_like(dcw_ref)
            dcb_ref[...] = jnp.zeros_like(dcb_ref)

        for z in range(2):
            dc = dcs[z]
            dc_dn, dc_up = _shifted(s_ref, dc.astype(BF16))
            d = w_ref[z, 0:1, :] * dc_up + w_ref[z, 1:2, :] * dc + w_ref[z, 2:3, :] * dc_dn
            dup_ref[z] = d[inner].astype(BF16)
            dci = dc[inner]
            dcb_ref[z] += jnp.sum(dci, axis=0, keepdims=True)
            for k in range(3):
                dcw_ref[z, k:k + 1, :] += jnp.sum(dci * taps[z][k][inner], axis=0, keepdims=True)

    hu = _halo_specs(2, R, tn, L, lambda g: g[0])
    hd = _halo_specs(1, R, tn, L, lambda g: g[0])
    outs, couts = _call(
        body, [up3, up3, up3, dact[None], dact[None], dact[None], cw, cb, _shift_mats(E)], name=name, grid=(Fd // tn, nrow),
        in_specs=[*hu, *hd, pl.BlockSpec((2, 3, tn), lambda j, i: (0, 0, j)), pl.BlockSpec((2, 1, tn), lambda j, i: (0, 0, j)),
                  pl.BlockSpec((2, E, E), lambda j, i: (0, 0, 0))],
        out_specs=[pl.BlockSpec((2, R, tn), lambda j, i: (0, i, j)), pl.BlockSpec((2, 3, tn), lambda j, i: (0, 0, j)),
                   pl.BlockSpec((2, 1, tn), lambda j, i: (0, 0, j))],
        out_shape=[jax.ShapeDtypeStruct((2, L, Fd), BF16), jax.ShapeDtypeStruct((2, 3, Fd), F32),
                   jax.ShapeDtypeStruct((2, 1, Fd), F32)],
        scratch_shapes=[pltpu.VMEM((2, E, tn), BF16), pltpu.VMEM((E, tn), BF16)],
        sem=("parallel", "arbitrary"), comm=comm)
    return outs if comm is None else (outs, couts)


def _glu_z0(blk):
    return blk[0].astype(F32) * _sigmoid(blk[1].astype(F32))


def conf_mid_fwd(name, ag3, dw_w, dw_b, ln_g, ln_b, R=128, cb=256):
    _, L, C = ag3.shape
    K = dw_w.shape[0]
    pad = (K - 1) // 2
    assert pad <= HALO
    R, cb = _pick(L, R, HALO), _pick(C, cb)
    nrow = L // R

    def body(p_ref, c_ref, n_ref, w_ref, b_ref, g_ref, bb_ref, z1_ref, z3_ref, s_ref):
        i = pl.program_id(0)
        _fill_halo(s_ref, i, nrow - 1, R, _glu_z0(p_ref), _glu_z0(c_ref), _glu_z0(n_ref))
        for c0 in range(0, C, cb):
            acc = jnp.broadcast_to(b_ref[:, c0:c0 + cb], (R, cb))
            for k in range(K):
                acc = acc + w_ref[k:k + 1, c0:c0 + cb] * s_ref[pl.ds(HALO - pad + k, R), c0:c0 + cb]
            z1_ref[:, c0:c0 + cb] = acc
        z1 = z1_ref[...]
        zc = z1 - jnp.mean(z1, axis=-1, keepdims=True)
        zn = zc * lax.rsqrt(jnp.mean(zc * zc, axis=-1, keepdims=True) + EPS)
        z2 = zn * g_ref[...] + bb_ref[...]
        z3_ref[...] = (z2 * _sigmoid(z2)).astype(BF16)

    hs = _halo_specs(2, R, C, L, lambda g: 0)
    vec = pl.BlockSpec((1, C), lambda i: (0, 0))
    return pl.pallas_call(
        body, name=name, grid=(nrow,),
        in_specs=[*hs, pl.BlockSpec((K, C), lambda i: (0, 0)), vec, vec, vec],
        out_specs=[pl.BlockSpec((R, C), lambda i: (i, 0)), pl.BlockSpec((R, C), lambda i: (i, 0))],
        out_shape=[jax.ShapeDtypeStruct((L, C), F32), jax.ShapeDtypeStruct((L, C), BF16)],
        scratch_shapes=[pltpu.VMEM((R + 2 * HALO, C), F32)],
        compiler_params=_params("parallel"),
    )(ag3, ag3, ag3, dw_w, dw_b.reshape(1, C), ln_g.reshape(1, C), ln_b.reshape(1, C))


def conf_ln_bwd(name, z1, dz3, ln_g, ln_b, tr=256):
    rows, C = z1.shape
    tr = _pick(rows, tr, HALO)

    def body(z_ref, d_ref, g_ref, b_ref, dz_ref, dg_ref, db_ref):
        z1v = z_ref[...]
        zc = z1v - jnp.mean(z1v, axis=-1, keepdims=True)
        rs = lax.rsqrt(jnp.mean(zc * zc, axis=-1, keepdims=True) + EPS)
        zn = zc * rs
        z2 = zn * g_ref[...] + b_ref[...]
        sg = _sigmoid(z2)
        dz2 = d_ref[...].astype(F32) * sg * (1.0 + z2 * (1.0 - sg))
        _acc(dg_ref, jnp.sum(dz2 * zn, axis=0, keepdims=True))
        _acc(db_ref, jnp.sum(dz2, axis=0, keepdims=True))
        dzn = dz2 * g_ref[...]
        dz1 = rs * (dzn - jnp.mean(dzn, axis=-1, keepdims=True) - zn * jnp.mean(dzn * zn, axis=-1, keepdims=True))
        dz_ref[...] = dz1.astype(BF16)

    return _row_call(name, body, [z1, dz3, ln_g.reshape(1, C), ln_b.reshape(1, C)], ["row", "row", "vec", "vec"],
                     [("row", (rows, C), BF16), ("acc", (1, C), F32), ("acc", (1, C), F32)], rows, tr)


def conf_conv_bwd(name, ag3, dz1, dw_w, R=128, cb=256, comm=None):
    _, L, C = ag3.shape
    K = dw_w.shape[0]
    pad = (K - 1) // 2
    R, cb = _pick(L, R, HALO), _pick(C, cb)
    nrow = L // R

    def body(pa, ca, na, pd, cd, nd, w_ref, dag_ref, dw_ref, db_ref, s_ref, d_ref, z_ref):
        i = pl.program_id(0)
        _fill_halo(s_ref, i, nrow - 1, R, _glu_z0(pa), _glu_z0(ca), _glu_z0(na))
        _fill_halo(d_ref, i, nrow - 1, R, pd[0].astype(F32), cd[0].astype(F32), nd[0].astype(F32))

        @pl.when(i == 0)
        def _():
            dw_ref[...] = jnp.zeros_like(dw_ref)
            db_ref[...] = jnp.zeros_like(db_ref)

        for c0 in range(0, C, cb):
            cs = slice(c0, c0 + cb)
            dcur = d_ref[pl.ds(HALO, R), cs]
            acc = jnp.zeros((R, cb), F32)
            for k in range(K):
                acc = acc + w_ref[k:k + 1, cs] * d_ref[pl.ds(HALO + pad - k, R), cs]
                dw_ref[k:k + 1, cs] += jnp.sum(dcur * s_ref[pl.ds(HALO - pad + k, R), cs], axis=0, keepdims=True)
            z_ref[:, cs] = acc
            db_ref[:, cs] += jnp.sum(dcur, axis=0, keepdims=True)
        dz0 = z_ref[...]
        a = ca[0].astype(F32)
        sg = _sigmoid(ca[1].astype(F32))
        dag_ref[0] = (dz0 * sg).astype(BF16)
        dag_ref[1] = (dz0 * a * sg * (1.0 - sg)).astype(BF16)

    ha = _halo_specs(2, R, C, L, lambda g: 0)
    hd = _halo_specs(1, R, C, L, lambda g: 0)
    outs, couts = _call(
        body, [ag3, ag3, ag3, dz1[None], dz1[None], dz1[None], dw_w], name=name, grid=(nrow,),
        in_specs=[*ha, *hd, pl.BlockSpec((K, C), lambda i: (0, 0))],
        out_specs=[pl.BlockSpec((2, R, C), lambda i: (0, i, 0)), pl.BlockSpec((K, C), lambda i: (0, 0)),
                   pl.BlockSpec((1, C), lambda i: (0, 0))],
        out_shape=[jax.ShapeDtypeStruct((2, L, C), BF16), jax.ShapeDtypeStruct((K, C), F32),
                   jax.ShapeDtypeStruct((1, C), F32)],
        scratch_shapes=[pltpu.VMEM((R + 2 * HALO, C), F32), pltpu.VMEM((R + 2 * HALO, C), F32), pltpu.VMEM((R, C), F32)],
        sem=("arbitrary",), comm=comm)
    return outs if comm is None else (outs, couts)


_GELU_C = math.sqrt(2.0 / math.pi)


def _gelu(x):
    return 0.5 * x * (1.0 + jnp.tanh(_GELU_C * (x + 0.044715 * x * x * x)))


def _gelu_grad(x):
    t = jnp.tanh(_GELU_C * (x + 0.044715 * x * x * x))
    return 0.5 * (1.0 + t) + 0.5 * x * (1.0 - t * t) * _GELU_C * (1.0 + 3.0 * 0.044715 * x * x)


def glu_fwd(name, u, y0, y1, d, wg, tr=512):
    rows, W = u.shape
    tr = _pick(rows, tr, HALO)

    def body(u_ref, y0_ref, y1_ref, d_ref, w_ref, o_ref):
        z = _gelu(d_ref[...] * u_ref[...] + y0_ref[...] + y1_ref[...])
        zz = jnp.dot(z.astype(BF16), w_ref[...], preferred_element_type=F32)
        o_ref[...] = (z * _sigmoid(zz)).astype(BF16)

    return _row_call(name, body, [u, y0, y1, d.reshape(1, W), wg], ["row", "row", "row", "vec", "vec"],
                     [("row", (rows, W), BF16)], rows, tr)[0]


def glu_bwd(name, u, y0, y1, d, wg, dmix, tr=512):
    rows, W = u.shape
    tr = _pick(rows, tr, HALO)

    def body(u_ref, y0_ref, y1_ref, d_ref, w_ref, do_ref, dy_ref, z_ref, dzz_ref, dd_ref):
        uv = u_ref[...]
        y = d_ref[...] * uv + y0_ref[...] + y1_ref[...]
        z = _gelu(y)
        zz = jnp.dot(z.astype(BF16), w_ref[...], preferred_element_type=F32)
        sg = _sigmoid(zz)
        do = do_ref[...].astype(F32)
        dzz = (do * z * sg * (1.0 - sg)).astype(BF16)
        dz = do * sg + lax.dot_general(dzz, w_ref[...], NT, preferred_element_type=F32)
        dy = dz * _gelu_grad(y)
        dy_ref[...] = dy
        z_ref[...] = z.astype(BF16)
        dzz_ref[...] = dzz
        _acc(dd_ref, jnp.sum(dy * uv, axis=0, keepdims=True))

    do_spec = pl.BlockSpec((tr, W), lambda i: (i, 0))
    return _row_call(name, body, [u, y0, y1, d.reshape(1, W), wg, dmix], ["row", "row", "row", "vec", "vec", do_spec],
                     [("row", (rows, W), F32), ("row", (rows, W), BF16), ("row", (rows, W), BF16), ("acc", (1, W), F32)],
                     rows, tr)


NA_KEYS = NA_WIN_R * GRID_W


def na_bias(rpb):
    H, nr, nc = rpb.shape
    e, ok = _na_col_select()
    rp = jnp.pad(rpb.reshape(H * nr, nc), ((0, (-H * nr) % SUB), (0, LANE - nc)))
    cols = mm_nn("na_bias_mm", rp, jnp.asarray(e, F32), F32, exact=True)[:H * nr].reshape(H, nr, GRID_W * GRID_W)
    cols = cols + jnp.asarray(np.where(ok, 0.0, NEG), F32)
    b = jnp.stack([cols[:, NA_WIN_R - 1 - o:2 * NA_WIN_R - 1 - o] for o in range(NA_WIN_R)], axis=1)
    b = b.reshape(H, NA_WIN_R, NA_WIN_R, GRID_W, GRID_W)
    return b.transpose(0, 1, 3, 2, 4).reshape(H, NA_WIN_R, GRID_W, NA_KEYS)


def _na_col_select():
    q = np.arange(GRID_W)
    cs = np.clip(q - NA_WIN_C // 2, 0, GRID_W - NA_WIN_C)
    ok = ((q[None, :] >= cs[:, None]) & (q[None, :] < cs[:, None] + NA_WIN_C)).reshape(-1)
    cidx = np.clip(q[None, :] - q[:, None] + (NA_WIN_C - 1), 0, 2 * NA_WIN_C - 2).reshape(-1)
    return (cidx[None, :] == np.arange(LANE)[:, None]) & ok[None, :], ok


def na_bias_fold_matrix():
    return jnp.asarray(_na_col_select()[0], F32)


def _na_window(r, rows):
    kr0 = jnp.clip(r - NA_WIN_R // 2, 0, rows - NA_WIN_R)
    return pl.multiple_of(kr0 * GRID_W, GRID_W), r - kr0


def natten_fwd(name, qkv, kvc, bias, comm=None):
    L = qkv.shape[0]
    NA = qkv.shape[1] // 3
    H, rows, Lc = NA // NA_HEAD_DIM, L // GRID_W, kvc.shape[0]
    scale = NA_HEAD_DIM ** -0.5
    hd = NA_HEAD_DIM

    def body(q_ref, k_ref, v_ref, kc_ref, vc_ref, b_ref, o_ref, lse_ref):
        st, off = _na_window(pl.program_id(1), rows)
        q = q_ref[...]
        s_loc = lax.dot_general(q, k_ref[pl.ds(st, NA_KEYS), :], NT, preferred_element_type=F32) * scale + b_ref[off]
        s_ctx = lax.dot_general(q, kc_ref[...], NT, preferred_element_type=F32) * scale
        m = jnp.maximum(jnp.max(s_loc, axis=-1, keepdims=True), jnp.max(s_ctx, axis=-1, keepdims=True))
        p_loc, p_ctx = jnp.exp(s_loc - m), jnp.exp(s_ctx - m)
        l = jnp.sum(p_loc, axis=-1, keepdims=True) + jnp.sum(p_ctx, axis=-1, keepdims=True)
        o = (jnp.dot(p_loc.astype(BF16), v_ref[pl.ds(st, NA_KEYS), :], preferred_element_type=F32)
             + jnp.dot(p_ctx.astype(BF16), vc_ref[...], preferred_element_type=F32))
        o_ref[...] = (o / l).astype(BF16)
        lse_ref[...] = m + jnp.log(l)

    outs, couts = _call(
        body, [qkv, qkv, qkv, kvc, kvc, bias], name=name, grid=(H, rows),
        in_specs=[pl.BlockSpec((GRID_W, hd), lambda h, r: (r, h)),
                  pl.BlockSpec((L, hd), lambda h, r: (0, H + h)),
                  pl.BlockSpec((L, hd), lambda h, r: (0, 2 * H + h)),
                  pl.BlockSpec((Lc, hd), lambda h, r: (0, h)),
                  pl.BlockSpec((Lc, hd), lambda h, r: (0, H + h)),
                  pl.BlockSpec((None, NA_WIN_R, GRID_W, NA_KEYS), lambda h, r: (h, 0, 0, 0))],
        out_specs=[pl.BlockSpec((GRID_W, hd), lambda h, r: (r, h)),
                   pl.BlockSpec((None, GRID_W, 1), lambda h, r: (h, r, 0))],
        out_shape=[jax.ShapeDtypeStruct((L, NA), BF16), jax.ShapeDtypeStruct((H, L, 1), F32)],
        sem=("parallel", "arbitrary"), comm=comm)
    return outs if comm is None else (outs, couts)


def natten_bwd(name, qkv, kvc, bias, o, lse, dmix, comm=None):
    L = qkv.shape[0]
    NA = qkv.shape[1] // 3
    H, rows, Lc = NA // NA_HEAD_DIM, L // GRID_W, kvc.shape[0]
    scale = NA_HEAD_DIM ** -0.5
    hd = NA_HEAD_DIM

    def body(q_ref, k_ref, v_ref, kc_ref, vc_ref, b_ref, o_ref, lse_ref, do_ref,
             dq_ref, dk_ref, dv_ref, dkc_ref, dvc_ref, db_ref):
        r = pl.program_id(1)
        st, off = _na_window(r, rows)

        @pl.when(r == 0)
        def _():
            for ref in (dk_ref, dv_ref, dkc_ref, dvc_ref, db_ref):
                ref[...] = jnp.zeros_like(ref)

        q, kl, vl, kc, vc = q_ref[...], k_ref[pl.ds(st, NA_KEYS), :], v_ref[pl.ds(st, NA_KEYS), :], kc_ref[...], vc_ref[...]
        do = do_ref[...]
        lse_v = lse_ref[...]
        p_loc = jnp.exp(lax.dot_general(q, kl, NT, preferred_element_type=F32) * scale + b_ref[off] - lse_v)
        p_ctx = jnp.exp(lax.dot_general(q, kc, NT, preferred_element_type=F32) * scale - lse_v)
        delta = jnp.sum(do.astype(F32) * o_ref[...].astype(F32), axis=-1, keepdims=True)
        ds_loc = p_loc * (lax.dot_general(do, vl, NT, preferred_element_type=F32) - delta)
        ds_ctx = p_ctx * (lax.dot_general(do, vc, NT, preferred_element_type=F32) - delta)
        dsl, dsc = ds_loc.astype(BF16), ds_ctx.astype(BF16)
        dq = jnp.dot(dsl, kl, preferred_element_type=F32) + jnp.dot(dsc, kc, preferred_element_type=F32)
        dq_ref[...] = (dq * scale).astype(BF16)
        dk_ref[pl.ds(st, NA_KEYS), :] += lax.dot_general(dsl, q, TN, preferred_element_type=F32) * scale
        dv_ref[pl.ds(st, NA_KEYS), :] += lax.dot_general(p_loc.astype(BF16), do, TN, preferred_element_type=F32)
        dkc_ref[...] += lax.dot_general(dsc, q, TN, preferred_element_type=F32) * scale
        dvc_ref[...] += lax.dot_general(p_ctx.astype(BF16), do, TN, preferred_element_type=F32)
        db_ref[off] += ds_loc

    tok = pl.BlockSpec((GRID_W, hd), lambda h, r: (r, h))
    outs, couts = _call(
        body, [qkv, qkv, qkv, kvc, kvc, bias, o, lse, dmix], name=name, grid=(H, rows),
        in_specs=[tok,
                  pl.BlockSpec((L, hd), lambda h, r: (0, H + h)),
                  pl.BlockSpec((L, hd), lambda h, r: (0, 2 * H + h)),
                  pl.BlockSpec((Lc, hd), lambda h, r: (0, h)),
                  pl.BlockSpec((Lc, hd), lambda h, r: (0, H + h)),
                  pl.BlockSpec((None, NA_WIN_R, GRID_W, NA_KEYS), lambda h, r: (h, 0, 0, 0)),
                  tok,
                  pl.BlockSpec((None, GRID_W, 1), lambda h, r: (h, r, 0)),
                  pl.BlockSpec((GRID_W, hd), lambda h, r: (r, H + h))],
        out_specs=[tok,
                   pl.BlockSpec((L, hd), lambda h, r: (0, h)),
                   pl.BlockSpec((L, hd), lambda h, r: (0, h)),
                   pl.BlockSpec((Lc, hd), lambda h, r: (0, h)),
                   pl.BlockSpec((Lc, hd), lambda h, r: (0, h)),
                   pl.BlockSpec((None, NA_WIN_R, GRID_W, NA_KEYS), lambda h, r: (h, 0, 0, 0))],
        out_shape=[jax.ShapeDtypeStruct((L, NA), BF16), jax.ShapeDtypeStruct((L, NA), F32), jax.ShapeDtypeStruct((L, NA), F32),
                   jax.ShapeDtypeStruct((Lc, NA), F32), jax.ShapeDtypeStruct((Lc, NA), F32),
                   jax.ShapeDtypeStruct(bias.shape, F32)],
        sem=("parallel", "arbitrary"), comm=comm)
    return outs if comm is None else (outs, couts)


def _s5_dims(T, N):
    TC = T // S5_SEG
    assert T % (S5_SEG * SUB * 2) == 0 and N % S5_STRIP == 0
    return TC, TC // SUB, S5_SEG, N // S5_STRIP


def _s5_backward(d, rev):
    return (d == 1) != rev


def s5_scan(name, xin, mats, a, rev):
    _, T, W = xin.shape
    N = a.shape[-1]
    TC, NG, NCH, NS = _s5_dims(T, N)
    CW, SL = W // NS, S5_STRIP

    def ck(d, k):
        return jnp.where(_s5_backward(d, rev), NCH - 1 - k, k)

    def body(x_ref, m_ref, a_ref, h_ref, f_ref, carry):
        @pl.when(pl.program_id(2) == 0)
        def _():
            carry[...] = jnp.zeros_like(carry)

        xb = x_ref[...].astype(BF16)
        h_ref[0] = jnp.dot(xb, m_ref[0], preferred_element_type=F32)
        h_ref[1] = jnp.dot(xb, m_ref[1], preferred_element_type=F32)
        ar, ai = jnp.broadcast_to(a_ref[0], (SUB, SL)), jnp.broadcast_to(a_ref[1], (SUB, SL))
        bw = _s5_backward(pl.program_id(0), rev)

        def step(t, c):
            hr, hi = c
            row = pl.multiple_of(jnp.where(bw, NG - 1 - t, t) * SUB, SUB)
            nr = ar * hr - ai * hi + h_ref[0, pl.ds(row, SUB), :]
            ni = ar * hi + ai * hr + h_ref[1, pl.ds(row, SUB), :]
            h_ref[0, pl.ds(row, SUB), :] = nr
            h_ref[1, pl.ds(row, SUB), :] = ni
            return nr, ni

        hr, hi = lax.fori_loop(0, NG, step, (carry[0], carry[1]))
        carry[0], carry[1] = hr, hi
        f_ref[0], f_ref[1] = hr, hi

    return pl.pallas_call(
        body, name=name, grid=(2, NS, NCH),
        in_specs=[pl.BlockSpec((None, TC, CW), lambda d, j, k: (d, ck(d, k), j)),
                  pl.BlockSpec((None, 2, None, CW, SL), lambda d, j, k: (d, 0, j, 0, 0)),
                  pl.BlockSpec((None, 2, 1, SL), lambda d, j, k: (d, 0, 0, j))],
        out_specs=[pl.BlockSpec((None, 2, TC, SL), lambda d, j, k: (d, 0, ck(d, k), j)),
                   pl.BlockSpec((None, 2, SUB, SL), lambda d, j, k: (d, 0, 0, j))],
        out_shape=[jax.ShapeDtypeStruct((2, 2, T, N), F32), jax.ShapeDtypeStruct((2, 2, SUB, N), F32)],
        scratch_shapes=[pltpu.VMEM((2, SUB, SL), F32)],
        compiler_params=_params("parallel", "parallel", "arbitrary"),
    )(xin, mats, a)


def s5_fix(name, hloc, hin, a, mats, rev):
    _, _, T, N = hloc.shape
    TC, NG, NCH, NS = _s5_dims(T, N)
    SL = S5_STRIP
    CW = mats.shape[-1]

    def ck(d, k):
        return jnp.where(_s5_backward(d, rev), NCH - 1 - k, k)

    def body(h_ref, hin_ref, a_ref, m_ref, ho_ref, y_ref, g):
        @pl.when(pl.program_id(2) == 0)
        def _():
            g[...] = hin_ref[...]

        ar, ai = jnp.broadcast_to(a_ref[0], (SUB, SL)), jnp.broadcast_to(a_ref[1], (SUB, SL))
        bw = _s5_backward(pl.program_id(0), rev)

        def step(t, c):
            gr, gi = c
            row = pl.multiple_of(jnp.where(bw, NG - 1 - t, t) * SUB, SUB)
            nr = ar * gr - ai * gi
            ni = ar * gi + ai * gr
            ho_ref[0, pl.ds(row, SUB), :] = h_ref[0, pl.ds(row, SUB), :] + nr
            ho_ref[1, pl.ds(row, SUB), :] = h_ref[1, pl.ds(row, SUB), :] + ni
            return nr, ni

        gr, gi = lax.fori_loop(0, NG, step, (g[0], g[1]))
        g[0], g[1] = gr, gi
        y_ref[...] = (jnp.dot(ho_ref[0].astype(BF16), m_ref[0], preferred_element_type=F32)
                      + jnp.dot(ho_ref[1].astype(BF16), m_ref[1], preferred_element_type=F32))

    return pl.pallas_call(
        body, name=name, grid=(2, NS, NCH),
        in_specs=[pl.BlockSpec((None, 2, TC, SL), lambda d, j, k: (d, 0, ck(d, k), j)),
                  pl.BlockSpec((None, 2, SUB, SL), lambda d, j, k: (d, 0, 0, j)),
                  pl.BlockSpec((None, 2, 1, SL), lambda d, j, k: (d, 0, 0, j)),
                  pl.BlockSpec((None, 2, None, SL, CW), lambda d, j, k: (d, 0, j, 0, 0))],
        out_specs=[pl.BlockSpec((None, 2, TC, SL), lambda d, j, k: (d, 0, ck(d, k), j)),
                   pl.BlockSpec((None, TC, CW), lambda d, j, k: (d, ck(d, k), j))],
        out_shape=[jax.ShapeDtypeStruct((2, 2, T, N), F32), jax.ShapeDtypeStruct((2, T, NS * CW), F32)],
        scratch_shapes=[pltpu.VMEM((2, SUB, SL), F32)],
        compiler_params=_params("parallel", "parallel", "arbitrary"),
    )(hloc, hin, a, mats)


def s5_grads(name, g, h, u, dy):
    _, _, T, N = g.shape
    W = u.shape[-1]
    TC, NG, NCH, NS = _s5_dims(T, N)
    CW, SL = W // NS, S5_STRIP

    def body(g_ref, h_ref, hp_ref, hl_ref, u_ref, dy_ref, dm_ref, dc_ref, da_ref, hs):
        k = pl.program_id(2)
        sub = lax.broadcasted_iota(jnp.int32, (SUB, SL), 0)

        @pl.when(pl.program_id(0) == 0)
        def _():
            for z in range(2):
                wrapped = jnp.where(sub == 0, 0.0, pltpu.roll(hl_ref[z], 1, 0))
                hs[z, 0:SUB, :] = jnp.where(k == 0, wrapped, hp_ref[z])
                hs[z, SUB:TC, :] = h_ref[z, 0:TC - SUB, :]

        @pl.when(pl.program_id(0) == 1)
        def _():
            for z in range(2):
                wrapped = jnp.where(sub == SUB - 1, 0.0, pltpu.roll(hl_ref[z], SUB - 1, 0))
                hs[z, TC - SUB:TC, :] = jnp.where(k == NCH - 1, wrapped, hp_ref[z])
                hs[z, 0:TC - SUB, :] = h_ref[z, SUB:TC, :]

        gr, gi, pr, pi = g_ref[0], g_ref[1], hs[0], hs[1]
        dar = jnp.sum((gr * pr + gi * pi).reshape(NG, SUB, SL), axis=0)
        dai = jnp.sum((gi * pr - gr * pi).reshape(NG, SUB, SL), axis=0)
        ub, dyb = u_ref[...].astype(BF16), dy_ref[...].astype(BF16)
        dm = [lax.dot_general(ub, g_ref[z].astype(BF16), TN, preferred_element_type=F32) for z in range(2)]
        dc = [lax.dot_general(dyb, h_ref[z].astype(BF16), TN, preferred_element_type=F32) for z in range(2)]

        @pl.when(k == 0)
        def _():
            da_ref[0], da_ref[1] = dar, dai
            for z in range(2):
                dm_ref[z], dc_ref[z] = dm[z], dc[z]

        @pl.when(k > 0)
        def _():
            da_ref[0] += dar
            da_ref[1] += dai
            for z in range(2):
                dm_ref[z] += dm[z]
                dc_ref[z] += dc[z]

    big = pl.BlockSpec((None, 2, TC, SL), lambda d, j, k: (d, 0, k, j))
    tok = pl.BlockSpec((None, TC, CW), lambda d, j, k: (d, k, j))
    mat = pl.BlockSpec((None, 2, None, CW, SL), lambda d, j, k: (d, 0, j, 0, 0))
    return pl.pallas_call(
        body, name=name, grid=(2, NS, NCH),
        in_specs=[big, big,
                  pl.BlockSpec((None, 2, SUB, SL), lambda d, j, k: (
                      d, 0, jnp.where(d == 0, jnp.maximum(k * NG - 1, 0), jnp.minimum((k + 1) * NG, T // SUB - 1)), j)),
                  pl.BlockSpec((None, 2, SUB, SL), lambda d, j, k: (d, 0, jnp.where(d == 0, T // SUB - 1, 0), j)),
                  tok, tok],
        out_specs=[mat, mat, pl.BlockSpec((None, 2, SUB, SL), lambda d, j, k: (d, 0, 0, j))],
        out_shape=[jax.ShapeDtypeStruct((2, 2, NS, CW, SL), F32), jax.ShapeDtypeStruct((2, 2, NS, CW, SL), F32),
                   jax.ShapeDtypeStruct((2, 2, SUB, N), F32)],
        scratch_shapes=[pltpu.VMEM((2, TC, SL), F32)],
        compiler_params=_params("parallel", "parallel", "arbitrary"),
    )(g, h, h, h, u, dy)


def _interleave(seq):
    *lead, T, W = seq.shape
    n = len(lead)
    return seq.reshape(*lead, S5_SEG, T // S5_SEG, W).swapaxes(n, n + 1).reshape(*lead, T, W)


def _deinterleave(seq):
    *lead, T, W = seq.shape
    n = len(lead)
    return seq.reshape(*lead, T // S5_SEG, S5_SEG, W).swapaxes(n, n + 1).reshape(*lead, T, W)


def _s5_discretize(lam_re, lam_im, log_dt, b_re, b_im):
    dt = jnp.exp(log_dt)[..., None]
    mag = jnp.exp(lam_re * dt)
    a_re = mag * jnp.cos(lam_im * dt)
    a_im = mag * jnp.sin(lam_im * dt)
    den = jnp.square(lam_re) + jnp.square(lam_im)
    f_re = ((a_re - 1.0) * lam_re + a_im * lam_im) / den
    f_im = (a_im * lam_re - (a_re - 1.0) * lam_im) / den
    bb_re = f_re[..., None] * b_re - f_im[..., None] * b_im
    bb_im = f_re[..., None] * b_im + f_im[..., None] * b_re
    return a_re, a_im, bb_re, bb_im


_GPS = S5_STRIP // SSM_STATE


def _blockdiag(t):
    d2, G, P, Cg = t.shape
    t5 = t.reshape(d2, G // _GPS, _GPS, P, Cg).transpose(0, 1, 2, 4, 3)
    m = t5[:, :, :, :, None, :] * jnp.eye(_GPS, dtype=t.dtype)[None, None, :, None, :, None]
    return m.reshape(d2, G // _GPS, _GPS * Cg, _GPS * P)


def _blockdiag_extract(m, Cg, P):
    d2, NS = m.shape[:2]
    m6 = m.reshape(d2, NS, _GPS, Cg, _GPS, P)
    diag = jnp.stack([m6[:, :, i, :, i, :] for i in range(_GPS)], axis=2)
    return diag.transpose(0, 1, 2, 4, 3).reshape(d2, NS * _GPS, P, Cg)


def _cmul(a, b):
    return a[0] * b[0] - a[1] * b[1], a[0] * b[1] + a[1] * b[0]


def _cpow(a, n):
    out, base = None, a
    while n:
        if n & 1:
            out = base if out is None else _cmul(out, base)
        base = _cmul(base, base)
        n >>= 1
    return out


def _segment_carry(fin, apow, rev):
    per_dir = []
    for d in range(2):
        fr, fi = fin[d, 0], fin[d, 1]
        ap = (apow[0][d], apow[1][d])
        cr = ci = jnp.zeros_like(fr[0:1])
        outs = [None] * S5_SEG
        backward = (d == 1) != rev
        for s in (range(S5_SEG - 1, -1, -1) if backward else range(S5_SEG)):
            outs[s] = (cr, ci)
            pr, pi = _cmul(ap, (cr, ci))
            cr, ci = pr + fr[s:s + 1], pi + fi[s:s + 1]
        per_dir.append(jnp.stack([jnp.concatenate([o[0] for o in outs]), jnp.concatenate([o[1] for o in outs])]))
    return jnp.stack(per_dir)


def _coords():
    x, y, c = lax.axis_index("x"), lax.axis_index("y"), lax.axis_index("c")
    others = [(1 - x, y), (x, 1 - y), (1 - x, 1 - y)]
    return x, y, c, 2 * x + y, others


def _comm(name, ins, out_shapes, aliases, n_local, n_remote, plan):
    n_in, n_out = len(ins), len(out_shapes)

    def body(*refs):
        in_refs, out_refs = refs[:n_in], refs[n_in:n_in + n_out]
        send_sems, recv_sems, local_sems = refs[n_in + n_out:]
        x, y, c = lax.axis_index("x"), lax.axis_index("y"), lax.axis_index("c")
        locs, sends, lands = plan(in_refs, out_refs)
        assert len(locs) == n_local and len(sends) == n_remote and len(lands) == n_remote
        local = [pltpu.make_async_copy(s, d, local_sems.at[i]) for i, (s, d) in enumerate(locs)]
        for cp in local:
            cp.start()
        remote = [pltpu.make_async_remote_copy(src_ref=s, dst_ref=d, send_sem=send_sems.at[i], recv_sem=recv_sems.at[i],
                                               device_id=peer, device_id_type=MESH)
                  for i, (s, d, peer) in enumerate(sends)]
        for cp in remote:
            cp.start()
        for i, d in enumerate(lands):
            pltpu.make_async_remote_copy(src_ref=d, dst_ref=d, send_sem=send_sems.at[i], recv_sem=recv_sems.at[i],
                                         device_id=(x, y, c), device_id_type=MESH).wait_recv()
        for cp in remote:
            cp.wait_send()
        for cp in local:
            cp.wait()

    any_spec = pl.BlockSpec(memory_space=pl.ANY)
    return pl.pallas_call(
        body, name=name,
        in_specs=[any_spec] * n_in, out_specs=[any_spec] * n_out,
        out_shape=[jax.ShapeDtypeStruct(s, d) for s, d in out_shapes],
        input_output_aliases=aliases,
        scratch_shapes=[pltpu.SemaphoreType.DMA((n_remote,)), pltpu.SemaphoreType.DMA((n_remote,)),
                        pltpu.SemaphoreType.DMA((max(n_local, 1),))],
        compiler_params=pltpu.CompilerParams(has_side_effects=True),
    )(*ins)


def allgather_dev(name, v):
    M, Nc = v.shape

    def plan(in_refs, out_refs):
        (v_ref,), (o_ref,) = in_refs, out_refs
        x, y, c = lax.axis_index("x"), lax.axis_index("y"), lax.axis_index("c")

        def rows(px, py, pc):
            return o_ref.at[pl.ds((4 * px + 2 * py + pc) * M, M), :]

        peers = [(x ^ fx, y ^ fy, c ^ fc) for fx in (0, 1) for fy in (0, 1) for fc in (0, 1) if fx or fy or fc]
        return ([(v_ref, rows(x, y, c))],
                [(v_ref, rows(x, y, c), p) for p in peers],
                [rows(*p) for p in peers])

    return _comm(name, [v], [((N_DEV * M, Nc), v.dtype)], {}, 1, N_DEV - 1, plan)[0]


def allgather_chips_1(name, shards):
    def plan(in_refs, out_refs):
        x, y, c, chip, others = _coords()
        sends, lands = [], []
        for s_ref, g_ref in zip(in_refs, out_refs):
            hr = s_ref.shape[0] // 2
            mine = pl.ds(c * hr, hr)
            for qx, qy in others:
                sends.append((s_ref.at[mine], g_ref.at[chip, mine], (qx, qy, c)))
                lands.append(g_ref.at[2 * qx + qy, mine])
        return [], sends, lands

    n = len(shards)
    comm = (list(shards), [((N_CHIP,) + s.shape, s.dtype) for s in shards], {}, 3 * n, plan)
    return comm if name is None else _comm(name, comm[0], comm[1], comm[2], 0, comm[3], comm[4])


def allgather_chips_2(name, gathered, shards):
    n = len(gathered)

    def plan(in_refs, out_refs):
        x, y, c, chip, others = _coords()
        sends, lands = [], []
        for s_ref, g_ref in zip(in_refs[n:], out_refs):
            hr = g_ref.shape[1] // 2
            for qx, qy in others:
                q = 2 * qx + qy
                sends.append((g_ref.at[q, pl.ds(c * hr, hr)], g_ref.at[q, pl.ds(c * hr, hr)], (x, y, 1 - c)))
                lands.append(g_ref.at[q, pl.ds((1 - c) * hr, hr)])
            sends.append((s_ref, g_ref.at[chip], (x, y, 1 - c)))
            lands.append(g_ref.at[chip])
        return [], sends, lands

    return _comm(name, list(gathered) + list(shards), [(g.shape, g.dtype) for g in gathered], {i: i for i in range(n)},
                 0, 4 * n, plan)


def reduce_1(name, grads):
    def plan(in_refs, out_refs):
        x, y, c, chip, others = _coords()
        sends, lands = [], []
        for g_ref, got_ref in zip(in_refs, out_refs):
            hr = g_ref.shape[1] // 2
            sends.append((g_ref.at[:, pl.ds((1 - c) * hr, hr), :], got_ref, (x, y, 1 - c)))
            lands.append(got_ref)
        return [], sends, lands

    n = len(grads)
    return _comm(name, grads, [((g.shape[0], g.shape[1] // 2, g.shape[2]), g.dtype) for g in grads], {}, 0, n, plan)


def reduce_2(name, parts):
    def plan(in_refs, out_refs):
        x, y, c, chip, others = _coords()
        sends, lands = [], []
        for t_ref, q_ref in zip(in_refs, out_refs):
            for qx, qy in others:
                sends.append((t_ref.at[2 * qx + qy], q_ref.at[chip], (qx, qy, c)))
                lands.append(q_ref.at[2 * qx + qy])
        return [], sends, lands

    n = len(parts)
    comm = (list(parts), [(p.shape, p.dtype) for p in parts], {}, 3 * n, plan)
    return comm if name is None else _comm(name, comm[0], comm[1], comm[2], 0, comm[3], comm[4])


def reduce_3(name, fulls):
    def plan(in_refs, out_refs):
        x, y, c, chip, others = _coords()
        sends, lands = [], []
        for o_ref in out_refs:
            sends.append((o_ref.at[c], o_ref.at[c], (x, y, 1 - c)))
            lands.append(o_ref.at[1 - c])
        return [], sends, lands

    n = len(fulls)
    return _comm(name, fulls, [(f.shape, f.dtype) for f in fulls], {i: i for i in range(n)}, 0, n, plan)


_WEIGHTS = ['c_ctx', 'w_mod', 'b_mod', 'g_mix', 'g_ffn', 'w_in', 'ssm_lam_re', 'ssm_lam_im', 'ssm_log_dt', 'ssm_b_re',
            'ssm_b_im', 'ssm_c_re', 'ssm_c_im', 'ssm_d', 'ssm_w_glu', 'na_rpb', 'w_out', 'cv_w_pw1', 'cv_dw_w', 'cv_dw_b',
            'cv_ln_g', 'cv_ln_b', 'cv_w_pw2', 'ffn_w_up', 'ffn_conv_w', 'ffn_conv_b', 'ffn_w_down', 'g_out']
_INPUTS = ['x', 'c', 'ctx'] + _WEIGHTS + ['loss_target'] + ['m_' + w for w in _WEIGHTS] + ['v_' + w for w in _WEIGHTS]
_GATHERED_SMALL = ['ffn_conv_w', 'cv_dw_w', 'cv_dw_b', 'cv_ln_g', 'cv_ln_b']


def _silu(v):
    return v * jax.nn.sigmoid(v)


def _pack(arrs, cols):
    flat = jnp.concatenate([a.reshape(-1).astype(F32) for a in arrs])
    n = flat.shape[0]
    unit = SUB * cols
    flat = jnp.pad(flat, (0, (-n) % unit))
    return flat.reshape(-1, cols)


def _unpack(buf, shapes):
    flat = buf.reshape(-1)
    out, o = [], 0
    for s in shapes:
        n = int(np.prod(s))
        out.append(flat[o:o + n].reshape(s))
        o += n
    return out


def _ffn_fwd(tag, xin, sh, sc, gt, g, wup, cw3, cb3, wdn, comm_up=None, comm_mid=None):
    hf = norm_mod_fwd(tag + "_norm", xin, g * (1.0 + sc), sh)
    up3 = mm_nn_pieces(tag + "_up", hf, wup, 0, N_CHIP, BF16, halves=2, comm=comm_up)
    up3, got_up = up3 if comm_up is not None else (up3, [])
    act = ffn_mid_fwd(tag + "_mid", up3, cw3, cb3, comm=comm_mid)
    act, got_mid = act if comm_mid is not None else (act, [])
    yf = mm_nn(tag + "_down", act, wdn, BF16)
    return gate_res_fwd(tag + "_res", xin, yf, gt), (xin, hf, up3, act, yf), got_up, got_mid


def _ffn_bwd(tag, dxo, saved, sc, gt, g, wup, cw3, cb3, wdn, comm_mid=None):
    xin, hf, up3, act, yf = saved
    dyf, dgt = gate_res_bwd(tag + "_res_b", dxo, yf, gt)
    dact = mm_nt(tag + "_down_bx", dyf, wdn, BF16)
    dwdn = mm_tn(tag + "_down_bw", act, dyf, BF16)
    mid = ffn_mid_bwd(tag + "_mid_b", up3, dact, cw3, cb3, comm=comm_mid)
    (dup3, dcw3, dcb3), got_mid = mid if comm_mid is not None else (mid, [])
    dhf = mm_nt_pieces(tag + "_up_bx", dup3, wup, BF16, halves=2)
    dwup = mm_tn_pieces(tag + "_up_bw", hf, dup3, N_CHIP, BF16, halves=2)
    dxi, cs1, cs2 = norm_mod_bwd(tag + "_norm_b", xin, dhf, g * (1.0 + sc), dxo)
    return dxi, dict(dsh=cs1[0], dsc=cs2[0] * g, dgt=dgt[0], dg=cs2[0] * (1.0 + sc), dwup=dwup, dwdn=dwdn,
                     dcw=dcw3.transpose(1, 0, 2).reshape(3, -1), dcb=dcb3.reshape(-1)), got_mid


def kernel(x, c, ctx, c_ctx, w_mod, b_mod, g_mix, g_ffn, w_in, ssm_lam_re, ssm_lam_im, ssm_log_dt, ssm_b_re, ssm_b_im, ssm_c_re, ssm_c_im, ssm_d, ssm_w_glu, na_rpb, w_out, cv_w_pw1, cv_dw_w, cv_dw_b, cv_ln_g, cv_ln_b, cv_w_pw2, ffn_w_up, ffn_conv_w, ffn_conv_b, ffn_w_down, g_out, loss_target, m_c_ctx, m_w_mod, m_b_mod, m_g_mix, m_g_ffn, m_w_in, m_ssm_lam_re, m_ssm_lam_im, m_ssm_log_dt, m_ssm_b_re, m_ssm_b_im, m_ssm_c_re, m_ssm_c_im, m_ssm_d, m_ssm_w_glu, m_na_rpb, m_w_out, m_cv_w_pw1, m_cv_dw_w, m_cv_dw_b, m_cv_ln_g, m_cv_ln_b, m_cv_w_pw2, m_ffn_w_up, m_ffn_conv_w, m_ffn_conv_b, m_ffn_w_down, m_g_out, v_c_ctx, v_w_mod, v_b_mod, v_g_mix, v_g_ffn, v_w_in, v_ssm_lam_re, v_ssm_lam_im, v_ssm_log_dt, v_ssm_b_re, v_ssm_b_im, v_ssm_c_re, v_ssm_c_im, v_ssm_d, v_ssm_w_glu, v_na_rpb, v_w_out, v_cv_w_pw1, v_cv_dw_w, v_cv_dw_b, v_cv_ln_g, v_cv_ln_b, v_cv_w_pw2, v_ffn_w_up, v_ffn_conv_w, v_ffn_conv_b, v_ffn_w_down, v_g_out):
    p = dict(locals())
    xi, yi, ci = lax.axis_index("x"), lax.axis_index("y"), lax.axis_index("c")
    me, chip = 4 * xi + 2 * yi + ci, 2 * xi + yi
    xs, cx, tgt = x[0], ctx[0], loss_target[0]
    L, D = xs.shape
    Lc = cx.shape[0]
    T = L + Lc
    W = D // 2
    Cq = w_mod.shape[2]

    s_mix = [t.astype(BF16) for t in (w_in[0], ssm_w_glu[0], w_out[0])]
    s_ffn0 = [t.astype(BF16) for t in (ffn_w_up[0], ffn_w_down[0])]
    s_conv = [t.astype(BF16) for t in (cv_w_pw1[0], cv_w_pw2[0])]
    s_ffn1 = [t.astype(BF16) for t in (ffn_w_up[1], ffn_w_down[1])]
    Win, Wglu, Wout = allgather_chips_2("gather_mix_2", allgather_chips_1("gather_mix_1", s_mix), s_mix)
    Wglu, Wout = Wglu.reshape(-1, Wglu.shape[-1]), Wout.reshape(-1, D)
    Fd = ffn_w_down.shape[1] * N_CHIP
    c_idx = jnp.reshape(ci, (1,)).astype(jnp.int32)
    ids = jnp.stack([chip, ci]).astype(jnp.int32)

    def reduce_front(tag, grads):
        got = reduce_1("reduce_%s_1" % tag, grads)
        return [add_half("reduce_%s_add%d" % (tag, i), g, r, c_idx) for i, (g, r) in enumerate(zip(grads, got))]

    small_shapes = [p[n].shape for n in _GATHERED_SMALL]
    sm = allgather_dev("gather_small", _pack([p[n] for n in _GATHERED_SMALL], 1024))
    sm = sm.reshape(N_DEV, -1)[0::2]
    per_chip = [_unpack(sm[q], small_shapes) for q in range(N_CHIP)]
    conv_w_f, dw_w_f, dw_b_f, ln_g_f, ln_b_f = (jnp.concatenate([pc[i] for pc in per_chip], axis=-1)
                                                for i in range(len(_GATHERED_SMALL)))
    cw3 = [conv_w_f[l].reshape(3, 2, Fd).transpose(1, 0, 2) for l in range(2)]
    cb3 = [ffn_conv_b[l].reshape(2, 1, Fd) for l in range(2)]
    dw_w_f, dw_b_f, ln_g_f, ln_b_f = dw_w_f[0], dw_b_f[0], ln_g_f[0], ln_b_f[0]

    c_all = allgather_dev("gather_c", jnp.zeros((SUB, D), F32).at[0].set(c[0])).reshape(N_DEV, SUB, D)[:, 0]
    S16 = jnp.concatenate([_silu(c_all), _silu(c_ctx)[None], jnp.zeros((2 * SUB - N_DEV - 1, D), F32)])
    modp = mm_nn_pieces("mod_fwd", S16, w_mod, 0, 2, F32)
    modg = allgather_dev("gather_mod", modp).reshape(N_DEV, 2 * SUB, 2, Cq)[0::2]
    mod_full = modg.transpose(2, 1, 0, 3).reshape(2, 2 * SUB, N_CHIP * Cq) + b_mod[:, None, :]
    mod_me = lax.dynamic_index_in_dim(mod_full, me, axis=1, keepdims=False)
    mods = [[mod_me[l, i * D:(i + 1) * D] for i in range(N_MOD)] for l in range(2)]
    shc, scc = mod_full[0, N_DEV, :D], mod_full[0, N_DEV, D:2 * D]

    sh_m, sc_m, gt_m, sh_f, sc_f, gt_f = mods[0]
    h0 = norm_mod_fwd("l0_norm", xs, g_mix[0] * (1.0 + sc_m), sh_m)
    hc0 = norm_mod_fwd("l0_norm_c", cx, g_mix[0] * (1.0 + scc), shc)
    u = mm_nn_pieces("l0_in_u", h0, Win, 0, 1, F32)
    qkv = mm_nn_pieces("l0_in_qkv", h0, Win, 1, 3, BF16)
    uc = mm_nn_pieces("l0_in_uc", hc0, Win, 0, 1, F32)
    kvc = mm_nn_pieces("l0_in_kvc", hc0, Win, 2, 2, BF16)

    lam_re, lam_im, log_dt = ssm_lam_re[0], ssm_lam_im[0], ssm_log_dt[0]
    b_re, b_im, c_re, c_im = ssm_b_re[0], ssm_b_im[0], ssm_c_re[0], ssm_c_im[0]
    (a_re, a_im, bb_re, bb_im), disc_vjp = jax.vjp(_s5_discretize, lam_re, lam_im, log_dt, b_re, b_im)
    G, P, Cg = bb_re.shape[1:]
    N = G * P
    a_re, a_im = a_re.reshape(2, 1, N), a_im.reshape(2, 1, N)
    a_f, a_b = jnp.stack([a_re, a_im], axis=1), jnp.stack([a_re, -a_im], axis=1)
    Bblk = jnp.stack([_blockdiag(bb_re), _blockdiag(bb_im)], axis=1)
    Cblk = jnp.stack([_blockdiag(c_re.swapaxes(-1, -2)), -_blockdiag(c_im.swapaxes(-1, -2))], axis=1)
    apow = _cpow((a_re, a_im), T // S5_SEG)

    useq = _interleave(jnp.stack([jnp.concatenate([uc, u]), jnp.concatenate([u, uc])]))
    hloc, fin = s5_scan("s5_scan", useq, Bblk.astype(BF16), a_f, rev=False)
    hst, yseq = s5_fix("s5_fix", hloc, _segment_carry(fin, apow, False), a_f, Cblk.swapaxes(-1, -2).astype(BF16), rev=False)
    ys = _deinterleave(yseq)
    y0, y1 = ys[0, Lc:], ys[1, :L]
    s5o = glu_fwd("s5_glu", u, y0, y1, ssm_d[0], Wglu)

    bias = na_bias(na_rpb[0])
    (o_na, lse), g_ffn0 = natten_fwd("na_fwd", qkv, kvc, bias, comm=allgather_chips_1(None, s_ffn0))
    Wup0, Wdn0 = allgather_chips_2("gather_ffn0_2", g_ffn0, s_ffn0)
    mixcat = jnp.concatenate([s5o, o_na], axis=1)
    ymix = mm_nn("l0_out", mixcat, Wout, BF16)
    x1 = gate_res_fwd("l0_res", xs, ymix, gt_m)
    x2, ffn0, g_ffn1, g_conv = _ffn_fwd("f0", x1, sh_f, sc_f, gt_f, g_ffn[0], Wup0, cw3[0], cb3[0], Wdn0.reshape(-1, D),
                                        comm_up=allgather_chips_1(None, s_ffn1), comm_mid=allgather_chips_1(None, s_conv))
    Wpw1, Wpw2, Wup1, Wdn1 = allgather_chips_2("gather_l1_2", g_conv + g_ffn1, s_conv + s_ffn1)
    Wpw2 = Wpw2.reshape(-1, D)
    Wup, Wdn = [Wup0, Wup1], [Wdn0.reshape(-1, D), Wdn1.reshape(-1, D)]

    sh_v, sc_v, gt_v, sh_g, sc_g, gt_g = mods[1]
    hcv = norm_mod_fwd("l1_norm", x2, g_mix[1] * (1.0 + sc_v), sh_v)
    ag3 = mm_nn_pieces("l1_pw1", hcv, Wpw1, 0, N_CHIP, BF16, halves=2)
    z1, z3 = conf_mid_fwd("l1_mid", ag3, dw_w_f, dw_b_f, ln_g_f, ln_b_f)
    ycv = mm_nn("l1_pw2", z3, Wpw2, BF16)
    x3 = gate_res_fwd("l1_res", x2, ycv, gt_v)
    x4, ffn1, _, _ = _ffn_fwd("f1", x3, sh_g, sc_g, gt_g, g_ffn[1], Wup[1], cw3[1], cb3[1], Wdn[1])

    dx4, dg_out, loss_part = loss_head("loss", x4, g_out, tgt)
    loss = lax.psum(loss_part[0, 0], ("x", "y", "c"))

    dx3, gf1, _ = _ffn_bwd("f1", dx4, ffn1, sc_g, gt_g, g_ffn[1], Wup[1], cw3[1], cb3[1], Wdn[1])
    parts_ffn1 = reduce_front("ffn1", [gf1["dwup"], gf1["dwdn"].reshape(N_CHIP, -1, D)])
    dycv, dgt_v = gate_res_bwd("l1_res_b", dx3, ycv, gt_v)
    dz3 = mm_nt("l1_pw2_bx", dycv, Wpw2, BF16)
    dWpw2 = mm_tn("l1_pw2_bw", z3, dycv, BF16)
    dz1, dln_g, dln_b = conf_ln_bwd("l1_ln_b", z1, dz3, ln_g_f, ln_b_f)
    (dag3, ddw_w, ddw_b), slots_ffn1 = conf_conv_bwd("l1_conv_b", ag3, dz1, dw_w_f, comm=reduce_2(None, parts_ffn1))
    dhcv = mm_nt_pieces("l1_pw1_bx", dag3, Wpw1, BF16, halves=2)
    dWpw1 = mm_tn_pieces("l1_pw1_bw", hcv, dag3, N_CHIP, BF16, halves=2)
    dx2, cs1_v, cs2_v = norm_mod_bwd("l1_norm_b", x2, dhcv, g_mix[1] * (1.0 + sc_v), dx3)
    parts_conv = reduce_front("conv", [dWpw1, dWpw2.reshape(N_CHIP, -1, D)])

    dx1, gf0, slots_conv = _ffn_bwd("f0", dx2, ffn0, sc_f, gt_f, g_ffn[0], Wup[0], cw3[0], cb3[0], Wdn[0],
                                    comm_mid=reduce_2(None, parts_conv))
    parts_ffn0 = reduce_front("ffn0", [gf0["dwup"], gf0["dwdn"].reshape(N_CHIP, -1, D)])
    dymix, dgt_m = gate_res_bwd("l0_res_b", dx1, ymix, gt_m)
    dmix = mm_nt("l0_out_bx", dymix, Wout, BF16)
    dWout = mm_tn("l0_out_bw", mixcat, dymix, BF16)
    (dq, dk, dv, dkc, dvc, dbias), slots_ffn0 = natten_bwd("na_bwd", qkv, kvc, bias, o_na, lse, dmix,
                                                           comm=reduce_2(None, parts_ffn0))
    dy, zg, dzz, dd_skip = glu_bwd("s5_glu_b", u, y0, y1, ssm_d[0], Wglu, dmix)
    dWglu = mm_tn("s5_glu_bw", zg, dzz, BF16)

    zc = jnp.zeros((Lc, W), F32)
    dyseq = _interleave(jnp.stack([jnp.concatenate([zc, dy]), jnp.concatenate([dy, zc])]))
    gloc, gfin = s5_scan("s5_scan_b", dyseq, Cblk.astype(BF16), a_b, rev=True)
    apow_b = (apow[0], -apow[1])
    gst, duseq = s5_fix("s5_fix_b", gloc, _segment_carry(gfin, apow_b, True), a_b, Bblk.swapaxes(-1, -2).astype(BF16), rev=True)
    dBm, dCm, da8 = s5_grads("s5_grads", gst, hst, useq, dyseq)
    dus = _deinterleave(duseq)
    du = fma3("s5_du", dy, dus[0, Lc:], dus[1, :L], ssm_d[0], BF16)
    duc = dus[0, :Lc] + dus[1, L:]

    d_in = jnp.concatenate([du, dq, dk.astype(BF16), dv.astype(BF16)], axis=1)
    d_in_c = jnp.concatenate([duc.astype(BF16), jnp.zeros((Lc, W), BF16), dkc.astype(BF16), dvc.astype(BF16)], axis=1)
    dh0 = mm_nt_pieces("l0_in_bx", d_in, Win, BF16)
    dhc0 = mm_nt_pieces("l0_in_bxc", d_in_c, Win, BF16)
    dWin = mm_tn_pieces("l0_in_bw", jnp.concatenate([hc0, h0]), jnp.concatenate([d_in_c, d_in]), N_CHIP, BF16)
    dx0, cs1_m, cs2_m = norm_mod_bwd("l0_norm_b", xs, dh0, g_mix[0] * (1.0 + sc_m), dx1)
    _, cs1_c, cs2_c = norm_mod_bwd("l0_norm_bc", cx, dhc0, g_mix[0] * (1.0 + scc), jnp.zeros_like(cx))

    dmod0 = jnp.concatenate([cs1_m[0], cs2_m[0] * g_mix[0], dgt_m[0], gf0["dsh"], gf0["dsc"], gf0["dgt"]])
    dmod1 = jnp.concatenate([cs1_v[0], cs2_v[0] * g_mix[1], dgt_v[0], gf1["dsh"], gf1["dsc"], gf1["dgt"]])
    dmodc = jnp.concatenate([cs1_c[0], cs2_c[0] * g_mix[0], jnp.zeros((4 * D,), F32)])
    dm_rows = jnp.concatenate([jnp.stack([dmod0, dmod1, dmodc]), jnp.zeros((SUB - 3, N_MOD * D), F32)])
    dm_all = allgather_dev("gather_dmod", dm_rows).reshape(N_DEV, SUB, N_MOD * D)
    dm_sum = sum_lead("sum_dmod", dm_all, F32)
    pad7 = jnp.zeros((2 * SUB - N_DEV - 1, N_MOD * D), F32)
    dMod = [jnp.concatenate([dm_all[:, 0], dm_sum[2:3], pad7]), jnp.concatenate([dm_all[:, 1], jnp.zeros_like(dm_sum[2:3]), pad7])]
    dMod_cols = [lax.dynamic_slice_in_dim(m, chip * Cq, Cq, axis=1) for m in dMod]
    g_w_mod = jnp.stack([mm_tn("mod_bw%d" % l, S16, dMod_cols[l], F32) for l in range(2)])
    g_b_mod = jnp.stack([dm_sum[0] + dm_sum[2], dm_sum[1]])
    ds_part = mm_nt("mod_bx", dMod_cols[0], w_mod[0], F32)
    ds_all = allgather_dev("gather_dsc", jnp.zeros((SUB, D), F32).at[0].set(ds_part[N_DEV]))
    ds_c = sum_lead("sum_dsc", ds_all.reshape(N_DEV, SUB, D)[0::2], F32)[0]
    sg_c = jax.nn.sigmoid(c_ctx)
    g_c_ctx = ds_c * sg_c * (1.0 + c_ctx * (1.0 - sg_c))

    H = W // NA_HEAD_DIM
    db5 = dbias.reshape(H, NA_WIN_R, GRID_W, NA_WIN_R, GRID_W).transpose(0, 1, 3, 2, 4).reshape(H * NA_WIN_R * NA_WIN_R, GRID_W * GRID_W)
    dcol = mm_nt("na_bias_fold", db5, na_bias_fold_matrix(), F32, exact=True)
    dcol = dcol.reshape(H, NA_WIN_R, NA_WIN_R, LANE)[..., :2 * NA_WIN_C - 1]
    ridx = np.arange(NA_WIN_R)[None, :] - np.arange(NA_WIN_R)[:, None] + (NA_WIN_R - 1)
    rsel = jnp.asarray(ridx[:, :, None] == np.arange(2 * NA_WIN_R - 1)[None, None, :], F32)
    g_rpb_loc = jnp.einsum("hoic,oir->hrc", dcol, rsel)

    dbb = [_blockdiag_extract(dBm[:, z], Cg, P) for z in range(2)]
    dcc = [_blockdiag_extract(dCm[:, z], Cg, P).swapaxes(-1, -2) for z in range(2)]
    da = jnp.sum(da8, axis=2).reshape(2, 2, G, P)
    small = {
        "g_mix": jnp.stack([cs2_m[0] * (1.0 + sc_m) + cs2_c[0] * (1.0 + scc), cs2_v[0] * (1.0 + sc_v)]),
        "g_ffn": jnp.stack([gf0["dg"], gf1["dg"]]),
        "a_re": da[:, 0], "a_im": da[:, 1], "bb_re": dbb[0], "bb_im": dbb[1], "c_re": dcc[0], "c_im": -dcc[1],
        "ssm_d": dd_skip, "na_rpb": g_rpb_loc, "cv_dw_w": ddw_w, "cv_dw_b": ddw_b, "cv_ln_g": dln_g, "cv_ln_b": dln_b,
        "ffn_conv_w": jnp.stack([gf0["dcw"], gf1["dcw"]]), "ffn_conv_b": jnp.stack([gf0["dcb"], gf1["dcb"]]),
        "g_out": dg_out,
    }
    skeys = list(small)
    sbuf = _pack([small[k] for k in skeys], 1024)
    sall = allgather_dev("gather_small_g", sbuf).reshape(N_DEV, sbuf.shape[0], 1024)
    ssum = dict(zip(skeys, _unpack(sum_lead("sum_small_g", sall, F32), [small[k].shape for k in skeys])))
    g_lam_re, g_lam_im, g_log_dt, g_b_re, g_b_im = disc_vjp((ssum["a_re"], ssum["a_im"], ssum["bb_re"], ssum["bb_im"]))

    def my_cols(t):
        n = t.shape[-1] // N_CHIP
        return lax.dynamic_slice_in_dim(t, chip * n, n, axis=t.ndim - 1)

    parts_mix = reduce_front("mix", [dWin, dWglu.reshape(N_CHIP, -1, W), dWout.reshape(N_CHIP, -1, D)])
    slots_mix = reduce_2("reduce_mix_2", parts_mix)
    parts = parts_mix + parts_conv + parts_ffn0 + parts_ffn1
    slots = list(slots_mix) + list(slots_conv) + list(slots_ffn0) + list(slots_ffn1)
    fulls = [sum_slots("reduce_sum_%d" % i, s, t, ids) for i, (s, t) in enumerate(zip(slots, parts))]
    full = [f.reshape(-1, f.shape[-1]) for f in reduce_3("reduce_g_3", fulls)]
    gWin, gWglu, gWout, gWpw1, gWpw2, gWup0, gWdn0, gWup1, gWdn1 = full

    grads = {
        "c_ctx": g_c_ctx, "w_mod": g_w_mod, "b_mod": g_b_mod, "g_mix": ssum["g_mix"], "g_ffn": ssum["g_ffn"],
        "w_in": gWin[None], "ssm_lam_re": g_lam_re[None], "ssm_lam_im": g_lam_im[None], "ssm_log_dt": g_log_dt[None],
        "ssm_b_re": g_b_re[None], "ssm_b_im": g_b_im[None], "ssm_c_re": ssum["c_re"][None], "ssm_c_im": ssum["c_im"][None],
        "ssm_d": ssum["ssm_d"], "ssm_w_glu": gWglu[None], "na_rpb": ssum["na_rpb"][None], "w_out": gWout[None],
        "cv_w_pw1": gWpw1[None], "cv_dw_w": my_cols(ssum["cv_dw_w"])[None], "cv_dw_b": my_cols(ssum["cv_dw_b"]),
        "cv_ln_g": my_cols(ssum["cv_ln_g"]), "cv_ln_b": my_cols(ssum["cv_ln_b"]), "cv_w_pw2": gWpw2[None],
        "ffn_w_up": jnp.stack([gWup0, gWup1]), "ffn_conv_w": my_cols(ssum["ffn_conv_w"]), "ffn_conv_b": ssum["ffn_conv_b"],
        "ffn_w_down": jnp.stack([gWdn0, gWdn1]), "g_out": ssum["g_out"][0],
    }
    grads = {k: grads[k].reshape(p[k].shape) for k in _WEIGHTS}

    large = [k for k in _WEIGHTS if p[k].size >= (1 << 18)]
    tiny = [k for k in _WEIGHTS if k not in large]
    delta, new_m, new_v = {}, {}, {}
    for k in large:
        delta[k], new_m[k], new_v[k] = adamw("adamw_" + k, p[k], grads[k], p["m_" + k], p["v_" + k])
    packs = [_pack([src[pre + k] for k in tiny], 1024) for src, pre in ((p, ""), (grads, ""), (p, "m_"), (p, "v_"))]
    outs = adamw("adamw_small", *packs)
    shapes = [p[k].shape for k in tiny]
    for dst, buf in zip((delta, new_m, new_v), outs):
        dst.update(zip(tiny, _unpack(buf, shapes)))

    return (loss, dx0[None], *[grads[k] for k in _WEIGHTS], *[delta[k] for k in _WEIGHTS],
            *[new_m[k] for k in _WEIGHTS], *[new_v[k] for k in _WEIGHTS])
```

```python
import functools
import math

import numpy as np
import jax
import jax.numpy as jnp
from jax import lax
from jax.experimental import pallas as pl
from jax.experimental.pallas import tpu as pltpu

F32, BF16 = jnp.float32, jnp.bfloat16
MESH = pl.DeviceIdType.MESH
V7X_VMEM_LIMIT = 56 << 20
LANE, SUB = 128, 8
N_CHIP, N_DEV = 4, 8

GRID_W = 64
N_MOD = 6
SSM_GROUP, SSM_STATE = 16, 64
NA_HEAD_DIM, NA_WIN_R, NA_WIN_C = 128, 8, 16
EPS = 1e-6
NEG = -1e30
ADAM_LR, ADAM_B1, ADAM_B2, ADAM_EPS, ADAM_WD, ADAM_STEP = 0.001, 0.9, 0.999, 1e-08, 0.01, 10
S5_STRIP = 512
S5_SEG = 8

NN = (((1,), (0,)), ((), ()))
NT = (((1,), (1,)), ((), ()))
TN = (((0,), (0,)), ((), ()))


def _params(*sem, side_effects=False):
    return pltpu.CompilerParams(dimension_semantics=sem if sem else None, vmem_limit_bytes=V7X_VMEM_LIMIT,
                                has_side_effects=side_effects)


def _call(body, args, *, name, grid, in_specs, out_specs, out_shape, sem, scratch_shapes=(), comm=None):
    out_specs, out_shape, scratch_shapes = list(out_specs), list(out_shape), list(scratch_shapes)
    if comm is None:
        outs = pl.pallas_call(body, name=name, grid=grid, in_specs=list(in_specs), out_specs=out_specs, out_shape=out_shape,
                              scratch_shapes=scratch_shapes, compiler_params=_params(*sem))(*args)
        return list(outs), []
    c_args, c_shapes, c_alias, n_remote, plan = comm
    n_in, n_out, n_ci, n_co, n_sc = len(args), len(out_shape), len(c_args), len(c_shapes), len(scratch_shapes)

    def wrapped(*refs):
        ins, cins = refs[:n_in], refs[n_in:n_in + n_ci]
        o0 = n_in + n_ci
        outs, couts = refs[o0:o0 + n_out], refs[o0 + n_out:o0 + n_out + n_co]
        s0 = o0 + n_out + n_co
        scr, (send_sems, recv_sems) = refs[s0:s0 + n_sc], refs[s0 + n_sc:]
        pids = [pl.program_id(a) for a in range(len(grid))]
        first = functools.reduce(jnp.logical_and, [q == 0 for q in pids])
        last = functools.reduce(jnp.logical_and, [q == g - 1 for q, g in zip(pids, grid)])
        me = (lax.axis_index("x"), lax.axis_index("y"), lax.axis_index("c"))

        def copies():
            _, sends, lands = plan(cins, couts)
            assert len(sends) == n_remote and len(lands) == n_remote
            out = [pltpu.make_async_remote_copy(src_ref=s, dst_ref=d, send_sem=send_sems.at[i], recv_sem=recv_sems.at[i],
                                                device_id=peer, device_id_type=MESH) for i, (s, d, peer) in enumerate(sends)]
            arrivals = [pltpu.make_async_remote_copy(src_ref=d, dst_ref=d, send_sem=send_sems.at[i], recv_sem=recv_sems.at[i],
                                                     device_id=me, device_id_type=MESH) for i, d in enumerate(lands)]
            return out, arrivals

        @pl.when(first)
        def _():
            for cp in copies()[0]:
                cp.start()

        body(*ins, *outs, *scr)

        @pl.when(last)
        def _():
            out, arrivals = copies()
            for cp in arrivals:
                cp.wait_recv()
            for cp in out:
                cp.wait_send()

    any_spec = pl.BlockSpec(memory_space=pl.ANY)
    res = pl.pallas_call(
        wrapped, name=name, grid=grid,
        in_specs=[*in_specs, *[any_spec] * n_ci], out_specs=[*out_specs, *[any_spec] * n_co],
        out_shape=[*out_shape, *[jax.ShapeDtypeStruct(s, d) for s, d in c_shapes]],
        input_output_aliases={n_in + i: n_out + j for i, j in c_alias.items()},
        scratch_shapes=[*scratch_shapes, pltpu.SemaphoreType.DMA((n_remote,)), pltpu.SemaphoreType.DMA((n_remote,))],
        compiler_params=_params(*["arbitrary"] * len(grid), side_effects=True),
    )(*args, *c_args)
    return list(res[:n_out]), list(res[n_out:])


def _pick(n, pref, mult=LANE):
    if n <= pref:
        return n
    best = None
    for t in range(mult, pref + 1, mult):
        if n % t == 0:
            best = t
    assert best is not None, (n, pref, mult)
    return best


def _sigmoid(x):
    return 1.0 / (1.0 + jnp.exp(-x))


def _mm(name, a, b, *, dims, grid, a_spec, b_spec, o_spec, out_shape, out_dtype, acc_shape, exact=False, comm=None):
    nk = grid[2]

    def body(a_ref, b_ref, o_ref, *scratch):
        if exact:
            part = lax.dot_general(a_ref[...], b_ref[...], dims, preferred_element_type=F32,
                                   precision=lax.Precision.HIGHEST)
        else:
            part = lax.dot_general(a_ref[...].astype(BF16), b_ref[...].astype(BF16), dims,
                                   preferred_element_type=F32)
        if nk == 1:
            o_ref[...] = part.astype(o_ref.dtype)
        else:
            acc = scratch[0]
            kk = pl.program_id(2)

            @pl.when(kk == 0)
            def _():
                acc[...] = part

            @pl.when(kk > 0)
            def _():
                acc[...] += part

            @pl.when(kk == nk - 1)
            def _():
                o_ref[...] = acc[...].astype(o_ref.dtype)

    outs, couts = _call(body, [a, b], name=name, grid=grid, in_specs=[a_spec, b_spec], out_specs=[o_spec],
                        out_shape=[jax.ShapeDtypeStruct(out_shape, out_dtype)],
                        scratch_shapes=[] if nk == 1 else [pltpu.VMEM(acc_shape, F32)],
                        sem=("parallel", "parallel", "arbitrary"), comm=comm)
    return outs[0] if comm is None else (outs[0], couts)


MM_VMEM_BUDGET = 36 << 20


def _fit(M, N, cost, m_mult=SUB):
    best = None
    for tm in sorted({_pick(M, p, m_mult) for p in (2048, 1024, 512, 256, 128)}):
        for tn in sorted({_pick(N, p) for p in (1408, 1024, 512, 256, 128)}):
            if best is None or (cost(tm, tn) <= MM_VMEM_BUDGET and tm * tn > best[0] * best[1]):
                best = (tm, tn)
    return best


def _sz(t):
    return jnp.dtype(t).itemsize


def mm_nn_pieces(name, a, w, p0, n_p, out_dtype, halves=1, comm=None):
    M, K = a.shape
    Nq = w.shape[2]
    tm, tn = _fit(M, Nq, lambda m, n: 2 * (m * K * _sz(a.dtype) + K * n * _sz(w.dtype) + m * n * _sz(out_dtype)))
    tpp = Nq // tn
    pph = n_p // halves
    if halves == 1:
        o_spec = pl.BlockSpec((tm, tn), lambda i, j, k: (i, j))
        oshape = (M, n_p * Nq)
    else:
        o_spec = pl.BlockSpec((None, tm, tn), lambda i, j, k: ((j // tpp) // pph, i, ((j // tpp) % pph) * tpp + j % tpp))
        oshape = (halves, M, pph * Nq)
    return _mm(name, a, w, dims=NN, grid=(M // tm, n_p * tpp, 1),
               a_spec=pl.BlockSpec((tm, K), lambda i, j, k: (i, 0)),
               b_spec=pl.BlockSpec((None, K, tn), lambda i, j, k: (p0 + j // tpp, 0, j % tpp)),
               o_spec=o_spec, out_shape=oshape, out_dtype=out_dtype, acc_shape=(tm, tn), comm=comm)


def mm_nn(name, a, w, out_dtype, exact=False, comm=None):
    M, K = a.shape
    N = w.shape[1]
    tm, tn = _fit(M, N, lambda m, n: 2 * (m * K * _sz(a.dtype) + K * n * _sz(w.dtype) + m * n * _sz(out_dtype)))
    return _mm(name, a, w, dims=NN, grid=(M // tm, N // tn, 1),
               a_spec=pl.BlockSpec((tm, K), lambda i, j, k: (i, 0)),
               b_spec=pl.BlockSpec((K, tn), lambda i, j, k: (0, j)),
               o_spec=pl.BlockSpec((tm, tn), lambda i, j, k: (i, j)),
               out_shape=(M, N), out_dtype=out_dtype, acc_shape=(tm, tn), exact=exact, comm=comm)


def mm_nt(name, dy, w, out_dtype, exact=False):
    M, N = dy.shape
    K = w.shape[0]
    tm, tn = _fit(M, K, lambda m, n: 2 * (m * N * _sz(dy.dtype) + n * N * _sz(w.dtype) + m * n * _sz(out_dtype)))
    return _mm(name, dy, w, dims=NT, grid=(M // tm, K // tn, 1),
               a_spec=pl.BlockSpec((tm, N), lambda i, j, k: (i, 0)),
               b_spec=pl.BlockSpec((tn, N), lambda i, j, k: (j, 0)),
               o_spec=pl.BlockSpec((tm, tn), lambda i, j, k: (i, j)),
               out_shape=(M, K), out_dtype=out_dtype, acc_shape=(tm, tn), exact=exact)


def mm_nt_pieces(name, dy, w, out_dtype, halves=1):
    P, K, Nq = w.shape
    M = dy.shape[-2]
    tm, tn = _fit(M, K, lambda m, n: 2 * (m * Nq * _sz(dy.dtype) + n * Nq * _sz(w.dtype) + m * n * _sz(out_dtype)) + 4 * m * n)
    pph = P // halves
    if halves == 1:
        a_spec = pl.BlockSpec((tm, Nq), lambda i, j, k: (i, k))
    else:
        a_spec = pl.BlockSpec((None, tm, Nq), lambda i, j, k: (k // pph, i, k % pph))
    return _mm(name, dy, w, dims=NT, grid=(M // tm, K // tn, P),
               a_spec=a_spec,
               b_spec=pl.BlockSpec((None, tn, Nq), lambda i, j, k: (k, j, 0)),
               o_spec=pl.BlockSpec((tm, tn), lambda i, j, k: (i, j)),
               out_shape=(M, K), out_dtype=out_dtype, acc_shape=(tm, tn))


def mm_tn(name, a, dy, out_dtype):
    M, K = a.shape
    N = dy.shape[1]
    tm, tn = _fit(K, N, lambda m, n: 2 * (M * m * _sz(a.dtype) + M * n * _sz(dy.dtype) + m * n * _sz(out_dtype)), LANE)
    return _mm(name, a, dy, dims=TN, grid=(K // tm, N // tn, 1),
               a_spec=pl.BlockSpec((M, tm), lambda i, j, k: (0, i)),
               b_spec=pl.BlockSpec((M, tn), lambda i, j, k: (0, j)),
               o_spec=pl.BlockSpec((tm, tn), lambda i, j, k: (i, j)),
               out_shape=(K, N), out_dtype=out_dtype, acc_shape=(tm, tn))


def mm_tn_pieces(name, a, dy, n_p, out_dtype, halves=1):
    M, K = a.shape
    Nq = (dy.shape[-1] * halves) // n_p
    tm, tn = _fit(K, Nq, lambda m, n: 2 * (M * m * _sz(a.dtype) + M * n * _sz(dy.dtype) + m * n * _sz(out_dtype)), LANE)
    tpp = Nq // tn
    pph = n_p // halves
    if halves == 1:
        b_spec = pl.BlockSpec((M, tn), lambda i, j, k: (0, j))
    else:
        b_spec = pl.BlockSpec((None, M, tn), lambda i, j, k: ((j // tpp) // pph, 0, ((j // tpp) % pph) * tpp + j % tpp))
    return _mm(name, a, dy, dims=TN, grid=(K // tm, n_p * tpp, 1),
               a_spec=pl.BlockSpec((M, tm), lambda i, j, k: (0, i)),
               b_spec=b_spec,
               o_spec=pl.BlockSpec((None, tm, tn), lambda i, j, k: (j // tpp, i, j % tpp)),
               out_shape=(n_p, K, Nq), out_dtype=out_dtype, acc_shape=(tm, tn))


def _row_call(name, body, ins, in_kinds, outs, rows, tr, scratch=()):
    def spec(kind, shape):
        if isinstance(kind, pl.BlockSpec):
            return kind
        if kind == "row":
            return pl.BlockSpec((tr,) + tuple(shape[1:]), lambda i: (i,) + (0,) * (len(shape) - 1))
        return pl.BlockSpec(tuple(shape), lambda i: (0,) * len(shape))

    return pl.pallas_call(
        body, name=name, grid=(rows // tr,),
        in_specs=[spec(k, a.shape) for k, a in zip(in_kinds, ins)],
        out_specs=[spec(k, s) for k, s, _ in outs],
        out_shape=[jax.ShapeDtypeStruct(s, d) for _, s, d in outs],
        scratch_shapes=list(scratch),
        compiler_params=_params("arbitrary"),
    )(*ins)


def _acc(ref, val):
    @pl.when(pl.program_id(0) == 0)
    def _():
        ref[...] = val

    @pl.when(pl.program_id(0) > 0)
    def _():
        ref[...] += val


def norm_mod_fwd(name, x, w, b, tr=256):
    rows, d = x.shape
    tr = _pick(rows, tr, SUB)

    def body(x_ref, w_ref, b_ref, h_ref):
        xv = x_ref[...]
        r = lax.rsqrt(jnp.mean(xv * xv, axis=-1, keepdims=True) + EPS)
        h_ref[...] = (xv * r * w_ref[...] + b_ref[...]).astype(BF16)

    return _row_call(name, body, [x, w.reshape(1, d), b.reshape(1, d)], ["row", "vec", "vec"],
                     [("row", (rows, d), BF16)], rows, tr)[0]


def norm_mod_bwd(name, x, dh, w, dx_in, tr=256):
    rows, d = x.shape
    tr = _pick(rows, tr, SUB)

    def body(x_ref, dh_ref, w_ref, dxi_ref, dx_ref, cs1_ref, cs2_ref):
        xv = x_ref[...]
        r = lax.rsqrt(jnp.mean(xv * xv, axis=-1, keepdims=True) + EPS)
        xn = xv * r
        dhv = dh_ref[...].astype(F32)
        dxn = dhv * w_ref[...]
        dx_ref[...] = dxi_ref[...] + r * (dxn - xn * jnp.mean(dxn * xn, axis=-1, keepdims=True))
        _acc(cs1_ref, jnp.sum(dhv, axis=0, keepdims=True))
        _acc(cs2_ref, jnp.sum(dhv * xn, axis=0, keepdims=True))

    return _row_call(name, body, [x, dh, w.reshape(1, d), dx_in], ["row", "row", "vec", "row"],
                     [("row", (rows, d), F32), ("acc", (1, d), F32), ("acc", (1, d), F32)], rows, tr)


def gate_res_fwd(name, x, y, gate, tr=256):
    rows, d = x.shape
    tr = _pick(rows, tr, SUB)

    def body(x_ref, y_ref, g_ref, o_ref):
        o_ref[...] = x_ref[...] + g_ref[...] * y_ref[...].astype(F32)

    return _row_call(name, body, [x, y, gate.reshape(1, d)], ["row", "row", "vec"],
                     [("row", (rows, d), F32)], rows, tr)[0]


def gate_res_bwd(name, dx, y, gate, tr=256):
    rows, d = dx.shape
    tr = _pick(rows, tr, SUB)

    def body(dx_ref, y_ref, g_ref, dy_ref, dg_ref):
        dxv = dx_ref[...]
        dy_ref[...] = (g_ref[...] * dxv).astype(BF16)
        _acc(dg_ref, jnp.sum(dxv * y_ref[...].astype(F32), axis=0, keepdims=True))

    return _row_call(name, body, [dx, y, gate.reshape(1, d)], ["row", "row", "vec"],
                     [("row", (rows, d), BF16), ("acc", (1, d), F32)], rows, tr)


def loss_head(name, x, g, target, tr=256):
    rows, d = x.shape
    tr = _pick(rows, tr, SUB)

    def body(x_ref, g_ref, t_ref, dx_ref, dg_ref, loss_ref):
        xv = x_ref[...]
        r = lax.rsqrt(jnp.mean(xv * xv, axis=-1, keepdims=True) + EPS)
        xn = xv * r
        err = xn * g_ref[...] - t_ref[...]
        dy = err * (1.0 / d)
        dxn = dy * g_ref[...]
        dx_ref[...] = r * (dxn - xn * jnp.mean(dxn * xn, axis=-1, keepdims=True))
        _acc(dg_ref, jnp.sum(dy * xn, axis=0, keepdims=True))
        part = 0.5 * jnp.sum(jnp.sum(err * err, axis=-1, keepdims=True) * (1.0 / d), axis=0, keepdims=True)
        _acc(loss_ref, jnp.broadcast_to(part, (1, LANE)))

    return _row_call(name, body, [x, g.reshape(1, d), target], ["row", "vec", "row"],
                     [("row", (rows, d), F32), ("acc", (1, d), F32), ("acc", (1, LANE), F32)], rows, tr)


def fma3(name, a, b, c, dvec, out_dtype, tr=256):
    rows, d = a.shape
    tr = _pick(rows, tr, SUB)

    def body(a_ref, b_ref, c_ref, d_ref, o_ref):
        o_ref[...] = (d_ref[...] * a_ref[...] + b_ref[...] + c_ref[...]).astype(o_ref.dtype)

    return _row_call(name, body, [a, b, c, dvec.reshape(1, d)], ["row", "row", "row", "vec"],
                     [("row", (rows, d), out_dtype)], rows, tr)[0]


def sum_lead(name, a, out_dtype, tr=512):
    n, rows, cols = a.shape
    tr = _pick(rows, tr, 16)

    def body(a_ref, o_ref):
        acc = a_ref[0].astype(F32)
        for s in range(1, n):
            acc = acc + a_ref[s].astype(F32)
        o_ref[...] = acc.astype(o_ref.dtype)

    return pl.pallas_call(
        body, name=name, grid=(rows // tr,),
        in_specs=[pl.BlockSpec((n, tr, cols), lambda i: (0, i, 0))],
        out_specs=pl.BlockSpec((tr, cols), lambda i: (i, 0)),
        out_shape=jax.ShapeDtypeStruct((rows, cols), out_dtype),
        compiler_params=_params("parallel"),
    )(a)


def adamw(name, w, g, m, v, tr=512):
    shape = w.shape
    cols = shape[-1]
    w2, g2, m2, v2 = (t.reshape(-1, cols) for t in (w, g, m, v))
    rows = w2.shape[0]
    tr, tc = _pick(rows, 256, SUB), _pick(cols, 1536)
    c1 = 1.0 - ADAM_B1 ** ADAM_STEP
    c2 = 1.0 - ADAM_B2 ** ADAM_STEP

    def body(w_ref, g_ref, m_ref, v_ref, d_ref, mo_ref, vo_ref):
        gv = g_ref[...]
        mn = ADAM_B1 * m_ref[...] + (1.0 - ADAM_B1) * gv
        vn = ADAM_B2 * v_ref[...] + (1.0 - ADAM_B2) * (gv * gv)
        mo_ref[...] = mn
        vo_ref[...] = vn
        d_ref[...] = -ADAM_LR * ((mn / c1) / (jnp.sqrt(vn / c2) + ADAM_EPS) + ADAM_WD * w_ref[...])

    blk = pl.BlockSpec((tr, tc), lambda i, j: (i, j))
    outs = pl.pallas_call(
        body, name=name, grid=(rows // tr, cols // tc), in_specs=[blk] * 4, out_specs=[blk] * 3,
        out_shape=[jax.ShapeDtypeStruct(w2.shape, F32)] * 3, compiler_params=_params("parallel", "parallel"),
    )(w2, g2, m2, v2)
    return tuple(o.reshape(shape) for o in outs)


def add_half(name, grad, got, c_idx, tr=256):
    Pn, R, C = grad.shape
    hr = R // 2
    tr = _pick(hr, tr, HALO)
    nb = hr // tr

    def body(c_ref, a_ref, b_ref, o_ref):
        o_ref[...] = (a_ref[...].astype(F32) + b_ref[...].astype(F32)).astype(o_ref.dtype)

    return pl.pallas_call(
        body, name=name,
        grid_spec=pltpu.PrefetchScalarGridSpec(
            num_scalar_prefetch=1, grid=(Pn, nb),
            in_specs=[pl.BlockSpec((None, tr, C), lambda q, i, c: (q, c[0] * nb + i, 0)),
                      pl.BlockSpec((None, tr, C), lambda q, i, c: (q, i, 0))],
            out_specs=pl.BlockSpec((None, tr, C), lambda q, i, c: (q, i, 0))),
        out_shape=jax.ShapeDtypeStruct((Pn, hr, C), BF16),
        compiler_params=_params("parallel", "parallel"),
    )(c_idx, grad, got)


def pair_sum_to_slot(name, buf, got, ids, tr=256):
    R, C = buf.shape
    hr = R // 2
    tr = _pick(hr, tr, SUB)
    nb = hr // tr

    def body(ids_ref, a_ref, b_ref, o_ref):
        o_ref[...] = a_ref[...] + b_ref[...]

    return pl.pallas_call(
        body, name=name,
        grid_spec=pltpu.PrefetchScalarGridSpec(
            num_scalar_prefetch=1, grid=(nb,),
            in_specs=[pl.BlockSpec((tr, C), lambda i, ids: (ids[1] * nb + i, 0)),
                      pl.BlockSpec((tr, C), lambda i, ids: (i, 0))],
            out_specs=pl.BlockSpec((None, tr, C), lambda i, ids: (ids[0], i, 0))),
        out_shape=jax.ShapeDtypeStruct((N_CHIP, hr, C), F32),
        compiler_params=_params("parallel"),
    )(ids, buf, got)


def sum_chips_to_half(name, slots, ids, tr=256):
    n, hr, C = slots.shape
    tr = _pick(hr, tr, SUB)

    def body(ids_ref, s_ref, o_ref):
        acc = s_ref[0]
        for q in range(1, n):
            acc = acc + s_ref[q]
        o_ref[...] = acc

    return pl.pallas_call(
        body, name=name,
        grid_spec=pltpu.PrefetchScalarGridSpec(
            num_scalar_prefetch=1, grid=(hr // tr,),
            in_specs=[pl.BlockSpec((n, tr, C), lambda i, ids: (0, i, 0))],
            out_specs=pl.BlockSpec((None, tr, C), lambda i, ids: (ids[1], i, 0))),
        out_shape=jax.ShapeDtypeStruct((2, hr, C), F32),
        compiler_params=_params("parallel"),
    )(ids, slots)


def sum_slots(name, slots, mine, ids, tr=256):
    Pn, hr, C = slots.shape
    tr = _pick(hr, tr, HALO)

    def body(ids_ref, m_ref, s1_ref, s2_ref, s3_ref, o_ref):
        o_ref[...] = (m_ref[...].astype(F32) + s1_ref[...].astype(F32)) + (s2_ref[...].astype(F32) + s3_ref[...].astype(F32))

    def other(k):
        return pl.BlockSpec((None, tr, C), lambda i, ids: ((ids[0] + k) % Pn, i, 0))

    return pl.pallas_call(
        body, name=name,
        grid_spec=pltpu.PrefetchScalarGridSpec(
            num_scalar_prefetch=1, grid=(hr // tr,),
            in_specs=[pl.BlockSpec((None, tr, C), lambda i, ids: (ids[0], i, 0)), other(1), other(2), other(3)],
            out_specs=pl.BlockSpec((None, tr, C), lambda i, ids: (ids[1], i, 0))),
        out_shape=jax.ShapeDtypeStruct((2, hr, C), F32),
        compiler_params=_params("parallel"),
    )(ids, mine, slots, slots, slots)


HALO = 16


def _halo_specs(lead, R, tn, n_rows, col_of):
    nb, nblk = R // HALO, n_rows // HALO

    def mk(rows, row_of):
        return pl.BlockSpec((lead, rows, tn), lambda *g: (0, row_of(g[-1]), col_of(g)))

    return (mk(HALO, lambda i: jnp.maximum(i * nb - 1, 0)), mk(R, lambda i: i),
            mk(HALO, lambda i: jnp.minimum((i + 1) * nb, nblk - 1)))


def _fill_halo(dst, i, last, R, prev, cur, nxt):
    nd = len(dst.shape)
    lead = (slice(None),) * (nd - 2)
    dst[lead + (slice(0, HALO), slice(None))] = jnp.where(i == 0, 0.0, prev)
    dst[lead + (slice(HALO, HALO + R), slice(None))] = cur
    dst[lead + (slice(HALO + R, HALO + R + HALO), slice(None))] = jnp.where(i == last, 0.0, nxt)


def _shift_mats(n):
    i = np.arange(n)
    return jnp.asarray(np.stack([i[:, None] - 1 == i[None, :], i[:, None] + 1 == i[None, :]]), BF16)


def _shifted(s_ref, xb):
    return (jnp.dot(s_ref[0], xb, preferred_element_type=F32), jnp.dot(s_ref[1], xb, preferred_element_type=F32))


def ffn_mid_fwd(name, up3, cw, cb, R=256, tn=512, comm=None):
    _, L, Fd = up3.shape
    R, tn = _pick(L, R, HALO), _pick(Fd, tn)
    nrow = L // R

    def body(p_ref, c_ref, n_ref, w_ref, b_ref, s_ref, act_ref):
        i = pl.program_id(1)
        row = lax.broadcasted_iota(jnp.int32, (R, tn), 0)
        cv = []
        for z in range(2):
            xb = c_ref[z]
            before = jnp.where(i == 0, 0.0, p_ref[z].astype(F32)[HALO - 1:HALO])
            after = jnp.where(i == nrow - 1, 0.0, n_ref[z].astype(F32)[0:1])
            dn, up = _shifted(s_ref, xb)
            dn = jnp.where(row == 0, before, dn)
            up = jnp.where(row == R - 1, after, up)
            cv.append(b_ref[z] + w_ref[z, 0:1, :] * dn + w_ref[z, 1:2, :] * xb.astype(F32) + w_ref[z, 2:3, :] * up)
        u, g = cv
        act_ref[...] = (u * g * _sigmoid(g)).astype(BF16)

    hs = _halo_specs(2, R, tn, L, lambda g: g[0])
    outs, couts = _call(
        body, [up3, up3, up3, cw, cb, _shift_mats(R)], name=name, grid=(Fd // tn, nrow),
        in_specs=[*hs, pl.BlockSpec((2, 3, tn), lambda j, i: (0, 0, j)), pl.BlockSpec((2, 1, tn), lambda j, i: (0, 0, j)),
                  pl.BlockSpec((2, R, R), lambda j, i: (0, 0, 0))],
        out_specs=[pl.BlockSpec((R, tn), lambda j, i: (i, j))],
        out_shape=[jax.ShapeDtypeStruct((L, Fd), BF16)], sem=("parallel", "arbitrary"), comm=comm)
    return outs[0] if comm is None else (outs[0], couts)


def ffn_mid_bwd(name, up3, dact, cw, cb, R=256, tn=512, comm=None):
    _, L, Fd = up3.shape
    R, tn = _pick(L, R, HALO), _pick(Fd, tn)
    nrow = L // R
    E = R + 2 * HALO
    inner = slice(HALO, HALO + R)

    def body(pu, cu, nu, pd, cd, nd, w_ref, b_ref, s_ref, dup_ref, dcw_ref, dcb_ref, xs, ds):
        i = pl.program_id(1)
        _fill_halo(xs, i, nrow - 1, R, pu[...], cu[...], nu[...])
        _fill_halo(ds, i, nrow - 1, R, pd[0], cd[0], nd[0])
        da = ds[...].astype(F32)
        cv, taps = [], []
        for z in range(2):
            xb = xs[z]
            dn, up = _shifted(s_ref, xb)
            taps.append((dn, xb.astype(F32), up))
            cv.append(b_ref[z] + w_ref[z, 0:1, :] * dn + w_ref[z, 1:2, :] * taps[z][1] + w_ref[z, 2:3, :] * up)
        u, g = cv
        sg = _sigmoid(g)
        dcs = (da * g * sg, da * u * sg * (1.0 + g * (1.0 - sg)))

        @pl.when(i == 0)
        def _():
            dcw_ref[...] = jnp.zeros_like(dcw_ref)
            dcb_ref[...] = jnp.zeros_like(dcb_ref)

        for z in range(2):
            dc = dcs[z]
            dc_dn, dc_up = _shifted(s_ref, dc.astype(BF16))
            d = w_ref[z, 0:1, :] * dc_up + w_ref[z, 1:2, :] * dc + w_ref[z, 2:3, :] * dc_dn
            dup_ref[z] = d[inner].astype(BF16)
            dci = dc[inner]
            dcb_ref[z] += jnp.sum(dci, axis=0, keepdims=True)
            for k in range(3):
                dcw_ref[z, k:k + 1, :] += jnp.sum(dci * taps[z][k][inner], axis=0, keepdims=True)

    hu = _halo_specs(2, R, tn, L, lambda g: g[0])
    hd = _halo_specs(1, R, tn, L, lambda g: g[0])
    outs, couts = _call(
        body, [up3, up3, up3, dact[None], dact[None], dact[None], cw, cb, _shift_mats(E)], name=name, grid=(Fd // tn, nrow),
        in_specs=[*hu, *hd, pl.BlockSpec((2, 3, tn), lambda j, i: (0, 0, j)), pl.BlockSpec((2, 1, tn), lambda j, i: (0, 0, j)),
                  pl.BlockSpec((2, E, E), lambda j, i: (0, 0, 0))],
        out_specs=[pl.BlockSpec((2, R, tn), lambda j, i: (0, i, j)), pl.BlockSpec((2, 3, tn), lambda j, i: (0, 0, j)),
                   pl.BlockSpec((2, 1, tn), lambda j, i: (0, 0, j))],
        out_shape=[jax.ShapeDtypeStruct((2, L, Fd), BF16), jax.ShapeDtypeStruct((2, 3, Fd), F32),
                   jax.ShapeDtypeStruct((2, 1, Fd), F32)],
        scratch_shapes=[pltpu.VMEM((2, E, tn), BF16), pltpu.VMEM((E, tn), BF16)],
        sem=("parallel", "arbitrary"), comm=comm)
    return outs if comm is None else (outs, couts)


def _glu_z0(blk):
    return blk[0].astype(F32) * _sigmoid(blk[1].astype(F32))


def _sublane_copies(ref, cs):
    n = ref.shape[1]
    blk = ref[0, :, cs]
    for b in range(1, SUB):
        ref[b, :, cs] = pltpu.roll(blk, n - b, 0)


def _tap(ref, offset, rows, cs):
    return ref[offset % SUB, pl.ds(offset - offset % SUB, rows), cs]


def conf_mid_fwd(name, ag3, dw_w, dw_b, ln_g, ln_b, R=128, cb=256):
    _, L, C = ag3.shape
    K = dw_w.shape[0]
    pad = (K - 1) // 2
    assert pad <= HALO
    R, cb = _pick(L, R, HALO), _pick(C, cb)
    nrow = L // R

    def body(p_ref, c_ref, n_ref, w_ref, b_ref, g_ref, bb_ref, z1_ref, z3_ref, s_ref):
        i = pl.program_id(0)
        _fill_halo(s_ref.at[0], i, nrow - 1, R, _glu_z0(p_ref), _glu_z0(c_ref), _glu_z0(n_ref))
        for c0 in range(0, C, cb):
            cs = slice(c0, c0 + cb)
            _sublane_copies(s_ref, cs)
            acc = jnp.broadcast_to(b_ref[:, cs], (R, cb))
            for k in range(K):
                acc = acc + w_ref[k:k + 1, cs] * _tap(s_ref, HALO - pad + k, R, cs)
            z1_ref[:, cs] = acc
        z1 = z1_ref[...]
        zc = z1 - jnp.mean(z1, axis=-1, keepdims=True)
        zn = zc * lax.rsqrt(jnp.mean(zc * zc, axis=-1, keepdims=True) + EPS)
        z2 = zn * g_ref[...] + bb_ref[...]
        z3_ref[...] = (z2 * _sigmoid(z2)).astype(BF16)

    hs = _halo_specs(2, R, C, L, lambda g: 0)
    vec = pl.BlockSpec((1, C), lambda i: (0, 0))
    return pl.pallas_call(
        body, name=name, grid=(nrow,),
        in_specs=[*hs, pl.BlockSpec((K, C), lambda i: (0, 0)), vec, vec, vec],
        out_specs=[pl.BlockSpec((R, C), lambda i: (i, 0)), pl.BlockSpec((R, C), lambda i: (i, 0))],
        out_shape=[jax.ShapeDtypeStruct((L, C), F32), jax.ShapeDtypeStruct((L, C), BF16)],
        scratch_shapes=[pltpu.VMEM((SUB, R + 2 * HALO, C), F32)],
        compiler_params=_params("parallel"),
    )(ag3, ag3, ag3, dw_w, dw_b.reshape(1, C), ln_g.reshape(1, C), ln_b.reshape(1, C))


def conf_ln_bwd(name, z1, dz3, ln_g, ln_b, tr=256):
    rows, C = z1.shape
    tr = _pick(rows, tr, HALO)

    def body(z_ref, d_ref, g_ref, b_ref, dz_ref, dg_ref, db_ref):
        z1v = z_ref[...]
        zc = z1v - jnp.mean(z1v, axis=-1, keepdims=True)
        rs = lax.rsqrt(jnp.mean(zc * zc, axis=-1, keepdims=True) + EPS)
        zn = zc * rs
        z2 = zn * g_ref[...] + b_ref[...]
        sg = _sigmoid(z2)
        dz2 = d_ref[...].astype(F32) * sg * (1.0 + z2 * (1.0 - sg))
        _acc(dg_ref, jnp.sum(dz2 * zn, axis=0, keepdims=True))
        _acc(db_ref, jnp.sum(dz2, axis=0, keepdims=True))
        dzn = dz2 * g_ref[...]
        dz1 = rs * (dzn - jnp.mean(dzn, axis=-1, keepdims=True) - zn * jnp.mean(dzn * zn, axis=-1, keepdims=True))
        dz_ref[...] = dz1.astype(BF16)

    return _row_call(name, body, [z1, dz3, ln_g.reshape(1, C), ln_b.reshape(1, C)], ["row", "row", "vec", "vec"],
                     [("row", (rows, C), BF16), ("acc", (1, C), F32), ("acc", (1, C), F32)], rows, tr)


def conf_conv_bwd(name, ag3, dz1, dw_w, R=128, cb=256, comm=None):
    _, L, C = ag3.shape
    K = dw_w.shape[0]
    pad = (K - 1) // 2
    R, cb = _pick(L, R, HALO), _pick(C, cb)
    nrow = L // R

    def body(pa, ca, na, pd, cd, nd, w_ref, dag_ref, dw_ref, db_ref, s_ref, d_ref, z_ref):
        i = pl.program_id(0)
        _fill_halo(s_ref.at[0], i, nrow - 1, R, _glu_z0(pa), _glu_z0(ca), _glu_z0(na))
        _fill_halo(d_ref.at[0], i, nrow - 1, R, pd[0].astype(F32), cd[0].astype(F32), nd[0].astype(F32))

        @pl.when(i == 0)
        def _():
            dw_ref[...] = jnp.zeros_like(dw_ref)
            db_ref[...] = jnp.zeros_like(db_ref)

        for c0 in range(0, C, cb):
            cs = slice(c0, c0 + cb)
            _sublane_copies(s_ref, cs)
            _sublane_copies(d_ref, cs)
            dcur = d_ref[0, pl.ds(HALO, R), cs]
            acc = jnp.zeros((R, cb), F32)
            for k in range(K):
                acc = acc + w_ref[k:k + 1, cs] * _tap(d_ref, HALO + pad - k, R, cs)
                dw_ref[k:k + 1, cs] += jnp.sum(dcur * _tap(s_ref, HALO - pad + k, R, cs), axis=0, keepdims=True)
            z_ref[:, cs] = acc
            db_ref[:, cs] += jnp.sum(dcur, axis=0, keepdims=True)
        dz0 = z_ref[...]
        a = ca[0].astype(F32)
        sg = _sigmoid(ca[1].astype(F32))
        dag_ref[0] = (dz0 * sg).astype(BF16)
        dag_ref[1] = (dz0 * a * sg * (1.0 - sg)).astype(BF16)

    ha = _halo_specs(2, R, C, L, lambda g: 0)
    hd = _halo_specs(1, R, C, L, lambda g: 0)
    outs, couts = _call(
        body, [ag3, ag3, ag3, dz1[None], dz1[None], dz1[None], dw_w], name=name, grid=(nrow,),
        in_specs=[*ha, *hd, pl.BlockSpec((K, C), lambda i: (0, 0))],
        out_specs=[pl.BlockSpec((2, R, C), lambda i: (0, i, 0)), pl.BlockSpec((K, C), lambda i: (0, 0)),
                   pl.BlockSpec((1, C), lambda i: (0, 0))],
        out_shape=[jax.ShapeDtypeStruct((2, L, C), BF16), jax.ShapeDtypeStruct((K, C), F32),
                   jax.ShapeDtypeStruct((1, C), F32)],
        scratch_shapes=[pltpu.VMEM((SUB, R + 2 * HALO, C), F32), pltpu.VMEM((SUB, R + 2 * HALO, C), F32),
                        pltpu.VMEM((R, C), F32)],
        sem=("arbitrary",), comm=comm)
    return outs if comm is None else (outs, couts)


_GELU_C = math.sqrt(2.0 / math.pi)


def _gelu(x):
    return 0.5 * x * (1.0 + jnp.tanh(_GELU_C * (x + 0.044715 * x * x * x)))


def _gelu_grad(x):
    t = jnp.tanh(_GELU_C * (x + 0.044715 * x * x * x))
    return 0.5 * (1.0 + t) + 0.5 * x * (1.0 - t * t) * _GELU_C * (1.0 + 3.0 * 0.044715 * x * x)


def glu_fwd(name, u, y0, y1, d, wg, tr=512):
    rows, W = u.shape
    tr = _pick(rows, tr, HALO)

    def body(u_ref, y0_ref, y1_ref, d_ref, w_ref, o_ref):
        z = _gelu(d_ref[...] * u_ref[...] + y0_ref[...] + y1_ref[...])
        zz = jnp.dot(z.astype(BF16), w_ref[...], preferred_element_type=F32)
        o_ref[...] = (z * _sigmoid(zz)).astype(BF16)

    return _row_call(name, body, [u, y0, y1, d.reshape(1, W), wg], ["row", "row", "row", "vec", "vec"],
                     [("row", (rows, W), BF16)], rows, tr)[0]


def glu_bwd(name, u, y0, y1, d, wg, dmix, tr=512):
    rows, W = u.shape
    tr = _pick(rows, tr, HALO)

    def body(u_ref, y0_ref, y1_ref, d_ref, w_ref, do_ref, dy_ref, z_ref, dzz_ref, dd_ref):
        uv = u_ref[...]
        y = d_ref[...] * uv + y0_ref[...] + y1_ref[...]
        z = _gelu(y)
        zz = jnp.dot(z.astype(BF16), w_ref[...], preferred_element_type=F32)
        sg = _sigmoid(zz)
        do = do_ref[...].astype(F32)
        dzz = (do * z * sg * (1.0 - sg)).astype(BF16)
        dz = do * sg + lax.dot_general(dzz, w_ref[...], NT, preferred_element_type=F32)
        dy = dz * _gelu_grad(y)
        dy_ref[...] = dy
        z_ref[...] = z.astype(BF16)
        dzz_ref[...] = dzz
        _acc(dd_ref, jnp.sum(dy * uv, axis=0, keepdims=True))

    do_spec = pl.BlockSpec((tr, W), lambda i: (i, 0))
    return _row_call(name, body, [u, y0, y1, d.reshape(1, W), wg, dmix], ["row", "row", "row", "vec", "vec", do_spec],
                     [("row", (rows, W), F32), ("row", (rows, W), BF16), ("row", (rows, W), BF16), ("acc", (1, W), F32)],
                     rows, tr)


NA_KEYS = NA_WIN_R * GRID_W


NA_PAIRS = NA_WIN_R // 2


def na_bias(rpb):
    H, nr, nc = rpb.shape
    e, ok = _na_col_select()
    rp = jnp.pad(rpb.reshape(H * nr, nc), ((0, (-H * nr) % SUB), (0, LANE - nc)))
    cols = mm_nn("na_bias_mm", rp, jnp.asarray(e, F32), F32, exact=True)[:H * nr]
    tiles = (cols + jnp.asarray(np.where(ok, 0.0, NEG), F32)).reshape(H, nr, GRID_W, GRID_W)
    return jnp.concatenate([tiles[:, :-1], tiles[:, 1:]], axis=-1)


def na_bias_grad(db2):
    H, n2 = db2.shape[:2]
    left, right = db2[..., :GRID_W], db2[..., GRID_W:]
    tiles = jnp.pad(left, ((0, 0), (0, 1), (0, 0), (0, 0))) + jnp.pad(right, ((0, 0), (1, 0), (0, 0), (0, 0)))
    flat = tiles.reshape(H * (n2 + 1), GRID_W * GRID_W)
    flat = jnp.pad(flat, ((0, (-flat.shape[0]) % SUB), (0, 0)))
    dcol = mm_nt("na_bias_fold", flat, na_bias_fold_matrix(), F32, exact=True)
    return dcol[:H * (n2 + 1), :2 * NA_WIN_C - 1].reshape(H, n2 + 1, 2 * NA_WIN_C - 1)


def _na_col_select():
    q = np.arange(GRID_W)
    cs = np.clip(q - NA_WIN_C // 2, 0, GRID_W - NA_WIN_C)
    ok = ((q[None, :] >= cs[:, None]) & (q[None, :] < cs[:, None] + NA_WIN_C)).reshape(-1)
    cidx = np.clip(q[None, :] - q[:, None] + (NA_WIN_C - 1), 0, 2 * NA_WIN_C - 2).reshape(-1)
    return (cidx[None, :] == np.arange(LANE)[:, None]) & ok[None, :], ok


def na_bias_fold_matrix():
    return jnp.asarray(_na_col_select()[0], F32)


def _na_window(r, rows):
    kr0 = jnp.clip(r - NA_WIN_R // 2, 0, rows - NA_WIN_R)
    return pl.multiple_of(kr0 * GRID_W, GRID_W), r - kr0


def _na_dims(qkv, kvc):
    L = qkv.shape[0]
    NA = qkv.shape[1] // 3
    H = NA // NA_HEAD_DIM
    hp = 2 if H % 2 == 0 else 1
    return L, NA, H, hp, H // hp, L // GRID_W, kvc.shape[0]


def _na_bias_tile(b_ref, hh, off):
    return jnp.concatenate([b_ref[hh, NA_WIN_R - 1 - off + 2 * j] for j in range(NA_PAIRS)], axis=-1)


def natten_fwd(name, qkv, kvc, bias, comm=None):
    L, NA, H, hp, G, rows, Lc = _na_dims(qkv, kvc)
    scale = NA_HEAD_DIM ** -0.5
    wd = hp * NA_HEAD_DIM

    def body(q_ref, k_ref, v_ref, kc_ref, vc_ref, b_ref, o_ref, lse_ref):
        st, off = _na_window(pl.program_id(1), rows)
        for hh in range(hp):
            sl = slice(hh * NA_HEAD_DIM, (hh + 1) * NA_HEAD_DIM)
            q = q_ref[:, sl]
            s_loc = (lax.dot_general(q, k_ref[pl.ds(st, NA_KEYS), sl], NT, preferred_element_type=F32) * scale
                     + _na_bias_tile(b_ref, hh, off))
            s_ctx = lax.dot_general(q, kc_ref[:, sl], NT, preferred_element_type=F32) * scale
            m = jnp.maximum(jnp.max(s_loc, axis=-1, keepdims=True), jnp.max(s_ctx, axis=-1, keepdims=True))
            p_loc, p_ctx = jnp.exp(s_loc - m), jnp.exp(s_ctx - m)
            l = jnp.sum(p_loc, axis=-1, keepdims=True) + jnp.sum(p_ctx, axis=-1, keepdims=True)
            o = (jnp.dot(p_loc.astype(BF16), v_ref[pl.ds(st, NA_KEYS), sl], preferred_element_type=F32)
                 + jnp.dot(p_ctx.astype(BF16), vc_ref[:, sl], preferred_element_type=F32))
            o_ref[:, sl] = (o / l).astype(BF16)
            lse_ref[hh] = m + jnp.log(l)

    outs, couts = _call(
        body, [qkv, qkv, qkv, kvc, kvc, bias], name=name, grid=(G, rows),
        in_specs=[pl.BlockSpec((GRID_W, wd), lambda h, r: (r, h)),
                  pl.BlockSpec((L, wd), lambda h, r: (0, G + h)),
                  pl.BlockSpec((L, wd), lambda h, r: (0, 2 * G + h)),
                  pl.BlockSpec((Lc, wd), lambda h, r: (0, h)),
                  pl.BlockSpec((Lc, wd), lambda h, r: (0, G + h)),
                  pl.BlockSpec((hp,) + bias.shape[1:], lambda h, r: (h, 0, 0, 0))],
        out_specs=[pl.BlockSpec((GRID_W, wd), lambda h, r: (r, h)),
                   pl.BlockSpec((hp, GRID_W, 1), lambda h, r: (h, r, 0))],
        out_shape=[jax.ShapeDtypeStruct((L, NA), BF16), jax.ShapeDtypeStruct((H, L, 1), F32)],
        sem=("parallel", "arbitrary"), comm=comm)
    return outs if comm is None else (outs, couts)


def natten_bwd(name, qkv, kvc, bias, o, lse, dmix, comm=None):
    L, NA, H, hp, G, rows, Lc = _na_dims(qkv, kvc)
    scale = NA_HEAD_DIM ** -0.5
    wd = hp * NA_HEAD_DIM

    def body(q_ref, k_ref, v_ref, kc_ref, vc_ref, b_ref, o_ref, lse_ref, do_ref,
             dq_ref, dk_ref, dv_ref, dkc_ref, dvc_ref, db_ref):
        r = pl.program_id(1)
        st, off = _na_window(r, rows)

        @pl.when(r == 0)
        def _():
            for ref in (dk_ref, dv_ref, dkc_ref, dvc_ref, db_ref):
                ref[...] = jnp.zeros_like(ref)

        for hh in range(hp):
            sl = slice(hh * NA_HEAD_DIM, (hh + 1) * NA_HEAD_DIM)
            q, kl, vl, kc, vc = q_ref[:, sl], k_ref[pl.ds(st, NA_KEYS), sl], v_ref[pl.ds(st, NA_KEYS), sl], kc_ref[:, sl], vc_ref[:, sl]
            do = do_ref[:, sl]
            lse_v = lse_ref[hh]
            p_loc = jnp.exp(lax.dot_general(q, kl, NT, preferred_element_type=F32) * scale + _na_bias_tile(b_ref, hh, off) - lse_v)
            p_ctx = jnp.exp(lax.dot_general(q, kc, NT, preferred_element_type=F32) * scale - lse_v)
            delta = jnp.sum(do.astype(F32) * o_ref[:, sl].astype(F32), axis=-1, keepdims=True)
            ds_loc = p_loc * (lax.dot_general(do, vl, NT, preferred_element_type=F32) - delta)
            ds_ctx = p_ctx * (lax.dot_general(do, vc, NT, preferred_element_type=F32) - delta)
            dsl, dsc = ds_loc.astype(BF16), ds_ctx.astype(BF16)
            dq = jnp.dot(dsl, kl, preferred_element_type=F32) + jnp.dot(dsc, kc, preferred_element_type=F32)
            dq_ref[:, sl] = (dq * scale).astype(BF16)
            dk_ref[pl.ds(st, NA_KEYS), sl] += lax.dot_general(dsl, q, TN, preferred_element_type=F32) * scale
            dv_ref[pl.ds(st, NA_KEYS), sl] += lax.dot_general(p_loc.astype(BF16), do, TN, preferred_element_type=F32)
            dkc_ref[:, sl] += lax.dot_general(dsc, q, TN, preferred_element_type=F32) * scale
            dvc_ref[:, sl] += lax.dot_general(p_ctx.astype(BF16), do, TN, preferred_element_type=F32)
            for j in range(NA_PAIRS):
                db_ref[hh, NA_WIN_R - 1 - off + 2 * j] += ds_loc[:, 2 * j * GRID_W:(2 * j + 2) * GRID_W]

    tok = pl.BlockSpec((GRID_W, wd), lambda h, r: (r, h))
    bia = pl.BlockSpec((hp,) + bias.shape[1:], lambda h, r: (h, 0, 0, 0))
    outs, couts = _call(
        body, [qkv, qkv, qkv, kvc, kvc, bias, o, lse, dmix], name=name, grid=(G, rows),
        in_specs=[tok,
                  pl.BlockSpec((L, wd), lambda h, r: (0, G + h)),
                  pl.BlockSpec((L, wd), lambda h, r: (0, 2 * G + h)),
                  pl.BlockSpec((Lc, wd), lambda h, r: (0, h)),
                  pl.BlockSpec((Lc, wd), lambda h, r: (0, G + h)),
                  bia,
                  tok,
                  pl.BlockSpec((hp, GRID_W, 1), lambda h, r: (h, r, 0)),
                  pl.BlockSpec((GRID_W, wd), lambda h, r: (r, G + h))],
        out_specs=[tok,
                   pl.BlockSpec((L, wd), lambda h, r: (0, h)),
                   pl.BlockSpec((L, wd), lambda h, r: (0, h)),
                   pl.BlockSpec((Lc, wd), lambda h, r: (0, h)),
                   pl.BlockSpec((Lc, wd), lambda h, r: (0, h)),
                   bia],
        out_shape=[jax.ShapeDtypeStruct((L, NA), BF16), jax.ShapeDtypeStruct((L, NA), F32), jax.ShapeDtypeStruct((L, NA), F32),
                   jax.ShapeDtypeStruct((Lc, NA), F32), jax.ShapeDtypeStruct((Lc, NA), F32),
                   jax.ShapeDtypeStruct(bias.shape, F32)],
        sem=("parallel", "arbitrary"), comm=comm)
    return outs if comm is None else (outs, couts)


def _s5_dims(T, N):
    TC = T // S5_SEG
    assert T % (S5_SEG * SUB * 2) == 0 and N % S5_STRIP == 0
    return TC, TC // SUB, S5_SEG, N // S5_STRIP


def _s5_backward(d, rev):
    return (d == 1) != rev


def s5_scan(name, xin, mats, a, rev):
    _, T, W = xin.shape
    N = a.shape[-1]
    TC, NG, NCH, NS = _s5_dims(T, N)
    CW, SL = W // NS, S5_STRIP

    def ck(d, k):
        return jnp.where(_s5_backward(d, rev), NCH - 1 - k, k)

    def body(x_ref, m_ref, a_ref, h_ref, f_ref, carry, hs):
        @pl.when(pl.program_id(2) == 0)
        def _():
            carry[...] = jnp.zeros_like(carry)

        xb = x_ref[...].astype(BF16)
        hs[0] = jnp.dot(xb, m_ref[0], preferred_element_type=F32)
        hs[1] = jnp.dot(xb, m_ref[1], preferred_element_type=F32)
        ar, ai = jnp.broadcast_to(a_ref[0], (SUB, SL)), jnp.broadcast_to(a_ref[1], (SUB, SL))
        bw = _s5_backward(pl.program_id(0), rev)

        def step(t, c):
            hr, hi = c
            row = pl.multiple_of(jnp.where(bw, NG - 1 - t, t) * SUB, SUB)
            nr = ar * hr - ai * hi + hs[0, pl.ds(row, SUB), :]
            ni = ar * hi + ai * hr + hs[1, pl.ds(row, SUB), :]
            hs[0, pl.ds(row, SUB), :] = nr
            hs[1, pl.ds(row, SUB), :] = ni
            return nr, ni

        hr, hi = lax.fori_loop(0, NG, step, (carry[0], carry[1]))
        carry[0], carry[1] = hr, hi
        f_ref[0], f_ref[1] = hr, hi
        h_ref[...] = hs[...].astype(BF16)

    return pl.pallas_call(
        body, name=name, grid=(2, NS, NCH),
        in_specs=[pl.BlockSpec((None, TC, CW), lambda d, j, k: (d, ck(d, k), j)),
                  pl.BlockSpec((None, 2, None, CW, SL), lambda d, j, k: (d, 0, j, 0, 0)),
                  pl.BlockSpec((None, 2, 1, SL), lambda d, j, k: (d, 0, 0, j))],
        out_specs=[pl.BlockSpec((None, 2, TC, SL), lambda d, j, k: (d, 0, ck(d, k), j)),
                   pl.BlockSpec((None, 2, SUB, SL), lambda d, j, k: (d, 0, 0, j))],
        out_shape=[jax.ShapeDtypeStruct((2, 2, T, N), BF16), jax.ShapeDtypeStruct((2, 2, SUB, N), F32)],
        scratch_shapes=[pltpu.VMEM((2, SUB, SL), F32), pltpu.VMEM((2, TC, SL), F32)],
        compiler_params=_params("parallel", "parallel", "arbitrary"),
    )(xin, mats, a)


def s5_fix(name, hloc, hin, a, mats, rev):
    _, _, T, N = hloc.shape
    TC, NG, NCH, NS = _s5_dims(T, N)
    SL = S5_STRIP
    CW = mats.shape[-1]

    def ck(d, k):
        return jnp.where(_s5_backward(d, rev), NCH - 1 - k, k)

    def body(h_ref, hin_ref, a_ref, m_ref, ho_ref, y_ref, g, hs):
        @pl.when(pl.program_id(2) == 0)
        def _():
            g[...] = hin_ref[...]

        hs[...] = h_ref[...].astype(F32)
        ar, ai = jnp.broadcast_to(a_ref[0], (SUB, SL)), jnp.broadcast_to(a_ref[1], (SUB, SL))
        bw = _s5_backward(pl.program_id(0), rev)

        def step(t, c):
            gr, gi = c
            row = pl.multiple_of(jnp.where(bw, NG - 1 - t, t) * SUB, SUB)
            nr = ar * gr - ai * gi
            ni = ar * gi + ai * gr
            hs[0, pl.ds(row, SUB), :] += nr
            hs[1, pl.ds(row, SUB), :] += ni
            return nr, ni

        gr, gi = lax.fori_loop(0, NG, step, (g[0], g[1]))
        g[0], g[1] = gr, gi
        hb = hs[...].astype(BF16)
        ho_ref[...] = hb
        y_ref[...] = (jnp.dot(hb[0], m_ref[0], preferred_element_type=F32)
                      + jnp.dot(hb[1], m_ref[1], preferred_element_type=F32))

    return pl.pallas_call(
        body, name=name, grid=(2, NS, NCH),
        in_specs=[pl.BlockSpec((None, 2, TC, SL), lambda d, j, k: (d, 0, ck(d, k), j)),
                  pl.BlockSpec((None, 2, SUB, SL), lambda d, j, k: (d, 0, 0, j)),
                  pl.BlockSpec((None, 2, 1, SL), lambda d, j, k: (d, 0, 0, j)),
                  pl.BlockSpec((None, 2, None, SL, CW), lambda d, j, k: (d, 0, j, 0, 0))],
        out_specs=[pl.BlockSpec((None, 2, TC, SL), lambda d, j, k: (d, 0, ck(d, k), j)),
                   pl.BlockSpec((None, TC, CW), lambda d, j, k: (d, ck(d, k), j))],
        out_shape=[jax.ShapeDtypeStruct((2, 2, T, N), BF16), jax.ShapeDtypeStruct((2, T, NS * CW), F32)],
        scratch_shapes=[pltpu.VMEM((2, SUB, SL), F32), pltpu.VMEM((2, TC, SL), F32)],
        compiler_params=_params("parallel", "parallel", "arbitrary"),
    )(hloc, hin, a, mats)


def s5_grads(name, g, h, u, dy):
    _, _, T, N = g.shape
    W = u.shape[-1]
    TC, NG, NCH, NS = _s5_dims(T, N)
    CW, SL = W // NS, S5_STRIP

    def body(g_ref, h_ref, hp_ref, hl_ref, u_ref, dy_ref, dm_ref, dc_ref, da_ref, hs):
        k = pl.program_id(2)
        sub = lax.broadcasted_iota(jnp.int32, (SUB, SL), 0)

        hf = h_ref[...].astype(F32)

        @pl.when(pl.program_id(0) == 0)
        def _():
            for z in range(2):
                wrapped = jnp.where(sub == 0, 0.0, pltpu.roll(hl_ref[z].astype(F32)[SUB:], 1, 0))
                hs[z, 0:SUB, :] = jnp.where(k == 0, wrapped, hp_ref[z].astype(F32)[SUB:])
                hs[z, SUB:TC, :] = hf[z, 0:TC - SUB]

        @pl.when(pl.program_id(0) == 1)
        def _():
            for z in range(2):
                wrapped = jnp.where(sub == SUB - 1, 0.0, pltpu.roll(hl_ref[z].astype(F32)[:SUB], SUB - 1, 0))
                hs[z, TC - SUB:TC, :] = jnp.where(k == NCH - 1, wrapped, hp_ref[z].astype(F32)[:SUB])
                hs[z, 0:TC - SUB, :] = hf[z, SUB:TC]

        gr, gi, pr, pi = g_ref[0].astype(F32), g_ref[1].astype(F32), hs[0], hs[1]
        dar = jnp.sum((gr * pr + gi * pi).reshape(NG, SUB, SL), axis=0)
        dai = jnp.sum((gi * pr - gr * pi).reshape(NG, SUB, SL), axis=0)
        ub, dyb = u_ref[...].astype(BF16), dy_ref[...].astype(BF16)
        dm = [lax.dot_general(ub, g_ref[z], TN, preferred_element_type=F32) for z in range(2)]
        dc = [lax.dot_general(dyb, h_ref[z], TN, preferred_element_type=F32) for z in range(2)]

        @pl.when(k == 0)
        def _():
            da_ref[0], da_ref[1] = dar, dai
            for z in range(2):
                dm_ref[z], dc_ref[z] = dm[z], dc[z]

        @pl.when(k > 0)
        def _():
            da_ref[0] += dar
            da_ref[1] += dai
            for z in range(2):
                dm_ref[z] += dm[z]
                dc_ref[z] += dc[z]

    big = pl.BlockSpec((None, 2, TC, SL), lambda d, j, k: (d, 0, k, j))
    tok = pl.BlockSpec((None, TC, CW), lambda d, j, k: (d, k, j))
    mat = pl.BlockSpec((None, 2, None, CW, SL), lambda d, j, k: (d, 0, j, 0, 0))
    return pl.pallas_call(
        body, name=name, grid=(2, NS, NCH),
        in_specs=[big, big,
                  pl.BlockSpec((None, 2, 2 * SUB, SL), lambda d, j, k: (
                      d, 0, jnp.where(d == 0, jnp.maximum(k * NG - 1, 0), jnp.minimum((k + 1) * NG, T // SUB - 1)) // 2, j)),
                  pl.BlockSpec((None, 2, 2 * SUB, SL), lambda d, j, k: (d, 0, jnp.where(d == 0, T // SUB - 1, 0) // 2, j)),
                  tok, tok],
        out_specs=[mat, mat, pl.BlockSpec((None, 2, SUB, SL), lambda d, j, k: (d, 0, 0, j))],
        out_shape=[jax.ShapeDtypeStruct((2, 2, NS, CW, SL), F32), jax.ShapeDtypeStruct((2, 2, NS, CW, SL), F32),
                   jax.ShapeDtypeStruct((2, 2, SUB, N), F32)],
        scratch_shapes=[pltpu.VMEM((2, TC, SL), F32)],
        compiler_params=_params("parallel", "parallel", "arbitrary"),
    )(g, h, h, h, u, dy)


def _interleave(seq):
    *lead, T, W = seq.shape
    n = len(lead)
    return seq.reshape(*lead, S5_SEG, T // S5_SEG, W).swapaxes(n, n + 1).reshape(*lead, T, W)


def _deinterleave(seq):
    *lead, T, W = seq.shape
    n = len(lead)
    return seq.reshape(*lead, T // S5_SEG, S5_SEG, W).swapaxes(n, n + 1).reshape(*lead, T, W)


def _s5_discretize(lam_re, lam_im, log_dt, b_re, b_im):
    dt = jnp.exp(log_dt)[..., None]
    mag = jnp.exp(lam_re * dt)
    a_re = mag * jnp.cos(lam_im * dt)
    a_im = mag * jnp.sin(lam_im * dt)
    den = jnp.square(lam_re) + jnp.square(lam_im)
    f_re = ((a_re - 1.0) * lam_re + a_im * lam_im) / den
    f_im = (a_im * lam_re - (a_re - 1.0) * lam_im) / den
    bb_re = f_re[..., None] * b_re - f_im[..., None] * b_im
    bb_im = f_re[..., None] * b_im + f_im[..., None] * b_re
    return a_re, a_im, bb_re, bb_im


_GPS = S5_STRIP // SSM_STATE


def _blockdiag(t):
    d2, G, P, Cg = t.shape
    t5 = t.reshape(d2, G // _GPS, _GPS, P, Cg).transpose(0, 1, 2, 4, 3)
    m = t5[:, :, :, :, None, :] * jnp.eye(_GPS, dtype=t.dtype)[None, None, :, None, :, None]
    return m.reshape(d2, G // _GPS, _GPS * Cg, _GPS * P)


def _blockdiag_extract(m, Cg, P):
    d2, NS = m.shape[:2]
    m6 = m.reshape(d2, NS, _GPS, Cg, _GPS, P)
    diag = jnp.stack([m6[:, :, i, :, i, :] for i in range(_GPS)], axis=2)
    return diag.transpose(0, 1, 2, 4, 3).reshape(d2, NS * _GPS, P, Cg)


def _cmul(a, b):
    return a[0] * b[0] - a[1] * b[1], a[0] * b[1] + a[1] * b[0]


def _cpow(a, n):
    out, base = None, a
    while n:
        if n & 1:
            out = base if out is None else _cmul(out, base)
        base = _cmul(base, base)
        n >>= 1
    return out


def _segment_carry(fin, apow, rev):
    per_dir = []
    for d in range(2):
        fr, fi = fin[d, 0], fin[d, 1]
        ap = (apow[0][d], apow[1][d])
        cr = ci = jnp.zeros_like(fr[0:1])
        outs = [None] * S5_SEG
        backward = (d == 1) != rev
        for s in (range(S5_SEG - 1, -1, -1) if backward else range(S5_SEG)):
            outs[s] = (cr, ci)
            pr, pi = _cmul(ap, (cr, ci))
            cr, ci = pr + fr[s:s + 1], pi + fi[s:s + 1]
        per_dir.append(jnp.stack([jnp.concatenate([o[0] for o in outs]), jnp.concatenate([o[1] for o in outs])]))
    return jnp.stack(per_dir)


def _coords():
    x, y, c = lax.axis_index("x"), lax.axis_index("y"), lax.axis_index("c")
    others = [(1 - x, y), (x, 1 - y), (1 - x, 1 - y)]
    return x, y, c, 2 * x + y, others


def _comm(name, ins, out_shapes, aliases, n_local, n_remote, plan):
    n_in, n_out = len(ins), len(out_shapes)

    def body(*refs):
        in_refs, out_refs = refs[:n_in], refs[n_in:n_in + n_out]
        send_sems, recv_sems, local_sems = refs[n_in + n_out:]
        x, y, c = lax.axis_index("x"), lax.axis_index("y"), lax.axis_index("c")
        locs, sends, lands = plan(in_refs, out_refs)
        assert len(locs) == n_local and len(sends) == n_remote and len(lands) == n_remote
        local = [pltpu.make_async_copy(s, d, local_sems.at[i]) for i, (s, d) in enumerate(locs)]
        for cp in local:
            cp.start()
        remote = [pltpu.make_async_remote_copy(src_ref=s, dst_ref=d, send_sem=send_sems.at[i], recv_sem=recv_sems.at[i],
                                               device_id=peer, device_id_type=MESH)
                  for i, (s, d, peer) in enumerate(sends)]
        for cp in remote:
            cp.start()
        for i, d in enumerate(lands):
            pltpu.make_async_remote_copy(src_ref=d, dst_ref=d, send_sem=send_sems.at[i], recv_sem=recv_sems.at[i],
                                         device_id=(x, y, c), device_id_type=MESH).wait_recv()
        for cp in remote:
            cp.wait_send()
        for cp in local:
            cp.wait()

    any_spec = pl.BlockSpec(memory_space=pl.ANY)
    return pl.pallas_call(
        body, name=name,
        in_specs=[any_spec] * n_in, out_specs=[any_spec] * n_out,
        out_shape=[jax.ShapeDtypeStruct(s, d) for s, d in out_shapes],
        input_output_aliases=aliases,
        scratch_shapes=[pltpu.SemaphoreType.DMA((n_remote,)), pltpu.SemaphoreType.DMA((n_remote,)),
                        pltpu.SemaphoreType.DMA((max(n_local, 1),))],
        compiler_params=pltpu.CompilerParams(has_side_effects=True),
    )(*ins)


def allgather_dev(name, v):
    M, Nc = v.shape

    def plan(in_refs, out_refs):
        (v_ref,), (o_ref,) = in_refs, out_refs
        x, y, c = lax.axis_index("x"), lax.axis_index("y"), lax.axis_index("c")

        def rows(px, py, pc):
            return o_ref.at[pl.ds((4 * px + 2 * py + pc) * M, M), :]

        peers = [(x ^ fx, y ^ fy, c ^ fc) for fx in (0, 1) for fy in (0, 1) for fc in (0, 1) if fx or fy or fc]
        return ([(v_ref, rows(x, y, c))],
                [(v_ref, rows(x, y, c), p) for p in peers],
                [rows(*p) for p in peers])

    return _comm(name, [v], [((N_DEV * M, Nc), v.dtype)], {}, 1, N_DEV - 1, plan)[0]


def allgather_chips_1(name, shards):
    def plan(in_refs, out_refs):
        x, y, c, chip, others = _coords()
        sends, lands = [], []
        for s_ref, g_ref in zip(in_refs, out_refs):
            hr = s_ref.shape[0] // 2
            mine = pl.ds(c * hr, hr)
            for qx, qy in others:
                sends.append((s_ref.at[mine], g_ref.at[chip, mine], (qx, qy, c)))
                lands.append(g_ref.at[2 * qx + qy, mine])
        return [], sends, lands

    n = len(shards)
    comm = (list(shards), [((N_CHIP,) + s.shape, s.dtype) for s in shards], {}, 3 * n, plan)
    return comm if name is None else _comm(name, comm[0], comm[1], comm[2], 0, comm[3], comm[4])


def allgather_chips_2(name, gathered, shards):
    n = len(gathered)

    def plan(in_refs, out_refs):
        x, y, c, chip, others = _coords()
        sends, lands = [], []
        for s_ref, g_ref in zip(in_refs[n:], out_refs):
            hr = g_ref.shape[1] // 2
            for qx, qy in others:
                q = 2 * qx + qy
                sends.append((g_ref.at[q, pl.ds(c * hr, hr)], g_ref.at[q, pl.ds(c * hr, hr)], (x, y, 1 - c)))
                lands.append(g_ref.at[q, pl.ds((1 - c) * hr, hr)])
            sends.append((s_ref, g_ref.at[chip], (x, y, 1 - c)))
            lands.append(g_ref.at[chip])
        return [], sends, lands

    return _comm(name, list(gathered) + list(shards), [(g.shape, g.dtype) for g in gathered], {i: i for i in range(n)},
                 0, 4 * n, plan)


def reduce_1(name, grads):
    def plan(in_refs, out_refs):
        x, y, c, chip, others = _coords()
        sends, lands = [], []
        for g_ref, got_ref in zip(in_refs, out_refs):
            hr = g_ref.shape[1] // 2
            sends.append((g_ref.at[:, pl.ds((1 - c) * hr, hr), :], got_ref, (x, y, 1 - c)))
            lands.append(got_ref)
        return [], sends, lands

    n = len(grads)
    return _comm(name, grads, [((g.shape[0], g.shape[1] // 2, g.shape[2]), g.dtype) for g in grads], {}, 0, n, plan)


def reduce_2(name, parts):
    def plan(in_refs, out_refs):
        x, y, c, chip, others = _coords()
        sends, lands = [], []
        for t_ref, q_ref in zip(in_refs, out_refs):
            for qx, qy in others:
                sends.append((t_ref.at[2 * qx + qy], q_ref.at[chip], (qx, qy, c)))
                lands.append(q_ref.at[2 * qx + qy])
        return [], sends, lands

    n = len(parts)
    comm = (list(parts), [(p.shape, p.dtype) for p in parts], {}, 3 * n, plan)
    return comm if name is None else _comm(name, comm[0], comm[1], comm[2], 0, comm[3], comm[4])


def share_slots(name, slots):
    def plan(in_refs, out_refs):
        x, y, c, chip, others = _coords()
        (q_ref,) = out_refs
        return ([], [(q_ref.at[chip], q_ref.at[chip], (qx, qy, c)) for qx, qy in others],
                [q_ref.at[2 * qx + qy] for qx, qy in others])

    return _comm(name, [slots], [(slots.shape, slots.dtype)], {0: 0}, 0, N_CHIP - 1, plan)[0]


def allreduce_small(tag, buf, ids):
    got = reduce_1(tag + "_1", [buf[None]])[0][0]
    slots = share_slots(tag + "_2", pair_sum_to_slot(tag + "_add", buf, got, ids))
    full = reduce_3(tag + "_3", [sum_chips_to_half(tag + "_sum", slots, ids)])[0]
    return full.reshape(buf.shape)


def reduce_3(name, fulls):
    def plan(in_refs, out_refs):
        x, y, c, chip, others = _coords()
        sends, lands = [], []
        for o_ref in out_refs:
            sends.append((o_ref.at[c], o_ref.at[c], (x, y, 1 - c)))
            lands.append(o_ref.at[1 - c])
        return [], sends, lands

    n = len(fulls)
    return _comm(name, fulls, [(f.shape, f.dtype) for f in fulls], {i: i for i in range(n)}, 0, n, plan)


_WEIGHTS = ['c_ctx', 'w_mod', 'b_mod', 'g_mix', 'g_ffn', 'w_in', 'ssm_lam_re', 'ssm_lam_im', 'ssm_log_dt', 'ssm_b_re',
            'ssm_b_im', 'ssm_c_re', 'ssm_c_im', 'ssm_d', 'ssm_w_glu', 'na_rpb', 'w_out', 'cv_w_pw1', 'cv_dw_w', 'cv_dw_b',
            'cv_ln_g', 'cv_ln_b', 'cv_w_pw2', 'ffn_w_up', 'ffn_conv_w', 'ffn_conv_b', 'ffn_w_down', 'g_out']
_INPUTS = ['x', 'c', 'ctx'] + _WEIGHTS + ['loss_target'] + ['m_' + w for w in _WEIGHTS] + ['v_' + w for w in _WEIGHTS]
_GATHERED_SMALL = ['ffn_conv_w', 'cv_dw_w', 'cv_dw_b', 'cv_ln_g', 'cv_ln_b']


def _silu(v):
    return v * jax.nn.sigmoid(v)


def _pack(arrs, cols, row_mult=SUB):
    flat = jnp.concatenate([a.reshape(-1).astype(F32) for a in arrs])
    n = flat.shape[0]
    unit = row_mult * cols
    flat = jnp.pad(flat, (0, (-n) % unit))
    return flat.reshape(-1, cols)


def _unpack(buf, shapes):
    flat = buf.reshape(-1)
    out, o = [], 0
    for s in shapes:
        n = int(np.prod(s))
        out.append(flat[o:o + n].reshape(s))
        o += n
    return out


def _ffn_fwd(tag, xin, sh, sc, gt, g, wup, cw3, cb3, wdn, comm_up=None, comm_mid=None, comm_down=None):
    hf = norm_mod_fwd(tag + "_norm", xin, g * (1.0 + sc), sh)
    up3 = mm_nn_pieces(tag + "_up", hf, wup, 0, N_CHIP, BF16, halves=2, comm=comm_up)
    up3, got_up = up3 if comm_up is not None else (up3, [])
    act = ffn_mid_fwd(tag + "_mid", up3, cw3, cb3, comm=comm_mid)
    act, got_mid = act if comm_mid is not None else (act, [])
    yf = mm_nn(tag + "_down", act, wdn, BF16, comm=comm_down)
    yf, got_down = yf if comm_down is not None else (yf, [])
    return gate_res_fwd(tag + "_res", xin, yf, gt), (xin, hf, up3, act, yf), got_up, got_mid, got_down


def _ffn_bwd(tag, dxo, saved, sc, gt, g, wup, cw3, cb3, wdn, comm_mid=None):
    xin, hf, up3, act, yf = saved
    dyf, dgt = gate_res_bwd(tag + "_res_b", dxo, yf, gt)
    dact = mm_nt(tag + "_down_bx", dyf, wdn, BF16)
    dwdn = mm_tn(tag + "_down_bw", act, dyf, BF16)
    mid = ffn_mid_bwd(tag + "_mid_b", up3, dact, cw3, cb3, comm=comm_mid)
    (dup3, dcw3, dcb3), got_mid = mid if comm_mid is not None else (mid, [])
    dhf = mm_nt_pieces(tag + "_up_bx", dup3, wup, BF16, halves=2)
    dwup = mm_tn_pieces(tag + "_up_bw", hf, dup3, N_CHIP, BF16, halves=2)
    dxi, cs1, cs2 = norm_mod_bwd(tag + "_norm_b", xin, dhf, g * (1.0 + sc), dxo)
    return dxi, dict(dsh=cs1[0], dsc=cs2[0] * g, dgt=dgt[0], dg=cs2[0] * (1.0 + sc), dwup=dwup, dwdn=dwdn,
                     dcw=dcw3.transpose(1, 0, 2).reshape(3, -1), dcb=dcb3.reshape(-1)), got_mid


def kernel(x, c, ctx, c_ctx, w_mod, b_mod, g_mix, g_ffn, w_in, ssm_lam_re, ssm_lam_im, ssm_log_dt, ssm_b_re, ssm_b_im, ssm_c_re, ssm_c_im, ssm_d, ssm_w_glu, na_rpb, w_out, cv_w_pw1, cv_dw_w, cv_dw_b, cv_ln_g, cv_ln_b, cv_w_pw2, ffn_w_up, ffn_conv_w, ffn_conv_b, ffn_w_down, g_out, loss_target, m_c_ctx, m_w_mod, m_b_mod, m_g_mix, m_g_ffn, m_w_in, m_ssm_lam_re, m_ssm_lam_im, m_ssm_log_dt, m_ssm_b_re, m_ssm_b_im, m_ssm_c_re, m_ssm_c_im, m_ssm_d, m_ssm_w_glu, m_na_rpb, m_w_out, m_cv_w_pw1, m_cv_dw_w, m_cv_dw_b, m_cv_ln_g, m_cv_ln_b, m_cv_w_pw2, m_ffn_w_up, m_ffn_conv_w, m_ffn_conv_b, m_ffn_w_down, m_g_out, v_c_ctx, v_w_mod, v_b_mod, v_g_mix, v_g_ffn, v_w_in, v_ssm_lam_re, v_ssm_lam_im, v_ssm_log_dt, v_ssm_b_re, v_ssm_b_im, v_ssm_c_re, v_ssm_c_im, v_ssm_d, v_ssm_w_glu, v_na_rpb, v_w_out, v_cv_w_pw1, v_cv_dw_w, v_cv_dw_b, v_cv_ln_g, v_cv_ln_b, v_cv_w_pw2, v_ffn_w_up, v_ffn_conv_w, v_ffn_conv_b, v_ffn_w_down, v_g_out):
    p = dict(locals())
    xi, yi, ci = lax.axis_index("x"), lax.axis_index("y"), lax.axis_index("c")
    me, chip = 4 * xi + 2 * yi + ci, 2 * xi + yi
    xs, cx, tgt = x[0], ctx[0], loss_target[0]
    L, D = xs.shape
    Lc = cx.shape[0]
    T = L + Lc
    W = D // 2
    Cq = w_mod.shape[2]

    s_mix = [t.astype(BF16) for t in (w_in[0], ssm_w_glu[0], w_out[0])]
    s_ffn0 = [t.astype(BF16) for t in (ffn_w_up[0], ffn_w_down[0])]
    s_conv = [t.astype(BF16) for t in (cv_w_pw1[0], cv_w_pw2[0])]
    s_ffn1 = [t.astype(BF16) for t in (ffn_w_up[1], ffn_w_down[1])]
    Win, Wglu, Wout = allgather_chips_2("gather_mix_2", allgather_chips_1("gather_mix_1", s_mix), s_mix)
    Wglu, Wout = Wglu.reshape(-1, Wglu.shape[-1]), Wout.reshape(-1, D)
    Fd = ffn_w_down.shape[1] * N_CHIP
    c_idx = jnp.reshape(ci, (1,)).astype(jnp.int32)
    ids = jnp.stack([chip, ci]).astype(jnp.int32)

    def reduce_front(tag, grads):
        got = reduce_1("reduce_%s_1" % tag, grads)
        return [add_half("reduce_%s_add%d" % (tag, i), g, r, c_idx) for i, (g, r) in enumerate(zip(grads, got))]

    small_shapes = [p[n].shape for n in _GATHERED_SMALL]
    sm = allgather_dev("gather_small", _pack([p[n] for n in _GATHERED_SMALL], 1024))
    sm = sm.reshape(N_DEV, -1)[0::2]
    per_chip = [_unpack(sm[q], small_shapes) for q in range(N_CHIP)]
    conv_w_f, dw_w_f, dw_b_f, ln_g_f, ln_b_f = (jnp.concatenate([pc[i] for pc in per_chip], axis=-1)
                                                for i in range(len(_GATHERED_SMALL)))
    cw3 = [conv_w_f[l].reshape(3, 2, Fd).transpose(1, 0, 2) for l in range(2)]
    cb3 = [ffn_conv_b[l].reshape(2, 1, Fd) for l in range(2)]
    dw_w_f, dw_b_f, ln_g_f, ln_b_f = dw_w_f[0], dw_b_f[0], ln_g_f[0], ln_b_f[0]

    c_all = allgather_dev("gather_c", jnp.zeros((SUB, D), F32).at[0].set(c[0])).reshape(N_DEV, SUB, D)[:, 0]
    S16 = jnp.concatenate([_silu(c_all), _silu(c_ctx)[None], jnp.zeros((2 * SUB - N_DEV - 1, D), F32)])
    modp = mm_nn_pieces("mod_fwd", S16, w_mod, 0, 2, F32)
    modg = allgather_dev("gather_mod", modp).reshape(N_DEV, 2 * SUB, 2, Cq)[0::2]
    mod_full = modg.transpose(2, 1, 0, 3).reshape(2, 2 * SUB, N_CHIP * Cq) + b_mod[:, None, :]
    mod_me = lax.dynamic_index_in_dim(mod_full, me, axis=1, keepdims=False)
    mods = [[mod_me[l, i * D:(i + 1) * D] for i in range(N_MOD)] for l in range(2)]
    shc, scc = mod_full[0, N_DEV, :D], mod_full[0, N_DEV, D:2 * D]

    sh_m, sc_m, gt_m, sh_f, sc_f, gt_f = mods[0]
    h0 = norm_mod_fwd("l0_norm", xs, g_mix[0] * (1.0 + sc_m), sh_m)
    hc0 = norm_mod_fwd("l0_norm_c", cx, g_mix[0] * (1.0 + scc), shc)
    u = mm_nn_pieces("l0_in_u", h0, Win, 0, 1, F32)
    qkv = mm_nn_pieces("l0_in_qkv", h0, Win, 1, 3, BF16)
    uc = mm_nn_pieces("l0_in_uc", hc0, Win, 0, 1, F32)
    kvc = mm_nn_pieces("l0_in_kvc", hc0, Win, 2, 2, BF16)

    lam_re, lam_im, log_dt = ssm_lam_re[0], ssm_lam_im[0], ssm_log_dt[0]
    b_re, b_im, c_re, c_im = ssm_b_re[0], ssm_b_im[0], ssm_c_re[0], ssm_c_im[0]
    (a_re, a_im, bb_re, bb_im), disc_vjp = jax.vjp(_s5_discretize, lam_re, lam_im, log_dt, b_re, b_im)
    G, P, Cg = bb_re.shape[1:]
    N = G * P
    a_re, a_im = a_re.reshape(2, 1, N), a_im.reshape(2, 1, N)
    a_f, a_b = jnp.stack([a_re, a_im], axis=1), jnp.stack([a_re, -a_im], axis=1)
    Bblk = jnp.stack([_blockdiag(bb_re), _blockdiag(bb_im)], axis=1)
    Cblk = jnp.stack([_blockdiag(c_re.swapaxes(-1, -2)), -_blockdiag(c_im.swapaxes(-1, -2))], axis=1)
    apow = _cpow((a_re, a_im), T // S5_SEG)

    useq = _interleave(jnp.stack([jnp.concatenate([uc, u]), jnp.concatenate([u, uc])]))
    hloc, fin = s5_scan("s5_scan", useq, Bblk.astype(BF16), a_f, rev=False)
    hst, yseq = s5_fix("s5_fix", hloc, _segment_carry(fin, apow, False), a_f, Cblk.swapaxes(-1, -2).astype(BF16), rev=False)
    ys = _deinterleave(yseq)
    y0, y1 = ys[0, Lc:], ys[1, :L]
    s5o = glu_fwd("s5_glu", u, y0, y1, ssm_d[0], Wglu)

    bias = na_bias(na_rpb[0])
    (o_na, lse), g_ffn0 = natten_fwd("na_fwd", qkv, kvc, bias, comm=allgather_chips_1(None, s_ffn0))
    Wup0, Wdn0 = allgather_chips_2("gather_ffn0_2", g_ffn0, s_ffn0)
    mixcat = jnp.concatenate([s5o, o_na], axis=1)
    ymix = mm_nn("l0_out", mixcat, Wout, BF16)
    x1 = gate_res_fwd("l0_res", xs, ymix, gt_m)
    x2, ffn0, g_up1, g_dn1, g_conv = _ffn_fwd(
        "f0", x1, sh_f, sc_f, gt_f, g_ffn[0], Wup0, cw3[0], cb3[0], Wdn0.reshape(-1, D),
        comm_up=allgather_chips_1(None, s_ffn1[:1]), comm_mid=allgather_chips_1(None, s_ffn1[1:]),
        comm_down=allgather_chips_1(None, s_conv))
    Wpw1, Wpw2, Wup1, Wdn1 = allgather_chips_2("gather_l1_2", g_conv + g_up1 + g_dn1, s_conv + s_ffn1)
    Wpw2 = Wpw2.reshape(-1, D)
    Wup, Wdn = [Wup0, Wup1], [Wdn0.reshape(-1, D), Wdn1.reshape(-1, D)]

    sh_v, sc_v, gt_v, sh_g, sc_g, gt_g = mods[1]
    hcv = norm_mod_fwd("l1_norm", x2, g_mix[1] * (1.0 + sc_v), sh_v)
    ag3 = mm_nn_pieces("l1_pw1", hcv, Wpw1, 0, N_CHIP, BF16, halves=2)
    z1, z3 = conf_mid_fwd("l1_mid", ag3, dw_w_f, dw_b_f, ln_g_f, ln_b_f)
    ycv = mm_nn("l1_pw2", z3, Wpw2, BF16)
    x3 = gate_res_fwd("l1_res", x2, ycv, gt_v)
    x4, ffn1, _, _, _ = _ffn_fwd("f1", x3, sh_g, sc_g, gt_g, g_ffn[1], Wup[1], cw3[1], cb3[1], Wdn[1])

    dx4, dg_out, loss_part = loss_head("loss", x4, g_out, tgt)
    loss = lax.psum(loss_part[0, 0], ("x", "y", "c"))

    dx3, gf1, _ = _ffn_bwd("f1", dx4, ffn1, sc_g, gt_g, g_ffn[1], Wup[1], cw3[1], cb3[1], Wdn[1])
    parts_ffn1 = reduce_front("ffn1", [gf1["dwup"], gf1["dwdn"].reshape(N_CHIP, -1, D)])
    dycv, dgt_v = gate_res_bwd("l1_res_b", dx3, ycv, gt_v)
    dz3 = mm_nt("l1_pw2_bx", dycv, Wpw2, BF16)
    dWpw2 = mm_tn("l1_pw2_bw", z3, dycv, BF16)
    dz1, dln_g, dln_b = conf_ln_bwd("l1_ln_b", z1, dz3, ln_g_f, ln_b_f)
    (dag3, ddw_w, ddw_b), slots_ffn1 = conf_conv_bwd("l1_conv_b", ag3, dz1, dw_w_f, comm=reduce_2(None, parts_ffn1))
    dhcv = mm_nt_pieces("l1_pw1_bx", dag3, Wpw1, BF16, halves=2)
    dWpw1 = mm_tn_pieces("l1_pw1_bw", hcv, dag3, N_CHIP, BF16, halves=2)
    dx2, cs1_v, cs2_v = norm_mod_bwd("l1_norm_b", x2, dhcv, g_mix[1] * (1.0 + sc_v), dx3)
    parts_conv = reduce_front("conv", [dWpw1, dWpw2.reshape(N_CHIP, -1, D)])

    dx1, gf0, slots_conv = _ffn_bwd("f0", dx2, ffn0, sc_f, gt_f, g_ffn[0], Wup[0], cw3[0], cb3[0], Wdn[0],
                                    comm_mid=reduce_2(None, parts_conv))
    parts_ffn0 = reduce_front("ffn0", [gf0["dwup"], gf0["dwdn"].reshape(N_CHIP, -1, D)])
    dymix, dgt_m = gate_res_bwd("l0_res_b", dx1, ymix, gt_m)
    dmix = mm_nt("l0_out_bx", dymix, Wout, BF16)
    dWout = mm_tn("l0_out_bw", mixcat, dymix, BF16)
    (dq, dk, dv, dkc, dvc, dbias), slots_ffn0 = natten_bwd("na_bwd", qkv, kvc, bias, o_na, lse, dmix,
                                                           comm=reduce_2(None, parts_ffn0))
    dy, zg, dzz, dd_skip = glu_bwd("s5_glu_b", u, y0, y1, ssm_d[0], Wglu, dmix)
    dWglu = mm_tn("s5_glu_bw", zg, dzz, BF16)

    zc = jnp.zeros((Lc, W), F32)
    dyseq = _interleave(jnp.stack([jnp.concatenate([zc, dy]), jnp.concatenate([dy, zc])]))
    gloc, gfin = s5_scan("s5_scan_b", dyseq, Cblk.astype(BF16), a_b, rev=True)
    apow_b = (apow[0], -apow[1])
    gst, duseq = s5_fix("s5_fix_b", gloc, _segment_carry(gfin, apow_b, True), a_b, Bblk.swapaxes(-1, -2).astype(BF16), rev=True)
    dBm, dCm, da8 = s5_grads("s5_grads", gst, hst, useq, dyseq)
    dus = _deinterleave(duseq)
    du = fma3("s5_du", dy, dus[0, Lc:], dus[1, :L], ssm_d[0], BF16)
    duc = dus[0, :Lc] + dus[1, L:]

    d_in = jnp.concatenate([du, dq, dk.astype(BF16), dv.astype(BF16)], axis=1)
    d_in_c = jnp.concatenate([duc.astype(BF16), jnp.zeros((Lc, W), BF16), dkc.astype(BF16), dvc.astype(BF16)], axis=1)
    dh0 = mm_nt_pieces("l0_in_bx", d_in, Win, BF16)
    dhc0 = mm_nt_pieces("l0_in_bxc", d_in_c, Win, BF16)
    dWin = mm_tn_pieces("l0_in_bw", jnp.concatenate([hc0, h0]), jnp.concatenate([d_in_c, d_in]), N_CHIP, BF16)
    dx0, cs1_m, cs2_m = norm_mod_bwd("l0_norm_b", xs, dh0, g_mix[0] * (1.0 + sc_m), dx1)
    _, cs1_c, cs2_c = norm_mod_bwd("l0_norm_bc", cx, dhc0, g_mix[0] * (1.0 + scc), jnp.zeros_like(cx))

    dmod0 = jnp.concatenate([cs1_m[0], cs2_m[0] * g_mix[0], dgt_m[0], gf0["dsh"], gf0["dsc"], gf0["dgt"]])
    dmod1 = jnp.concatenate([cs1_v[0], cs2_v[0] * g_mix[1], dgt_v[0], gf1["dsh"], gf1["dsc"], gf1["dgt"]])
    dmodc = jnp.concatenate([cs1_c[0], cs2_c[0] * g_mix[0], jnp.zeros((4 * D,), F32)])
    dm_rows = jnp.concatenate([jnp.stack([dmod0, dmod1, dmodc]), jnp.zeros((SUB - 3, N_MOD * D), F32)])
    dm_all = allgather_dev("gather_dmod", dm_rows).reshape(N_DEV, SUB, N_MOD * D)
    dm_sum = sum_lead("sum_dmod", dm_all, F32)
    pad7 = jnp.zeros((2 * SUB - N_DEV - 1, N_MOD * D), F32)
    dMod = [jnp.concatenate([dm_all[:, 0], dm_sum[2:3], pad7]), jnp.concatenate([dm_all[:, 1], jnp.zeros_like(dm_sum[2:3]), pad7])]
    dMod_cols = [lax.dynamic_slice_in_dim(m, chip * Cq, Cq, axis=1) for m in dMod]
    g_w_mod = jnp.stack([mm_tn("mod_bw%d" % l, S16, dMod_cols[l], F32) for l in range(2)])
    g_b_mod = jnp.stack([dm_sum[0] + dm_sum[2], dm_sum[1]])
    ds_part = mm_nt("mod_bx", dMod_cols[0], w_mod[0], F32)
    ds_all = allgather_dev("gather_dsc", jnp.zeros((SUB, D), F32).at[0].set(ds_part[N_DEV]))
    ds_c = sum_lead("sum_dsc", ds_all.reshape(N_DEV, SUB, D)[0::2], F32)[0]
    sg_c = jax.nn.sigmoid(c_ctx)
    g_c_ctx = ds_c * sg_c * (1.0 + c_ctx * (1.0 - sg_c))

    g_rpb_loc = na_bias_grad(dbias)

    dbb = [_blockdiag_extract(dBm[:, z], Cg, P) for z in range(2)]
    dcc = [_blockdiag_extract(dCm[:, z], Cg, P).swapaxes(-1, -2) for z in range(2)]
    da = jnp.sum(da8, axis=2).reshape(2, 2, G, P)
    small = {
        "g_mix": jnp.stack([cs2_m[0] * (1.0 + sc_m) + cs2_c[0] * (1.0 + scc), cs2_v[0] * (1.0 + sc_v)]),
        "g_ffn": jnp.stack([gf0["dg"], gf1["dg"]]),
        "a_re": da[:, 0], "a_im": da[:, 1], "bb_re": dbb[0], "bb_im": dbb[1], "c_re": dcc[0], "c_im": -dcc[1],
        "ssm_d": dd_skip, "na_rpb": g_rpb_loc, "cv_dw_w": ddw_w, "cv_dw_b": ddw_b, "cv_ln_g": dln_g, "cv_ln_b": dln_b,
        "ffn_conv_w": jnp.stack([gf0["dcw"], gf1["dcw"]]), "ffn_conv_b": jnp.stack([gf0["dcb"], gf1["dcb"]]),
        "g_out": dg_out,
    }
    skeys = list(small)
    sbuf = _pack([small[k] for k in skeys], 1024, 4 * SUB)
    ssum = dict(zip(skeys, _unpack(allreduce_small("reduce_small", sbuf, ids), [small[k].shape for k in skeys])))
    g_lam_re, g_lam_im, g_log_dt, g_b_re, g_b_im = disc_vjp((ssum["a_re"], ssum["a_im"], ssum["bb_re"], ssum["bb_im"]))

    def my_cols(t):
        n = t.shape[-1] // N_CHIP
        return lax.dynamic_slice_in_dim(t, chip * n, n, axis=t.ndim - 1)

    parts_mix = reduce_front("mix", [dWin, dWglu.reshape(N_CHIP, -1, W), dWout.reshape(N_CHIP, -1, D)])
    slots_mix = reduce_2("reduce_mix_2", parts_mix)
    parts = parts_mix + parts_conv + parts_ffn0 + parts_ffn1
    slots = list(slots_mix) + list(slots_conv) + list(slots_ffn0) + list(slots_ffn1)
    fulls = [sum_slots("reduce_sum_%d" % i, s, t, ids) for i, (s, t) in enumerate(zip(slots, parts))]
    full = [f.reshape(-1, f.shape[-1]) for f in reduce_3("reduce_g_3", fulls)]
    gWin, gWglu, gWout, gWpw1, gWpw2, gWup0, gWdn0, gWup1, gWdn1 = full

    grads = {
        "c_ctx": g_c_ctx, "w_mod": g_w_mod, "b_mod": g_b_mod, "g_mix": ssum["g_mix"], "g_ffn": ssum["g_ffn"],
        "w_in": gWin[None], "ssm_lam_re": g_lam_re[None], "ssm_lam_im": g_lam_im[None], "ssm_log_dt": g_log_dt[None],
        "ssm_b_re": g_b_re[None], "ssm_b_im": g_b_im[None], "ssm_c_re": ssum["c_re"][None], "ssm_c_im": ssum["c_im"][None],
        "ssm_d": ssum["ssm_d"], "ssm_w_glu": gWglu[None], "na_rpb": ssum["na_rpb"][None], "w_out": gWout[None],
        "cv_w_pw1": gWpw1[None], "cv_dw_w": my_cols(ssum["cv_dw_w"])[None], "cv_dw_b": my_cols(ssum["cv_dw_b"]),
        "cv_ln_g": my_cols(ssum["cv_ln_g"]), "cv_ln_b": my_cols(ssum["cv_ln_b"]), "cv_w_pw2": gWpw2[None],
        "ffn_w_up": jnp.stack([gWup0, gWup1]), "ffn_conv_w": my_cols(ssum["ffn_conv_w"]), "ffn_conv_b": ssum["ffn_conv_b"],
        "ffn_w_down": jnp.stack([gWdn0, gWdn1]), "g_out": ssum["g_out"][0],
    }
    grads = {k: grads[k].reshape(p[k].shape) for k in _WEIGHTS}

    large = [k for k in _WEIGHTS if p[k].size >= (1 << 18)]
    tiny = [k for k in _WEIGHTS if k not in large]
    delta, new_m, new_v = {}, {}, {}
    for k in large:
        delta[k], new_m[k], new_v[k] = adamw("adamw_" + k, p[k], grads[k], p["m_" + k], p["v_" + k])
    packs = [_pack([src[pre + k] for k in tiny], 1024) for src, pre in ((p, ""), (grads, ""), (p, "m_"), (p, "v_"))]
    outs = adamw("adamw_small", *packs)
    shapes = [p[k].shape for k in tiny]
    for dst, buf in zip((delta, new_m, new_v), outs):
        dst.update(zip(tiny, _unpack(buf, shapes)))

    return (loss, dx0[None], *[grads[k] for k in _WEIGHTS], *[delta[k] for k in _WEIGHTS],
            *[new_m[k] for k in _WEIGHTS], *[new_v[k] for k in _WEIGHTS])
```

```python
import functools
import math

import numpy as np
import jax
import jax.numpy as jnp
from jax import lax
from jax.experimental import pallas as pl
from jax.experimental.pallas import tpu as pltpu

F32, BF16 = jnp.float32, jnp.bfloat16
MESH = pl.DeviceIdType.MESH
V7X_VMEM_LIMIT = 56 << 20
LANE, SUB = 128, 8
N_CHIP, N_DEV = 4, 8

GRID_W = 64
N_MOD = 6
SSM_GROUP, SSM_STATE = 16, 64
NA_HEAD_DIM, NA_WIN_R, NA_WIN_C = 128, 8, 16
EPS = 1e-6
NEG = -1e30
ADAM_LR, ADAM_B1, ADAM_B2, ADAM_EPS, ADAM_WD, ADAM_STEP = 0.001, 0.9, 0.999, 1e-08, 0.01, 10
S5_STRIP = 512
S5_SEG = 8

NN = (((1,), (0,)), ((), ()))
NT = (((1,), (1,)), ((), ()))
TN = (((0,), (0,)), ((), ()))


def _params(*sem, side_effects=False):
    return pltpu.CompilerParams(dimension_semantics=sem if sem else None, vmem_limit_bytes=V7X_VMEM_LIMIT,
                                has_side_effects=side_effects)


def _call(body, args, *, name, grid, in_specs, out_specs, out_shape, sem, scratch_shapes=(), comm=None):
    out_specs, out_shape, scratch_shapes = list(out_specs), list(out_shape), list(scratch_shapes)
    if comm is None:
        outs = pl.pallas_call(body, name=name, grid=grid, in_specs=list(in_specs), out_specs=out_specs, out_shape=out_shape,
                              scratch_shapes=scratch_shapes, compiler_params=_params(*sem))(*args)
        return list(outs), []
    c_args, c_shapes, c_alias, n_remote, plan = comm
    n_in, n_out, n_ci, n_co, n_sc = len(args), len(out_shape), len(c_args), len(c_shapes), len(scratch_shapes)

    def wrapped(*refs):
        ins, cins = refs[:n_in], refs[n_in:n_in + n_ci]
        o0 = n_in + n_ci
        outs, couts = refs[o0:o0 + n_out], refs[o0 + n_out:o0 + n_out + n_co]
        s0 = o0 + n_out + n_co
        scr, (send_sems, recv_sems) = refs[s0:s0 + n_sc], refs[s0 + n_sc:]
        pids = [pl.program_id(a) for a in range(len(grid))]
        first = functools.reduce(jnp.logical_and, [q == 0 for q in pids])
        last = functools.reduce(jnp.logical_and, [q == g - 1 for q, g in zip(pids, grid)])
        me = (lax.axis_index("x"), lax.axis_index("y"), lax.axis_index("c"))

        def copies():
            _, sends, lands = plan(cins, couts)
            assert len(sends) == n_remote and len(lands) == n_remote
            out = [pltpu.make_async_remote_copy(src_ref=s, dst_ref=d, send_sem=send_sems.at[i], recv_sem=recv_sems.at[i],
                                                device_id=peer, device_id_type=MESH) for i, (s, d, peer) in enumerate(sends)]
            arrivals = [pltpu.make_async_remote_copy(src_ref=d, dst_ref=d, send_sem=send_sems.at[i], recv_sem=recv_sems.at[i],
                                                     device_id=me, device_id_type=MESH) for i, d in enumerate(lands)]
            return out, arrivals

        @pl.when(first)
        def _():
            for cp in copies()[0]:
                cp.start()

        body(*ins, *outs, *scr)

        @pl.when(last)
        def _():
            out, arrivals = copies()
            for cp in arrivals:
                cp.wait_recv()
            for cp in out:
                cp.wait_send()

    any_spec = pl.BlockSpec(memory_space=pl.ANY)
    res = pl.pallas_call(
        wrapped, name=name, grid=grid,
        in_specs=[*in_specs, *[any_spec] * n_ci], out_specs=[*out_specs, *[any_spec] * n_co],
        out_shape=[*out_shape, *[jax.ShapeDtypeStruct(s, d) for s, d in c_shapes]],
        input_output_aliases={n_in + i: n_out + j for i, j in c_alias.items()},
        scratch_shapes=[*scratch_shapes, pltpu.SemaphoreType.DMA((n_remote,)), pltpu.SemaphoreType.DMA((n_remote,))],
        compiler_params=_params(*["arbitrary"] * len(grid), side_effects=True),
    )(*args, *c_args)
    return list(res[:n_out]), list(res[n_out:])


def _pick(n, pref, mult=LANE):
    if n <= pref:
        return n
    best = None
    for t in range(mult, pref + 1, mult):
        if n % t == 0:
            best = t
    assert best is not None, (n, pref, mult)
    return best


def _sigmoid(x):
    return 1.0 / (1.0 + jnp.exp(-x))


def _mm(name, a, b, *, dims, grid, a_spec, b_spec, o_spec, out_shape, out_dtype, acc_shape, exact=False, comm=None):
    nk = grid[2]

    def body(a_ref, b_ref, o_ref, *scratch):
        if exact:
            part = lax.dot_general(a_ref[...], b_ref[...], dims, preferred_element_type=F32,
                                   precision=lax.Precision.HIGHEST)
        else:
            part = lax.dot_general(a_ref[...].astype(BF16), b_ref[...].astype(BF16), dims,
                                   preferred_element_type=F32)
        if nk == 1:
            o_ref[...] = part.astype(o_ref.dtype)
        else:
            acc = scratch[0]
            kk = pl.program_id(2)

            @pl.when(kk == 0)
            def _():
                acc[...] = part

            @pl.when(kk > 0)
            def _():
                acc[...] += part

            @pl.when(kk == nk - 1)
            def _():
                o_ref[...] = acc[...].astype(o_ref.dtype)

    outs, couts = _call(body, [a, b], name=name, grid=grid, in_specs=[a_spec, b_spec], out_specs=[o_spec],
                        out_shape=[jax.ShapeDtypeStruct(out_shape, out_dtype)],
                        scratch_shapes=[] if nk == 1 else [pltpu.VMEM(acc_shape, F32)],
                        sem=("parallel", "parallel", "arbitrary"), comm=comm)
    return outs[0] if comm is None else (outs[0], couts)


MM_VMEM_BUDGET = 36 << 20


def _fit(M, N, cost, m_mult=SUB):
    best = None
    for tm in sorted({_pick(M, p, m_mult) for p in (2048, 1024, 512, 256, 128)}):
        for tn in sorted({_pick(N, p) for p in (1408, 1024, 512, 256, 128)}):
            if best is None or (cost(tm, tn) <= MM_VMEM_BUDGET and tm * tn > best[0] * best[1]):
                best = (tm, tn)
    return best


def _sz(t):
    return jnp.dtype(t).itemsize


def mm_nn_pieces(name, a, w, p0, n_p, out_dtype, halves=1, comm=None):
    M, K = a.shape
    Nq = w.shape[2]
    tm, tn = _fit(M, Nq, lambda m, n: 2 * (m * K * _sz(a.dtype) + K * n * _sz(w.dtype) + m * n * _sz(out_dtype)))
    tpp = Nq // tn
    pph = n_p // halves
    if halves == 1:
        o_spec = pl.BlockSpec((tm, tn), lambda i, j, k: (i, j))
        oshape = (M, n_p * Nq)
    else:
        o_spec = pl.BlockSpec((None, tm, tn), lambda i, j, k: ((j // tpp) // pph, i, ((j // tpp) % pph) * tpp + j % tpp))
        oshape = (halves, M, pph * Nq)
    return _mm(name, a, w, dims=NN, grid=(M // tm, n_p * tpp, 1),
               a_spec=pl.BlockSpec((tm, K), lambda i, j, k: (i, 0)),
               b_spec=pl.BlockSpec((None, K, tn), lambda i, j, k: (p0 + j // tpp, 0, j % tpp)),
               o_spec=o_spec, out_shape=oshape, out_dtype=out_dtype, acc_shape=(tm, tn), comm=comm)


def mm_nn(name, a, w, out_dtype, exact=False, comm=None):
    M, K = a.shape
    N = w.shape[1]
    tm, tn = _fit(M, N, lambda m, n: 2 * (m * K * _sz(a.dtype) + K * n * _sz(w.dtype) + m * n * _sz(out_dtype)))
    return _mm(name, a, w, dims=NN, grid=(M // tm, N // tn, 1),
               a_spec=pl.BlockSpec((tm, K), lambda i, j, k: (i, 0)),
               b_spec=pl.BlockSpec((K, tn), lambda i, j, k: (0, j)),
               o_spec=pl.BlockSpec((tm, tn), lambda i, j, k: (i, j)),
               out_shape=(M, N), out_dtype=out_dtype, acc_shape=(tm, tn), exact=exact, comm=comm)


def mm_nt(name, dy, w, out_dtype, exact=False, comm=None):
    M, N = dy.shape
    K = w.shape[0]
    tm, tn = _fit(M, K, lambda m, n: 2 * (m * N * _sz(dy.dtype) + n * N * _sz(w.dtype) + m * n * _sz(out_dtype)))
    return _mm(name, dy, w, dims=NT, grid=(M // tm, K // tn, 1),
               a_spec=pl.BlockSpec((tm, N), lambda i, j, k: (i, 0)),
               b_spec=pl.BlockSpec((tn, N), lambda i, j, k: (j, 0)),
               o_spec=pl.BlockSpec((tm, tn), lambda i, j, k: (i, j)),
               out_shape=(M, K), out_dtype=out_dtype, acc_shape=(tm, tn), exact=exact, comm=comm)


def mm_nt_pieces(name, dy, w, out_dtype, halves=1, comm=None):
    P, K, Nq = w.shape
    M = dy.shape[-2]
    tm, tn = _fit(M, K, lambda m, n: 2 * (m * Nq * _sz(dy.dtype) + n * Nq * _sz(w.dtype) + m * n * _sz(out_dtype)) + 4 * m * n)
    pph = P // halves
    if halves == 1:
        a_spec = pl.BlockSpec((tm, Nq), lambda i, j, k: (i, k))
    else:
        a_spec = pl.BlockSpec((None, tm, Nq), lambda i, j, k: (k // pph, i, k % pph))
    return _mm(name, dy, w, dims=NT, grid=(M // tm, K // tn, P),
               a_spec=a_spec,
               b_spec=pl.BlockSpec((None, tn, Nq), lambda i, j, k: (k, j, 0)),
               o_spec=pl.BlockSpec((tm, tn), lambda i, j, k: (i, j)),
               out_shape=(M, K), out_dtype=out_dtype, acc_shape=(tm, tn), comm=comm)


def mm_tn(name, a, dy, out_dtype, comm=None):
    M, K = a.shape
    N = dy.shape[1]
    tm, tn = _fit(K, N, lambda m, n: 2 * (M * m * _sz(a.dtype) + M * n * _sz(dy.dtype) + m * n * _sz(out_dtype)), LANE)
    return _mm(name, a, dy, dims=TN, grid=(K // tm, N // tn, 1),
               a_spec=pl.BlockSpec((M, tm), lambda i, j, k: (0, i)),
               b_spec=pl.BlockSpec((M, tn), lambda i, j, k: (0, j)),
               o_spec=pl.BlockSpec((tm, tn), lambda i, j, k: (i, j)),
               out_shape=(K, N), out_dtype=out_dtype, acc_shape=(tm, tn), comm=comm)


def mm_tn_pieces(name, a, dy, n_p, out_dtype, halves=1, comm=None):
    M, K = a.shape
    Nq = (dy.shape[-1] * halves) // n_p
    tm, tn = _fit(K, Nq, lambda m, n: 2 * (M * m * _sz(a.dtype) + M * n * _sz(dy.dtype) + m * n * _sz(out_dtype)), LANE)
    tpp = Nq // tn
    pph = n_p // halves
    if halves == 1:
        b_spec = pl.BlockSpec((M, tn), lambda i, j, k: (0, j))
    else:
        b_spec = pl.BlockSpec((None, M, tn), lambda i, j, k: ((j // tpp) // pph, 0, ((j // tpp) % pph) * tpp + j % tpp))
    return _mm(name, a, dy, dims=TN, grid=(K // tm, n_p * tpp, 1),
               a_spec=pl.BlockSpec((M, tm), lambda i, j, k: (0, i)),
               b_spec=b_spec,
               o_spec=pl.BlockSpec((None, tm, tn), lambda i, j, k: (j // tpp, i, j % tpp)),
               out_shape=(n_p, K, Nq), out_dtype=out_dtype, acc_shape=(tm, tn), comm=comm)


def _row_call(name, body, ins, in_kinds, outs, rows, tr, scratch=()):
    def spec(kind, shape):
        if isinstance(kind, pl.BlockSpec):
            return kind
        if kind == "row":
            return pl.BlockSpec((tr,) + tuple(shape[1:]), lambda i: (i,) + (0,) * (len(shape) - 1))
        return pl.BlockSpec(tuple(shape), lambda i: (0,) * len(shape))

    return pl.pallas_call(
        body, name=name, grid=(rows // tr,),
        in_specs=[spec(k, a.shape) for k, a in zip(in_kinds, ins)],
        out_specs=[spec(k, s) for k, s, _ in outs],
        out_shape=[jax.ShapeDtypeStruct(s, d) for _, s, d in outs],
        scratch_shapes=list(scratch),
        compiler_params=_params("arbitrary"),
    )(*ins)


def _acc(ref, val):
    @pl.when(pl.program_id(0) == 0)
    def _():
        ref[...] = val

    @pl.when(pl.program_id(0) > 0)
    def _():
        ref[...] += val


def norm_mod_fwd(name, x, w, b, tr=256):
    rows, d = x.shape
    tr = _pick(rows, tr, SUB)

    def body(x_ref, w_ref, b_ref, h_ref):
        xv = x_ref[...]
        r = lax.rsqrt(jnp.mean(xv * xv, axis=-1, keepdims=True) + EPS)
        h_ref[...] = (xv * r * w_ref[...] + b_ref[...]).astype(BF16)

    return _row_call(name, body, [x, w.reshape(1, d), b.reshape(1, d)], ["row", "vec", "vec"],
                     [("row", (rows, d), BF16)], rows, tr)[0]


def norm_mod_bwd(name, x, dh, w, dx_in, tr=256):
    rows, d = x.shape
    tr = _pick(rows, tr, SUB)

    def body(x_ref, dh_ref, w_ref, dxi_ref, dx_ref, cs1_ref, cs2_ref):
        xv = x_ref[...]
        r = lax.rsqrt(jnp.mean(xv * xv, axis=-1, keepdims=True) + EPS)
        xn = xv * r
        dhv = dh_ref[...].astype(F32)
        dxn = dhv * w_ref[...]
        dx_ref[...] = dxi_ref[...] + r * (dxn - xn * jnp.mean(dxn * xn, axis=-1, keepdims=True))
        _acc(cs1_ref, jnp.sum(dhv, axis=0, keepdims=True))
        _acc(cs2_ref, jnp.sum(dhv * xn, axis=0, keepdims=True))

    return _row_call(name, body, [x, dh, w.reshape(1, d), dx_in], ["row", "row", "vec", "row"],
                     [("row", (rows, d), F32), ("acc", (1, d), F32), ("acc", (1, d), F32)], rows, tr)


def gate_res_fwd(name, x, y, gate, tr=256):
    rows, d = x.shape
    tr = _pick(rows, tr, SUB)

    def body(x_ref, y_ref, g_ref, o_ref):
        o_ref[...] = x_ref[...] + g_ref[...] * y_ref[...].astype(F32)

    return _row_call(name, body, [x, y, gate.reshape(1, d)], ["row", "row", "vec"],
                     [("row", (rows, d), F32)], rows, tr)[0]


def gate_res_bwd(name, dx, y, gate, tr=256):
    rows, d = dx.shape
    tr = _pick(rows, tr, SUB)

    def body(dx_ref, y_ref, g_ref, dy_ref, dg_ref):
        dxv = dx_ref[...]
        dy_ref[...] = (g_ref[...] * dxv).astype(BF16)
        _acc(dg_ref, jnp.sum(dxv * y_ref[...].astype(F32), axis=0, keepdims=True))

    return _row_call(name, body, [dx, y, gate.reshape(1, d)], ["row", "row", "vec"],
                     [("row", (rows, d), BF16), ("acc", (1, d), F32)], rows, tr)


def loss_head(name, x, g, target, tr=256):
    rows, d = x.shape
    tr = _pick(rows, tr, SUB)

    def body(x_ref, g_ref, t_ref, dx_ref, dg_ref, loss_ref):
        xv = x_ref[...]
        r = lax.rsqrt(jnp.mean(xv * xv, axis=-1, keepdims=True) + EPS)
        xn = xv * r
        err = xn * g_ref[...] - t_ref[...]
        dy = err * (1.0 / d)
        dxn = dy * g_ref[...]
        dx_ref[...] = r * (dxn - xn * jnp.mean(dxn * xn, axis=-1, keepdims=True))
        _acc(dg_ref, jnp.sum(dy * xn, axis=0, keepdims=True))
        part = 0.5 * jnp.sum(jnp.sum(err * err, axis=-1, keepdims=True) * (1.0 / d), axis=0, keepdims=True)
        _acc(loss_ref, jnp.broadcast_to(part, (1, LANE)))

    return _row_call(name, body, [x, g.reshape(1, d), target], ["row", "vec", "row"],
                     [("row", (rows, d), F32), ("acc", (1, d), F32), ("acc", (1, LANE), F32)], rows, tr)


def fma3(name, a, b, c, dvec, out_dtype, tr=256):
    rows, d = a.shape
    tr = _pick(rows, tr, SUB)

    def body(a_ref, b_ref, c_ref, d_ref, o_ref):
        o_ref[...] = (d_ref[...] * a_ref[...] + b_ref[...] + c_ref[...]).astype(o_ref.dtype)

    return _row_call(name, body, [a, b, c, dvec.reshape(1, d)], ["row", "row", "row", "vec"],
                     [("row", (rows, d), out_dtype)], rows, tr)[0]


def sum_lead(name, a, out_dtype, tr=512):
    n, rows, cols = a.shape
    tr = _pick(rows, tr, 16)

    def body(a_ref, o_ref):
        acc = a_ref[0].astype(F32)
        for s in range(1, n):
            acc = acc + a_ref[s].astype(F32)
        o_ref[...] = acc.astype(o_ref.dtype)

    return pl.pallas_call(
        body, name=name, grid=(rows // tr,),
        in_specs=[pl.BlockSpec((n, tr, cols), lambda i: (0, i, 0))],
        out_specs=pl.BlockSpec((tr, cols), lambda i: (i, 0)),
        out_shape=jax.ShapeDtypeStruct((rows, cols), out_dtype),
        compiler_params=_params("parallel"),
    )(a)


def adamw(name, w, g, m, v, tr=512):
    shape = w.shape
    cols = shape[-1]
    w2, g2, m2, v2 = (t.reshape(-1, cols) for t in (w, g, m, v))
    rows = w2.shape[0]
    tr, tc = _pick(rows, 256, SUB), _pick(cols, 1536)
    c1 = 1.0 - ADAM_B1 ** ADAM_STEP
    c2 = 1.0 - ADAM_B2 ** ADAM_STEP

    def body(w_ref, g_ref, m_ref, v_ref, d_ref, mo_ref, vo_ref):
        gv = g_ref[...]
        mn = ADAM_B1 * m_ref[...] + (1.0 - ADAM_B1) * gv
        vn = ADAM_B2 * v_ref[...] + (1.0 - ADAM_B2) * (gv * gv)
        mo_ref[...] = mn
        vo_ref[...] = vn
        d_ref[...] = -ADAM_LR * ((mn / c1) / (jnp.sqrt(vn / c2) + ADAM_EPS) + ADAM_WD * w_ref[...])

    blk = pl.BlockSpec((tr, tc), lambda i, j: (i, j))
    outs = pl.pallas_call(
        body, name=name, grid=(rows // tr, cols // tc), in_specs=[blk] * 4, out_specs=[blk] * 3,
        out_shape=[jax.ShapeDtypeStruct(w2.shape, F32)] * 3, compiler_params=_params("parallel", "parallel"),
    )(w2, g2, m2, v2)
    return tuple(o.reshape(shape) for o in outs)


def add_half(name, grad, got, c_idx, tr=256):
    Pn, R, C = grad.shape
    hr = R // 2
    tr = _pick(hr, tr, HALO)
    nb = hr // tr

    def body(c_ref, a_ref, b_ref, o_ref):
        o_ref[...] = (a_ref[...].astype(F32) + b_ref[...].astype(F32)).astype(o_ref.dtype)

    return pl.pallas_call(
        body, name=name,
        grid_spec=pltpu.PrefetchScalarGridSpec(
            num_scalar_prefetch=1, grid=(Pn, nb),
            in_specs=[pl.BlockSpec((None, tr, C), lambda q, i, c: (q, c[0] * nb + i, 0)),
                      pl.BlockSpec((None, tr, C), lambda q, i, c: (q, i, 0))],
            out_specs=pl.BlockSpec((None, tr, C), lambda q, i, c: (q, i, 0))),
        out_shape=jax.ShapeDtypeStruct((Pn, hr, C), BF16),
        compiler_params=_params("parallel", "parallel"),
    )(c_idx, grad, got)


def pair_sum_to_slot(name, buf, got, ids, tr=256):
    R, C = buf.shape
    hr = R // 2
    tr = _pick(hr, tr, SUB)
    nb = hr // tr

    def body(ids_ref, a_ref, b_ref, o_ref):
        o_ref[...] = a_ref[...] + b_ref[...]

    return pl.pallas_call(
        body, name=name,
        grid_spec=pltpu.PrefetchScalarGridSpec(
            num_scalar_prefetch=1, grid=(nb,),
            in_specs=[pl.BlockSpec((tr, C), lambda i, ids: (ids[1] * nb + i, 0)),
                      pl.BlockSpec((tr, C), lambda i, ids: (i, 0))],
            out_specs=pl.BlockSpec((None, tr, C), lambda i, ids: (ids[0], i, 0))),
        out_shape=jax.ShapeDtypeStruct((N_CHIP, hr, C), F32),
        compiler_params=_params("parallel"),
    )(ids, buf, got)


def sum_chips_to_half(name, slots, ids, tr=256):
    n, hr, C = slots.shape
    tr = _pick(hr, tr, SUB)

    def body(ids_ref, s_ref, o_ref):
        acc = s_ref[0]
        for q in range(1, n):
            acc = acc + s_ref[q]
        o_ref[...] = acc

    return pl.pallas_call(
        body, name=name,
        grid_spec=pltpu.PrefetchScalarGridSpec(
            num_scalar_prefetch=1, grid=(hr // tr,),
            in_specs=[pl.BlockSpec((n, tr, C), lambda i, ids: (0, i, 0))],
            out_specs=pl.BlockSpec((None, tr, C), lambda i, ids: (ids[1], i, 0))),
        out_shape=jax.ShapeDtypeStruct((2, hr, C), F32),
        compiler_params=_params("parallel"),
    )(ids, slots)


def sum_slots(name, slots, mine, ids, tr=256):
    Pn, hr, C = slots.shape
    tr = _pick(hr, tr, HALO)

    def body(ids_ref, m_ref, s1_ref, s2_ref, s3_ref, o_ref):
        o_ref[...] = (m_ref[...].astype(F32) + s1_ref[...].astype(F32)) + (s2_ref[...].astype(F32) + s3_ref[...].astype(F32))

    def other(k):
        return pl.BlockSpec((None, tr, C), lambda i, ids: ((ids[0] + k) % Pn, i, 0))

    return pl.pallas_call(
        body, name=name,
        grid_spec=pltpu.PrefetchScalarGridSpec(
            num_scalar_prefetch=1, grid=(hr // tr,),
            in_specs=[pl.BlockSpec((None, tr, C), lambda i, ids: (ids[0], i, 0)), other(1), other(2), other(3)],
            out_specs=pl.BlockSpec((None, tr, C), lambda i, ids: (ids[1], i, 0))),
        out_shape=jax.ShapeDtypeStruct((2, hr, C), F32),
        compiler_params=_params("parallel"),
    )(ids, mine, slots, slots, slots)


HALO = 16


def _halo_specs(lead, R, tn, n_rows, col_of):
    nb, nblk = R // HALO, n_rows // HALO

    def mk(rows, row_of):
        return pl.BlockSpec((lead, rows, tn), lambda *g: (0, row_of(g[-1]), col_of(g)))

    return (mk(HALO, lambda i: jnp.maximum(i * nb - 1, 0)), mk(R, lambda i: i),
            mk(HALO, lambda i: jnp.minimum((i + 1) * nb, nblk - 1)))


def _fill_halo(dst, i, last, R, prev, cur, nxt):
    nd = len(dst.shape)
    lead = (slice(None),) * (nd - 2)
    dst[lead + (slice(0, HALO), slice(None))] = jnp.where(i == 0, 0.0, prev)
    dst[lead + (slice(HALO, HALO + R), slice(None))] = cur
    dst[lead + (slice(HALO + R, HALO + R + HALO), slice(None))] = jnp.where(i == last, 0.0, nxt)


def _shift_mats(n):
    i = np.arange(n)
    return jnp.asarray(np.stack([i[:, None] - 1 == i[None, :], i[:, None] + 1 == i[None, :]]), BF16)


def _shifted(s_ref, xb):
    return (jnp.dot(s_ref[0], xb, preferred_element_type=F32), jnp.dot(s_ref[1], xb, preferred_element_type=F32))


def ffn_mid_fwd(name, up3, cw, cb, R=256, tn=512, comm=None):
    _, L, Fd = up3.shape
    R, tn = _pick(L, R, HALO), _pick(Fd, tn)
    nrow = L // R

    def body(p_ref, c_ref, n_ref, w_ref, b_ref, s_ref, act_ref):
        i = pl.program_id(1)
        row = lax.broadcasted_iota(jnp.int32, (R, tn), 0)
        cv = []
        for z in range(2):
            xb = c_ref[z]
            before = jnp.where(i == 0, 0.0, p_ref[z].astype(F32)[HALO - 1:HALO])
            after = jnp.where(i == nrow - 1, 0.0, n_ref[z].astype(F32)[0:1])
            dn, up = _shifted(s_ref, xb)
            dn = jnp.where(row == 0, before, dn)
            up = jnp.where(row == R - 1, after, up)
            cv.append(b_ref[z] + w_ref[z, 0:1, :] * dn + w_ref[z, 1:2, :] * xb.astype(F32) + w_ref[z, 2:3, :] * up)
        u, g = cv
        act_ref[...] = (u * g * _sigmoid(g)).astype(BF16)

    hs = _halo_specs(2, R, tn, L, lambda g: g[0])
    outs, couts = _call(
        body, [up3, up3, up3, cw, cb, _shift_mats(R)], name=name, grid=(Fd // tn, nrow),
        in_specs=[*hs, pl.BlockSpec((2, 3, tn), lambda j, i: (0, 0, j)), pl.BlockSpec((2, 1, tn), lambda j, i: (0, 0, j)),
                  pl.BlockSpec((2, R, R), lambda j, i: (0, 0, 0))],
        out_specs=[pl.BlockSpec((R, tn), lambda j, i: (i, j))],
        out_shape=[jax.ShapeDtypeStruct((L, Fd), BF16)], sem=("parallel", "arbitrary"), comm=comm)
    return outs[0] if comm is None else (outs[0], couts)


def ffn_mid_bwd(name, up3, dact, cw, cb, R=256, tn=512, comm=None):
    _, L, Fd = up3.shape
    R, tn = _pick(L, R, HALO), _pick(Fd, tn)
    nrow = L // R
    E = R + 2 * HALO
    inner = slice(HALO, HALO + R)

    def body(pu, cu, nu, pd, cd, nd, w_ref, b_ref, s_ref, dup_ref, dcw_ref, dcb_ref, xs, ds):
        i = pl.program_id(1)
        _fill_halo(xs, i, nrow - 1, R, pu[...], cu[...], nu[...])
        _fill_halo(ds, i, nrow - 1, R, pd[0], cd[0], nd[0])
        da = ds[...].astype(F32)
        cv, taps = [], []
        for z in range(2):
            xb = xs[z]
            dn, up = _shifted(s_ref, xb)
            taps.append((dn, xb.astype(F32), up))
            cv.append(b_ref[z] + w_ref[z, 0:1, :] * dn + w_ref[z, 1:2, :] * taps[z][1] + w_ref[z, 2:3, :] * up)
        u, g = cv
        sg = _sigmoid(g)
        dcs = (da * g * sg, da * u * sg * (1.0 + g * (1.0 - sg)))

        @pl.when(i == 0)
        def _():
            dcw_ref[...] = jnp.zeros_like(dcw_ref)
            dcb_ref[...] = jnp.zeros_like(dcb_ref)

        for z in range(2):
            dc = dcs[z]
            dc_dn, dc_up = _shifted(s_ref, dc.astype(BF16))
            d = w_ref[z, 0:1, :] * dc_up + w_ref[z, 1:2, :] * dc + w_ref[z, 2:3, :] * dc_dn
            dup_ref[z] = d[inner].astype(BF16)
            dci = dc[inner]
            dcb_ref[z] += jnp.sum(dci, axis=0, keepdims=True)
            for k in range(3):
                dcw_ref[z, k:k + 1, :] += jnp.sum(dci * taps[z][k][inner], axis=0, keepdims=True)

    hu = _halo_specs(2, R, tn, L, lambda g: g[0])
    hd = _halo_specs(1, R, tn, L, lambda g: g[0])
    outs, couts = _call(
        body, [up3, up3, up3, dact[None], dact[None], dact[None], cw, cb, _shift_mats(E)], name=name, grid=(Fd // tn, nrow),
        in_specs=[*hu, *hd, pl.BlockSpec((2, 3, tn), lambda j, i: (0, 0, j)), pl.BlockSpec((2, 1, tn), lambda j, i: (0, 0, j)),
                  pl.BlockSpec((2, E, E), lambda j, i: (0, 0, 0))],
        out_specs=[pl.BlockSpec((2, R, tn), lambda j, i: (0, i, j)), pl.BlockSpec((2, 3, tn), lambda j, i: (0, 0, j)),
                   pl.BlockSpec((2, 1, tn), lambda j, i: (0, 0, j))],
        out_shape=[jax.ShapeDtypeStruct((2, L, Fd), BF16), jax.ShapeDtypeStruct((2, 3, Fd), F32),
                   jax.ShapeDtypeStruct((2, 1, Fd), F32)],
        scratch_shapes=[pltpu.VMEM((2, E, tn), BF16), pltpu.VMEM((E, tn), BF16)],
        sem=("parallel", "arbitrary"), comm=comm)
    return outs if comm is None else (outs, couts)


def _glu_z0(blk):
    return blk[0].astype(F32) * _sigmoid(blk[1].astype(F32))


def _sublane_copies(ref, cs):
    n = ref.shape[1]
    blk = ref[0, :, cs]
    for b in range(1, SUB):
        ref[b, :, cs] = pltpu.roll(blk, n - b, 0)


def _tap(ref, offset, rows, cs):
    return ref[offset % SUB, pl.ds(offset - offset % SUB, rows), cs]


def conf_mid_fwd(name, ag3, dw_w, dw_b, ln_g, ln_b, R=128, cb=256):
    _, L, C = ag3.shape
    K = dw_w.shape[0]
    pad = (K - 1) // 2
    assert pad <= HALO
    R, cb = _pick(L, R, HALO), _pick(C, cb)
    nrow = L // R

    def body(p_ref, c_ref, n_ref, w_ref, b_ref, g_ref, bb_ref, z1_ref, z3_ref, s_ref):
        i = pl.program_id(0)
        _fill_halo(s_ref.at[0], i, nrow - 1, R, _glu_z0(p_ref), _glu_z0(c_ref), _glu_z0(n_ref))
        for c0 in range(0, C, cb):
            cs = slice(c0, c0 + cb)
            _sublane_copies(s_ref, cs)
            acc = jnp.broadcast_to(b_ref[:, cs], (R, cb))
            for k in range(K):
                acc = acc + w_ref[k:k + 1, cs] * _tap(s_ref, HALO - pad + k, R, cs)
            z1_ref[:, cs] = acc
        z1 = z1_ref[...]
        zc = z1 - jnp.mean(z1, axis=-1, keepdims=True)
        zn = zc * lax.rsqrt(jnp.mean(zc * zc, axis=-1, keepdims=True) + EPS)
        z2 = zn * g_ref[...] + bb_ref[...]
        z3_ref[...] = (z2 * _sigmoid(z2)).astype(BF16)

    hs = _halo_specs(2, R, C, L, lambda g: 0)
    vec = pl.BlockSpec((1, C), lambda i: (0, 0))
    return pl.pallas_call(
        body, name=name, grid=(nrow,),
        in_specs=[*hs, pl.BlockSpec((K, C), lambda i: (0, 0)), vec, vec, vec],
        out_specs=[pl.BlockSpec((R, C), lambda i: (i, 0)), pl.BlockSpec((R, C), lambda i: (i, 0))],
        out_shape=[jax.ShapeDtypeStruct((L, C), F32), jax.ShapeDtypeStruct((L, C), BF16)],
        scratch_shapes=[pltpu.VMEM((SUB, R + 2 * HALO, C), F32)],
        compiler_params=_params("parallel"),
    )(ag3, ag3, ag3, dw_w, dw_b.reshape(1, C), ln_g.reshape(1, C), ln_b.reshape(1, C))


def conf_ln_bwd(name, z1, dz3, ln_g, ln_b, tr=256):
    rows, C = z1.shape
    tr = _pick(rows, tr, HALO)

    def body(z_ref, d_ref, g_ref, b_ref, dz_ref, dg_ref, db_ref):
        z1v = z_ref[...]
        zc = z1v - jnp.mean(z1v, axis=-1, keepdims=True)
        rs = lax.rsqrt(jnp.mean(zc * zc, axis=-1, keepdims=True) + EPS)
        zn = zc * rs
        z2 = zn * g_ref[...] + b_ref[...]
        sg = _sigmoid(z2)
        dz2 = d_ref[...].astype(F32) * sg * (1.0 + z2 * (1.0 - sg))
        _acc(dg_ref, jnp.sum(dz2 * zn, axis=0, keepdims=True))
        _acc(db_ref, jnp.sum(dz2, axis=0, keepdims=True))
        dzn = dz2 * g_ref[...]
        dz1 = rs * (dzn - jnp.mean(dzn, axis=-1, keepdims=True) - zn * jnp.mean(dzn * zn, axis=-1, keepdims=True))
        dz_ref[...] = dz1.astype(BF16)

    return _row_call(name, body, [z1, dz3, ln_g.reshape(1, C), ln_b.reshape(1, C)], ["row", "row", "vec", "vec"],
                     [("row", (rows, C), BF16), ("acc", (1, C), F32), ("acc", (1, C), F32)], rows, tr)


def conf_conv_bwd(name, ag3, dz1, dw_w, R=128, cb=256, comm=None):
    _, L, C = ag3.shape
    K = dw_w.shape[0]
    pad = (K - 1) // 2
    R, cb = _pick(L, R, HALO), _pick(C, cb)
    nrow = L // R

    def body(pa, ca, na, pd, cd, nd, w_ref, dag_ref, dw_ref, db_ref, s_ref, d_ref, z_ref):
        i = pl.program_id(0)
        _fill_halo(s_ref.at[0], i, nrow - 1, R, _glu_z0(pa), _glu_z0(ca), _glu_z0(na))
        _fill_halo(d_ref.at[0], i, nrow - 1, R, pd[0].astype(F32), cd[0].astype(F32), nd[0].astype(F32))

        @pl.when(i == 0)
        def _():
            dw_ref[...] = jnp.zeros_like(dw_ref)
            db_ref[...] = jnp.zeros_like(db_ref)

        for c0 in range(0, C, cb):
            cs = slice(c0, c0 + cb)
            _sublane_copies(s_ref, cs)
            _sublane_copies(d_ref, cs)
            dcur = d_ref[0, pl.ds(HALO, R), cs]
            acc = jnp.zeros((R, cb), F32)
            for k in range(K):
                acc = acc + w_ref[k:k + 1, cs] * _tap(d_ref, HALO + pad - k, R, cs)
                dw_ref[k:k + 1, cs] += jnp.sum(dcur * _tap(s_ref, HALO - pad + k, R, cs), axis=0, keepdims=True)
            z_ref[:, cs] = acc
            db_ref[:, cs] += jnp.sum(dcur, axis=0, keepdims=True)
        dz0 = z_ref[...]
        a = ca[0].astype(F32)
        sg = _sigmoid(ca[1].astype(F32))
        dag_ref[0] = (dz0 * sg).astype(BF16)
        dag_ref[1] = (dz0 * a * sg * (1.0 - sg)).astype(BF16)

    ha = _halo_specs(2, R, C, L, lambda g: 0)
    hd = _halo_specs(1, R, C, L, lambda g: 0)
    outs, couts = _call(
        body, [ag3, ag3, ag3, dz1[None], dz1[None], dz1[None], dw_w], name=name, grid=(nrow,),
        in_specs=[*ha, *hd, pl.BlockSpec((K, C), lambda i: (0, 0))],
        out_specs=[pl.BlockSpec((2, R, C), lambda i: (0, i, 0)), pl.BlockSpec((K, C), lambda i: (0, 0)),
                   pl.BlockSpec((1, C), lambda i: (0, 0))],
        out_shape=[jax.ShapeDtypeStruct((2, L, C), BF16), jax.ShapeDtypeStruct((K, C), F32),
                   jax.ShapeDtypeStruct((1, C), F32)],
        scratch_shapes=[pltpu.VMEM((SUB, R + 2 * HALO, C), F32), pltpu.VMEM((SUB, R + 2 * HALO, C), F32),
                        pltpu.VMEM((R, C), F32)],
        sem=("arbitrary",), comm=comm)
    return outs if comm is None else (outs, couts)


_GELU_C = math.sqrt(2.0 / math.pi)


def _gelu(x):
    return 0.5 * x * (1.0 + jnp.tanh(_GELU_C * (x + 0.044715 * x * x * x)))


def _gelu_grad(x):
    t = jnp.tanh(_GELU_C * (x + 0.044715 * x * x * x))
    return 0.5 * (1.0 + t) + 0.5 * x * (1.0 - t * t) * _GELU_C * (1.0 + 3.0 * 0.044715 * x * x)


def glu_fwd(name, u, y0, y1, d, wg, tr=512):
    rows, W = u.shape
    tr = _pick(rows, tr, HALO)

    def body(u_ref, y0_ref, y1_ref, d_ref, w_ref, o_ref):
        z = _gelu(d_ref[...] * u_ref[...] + y0_ref[...] + y1_ref[...])
        zz = jnp.dot(z.astype(BF16), w_ref[...], preferred_element_type=F32)
        o_ref[...] = (z * _sigmoid(zz)).astype(BF16)

    return _row_call(name, body, [u, y0, y1, d.reshape(1, W), wg], ["row", "row", "row", "vec", "vec"],
                     [("row", (rows, W), BF16)], rows, tr)[0]


def glu_bwd(name, u, y0, y1, d, wg, dmix, tr=512):
    rows, W = u.shape
    tr = _pick(rows, tr, HALO)

    def body(u_ref, y0_ref, y1_ref, d_ref, w_ref, do_ref, dy_ref, z_ref, dzz_ref, dd_ref):
        uv = u_ref[...]
        y = d_ref[...] * uv + y0_ref[...] + y1_ref[...]
        z = _gelu(y)
        zz = jnp.dot(z.astype(BF16), w_ref[...], preferred_element_type=F32)
        sg = _sigmoid(zz)
        do = do_ref[...].astype(F32)
        dzz = (do * z * sg * (1.0 - sg)).astype(BF16)
        dz = do * sg + lax.dot_general(dzz, w_ref[...], NT, preferred_element_type=F32)
        dy = dz * _gelu_grad(y)
        dy_ref[...] = dy
        z_ref[...] = z.astype(BF16)
        dzz_ref[...] = dzz
        _acc(dd_ref, jnp.sum(dy * uv, axis=0, keepdims=True))

    do_spec = pl.BlockSpec((tr, W), lambda i: (i, 0))
    return _row_call(name, body, [u, y0, y1, d.reshape(1, W), wg, dmix], ["row", "row", "row", "vec", "vec", do_spec],
                     [("row", (rows, W), F32), ("row", (rows, W), BF16), ("row", (rows, W), BF16), ("acc", (1, W), F32)],
                     rows, tr)


NA_KEYS = NA_WIN_R * GRID_W


NA_PAIRS = NA_WIN_R // 2


def na_bias(rpb):
    H, nr, nc = rpb.shape
    e, ok = _na_col_select()
    rp = jnp.pad(rpb.reshape(H * nr, nc), ((0, (-H * nr) % SUB), (0, LANE - nc)))
    cols = mm_nn("na_bias_mm", rp, jnp.asarray(e, F32), F32, exact=True)[:H * nr]
    tiles = (cols + jnp.asarray(np.where(ok, 0.0, NEG), F32)).reshape(H, nr, GRID_W, GRID_W)
    return jnp.concatenate([tiles[:, :-1], tiles[:, 1:]], axis=-1)


def na_bias_grad(db2):
    H, n2 = db2.shape[:2]
    left, right = db2[..., :GRID_W], db2[..., GRID_W:]
    tiles = jnp.pad(left, ((0, 0), (0, 1), (0, 0), (0, 0))) + jnp.pad(right, ((0, 0), (1, 0), (0, 0), (0, 0)))
    flat = tiles.reshape(H * (n2 + 1), GRID_W * GRID_W)
    flat = jnp.pad(flat, ((0, (-flat.shape[0]) % SUB), (0, 0)))
    dcol = mm_nt("na_bias_fold", flat, na_bias_fold_matrix(), F32, exact=True)
    return dcol[:H * (n2 + 1), :2 * NA_WIN_C - 1].reshape(H, n2 + 1, 2 * NA_WIN_C - 1)


def _na_col_select():
    q = np.arange(GRID_W)
    cs = np.clip(q - NA_WIN_C // 2, 0, GRID_W - NA_WIN_C)
    ok = ((q[None, :] >= cs[:, None]) & (q[None, :] < cs[:, None] + NA_WIN_C)).reshape(-1)
    cidx = np.clip(q[None, :] - q[:, None] + (NA_WIN_C - 1), 0, 2 * NA_WIN_C - 2).reshape(-1)
    return (cidx[None, :] == np.arange(LANE)[:, None]) & ok[None, :], ok


def na_bias_fold_matrix():
    return jnp.asarray(_na_col_select()[0], F32)


def _na_window(r, rows):
    kr0 = jnp.clip(r - NA_WIN_R // 2, 0, rows - NA_WIN_R)
    return pl.multiple_of(kr0 * GRID_W, GRID_W), r - kr0


def _na_dims(qkv, kvc):
    L = qkv.shape[0]
    NA = qkv.shape[1] // 3
    H = NA // NA_HEAD_DIM
    hp = 2 if H % 2 == 0 else 1
    return L, NA, H, hp, H // hp, L // GRID_W, kvc.shape[0]


def _na_bias_tile(b_ref, hh, off):
    return jnp.concatenate([b_ref[hh, NA_WIN_R - 1 - off + 2 * j] for j in range(NA_PAIRS)], axis=-1)


def natten_fwd(name, qkv, kvc, bias, comm=None):
    L, NA, H, hp, G, rows, Lc = _na_dims(qkv, kvc)
    scale = NA_HEAD_DIM ** -0.5
    wd = hp * NA_HEAD_DIM

    def body(q_ref, k_ref, v_ref, kc_ref, vc_ref, b_ref, o_ref, lse_ref):
        st, off = _na_window(pl.program_id(1), rows)
        for hh in range(hp):
            sl = slice(hh * NA_HEAD_DIM, (hh + 1) * NA_HEAD_DIM)
            q = q_ref[:, sl]
            s_loc = (lax.dot_general(q, k_ref[pl.ds(st, NA_KEYS), sl], NT, preferred_element_type=F32) * scale
                     + _na_bias_tile(b_ref, hh, off))
            s_ctx = lax.dot_general(q, kc_ref[:, sl], NT, preferred_element_type=F32) * scale
            m = jnp.maximum(jnp.max(s_loc, axis=-1, keepdims=True), jnp.max(s_ctx, axis=-1, keepdims=True))
            p_loc, p_ctx = jnp.exp(s_loc - m), jnp.exp(s_ctx - m)
            l = jnp.sum(p_loc, axis=-1, keepdims=True) + jnp.sum(p_ctx, axis=-1, keepdims=True)
            o = (jnp.dot(p_loc.astype(BF16), v_ref[pl.ds(st, NA_KEYS), sl], preferred_element_type=F32)
                 + jnp.dot(p_ctx.astype(BF16), vc_ref[:, sl], preferred_element_type=F32))
            o_ref[:, sl] = (o / l).astype(BF16)
            lse_ref[hh] = m + jnp.log(l)

    outs, couts = _call(
        body, [qkv, qkv, qkv, kvc, kvc, bias], name=name, grid=(G, rows),
        in_specs=[pl.BlockSpec((GRID_W, wd), lambda h, r: (r, h)),
                  pl.BlockSpec((L, wd), lambda h, r: (0, G + h)),
                  pl.BlockSpec((L, wd), lambda h, r: (0, 2 * G + h)),
                  pl.BlockSpec((Lc, wd), lambda h, r: (0, h)),
                  pl.BlockSpec((Lc, wd), lambda h, r: (0, G + h)),
                  pl.BlockSpec((hp,) + bias.shape[1:], lambda h, r: (h, 0, 0, 0))],
        out_specs=[pl.BlockSpec((GRID_W, wd), lambda h, r: (r, h)),
                   pl.BlockSpec((hp, GRID_W, 1), lambda h, r: (h, r, 0))],
        out_shape=[jax.ShapeDtypeStruct((L, NA), BF16), jax.ShapeDtypeStruct((H, L, 1), F32)],
        sem=("parallel", "arbitrary"), comm=comm)
    return outs if comm is None else (outs, couts)


def natten_bwd(name, qkv, kvc, bias, o, lse, dmix, comm=None):
    L, NA, H, hp, G, rows, Lc = _na_dims(qkv, kvc)
    scale = NA_HEAD_DIM ** -0.5
    wd = hp * NA_HEAD_DIM

    def body(q_ref, k_ref, v_ref, kc_ref, vc_ref, b_ref, o_ref, lse_ref, do_ref,
             dq_ref, dk_ref, dv_ref, dkc_ref, dvc_ref, db_ref):
        r = pl.program_id(1)
        st, off = _na_window(r, rows)

        @pl.when(r == 0)
        def _():
            for ref in (dk_ref, dv_ref, dkc_ref, dvc_ref, db_ref):
                ref[...] = jnp.zeros_like(ref)

        for hh in range(hp):
            sl = slice(hh * NA_HEAD_DIM, (hh + 1) * NA_HEAD_DIM)
            q, kl, vl, kc, vc = q_ref[:, sl], k_ref[pl.ds(st, NA_KEYS), sl], v_ref[pl.ds(st, NA_KEYS), sl], kc_ref[:, sl], vc_ref[:, sl]
            do = do_ref[:, sl]
            lse_v = lse_ref[hh]
            p_loc = jnp.exp(lax.dot_general(q, kl, NT, preferred_element_type=F32) * scale + _na_bias_tile(b_ref, hh, off) - lse_v)
            p_ctx = jnp.exp(lax.dot_general(q, kc, NT, preferred_element_type=F32) * scale - lse_v)
            delta = jnp.sum(do.astype(F32) * o_ref[:, sl].astype(F32), axis=-1, keepdims=True)
            ds_loc = p_loc * (lax.dot_general(do, vl, NT, preferred_element_type=F32) - delta)
            ds_ctx = p_ctx * (lax.dot_general(do, vc, NT, preferred_element_type=F32) - delta)
            dsl, dsc = ds_loc.astype(BF16), ds_ctx.astype(BF16)
            dq = jnp.dot(dsl, kl, preferred_element_type=F32) + jnp.dot(dsc, kc, preferred_element_type=F32)
            dq_ref[:, sl] = (dq * scale).astype(BF16)
            dk_ref[pl.ds(st, NA_KEYS), sl] += lax.dot_general(dsl, q, TN, preferred_element_type=F32) * scale
            dv_ref[pl.ds(st, NA_KEYS), sl] += lax.dot_general(p_loc.astype(BF16), do, TN, preferred_element_type=F32)
            dkc_ref[:, sl] += lax.dot_general(dsc, q, TN, preferred_element_type=F32) * scale
            dvc_ref[:, sl] += lax.dot_general(p_ctx.astype(BF16), do, TN, preferred_element_type=F32)
            for j in range(NA_PAIRS):
                db_ref[hh, NA_WIN_R - 1 - off + 2 * j] += ds_loc[:, 2 * j * GRID_W:(2 * j + 2) * GRID_W]

    tok = pl.BlockSpec((GRID_W, wd), lambda h, r: (r, h))
    bia = pl.BlockSpec((hp,) + bias.shape[1:], lambda h, r: (h, 0, 0, 0))
    outs, couts = _call(
        body, [qkv, qkv, qkv, kvc, kvc, bias, o, lse, dmix], name=name, grid=(G, rows),
        in_specs=[tok,
                  pl.BlockSpec((L, wd), lambda h, r: (0, G + h)),
                  pl.BlockSpec((L, wd), lambda h, r: (0, 2 * G + h)),
                  pl.BlockSpec((Lc, wd), lambda h, r: (0, h)),
                  pl.BlockSpec((Lc, wd), lambda h, r: (0, G + h)),
                  bia,
                  tok,
                  pl.BlockSpec((hp, GRID_W, 1), lambda h, r: (h, r, 0)),
                  pl.BlockSpec((GRID_W, wd), lambda h, r: (r, G + h))],
        out_specs=[tok,
                   pl.BlockSpec((L, wd), lambda h, r: (0, h)),
                   pl.BlockSpec((L, wd), lambda h, r: (0, h)),
                   pl.BlockSpec((Lc, wd), lambda h, r: (0, h)),
                   pl.BlockSpec((Lc, wd), lambda h, r: (0, h)),
                   bia],
        out_shape=[jax.ShapeDtypeStruct((L, NA), BF16), jax.ShapeDtypeStruct((L, NA), F32), jax.ShapeDtypeStruct((L, NA), F32),
                   jax.ShapeDtypeStruct((Lc, NA), F32), jax.ShapeDtypeStruct((Lc, NA), F32),
                   jax.ShapeDtypeStruct(bias.shape, F32)],
        sem=("parallel", "arbitrary"), comm=comm)
    return outs if comm is None else (outs, couts)


def _s5_dims(T, N):
    TC = T // S5_SEG
    assert T % (S5_SEG * SUB * 2) == 0 and N % S5_STRIP == 0
    return TC, TC // SUB, S5_SEG, N // S5_STRIP


def _s5_backward(d, rev):
    return (d == 1) != rev


def s5_scan(name, xin, mats, a, rev, comm=None):
    _, T, W = xin.shape
    N = a.shape[-1]
    TC, NG, NCH, NS = _s5_dims(T, N)
    CW, SL = W // NS, S5_STRIP

    def ck(d, k):
        return jnp.where(_s5_backward(d, rev), NCH - 1 - k, k)

    def body(x_ref, m_ref, a_ref, h_ref, f_ref, carry, hs):
        @pl.when(pl.program_id(2) == 0)
        def _():
            carry[...] = jnp.zeros_like(carry)

        xb = x_ref[...].astype(BF16)
        hs[0] = jnp.dot(xb, m_ref[0], preferred_element_type=F32)
        hs[1] = jnp.dot(xb, m_ref[1], preferred_element_type=F32)
        ar, ai = jnp.broadcast_to(a_ref[0], (SUB, SL)), jnp.broadcast_to(a_ref[1], (SUB, SL))
        bw = _s5_backward(pl.program_id(0), rev)

        def step(t, c):
            hr, hi = c
            row = pl.multiple_of(jnp.where(bw, NG - 1 - t, t) * SUB, SUB)
            nr = ar * hr - ai * hi + hs[0, pl.ds(row, SUB), :]
            ni = ar * hi + ai * hr + hs[1, pl.ds(row, SUB), :]
            hs[0, pl.ds(row, SUB), :] = nr
            hs[1, pl.ds(row, SUB), :] = ni
            return nr, ni

        hr, hi = lax.fori_loop(0, NG, step, (carry[0], carry[1]))
        carry[0], carry[1] = hr, hi
        f_ref[0], f_ref[1] = hr, hi
        h_ref[...] = hs[...].astype(BF16)

    outs, couts = _call(
        body, [xin, mats, a], name=name, grid=(2, NS, NCH),
        in_specs=[pl.BlockSpec((None, TC, CW), lambda d, j, k: (d, ck(d, k), j)),
                  pl.BlockSpec((None, 2, None, CW, SL), lambda d, j, k: (d, 0, j, 0, 0)),
                  pl.BlockSpec((None, 2, 1, SL), lambda d, j, k: (d, 0, 0, j))],
        out_specs=[pl.BlockSpec((None, 2, TC, SL), lambda d, j, k: (d, 0, ck(d, k), j)),
                   pl.BlockSpec((None, 2, SUB, SL), lambda d, j, k: (d, 0, 0, j))],
        out_shape=[jax.ShapeDtypeStruct((2, 2, T, N), BF16), jax.ShapeDtypeStruct((2, 2, SUB, N), F32)],
        scratch_shapes=[pltpu.VMEM((2, SUB, SL), F32), pltpu.VMEM((2, TC, SL), F32)],
        sem=("parallel", "parallel", "arbitrary"), comm=comm)
    return outs if comm is None else (outs, couts)


def s5_fix(name, hloc, hin, a, mats, rev, comm=None):
    _, _, T, N = hloc.shape
    TC, NG, NCH, NS = _s5_dims(T, N)
    SL = S5_STRIP
    CW = mats.shape[-1]

    def ck(d, k):
        return jnp.where(_s5_backward(d, rev), NCH - 1 - k, k)

    def body(h_ref, hin_ref, a_ref, m_ref, ho_ref, y_ref, g, hs):
        @pl.when(pl.program_id(2) == 0)
        def _():
            g[...] = hin_ref[...]

        hs[...] = h_ref[...].astype(F32)
        ar, ai = jnp.broadcast_to(a_ref[0], (SUB, SL)), jnp.broadcast_to(a_ref[1], (SUB, SL))
        bw = _s5_backward(pl.program_id(0), rev)

        def step(t, c):
            gr, gi = c
            row = pl.multiple_of(jnp.where(bw, NG - 1 - t, t) * SUB, SUB)
            nr = ar * gr - ai * gi
            ni = ar * gi + ai * gr
            hs[0, pl.ds(row, SUB), :] += nr
            hs[1, pl.ds(row, SUB), :] += ni
            return nr, ni

        gr, gi = lax.fori_loop(0, NG, step, (g[0], g[1]))
        g[0], g[1] = gr, gi
        hb = hs[...].astype(BF16)
        ho_ref[...] = hb
        y_ref[...] = (jnp.dot(hb[0], m_ref[0], preferred_element_type=F32)
                      + jnp.dot(hb[1], m_ref[1], preferred_element_type=F32))

    outs, couts = _call(
        body, [hloc, hin, a, mats], name=name, grid=(2, NS, NCH),
        in_specs=[pl.BlockSpec((None, 2, TC, SL), lambda d, j, k: (d, 0, ck(d, k), j)),
                  pl.BlockSpec((None, 2, SUB, SL), lambda d, j, k: (d, 0, 0, j)),
                  pl.BlockSpec((None, 2, 1, SL), lambda d, j, k: (d, 0, 0, j)),
                  pl.BlockSpec((None, 2, None, SL, CW), lambda d, j, k: (d, 0, j, 0, 0))],
        out_specs=[pl.BlockSpec((None, 2, TC, SL), lambda d, j, k: (d, 0, ck(d, k), j)),
                   pl.BlockSpec((None, TC, CW), lambda d, j, k: (d, ck(d, k), j))],
        out_shape=[jax.ShapeDtypeStruct((2, 2, T, N), BF16), jax.ShapeDtypeStruct((2, T, NS * CW), F32)],
        scratch_shapes=[pltpu.VMEM((2, SUB, SL), F32), pltpu.VMEM((2, TC, SL), F32)],
        sem=("parallel", "parallel", "arbitrary"), comm=comm)
    return outs if comm is None else (outs, couts)


def s5_grads(name, g, h, u, dy, comm=None):
    _, _, T, N = g.shape
    W = u.shape[-1]
    TC, NG, NCH, NS = _s5_dims(T, N)
    CW, SL = W // NS, S5_STRIP

    def body(g_ref, h_ref, hp_ref, hl_ref, u_ref, dy_ref, dm_ref, dc_ref, da_ref, hs):
        k = pl.program_id(2)
        sub = lax.broadcasted_iota(jnp.int32, (SUB, SL), 0)

        hf = h_ref[...].astype(F32)

        @pl.when(pl.program_id(0) == 0)
        def _():
            for z in range(2):
                wrapped = jnp.where(sub == 0, 0.0, pltpu.roll(hl_ref[z].astype(F32)[SUB:], 1, 0))
                hs[z, 0:SUB, :] = jnp.where(k == 0, wrapped, hp_ref[z].astype(F32)[SUB:])
                hs[z, SUB:TC, :] = hf[z, 0:TC - SUB]

        @pl.when(pl.program_id(0) == 1)
        def _():
            for z in range(2):
                wrapped = jnp.where(sub == SUB - 1, 0.0, pltpu.roll(hl_ref[z].astype(F32)[:SUB], SUB - 1, 0))
                hs[z, TC - SUB:TC, :] = jnp.where(k == NCH - 1, wrapped, hp_ref[z].astype(F32)[:SUB])
                hs[z, 0:TC - SUB, :] = hf[z, SUB:TC]

        gr, gi, pr, pi = g_ref[0].astype(F32), g_ref[1].astype(F32), hs[0], hs[1]
        dar = jnp.sum((gr * pr + gi * pi).reshape(NG, SUB, SL), axis=0)
        dai = jnp.sum((gi * pr - gr * pi).reshape(NG, SUB, SL), axis=0)
        ub, dyb = u_ref[...].astype(BF16), dy_ref[...].astype(BF16)
        dm = [lax.dot_general(ub, g_ref[z], TN, preferred_element_type=F32) for z in range(2)]
        dc = [lax.dot_general(dyb, h_ref[z], TN, preferred_element_type=F32) for z in range(2)]

        @pl.when(k == 0)
        def _():
            da_ref[0], da_ref[1] = dar, dai
            for z in range(2):
                dm_ref[z], dc_ref[z] = dm[z], dc[z]

        @pl.when(k > 0)
        def _():
            da_ref[0] += dar
            da_ref[1] += dai
            for z in range(2):
                dm_ref[z] += dm[z]
                dc_ref[z] += dc[z]

    big = pl.BlockSpec((None, 2, TC, SL), lambda d, j, k: (d, 0, k, j))
    tok = pl.BlockSpec((None, TC, CW), lambda d, j, k: (d, k, j))
    mat = pl.BlockSpec((None, 2, None, CW, SL), lambda d, j, k: (d, 0, j, 0, 0))
    outs, couts = _call(
        body, [g, h, h, h, u, dy], name=name, grid=(2, NS, NCH),
        in_specs=[big, big,
                  pl.BlockSpec((None, 2, 2 * SUB, SL), lambda d, j, k: (
                      d, 0, jnp.where(d == 0, jnp.maximum(k * NG - 1, 0), jnp.minimum((k + 1) * NG, T // SUB - 1)) // 2, j)),
                  pl.BlockSpec((None, 2, 2 * SUB, SL), lambda d, j, k: (d, 0, jnp.where(d == 0, T // SUB - 1, 0) // 2, j)),
                  tok, tok],
        out_specs=[mat, mat, pl.BlockSpec((None, 2, SUB, SL), lambda d, j, k: (d, 0, 0, j))],
        out_shape=[jax.ShapeDtypeStruct((2, 2, NS, CW, SL), F32), jax.ShapeDtypeStruct((2, 2, NS, CW, SL), F32),
                   jax.ShapeDtypeStruct((2, 2, SUB, N), F32)],
        scratch_shapes=[pltpu.VMEM((2, TC, SL), F32)],
        sem=("parallel", "parallel", "arbitrary"), comm=comm)
    return outs if comm is None else (outs, couts)


def _interleave(seq):
    *lead, T, W = seq.shape
    n = len(lead)
    return seq.reshape(*lead, S5_SEG, T // S5_SEG, W).swapaxes(n, n + 1).reshape(*lead, T, W)


def _deinterleave(seq):
    *lead, T, W = seq.shape
    n = len(lead)
    return seq.reshape(*lead, T // S5_SEG, S5_SEG, W).swapaxes(n, n + 1).reshape(*lead, T, W)


def _s5_discretize(lam_re, lam_im, log_dt, b_re, b_im):
    dt = jnp.exp(log_dt)[..., None]
    mag = jnp.exp(lam_re * dt)
    a_re = mag * jnp.cos(lam_im * dt)
    a_im = mag * jnp.sin(lam_im * dt)
    den = jnp.square(lam_re) + jnp.square(lam_im)
    f_re = ((a_re - 1.0) * lam_re + a_im * lam_im) / den
    f_im = (a_im * lam_re - (a_re - 1.0) * lam_im) / den
    bb_re = f_re[..., None] * b_re - f_im[..., None] * b_im
    bb_im = f_re[..., None] * b_im + f_im[..., None] * b_re
    return a_re, a_im, bb_re, bb_im


_GPS = S5_STRIP // SSM_STATE


def _blockdiag(t):
    d2, G, P, Cg = t.shape
    t5 = t.reshape(d2, G // _GPS, _GPS, P, Cg).transpose(0, 1, 2, 4, 3)
    m = t5[:, :, :, :, None, :] * jnp.eye(_GPS, dtype=t.dtype)[None, None, :, None, :, None]
    return m.reshape(d2, G // _GPS, _GPS * Cg, _GPS * P)


def _blockdiag_extract(m, Cg, P):
    d2, NS = m.shape[:2]
    m6 = m.reshape(d2, NS, _GPS, Cg, _GPS, P)
    diag = jnp.stack([m6[:, :, i, :, i, :] for i in range(_GPS)], axis=2)
    return diag.transpose(0, 1, 2, 4, 3).reshape(d2, NS * _GPS, P, Cg)


def _cmul(a, b):
    return a[0] * b[0] - a[1] * b[1], a[0] * b[1] + a[1] * b[0]


def _cpow(a, n):
    out, base = None, a
    while n:
        if n & 1:
            out = base if out is None else _cmul(out, base)
        base = _cmul(base, base)
        n >>= 1
    return out


def _segment_carry(fin, apow, rev):
    per_dir = []
    for d in range(2):
        fr, fi = fin[d, 0], fin[d, 1]
        ap = (apow[0][d], apow[1][d])
        cr = ci = jnp.zeros_like(fr[0:1])
        outs = [None] * S5_SEG
        backward = (d == 1) != rev
        for s in (range(S5_SEG - 1, -1, -1) if backward else range(S5_SEG)):
            outs[s] = (cr, ci)
            pr, pi = _cmul(ap, (cr, ci))
            cr, ci = pr + fr[s:s + 1], pi + fi[s:s + 1]
        per_dir.append(jnp.stack([jnp.concatenate([o[0] for o in outs]), jnp.concatenate([o[1] for o in outs])]))
    return jnp.stack(per_dir)


def _coords():
    x, y, c = lax.axis_index("x"), lax.axis_index("y"), lax.axis_index("c")
    others = [(1 - x, y), (x, 1 - y), (1 - x, 1 - y)]
    return x, y, c, 2 * x + y, others


def _comm(name, ins, out_shapes, aliases, n_local, n_remote, plan):
    n_in, n_out = len(ins), len(out_shapes)

    def body(*refs):
        in_refs, out_refs = refs[:n_in], refs[n_in:n_in + n_out]
        send_sems, recv_sems, local_sems = refs[n_in + n_out:]
        x, y, c = lax.axis_index("x"), lax.axis_index("y"), lax.axis_index("c")
        locs, sends, lands = plan(in_refs, out_refs)
        assert len(locs) == n_local and len(sends) == n_remote and len(lands) == n_remote
        local = [pltpu.make_async_copy(s, d, local_sems.at[i]) for i, (s, d) in enumerate(locs)]
        for cp in local:
            cp.start()
        remote = [pltpu.make_async_remote_copy(src_ref=s, dst_ref=d, send_sem=send_sems.at[i], recv_sem=recv_sems.at[i],
                                               device_id=peer, device_id_type=MESH)
                  for i, (s, d, peer) in enumerate(sends)]
        for cp in remote:
            cp.start()
        for i, d in enumerate(lands):
            pltpu.make_async_remote_copy(src_ref=d, dst_ref=d, send_sem=send_sems.at[i], recv_sem=recv_sems.at[i],
                                         device_id=(x, y, c), device_id_type=MESH).wait_recv()
        for cp in remote:
            cp.wait_send()
        for cp in local:
            cp.wait()

    any_spec = pl.BlockSpec(memory_space=pl.ANY)
    return pl.pallas_call(
        body, name=name,
        in_specs=[any_spec] * n_in, out_specs=[any_spec] * n_out,
        out_shape=[jax.ShapeDtypeStruct(s, d) for s, d in out_shapes],
        input_output_aliases=aliases,
        scratch_shapes=[pltpu.SemaphoreType.DMA((n_remote,)), pltpu.SemaphoreType.DMA((n_remote,)),
                        pltpu.SemaphoreType.DMA((max(n_local, 1),))],
        compiler_params=pltpu.CompilerParams(has_side_effects=True),
    )(*ins)


def allgather_dev(name, v):
    M, Nc = v.shape

    def plan(in_refs, out_refs):
        (v_ref,), (o_ref,) = in_refs, out_refs
        x, y, c = lax.axis_index("x"), lax.axis_index("y"), lax.axis_index("c")

        def rows(px, py, pc):
            return o_ref.at[pl.ds((4 * px + 2 * py + pc) * M, M), :]

        peers = [(x ^ fx, y ^ fy, c ^ fc) for fx in (0, 1) for fy in (0, 1) for fc in (0, 1) if fx or fy or fc]
        return ([(v_ref, rows(x, y, c))],
                [(v_ref, rows(x, y, c), p) for p in peers],
                [rows(*p) for p in peers])

    return _comm(name, [v], [((N_DEV * M, Nc), v.dtype)], {}, 1, N_DEV - 1, plan)[0]


def allgather_chips_1(name, shards):
    def plan(in_refs, out_refs):
        x, y, c, chip, others = _coords()
        sends, lands = [], []
        for s_ref, g_ref in zip(in_refs, out_refs):
            hr = s_ref.shape[0] // 2
            mine = pl.ds(c * hr, hr)
            for qx, qy in others:
                sends.append((s_ref.at[mine], g_ref.at[chip, mine], (qx, qy, c)))
                lands.append(g_ref.at[2 * qx + qy, mine])
        return [], sends, lands

    n = len(shards)
    comm = (list(shards), [((N_CHIP,) + s.shape, s.dtype) for s in shards], {}, 3 * n, plan)
    return comm if name is None else _comm(name, comm[0], comm[1], comm[2], 0, comm[3], comm[4])


def allgather_chips_2(name, gathered, shards):
    n = len(gathered)

    def plan(in_refs, out_refs):
        x, y, c, chip, others = _coords()
        sends, lands = [], []
        for s_ref, g_ref in zip(in_refs[n:], out_refs):
            hr = g_ref.shape[1] // 2
            for qx, qy in others:
                q = 2 * qx + qy
                sends.append((g_ref.at[q, pl.ds(c * hr, hr)], g_ref.at[q, pl.ds(c * hr, hr)], (x, y, 1 - c)))
                lands.append(g_ref.at[q, pl.ds((1 - c) * hr, hr)])
            sends.append((s_ref, g_ref.at[chip], (x, y, 1 - c)))
            lands.append(g_ref.at[chip])
        return [], sends, lands

    comm = (list(gathered) + list(shards), [(g.shape, g.dtype) for g in gathered], {i: i for i in range(n)}, 4 * n, plan)
    return comm if name is None else _comm(name, comm[0], comm[1], comm[2], 0, comm[3], comm[4])


def reduce_1(name, grads):
    def plan(in_refs, out_refs):
        x, y, c, chip, others = _coords()
        sends, lands = [], []
        for g_ref, got_ref in zip(in_refs, out_refs):
            hr = g_ref.shape[1] // 2
            sends.append((g_ref.at[:, pl.ds((1 - c) * hr, hr), :], got_ref, (x, y, 1 - c)))
            lands.append(got_ref)
        return [], sends, lands

    n = len(grads)
    comm = (list(grads), [((g.shape[0], g.shape[1] // 2, g.shape[2]), g.dtype) for g in grads], {}, n, plan)
    return comm if name is None else _comm(name, comm[0], comm[1], comm[2], 0, comm[3], comm[4])


def _merge_comm(a, b):
    if a is None or b is None:
        return a if b is None else b
    na_in, na_out = len(a[0]), len(a[1])

    def plan(in_refs, out_refs):
        _, s1, l1 = a[4](in_refs[:na_in], out_refs[:na_out])
        _, s2, l2 = b[4](in_refs[na_in:], out_refs[na_out:])
        return [], s1 + s2, l1 + l2

    alias = dict(a[2])
    alias.update({na_in + i: na_out + j for i, j in b[2].items()})
    return (a[0] + b[0], a[1] + b[1], alias, a[3] + b[3], plan)


def reduce_2(name, parts):
    def plan(in_refs, out_refs):
        x, y, c, chip, others = _coords()
        sends, lands = [], []
        for t_ref, q_ref in zip(in_refs, out_refs):
            for qx, qy in others:
                sends.append((t_ref.at[2 * qx + qy], q_ref.at[chip], (qx, qy, c)))
                lands.append(q_ref.at[2 * qx + qy])
        return [], sends, lands

    n = len(parts)
    comm = (list(parts), [(p.shape, p.dtype) for p in parts], {}, 3 * n, plan)
    return comm if name is None else _comm(name, comm[0], comm[1], comm[2], 0, comm[3], comm[4])


def share_slots(name, slots):
    def plan(in_refs, out_refs):
        x, y, c, chip, others = _coords()
        (q_ref,) = out_refs
        return ([], [(q_ref.at[chip], q_ref.at[chip], (qx, qy, c)) for qx, qy in others],
                [q_ref.at[2 * qx + qy] for qx, qy in others])

    return _comm(name, [slots], [(slots.shape, slots.dtype)], {0: 0}, 0, N_CHIP - 1, plan)[0]


def allreduce_small(tag, buf, ids):
    got = reduce_1(tag + "_1", [buf[None]])[0][0]
    slots = share_slots(tag + "_2", pair_sum_to_slot(tag + "_add", buf, got, ids))
    full = reduce_3(tag + "_3", [sum_chips_to_half(tag + "_sum", slots, ids)])[0]
    return full.reshape(buf.shape)


def reduce_3(name, fulls):
    def plan(in_refs, out_refs):
        x, y, c, chip, others = _coords()
        sends, lands = [], []
        for o_ref in out_refs:
            sends.append((o_ref.at[c], o_ref.at[c], (x, y, 1 - c)))
            lands.append(o_ref.at[1 - c])
        return [], sends, lands

    n = len(fulls)
    return _comm(name, fulls, [(f.shape, f.dtype) for f in fulls], {i: i for i in range(n)}, 0, n, plan)


_WEIGHTS = ['c_ctx', 'w_mod', 'b_mod', 'g_mix', 'g_ffn', 'w_in', 'ssm_lam_re', 'ssm_lam_im', 'ssm_log_dt', 'ssm_b_re',
            'ssm_b_im', 'ssm_c_re', 'ssm_c_im', 'ssm_d', 'ssm_w_glu', 'na_rpb', 'w_out', 'cv_w_pw1', 'cv_dw_w', 'cv_dw_b',
            'cv_ln_g', 'cv_ln_b', 'cv_w_pw2', 'ffn_w_up', 'ffn_conv_w', 'ffn_conv_b', 'ffn_w_down', 'g_out']
_INPUTS = ['x', 'c', 'ctx'] + _WEIGHTS + ['loss_target'] + ['m_' + w for w in _WEIGHTS] + ['v_' + w for w in _WEIGHTS]
_GATHERED_SMALL = ['ffn_conv_w', 'cv_dw_w', 'cv_dw_b', 'cv_ln_g', 'cv_ln_b']


def _silu(v):
    return v * jax.nn.sigmoid(v)


def _pack(arrs, cols, row_mult=SUB):
    flat = jnp.concatenate([a.reshape(-1).astype(F32) for a in arrs])
    n = flat.shape[0]
    unit = row_mult * cols
    flat = jnp.pad(flat, (0, (-n) % unit))
    return flat.reshape(-1, cols)


def _unpack(buf, shapes):
    flat = buf.reshape(-1)
    out, o = [], 0
    for s in shapes:
        n = int(np.prod(s))
        out.append(flat[o:o + n].reshape(s))
        o += n
    return out


def _carried(res, comm):
    return res if comm is not None else (res, [])


def _ffn_fwd(tag, xin, sh, sc, gt, g, wup, cw3, cb3, wdn, comm_up=None, comm_mid=None, comm_down=None):
    hf = norm_mod_fwd(tag + "_norm", xin, g * (1.0 + sc), sh)
    up3, got_up = _carried(mm_nn_pieces(tag + "_up", hf, wup, 0, N_CHIP, BF16, halves=2, comm=comm_up), comm_up)
    comm_mid = comm_mid(got_up) if callable(comm_mid) else comm_mid
    act, got_mid = _carried(ffn_mid_fwd(tag + "_mid", up3, cw3, cb3, comm=comm_mid), comm_mid)
    comm_down = comm_down(got_mid) if callable(comm_down) else comm_down
    yf, got_down = _carried(mm_nn(tag + "_down", act, wdn, BF16, comm=comm_down), comm_down)
    return gate_res_fwd(tag + "_res", xin, yf, gt), (xin, hf, up3, act, yf), got_up, got_mid, got_down


def _ffn_bwd(tag, dxo, saved, sc, gt, g, wup, cw3, cb3, wdn, comm_down=None, comm_mid=None):
    xin, hf, up3, act, yf = saved
    dyf, dgt = gate_res_bwd(tag + "_res_b", dxo, yf, gt)
    dact, got_down = _carried(mm_nt(tag + "_down_bx", dyf, wdn, BF16, comm=comm_down), comm_down)
    dwdn = mm_tn(tag + "_down_bw", act, dyf, BF16)
    comm_mid = comm_mid(got_down) if callable(comm_mid) else comm_mid
    (dup3, dcw3, dcb3), got_mid = _carried(ffn_mid_bwd(tag + "_mid_b", up3, dact, cw3, cb3, comm=comm_mid), comm_mid)
    dhf = mm_nt_pieces(tag + "_up_bx", dup3, wup, BF16, halves=2)
    dwup = mm_tn_pieces(tag + "_up_bw", hf, dup3, N_CHIP, BF16, halves=2)
    dxi, cs1, cs2 = norm_mod_bwd(tag + "_norm_b", xin, dhf, g * (1.0 + sc), dxo)
    return dxi, dict(dsh=cs1[0], dsc=cs2[0] * g, dgt=dgt[0], dg=cs2[0] * (1.0 + sc), dwup=dwup, dwdn=dwdn,
                     dcw=dcw3.transpose(1, 0, 2).reshape(3, -1), dcb=dcb3.reshape(-1)), got_down, got_mid


def kernel(x, c, ctx, c_ctx, w_mod, b_mod, g_mix, g_ffn, w_in, ssm_lam_re, ssm_lam_im, ssm_log_dt, ssm_b_re, ssm_b_im, ssm_c_re, ssm_c_im, ssm_d, ssm_w_glu, na_rpb, w_out, cv_w_pw1, cv_dw_w, cv_dw_b, cv_ln_g, cv_ln_b, cv_w_pw2, ffn_w_up, ffn_conv_w, ffn_conv_b, ffn_w_down, g_out, loss_target, m_c_ctx, m_w_mod, m_b_mod, m_g_mix, m_g_ffn, m_w_in, m_ssm_lam_re, m_ssm_lam_im, m_ssm_log_dt, m_ssm_b_re, m_ssm_b_im, m_ssm_c_re, m_ssm_c_im, m_ssm_d, m_ssm_w_glu, m_na_rpb, m_w_out, m_cv_w_pw1, m_cv_dw_w, m_cv_dw_b, m_cv_ln_g, m_cv_ln_b, m_cv_w_pw2, m_ffn_w_up, m_ffn_conv_w, m_ffn_conv_b, m_ffn_w_down, m_g_out, v_c_ctx, v_w_mod, v_b_mod, v_g_mix, v_g_ffn, v_w_in, v_ssm_lam_re, v_ssm_lam_im, v_ssm_log_dt, v_ssm_b_re, v_ssm_b_im, v_ssm_c_re, v_ssm_c_im, v_ssm_d, v_ssm_w_glu, v_na_rpb, v_w_out, v_cv_w_pw1, v_cv_dw_w, v_cv_dw_b, v_cv_ln_g, v_cv_ln_b, v_cv_w_pw2, v_ffn_w_up, v_ffn_conv_w, v_ffn_conv_b, v_ffn_w_down, v_g_out):
    p = dict(locals())
    xi, yi, ci = lax.axis_index("x"), lax.axis_index("y"), lax.axis_index("c")
    me, chip = 4 * xi + 2 * yi + ci, 2 * xi + yi
    xs, cx, tgt = x[0], ctx[0], loss_target[0]
    L, D = xs.shape
    Lc = cx.shape[0]
    T = L + Lc
    W = D // 2
    Cq = w_mod.shape[2]

    s_mix = [t.astype(BF16) for t in (w_in[0], ssm_w_glu[0], w_out[0])]
    s_ffn0 = [t.astype(BF16) for t in (ffn_w_up[0], ffn_w_down[0])]
    s_conv = [t.astype(BF16) for t in (cv_w_pw1[0], cv_w_pw2[0])]
    s_ffn1 = [t.astype(BF16) for t in (ffn_w_up[1], ffn_w_down[1])]
    (Win,) = allgather_chips_2("gather_win_2", allgather_chips_1("gather_win_1", s_mix[:1]), s_mix[:1])
    Fd = ffn_w_down.shape[1] * N_CHIP
    c_idx = jnp.reshape(ci, (1,)).astype(jnp.int32)
    ids = jnp.stack([chip, ci]).astype(jnp.int32)

    def added(tag, grads, got):
        return [add_half("reduce_%s_add%d" % (tag, i), g, r, c_idx) for i, (g, r) in enumerate(zip(grads, got))]

    small_shapes = [p[n].shape for n in _GATHERED_SMALL]
    sm = allgather_dev("gather_small", _pack([p[n] for n in _GATHERED_SMALL], 1024))
    sm = sm.reshape(N_DEV, -1)[0::2]
    per_chip = [_unpack(sm[q], small_shapes) for q in range(N_CHIP)]
    conv_w_f, dw_w_f, dw_b_f, ln_g_f, ln_b_f = (jnp.concatenate([pc[i] for pc in per_chip], axis=-1)
                                                for i in range(len(_GATHERED_SMALL)))
    cw3 = [conv_w_f[l].reshape(3, 2, Fd).transpose(1, 0, 2) for l in range(2)]
    cb3 = [ffn_conv_b[l].reshape(2, 1, Fd) for l in range(2)]
    dw_w_f, dw_b_f, ln_g_f, ln_b_f = dw_w_f[0], dw_b_f[0], ln_g_f[0], ln_b_f[0]

    c_all = allgather_dev("gather_c", jnp.zeros((SUB, D), F32).at[0].set(c[0])).reshape(N_DEV, SUB, D)[:, 0]
    S16 = jnp.concatenate([_silu(c_all), _silu(c_ctx)[None], jnp.zeros((2 * SUB - N_DEV - 1, D), F32)])
    modp = mm_nn_pieces("mod_fwd", S16, w_mod, 0, 2, F32)
    modg = allgather_dev("gather_mod", modp).reshape(N_DEV, 2 * SUB, 2, Cq)[0::2]

    def mod_row(r):
        return r.transpose(1, 0, 2).reshape(2, N_CHIP * Cq) + b_mod

    mod_me = mod_row(lax.dynamic_index_in_dim(modg, me, axis=1, keepdims=False))
    mod_c = mod_row(modg[:, N_DEV])
    mods = [[mod_me[l, i * D:(i + 1) * D] for i in range(N_MOD)] for l in range(2)]
    shc, scc = mod_c[0, :D], mod_c[0, D:2 * D]

    sh_m, sc_m, gt_m, sh_f, sc_f, gt_f = mods[0]
    h0 = norm_mod_fwd("l0_norm", xs, g_mix[0] * (1.0 + sc_m), sh_m)
    hc0 = norm_mod_fwd("l0_norm_c", cx, g_mix[0] * (1.0 + scc), shc)
    u = mm_nn_pieces("l0_in_u", h0, Win, 0, 1, F32)
    qkv, g_mix1 = mm_nn_pieces("l0_in_qkv", h0, Win, 1, 3, BF16, comm=allgather_chips_1(None, s_mix[1:]))
    uc = mm_nn_pieces("l0_in_uc", hc0, Win, 0, 1, F32)
    kvc = mm_nn_pieces("l0_in_kvc", hc0, Win, 2, 2, BF16)

    lam_re, lam_im, log_dt = ssm_lam_re[0], ssm_lam_im[0], ssm_log_dt[0]
    b_re, b_im, c_re, c_im = ssm_b_re[0], ssm_b_im[0], ssm_c_re[0], ssm_c_im[0]
    (a_re, a_im, bb_re, bb_im), disc_vjp = jax.vjp(_s5_discretize, lam_re, lam_im, log_dt, b_re, b_im)
    G, P, Cg = bb_re.shape[1:]
    N = G * P
    a_re, a_im = a_re.reshape(2, 1, N), a_im.reshape(2, 1, N)
    a_f, a_b = jnp.stack([a_re, a_im], axis=1), jnp.stack([a_re, -a_im], axis=1)
    Bblk = jnp.stack([_blockdiag(bb_re), _blockdiag(bb_im)], axis=1)
    Cblk = jnp.stack([_blockdiag(c_re.swapaxes(-1, -2)), -_blockdiag(c_im.swapaxes(-1, -2))], axis=1)
    apow = _cpow((a_re, a_im), T // S5_SEG)

    useq = _interleave(jnp.stack([jnp.concatenate([uc, u]), jnp.concatenate([u, uc])]))
    (hloc, fin), (Wglu, Wout) = s5_scan("s5_scan", useq, Bblk.astype(BF16), a_f, rev=False,
                                        comm=allgather_chips_2(None, g_mix1, s_mix[1:]))
    Wglu, Wout = Wglu.reshape(-1, Wglu.shape[-1]), Wout.reshape(-1, D)
    (hst, yseq), g_dn0 = s5_fix("s5_fix", hloc, _segment_carry(fin, apow, False), a_f, Cblk.swapaxes(-1, -2).astype(BF16),
                                rev=False, comm=allgather_chips_1(None, s_ffn0[1:]))
    ys = _deinterleave(yseq)
    y0, y1 = ys[0, Lc:], ys[1, :L]
    s5o = glu_fwd("s5_glu", u, y0, y1, ssm_d[0], Wglu)

    bias = na_bias(na_rpb[0])
    (o_na, lse), (g_up0, Wdn0) = natten_fwd(
        "na_fwd", qkv, kvc, bias,
        comm=_merge_comm(allgather_chips_1(None, s_ffn0[:1]), allgather_chips_2(None, g_dn0, s_ffn0[1:])))
    mixcat = jnp.concatenate([s5o, o_na], axis=1)
    ymix, (Wup0,) = mm_nn("l0_out", mixcat, Wout, BF16, comm=allgather_chips_2(None, [g_up0], s_ffn0[:1]))
    x1 = gate_res_fwd("l0_res", xs, ymix, gt_m)
    Wdn0 = Wdn0.reshape(-1, D)
    x2, ffn0, (g_up1,), (g_dn1, Wup1), (g_pw1, g_pw2, Wdn1) = _ffn_fwd(
        "f0", x1, sh_f, sc_f, gt_f, g_ffn[0], Wup0, cw3[0], cb3[0], Wdn0,
        comm_up=allgather_chips_1(None, s_ffn1[:1]),
        comm_mid=lambda got_up: _merge_comm(allgather_chips_1(None, s_ffn1[1:]), allgather_chips_2(None, got_up, s_ffn1[:1])),
        comm_down=lambda got_mid: _merge_comm(allgather_chips_1(None, s_conv), allgather_chips_2(None, got_mid[:1], s_ffn1[1:])))
    Wpw1, Wpw2 = allgather_chips_2("gather_conv_2", [g_pw1, g_pw2], s_conv)
    Wpw2 = Wpw2.reshape(-1, D)
    Wup, Wdn = [Wup0, Wup1], [Wdn0.reshape(-1, D), Wdn1.reshape(-1, D)]

    sh_v, sc_v, gt_v, sh_g, sc_g, gt_g = mods[1]
    hcv = norm_mod_fwd("l1_norm", x2, g_mix[1] * (1.0 + sc_v), sh_v)
    ag3 = mm_nn_pieces("l1_pw1", hcv, Wpw1, 0, N_CHIP, BF16, halves=2)
    z1, z3 = conf_mid_fwd("l1_mid", ag3, dw_w_f, dw_b_f, ln_g_f, ln_b_f)
    ycv = mm_nn("l1_pw2", z3, Wpw2, BF16)
    x3 = gate_res_fwd("l1_res", x2, ycv, gt_v)
    x4, ffn1, _, _, _ = _ffn_fwd("f1", x3, sh_g, sc_g, gt_g, g_ffn[1], Wup[1], cw3[1], cb3[1], Wdn[1])

    dx4, dg_out, loss_part = loss_head("loss", x4, g_out, tgt)
    loss = lax.psum(loss_part[0, 0], ("x", "y", "c"))

    dx3, gf1, _, _ = _ffn_bwd("f1", dx4, ffn1, sc_g, gt_g, g_ffn[1], Wup[1], cw3[1], cb3[1], Wdn[1])
    g_up1, g_dn1 = [gf1["dwup"]], [gf1["dwdn"].reshape(N_CHIP, -1, D)]
    dycv, dgt_v = gate_res_bwd("l1_res_b", dx3, ycv, gt_v)
    dz3, got = mm_nt("l1_pw2_bx", dycv, Wpw2, BF16, comm=reduce_1(None, g_up1))
    parts_up1 = added("up1", g_up1, got)
    dWpw2, got = mm_tn("l1_pw2_bw", z3, dycv, BF16, comm=reduce_1(None, g_dn1))
    parts_dn1 = added("dn1", g_dn1, got)
    dz1, dln_g, dln_b = conf_ln_bwd("l1_ln_b", z1, dz3, ln_g_f, ln_b_f)
    (dag3, ddw_w, ddw_b), slots_up1 = conf_conv_bwd("l1_conv_b", ag3, dz1, dw_w_f, comm=reduce_2(None, parts_up1))
    dhcv, slots_dn1 = mm_nt_pieces("l1_pw1_bx", dag3, Wpw1, BF16, halves=2, comm=reduce_2(None, parts_dn1))
    dWpw1 = mm_tn_pieces("l1_pw1_bw", hcv, dag3, N_CHIP, BF16, halves=2)
    dx2, cs1_v, cs2_v = norm_mod_bwd("l1_norm_b", x2, dhcv, g_mix[1] * (1.0 + sc_v), dx3)
    g_conv = [dWpw1, dWpw2.reshape(N_CHIP, -1, D)]

    held = {}

    def conv_stage_2(got_down):
        held["parts_conv"] = added("conv", g_conv, got_down)
        return reduce_2(None, held["parts_conv"])

    dx1, gf0, _, slots_conv = _ffn_bwd("f0", dx2, ffn0, sc_f, gt_f, g_ffn[0], Wup[0], cw3[0], cb3[0], Wdn[0],
                                       comm_down=reduce_1(None, g_conv), comm_mid=conv_stage_2)
    parts_conv = held["parts_conv"]
    g_up0, g_dn0 = [gf0["dwup"]], [gf0["dwdn"].reshape(N_CHIP, -1, D)]
    dymix, dgt_m = gate_res_bwd("l0_res_b", dx1, ymix, gt_m)
    dmix, got = mm_nt("l0_out_bx", dymix, Wout, BF16, comm=reduce_1(None, g_up0))
    parts_up0 = added("up0", g_up0, got)
    dWout, got = mm_tn("l0_out_bw", mixcat, dymix, BF16, comm=reduce_1(None, g_dn0))
    parts_dn0 = added("dn0", g_dn0, got)
    (dq, dk, dv, dkc, dvc, dbias), slots_up0 = natten_bwd("na_bwd", qkv, kvc, bias, o_na, lse, dmix,
                                                          comm=reduce_2(None, parts_up0))
    dy, zg, dzz, dd_skip = glu_bwd("s5_glu_b", u, y0, y1, ssm_d[0], Wglu, dmix)
    dWglu = mm_tn("s5_glu_bw", zg, dzz, BF16)
    g_mix2 = [dWglu.reshape(N_CHIP, -1, W), dWout.reshape(N_CHIP, -1, D)]

    zc = jnp.zeros((Lc, W), F32)
    dyseq = _interleave(jnp.stack([jnp.concatenate([zc, dy]), jnp.concatenate([dy, zc])]))
    (gloc, gfin), slots_dn0 = s5_scan("s5_scan_b", dyseq, Cblk.astype(BF16), a_b, rev=True, comm=reduce_2(None, parts_dn0))
    apow_b = (apow[0], -apow[1])
    (gst, duseq), got = s5_fix("s5_fix_b", gloc, _segment_carry(gfin, apow_b, True), a_b, Bblk.swapaxes(-1, -2).astype(BF16),
                               rev=True, comm=reduce_1(None, g_mix2))
    parts_mix2 = added("mix2", g_mix2, got)
    (dBm, dCm, da8), slots_mix2 = s5_grads("s5_grads", gst, hst, useq, dyseq, comm=reduce_2(None, parts_mix2))
    dus = _deinterleave(duseq)
    du = fma3("s5_du", dy, dus[0, Lc:], dus[1, :L], ssm_d[0], BF16)
    duc = dus[0, :Lc] + dus[1, L:]

    d_in = jnp.concatenate([du, dq, dk.astype(BF16), dv.astype(BF16)], axis=1)
    d_in_c = jnp.concatenate([duc.astype(BF16), jnp.zeros((Lc, W), BF16), dkc.astype(BF16), dvc.astype(BF16)], axis=1)
    dh0 = mm_nt_pieces("l0_in_bx", d_in, Win, BF16)
    dhc0 = mm_nt_pieces("l0_in_bxc", d_in_c, Win, BF16)
    dWin = mm_tn_pieces("l0_in_bw", jnp.concatenate([hc0, h0]), jnp.concatenate([d_in_c, d_in]), N_CHIP, BF16)
    dx0, cs1_m, cs2_m = norm_mod_bwd("l0_norm_b", xs, dh0, g_mix[0] * (1.0 + sc_m), dx1)
    _, cs1_c, cs2_c = norm_mod_bwd("l0_norm_bc", cx, dhc0, g_mix[0] * (1.0 + scc), jnp.zeros_like(cx))

    dmod0 = jnp.concatenate([cs1_m[0], cs2_m[0] * g_mix[0], dgt_m[0], gf0["dsh"], gf0["dsc"], gf0["dgt"]])
    dmod1 = jnp.concatenate([cs1_v[0], cs2_v[0] * g_mix[1], dgt_v[0], gf1["dsh"], gf1["dsc"], gf1["dgt"]])
    dmodc = jnp.concatenate([cs1_c[0], cs2_c[0] * g_mix[0], jnp.zeros((4 * D,), F32)])
    dm_rows = jnp.concatenate([jnp.stack([dmod0, dmod1, dmodc]), jnp.zeros((SUB - 3, N_MOD * D), F32)])
    dm_all = allgather_dev("gather_dmod", dm_rows).reshape(N_DEV, SUB, N_MOD * D)
    dm_sum = sum_lead("sum_dmod", dm_all, F32)
    pad7 = jnp.zeros((2 * SUB - N_DEV - 1, N_MOD * D), F32)
    dMod = [jnp.concatenate([dm_all[:, 0], dm_sum[2:3], pad7]), jnp.concatenate([dm_all[:, 1], jnp.zeros_like(dm_sum[2:3]), pad7])]
    dMod_cols = [lax.dynamic_slice_in_dim(m, chip * Cq, Cq, axis=1) for m in dMod]
    g_w_mod = jnp.stack([mm_tn("mod_bw%d" % l, S16, dMod_cols[l], F32) for l in range(2)])
    g_b_mod = jnp.stack([dm_sum[0] + dm_sum[2], dm_sum[1]])
    ds_part = mm_nt("mod_bx", dMod_cols[0], w_mod[0], F32)
    ds_all = allgather_dev("gather_dsc", jnp.zeros((SUB, D), F32).at[0].set(ds_part[N_DEV]))
    ds_c = sum_lead("sum_dsc", ds_all.reshape(N_DEV, SUB, D)[0::2], F32)[0]
    sg_c = jax.nn.sigmoid(c_ctx)
    g_c_ctx = ds_c * sg_c * (1.0 + c_ctx * (1.0 - sg_c))

    g_rpb_loc = na_bias_grad(dbias)

    dbb = [_blockdiag_extract(dBm[:, z], Cg, P) for z in range(2)]
    dcc = [_blockdiag_extract(dCm[:, z], Cg, P).swapaxes(-1, -2) for z in range(2)]
    da = jnp.sum(da8, axis=2).reshape(2, 2, G, P)
    small = {
        "g_mix": jnp.stack([cs2_m[0] * (1.0 + sc_m) + cs2_c[0] * (1.0 + scc), cs2_v[0] * (1.0 + sc_v)]),
        "g_ffn": jnp.stack([gf0["dg"], gf1["dg"]]),
        "a_re": da[:, 0], "a_im": da[:, 1], "bb_re": dbb[0], "bb_im": dbb[1], "c_re": dcc[0], "c_im": -dcc[1],
        "ssm_d": dd_skip, "na_rpb": g_rpb_loc, "cv_dw_w": ddw_w, "cv_dw_b": ddw_b, "cv_ln_g": dln_g, "cv_ln_b": dln_b,
        "ffn_conv_w": jnp.stack([gf0["dcw"], gf1["dcw"]]), "ffn_conv_b": jnp.stack([gf0["dcb"], gf1["dcb"]]),
        "g_out": dg_out,
    }
    skeys = list(small)
    sbuf = _pack([small[k] for k in skeys], 1024, 4 * SUB)
    ssum = dict(zip(skeys, _unpack(allreduce_small("reduce_small", sbuf, ids), [small[k].shape for k in skeys])))
    g_lam_re, g_lam_im, g_log_dt, g_b_re, g_b_im = disc_vjp((ssum["a_re"], ssum["a_im"], ssum["bb_re"], ssum["bb_im"]))

    def my_cols(t):
        n = t.shape[-1] // N_CHIP
        return lax.dynamic_slice_in_dim(t, chip * n, n, axis=t.ndim - 1)

    parts_win = added("win", [dWin], reduce_1("reduce_win_1", [dWin]))
    slots_win = reduce_2("reduce_win_2", parts_win)
    parts = parts_win + parts_mix2 + parts_conv + parts_up0 + parts_dn0 + parts_up1 + parts_dn1
    slots = [*slots_win, *slots_mix2, *slots_conv, *slots_up0, *slots_dn0, *slots_up1, *slots_dn1]
    fulls = [sum_slots("reduce_sum_%d" % i, s, t, ids) for i, (s, t) in enumerate(zip(slots, parts))]
    full = [f.reshape(-1, f.shape[-1]) for f in reduce_3("reduce_g_3", fulls)]
    gWin, gWglu, gWout, gWpw1, gWpw2, gWup0, gWdn0, gWup1, gWdn1 = full

    grads = {
        "c_ctx": g_c_ctx, "w_mod": g_w_mod, "b_mod": g_b_mod, "g_mix": ssum["g_mix"], "g_ffn": ssum["g_ffn"],
        "w_in": gWin[None], "ssm_lam_re": g_lam_re[None], "ssm_lam_im": g_lam_im[None], "ssm_log_dt": g_log_dt[None],
        "ssm_b_re": g_b_re[None], "ssm_b_im": g_b_im[None], "ssm_c_re": ssum["c_re"][None], "ssm_c_im": ssum["c_im"][None],
        "ssm_d": ssum["ssm_d"], "ssm_w_glu": gWglu[None], "na_rpb": ssum["na_rpb"][None], "w_out": gWout[None],
        "cv_w_pw1": gWpw1[None], "cv_dw_w": my_cols(ssum["cv_dw_w"])[None], "cv_dw_b": my_cols(ssum["cv_dw_b"]),
        "cv_ln_g": my_cols(ssum["cv_ln_g"]), "cv_ln_b": my_cols(ssum["cv_ln_b"]), "cv_w_pw2": gWpw2[None],
        "ffn_w_up": jnp.stack([gWup0, gWup1]), "ffn_conv_w": my_cols(ssum["ffn_conv_w"]), "ffn_conv_b": ssum["ffn_conv_b"],
        "ffn_w_down": jnp.stack([gWdn0, gWdn1]), "g_out": ssum["g_out"][0],
    }
    grads = {k: grads[k].reshape(p[k].shape) for k in _WEIGHTS}

    large = [k for k in _WEIGHTS if p[k].size >= (1 << 18)]
    tiny = [k for k in _WEIGHTS if k not in large]
    delta, new_m, new_v = {}, {}, {}
    for k in large:
        delta[k], new_m[k], new_v[k] = adamw("adamw_" + k, p[k], grads[k], p["m_" + k], p["v_" + k])
    packs = [_pack([src[pre + k] for k in tiny], 1024) for src, pre in ((p, ""), (grads, ""), (p, "m_"), (p, "v_"))]
    outs = adamw("adamw_small", *packs)
    shapes = [p[k].shape for k in tiny]
    for dst, buf in zip((delta, new_m, new_v), outs):
        dst.update(zip(tiny, _unpack(buf, shapes)))

    return (loss, dx0[None], *[grads[k] for k in _WEIGHTS], *[delta[k] for k in _WEIGHTS],
            *[new_m[k] for k in _WEIGHTS], *[new_v[k] for k in _WEIGHTS])
```

```python
import functools
import math

import numpy as np
import jax
import jax.numpy as jnp
from jax import lax
from jax.experimental import pallas as pl
from jax.experimental.pallas import tpu as pltpu

F32, BF16 = jnp.float32, jnp.bfloat16
MESH = pl.DeviceIdType.MESH
V7X_VMEM_LIMIT = 56 << 20
LANE, SUB = 128, 8
N_CHIP, N_DEV = 4, 8

GRID_W = 64
N_MOD = 6
SSM_GROUP, SSM_STATE = 16, 64
NA_HEAD_DIM, NA_WIN_R, NA_WIN_C = 128, 8, 16
EPS = 1e-6
NEG = -1e30
ADAM_LR, ADAM_B1, ADAM_B2, ADAM_EPS, ADAM_WD, ADAM_STEP = 0.001, 0.9, 0.999, 1e-08, 0.01, 10
S5_STRIP = 512
S5_SEG = 8

NN = (((1,), (0,)), ((), ()))
NT = (((1,), (1,)), ((), ()))
TN = (((0,), (0,)), ((), ()))


def _params(*sem, side_effects=False):
    return pltpu.CompilerParams(dimension_semantics=sem if sem else None, vmem_limit_bytes=V7X_VMEM_LIMIT,
                                has_side_effects=side_effects)


def _call(body, args, *, name, grid, in_specs, out_specs, out_shape, sem, scratch_shapes=(), comm=None):
    out_specs, out_shape, scratch_shapes = list(out_specs), list(out_shape), list(scratch_shapes)
    if comm is None:
        outs = pl.pallas_call(body, name=name, grid=grid, in_specs=list(in_specs), out_specs=out_specs, out_shape=out_shape,
                              scratch_shapes=scratch_shapes, compiler_params=_params(*sem))(*args)
        return list(outs), []
    c_args, c_shapes, c_alias, n_remote, plan = comm
    n_in, n_out, n_ci, n_co, n_sc = len(args), len(out_shape), len(c_args), len(c_shapes), len(scratch_shapes)

    def wrapped(*refs):
        ins, cins = refs[:n_in], refs[n_in:n_in + n_ci]
        o0 = n_in + n_ci
        outs, couts = refs[o0:o0 + n_out], refs[o0 + n_out:o0 + n_out + n_co]
        s0 = o0 + n_out + n_co
        scr, (send_sems, recv_sems) = refs[s0:s0 + n_sc], refs[s0 + n_sc:]
        pids = [pl.program_id(a) for a in range(len(grid))]
        first = functools.reduce(jnp.logical_and, [q == 0 for q in pids])
        last = functools.reduce(jnp.logical_and, [q == g - 1 for q, g in zip(pids, grid)])
        me = (lax.axis_index("x"), lax.axis_index("y"), lax.axis_index("c"))

        def copies():
            _, sends, lands = plan(cins, couts)
            assert len(sends) == n_remote and len(lands) == n_remote
            out = [pltpu.make_async_remote_copy(src_ref=s, dst_ref=d, send_sem=send_sems.at[i], recv_sem=recv_sems.at[i],
                                                device_id=peer, device_id_type=MESH) for i, (s, d, peer) in enumerate(sends)]
            arrivals = [pltpu.make_async_remote_copy(src_ref=d, dst_ref=d, send_sem=send_sems.at[i], recv_sem=recv_sems.at[i],
                                                     device_id=me, device_id_type=MESH) for i, d in enumerate(lands)]
            return out, arrivals

        @pl.when(first)
        def _():
            for cp in copies()[0]:
                cp.start()

        body(*ins, *outs, *scr)

        @pl.when(last)
        def _():
            out, arrivals = copies()
            for cp in arrivals:
                cp.wait_recv()
            for cp in out:
                cp.wait_send()

    any_spec = pl.BlockSpec(memory_space=pl.ANY)
    res = pl.pallas_call(
        wrapped, name=name, grid=grid,
        in_specs=[*in_specs, *[any_spec] * n_ci], out_specs=[*out_specs, *[any_spec] * n_co],
        out_shape=[*out_shape, *[jax.ShapeDtypeStruct(s, d) for s, d in c_shapes]],
        input_output_aliases={n_in + i: n_out + j for i, j in c_alias.items()},
        scratch_shapes=[*scratch_shapes, pltpu.SemaphoreType.DMA((n_remote,)), pltpu.SemaphoreType.DMA((n_remote,))],
        compiler_params=_params(*["arbitrary"] * len(grid), side_effects=True),
    )(*args, *c_args)
    return list(res[:n_out]), list(res[n_out:])


def _pick(n, pref, mult=LANE):
    if n <= pref:
        return n
    best = None
    for t in range(mult, pref + 1, mult):
        if n % t == 0:
            best = t
    assert best is not None, (n, pref, mult)
    return best


def _sigmoid(x):
    return 1.0 / (1.0 + jnp.exp(-x))


def _mm(name, a, b, *, dims, grid, a_spec, b_spec, o_spec, out_shape, out_dtype, acc_shape, exact=False, comm=None):
    nk = grid[2]

    def body(a_ref, b_ref, o_ref, *scratch):
        if exact:
            part = lax.dot_general(a_ref[...], b_ref[...], dims, preferred_element_type=F32,
                                   precision=lax.Precision.HIGHEST)
        else:
            part = lax.dot_general(a_ref[...].astype(BF16), b_ref[...].astype(BF16), dims,
                                   preferred_element_type=F32)
        if nk == 1:
            o_ref[...] = part.astype(o_ref.dtype)
        else:
            acc = scratch[0]
            kk = pl.program_id(2)

            @pl.when(kk == 0)
            def _():
                acc[...] = part

            @pl.when(kk > 0)
            def _():
                acc[...] += part

            @pl.when(kk == nk - 1)
            def _():
                o_ref[...] = acc[...].astype(o_ref.dtype)

    outs, couts = _call(body, [a, b], name=name, grid=grid, in_specs=[a_spec, b_spec], out_specs=[o_spec],
                        out_shape=[jax.ShapeDtypeStruct(out_shape, out_dtype)],
                        scratch_shapes=[] if nk == 1 else [pltpu.VMEM(acc_shape, F32)],
                        sem=("parallel", "parallel", "arbitrary"), comm=comm)
    return outs[0] if comm is None else (outs[0], couts)


MM_VMEM_BUDGET = 36 << 20


def _fit(M, N, cost, m_mult=SUB):
    best = None
    for tm in sorted({_pick(M, p, m_mult) for p in (2048, 1024, 512, 256, 128)}):
        for tn in sorted({_pick(N, p) for p in (1408, 1024, 512, 256, 128)}):
            if best is None or (cost(tm, tn) <= MM_VMEM_BUDGET and tm * tn > best[0] * best[1]):
                best = (tm, tn)
    return best


def _sz(t):
    return jnp.dtype(t).itemsize


def mm_nn_pieces(name, a, w, p0, n_p, out_dtype, halves=1, comm=None):
    M, K = a.shape
    Nq = w.shape[2]
    tm, tn = _fit(M, Nq, lambda m, n: 2 * (m * K * _sz(a.dtype) + K * n * _sz(w.dtype) + m * n * _sz(out_dtype)))
    tpp = Nq // tn
    pph = n_p // halves
    if halves == 1:
        o_spec = pl.BlockSpec((tm, tn), lambda i, j, k: (i, j))
        oshape = (M, n_p * Nq)
    else:
        o_spec = pl.BlockSpec((None, tm, tn), lambda i, j, k: ((j // tpp) // pph, i, ((j // tpp) % pph) * tpp + j % tpp))
        oshape = (halves, M, pph * Nq)
    return _mm(name, a, w, dims=NN, grid=(M // tm, n_p * tpp, 1),
               a_spec=pl.BlockSpec((tm, K), lambda i, j, k: (i, 0)),
               b_spec=pl.BlockSpec((None, K, tn), lambda i, j, k: (p0 + j // tpp, 0, j % tpp)),
               o_spec=o_spec, out_shape=oshape, out_dtype=out_dtype, acc_shape=(tm, tn), comm=comm)


def mm_nn(name, a, w, out_dtype, exact=False, comm=None):
    M, K = a.shape
    N = w.shape[1]
    tm, tn = _fit(M, N, lambda m, n: 2 * (m * K * _sz(a.dtype) + K * n * _sz(w.dtype) + m * n * _sz(out_dtype)))
    return _mm(name, a, w, dims=NN, grid=(M // tm, N // tn, 1),
               a_spec=pl.BlockSpec((tm, K), lambda i, j, k: (i, 0)),
               b_spec=pl.BlockSpec((K, tn), lambda i, j, k: (0, j)),
               o_spec=pl.BlockSpec((tm, tn), lambda i, j, k: (i, j)),
               out_shape=(M, N), out_dtype=out_dtype, acc_shape=(tm, tn), exact=exact, comm=comm)


def mm_nt(name, dy, w, out_dtype, exact=False, comm=None):
    M, N = dy.shape
    K = w.shape[0]
    tm, tn = _fit(M, K, lambda m, n: 2 * (m * N * _sz(dy.dtype) + n * N * _sz(w.dtype) + m * n * _sz(out_dtype)))
    return _mm(name, dy, w, dims=NT, grid=(M // tm, K // tn, 1),
               a_spec=pl.BlockSpec((tm, N), lambda i, j, k: (i, 0)),
               b_spec=pl.BlockSpec((tn, N), lambda i, j, k: (j, 0)),
               o_spec=pl.BlockSpec((tm, tn), lambda i, j, k: (i, j)),
               out_shape=(M, K), out_dtype=out_dtype, acc_shape=(tm, tn), exact=exact, comm=comm)


def mm_nt_pieces(name, dy, w, out_dtype, halves=1, comm=None):
    P, K, Nq = w.shape
    M = dy.shape[-2]
    tm, tn = _fit(M, K, lambda m, n: 2 * (m * Nq * _sz(dy.dtype) + n * Nq * _sz(w.dtype) + m * n * _sz(out_dtype)) + 4 * m * n)
    pph = P // halves
    if halves == 1:
        a_spec = pl.BlockSpec((tm, Nq), lambda i, j, k: (i, k))
    else:
        a_spec = pl.BlockSpec((None, tm, Nq), lambda i, j, k: (k // pph, i, k % pph))
    return _mm(name, dy, w, dims=NT, grid=(M // tm, K // tn, P),
               a_spec=a_spec,
               b_spec=pl.BlockSpec((None, tn, Nq), lambda i, j, k: (k, j, 0)),
               o_spec=pl.BlockSpec((tm, tn), lambda i, j, k: (i, j)),
               out_shape=(M, K), out_dtype=out_dtype, acc_shape=(tm, tn), comm=comm)


def mm_tn(name, a, dy, out_dtype, comm=None):
    M, K = a.shape
    N = dy.shape[1]
    tm, tn = _fit(K, N, lambda m, n: 2 * (M * m * _sz(a.dtype) + M * n * _sz(dy.dtype) + m * n * _sz(out_dtype)), LANE)
    return _mm(name, a, dy, dims=TN, grid=(K // tm, N // tn, 1),
               a_spec=pl.BlockSpec((M, tm), lambda i, j, k: (0, i)),
               b_spec=pl.BlockSpec((M, tn), lambda i, j, k: (0, j)),
               o_spec=pl.BlockSpec((tm, tn), lambda i, j, k: (i, j)),
               out_shape=(K, N), out_dtype=out_dtype, acc_shape=(tm, tn), comm=comm)


def mm_tn_pieces(name, a, dy, n_p, out_dtype, halves=1, comm=None):
    M, K = a.shape
    Nq = (dy.shape[-1] * halves) // n_p
    tm, tn = _fit(K, Nq, lambda m, n: 2 * (M * m * _sz(a.dtype) + M * n * _sz(dy.dtype) + m * n * _sz(out_dtype)), LANE)
    tpp = Nq // tn
    pph = n_p // halves
    if halves == 1:
        b_spec = pl.BlockSpec((M, tn), lambda i, j, k: (0, j))
    else:
        b_spec = pl.BlockSpec((None, M, tn), lambda i, j, k: ((j // tpp) // pph, 0, ((j // tpp) % pph) * tpp + j % tpp))
    return _mm(name, a, dy, dims=TN, grid=(K // tm, n_p * tpp, 1),
               a_spec=pl.BlockSpec((M, tm), lambda i, j, k: (0, i)),
               b_spec=b_spec,
               o_spec=pl.BlockSpec((None, tm, tn), lambda i, j, k: (j // tpp, i, j % tpp)),
               out_shape=(n_p, K, Nq), out_dtype=out_dtype, acc_shape=(tm, tn), comm=comm)


def _row_call(name, body, ins, in_kinds, outs, rows, tr, scratch=()):
    def spec(kind, shape):
        if isinstance(kind, pl.BlockSpec):
            return kind
        if kind == "row":
            return pl.BlockSpec((tr,) + tuple(shape[1:]), lambda i: (i,) + (0,) * (len(shape) - 1))
        return pl.BlockSpec(tuple(shape), lambda i: (0,) * len(shape))

    return pl.pallas_call(
        body, name=name, grid=(rows // tr,),
        in_specs=[spec(k, a.shape) for k, a in zip(in_kinds, ins)],
        out_specs=[spec(k, s) for k, s, _ in outs],
        out_shape=[jax.ShapeDtypeStruct(s, d) for _, s, d in outs],
        scratch_shapes=list(scratch),
        compiler_params=_params("arbitrary"),
    )(*ins)


def _acc(ref, val):
    @pl.when(pl.program_id(0) == 0)
    def _():
        ref[...] = val

    @pl.when(pl.program_id(0) > 0)
    def _():
        ref[...] += val


def norm_mod_fwd(name, x, w, b, tr=256):
    rows, d = x.shape
    tr = _pick(rows, tr, SUB)

    def body(x_ref, w_ref, b_ref, h_ref):
        xv = x_ref[...]
        r = lax.rsqrt(jnp.mean(xv * xv, axis=-1, keepdims=True) + EPS)
        h_ref[...] = (xv * r * w_ref[...] + b_ref[...]).astype(BF16)

    return _row_call(name, body, [x, w.reshape(1, d), b.reshape(1, d)], ["row", "vec", "vec"],
                     [("row", (rows, d), BF16)], rows, tr)[0]


def norm_mod_bwd(name, x, dh, w, dx_in, tr=256):
    rows, d = x.shape
    tr = _pick(rows, tr, SUB)

    def body(x_ref, dh_ref, w_ref, dxi_ref, dx_ref, cs1_ref, cs2_ref):
        xv = x_ref[...]
        r = lax.rsqrt(jnp.mean(xv * xv, axis=-1, keepdims=True) + EPS)
        xn = xv * r
        dhv = dh_ref[...].astype(F32)
        dxn = dhv * w_ref[...]
        dx_ref[...] = dxi_ref[...] + r * (dxn - xn * jnp.mean(dxn * xn, axis=-1, keepdims=True))
        _acc(cs1_ref, jnp.sum(dhv, axis=0, keepdims=True))
        _acc(cs2_ref, jnp.sum(dhv * xn, axis=0, keepdims=True))

    return _row_call(name, body, [x, dh, w.reshape(1, d), dx_in], ["row", "row", "vec", "row"],
                     [("row", (rows, d), F32), ("acc", (1, d), F32), ("acc", (1, d), F32)], rows, tr)


def gate_res_fwd(name, x, y, gate, tr=256):
    rows, d = x.shape
    tr = _pick(rows, tr, SUB)

    def body(x_ref, y_ref, g_ref, o_ref):
        o_ref[...] = x_ref[...] + g_ref[...] * y_ref[...].astype(F32)

    return _row_call(name, body, [x, y, gate.reshape(1, d)], ["row", "row", "vec"],
                     [("row", (rows, d), F32)], rows, tr)[0]


def gate_res_bwd(name, dx, y, gate, tr=256):
    rows, d = dx.shape
    tr = _pick(rows, tr, SUB)

    def body(dx_ref, y_ref, g_ref, dy_ref, dg_ref):
        dxv = dx_ref[...]
        dy_ref[...] = (g_ref[...] * dxv).astype(BF16)
        _acc(dg_ref, jnp.sum(dxv * y_ref[...].astype(F32), axis=0, keepdims=True))

    return _row_call(name, body, [dx, y, gate.reshape(1, d)], ["row", "row", "vec"],
                     [("row", (rows, d), BF16), ("acc", (1, d), F32)], rows, tr)


def loss_head(name, x, g, target, tr=256):
    rows, d = x.shape
    tr = _pick(rows, tr, SUB)

    def body(x_ref, g_ref, t_ref, dx_ref, dg_ref, loss_ref):
        xv = x_ref[...]
        r = lax.rsqrt(jnp.mean(xv * xv, axis=-1, keepdims=True) + EPS)
        xn = xv * r
        err = xn * g_ref[...] - t_ref[...]
        dy = err * (1.0 / d)
        dxn = dy * g_ref[...]
        dx_ref[...] = r * (dxn - xn * jnp.mean(dxn * xn, axis=-1, keepdims=True))
        _acc(dg_ref, jnp.sum(dy * xn, axis=0, keepdims=True))
        part = 0.5 * jnp.sum(jnp.sum(err * err, axis=-1, keepdims=True) * (1.0 / d), axis=0, keepdims=True)
        _acc(loss_ref, jnp.broadcast_to(part, (1, LANE)))

    return _row_call(name, body, [x, g.reshape(1, d), target], ["row", "vec", "row"],
                     [("row", (rows, d), F32), ("acc", (1, d), F32), ("acc", (1, LANE), F32)], rows, tr)


def fma3(name, a, b, c, dvec, out_dtype, tr=256):
    rows, d = a.shape
    tr = _pick(rows, tr, SUB)

    def body(a_ref, b_ref, c_ref, d_ref, o_ref):
        o_ref[...] = (d_ref[...] * a_ref[...] + b_ref[...] + c_ref[...]).astype(o_ref.dtype)

    return _row_call(name, body, [a, b, c, dvec.reshape(1, d)], ["row", "row", "row", "vec"],
                     [("row", (rows, d), out_dtype)], rows, tr)[0]


def sum_lead(name, a, out_dtype, tr=512):
    n, rows, cols = a.shape
    tr = _pick(rows, tr, 16)

    def body(a_ref, o_ref):
        acc = a_ref[0].astype(F32)
        for s in range(1, n):
            acc = acc + a_ref[s].astype(F32)
        o_ref[...] = acc.astype(o_ref.dtype)

    return pl.pallas_call(
        body, name=name, grid=(rows // tr,),
        in_specs=[pl.BlockSpec((n, tr, cols), lambda i: (0, i, 0))],
        out_specs=pl.BlockSpec((tr, cols), lambda i: (i, 0)),
        out_shape=jax.ShapeDtypeStruct((rows, cols), out_dtype),
        compiler_params=_params("parallel"),
    )(a)


def adamw(name, w, g, m, v, tr=512):
    shape = w.shape
    cols = shape[-1]
    w2, g2, m2, v2 = (t.reshape(-1, cols) for t in (w, g, m, v))
    rows = w2.shape[0]
    tr, tc = _pick(rows, 256, SUB), _pick(cols, 1536)
    c1 = 1.0 - ADAM_B1 ** ADAM_STEP
    c2 = 1.0 - ADAM_B2 ** ADAM_STEP

    def body(w_ref, g_ref, m_ref, v_ref, d_ref, mo_ref, vo_ref):
        gv = g_ref[...]
        mn = ADAM_B1 * m_ref[...] + (1.0 - ADAM_B1) * gv
        vn = ADAM_B2 * v_ref[...] + (1.0 - ADAM_B2) * (gv * gv)
        mo_ref[...] = mn
        vo_ref[...] = vn
        d_ref[...] = -ADAM_LR * ((mn / c1) / (jnp.sqrt(vn / c2) + ADAM_EPS) + ADAM_WD * w_ref[...])

    blk = pl.BlockSpec((tr, tc), lambda i, j: (i, j))
    outs = pl.pallas_call(
        body, name=name, grid=(rows // tr, cols // tc), in_specs=[blk] * 4, out_specs=[blk] * 3,
        out_shape=[jax.ShapeDtypeStruct(w2.shape, F32)] * 3, compiler_params=_params("parallel", "parallel"),
    )(w2, g2, m2, v2)
    return tuple(o.reshape(shape) for o in outs)


def add_half(name, grad, got, c_idx, tr=256):
    Pn, R, C = grad.shape
    hr = R // 2
    tr = _pick(hr, tr, HALO)
    nb = hr // tr

    def body(c_ref, a_ref, b_ref, o_ref):
        o_ref[...] = (a_ref[...].astype(F32) + b_ref[...].astype(F32)).astype(o_ref.dtype)

    return pl.pallas_call(
        body, name=name,
        grid_spec=pltpu.PrefetchScalarGridSpec(
            num_scalar_prefetch=1, grid=(Pn, nb),
            in_specs=[pl.BlockSpec((None, tr, C), lambda q, i, c: (q, c[0] * nb + i, 0)),
                      pl.BlockSpec((None, tr, C), lambda q, i, c: (q, i, 0))],
            out_specs=pl.BlockSpec((None, tr, C), lambda q, i, c: (q, i, 0))),
        out_shape=jax.ShapeDtypeStruct((Pn, hr, C), BF16),
        compiler_params=_params("parallel", "parallel"),
    )(c_idx, grad, got)


def pair_sum_to_slot(name, buf, got, ids, tr=256):
    R, C = buf.shape
    hr = R // 2
    tr = _pick(hr, tr, SUB)
    nb = hr // tr

    def body(ids_ref, a_ref, b_ref, o_ref):
        o_ref[...] = a_ref[...] + b_ref[...]

    return pl.pallas_call(
        body, name=name,
        grid_spec=pltpu.PrefetchScalarGridSpec(
            num_scalar_prefetch=1, grid=(nb,),
            in_specs=[pl.BlockSpec((tr, C), lambda i, ids: (ids[1] * nb + i, 0)),
                      pl.BlockSpec((tr, C), lambda i, ids: (i, 0))],
            out_specs=pl.BlockSpec((None, tr, C), lambda i, ids: (ids[0], i, 0))),
        out_shape=jax.ShapeDtypeStruct((N_CHIP, hr, C), F32),
        compiler_params=_params("parallel"),
    )(ids, buf, got)


def sum_chips_to_half(name, slots, ids, tr=256):
    n, hr, C = slots.shape
    tr = _pick(hr, tr, SUB)

    def body(ids_ref, s_ref, o_ref):
        acc = s_ref[0]
        for q in range(1, n):
            acc = acc + s_ref[q]
        o_ref[...] = acc

    return pl.pallas_call(
        body, name=name,
        grid_spec=pltpu.PrefetchScalarGridSpec(
            num_scalar_prefetch=1, grid=(hr // tr,),
            in_specs=[pl.BlockSpec((n, tr, C), lambda i, ids: (0, i, 0))],
            out_specs=pl.BlockSpec((None, tr, C), lambda i, ids: (ids[1], i, 0))),
        out_shape=jax.ShapeDtypeStruct((2, hr, C), F32),
        compiler_params=_params("parallel"),
    )(ids, slots)


def sum_slots(name, slots, mine, ids, tr=256):
    Pn, hr, C = slots.shape
    tr = _pick(hr, tr, HALO)

    def body(ids_ref, m_ref, s1_ref, s2_ref, s3_ref, o_ref):
        o_ref[...] = (m_ref[...].astype(F32) + s1_ref[...].astype(F32)) + (s2_ref[...].astype(F32) + s3_ref[...].astype(F32))

    def other(k):
        return pl.BlockSpec((None, tr, C), lambda i, ids: ((ids[0] + k) % Pn, i, 0))

    return pl.pallas_call(
        body, name=name,
        grid_spec=pltpu.PrefetchScalarGridSpec(
            num_scalar_prefetch=1, grid=(hr // tr,),
            in_specs=[pl.BlockSpec((None, tr, C), lambda i, ids: (ids[0], i, 0)), other(1), other(2), other(3)],
            out_specs=pl.BlockSpec((None, tr, C), lambda i, ids: (ids[1], i, 0))),
        out_shape=jax.ShapeDtypeStruct((2, hr, C), F32),
        compiler_params=_params("parallel"),
    )(ids, mine, slots, slots, slots)


HALO = 16


def _halo_specs(lead, R, tn, n_rows, col_of):
    nb, nblk = R // HALO, n_rows // HALO

    def mk(rows, row_of):
        return pl.BlockSpec((lead, rows, tn), lambda *g: (0, row_of(g[-1]), col_of(g)))

    return (mk(HALO, lambda i: jnp.maximum(i * nb - 1, 0)), mk(R, lambda i: i),
            mk(HALO, lambda i: jnp.minimum((i + 1) * nb, nblk - 1)))


def _fill_halo(dst, i, last, R, prev, cur, nxt):
    nd = len(dst.shape)
    lead = (slice(None),) * (nd - 2)
    dst[lead + (slice(0, HALO), slice(None))] = jnp.where(i == 0, 0.0, prev)
    dst[lead + (slice(HALO, HALO + R), slice(None))] = cur
    dst[lead + (slice(HALO + R, HALO + R + HALO), slice(None))] = jnp.where(i == last, 0.0, nxt)


def _shift_mats(n):
    i = np.arange(n)
    return jnp.asarray(np.stack([i[:, None] - 1 == i[None, :], i[:, None] + 1 == i[None, :]]), BF16)


def _shifted(s_ref, xb):
    return (jnp.dot(s_ref[0], xb, preferred_element_type=F32), jnp.dot(s_ref[1], xb, preferred_element_type=F32))


def ffn_mid_fwd(name, up3, cw, cb, R=256, tn=512, comm=None):
    _, L, Fd = up3.shape
    R, tn = _pick(L, R, HALO), _pick(Fd, tn)
    nrow = L // R

    def body(p_ref, c_ref, n_ref, w_ref, b_ref, s_ref, act_ref):
        i = pl.program_id(1)
        row = lax.broadcasted_iota(jnp.int32, (R, tn), 0)
        cv = []
        for z in range(2):
            xb = c_ref[z]
            before = jnp.where(i == 0, 0.0, p_ref[z].astype(F32)[HALO - 1:HALO])
            after = jnp.where(i == nrow - 1, 0.0, n_ref[z].astype(F32)[0:1])
            dn, up = _shifted(s_ref, xb)
            dn = jnp.where(row == 0, before, dn)
            up = jnp.where(row == R - 1, after, up)
            cv.append(b_ref[z] + w_ref[z, 0:1, :] * dn + w_ref[z, 1:2, :] * xb.astype(F32) + w_ref[z, 2:3, :] * up)
        u, g = cv
        act_ref[...] = (u * g * _sigmoid(g)).astype(BF16)

    hs = _halo_specs(2, R, tn, L, lambda g: g[0])
    outs, couts = _call(
        body, [up3, up3, up3, cw, cb, _shift_mats(R)], name=name, grid=(Fd // tn, nrow),
        in_specs=[*hs, pl.BlockSpec((2, 3, tn), lambda j, i: (0, 0, j)), pl.BlockSpec((2, 1, tn), lambda j, i: (0, 0, j)),
                  pl.BlockSpec((2, R, R), lambda j, i: (0, 0, 0))],
        out_specs=[pl.BlockSpec((R, tn), lambda j, i: (i, j))],
        out_shape=[jax.ShapeDtypeStruct((L, Fd), BF16)], sem=("parallel", "arbitrary"), comm=comm)
    return outs[0] if comm is None else (outs[0], couts)


def ffn_mid_bwd(name, up3, dact, cw, cb, R=256, tn=512, comm=None):
    _, L, Fd = up3.shape
    R, tn = _pick(L, R, HALO), _pick(Fd, tn)
    nrow = L // R

    def gate_grads(u, g, d):
        sg = _sigmoid(g)
        return d * g * sg, d * u * sg * (1.0 + g * (1.0 - sg))

    def body(pu, cu, nu, pd, cd, nd, w_ref, b_ref, s_ref, dup_ref, dcw_ref, dcb_ref):
        i = pl.program_id(1)
        first, last = i == 0, i == nrow - 1
        row = lax.broadcasted_iota(jnp.int32, (R, tn), 0)
        cv, cv_b, cv_a, taps = [], [], [], []
        for z in range(2):
            xb = cu[z]
            xf = xb.astype(F32)
            pf = jnp.where(first, 0.0, pu[z].astype(F32))
            nf = jnp.where(last, 0.0, nu[z].astype(F32))
            xm2, xm1, xp0, xp1 = pf[HALO - 2:HALO - 1], pf[HALO - 1:HALO], nf[0:1], nf[1:2]
            dn, up = _shifted(s_ref, xb)
            dn = jnp.where(row == 0, xm1, dn)
            up = jnp.where(row == R - 1, xp0, up)
            w0, w1, w2, b = w_ref[z, 0:1, :], w_ref[z, 1:2, :], w_ref[z, 2:3, :], b_ref[z]
            cv.append(b + w0 * dn + w1 * xf + w2 * up)
            cv_b.append(b + w0 * xm2 + w1 * xm1 + w2 * xf[0:1])
            cv_a.append(b + w0 * xf[R - 1:R] + w1 * xp0 + w2 * xp1)
            taps.append((dn, xf, up))
        dcs = gate_grads(cv[0], cv[1], cd[0].astype(F32))
        dcs_b = gate_grads(cv_b[0], cv_b[1], jnp.where(first, 0.0, pd[0].astype(F32)[HALO - 1:HALO]))
        dcs_a = gate_grads(cv_a[0], cv_a[1], jnp.where(last, 0.0, nd[0].astype(F32)[0:1]))

        @pl.when(first)
        def _():
            dcw_ref[...] = jnp.zeros_like(dcw_ref)
            dcb_ref[...] = jnp.zeros_like(dcb_ref)

        for z in range(2):
            dc = dcs[z]
            dc_dn, dc_up = _shifted(s_ref, dc.astype(BF16))
            dc_dn = jnp.where(row == 0, dcs_b[z], dc_dn)
            dc_up = jnp.where(row == R - 1, dcs_a[z], dc_up)
            dup_ref[z] = (w_ref[z, 0:1, :] * dc_up + w_ref[z, 1:2, :] * dc + w_ref[z, 2:3, :] * dc_dn).astype(BF16)
            dcb_ref[z] += jnp.sum(dc, axis=0, keepdims=True)
            for k in range(3):
                dcw_ref[z, k:k + 1, :] += jnp.sum(dc * taps[z][k], axis=0, keepdims=True)

    hu = _halo_specs(2, R, tn, L, lambda g: g[0])
    hd = _halo_specs(1, R, tn, L, lambda g: g[0])
    outs, couts = _call(
        body, [up3, up3, up3, dact[None], dact[None], dact[None], cw, cb, _shift_mats(R)], name=name, grid=(Fd // tn, nrow),
        in_specs=[*hu, *hd, pl.BlockSpec((2, 3, tn), lambda j, i: (0, 0, j)), pl.BlockSpec((2, 1, tn), lambda j, i: (0, 0, j)),
                  pl.BlockSpec((2, R, R), lambda j, i: (0, 0, 0))],
        out_specs=[pl.BlockSpec((2, R, tn), lambda j, i: (0, i, j)), pl.BlockSpec((2, 3, tn), lambda j, i: (0, 0, j)),
                   pl.BlockSpec((2, 1, tn), lambda j, i: (0, 0, j))],
        out_shape=[jax.ShapeDtypeStruct((2, L, Fd), BF16), jax.ShapeDtypeStruct((2, 3, Fd), F32),
                   jax.ShapeDtypeStruct((2, 1, Fd), F32)],
        sem=("parallel", "arbitrary"), comm=comm)
    return outs if comm is None else (outs, couts)


def _glu_z0(blk):
    return blk[0].astype(F32) * _sigmoid(blk[1].astype(F32))


def _sublane_copies(ref, cs):
    n = ref.shape[1]
    blk = ref[0, :, cs]
    for b in range(1, SUB):
        ref[b, :, cs] = pltpu.roll(blk, n - b, 0)


def _tap(ref, offset, rows, cs):
    return ref[offset % SUB, pl.ds(offset - offset % SUB, rows), cs]


def conf_mid_fwd(name, ag3, dw_w, dw_b, ln_g, ln_b, R=128, cb=256):
    _, L, C = ag3.shape
    K = dw_w.shape[0]
    pad = (K - 1) // 2
    assert pad <= HALO
    R, cb = _pick(L, R, HALO), _pick(C, cb)
    nrow = L // R

    def body(p_ref, c_ref, n_ref, w_ref, b_ref, g_ref, bb_ref, z1_ref, z3_ref, s_ref):
        i = pl.program_id(0)
        _fill_halo(s_ref.at[0], i, nrow - 1, R, _glu_z0(p_ref), _glu_z0(c_ref), _glu_z0(n_ref))
        for c0 in range(0, C, cb):
            cs = slice(c0, c0 + cb)
            _sublane_copies(s_ref, cs)
            acc = jnp.broadcast_to(b_ref[:, cs], (R, cb))
            for k in range(K):
                acc = acc + w_ref[k:k + 1, cs] * _tap(s_ref, HALO - pad + k, R, cs)
            z1_ref[:, cs] = acc
        z1 = z1_ref[...]
        zc = z1 - jnp.mean(z1, axis=-1, keepdims=True)
        zn = zc * lax.rsqrt(jnp.mean(zc * zc, axis=-1, keepdims=True) + EPS)
        z2 = zn * g_ref[...] + bb_ref[...]
        z3_ref[...] = (z2 * _sigmoid(z2)).astype(BF16)

    hs = _halo_specs(2, R, C, L, lambda g: 0)
    vec = pl.BlockSpec((1, C), lambda i: (0, 0))
    return pl.pallas_call(
        body, name=name, grid=(nrow,),
        in_specs=[*hs, pl.BlockSpec((K, C), lambda i: (0, 0)), vec, vec, vec],
        out_specs=[pl.BlockSpec((R, C), lambda i: (i, 0)), pl.BlockSpec((R, C), lambda i: (i, 0))],
        out_shape=[jax.ShapeDtypeStruct((L, C), F32), jax.ShapeDtypeStruct((L, C), BF16)],
        scratch_shapes=[pltpu.VMEM((SUB, R + 2 * HALO, C), F32)],
        compiler_params=_params("parallel"),
    )(ag3, ag3, ag3, dw_w, dw_b.reshape(1, C), ln_g.reshape(1, C), ln_b.reshape(1, C))


def conf_ln_bwd(name, z1, dz3, ln_g, ln_b, tr=256):
    rows, C = z1.shape
    tr = _pick(rows, tr, HALO)

    def body(z_ref, d_ref, g_ref, b_ref, dz_ref, dg_ref, db_ref):
        z1v = z_ref[...]
        zc = z1v - jnp.mean(z1v, axis=-1, keepdims=True)
        rs = lax.rsqrt(jnp.mean(zc * zc, axis=-1, keepdims=True) + EPS)
        zn = zc * rs
        z2 = zn * g_ref[...] + b_ref[...]
        sg = _sigmoid(z2)
        dz2 = d_ref[...].astype(F32) * sg * (1.0 + z2 * (1.0 - sg))
        _acc(dg_ref, jnp.sum(dz2 * zn, axis=0, keepdims=True))
        _acc(db_ref, jnp.sum(dz2, axis=0, keepdims=True))
        dzn = dz2 * g_ref[...]
        dz1 = rs * (dzn - jnp.mean(dzn, axis=-1, keepdims=True) - zn * jnp.mean(dzn * zn, axis=-1, keepdims=True))
        dz_ref[...] = dz1.astype(BF16)

    return _row_call(name, body, [z1, dz3, ln_g.reshape(1, C), ln_b.reshape(1, C)], ["row", "row", "vec", "vec"],
                     [("row", (rows, C), BF16), ("acc", (1, C), F32), ("acc", (1, C), F32)], rows, tr)


def conf_conv_bwd(name, ag3, dz1, dw_w, R=128, cb=256, comm=None):
    _, L, C = ag3.shape
    K = dw_w.shape[0]
    pad = (K - 1) // 2
    R, cb = _pick(L, R, HALO), _pick(C, cb)
    nrow = L // R

    def body(pa, ca, na, pd, cd, nd, w_ref, dag_ref, dw_ref, db_ref, s_ref, d_ref, z_ref):
        i = pl.program_id(0)
        _fill_halo(s_ref.at[0], i, nrow - 1, R, _glu_z0(pa), _glu_z0(ca), _glu_z0(na))
        _fill_halo(d_ref.at[0], i, nrow - 1, R, pd[0].astype(F32), cd[0].astype(F32), nd[0].astype(F32))

        @pl.when(i == 0)
        def _():
            dw_ref[...] = jnp.zeros_like(dw_ref)
            db_ref[...] = jnp.zeros_like(db_ref)

        for c0 in range(0, C, cb):
            cs = slice(c0, c0 + cb)
            _sublane_copies(s_ref, cs)
            _sublane_copies(d_ref, cs)
            dcur = d_ref[0, pl.ds(HALO, R), cs]
            acc = jnp.zeros((R, cb), F32)
            for k in range(K):
                acc = acc + w_ref[k:k + 1, cs] * _tap(d_ref, HALO + pad - k, R, cs)
                dw_ref[k:k + 1, cs] += jnp.sum(dcur * _tap(s_ref, HALO - pad + k, R, cs), axis=0, keepdims=True)
            z_ref[:, cs] = acc
            db_ref[:, cs] += jnp.sum(dcur, axis=0, keepdims=True)
        dz0 = z_ref[...]
        a = ca[0].astype(F32)
        sg = _sigmoid(ca[1].astype(F32))
        dag_ref[0] = (dz0 * sg).astype(BF16)
        dag_ref[1] = (dz0 * a * sg * (1.0 - sg)).astype(BF16)

    ha = _halo_specs(2, R, C, L, lambda g: 0)
    hd = _halo_specs(1, R, C, L, lambda g: 0)
    outs, couts = _call(
        body, [ag3, ag3, ag3, dz1[None], dz1[None], dz1[None], dw_w], name=name, grid=(nrow,),
        in_specs=[*ha, *hd, pl.BlockSpec((K, C), lambda i: (0, 0))],
        out_specs=[pl.BlockSpec((2, R, C), lambda i: (0, i, 0)), pl.BlockSpec((K, C), lambda i: (0, 0)),
                   pl.BlockSpec((1, C), lambda i: (0, 0))],
        out_shape=[jax.ShapeDtypeStruct((2, L, C), BF16), jax.ShapeDtypeStruct((K, C), F32),
                   jax.ShapeDtypeStruct((1, C), F32)],
        scratch_shapes=[pltpu.VMEM((SUB, R + 2 * HALO, C), F32), pltpu.VMEM((SUB, R + 2 * HALO, C), F32),
                        pltpu.VMEM((R, C), F32)],
        sem=("arbitrary",), comm=comm)
    return outs if comm is None else (outs, couts)


_GELU_C = math.sqrt(2.0 / math.pi)


def _gelu(x):
    return 0.5 * x * (1.0 + jnp.tanh(_GELU_C * (x + 0.044715 * x * x * x)))


def _gelu_grad(x):
    t = jnp.tanh(_GELU_C * (x + 0.044715 * x * x * x))
    return 0.5 * (1.0 + t) + 0.5 * x * (1.0 - t * t) * _GELU_C * (1.0 + 3.0 * 0.044715 * x * x)


def glu_fwd(name, u, y0, y1, d, wg, tr=512):
    rows, W = u.shape
    tr = _pick(rows, tr, HALO)

    def body(u_ref, y0_ref, y1_ref, d_ref, w_ref, o_ref):
        z = _gelu(d_ref[...] * u_ref[...] + y0_ref[...] + y1_ref[...])
        zz = jnp.dot(z.astype(BF16), w_ref[...], preferred_element_type=F32)
        o_ref[...] = (z * _sigmoid(zz)).astype(BF16)

    return _row_call(name, body, [u, y0, y1, d.reshape(1, W), wg], ["row", "row", "row", "vec", "vec"],
                     [("row", (rows, W), BF16)], rows, tr)[0]


def glu_bwd(name, u, y0, y1, d, wg, dmix, tr=512):
    rows, W = u.shape
    tr = _pick(rows, tr, HALO)

    def body(u_ref, y0_ref, y1_ref, d_ref, w_ref, do_ref, dy_ref, z_ref, dzz_ref, dd_ref):
        uv = u_ref[...]
        y = d_ref[...] * uv + y0_ref[...] + y1_ref[...]
        z = _gelu(y)
        zz = jnp.dot(z.astype(BF16), w_ref[...], preferred_element_type=F32)
        sg = _sigmoid(zz)
        do = do_ref[...].astype(F32)
        dzz = (do * z * sg * (1.0 - sg)).astype(BF16)
        dz = do * sg + lax.dot_general(dzz, w_ref[...], NT, preferred_element_type=F32)
        dy = dz * _gelu_grad(y)
        dy_ref[...] = dy
        z_ref[...] = z.astype(BF16)
        dzz_ref[...] = dzz
        _acc(dd_ref, jnp.sum(dy * uv, axis=0, keepdims=True))

    do_spec = pl.BlockSpec((tr, W), lambda i: (i, 0))
    return _row_call(name, body, [u, y0, y1, d.reshape(1, W), wg, dmix], ["row", "row", "row", "vec", "vec", do_spec],
                     [("row", (rows, W), F32), ("row", (rows, W), BF16), ("row", (rows, W), BF16), ("acc", (1, W), F32)],
                     rows, tr)


NA_KEYS = NA_WIN_R * GRID_W


NA_PAIRS = NA_WIN_R // 2


def na_bias(rpb):
    H, nr, nc = rpb.shape
    e, ok = _na_col_select()
    rp = jnp.pad(rpb.reshape(H * nr, nc), ((0, (-H * nr) % SUB), (0, LANE - nc)))
    cols = mm_nn("na_bias_mm", rp, jnp.asarray(e, F32), F32, exact=True)[:H * nr]
    tiles = (cols + jnp.asarray(np.where(ok, 0.0, NEG), F32)).reshape(H, nr, GRID_W, GRID_W)
    return jnp.concatenate([tiles[:, :-1], tiles[:, 1:]], axis=-1)


def na_bias_grad(db2):
    H, n2 = db2.shape[:2]
    left, right = db2[..., :GRID_W], db2[..., GRID_W:]
    tiles = jnp.pad(left, ((0, 0), (0, 1), (0, 0), (0, 0))) + jnp.pad(right, ((0, 0), (1, 0), (0, 0), (0, 0)))
    flat = tiles.reshape(H * (n2 + 1), GRID_W * GRID_W)
    flat = jnp.pad(flat, ((0, (-flat.shape[0]) % SUB), (0, 0)))
    dcol = mm_nt("na_bias_fold", flat, na_bias_fold_matrix(), F32, exact=True)
    return dcol[:H * (n2 + 1), :2 * NA_WIN_C - 1].reshape(H, n2 + 1, 2 * NA_WIN_C - 1)


def _na_col_select():
    q = np.arange(GRID_W)
    cs = np.clip(q - NA_WIN_C // 2, 0, GRID_W - NA_WIN_C)
    ok = ((q[None, :] >= cs[:, None]) & (q[None, :] < cs[:, None] + NA_WIN_C)).reshape(-1)
    cidx = np.clip(q[None, :] - q[:, None] + (NA_WIN_C - 1), 0, 2 * NA_WIN_C - 2).reshape(-1)
    return (cidx[None, :] == np.arange(LANE)[:, None]) & ok[None, :], ok


def na_bias_fold_matrix():
    return jnp.asarray(_na_col_select()[0], F32)


def _na_window(r, rows):
    kr0 = jnp.clip(r - NA_WIN_R // 2, 0, rows - NA_WIN_R)
    return pl.multiple_of(kr0 * GRID_W, GRID_W), r - kr0


def _na_dims(qkv, kvc):
    L = qkv.shape[0]
    NA = qkv.shape[1] // 3
    H = NA // NA_HEAD_DIM
    hp = 2 if H % 2 == 0 else 1
    return L, NA, H, hp, H // hp, L // GRID_W, kvc.shape[0]


def _na_bias_tile(b_ref, hh, off):
    return jnp.concatenate([b_ref[hh, NA_WIN_R - 1 - off + 2 * j] for j in range(NA_PAIRS)], axis=-1)


def natten_fwd(name, qkv, kvc, bias, comm=None):
    L, NA, H, hp, G, rows, Lc = _na_dims(qkv, kvc)
    scale = NA_HEAD_DIM ** -0.5
    wd = hp * NA_HEAD_DIM

    def body(q_ref, k_ref, v_ref, kc_ref, vc_ref, b_ref, o_ref, lse_ref):
        st, off = _na_window(pl.program_id(1), rows)
        for hh in range(hp):
            sl = slice(hh * NA_HEAD_DIM, (hh + 1) * NA_HEAD_DIM)
            q = q_ref[:, sl]
            s_loc = (lax.dot_general(q, k_ref[pl.ds(st, NA_KEYS), sl], NT, preferred_element_type=F32) * scale
                     + _na_bias_tile(b_ref, hh, off))
            s_ctx = lax.dot_general(q, kc_ref[:, sl], NT, preferred_element_type=F32) * scale
            m = jnp.maximum(jnp.max(s_loc, axis=-1, keepdims=True), jnp.max(s_ctx, axis=-1, keepdims=True))
            p_loc, p_ctx = jnp.exp(s_loc - m), jnp.exp(s_ctx - m)
            l = jnp.sum(p_loc, axis=-1, keepdims=True) + jnp.sum(p_ctx, axis=-1, keepdims=True)
            o = (jnp.dot(p_loc.astype(BF16), v_ref[pl.ds(st, NA_KEYS), sl], preferred_element_type=F32)
                 + jnp.dot(p_ctx.astype(BF16), vc_ref[:, sl], preferred_element_type=F32))
            o_ref[:, sl] = (o / l).astype(BF16)
            lse_ref[hh] = m + jnp.log(l)

    outs, couts = _call(
        body, [qkv, qkv, qkv, kvc, kvc, bias], name=name, grid=(G, rows),
        in_specs=[pl.BlockSpec((GRID_W, wd), lambda h, r: (r, h)),
                  pl.BlockSpec((L, wd), lambda h, r: (0, G + h)),
                  pl.BlockSpec((L, wd), lambda h, r: (0, 2 * G + h)),
                  pl.BlockSpec((Lc, wd), lambda h, r: (0, h)),
                  pl.BlockSpec((Lc, wd), lambda h, r: (0, G + h)),
                  pl.BlockSpec((hp,) + bias.shape[1:], lambda h, r: (h, 0, 0, 0))],
        out_specs=[pl.BlockSpec((GRID_W, wd), lambda h, r: (r, h)),
                   pl.BlockSpec((hp, GRID_W, 1), lambda h, r: (h, r, 0))],
        out_shape=[jax.ShapeDtypeStruct((L, NA), BF16), jax.ShapeDtypeStruct((H, L, 1), F32)],
        sem=("parallel", "arbitrary"), comm=comm)
    return outs if comm is None else (outs, couts)


def natten_bwd(name, qkv, kvc, bias, o, lse, dmix, comm=None):
    L, NA, H, hp, G, rows, Lc = _na_dims(qkv, kvc)
    scale = NA_HEAD_DIM ** -0.5
    wd = hp * NA_HEAD_DIM

    def body(q_ref, k_ref, v_ref, kc_ref, vc_ref, b_ref, o_ref, lse_ref, do_ref,
             dq_ref, dk_ref, dv_ref, dkc_ref, dvc_ref, db_ref):
        r = pl.program_id(1)
        st, off = _na_window(r, rows)

        @pl.when(r == 0)
        def _():
            for ref in (dk_ref, dv_ref, dkc_ref, dvc_ref, db_ref):
                ref[...] = jnp.zeros_like(ref)

        for hh in range(hp):
            sl = slice(hh * NA_HEAD_DIM, (hh + 1) * NA_HEAD_DIM)
            q, kl, vl, kc, vc = q_ref[:, sl], k_ref[pl.ds(st, NA_KEYS), sl], v_ref[pl.ds(st, NA_KEYS), sl], kc_ref[:, sl], vc_ref[:, sl]
            do = do_ref[:, sl]
            lse_v = lse_ref[hh]
            p_loc = jnp.exp(lax.dot_general(q, kl, NT, preferred_element_type=F32) * scale + _na_bias_tile(b_ref, hh, off) - lse_v)
            p_ctx = jnp.exp(lax.dot_general(q, kc, NT, preferred_element_type=F32) * scale - lse_v)
            delta = jnp.sum(do.astype(F32) * o_ref[:, sl].astype(F32), axis=-1, keepdims=True)
            ds_loc = p_loc * (lax.dot_general(do, vl, NT, preferred_element_type=F32) - delta)
            ds_ctx = p_ctx * (lax.dot_general(do, vc, NT, preferred_element_type=F32) - delta)
            dsl, dsc = ds_loc.astype(BF16), ds_ctx.astype(BF16)
            dq = jnp.dot(dsl, kl, preferred_element_type=F32) + jnp.dot(dsc, kc, preferred_element_type=F32)
            dq_ref[:, sl] = (dq * scale).astype(BF16)
            dk_ref[pl.ds(st, NA_KEYS), sl] += lax.dot_general(dsl, q, TN, preferred_element_type=F32) * scale
            dv_ref[pl.ds(st, NA_KEYS), sl] += lax.dot_general(p_loc.astype(BF16), do, TN, preferred_element_type=F32)
            dkc_ref[:, sl] += lax.dot_general(dsc, q, TN, preferred_element_type=F32) * scale
            dvc_ref[:, sl] += lax.dot_general(p_ctx.astype(BF16), do, TN, preferred_element_type=F32)
            for j in range(NA_PAIRS):
                db_ref[hh, NA_WIN_R - 1 - off + 2 * j] += ds_loc[:, 2 * j * GRID_W:(2 * j + 2) * GRID_W]

    tok = pl.BlockSpec((GRID_W, wd), lambda h, r: (r, h))
    bia = pl.BlockSpec((hp,) + bias.shape[1:], lambda h, r: (h, 0, 0, 0))
    outs, couts = _call(
        body, [qkv, qkv, qkv, kvc, kvc, bias, o, lse, dmix], name=name, grid=(G, rows),
        in_specs=[tok,
                  pl.BlockSpec((L, wd), lambda h, r: (0, G + h)),
                  pl.BlockSpec((L, wd), lambda h, r: (0, 2 * G + h)),
                  pl.BlockSpec((Lc, wd), lambda h, r: (0, h)),
                  pl.BlockSpec((Lc, wd), lambda h, r: (0, G + h)),
                  bia,
                  tok,
                  pl.BlockSpec((hp, GRID_W, 1), lambda h, r: (h, r, 0)),
                  pl.BlockSpec((GRID_W, wd), lambda h, r: (r, G + h))],
        out_specs=[tok,
                   pl.BlockSpec((L, wd), lambda h, r: (0, h)),
                   pl.BlockSpec((L, wd), lambda h, r: (0, h)),
                   pl.BlockSpec((Lc, wd), lambda h, r: (0, h)),
                   pl.BlockSpec((Lc, wd), lambda h, r: (0, h)),
                   bia],
        out_shape=[jax.ShapeDtypeStruct((L, NA), BF16), jax.ShapeDtypeStruct((L, NA), F32), jax.ShapeDtypeStruct((L, NA), F32),
                   jax.ShapeDtypeStruct((Lc, NA), F32), jax.ShapeDtypeStruct((Lc, NA), F32),
                   jax.ShapeDtypeStruct(bias.shape, F32)],
        sem=("parallel", "arbitrary"), comm=comm)
    return outs if comm is None else (outs, couts)


def _s5_dims(T, N):
    TC = T // S5_SEG
    assert T % (S5_SEG * SUB * 2) == 0 and N % S5_STRIP == 0
    return TC, TC // SUB, S5_SEG, N // S5_STRIP


def _s5_backward(d, rev):
    return (d == 1) != rev


def s5_scan(name, xin, mats, a, rev, comm=None):
    _, T, W = xin.shape
    N = a.shape[-1]
    TC, NG, NCH, NS = _s5_dims(T, N)
    CW, SL = W // NS, S5_STRIP

    def ck(d, k):
        return jnp.where(_s5_backward(d, rev), NCH - 1 - k, k)

    def body(x_ref, m_ref, a_ref, h_ref, f_ref, carry, hs):
        @pl.when(pl.program_id(2) == 0)
        def _():
            carry[...] = jnp.zeros_like(carry)

        xb = x_ref[...].astype(BF16)
        hs[0] = jnp.dot(xb, m_ref[0], preferred_element_type=F32)
        hs[1] = jnp.dot(xb, m_ref[1], preferred_element_type=F32)
        ar, ai = jnp.broadcast_to(a_ref[0], (SUB, SL)), jnp.broadcast_to(a_ref[1], (SUB, SL))
        bw = _s5_backward(pl.program_id(0), rev)

        def step(t, c):
            hr, hi = c
            row = pl.multiple_of(jnp.where(bw, NG - 1 - t, t) * SUB, SUB)
            nr = ar * hr - ai * hi + hs[0, pl.ds(row, SUB), :]
            ni = ar * hi + ai * hr + hs[1, pl.ds(row, SUB), :]
            hs[0, pl.ds(row, SUB), :] = nr
            hs[1, pl.ds(row, SUB), :] = ni
            return nr, ni

        hr, hi = lax.fori_loop(0, NG, step, (carry[0], carry[1]))
        carry[0], carry[1] = hr, hi
        f_ref[0], f_ref[1] = hr, hi
        h_ref[...] = hs[...].astype(BF16)

    outs, couts = _call(
        body, [xin, mats, a], name=name, grid=(2, NS, NCH),
        in_specs=[pl.BlockSpec((None, TC, CW), lambda d, j, k: (d, ck(d, k), j)),
                  pl.BlockSpec((None, 2, None, CW, SL), lambda d, j, k: (d, 0, j, 0, 0)),
                  pl.BlockSpec((None, 2, 1, SL), lambda d, j, k: (d, 0, 0, j))],
        out_specs=[pl.BlockSpec((None, 2, TC, SL), lambda d, j, k: (d, 0, ck(d, k), j)),
                   pl.BlockSpec((None, 2, SUB, SL), lambda d, j, k: (d, 0, 0, j))],
        out_shape=[jax.ShapeDtypeStruct((2, 2, T, N), BF16), jax.ShapeDtypeStruct((2, 2, SUB, N), F32)],
        scratch_shapes=[pltpu.VMEM((2, SUB, SL), F32), pltpu.VMEM((2, TC, SL), F32)],
        sem=("parallel", "parallel", "arbitrary"), comm=comm)
    return outs if comm is None else (outs, couts)


def s5_fix(name, hloc, hin, a, mats, rev, comm=None):
    _, _, T, N = hloc.shape
    TC, NG, NCH, NS = _s5_dims(T, N)
    SL = S5_STRIP
    CW = mats.shape[-1]

    def ck(d, k):
        return jnp.where(_s5_backward(d, rev), NCH - 1 - k, k)

    def body(h_ref, hin_ref, a_ref, m_ref, ho_ref, y_ref, g, hs):
        @pl.when(pl.program_id(2) == 0)
        def _():
            g[...] = hin_ref[...]

        hs[...] = h_ref[...].astype(F32)
        ar, ai = jnp.broadcast_to(a_ref[0], (SUB, SL)), jnp.broadcast_to(a_ref[1], (SUB, SL))
        bw = _s5_backward(pl.program_id(0), rev)

        def step(t, c):
            gr, gi = c
            row = pl.multiple_of(jnp.where(bw, NG - 1 - t, t) * SUB, SUB)
            nr = ar * gr - ai * gi
            ni = ar * gi + ai * gr
            hs[0, pl.ds(row, SUB), :] += nr
            hs[1, pl.ds(row, SUB), :] += ni
            return nr, ni

        gr, gi = lax.fori_loop(0, NG, step, (g[0], g[1]))
        g[0], g[1] = gr, gi
        hb = hs[...].astype(BF16)
        ho_ref[...] = hb
        y_ref[...] = (jnp.dot(hb[0], m_ref[0], preferred_element_type=F32)
                      + jnp.dot(hb[1], m_ref[1], preferred_element_type=F32))

    outs, couts = _call(
        body, [hloc, hin, a, mats], name=name, grid=(2, NS, NCH),
        in_specs=[pl.BlockSpec((None, 2, TC, SL), lambda d, j, k: (d, 0, ck(d, k), j)),
                  pl.BlockSpec((None, 2, SUB, SL), lambda d, j, k: (d, 0, 0, j)),
                  pl.BlockSpec((None, 2, 1, SL), lambda d, j, k: (d, 0, 0, j)),
                  pl.BlockSpec((None, 2, None, SL, CW), lambda d, j, k: (d, 0, j, 0, 0))],
        out_specs=[pl.BlockSpec((None, 2, TC, SL), lambda d, j, k: (d, 0, ck(d, k), j)),
                   pl.BlockSpec((None, TC, CW), lambda d, j, k: (d, ck(d, k), j))],
        out_shape=[jax.ShapeDtypeStruct((2, 2, T, N), BF16), jax.ShapeDtypeStruct((2, T, NS * CW), F32)],
        scratch_shapes=[pltpu.VMEM((2, SUB, SL), F32), pltpu.VMEM((2, TC, SL), F32)],
        sem=("parallel", "parallel", "arbitrary"), comm=comm)
    return outs if comm is None else (outs, couts)


def s5_grads(name, g, h, u, dy, comm=None):
    _, _, T, N = g.shape
    W = u.shape[-1]
    TC, NG, NCH, NS = _s5_dims(T, N)
    CW, SL = W // NS, S5_STRIP

    def body(g_ref, h_ref, hp_ref, hl_ref, u_ref, dy_ref, dm_ref, dc_ref, da_ref, hs):
        k = pl.program_id(2)
        sub = lax.broadcasted_iota(jnp.int32, (SUB, SL), 0)

        hf = h_ref[...].astype(F32)

        @pl.when(pl.program_id(0) == 0)
        def _():
            for z in range(2):
                wrapped = jnp.where(sub == 0, 0.0, pltpu.roll(hl_ref[z].astype(F32)[SUB:], 1, 0))
                hs[z, 0:SUB, :] = jnp.where(k == 0, wrapped, hp_ref[z].astype(F32)[SUB:])
                hs[z, SUB:TC, :] = hf[z, 0:TC - SUB]

        @pl.when(pl.program_id(0) == 1)
        def _():
            for z in range(2):
                wrapped = jnp.where(sub == SUB - 1, 0.0, pltpu.roll(hl_ref[z].astype(F32)[:SUB], SUB - 1, 0))
                hs[z, TC - SUB:TC, :] = jnp.where(k == NCH - 1, wrapped, hp_ref[z].astype(F32)[:SUB])
                hs[z, 0:TC - SUB, :] = hf[z, SUB:TC]

        gr, gi, pr, pi = g_ref[0].astype(F32), g_ref[1].astype(F32), hs[0], hs[1]
        dar = jnp.sum((gr * pr + gi * pi).reshape(NG, SUB, SL), axis=0)
        dai = jnp.sum((gi * pr - gr * pi).reshape(NG, SUB, SL), axis=0)
        ub, dyb = u_ref[...].astype(BF16), dy_ref[...].astype(BF16)
        dm = [lax.dot_general(ub, g_ref[z], TN, preferred_element_type=F32) for z in range(2)]
        dc = [lax.dot_general(dyb, h_ref[z], TN, preferred_element_type=F32) for z in range(2)]

        @pl.when(k == 0)
        def _():
            da_ref[0], da_ref[1] = dar, dai
            for z in range(2):
                dm_ref[z], dc_ref[z] = dm[z], dc[z]

        @pl.when(k > 0)
        def _():
            da_ref[0] += dar
            da_ref[1] += dai
            for z in range(2):
                dm_ref[z] += dm[z]
                dc_ref[z] += dc[z]

    big = pl.BlockSpec((None, 2, TC, SL), lambda d, j, k: (d, 0, k, j))
    tok = pl.BlockSpec((None, TC, CW), lambda d, j, k: (d, k, j))
    mat = pl.BlockSpec((None, 2, None, CW, SL), lambda d, j, k: (d, 0, j, 0, 0))
    outs, couts = _call(
        body, [g, h, h, h, u, dy], name=name, grid=(2, NS, NCH),
        in_specs=[big, big,
                  pl.BlockSpec((None, 2, 2 * SUB, SL), lambda d, j, k: (
                      d, 0, jnp.where(d == 0, jnp.maximum(k * NG - 1, 0), jnp.minimum((k + 1) * NG, T // SUB - 1)) // 2, j)),
                  pl.BlockSpec((None, 2, 2 * SUB, SL), lambda d, j, k: (d, 0, jnp.where(d == 0, T // SUB - 1, 0) // 2, j)),
                  tok, tok],
        out_specs=[mat, mat, pl.BlockSpec((None, 2, SUB, SL), lambda d, j, k: (d, 0, 0, j))],
        out_shape=[jax.ShapeDtypeStruct((2, 2, NS, CW, SL), F32), jax.ShapeDtypeStruct((2, 2, NS, CW, SL), F32),
                   jax.ShapeDtypeStruct((2, 2, SUB, N), F32)],
        scratch_shapes=[pltpu.VMEM((2, TC, SL), F32)],
        sem=("parallel", "parallel", "arbitrary"), comm=comm)
    return outs if comm is None else (outs, couts)


def _interleave(seq):
    *lead, T, W = seq.shape
    n = len(lead)
    return seq.reshape(*lead, S5_SEG, T // S5_SEG, W).swapaxes(n, n + 1).reshape(*lead, T, W)


def _deinterleave(seq):
    *lead, T, W = seq.shape
    n = len(lead)
    return seq.reshape(*lead, T // S5_SEG, S5_SEG, W).swapaxes(n, n + 1).reshape(*lead, T, W)


def _s5_discretize(lam_re, lam_im, log_dt, b_re, b_im):
    dt = jnp.exp(log_dt)[..., None]
    mag = jnp.exp(lam_re * dt)
    a_re = mag * jnp.cos(lam_im * dt)
    a_im = mag * jnp.sin(lam_im * dt)
    den = jnp.square(lam_re) + jnp.square(lam_im)
    f_re = ((a_re - 1.0) * lam_re + a_im * lam_im) / den
    f_im = (a_im * lam_re - (a_re - 1.0) * lam_im) / den
    bb_re = f_re[..., None] * b_re - f_im[..., None] * b_im
    bb_im = f_re[..., None] * b_im + f_im[..., None] * b_re
    return a_re, a_im, bb_re, bb_im


_GPS = S5_STRIP // SSM_STATE


def _blockdiag(t):
    d2, G, P, Cg = t.shape
    t5 = t.reshape(d2, G // _GPS, _GPS, P, Cg).transpose(0, 1, 2, 4, 3)
    m = t5[:, :, :, :, None, :] * jnp.eye(_GPS, dtype=t.dtype)[None, None, :, None, :, None]
    return m.reshape(d2, G // _GPS, _GPS * Cg, _GPS * P)


def _blockdiag_extract(m, Cg, P):
    d2, NS = m.shape[:2]
    m6 = m.reshape(d2, NS, _GPS, Cg, _GPS, P)
    diag = jnp.stack([m6[:, :, i, :, i, :] for i in range(_GPS)], axis=2)
    return diag.transpose(0, 1, 2, 4, 3).reshape(d2, NS * _GPS, P, Cg)


def _cmul(a, b):
    return a[0] * b[0] - a[1] * b[1], a[0] * b[1] + a[1] * b[0]


def _cpow(a, n):
    out, base = None, a
    while n:
        if n & 1:
            out = base if out is None else _cmul(out, base)
        base = _cmul(base, base)
        n >>= 1
    return out


def _segment_carry(fin, apow, rev):
    per_dir = []
    for d in range(2):
        fr, fi = fin[d, 0], fin[d, 1]
        ap = (apow[0][d], apow[1][d])
        cr = ci = jnp.zeros_like(fr[0:1])
        outs = [None] * S5_SEG
        backward = (d == 1) != rev
        for s in (range(S5_SEG - 1, -1, -1) if backward else range(S5_SEG)):
            outs[s] = (cr, ci)
            pr, pi = _cmul(ap, (cr, ci))
            cr, ci = pr + fr[s:s + 1], pi + fi[s:s + 1]
        per_dir.append(jnp.stack([jnp.concatenate([o[0] for o in outs]), jnp.concatenate([o[1] for o in outs])]))
    return jnp.stack(per_dir)


def _coords():
    x, y, c = lax.axis_index("x"), lax.axis_index("y"), lax.axis_index("c")
    others = [(1 - x, y), (x, 1 - y), (1 - x, 1 - y)]
    return x, y, c, 2 * x + y, others


def _comm(name, ins, out_shapes, aliases, n_local, n_remote, plan):
    n_in, n_out = len(ins), len(out_shapes)

    def body(*refs):
        in_refs, out_refs = refs[:n_in], refs[n_in:n_in + n_out]
        send_sems, recv_sems, local_sems = refs[n_in + n_out:]
        x, y, c = lax.axis_index("x"), lax.axis_index("y"), lax.axis_index("c")
        locs, sends, lands = plan(in_refs, out_refs)
        assert len(locs) == n_local and len(sends) == n_remote and len(lands) == n_remote
        local = [pltpu.make_async_copy(s, d, local_sems.at[i]) for i, (s, d) in enumerate(locs)]
        for cp in local:
            cp.start()
        remote = [pltpu.make_async_remote_copy(src_ref=s, dst_ref=d, send_sem=send_sems.at[i], recv_sem=recv_sems.at[i],
                                               device_id=peer, device_id_type=MESH)
                  for i, (s, d, peer) in enumerate(sends)]
        for cp in remote:
            cp.start()
        for i, d in enumerate(lands):
            pltpu.make_async_remote_copy(src_ref=d, dst_ref=d, send_sem=send_sems.at[i], recv_sem=recv_sems.at[i],
                                         device_id=(x, y, c), device_id_type=MESH).wait_recv()
        for cp in remote:
            cp.wait_send()
        for cp in local:
            cp.wait()

    any_spec = pl.BlockSpec(memory_space=pl.ANY)
    return pl.pallas_call(
        body, name=name,
        in_specs=[any_spec] * n_in, out_specs=[any_spec] * n_out,
        out_shape=[jax.ShapeDtypeStruct(s, d) for s, d in out_shapes],
        input_output_aliases=aliases,
        scratch_shapes=[pltpu.SemaphoreType.DMA((n_remote,)), pltpu.SemaphoreType.DMA((n_remote,)),
                        pltpu.SemaphoreType.DMA((max(n_local, 1),))],
        compiler_params=pltpu.CompilerParams(has_side_effects=True),
    )(*ins)


def allgather_dev(name, v):
    M, Nc = v.shape

    def plan(in_refs, out_refs):
        (v_ref,), (o_ref,) = in_refs, out_refs
        x, y, c = lax.axis_index("x"), lax.axis_index("y"), lax.axis_index("c")

        def rows(px, py, pc):
            return o_ref.at[pl.ds((4 * px + 2 * py + pc) * M, M), :]

        peers = [(x ^ fx, y ^ fy, c ^ fc) for fx in (0, 1) for fy in (0, 1) for fc in (0, 1) if fx or fy or fc]
        return ([(v_ref, rows(x, y, c))],
                [(v_ref, rows(x, y, c), p) for p in peers],
                [rows(*p) for p in peers])

    return _comm(name, [v], [((N_DEV * M, Nc), v.dtype)], {}, 1, N_DEV - 1, plan)[0]


def allgather_chips_1(name, shards):
    def plan(in_refs, out_refs):
        x, y, c, chip, others = _coords()
        sends, lands = [], []
        for s_ref, g_ref in zip(in_refs, out_refs):
            hr = s_ref.shape[0] // 2
            mine = pl.ds(c * hr, hr)
            for qx, qy in others:
                sends.append((s_ref.at[mine], g_ref.at[chip, mine], (qx, qy, c)))
                lands.append(g_ref.at[2 * qx + qy, mine])
        return [], sends, lands

    n = len(shards)
    comm = (list(shards), [((N_CHIP,) + s.shape, s.dtype) for s in shards], {}, 3 * n, plan)
    return comm if name is None else _comm(name, comm[0], comm[1], comm[2], 0, comm[3], comm[4])


def allgather_chips_2(name, gathered, shards):
    n = len(gathered)

    def plan(in_refs, out_refs):
        x, y, c, chip, others = _coords()
        sends, lands = [], []
        for s_ref, g_ref in zip(in_refs[n:], out_refs):
            hr = g_ref.shape[1] // 2
            for qx, qy in others:
                q = 2 * qx + qy
                sends.append((g_ref.at[q, pl.ds(c * hr, hr)], g_ref.at[q, pl.ds(c * hr, hr)], (x, y, 1 - c)))
                lands.append(g_ref.at[q, pl.ds((1 - c) * hr, hr)])
            sends.append((s_ref, g_ref.at[chip], (x, y, 1 - c)))
            lands.append(g_ref.at[chip])
        return [], sends, lands

    comm = (list(gathered) + list(shards), [(g.shape, g.dtype) for g in gathered], {i: i for i in range(n)}, 4 * n, plan)
    return comm if name is None else _comm(name, comm[0], comm[1], comm[2], 0, comm[3], comm[4])


def reduce_1(name, grads):
    def plan(in_refs, out_refs):
        x, y, c, chip, others = _coords()
        sends, lands = [], []
        for g_ref, got_ref in zip(in_refs, out_refs):
            hr = g_ref.shape[1] // 2
            sends.append((g_ref.at[:, pl.ds((1 - c) * hr, hr), :], got_ref, (x, y, 1 - c)))
            lands.append(got_ref)
        return [], sends, lands

    n = len(grads)
    comm = (list(grads), [((g.shape[0], g.shape[1] // 2, g.shape[2]), g.dtype) for g in grads], {}, n, plan)
    return comm if name is None else _comm(name, comm[0], comm[1], comm[2], 0, comm[3], comm[4])


def _merge_comm(a, b):
    if a is None or b is None:
        return a if b is None else b
    na_in, na_out = len(a[0]), len(a[1])

    def plan(in_refs, out_refs):
        _, s1, l1 = a[4](in_refs[:na_in], out_refs[:na_out])
        _, s2, l2 = b[4](in_refs[na_in:], out_refs[na_out:])
        return [], s1 + s2, l1 + l2

    alias = dict(a[2])
    alias.update({na_in + i: na_out + j for i, j in b[2].items()})
    return (a[0] + b[0], a[1] + b[1], alias, a[3] + b[3], plan)


def reduce_2(name, parts):
    def plan(in_refs, out_refs):
        x, y, c, chip, others = _coords()
        sends, lands = [], []
        for t_ref, q_ref in zip(in_refs, out_refs):
            for qx, qy in others:
                sends.append((t_ref.at[2 * qx + qy], q_ref.at[chip], (qx, qy, c)))
                lands.append(q_ref.at[2 * qx + qy])
        return [], sends, lands

    n = len(parts)
    comm = (list(parts), [(p.shape, p.dtype) for p in parts], {}, 3 * n, plan)
    return comm if name is None else _comm(name, comm[0], comm[1], comm[2], 0, comm[3], comm[4])


def share_slots(name, slots):
    def plan(in_refs, out_refs):
        x, y, c, chip, others = _coords()
        (q_ref,) = out_refs
        return ([], [(q_ref.at[chip], q_ref.at[chip], (qx, qy, c)) for qx, qy in others],
                [q_ref.at[2 * qx + qy] for qx, qy in others])

    return _comm(name, [slots], [(slots.shape, slots.dtype)], {0: 0}, 0, N_CHIP - 1, plan)[0]


def allreduce_small(tag, buf, ids):
    got = reduce_1(tag + "_1", [buf[None]])[0][0]
    slots = share_slots(tag + "_2", pair_sum_to_slot(tag + "_add", buf, got, ids))
    full = reduce_3(tag + "_3", [sum_chips_to_half(tag + "_sum", slots, ids)])[0]
    return full.reshape(buf.shape)


def reduce_3(name, fulls):
    def plan(in_refs, out_refs):
        x, y, c, chip, others = _coords()
        sends, lands = [], []
        for o_ref in out_refs:
            sends.append((o_ref.at[c], o_ref.at[c], (x, y, 1 - c)))
            lands.append(o_ref.at[1 - c])
        return [], sends, lands

    n = len(fulls)
    return _comm(name, fulls, [(f.shape, f.dtype) for f in fulls], {i: i for i in range(n)}, 0, n, plan)


_WEIGHTS = ['c_ctx', 'w_mod', 'b_mod', 'g_mix', 'g_ffn', 'w_in', 'ssm_lam_re', 'ssm_lam_im', 'ssm_log_dt', 'ssm_b_re',
            'ssm_b_im', 'ssm_c_re', 'ssm_c_im', 'ssm_d', 'ssm_w_glu', 'na_rpb', 'w_out', 'cv_w_pw1', 'cv_dw_w', 'cv_dw_b',
            'cv_ln_g', 'cv_ln_b', 'cv_w_pw2', 'ffn_w_up', 'ffn_conv_w', 'ffn_conv_b', 'ffn_w_down', 'g_out']
_INPUTS = ['x', 'c', 'ctx'] + _WEIGHTS + ['loss_target'] + ['m_' + w for w in _WEIGHTS] + ['v_' + w for w in _WEIGHTS]
_GATHERED_SMALL = ['ffn_conv_w', 'cv_dw_w', 'cv_dw_b', 'cv_ln_g', 'cv_ln_b']


def _silu(v):
    return v * jax.nn.sigmoid(v)


def _pick_index(t, idx, axis):
    shape = [1] * t.ndim
    shape[axis] = t.shape[axis]
    mask = (jnp.arange(t.shape[axis]) == idx).reshape(shape)
    return jnp.sum(jnp.where(mask, t, jnp.zeros((), t.dtype)), axis=axis)


def _pack(arrs, cols, row_mult=SUB):
    flat = jnp.concatenate([a.reshape(-1).astype(F32) for a in arrs])
    n = flat.shape[0]
    unit = row_mult * cols
    flat = jnp.pad(flat, (0, (-n) % unit))
    return flat.reshape(-1, cols)


def _unpack(buf, shapes):
    flat = buf.reshape(-1)
    out, o = [], 0
    for s in shapes:
        n = int(np.prod(s))
        out.append(flat[o:o + n].reshape(s))
        o += n
    return out


def _carried(res, comm):
    return res if comm is not None else (res, [])


def _ffn_fwd(tag, xin, sh, sc, gt, g, wup, cw3, cb3, wdn, comm_up=None, comm_mid=None, comm_down=None):
    hf = norm_mod_fwd(tag + "_norm", xin, g * (1.0 + sc), sh)
    up3, got_up = _carried(mm_nn_pieces(tag + "_up", hf, wup, 0, N_CHIP, BF16, halves=2, comm=comm_up), comm_up)
    comm_mid = comm_mid(got_up) if callable(comm_mid) else comm_mid
    act, got_mid = _carried(ffn_mid_fwd(tag + "_mid", up3, cw3, cb3, comm=comm_mid), comm_mid)
    comm_down = comm_down(got_mid) if callable(comm_down) else comm_down
    yf, got_down = _carried(mm_nn(tag + "_down", act, wdn, BF16, comm=comm_down), comm_down)
    return gate_res_fwd(tag + "_res", xin, yf, gt), (xin, hf, up3, act, yf), got_up, got_mid, got_down


def _ffn_bwd(tag, dxo, saved, sc, gt, g, wup, cw3, cb3, wdn, comm_down=None, comm_mid=None):
    xin, hf, up3, act, yf = saved
    dyf, dgt = gate_res_bwd(tag + "_res_b", dxo, yf, gt)
    dact, got_down = _carried(mm_nt(tag + "_down_bx", dyf, wdn, BF16, comm=comm_down), comm_down)
    dwdn = mm_tn(tag + "_down_bw", act, dyf, BF16)
    comm_mid = comm_mid(got_down) if callable(comm_mid) else comm_mid
    (dup3, dcw3, dcb3), got_mid = _carried(ffn_mid_bwd(tag + "_mid_b", up3, dact, cw3, cb3, comm=comm_mid), comm_mid)
    dhf = mm_nt_pieces(tag + "_up_bx", dup3, wup, BF16, halves=2)
    dwup = mm_tn_pieces(tag + "_up_bw", hf, dup3, N_CHIP, BF16, halves=2)
    dxi, cs1, cs2 = norm_mod_bwd(tag + "_norm_b", xin, dhf, g * (1.0 + sc), dxo)
    return dxi, dict(dsh=cs1[0], dsc=cs2[0] * g, dgt=dgt[0], dg=cs2[0] * (1.0 + sc), dwup=dwup, dwdn=dwdn,
                     dcw=dcw3.transpose(1, 0, 2).reshape(3, -1), dcb=dcb3.reshape(-1)), got_down, got_mid


def kernel(x, c, ctx, c_ctx, w_mod, b_mod, g_mix, g_ffn, w_in, ssm_lam_re, ssm_lam_im, ssm_log_dt, ssm_b_re, ssm_b_im, ssm_c_re, ssm_c_im, ssm_d, ssm_w_glu, na_rpb, w_out, cv_w_pw1, cv_dw_w, cv_dw_b, cv_ln_g, cv_ln_b, cv_w_pw2, ffn_w_up, ffn_conv_w, ffn_conv_b, ffn_w_down, g_out, loss_target, m_c_ctx, m_w_mod, m_b_mod, m_g_mix, m_g_ffn, m_w_in, m_ssm_lam_re, m_ssm_lam_im, m_ssm_log_dt, m_ssm_b_re, m_ssm_b_im, m_ssm_c_re, m_ssm_c_im, m_ssm_d, m_ssm_w_glu, m_na_rpb, m_w_out, m_cv_w_pw1, m_cv_dw_w, m_cv_dw_b, m_cv_ln_g, m_cv_ln_b, m_cv_w_pw2, m_ffn_w_up, m_ffn_conv_w, m_ffn_conv_b, m_ffn_w_down, m_g_out, v_c_ctx, v_w_mod, v_b_mod, v_g_mix, v_g_ffn, v_w_in, v_ssm_lam_re, v_ssm_lam_im, v_ssm_log_dt, v_ssm_b_re, v_ssm_b_im, v_ssm_c_re, v_ssm_c_im, v_ssm_d, v_ssm_w_glu, v_na_rpb, v_w_out, v_cv_w_pw1, v_cv_dw_w, v_cv_dw_b, v_cv_ln_g, v_cv_ln_b, v_cv_w_pw2, v_ffn_w_up, v_ffn_conv_w, v_ffn_conv_b, v_ffn_w_down, v_g_out):
    p = dict(locals())
    xi, yi, ci = lax.axis_index("x"), lax.axis_index("y"), lax.axis_index("c")
    me, chip = 4 * xi + 2 * yi + ci, 2 * xi + yi
    xs, cx, tgt = x[0], ctx[0], loss_target[0]
    L, D = xs.shape
    Lc = cx.shape[0]
    T = L + Lc
    W = D // 2
    Cq = w_mod.shape[2]

    s_mix = [t.astype(BF16) for t in (w_in[0], ssm_w_glu[0], w_out[0])]
    s_ffn0 = [t.astype(BF16) for t in (ffn_w_up[0], ffn_w_down[0])]
    s_conv = [t.astype(BF16) for t in (cv_w_pw1[0], cv_w_pw2[0])]
    s_ffn1 = [t.astype(BF16) for t in (ffn_w_up[1], ffn_w_down[1])]
    (Win,) = allgather_chips_2("gather_win_2", allgather_chips_1("gather_win_1", s_mix[:1]), s_mix[:1])
    Fd = ffn_w_down.shape[1] * N_CHIP
    c_idx = jnp.reshape(ci, (1,)).astype(jnp.int32)
    ids = jnp.stack([chip, ci]).astype(jnp.int32)

    def added(tag, grads, got):
        return [add_half("reduce_%s_add%d" % (tag, i), g, r, c_idx) for i, (g, r) in enumerate(zip(grads, got))]

    small_shapes = [p[n].shape for n in _GATHERED_SMALL]
    sm = allgather_dev("gather_small", _pack([p[n] for n in _GATHERED_SMALL], 1024))
    sm = sm.reshape(N_DEV, -1)[0::2]
    per_chip = [_unpack(sm[q], small_shapes) for q in range(N_CHIP)]
    conv_w_f, dw_w_f, dw_b_f, ln_g_f, ln_b_f = (jnp.concatenate([pc[i] for pc in per_chip], axis=-1)
                                                for i in range(len(_GATHERED_SMALL)))
    cw3 = [conv_w_f[l].reshape(3, 2, Fd).transpose(1, 0, 2) for l in range(2)]
    cb3 = [ffn_conv_b[l].reshape(2, 1, Fd) for l in range(2)]
    dw_w_f, dw_b_f, ln_g_f, ln_b_f = dw_w_f[0], dw_b_f[0], ln_g_f[0], ln_b_f[0]

    c_all = allgather_dev("gather_c", jnp.zeros((SUB, D), F32).at[0].set(c[0])).reshape(N_DEV, SUB, D)[:, 0]
    S16 = jnp.concatenate([_silu(c_all), _silu(c_ctx)[None], jnp.zeros((2 * SUB - N_DEV - 1, D), F32)])
    modp = mm_nn_pieces("mod_fwd", S16, w_mod, 0, 2, F32)
    modg = allgather_dev("gather_mod", modp).reshape(N_DEV, 2 * SUB, 2, Cq)[0::2]

    def mod_row(r):
        return r.transpose(1, 0, 2).reshape(2, N_CHIP * Cq) + b_mod

    mod_me = mod_row(_pick_index(modg, me, 1))
    mod_c = mod_row(modg[:, N_DEV])
    mods = [[mod_me[l, i * D:(i + 1) * D] for i in range(N_MOD)] for l in range(2)]
    shc, scc = mod_c[0, :D], mod_c[0, D:2 * D]

    sh_m, sc_m, gt_m, sh_f, sc_f, gt_f = mods[0]
    h0 = norm_mod_fwd("l0_norm", xs, g_mix[0] * (1.0 + sc_m), sh_m)
    hc0 = norm_mod_fwd("l0_norm_c", cx, g_mix[0] * (1.0 + scc), shc)
    u = mm_nn_pieces("l0_in_u", h0, Win, 0, 1, F32)
    qkv, g_mix1 = mm_nn_pieces("l0_in_qkv", h0, Win, 1, 3, BF16, comm=allgather_chips_1(None, s_mix[1:]))
    uc = mm_nn_pieces("l0_in_uc", hc0, Win, 0, 1, F32)
    kvc = mm_nn_pieces("l0_in_kvc", hc0, Win, 2, 2, BF16)

    lam_re, lam_im, log_dt = ssm_lam_re[0], ssm_lam_im[0], ssm_log_dt[0]
    b_re, b_im, c_re, c_im = ssm_b_re[0], ssm_b_im[0], ssm_c_re[0], ssm_c_im[0]
    (a_re, a_im, bb_re, bb_im), disc_vjp = jax.vjp(_s5_discretize, lam_re, lam_im, log_dt, b_re, b_im)
    G, P, Cg = bb_re.shape[1:]
    N = G * P
    a_re, a_im = a_re.reshape(2, 1, N), a_im.reshape(2, 1, N)
    a_f, a_b = jnp.stack([a_re, a_im], axis=1), jnp.stack([a_re, -a_im], axis=1)
    Bblk = jnp.stack([_blockdiag(bb_re), _blockdiag(bb_im)], axis=1)
    Cblk = jnp.stack([_blockdiag(c_re.swapaxes(-1, -2)), -_blockdiag(c_im.swapaxes(-1, -2))], axis=1)
    apow = _cpow((a_re, a_im), T // S5_SEG)

    useq = _interleave(jnp.stack([jnp.concatenate([uc, u]), jnp.concatenate([u, uc])]).astype(BF16))
    (hloc, fin), (Wglu, Wout) = s5_scan("s5_scan", useq, Bblk.astype(BF16), a_f, rev=False,
                                        comm=allgather_chips_2(None, g_mix1, s_mix[1:]))
    Wglu, Wout = Wglu.reshape(-1, Wglu.shape[-1]), Wout.reshape(-1, D)
    (hst, yseq), g_dn0 = s5_fix("s5_fix", hloc, _segment_carry(fin, apow, False), a_f, Cblk.swapaxes(-1, -2).astype(BF16),
                                rev=False, comm=allgather_chips_1(None, s_ffn0[1:]))
    ys = _deinterleave(yseq)
    y0, y1 = ys[0, Lc:], ys[1, :L]
    s5o = glu_fwd("s5_glu", u, y0, y1, ssm_d[0], Wglu)

    bias = na_bias(na_rpb[0])
    (o_na, lse), (g_up0, Wdn0) = natten_fwd(
        "na_fwd", qkv, kvc, bias,
        comm=_merge_comm(allgather_chips_1(None, s_ffn0[:1]), allgather_chips_2(None, g_dn0, s_ffn0[1:])))
    mixcat = jnp.concatenate([s5o, o_na], axis=1)
    ymix, (Wup0,) = mm_nn("l0_out", mixcat, Wout, BF16, comm=allgather_chips_2(None, [g_up0], s_ffn0[:1]))
    x1 = gate_res_fwd("l0_res", xs, ymix, gt_m)
    Wdn0 = Wdn0.reshape(-1, D)
    x2, ffn0, (g_up1,), (g_dn1, Wup1), (g_pw1, g_pw2, Wdn1) = _ffn_fwd(
        "f0", x1, sh_f, sc_f, gt_f, g_ffn[0], Wup0, cw3[0], cb3[0], Wdn0,
        comm_up=allgather_chips_1(None, s_ffn1[:1]),
        comm_mid=lambda got_up: _merge_comm(allgather_chips_1(None, s_ffn1[1:]), allgather_chips_2(None, got_up, s_ffn1[:1])),
        comm_down=lambda got_mid: _merge_comm(allgather_chips_1(None, s_conv), allgather_chips_2(None, got_mid[:1], s_ffn1[1:])))
    Wpw1, Wpw2 = allgather_chips_2("gather_conv_2", [g_pw1, g_pw2], s_conv)
    Wpw2 = Wpw2.reshape(-1, D)
    Wup, Wdn = [Wup0, Wup1], [Wdn0.reshape(-1, D), Wdn1.reshape(-1, D)]

    sh_v, sc_v, gt_v, sh_g, sc_g, gt_g = mods[1]
    hcv = norm_mod_fwd("l1_norm", x2, g_mix[1] * (1.0 + sc_v), sh_v)
    ag3 = mm_nn_pieces("l1_pw1", hcv, Wpw1, 0, N_CHIP, BF16, halves=2)
    z1, z3 = conf_mid_fwd("l1_mid", ag3, dw_w_f, dw_b_f, ln_g_f, ln_b_f)
    ycv = mm_nn("l1_pw2", z3, Wpw2, BF16)
    x3 = gate_res_fwd("l1_res", x2, ycv, gt_v)
    x4, ffn1, _, _, _ = _ffn_fwd("f1", x3, sh_g, sc_g, gt_g, g_ffn[1], Wup[1], cw3[1], cb3[1], Wdn[1])

    dx4, dg_out, loss_part = loss_head("loss", x4, g_out, tgt)
    loss = lax.psum(loss_part[0, 0], ("x", "y", "c"))

    dx3, gf1, _, _ = _ffn_bwd("f1", dx4, ffn1, sc_g, gt_g, g_ffn[1], Wup[1], cw3[1], cb3[1], Wdn[1])
    g_up1, g_dn1 = [gf1["dwup"]], [gf1["dwdn"].reshape(N_CHIP, -1, D)]
    dycv, dgt_v = gate_res_bwd("l1_res_b", dx3, ycv, gt_v)
    dz3, got = mm_nt("l1_pw2_bx", dycv, Wpw2, BF16, comm=reduce_1(None, g_up1))
    parts_up1 = added("up1", g_up1, got)
    dWpw2, got = mm_tn("l1_pw2_bw", z3, dycv, BF16, comm=reduce_1(None, g_dn1))
    parts_dn1 = added("dn1", g_dn1, got)
    dz1, dln_g, dln_b = conf_ln_bwd("l1_ln_b", z1, dz3, ln_g_f, ln_b_f)
    (dag3, ddw_w, ddw_b), slots_up1 = conf_conv_bwd("l1_conv_b", ag3, dz1, dw_w_f, comm=reduce_2(None, parts_up1))
    dhcv, slots_dn1 = mm_nt_pieces("l1_pw1_bx", dag3, Wpw1, BF16, halves=2, comm=reduce_2(None, parts_dn1))
    dWpw1 = mm_tn_pieces("l1_pw1_bw", hcv, dag3, N_CHIP, BF16, halves=2)
    dx2, cs1_v, cs2_v = norm_mod_bwd("l1_norm_b", x2, dhcv, g_mix[1] * (1.0 + sc_v), dx3)
    g_conv = [dWpw1, dWpw2.reshape(N_CHIP, -1, D)]

    held = {}

    def conv_stage_2(got_down):
        held["parts_conv"] = added("conv", g_conv, got_down)
        return reduce_2(None, held["parts_conv"])

    dx1, gf0, _, slots_conv = _ffn_bwd("f0", dx2, ffn0, sc_f, gt_f, g_ffn[0], Wup[0], cw3[0], cb3[0], Wdn[0],
                                       comm_down=reduce_1(None, g_conv), comm_mid=conv_stage_2)
    parts_conv = held["parts_conv"]
    g_up0, g_dn0 = [gf0["dwup"]], [gf0["dwdn"].reshape(N_CHIP, -1, D)]
    dymix, dgt_m = gate_res_bwd("l0_res_b", dx1, ymix, gt_m)
    dmix, got = mm_nt("l0_out_bx", dymix, Wout, BF16, comm=reduce_1(None, g_up0))
    parts_up0 = added("up0", g_up0, got)
    dWout, got = mm_tn("l0_out_bw", mixcat, dymix, BF16, comm=reduce_1(None, g_dn0))
    parts_dn0 = added("dn0", g_dn0, got)
    (dq, dk, dv, dkc, dvc, dbias), slots_up0 = natten_bwd("na_bwd", qkv, kvc, bias, o_na, lse, dmix,
                                                          comm=reduce_2(None, parts_up0))
    dy, zg, dzz, dd_skip = glu_bwd("s5_glu_b", u, y0, y1, ssm_d[0], Wglu, dmix)
    dWglu = mm_tn("s5_glu_bw", zg, dzz, BF16)
    g_mix2 = [dWglu.reshape(N_CHIP, -1, W), dWout.reshape(N_CHIP, -1, D)]

    zc = jnp.zeros((Lc, W), F32)
    dyseq = _interleave(jnp.stack([jnp.concatenate([zc, dy]), jnp.concatenate([dy, zc])]).astype(BF16))
    (gloc, gfin), slots_dn0 = s5_scan("s5_scan_b", dyseq, Cblk.astype(BF16), a_b, rev=True, comm=reduce_2(None, parts_dn0))
    apow_b = (apow[0], -apow[1])
    (gst, duseq), got = s5_fix("s5_fix_b", gloc, _segment_carry(gfin, apow_b, True), a_b, Bblk.swapaxes(-1, -2).astype(BF16),
                               rev=True, comm=reduce_1(None, g_mix2))
    parts_mix2 = added("mix2", g_mix2, got)
    (dBm, dCm, da8), slots_mix2 = s5_grads("s5_grads", gst, hst, useq, dyseq, comm=reduce_2(None, parts_mix2))
    dus = _deinterleave(duseq)
    du = fma3("s5_du", dy, dus[0, Lc:], dus[1, :L], ssm_d[0], BF16)
    duc = dus[0, :Lc] + dus[1, L:]

    d_in = jnp.concatenate([du, dq, dk.astype(BF16), dv.astype(BF16)], axis=1)
    d_in_c = jnp.concatenate([duc.astype(BF16), jnp.zeros((Lc, W), BF16), dkc.astype(BF16), dvc.astype(BF16)], axis=1)
    dh0 = mm_nt_pieces("l0_in_bx", d_in, Win, BF16)
    dhc0 = mm_nt_pieces("l0_in_bxc", d_in_c, Win, BF16)
    dWin = mm_tn_pieces("l0_in_bw", jnp.concatenate([hc0, h0]), jnp.concatenate([d_in_c, d_in]), N_CHIP, BF16)
    dx0, cs1_m, cs2_m = norm_mod_bwd("l0_norm_b", xs, dh0, g_mix[0] * (1.0 + sc_m), dx1)
    _, cs1_c, cs2_c = norm_mod_bwd("l0_norm_bc", cx, dhc0, g_mix[0] * (1.0 + scc), jnp.zeros_like(cx))

    dmod0 = jnp.concatenate([cs1_m[0], cs2_m[0] * g_mix[0], dgt_m[0], gf0["dsh"], gf0["dsc"], gf0["dgt"]])
    dmod1 = jnp.concatenate([cs1_v[0], cs2_v[0] * g_mix[1], dgt_v[0], gf1["dsh"], gf1["dsc"], gf1["dgt"]])
    dmodc = jnp.concatenate([cs1_c[0], cs2_c[0] * g_mix[0], jnp.zeros((4 * D,), F32)])
    dm_rows = jnp.concatenate([jnp.stack([dmod0, dmod1, dmodc]), jnp.zeros((SUB - 3, N_MOD * D), F32)])
    dm_all = allgather_dev("gather_dmod", dm_rows).reshape(N_DEV, SUB, N_MOD * D)
    dm_sum = sum_lead("sum_dmod", dm_all, F32)
    pad7 = jnp.zeros((2 * SUB - N_DEV - 1, N_MOD * D), F32)
    dMod = [jnp.concatenate([dm_all[:, 0], dm_sum[2:3], pad7]), jnp.concatenate([dm_all[:, 1], jnp.zeros_like(dm_sum[2:3]), pad7])]
    dMod_cols = [_pick_index(m.reshape(m.shape[0], N_CHIP, Cq), chip, 1) for m in dMod]
    g_w_mod = jnp.stack([mm_tn("mod_bw%d" % l, S16, dMod_cols[l], F32) for l in range(2)])
    g_b_mod = jnp.stack([dm_sum[0] + dm_sum[2], dm_sum[1]])
    ds_part = mm_nt("mod_bx", dMod_cols[0], w_mod[0], F32)
    ds_all = allgather_dev("gather_dsc", jnp.zeros((SUB, D), F32).at[0].set(ds_part[N_DEV]))
    ds_c = sum_lead("sum_dsc", ds_all.reshape(N_DEV, SUB, D)[0::2], F32)[0]
    sg_c = jax.nn.sigmoid(c_ctx)
    g_c_ctx = ds_c * sg_c * (1.0 + c_ctx * (1.0 - sg_c))

    g_rpb_loc = na_bias_grad(dbias)

    dbb = [_blockdiag_extract(dBm[:, z], Cg, P) for z in range(2)]
    dcc = [_blockdiag_extract(dCm[:, z], Cg, P).swapaxes(-1, -2) for z in range(2)]
    da = jnp.sum(da8, axis=2).reshape(2, 2, G, P)
    small = {
        "g_mix": jnp.stack([cs2_m[0] * (1.0 + sc_m) + cs2_c[0] * (1.0 + scc), cs2_v[0] * (1.0 + sc_v)]),
        "g_ffn": jnp.stack([gf0["dg"], gf1["dg"]]),
        "a_re": da[:, 0], "a_im": da[:, 1], "bb_re": dbb[0], "bb_im": dbb[1], "c_re": dcc[0], "c_im": -dcc[1],
        "ssm_d": dd_skip, "na_rpb": g_rpb_loc, "cv_dw_w": ddw_w, "cv_dw_b": ddw_b, "cv_ln_g": dln_g, "cv_ln_b": dln_b,
        "ffn_conv_w": jnp.stack([gf0["dcw"], gf1["dcw"]]), "ffn_conv_b": jnp.stack([gf0["dcb"], gf1["dcb"]]),
        "g_out": dg_out,
    }
    skeys = list(small)
    sbuf = _pack([small[k] for k in skeys], 1024, 4 * SUB)
    ssum = dict(zip(skeys, _unpack(allreduce_small("reduce_small", sbuf, ids), [small[k].shape for k in skeys])))
    g_lam_re, g_lam_im, g_log_dt, g_b_re, g_b_im = disc_vjp((ssum["a_re"], ssum["a_im"], ssum["bb_re"], ssum["bb_im"]))

    def my_cols(t):
        n = t.shape[-1] // N_CHIP
        return _pick_index(t.reshape(t.shape[:-1] + (N_CHIP, n)), chip, t.ndim - 1)

    parts_win = added("win", [dWin], reduce_1("reduce_win_1", [dWin]))
    slots_win = reduce_2("reduce_win_2", parts_win)
    parts = parts_win + parts_mix2 + parts_conv + parts_up0 + parts_dn0 + parts_up1 + parts_dn1
    slots = [*slots_win, *slots_mix2, *slots_conv, *slots_up0, *slots_dn0, *slots_up1, *slots_dn1]
    fulls = [sum_slots("reduce_sum_%d" % i, s, t, ids) for i, (s, t) in enumerate(zip(slots, parts))]
    full = [f.reshape(-1, f.shape[-1]) for f in reduce_3("reduce_g_3", fulls)]
    gWin, gWglu, gWout, gWpw1, gWpw2, gWup0, gWdn0, gWup1, gWdn1 = full

    grads = {
        "c_ctx": g_c_ctx, "w_mod": g_w_mod, "b_mod": g_b_mod, "g_mix": ssum["g_mix"], "g_ffn": ssum["g_ffn"],
        "w_in": gWin[None], "ssm_lam_re": g_lam_re[None], "ssm_lam_im": g_lam_im[None], "ssm_log_dt": g_log_dt[None],
        "ssm_b_re": g_b_re[None], "ssm_b_im": g_b_im[None], "ssm_c_re": ssum["c_re"][None], "ssm_c_im": ssum["c_im"][None],
        "ssm_d": ssum["ssm_d"], "ssm_w_glu": gWglu[None], "na_rpb": ssum["na_rpb"][None], "w_out": gWout[None],
        "cv_w_pw1": gWpw1[None], "cv_dw_w": my_cols(ssum["cv_dw_w"])[None], "cv_dw_b": my_cols(ssum["cv_dw_b"]),
        "cv_ln_g": my_cols(ssum["cv_ln_g"]), "cv_ln_b": my_cols(ssum["cv_ln_b"]), "cv_w_pw2": gWpw2[None],
        "ffn_w_up": jnp.stack([gWup0, gWup1]), "ffn_conv_w": my_cols(ssum["ffn_conv_w"]), "ffn_conv_b": ssum["ffn_conv_b"],
        "ffn_w_down": jnp.stack([gWdn0, gWdn1]), "g_out": ssum["g_out"][0],
    }
    grads = {k: grads[k].reshape(p[k].shape) for k in _WEIGHTS}

    large = [k for k in _WEIGHTS if p[k].size >= (1 << 18)]
    tiny = [k for k in _WEIGHTS if k not in large]
    delta, new_m, new_v = {}, {}, {}
    for k in large:
        delta[k], new_m[k], new_v[k] = adamw("adamw_" + k, p[k], grads[k], p["m_" + k], p["v_" + k])
    packs = [_pack([src[pre + k] for k in tiny], 1024) for src, pre in ((p, ""), (grads, ""), (p, "m_"), (p, "v_"))]
    outs = adamw("adamw_small", *packs)
    shapes = [p[k].shape for k in tiny]
    for dst, buf in zip((delta, new_m, new_v), outs):
        dst.update(zip(tiny, _unpack(buf, shapes)))

    return (loss, dx0[None], *[grads[k] for k in _WEIGHTS], *[delta[k] for k in _WEIGHTS],
            *[new_m[k] for k in _WEIGHTS], *[new_v[k] for k in _WEIGHTS])
```

```python
import functools
import math

import numpy as np
import jax
import jax.numpy as jnp
from jax import lax
from jax.experimental import pallas as pl
from jax.experimental.pallas import tpu as pltpu

F32, BF16 = jnp.float32, jnp.bfloat16
MESH = pl.DeviceIdType.MESH
V7X_VMEM_LIMIT = 56 << 20
LANE, SUB = 128, 8
N_CHIP, N_DEV = 4, 8

GRID_W = 64
N_MOD = 6
SSM_GROUP, SSM_STATE = 16, 64
NA_HEAD_DIM, NA_WIN_R, NA_WIN_C = 128, 8, 16
EPS = 1e-6
NEG = -1e30
ADAM_LR, ADAM_B1, ADAM_B2, ADAM_EPS, ADAM_WD, ADAM_STEP = 0.001, 0.9, 0.999, 1e-08, 0.01, 10
S5_STRIP = 512
S5_SEG = 8

NN = (((1,), (0,)), ((), ()))
NT = (((1,), (1,)), ((), ()))
TN = (((0,), (0,)), ((), ()))


def _params(*sem, side_effects=False):
    return pltpu.CompilerParams(dimension_semantics=sem if sem else None, vmem_limit_bytes=V7X_VMEM_LIMIT,
                                has_side_effects=side_effects)


def _call(body, args, *, name, grid, in_specs, out_specs, out_shape, sem, scratch_shapes=(), comm=None):
    out_specs, out_shape, scratch_shapes = list(out_specs), list(out_shape), list(scratch_shapes)
    if comm is None:
        outs = pl.pallas_call(body, name=name, grid=grid, in_specs=list(in_specs), out_specs=out_specs, out_shape=out_shape,
                              scratch_shapes=scratch_shapes, compiler_params=_params(*sem))(*args)
        return list(outs), []
    c_args, c_shapes, c_alias, n_remote, plan = comm
    n_in, n_out, n_ci, n_co, n_sc = len(args), len(out_shape), len(c_args), len(c_shapes), len(scratch_shapes)

    def wrapped(*refs):
        ins, cins = refs[:n_in], refs[n_in:n_in + n_ci]
        o0 = n_in + n_ci
        outs, couts = refs[o0:o0 + n_out], refs[o0 + n_out:o0 + n_out + n_co]
        s0 = o0 + n_out + n_co
        scr, (send_sems, recv_sems) = refs[s0:s0 + n_sc], refs[s0 + n_sc:]
        pids = [pl.program_id(a) for a in range(len(grid))]
        first = functools.reduce(jnp.logical_and, [q == 0 for q in pids])
        last = functools.reduce(jnp.logical_and, [q == g - 1 for q, g in zip(pids, grid)])
        me = (lax.axis_index("x"), lax.axis_index("y"), lax.axis_index("c"))

        def copies():
            _, sends, lands = plan(cins, couts)
            assert len(sends) == n_remote and len(lands) == n_remote
            out = [pltpu.make_async_remote_copy(src_ref=s, dst_ref=d, send_sem=send_sems.at[i], recv_sem=recv_sems.at[i],
                                                device_id=peer, device_id_type=MESH) for i, (s, d, peer) in enumerate(sends)]
            arrivals = [pltpu.make_async_remote_copy(src_ref=d, dst_ref=d, send_sem=send_sems.at[i], recv_sem=recv_sems.at[i],
                                                     device_id=me, device_id_type=MESH) for i, d in enumerate(lands)]
            return out, arrivals

        @pl.when(first)
        def _():
            for cp in copies()[0]:
                cp.start()

        body(*ins, *outs, *scr)

        @pl.when(last)
        def _():
            out, arrivals = copies()
            for cp in arrivals:
                cp.wait_recv()
            for cp in out:
                cp.wait_send()

    any_spec = pl.BlockSpec(memory_space=pl.ANY)
    res = pl.pallas_call(
        wrapped, name=name, grid=grid,
        in_specs=[*in_specs, *[any_spec] * n_ci], out_specs=[*out_specs, *[any_spec] * n_co],
        out_shape=[*out_shape, *[jax.ShapeDtypeStruct(s, d) for s, d in c_shapes]],
        input_output_aliases={n_in + i: n_out + j for i, j in c_alias.items()},
        scratch_shapes=[*scratch_shapes, pltpu.SemaphoreType.DMA((n_remote,)), pltpu.SemaphoreType.DMA((n_remote,))],
        compiler_params=_params(*["arbitrary"] * len(grid), side_effects=True),
    )(*args, *c_args)
    return list(res[:n_out]), list(res[n_out:])


def _pick(n, pref, mult=LANE):
    if n <= pref:
        return n
    best = None
    for t in range(mult, pref + 1, mult):
        if n % t == 0:
            best = t
    assert best is not None, (n, pref, mult)
    return best


def _sigmoid(x):
    return 1.0 / (1.0 + jnp.exp(-x))


def _mm(name, a, b, *, dims, grid, a_spec, b_spec, o_spec, out_shape, out_dtype, acc_shape, exact=False, comm=None):
    nk = grid[2]

    def body(a_ref, b_ref, o_ref, *scratch):
        if exact:
            part = lax.dot_general(a_ref[...], b_ref[...], dims, preferred_element_type=F32,
                                   precision=lax.Precision.HIGHEST)
        else:
            part = lax.dot_general(a_ref[...].astype(BF16), b_ref[...].astype(BF16), dims,
                                   preferred_element_type=F32)
        if nk == 1:
            o_ref[...] = part.astype(o_ref.dtype)
        else:
            acc = scratch[0]
            kk = pl.program_id(2)

            @pl.when(kk == 0)
            def _():
                acc[...] = part

            @pl.when(kk > 0)
            def _():
                acc[...] += part

            @pl.when(kk == nk - 1)
            def _():
                o_ref[...] = acc[...].astype(o_ref.dtype)

    outs, couts = _call(body, [a, b], name=name, grid=grid, in_specs=[a_spec, b_spec], out_specs=[o_spec],
                        out_shape=[jax.ShapeDtypeStruct(out_shape, out_dtype)],
                        scratch_shapes=[] if nk == 1 else [pltpu.VMEM(acc_shape, F32)],
                        sem=("parallel", "parallel", "arbitrary"), comm=comm)
    return outs[0] if comm is None else (outs[0], couts)


MM_VMEM_BUDGET = 36 << 20


def _fit(M, N, cost, m_mult=SUB):
    best = None
    for tm in sorted({_pick(M, p, m_mult) for p in (2048, 1024, 512, 256, 128)}):
        for tn in sorted({_pick(N, p) for p in (1408, 1024, 512, 256, 128)}):
            if best is None or (cost(tm, tn) <= MM_VMEM_BUDGET and tm * tn > best[0] * best[1]):
                best = (tm, tn)
    return best


def _sz(t):
    return jnp.dtype(t).itemsize


def mm_nn_pieces(name, a, w, p0, n_p, out_dtype, halves=1, comm=None):
    M, K = a.shape
    Nq = w.shape[2]
    tm, tn = _fit(M, Nq, lambda m, n: 2 * (m * K * _sz(a.dtype) + K * n * _sz(w.dtype) + m * n * _sz(out_dtype)))
    tpp = Nq // tn
    pph = n_p // halves
    if halves == 1:
        o_spec = pl.BlockSpec((tm, tn), lambda i, j, k: (i, j))
        oshape = (M, n_p * Nq)
    else:
        o_spec = pl.BlockSpec((None, tm, tn), lambda i, j, k: ((j // tpp) // pph, i, ((j // tpp) % pph) * tpp + j % tpp))
        oshape = (halves, M, pph * Nq)
    return _mm(name, a, w, dims=NN, grid=(M // tm, n_p * tpp, 1),
               a_spec=pl.BlockSpec((tm, K), lambda i, j, k: (i, 0)),
               b_spec=pl.BlockSpec((None, K, tn), lambda i, j, k: (p0 + j // tpp, 0, j % tpp)),
               o_spec=o_spec, out_shape=oshape, out_dtype=out_dtype, acc_shape=(tm, tn), comm=comm)


def mm_nn(name, a, w, out_dtype, exact=False, comm=None):
    M, K = a.shape
    N = w.shape[1]
    tm, tn = _fit(M, N, lambda m, n: 2 * (m * K * _sz(a.dtype) + K * n * _sz(w.dtype) + m * n * _sz(out_dtype)))
    return _mm(name, a, w, dims=NN, grid=(M // tm, N // tn, 1),
               a_spec=pl.BlockSpec((tm, K), lambda i, j, k: (i, 0)),
               b_spec=pl.BlockSpec((K, tn), lambda i, j, k: (0, j)),
               o_spec=pl.BlockSpec((tm, tn), lambda i, j, k: (i, j)),
               out_shape=(M, N), out_dtype=out_dtype, acc_shape=(tm, tn), exact=exact, comm=comm)


def mm_nt(name, dy, w, out_dtype, exact=False, comm=None):
    M, N = dy.shape
    K = w.shape[0]
    tm, tn = _fit(M, K, lambda m, n: 2 * (m * N * _sz(dy.dtype) + n * N * _sz(w.dtype) + m * n * _sz(out_dtype)))
    return _mm(name, dy, w, dims=NT, grid=(M // tm, K // tn, 1),
               a_spec=pl.BlockSpec((tm, N), lambda i, j, k: (i, 0)),
               b_spec=pl.BlockSpec((tn, N), lambda i, j, k: (j, 0)),
               o_spec=pl.BlockSpec((tm, tn), lambda i, j, k: (i, j)),
               out_shape=(M, K), out_dtype=out_dtype, acc_shape=(tm, tn), exact=exact, comm=comm)


def mm_nt_pieces(name, dy, w, out_dtype, halves=1, comm=None):
    P, K, Nq = w.shape
    M = dy.shape[-2]
    tm, tn = _fit(M, K, lambda m, n: 2 * (m * Nq * _sz(dy.dtype) + n * Nq * _sz(w.dtype) + m * n * _sz(out_dtype)) + 4 * m * n)
    pph = P // halves
    if halves == 1:
        a_spec = pl.BlockSpec((tm, Nq), lambda i, j, k: (i, k))
    else:
        a_spec = pl.BlockSpec((None, tm, Nq), lambda i, j, k: (k // pph, i, k % pph))
    return _mm(name, dy, w, dims=NT, grid=(M // tm, K // tn, P),
               a_spec=a_spec,
               b_spec=pl.BlockSpec((None, tn, Nq), lambda i, j, k: (k, j, 0)),
               o_spec=pl.BlockSpec((tm, tn), lambda i, j, k: (i, j)),
               out_shape=(M, K), out_dtype=out_dtype, acc_shape=(tm, tn), comm=comm)


def mm_nt_list(name, dys, w, out_dtype):
    P, K, Nq = w.shape
    M = dys[0].shape[0]
    assert len(dys) == P
    tm, tn = _fit(M, K, lambda m, n: 2 * (sum(m * Nq * _sz(d.dtype) for d in dys) + n * Nq * _sz(w.dtype)
                                          + m * n * _sz(out_dtype)) + 4 * m * n)

    def body(*refs):
        d_refs, w_ref, o_ref, acc = refs[:P], refs[P], refs[P + 1], refs[P + 2]
        kk = pl.program_id(2)
        for q in range(P):
            @pl.when(kk == q)
            def _(q=q):
                part = lax.dot_general(d_refs[q][...].astype(BF16), w_ref[...], NT, preferred_element_type=F32)
                acc[...] = part if q == 0 else acc[...] + part

        @pl.when(kk == P - 1)
        def _():
            o_ref[...] = acc[...].astype(o_ref.dtype)

    return pl.pallas_call(
        body, name=name, grid=(M // tm, K // tn, P),
        in_specs=[*[pl.BlockSpec((tm, Nq), lambda i, j, k: (i, 0)) for _ in range(P)],
                  pl.BlockSpec((None, tn, Nq), lambda i, j, k: (k, j, 0))],
        out_specs=pl.BlockSpec((tm, tn), lambda i, j, k: (i, j)),
        out_shape=jax.ShapeDtypeStruct((M, K), out_dtype),
        scratch_shapes=[pltpu.VMEM((tm, tn), F32)],
        compiler_params=_params("parallel", "arbitrary", "arbitrary"),
    )(*dys, w)


def mm_tn(name, a, dy, out_dtype, comm=None):
    M, K = a.shape
    N = dy.shape[1]
    tm, tn = _fit(K, N, lambda m, n: 2 * (M * m * _sz(a.dtype) + M * n * _sz(dy.dtype) + m * n * _sz(out_dtype)), LANE)
    return _mm(name, a, dy, dims=TN, grid=(K // tm, N // tn, 1),
               a_spec=pl.BlockSpec((M, tm), lambda i, j, k: (0, i)),
               b_spec=pl.BlockSpec((M, tn), lambda i, j, k: (0, j)),
               o_spec=pl.BlockSpec((tm, tn), lambda i, j, k: (i, j)),
               out_shape=(K, N), out_dtype=out_dtype, acc_shape=(tm, tn), comm=comm)


def mm_tn_pieces(name, a, dy, n_p, out_dtype, halves=1, comm=None):
    M, K = a.shape
    Nq = (dy.shape[-1] * halves) // n_p
    tm, tn = _fit(K, Nq, lambda m, n: 2 * (M * m * _sz(a.dtype) + M * n * _sz(dy.dtype) + m * n * _sz(out_dtype)), LANE)
    tpp = Nq // tn
    pph = n_p // halves
    if halves == 1:
        b_spec = pl.BlockSpec((M, tn), lambda i, j, k: (0, j))
    else:
        b_spec = pl.BlockSpec((None, M, tn), lambda i, j, k: ((j // tpp) // pph, 0, ((j // tpp) % pph) * tpp + j % tpp))
    return _mm(name, a, dy, dims=TN, grid=(K // tm, n_p * tpp, 1),
               a_spec=pl.BlockSpec((M, tm), lambda i, j, k: (0, i)),
               b_spec=b_spec,
               o_spec=pl.BlockSpec((None, tm, tn), lambda i, j, k: (j // tpp, i, j % tpp)),
               out_shape=(n_p, K, Nq), out_dtype=out_dtype, acc_shape=(tm, tn), comm=comm)


def _row_call(name, body, ins, in_kinds, outs, rows, tr, scratch=()):
    def spec(kind, shape):
        if isinstance(kind, pl.BlockSpec):
            return kind
        if kind == "row":
            return pl.BlockSpec((tr,) + tuple(shape[1:]), lambda i: (i,) + (0,) * (len(shape) - 1))
        return pl.BlockSpec(tuple(shape), lambda i: (0,) * len(shape))

    return pl.pallas_call(
        body, name=name, grid=(rows // tr,),
        in_specs=[spec(k, a.shape) for k, a in zip(in_kinds, ins)],
        out_specs=[spec(k, s) for k, s, _ in outs],
        out_shape=[jax.ShapeDtypeStruct(s, d) for _, s, d in outs],
        scratch_shapes=list(scratch),
        compiler_params=_params("arbitrary"),
    )(*ins)


def _acc(ref, val):
    @pl.when(pl.program_id(0) == 0)
    def _():
        ref[...] = val

    @pl.when(pl.program_id(0) > 0)
    def _():
        ref[...] += val


def norm_mod_fwd(name, x, w, b, tr=256):
    rows, d = x.shape
    tr = _pick(rows, tr, SUB)

    def body(x_ref, w_ref, b_ref, h_ref):
        xv = x_ref[...]
        r = lax.rsqrt(jnp.mean(xv * xv, axis=-1, keepdims=True) + EPS)
        h_ref[...] = (xv * r * w_ref[...] + b_ref[...]).astype(BF16)

    return _row_call(name, body, [x, w.reshape(1, d), b.reshape(1, d)], ["row", "vec", "vec"],
                     [("row", (rows, d), BF16)], rows, tr)[0]


def norm_mod_bwd(name, x, dh, w, dx_in, tr=256):
    rows, d = x.shape
    tr = _pick(rows, tr, SUB)

    def body(x_ref, dh_ref, w_ref, dxi_ref, dx_ref, cs1_ref, cs2_ref):
        xv = x_ref[...]
        r = lax.rsqrt(jnp.mean(xv * xv, axis=-1, keepdims=True) + EPS)
        xn = xv * r
        dhv = dh_ref[...].astype(F32)
        dxn = dhv * w_ref[...]
        dx_ref[...] = dxi_ref[...] + r * (dxn - xn * jnp.mean(dxn * xn, axis=-1, keepdims=True))
        _acc(cs1_ref, jnp.sum(dhv, axis=0, keepdims=True))
        _acc(cs2_ref, jnp.sum(dhv * xn, axis=0, keepdims=True))

    return _row_call(name, body, [x, dh, w.reshape(1, d), dx_in], ["row", "row", "vec", "row"],
                     [("row", (rows, d), F32), ("acc", (1, d), F32), ("acc", (1, d), F32)], rows, tr)


def gate_res_fwd(name, x, y, gate, tr=256):
    rows, d = x.shape
    tr = _pick(rows, tr, SUB)

    def body(x_ref, y_ref, g_ref, o_ref):
        o_ref[...] = x_ref[...] + g_ref[...] * y_ref[...].astype(F32)

    return _row_call(name, body, [x, y, gate.reshape(1, d)], ["row", "row", "vec"],
                     [("row", (rows, d), F32)], rows, tr)[0]


def gate_res_bwd(name, dx, y, gate, tr=256):
    rows, d = dx.shape
    tr = _pick(rows, tr, SUB)

    def body(dx_ref, y_ref, g_ref, dy_ref, dg_ref):
        dxv = dx_ref[...]
        dy_ref[...] = (g_ref[...] * dxv).astype(BF16)
        _acc(dg_ref, jnp.sum(dxv * y_ref[...].astype(F32), axis=0, keepdims=True))

    return _row_call(name, body, [dx, y, gate.reshape(1, d)], ["row", "row", "vec"],
                     [("row", (rows, d), BF16), ("acc", (1, d), F32)], rows, tr)


def loss_head(name, x, g, target, tr=256):
    rows, d = x.shape
    tr = _pick(rows, tr, SUB)

    def body(x_ref, g_ref, t_ref, dx_ref, dg_ref, loss_ref):
        xv = x_ref[...]
        r = lax.rsqrt(jnp.mean(xv * xv, axis=-1, keepdims=True) + EPS)
        xn = xv * r
        err = xn * g_ref[...] - t_ref[...]
        dy = err * (1.0 / d)
        dxn = dy * g_ref[...]
        dx_ref[...] = r * (dxn - xn * jnp.mean(dxn * xn, axis=-1, keepdims=True))
        _acc(dg_ref, jnp.sum(dy * xn, axis=0, keepdims=True))
        part = 0.5 * jnp.sum(jnp.sum(err * err, axis=-1, keepdims=True) * (1.0 / d), axis=0, keepdims=True)
        _acc(loss_ref, jnp.broadcast_to(part, (1, LANE)))

    return _row_call(name, body, [x, g.reshape(1, d), target], ["row", "vec", "row"],
                     [("row", (rows, d), F32), ("acc", (1, d), F32), ("acc", (1, LANE), F32)], rows, tr)


def fma3(name, a, b, c, dvec, out_dtype, tr=256):
    rows, d = a.shape
    tr = _pick(rows, tr, SUB)

    def body(a_ref, b_ref, c_ref, d_ref, o_ref):
        o_ref[...] = (d_ref[...] * a_ref[...] + b_ref[...] + c_ref[...]).astype(o_ref.dtype)

    return _row_call(name, body, [a, b, c, dvec.reshape(1, d)], ["row", "row", "row", "vec"],
                     [("row", (rows, d), out_dtype)], rows, tr)[0]


def sum_lead(name, a, out_dtype, tr=512):
    n, rows, cols = a.shape
    tr = _pick(rows, tr, 16)

    def body(a_ref, o_ref):
        acc = a_ref[0].astype(F32)
        for s in range(1, n):
            acc = acc + a_ref[s].astype(F32)
        o_ref[...] = acc.astype(o_ref.dtype)

    return pl.pallas_call(
        body, name=name, grid=(rows // tr,),
        in_specs=[pl.BlockSpec((n, tr, cols), lambda i: (0, i, 0))],
        out_specs=pl.BlockSpec((tr, cols), lambda i: (i, 0)),
        out_shape=jax.ShapeDtypeStruct((rows, cols), out_dtype),
        compiler_params=_params("parallel"),
    )(a)


def adamw(name, w, g, m, v, comm=None):
    shape = w.shape
    cols = shape[-1]
    w2, g2, m2, v2 = (t.reshape(-1, cols) for t in (w, g, m, v))
    rows = w2.shape[0]
    tr, tc = _pick(rows, 256, SUB), _pick(cols, 1536)
    c1 = 1.0 - ADAM_B1 ** ADAM_STEP
    c2 = 1.0 - ADAM_B2 ** ADAM_STEP

    def body(w_ref, g_ref, m_ref, v_ref, d_ref, mo_ref, vo_ref):
        gv = g_ref[...]
        mn = ADAM_B1 * m_ref[...] + (1.0 - ADAM_B1) * gv
        vn = ADAM_B2 * v_ref[...] + (1.0 - ADAM_B2) * (gv * gv)
        mo_ref[...] = mn
        vo_ref[...] = vn
        d_ref[...] = -ADAM_LR * ((mn / c1) / (jnp.sqrt(vn / c2) + ADAM_EPS) + ADAM_WD * w_ref[...])

    blk = pl.BlockSpec((tr, tc), lambda i, j: (i, j))
    outs, couts = _call(body, [w2, g2, m2, v2], name=name, grid=(rows // tr, cols // tc), in_specs=[blk] * 4,
                        out_specs=[blk] * 3, out_shape=[jax.ShapeDtypeStruct(w2.shape, F32)] * 3,
                        sem=("parallel", "parallel"), comm=comm)
    outs = tuple(o.reshape(shape) for o in outs)
    return outs if comm is None else (outs, couts)


def add_half(name, grad, got, c_idx, tr=256):
    Pn, R, C = grad.shape
    hr = R // 2
    tr = _pick(hr, tr, HALO)
    nb = hr // tr

    def body(c_ref, a_ref, b_ref, o_ref):
        o_ref[...] = (a_ref[...].astype(F32) + b_ref[...].astype(F32)).astype(o_ref.dtype)

    return pl.pallas_call(
        body, name=name,
        grid_spec=pltpu.PrefetchScalarGridSpec(
            num_scalar_prefetch=1, grid=(Pn, nb),
            in_specs=[pl.BlockSpec((None, tr, C), lambda q, i, c: (q, c[0] * nb + i, 0)),
                      pl.BlockSpec((None, tr, C), lambda q, i, c: (q, i, 0))],
            out_specs=pl.BlockSpec((None, tr, C), lambda q, i, c: (q, i, 0))),
        out_shape=jax.ShapeDtypeStruct((Pn, hr, C), BF16),
        compiler_params=_params("parallel", "parallel"),
    )(c_idx, grad, got)


def pair_sum_to_slot(name, buf, got, ids, tr=256):
    R, C = buf.shape
    hr = R // 2
    tr = _pick(hr, tr, SUB)
    nb = hr // tr

    def body(ids_ref, a_ref, b_ref, o_ref):
        o_ref[...] = a_ref[...] + b_ref[...]

    return pl.pallas_call(
        body, name=name,
        grid_spec=pltpu.PrefetchScalarGridSpec(
            num_scalar_prefetch=1, grid=(nb,),
            in_specs=[pl.BlockSpec((tr, C), lambda i, ids: (ids[1] * nb + i, 0)),
                      pl.BlockSpec((tr, C), lambda i, ids: (i, 0))],
            out_specs=pl.BlockSpec((None, tr, C), lambda i, ids: (ids[0], i, 0))),
        out_shape=jax.ShapeDtypeStruct((N_CHIP, hr, C), F32),
        compiler_params=_params("parallel"),
    )(ids, buf, got)


def sum_chips_to_half(name, slots, ids, tr=256):
    n, hr, C = slots.shape
    tr = _pick(hr, tr, SUB)

    def body(ids_ref, s_ref, o_ref):
        acc = s_ref[0]
        for q in range(1, n):
            acc = acc + s_ref[q]
        o_ref[...] = acc

    return pl.pallas_call(
        body, name=name,
        grid_spec=pltpu.PrefetchScalarGridSpec(
            num_scalar_prefetch=1, grid=(hr // tr,),
            in_specs=[pl.BlockSpec((n, tr, C), lambda i, ids: (0, i, 0))],
            out_specs=pl.BlockSpec((None, tr, C), lambda i, ids: (ids[1], i, 0))),
        out_shape=jax.ShapeDtypeStruct((2, hr, C), F32),
        compiler_params=_params("parallel"),
    )(ids, slots)


def sum_slots(name, slots, mine, ids, tr=256):
    Pn, hr, C = slots.shape
    tr = _pick(hr, tr, HALO)

    def body(ids_ref, m_ref, s1_ref, s2_ref, s3_ref, o_ref):
        o_ref[...] = (m_ref[...].astype(F32) + s1_ref[...].astype(F32)) + (s2_ref[...].astype(F32) + s3_ref[...].astype(F32))

    def other(k):
        return pl.BlockSpec((None, tr, C), lambda i, ids: ((ids[0] + k) % Pn, i, 0))

    return pl.pallas_call(
        body, name=name,
        grid_spec=pltpu.PrefetchScalarGridSpec(
            num_scalar_prefetch=1, grid=(hr // tr,),
            in_specs=[pl.BlockSpec((None, tr, C), lambda i, ids: (ids[0], i, 0)), other(1), other(2), other(3)],
            out_specs=pl.BlockSpec((None, tr, C), lambda i, ids: (ids[1], i, 0))),
        out_shape=jax.ShapeDtypeStruct((2, hr, C), F32),
        compiler_params=_params("parallel"),
    )(ids, mine, slots, slots, slots)


HALO = 16


def _halo_specs(lead, R, tn, n_rows, col_of):
    nb, nblk = R // HALO, n_rows // HALO

    def mk(rows, row_of):
        return pl.BlockSpec((lead, rows, tn), lambda *g: (0, row_of(g[-1]), col_of(g)))

    return (mk(HALO, lambda i: jnp.maximum(i * nb - 1, 0)), mk(R, lambda i: i),
            mk(HALO, lambda i: jnp.minimum((i + 1) * nb, nblk - 1)))


def _fill_halo(dst, i, last, R, prev, cur, nxt):
    nd = len(dst.shape)
    lead = (slice(None),) * (nd - 2)
    dst[lead + (slice(0, HALO), slice(None))] = jnp.where(i == 0, 0.0, prev)
    dst[lead + (slice(HALO, HALO + R), slice(None))] = cur
    dst[lead + (slice(HALO + R, HALO + R + HALO), slice(None))] = jnp.where(i == last, 0.0, nxt)


def _shift_mats(n):
    i = np.arange(n)
    return jnp.asarray(np.stack([i[:, None] - 1 == i[None, :], i[:, None] + 1 == i[None, :]]), BF16)


def _shifted(s_ref, xb):
    return (jnp.dot(s_ref[0], xb, preferred_element_type=F32), jnp.dot(s_ref[1], xb, preferred_element_type=F32))


def ffn_mid_fwd(name, up3, cw, cb, R=256, tn=512, comm=None):
    _, L, Fd = up3.shape
    R, tn = _pick(L, R, HALO), _pick(Fd, tn)
    nrow = L // R

    def body(p_ref, c_ref, n_ref, w_ref, b_ref, s_ref, act_ref):
        i = pl.program_id(1)
        row = lax.broadcasted_iota(jnp.int32, (R, tn), 0)
        cv = []
        for z in range(2):
            xb = c_ref[z]
            before = jnp.where(i == 0, 0.0, p_ref[z].astype(F32)[HALO - 1:HALO])
            after = jnp.where(i == nrow - 1, 0.0, n_ref[z].astype(F32)[0:1])
            dn, up = _shifted(s_ref, xb)
            dn = jnp.where(row == 0, before, dn)
            up = jnp.where(row == R - 1, after, up)
            cv.append(b_ref[z] + w_ref[z, 0:1, :] * dn + w_ref[z, 1:2, :] * xb.astype(F32) + w_ref[z, 2:3, :] * up)
        u, g = cv
        act_ref[...] = (u * g * _sigmoid(g)).astype(BF16)

    hs = _halo_specs(2, R, tn, L, lambda g: g[0])
    outs, couts = _call(
        body, [up3, up3, up3, cw, cb, _shift_mats(R)], name=name, grid=(Fd // tn, nrow),
        in_specs=[*hs, pl.BlockSpec((2, 3, tn), lambda j, i: (0, 0, j)), pl.BlockSpec((2, 1, tn), lambda j, i: (0, 0, j)),
                  pl.BlockSpec((2, R, R), lambda j, i: (0, 0, 0))],
        out_specs=[pl.BlockSpec((R, tn), lambda j, i: (i, j))],
        out_shape=[jax.ShapeDtypeStruct((L, Fd), BF16)], sem=("parallel", "arbitrary"), comm=comm)
    return outs[0] if comm is None else (outs[0], couts)


def ffn_mid_bwd(name, up3, dact, cw, cb, R=256, tn=512, comm=None):
    _, L, Fd = up3.shape
    R, tn = _pick(L, R, HALO), _pick(Fd, tn)
    nrow = L // R

    def gate_grads(u, g, d):
        sg = _sigmoid(g)
        return d * g * sg, d * u * sg * (1.0 + g * (1.0 - sg))

    def body(pu, cu, nu, pd, cd, nd, w_ref, b_ref, s_ref, dup_ref, dcw_ref, dcb_ref):
        i = pl.program_id(1)
        first, last = i == 0, i == nrow - 1
        row = lax.broadcasted_iota(jnp.int32, (R, tn), 0)
        cv, cv_b, cv_a, taps = [], [], [], []
        for z in range(2):
            xb = cu[z]
            xf = xb.astype(F32)
            pf = jnp.where(first, 0.0, pu[z].astype(F32))
            nf = jnp.where(last, 0.0, nu[z].astype(F32))
            xm2, xm1, xp0, xp1 = pf[HALO - 2:HALO - 1], pf[HALO - 1:HALO], nf[0:1], nf[1:2]
            dn, up = _shifted(s_ref, xb)
            dn = jnp.where(row == 0, xm1, dn)
            up = jnp.where(row == R - 1, xp0, up)
            w0, w1, w2, b = w_ref[z, 0:1, :], w_ref[z, 1:2, :], w_ref[z, 2:3, :], b_ref[z]
            cv.append(b + w0 * dn + w1 * xf + w2 * up)
            cv_b.append(b + w0 * xm2 + w1 * xm1 + w2 * xf[0:1])
            cv_a.append(b + w0 * xf[R - 1:R] + w1 * xp0 + w2 * xp1)
            taps.append((dn, xf, up))
        dcs = gate_grads(cv[0], cv[1], cd[0].astype(F32))
        dcs_b = gate_grads(cv_b[0], cv_b[1], jnp.where(first, 0.0, pd[0].astype(F32)[HALO - 1:HALO]))
        dcs_a = gate_grads(cv_a[0], cv_a[1], jnp.where(last, 0.0, nd[0].astype(F32)[0:1]))

        @pl.when(first)
        def _():
            dcw_ref[...] = jnp.zeros_like(dcw_ref)
            dcb_ref[...] = jnp.zeros_like(dcb_ref)

        for z in range(2):
            dc = dcs[z]
            dc_dn, dc_up = _shifted(s_ref, dc.astype(BF16))
            dc_dn = jnp.where(row == 0, dcs_b[z], dc_dn)
            dc_up = jnp.where(row == R - 1, dcs_a[z], dc_up)
            dup_ref[z] = (w_ref[z, 0:1, :] * dc_up + w_ref[z, 1:2, :] * dc + w_ref[z, 2:3, :] * dc_dn).astype(BF16)
            dcb_ref[z] += jnp.sum(dc, axis=0, keepdims=True)
            for k in range(3):
                dcw_ref[z, k:k + 1, :] += jnp.sum(dc * taps[z][k], axis=0, keepdims=True)

    hu = _halo_specs(2, R, tn, L, lambda g: g[0])
    hd = _halo_specs(1, R, tn, L, lambda g: g[0])
    outs, couts = _call(
        body, [up3, up3, up3, dact[None], dact[None], dact[None], cw, cb, _shift_mats(R)], name=name, grid=(Fd // tn, nrow),
        in_specs=[*hu, *hd, pl.BlockSpec((2, 3, tn), lambda j, i: (0, 0, j)), pl.BlockSpec((2, 1, tn), lambda j, i: (0, 0, j)),
                  pl.BlockSpec((2, R, R), lambda j, i: (0, 0, 0))],
        out_specs=[pl.BlockSpec((2, R, tn), lambda j, i: (0, i, j)), pl.BlockSpec((2, 3, tn), lambda j, i: (0, 0, j)),
                   pl.BlockSpec((2, 1, tn), lambda j, i: (0, 0, j))],
        out_shape=[jax.ShapeDtypeStruct((2, L, Fd), BF16), jax.ShapeDtypeStruct((2, 3, Fd), F32),
                   jax.ShapeDtypeStruct((2, 1, Fd), F32)],
        sem=("parallel", "arbitrary"), comm=comm)
    return outs if comm is None else (outs, couts)


def _glu_z0(blk):
    return blk[0].astype(F32) * _sigmoid(blk[1].astype(F32))


def _sublane_copies(ref, cs):
    n = ref.shape[1]
    blk = ref[0, :, cs]
    for b in range(1, SUB):
        ref[b, :, cs] = pltpu.roll(blk, n - b, 0)


def _tap(ref, offset, rows, cs):
    return ref[offset % SUB, pl.ds(offset - offset % SUB, rows), cs]


def conf_mid_fwd(name, ag3, dw_w, dw_b, ln_g, ln_b, R=128, cb=256):
    _, L, C = ag3.shape
    K = dw_w.shape[0]
    pad = (K - 1) // 2
    assert pad <= HALO
    R, cb = _pick(L, R, HALO), _pick(C, cb)
    nrow = L // R

    def body(p_ref, c_ref, n_ref, w_ref, b_ref, g_ref, bb_ref, z1_ref, z3_ref, s_ref):
        i = pl.program_id(0)
        _fill_halo(s_ref.at[0], i, nrow - 1, R, _glu_z0(p_ref), _glu_z0(c_ref), _glu_z0(n_ref))
        for c0 in range(0, C, cb):
            cs = slice(c0, c0 + cb)
            _sublane_copies(s_ref, cs)
            acc = jnp.broadcast_to(b_ref[:, cs], (R, cb))
            for k in range(K):
                acc = acc + w_ref[k:k + 1, cs] * _tap(s_ref, HALO - pad + k, R, cs)
            z1_ref[:, cs] = acc
        z1 = z1_ref[...]
        zc = z1 - jnp.mean(z1, axis=-1, keepdims=True)
        zn = zc * lax.rsqrt(jnp.mean(zc * zc, axis=-1, keepdims=True) + EPS)
        z2 = zn * g_ref[...] + bb_ref[...]
        z3_ref[...] = (z2 * _sigmoid(z2)).astype(BF16)

    hs = _halo_specs(2, R, C, L, lambda g: 0)
    vec = pl.BlockSpec((1, C), lambda i: (0, 0))
    return pl.pallas_call(
        body, name=name, grid=(nrow,),
        in_specs=[*hs, pl.BlockSpec((K, C), lambda i: (0, 0)), vec, vec, vec],
        out_specs=[pl.BlockSpec((R, C), lambda i: (i, 0)), pl.BlockSpec((R, C), lambda i: (i, 0))],
        out_shape=[jax.ShapeDtypeStruct((L, C), F32), jax.ShapeDtypeStruct((L, C), BF16)],
        scratch_shapes=[pltpu.VMEM((SUB, R + 2 * HALO, C), F32)],
        compiler_params=_params("parallel"),
    )(ag3, ag3, ag3, dw_w, dw_b.reshape(1, C), ln_g.reshape(1, C), ln_b.reshape(1, C))


def conf_ln_bwd(name, z1, dz3, ln_g, ln_b, tr=256):
    rows, C = z1.shape
    tr = _pick(rows, tr, HALO)

    def body(z_ref, d_ref, g_ref, b_ref, dz_ref, dg_ref, db_ref):
        z1v = z_ref[...]
        zc = z1v - jnp.mean(z1v, axis=-1, keepdims=True)
        rs = lax.rsqrt(jnp.mean(zc * zc, axis=-1, keepdims=True) + EPS)
        zn = zc * rs
        z2 = zn * g_ref[...] + b_ref[...]
        sg = _sigmoid(z2)
        dz2 = d_ref[...].astype(F32) * sg * (1.0 + z2 * (1.0 - sg))
        _acc(dg_ref, jnp.sum(dz2 * zn, axis=0, keepdims=True))
        _acc(db_ref, jnp.sum(dz2, axis=0, keepdims=True))
        dzn = dz2 * g_ref[...]
        dz1 = rs * (dzn - jnp.mean(dzn, axis=-1, keepdims=True) - zn * jnp.mean(dzn * zn, axis=-1, keepdims=True))
        dz_ref[...] = dz1.astype(BF16)

    return _row_call(name, body, [z1, dz3, ln_g.reshape(1, C), ln_b.reshape(1, C)], ["row", "row", "vec", "vec"],
                     [("row", (rows, C), BF16), ("acc", (1, C), F32), ("acc", (1, C), F32)], rows, tr)


def conf_conv_bwd(name, ag3, dz1, dw_w, R=128, cb=256, comm=None):
    _, L, C = ag3.shape
    K = dw_w.shape[0]
    pad = (K - 1) // 2
    R, cb = _pick(L, R, HALO), _pick(C, cb)
    nrow = L // R

    def body(pa, ca, na, pd, cd, nd, w_ref, dag_ref, dw_ref, db_ref, s_ref, d_ref, z_ref):
        i = pl.program_id(0)
        _fill_halo(s_ref.at[0], i, nrow - 1, R, _glu_z0(pa), _glu_z0(ca), _glu_z0(na))
        _fill_halo(d_ref.at[0], i, nrow - 1, R, pd[0].astype(F32), cd[0].astype(F32), nd[0].astype(F32))

        @pl.when(i == 0)
        def _():
            dw_ref[...] = jnp.zeros_like(dw_ref)
            db_ref[...] = jnp.zeros_like(db_ref)

        for c0 in range(0, C, cb):
            cs = slice(c0, c0 + cb)
            _sublane_copies(s_ref, cs)
            _sublane_copies(d_ref, cs)
            dcur = d_ref[0, pl.ds(HALO, R), cs]
            acc = jnp.zeros((R, cb), F32)
            for k in range(K):
                acc = acc + w_ref[k:k + 1, cs] * _tap(d_ref, HALO + pad - k, R, cs)
                dw_ref[k:k + 1, cs] += jnp.sum(dcur * _tap(s_ref, HALO - pad + k, R, cs), axis=0, keepdims=True)
            z_ref[:, cs] = acc
            db_ref[:, cs] += jnp.sum(dcur, axis=0, keepdims=True)
        dz0 = z_ref[...]
        a = ca[0].astype(F32)
        sg = _sigmoid(ca[1].astype(F32))
        dag_ref[0] = (dz0 * sg).astype(BF16)
        dag_ref[1] = (dz0 * a * sg * (1.0 - sg)).astype(BF16)

    ha = _halo_specs(2, R, C, L, lambda g: 0)
    hd = _halo_specs(1, R, C, L, lambda g: 0)
    outs, couts = _call(
        body, [ag3, ag3, ag3, dz1[None], dz1[None], dz1[None], dw_w], name=name, grid=(nrow,),
        in_specs=[*ha, *hd, pl.BlockSpec((K, C), lambda i: (0, 0))],
        out_specs=[pl.BlockSpec((2, R, C), lambda i: (0, i, 0)), pl.BlockSpec((K, C), lambda i: (0, 0)),
                   pl.BlockSpec((1, C), lambda i: (0, 0))],
        out_shape=[jax.ShapeDtypeStruct((2, L, C), BF16), jax.ShapeDtypeStruct((K, C), F32),
                   jax.ShapeDtypeStruct((1, C), F32)],
        scratch_shapes=[pltpu.VMEM((SUB, R + 2 * HALO, C), F32), pltpu.VMEM((SUB, R + 2 * HALO, C), F32),
                        pltpu.VMEM((R, C), F32)],
        sem=("arbitrary",), comm=comm)
    return outs if comm is None else (outs, couts)


_GELU_C = math.sqrt(2.0 / math.pi)


def _gelu(x):
    return 0.5 * x * (1.0 + jnp.tanh(_GELU_C * (x + 0.044715 * x * x * x)))


def _gelu_grad(x):
    t = jnp.tanh(_GELU_C * (x + 0.044715 * x * x * x))
    return 0.5 * (1.0 + t) + 0.5 * x * (1.0 - t * t) * _GELU_C * (1.0 + 3.0 * 0.044715 * x * x)


def glu_fwd(name, u, y0, y1, d, wg, tr=512):
    rows, W = u.shape
    tr = _pick(rows, tr, HALO)

    def body(u_ref, y0_ref, y1_ref, d_ref, w_ref, o_ref):
        z = _gelu(d_ref[...] * u_ref[...] + y0_ref[...] + y1_ref[...])
        zz = jnp.dot(z.astype(BF16), w_ref[...], preferred_element_type=F32)
        o_ref[...] = (z * _sigmoid(zz)).astype(BF16)

    return _row_call(name, body, [u, y0, y1, d.reshape(1, W), wg], ["row", "row", "row", "vec", "vec"],
                     [("row", (rows, W), BF16)], rows, tr)[0]


def glu_bwd(name, u, y0, y1, d, wg, dmix, tr=512):
    rows, W = u.shape
    tr = _pick(rows, tr, HALO)

    def body(u_ref, y0_ref, y1_ref, d_ref, w_ref, do_ref, dy_ref, z_ref, dzz_ref, dd_ref):
        uv = u_ref[...]
        y = d_ref[...] * uv + y0_ref[...] + y1_ref[...]
        z = _gelu(y)
        zz = jnp.dot(z.astype(BF16), w_ref[...], preferred_element_type=F32)
        sg = _sigmoid(zz)
        do = do_ref[...].astype(F32)
        dzz = (do * z * sg * (1.0 - sg)).astype(BF16)
        dz = do * sg + lax.dot_general(dzz, w_ref[...], NT, preferred_element_type=F32)
        dy = dz * _gelu_grad(y)
        dy_ref[...] = dy
        z_ref[...] = z.astype(BF16)
        dzz_ref[...] = dzz
        _acc(dd_ref, jnp.sum(dy * uv, axis=0, keepdims=True))

    do_spec = pl.BlockSpec((tr, W), lambda i: (i, 0))
    return _row_call(name, body, [u, y0, y1, d.reshape(1, W), wg, dmix], ["row", "row", "row", "vec", "vec", do_spec],
                     [("row", (rows, W), F32), ("row", (rows, W), BF16), ("row", (rows, W), BF16), ("acc", (1, W), F32)],
                     rows, tr)


NA_KEYS = NA_WIN_R * GRID_W


NA_PAIRS = NA_WIN_R // 2


def na_bias(rpb):
    H, nr, nc = rpb.shape
    e, ok = _na_col_select()
    rp = jnp.pad(rpb.reshape(H * nr, nc), ((0, (-H * nr) % SUB), (0, LANE - nc)))
    cols = mm_nn("na_bias_mm", rp, jnp.asarray(e, F32), F32, exact=True)[:H * nr]
    tiles = (cols + jnp.asarray(np.where(ok, 0.0, NEG), F32)).reshape(H, nr, GRID_W, GRID_W)
    return jnp.concatenate([tiles[:, :-1], tiles[:, 1:]], axis=-1)


def na_bias_grad(db2):
    H, n2 = db2.shape[:2]
    left, right = db2[..., :GRID_W], db2[..., GRID_W:]
    tiles = jnp.pad(left, ((0, 0), (0, 1), (0, 0), (0, 0))) + jnp.pad(right, ((0, 0), (1, 0), (0, 0), (0, 0)))
    flat = tiles.reshape(H * (n2 + 1), GRID_W * GRID_W)
    flat = jnp.pad(flat, ((0, (-flat.shape[0]) % SUB), (0, 0)))
    dcol = mm_nt("na_bias_fold", flat, na_bias_fold_matrix(), F32, exact=True)
    return dcol[:H * (n2 + 1), :2 * NA_WIN_C - 1].reshape(H, n2 + 1, 2 * NA_WIN_C - 1)


def _na_col_select():
    q = np.arange(GRID_W)
    cs = np.clip(q - NA_WIN_C // 2, 0, GRID_W - NA_WIN_C)
    ok = ((q[None, :] >= cs[:, None]) & (q[None, :] < cs[:, None] + NA_WIN_C)).reshape(-1)
    cidx = np.clip(q[None, :] - q[:, None] + (NA_WIN_C - 1), 0, 2 * NA_WIN_C - 2).reshape(-1)
    return (cidx[None, :] == np.arange(LANE)[:, None]) & ok[None, :], ok


def na_bias_fold_matrix():
    return jnp.asarray(_na_col_select()[0], F32)


def _na_window(r, rows):
    kr0 = jnp.clip(r - NA_WIN_R // 2, 0, rows - NA_WIN_R)
    return pl.multiple_of(kr0 * GRID_W, GRID_W), r - kr0


def _na_dims(qkv, kvc):
    L = qkv.shape[0]
    NA = qkv.shape[1] // 3
    H = NA // NA_HEAD_DIM
    hp = 2 if H % 2 == 0 else 1
    return L, NA, H, hp, H // hp, L // GRID_W, kvc.shape[0]


def _na_bias_tile(b_ref, hh, off):
    return jnp.concatenate([b_ref[hh, NA_WIN_R - 1 - off + 2 * j] for j in range(NA_PAIRS)], axis=-1)


def natten_fwd(name, qkv, kvc, bias, comm=None):
    L, NA, H, hp, G, rows, Lc = _na_dims(qkv, kvc)
    scale = NA_HEAD_DIM ** -0.5
    wd = hp * NA_HEAD_DIM

    def body(q_ref, k_ref, v_ref, kc_ref, vc_ref, b_ref, o_ref, lse_ref):
        st, off = _na_window(pl.program_id(1), rows)
        for hh in range(hp):
            sl = slice(hh * NA_HEAD_DIM, (hh + 1) * NA_HEAD_DIM)
            q = q_ref[:, sl]
            s_loc = (lax.dot_general(q, k_ref[pl.ds(st, NA_KEYS), sl], NT, preferred_element_type=F32) * scale
                     + _na_bias_tile(b_ref, hh, off))
            s_ctx = lax.dot_general(q, kc_ref[:, sl], NT, preferred_element_type=F32) * scale
            m = jnp.maximum(jnp.max(s_loc, axis=-1, keepdims=True), jnp.max(s_ctx, axis=-1, keepdims=True))
            p_loc, p_ctx = jnp.exp(s_loc - m), jnp.exp(s_ctx - m)
            l = jnp.sum(p_loc, axis=-1, keepdims=True) + jnp.sum(p_ctx, axis=-1, keepdims=True)
            o = (jnp.dot(p_loc.astype(BF16), v_ref[pl.ds(st, NA_KEYS), sl], preferred_element_type=F32)
                 + jnp.dot(p_ctx.astype(BF16), vc_ref[:, sl], preferred_element_type=F32))
            o_ref[:, sl] = (o / l).astype(BF16)
            lse_ref[hh] = m + jnp.log(l)

    outs, couts = _call(
        body, [qkv, qkv, qkv, kvc, kvc, bias], name=name, grid=(G, rows),
        in_specs=[pl.BlockSpec((GRID_W, wd), lambda h, r: (r, h)),
                  pl.BlockSpec((L, wd), lambda h, r: (0, G + h)),
                  pl.BlockSpec((L, wd), lambda h, r: (0, 2 * G + h)),
                  pl.BlockSpec((Lc, wd), lambda h, r: (0, h)),
                  pl.BlockSpec((Lc, wd), lambda h, r: (0, G + h)),
                  pl.BlockSpec((hp,) + bias.shape[1:], lambda h, r: (h, 0, 0, 0))],
        out_specs=[pl.BlockSpec((GRID_W, wd), lambda h, r: (r, h)),
                   pl.BlockSpec((hp, GRID_W, 1), lambda h, r: (h, r, 0))],
        out_shape=[jax.ShapeDtypeStruct((L, NA), BF16), jax.ShapeDtypeStruct((H, L, 1), F32)],
        sem=("parallel", "arbitrary"), comm=comm)
    return outs if comm is None else (outs, couts)


def natten_bwd(name, qkv, kvc, bias, o, lse, dmix, comm=None):
    L, NA, H, hp, G, rows, Lc = _na_dims(qkv, kvc)
    scale = NA_HEAD_DIM ** -0.5
    wd = hp * NA_HEAD_DIM

    def body(q_ref, k_ref, v_ref, kc_ref, vc_ref, b_ref, o_ref, lse_ref, do_ref,
             dq_ref, dk_ref, dv_ref, dkc_ref, dvc_ref, db_ref):
        r = pl.program_id(1)
        st, off = _na_window(r, rows)

        @pl.when(r == 0)
        def _():
            for ref in (dk_ref, dv_ref, dkc_ref, dvc_ref, db_ref):
                ref[...] = jnp.zeros_like(ref)

        for hh in range(hp):
            sl = slice(hh * NA_HEAD_DIM, (hh + 1) * NA_HEAD_DIM)
            q, kl, vl, kc, vc = q_ref[:, sl], k_ref[pl.ds(st, NA_KEYS), sl], v_ref[pl.ds(st, NA_KEYS), sl], kc_ref[:, sl], vc_ref[:, sl]
            do = do_ref[:, sl]
            lse_v = lse_ref[hh]
            p_loc = jnp.exp(lax.dot_general(q, kl, NT, preferred_element_type=F32) * scale + _na_bias_tile(b_ref, hh, off) - lse_v)
            p_ctx = jnp.exp(lax.dot_general(q, kc, NT, preferred_element_type=F32) * scale - lse_v)
            delta = jnp.sum(do.astype(F32) * o_ref[:, sl].astype(F32), axis=-1, keepdims=True)
            ds_loc = p_loc * (lax.dot_general(do, vl, NT, preferred_element_type=F32) - delta)
            ds_ctx = p_ctx * (lax.dot_general(do, vc, NT, preferred_element_type=F32) - delta)
            dsl, dsc = ds_loc.astype(BF16), ds_ctx.astype(BF16)
            dq = jnp.dot(dsl, kl, preferred_element_type=F32) + jnp.dot(dsc, kc, preferred_element_type=F32)
            dq_ref[:, sl] = (dq * scale).astype(BF16)
            dk_ref[pl.ds(st, NA_KEYS), sl] += lax.dot_general(dsl, q, TN, preferred_element_type=F32) * scale
            dv_ref[pl.ds(st, NA_KEYS), sl] += lax.dot_general(p_loc.astype(BF16), do, TN, preferred_element_type=F32)
            dkc_ref[:, sl] += lax.dot_general(dsc, q, TN, preferred_element_type=F32) * scale
            dvc_ref[:, sl] += lax.dot_general(p_ctx.astype(BF16), do, TN, preferred_element_type=F32)
            for j in range(NA_PAIRS):
                db_ref[hh, NA_WIN_R - 1 - off + 2 * j] += ds_loc[:, 2 * j * GRID_W:(2 * j + 2) * GRID_W]

    tok = pl.BlockSpec((GRID_W, wd), lambda h, r: (r, h))
    bia = pl.BlockSpec((hp,) + bias.shape[1:], lambda h, r: (h, 0, 0, 0))
    outs, couts = _call(
        body, [qkv, qkv, qkv, kvc, kvc, bias, o, lse, dmix], name=name, grid=(G, rows),
        in_specs=[tok,
                  pl.BlockSpec((L, wd), lambda h, r: (0, G + h)),
                  pl.BlockSpec((L, wd), lambda h, r: (0, 2 * G + h)),
                  pl.BlockSpec((Lc, wd), lambda h, r: (0, h)),
                  pl.BlockSpec((Lc, wd), lambda h, r: (0, G + h)),
                  bia,
                  tok,
                  pl.BlockSpec((hp, GRID_W, 1), lambda h, r: (h, r, 0)),
                  pl.BlockSpec((GRID_W, wd), lambda h, r: (r, G + h))],
        out_specs=[tok,
                   pl.BlockSpec((L, wd), lambda h, r: (0, h)),
                   pl.BlockSpec((L, wd), lambda h, r: (0, h)),
                   pl.BlockSpec((Lc, wd), lambda h, r: (0, h)),
                   pl.BlockSpec((Lc, wd), lambda h, r: (0, h)),
                   bia],
        out_shape=[jax.ShapeDtypeStruct((L, NA), BF16), jax.ShapeDtypeStruct((L, NA), F32), jax.ShapeDtypeStruct((L, NA), F32),
                   jax.ShapeDtypeStruct((Lc, NA), F32), jax.ShapeDtypeStruct((Lc, NA), F32),
                   jax.ShapeDtypeStruct(bias.shape, F32)],
        sem=("parallel", "arbitrary"), comm=comm)
    return outs if comm is None else (outs, couts)


def _s5_dims(T, N):
    TC = T // S5_SEG
    assert T % (S5_SEG * SUB * 2) == 0 and N % S5_STRIP == 0
    return TC, TC // SUB, S5_SEG, N // S5_STRIP


def _s5_backward(d, rev):
    return (d == 1) != rev


def s5_scan(name, xin, mats, a, rev, comm=None):
    _, T, W = xin.shape
    N = a.shape[-1]
    TC, NG, NCH, NS = _s5_dims(T, N)
    CW, SL = W // NS, S5_STRIP

    def ck(d, k):
        return jnp.where(_s5_backward(d, rev), NCH - 1 - k, k)

    def body(x_ref, m_ref, a_ref, h_ref, f_ref, carry, hs):
        @pl.when(pl.program_id(2) == 0)
        def _():
            carry[...] = jnp.zeros_like(carry)

        xb = x_ref[...].astype(BF16)
        hs[0] = jnp.dot(xb, m_ref[0], preferred_element_type=F32)
        hs[1] = jnp.dot(xb, m_ref[1], preferred_element_type=F32)
        ar, ai = jnp.broadcast_to(a_ref[0], (SUB, SL)), jnp.broadcast_to(a_ref[1], (SUB, SL))
        bw = _s5_backward(pl.program_id(0), rev)

        def step(t, c):
            hr, hi = c
            row = pl.multiple_of(jnp.where(bw, NG - 1 - t, t) * SUB, SUB)
            nr = ar * hr - ai * hi + hs[0, pl.ds(row, SUB), :]
            ni = ar * hi + ai * hr + hs[1, pl.ds(row, SUB), :]
            hs[0, pl.ds(row, SUB), :] = nr
            hs[1, pl.ds(row, SUB), :] = ni
            return nr, ni

        hr, hi = lax.fori_loop(0, NG, step, (carry[0], carry[1]))
        carry[0], carry[1] = hr, hi
        f_ref[0], f_ref[1] = hr, hi
        h_ref[...] = hs[...].astype(BF16)

    outs, couts = _call(
        body, [xin, mats, a], name=name, grid=(2, NS, NCH),
        in_specs=[pl.BlockSpec((None, TC, CW), lambda d, j, k: (d, ck(d, k), j)),
                  pl.BlockSpec((None, 2, None, CW, SL), lambda d, j, k: (d, 0, j, 0, 0)),
                  pl.BlockSpec((None, 2, 1, SL), lambda d, j, k: (d, 0, 0, j))],
        out_specs=[pl.BlockSpec((None, 2, TC, SL), lambda d, j, k: (d, 0, ck(d, k), j)),
                   pl.BlockSpec((None, 2, SUB, SL), lambda d, j, k: (d, 0, 0, j))],
        out_shape=[jax.ShapeDtypeStruct((2, 2, T, N), BF16), jax.ShapeDtypeStruct((2, 2, SUB, N), F32)],
        scratch_shapes=[pltpu.VMEM((2, SUB, SL), F32), pltpu.VMEM((2, TC, SL), F32)],
        sem=("parallel", "parallel", "arbitrary"), comm=comm)
    return outs if comm is None else (outs, couts)


def s5_fix(name, hloc, hin, a, mats, rev, comm=None):
    _, _, T, N = hloc.shape
    TC, NG, NCH, NS = _s5_dims(T, N)
    SL = S5_STRIP
    CW = mats.shape[-1]

    def ck(d, k):
        return jnp.where(_s5_backward(d, rev), NCH - 1 - k, k)

    def body(h_ref, hin_ref, a_ref, m_ref, ho_ref, y_ref, g, hs):
        @pl.when(pl.program_id(2) == 0)
        def _():
            g[...] = hin_ref[...]

        hs[...] = h_ref[...].astype(F32)
        ar, ai = jnp.broadcast_to(a_ref[0], (SUB, SL)), jnp.broadcast_to(a_ref[1], (SUB, SL))
        bw = _s5_backward(pl.program_id(0), rev)

        def step(t, c):
            gr, gi = c
            row = pl.multiple_of(jnp.where(bw, NG - 1 - t, t) * SUB, SUB)
            nr = ar * gr - ai * gi
            ni = ar * gi + ai * gr
            hs[0, pl.ds(row, SUB), :] += nr
            hs[1, pl.ds(row, SUB), :] += ni
            return nr, ni

        gr, gi = lax.fori_loop(0, NG, step, (g[0], g[1]))
        g[0], g[1] = gr, gi
        hb = hs[...].astype(BF16)
        ho_ref[...] = hb
        y_ref[...] = (jnp.dot(hb[0], m_ref[0], preferred_element_type=F32)
                      + jnp.dot(hb[1], m_ref[1], preferred_element_type=F32))

    outs, couts = _call(
        body, [hloc, hin, a, mats], name=name, grid=(2, NS, NCH),
        in_specs=[pl.BlockSpec((None, 2, TC, SL), lambda d, j, k: (d, 0, ck(d, k), j)),
                  pl.BlockSpec((None, 2, SUB, SL), lambda d, j, k: (d, 0, 0, j)),
                  pl.BlockSpec((None, 2, 1, SL), lambda d, j, k: (d, 0, 0, j)),
                  pl.BlockSpec((None, 2, None, SL, CW), lambda d, j, k: (d, 0, j, 0, 0))],
        out_specs=[pl.BlockSpec((None, 2, TC, SL), lambda d, j, k: (d, 0, ck(d, k), j)),
                   pl.BlockSpec((None, TC, CW), lambda d, j, k: (d, ck(d, k), j))],
        out_shape=[jax.ShapeDtypeStruct((2, 2, T, N), BF16), jax.ShapeDtypeStruct((2, T, NS * CW), F32)],
        scratch_shapes=[pltpu.VMEM((2, SUB, SL), F32), pltpu.VMEM((2, TC, SL), F32)],
        sem=("parallel", "parallel", "arbitrary"), comm=comm)
    return outs if comm is None else (outs, couts)


def s5_grads(name, g, h, u, dy, comm=None):
    _, _, T, N = g.shape
    W = u.shape[-1]
    TC, NG, NCH, NS = _s5_dims(T, N)
    CW, SL = W // NS, S5_STRIP

    def body(g_ref, h_ref, hp_ref, hl_ref, u_ref, dy_ref, dm_ref, dc_ref, da_ref, hs):
        k = pl.program_id(2)
        sub = lax.broadcasted_iota(jnp.int32, (SUB, SL), 0)

        hf = h_ref[...].astype(F32)

        @pl.when(pl.program_id(0) == 0)
        def _():
            for z in range(2):
                wrapped = jnp.where(sub == 0, 0.0, pltpu.roll(hl_ref[z].astype(F32)[SUB:], 1, 0))
                hs[z, 0:SUB, :] = jnp.where(k == 0, wrapped, hp_ref[z].astype(F32)[SUB:])
                hs[z, SUB:TC, :] = hf[z, 0:TC - SUB]

        @pl.when(pl.program_id(0) == 1)
        def _():
            for z in range(2):
                wrapped = jnp.where(sub == SUB - 1, 0.0, pltpu.roll(hl_ref[z].astype(F32)[:SUB], SUB - 1, 0))
                hs[z, TC - SUB:TC, :] = jnp.where(k == NCH - 1, wrapped, hp_ref[z].astype(F32)[:SUB])
                hs[z, 0:TC - SUB, :] = hf[z, SUB:TC]

        gr, gi, pr, pi = g_ref[0].astype(F32), g_ref[1].astype(F32), hs[0], hs[1]
        dar = jnp.sum((gr * pr + gi * pi).reshape(NG, SUB, SL), axis=0)
        dai = jnp.sum((gi * pr - gr * pi).reshape(NG, SUB, SL), axis=0)
        ub, dyb = u_ref[...].astype(BF16), dy_ref[...].astype(BF16)
        dm = [lax.dot_general(ub, g_ref[z], TN, preferred_element_type=F32) for z in range(2)]
        dc = [lax.dot_general(dyb, h_ref[z], TN, preferred_element_type=F32) for z in range(2)]

        @pl.when(k == 0)
        def _():
            da_ref[0], da_ref[1] = dar, dai
            for z in range(2):
                dm_ref[z], dc_ref[z] = dm[z], dc[z]

        @pl.when(k > 0)
        def _():
            da_ref[0] += dar
            da_ref[1] += dai
            for z in range(2):
                dm_ref[z] += dm[z]
                dc_ref[z] += dc[z]

    big = pl.BlockSpec((None, 2, TC, SL), lambda d, j, k: (d, 0, k, j))
    tok = pl.BlockSpec((None, TC, CW), lambda d, j, k: (d, k, j))
    mat = pl.BlockSpec((None, 2, None, CW, SL), lambda d, j, k: (d, 0, j, 0, 0))
    outs, couts = _call(
        body, [g, h, h, h, u, dy], name=name, grid=(2, NS, NCH),
        in_specs=[big, big,
                  pl.BlockSpec((None, 2, 2 * SUB, SL), lambda d, j, k: (
                      d, 0, jnp.where(d == 0, jnp.maximum(k * NG - 1, 0), jnp.minimum((k + 1) * NG, T // SUB - 1)) // 2, j)),
                  pl.BlockSpec((None, 2, 2 * SUB, SL), lambda d, j, k: (d, 0, jnp.where(d == 0, T // SUB - 1, 0) // 2, j)),
                  tok, tok],
        out_specs=[mat, mat, pl.BlockSpec((None, 2, SUB, SL), lambda d, j, k: (d, 0, 0, j))],
        out_shape=[jax.ShapeDtypeStruct((2, 2, NS, CW, SL), F32), jax.ShapeDtypeStruct((2, 2, NS, CW, SL), F32),
                   jax.ShapeDtypeStruct((2, 2, SUB, N), F32)],
        scratch_shapes=[pltpu.VMEM((2, TC, SL), F32)],
        sem=("parallel", "parallel", "arbitrary"), comm=comm)
    return outs if comm is None else (outs, couts)


def _interleave(seq):
    *lead, T, W = seq.shape
    n = len(lead)
    return seq.reshape(*lead, S5_SEG, T // S5_SEG, W).swapaxes(n, n + 1).reshape(*lead, T, W)


def _deinterleave(seq):
    *lead, T, W = seq.shape
    n = len(lead)
    return seq.reshape(*lead, T // S5_SEG, S5_SEG, W).swapaxes(n, n + 1).reshape(*lead, T, W)


def _s5_discretize(lam_re, lam_im, log_dt, b_re, b_im):
    dt = jnp.exp(log_dt)[..., None]
    mag = jnp.exp(lam_re * dt)
    a_re = mag * jnp.cos(lam_im * dt)
    a_im = mag * jnp.sin(lam_im * dt)
    den = jnp.square(lam_re) + jnp.square(lam_im)
    f_re = ((a_re - 1.0) * lam_re + a_im * lam_im) / den
    f_im = (a_im * lam_re - (a_re - 1.0) * lam_im) / den
    bb_re = f_re[..., None] * b_re - f_im[..., None] * b_im
    bb_im = f_re[..., None] * b_im + f_im[..., None] * b_re
    return a_re, a_im, bb_re, bb_im


_GPS = S5_STRIP // SSM_STATE


def _blockdiag(t):
    d2, G, P, Cg = t.shape
    t5 = t.reshape(d2, G // _GPS, _GPS, P, Cg).transpose(0, 1, 2, 4, 3)
    m = t5[:, :, :, :, None, :] * jnp.eye(_GPS, dtype=t.dtype)[None, None, :, None, :, None]
    return m.reshape(d2, G // _GPS, _GPS * Cg, _GPS * P)


def _blockdiag_extract(m, Cg, P):
    d2, NS = m.shape[:2]
    m6 = m.reshape(d2, NS, _GPS, Cg, _GPS, P)
    diag = jnp.stack([m6[:, :, i, :, i, :] for i in range(_GPS)], axis=2)
    return diag.transpose(0, 1, 2, 4, 3).reshape(d2, NS * _GPS, P, Cg)


def _cmul(a, b):
    return a[0] * b[0] - a[1] * b[1], a[0] * b[1] + a[1] * b[0]


def _cpow(a, n):
    out, base = None, a
    while n:
        if n & 1:
            out = base if out is None else _cmul(out, base)
        base = _cmul(base, base)
        n >>= 1
    return out


def _segment_carry(fin, apow, rev):
    per_dir = []
    for d in range(2):
        fr, fi = fin[d, 0], fin[d, 1]
        ap = (apow[0][d], apow[1][d])
        cr = ci = jnp.zeros_like(fr[0:1])
        outs = [None] * S5_SEG
        backward = (d == 1) != rev
        for s in (range(S5_SEG - 1, -1, -1) if backward else range(S5_SEG)):
            outs[s] = (cr, ci)
            pr, pi = _cmul(ap, (cr, ci))
            cr, ci = pr + fr[s:s + 1], pi + fi[s:s + 1]
        per_dir.append(jnp.stack([jnp.concatenate([o[0] for o in outs]), jnp.concatenate([o[1] for o in outs])]))
    return jnp.stack(per_dir)


def _coords():
    x, y, c = lax.axis_index("x"), lax.axis_index("y"), lax.axis_index("c")
    others = [(1 - x, y), (x, 1 - y), (1 - x, 1 - y)]
    return x, y, c, 2 * x + y, others


def _comm(name, ins, out_shapes, aliases, n_local, n_remote, plan):
    n_in, n_out = len(ins), len(out_shapes)

    def body(*refs):
        in_refs, out_refs = refs[:n_in], refs[n_in:n_in + n_out]
        send_sems, recv_sems, local_sems = refs[n_in + n_out:]
        x, y, c = lax.axis_index("x"), lax.axis_index("y"), lax.axis_index("c")
        locs, sends, lands = plan(in_refs, out_refs)
        assert len(locs) == n_local and len(sends) == n_remote and len(lands) == n_remote
        local = [pltpu.make_async_copy(s, d, local_sems.at[i]) for i, (s, d) in enumerate(locs)]
        for cp in local:
            cp.start()
        remote = [pltpu.make_async_remote_copy(src_ref=s, dst_ref=d, send_sem=send_sems.at[i], recv_sem=recv_sems.at[i],
                                               device_id=peer, device_id_type=MESH)
                  for i, (s, d, peer) in enumerate(sends)]
        for cp in remote:
            cp.start()
        for i, d in enumerate(lands):
            pltpu.make_async_remote_copy(src_ref=d, dst_ref=d, send_sem=send_sems.at[i], recv_sem=recv_sems.at[i],
                                         device_id=(x, y, c), device_id_type=MESH).wait_recv()
        for cp in remote:
            cp.wait_send()
        for cp in local:
            cp.wait()

    any_spec = pl.BlockSpec(memory_space=pl.ANY)
    return pl.pallas_call(
        body, name=name,
        in_specs=[any_spec] * n_in, out_specs=[any_spec] * n_out,
        out_shape=[jax.ShapeDtypeStruct(s, d) for s, d in out_shapes],
        input_output_aliases=aliases,
        scratch_shapes=[pltpu.SemaphoreType.DMA((n_remote,)), pltpu.SemaphoreType.DMA((n_remote,)),
                        pltpu.SemaphoreType.DMA((max(n_local, 1),))],
        compiler_params=pltpu.CompilerParams(has_side_effects=True),
    )(*ins)


def allgather_dev(name, v):
    M, Nc = v.shape

    def plan(in_refs, out_refs):
        (v_ref,), (o_ref,) = in_refs, out_refs
        x, y, c = lax.axis_index("x"), lax.axis_index("y"), lax.axis_index("c")

        def rows(px, py, pc):
            return o_ref.at[pl.ds((4 * px + 2 * py + pc) * M, M), :]

        peers = [(x ^ fx, y ^ fy, c ^ fc) for fx in (0, 1) for fy in (0, 1) for fc in (0, 1) if fx or fy or fc]
        return ([(v_ref, rows(x, y, c))],
                [(v_ref, rows(x, y, c), p) for p in peers],
                [rows(*p) for p in peers])

    return _comm(name, [v], [((N_DEV * M, Nc), v.dtype)], {}, 1, N_DEV - 1, plan)[0]


def allgather_chips_1(name, shards):
    def plan(in_refs, out_refs):
        x, y, c, chip, others = _coords()
        sends, lands = [], []
        for s_ref, g_ref in zip(in_refs, out_refs):
            hr = s_ref.shape[0] // 2
            mine = pl.ds(c * hr, hr)
            for qx, qy in others:
                sends.append((s_ref.at[mine], g_ref.at[chip, mine], (qx, qy, c)))
                lands.append(g_ref.at[2 * qx + qy, mine])
        return [], sends, lands

    n = len(shards)
    comm = (list(shards), [((N_CHIP,) + s.shape, s.dtype) for s in shards], {}, 3 * n, plan)
    return comm if name is None else _comm(name, comm[0], comm[1], comm[2], 0, comm[3], comm[4])


def allgather_chips_2(name, gathered, shards):
    n = len(gathered)

    def plan(in_refs, out_refs):
        x, y, c, chip, others = _coords()
        sends, lands = [], []
        for s_ref, g_ref in zip(in_refs[n:], out_refs):
            hr = g_ref.shape[1] // 2
            for qx, qy in others:
                q = 2 * qx + qy
                sends.append((g_ref.at[q, pl.ds(c * hr, hr)], g_ref.at[q, pl.ds(c * hr, hr)], (x, y, 1 - c)))
                lands.append(g_ref.at[q, pl.ds((1 - c) * hr, hr)])
            sends.append((s_ref, g_ref.at[chip], (x, y, 1 - c)))
            lands.append(g_ref.at[chip])
        return [], sends, lands

    comm = (list(gathered) + list(shards), [(g.shape, g.dtype) for g in gathered], {i: i for i in range(n)}, 4 * n, plan)
    return comm if name is None else _comm(name, comm[0], comm[1], comm[2], 0, comm[3], comm[4])


def reduce_1(name, grads):
    def plan(in_refs, out_refs):
        x, y, c, chip, others = _coords()
        sends, lands = [], []
        for g_ref, got_ref in zip(in_refs, out_refs):
            hr = g_ref.shape[1] // 2
            sends.append((g_ref.at[:, pl.ds((1 - c) * hr, hr), :], got_ref, (x, y, 1 - c)))
            lands.append(got_ref)
        return [], sends, lands

    n = len(grads)
    comm = (list(grads), [((g.shape[0], g.shape[1] // 2, g.shape[2]), g.dtype) for g in grads], {}, n, plan)
    return comm if name is None else _comm(name, comm[0], comm[1], comm[2], 0, comm[3], comm[4])


def _merge_comm(a, b):
    if a is None or b is None:
        return a if b is None else b
    na_in, na_out = len(a[0]), len(a[1])

    def plan(in_refs, out_refs):
        _, s1, l1 = a[4](in_refs[:na_in], out_refs[:na_out])
        _, s2, l2 = b[4](in_refs[na_in:], out_refs[na_out:])
        return [], s1 + s2, l1 + l2

    alias = dict(a[2])
    alias.update({na_in + i: na_out + j for i, j in b[2].items()})
    return (a[0] + b[0], a[1] + b[1], alias, a[3] + b[3], plan)


def reduce_2(name, parts):
    def plan(in_refs, out_refs):
        x, y, c, chip, others = _coords()
        sends, lands = [], []
        for t_ref, q_ref in zip(in_refs, out_refs):
            for qx, qy in others:
                sends.append((t_ref.at[2 * qx + qy], q_ref.at[chip], (qx, qy, c)))
                lands.append(q_ref.at[2 * qx + qy])
        return [], sends, lands

    n = len(parts)
    comm = (list(parts), [(p.shape, p.dtype) for p in parts], {}, 3 * n, plan)
    return comm if name is None else _comm(name, comm[0], comm[1], comm[2], 0, comm[3], comm[4])


def share_slots(name, slots):
    def plan(in_refs, out_refs):
        x, y, c, chip, others = _coords()
        (q_ref,) = out_refs
        return ([], [(q_ref.at[chip], q_ref.at[chip], (qx, qy, c)) for qx, qy in others],
                [q_ref.at[2 * qx + qy] for qx, qy in others])

    return _comm(name, [slots], [(slots.shape, slots.dtype)], {0: 0}, 0, N_CHIP - 1, plan)[0]


def allreduce_small(tag, buf, ids):
    got = reduce_1(tag + "_1", [buf[None]])[0][0]
    slots = share_slots(tag + "_2", pair_sum_to_slot(tag + "_add", buf, got, ids))
    full = reduce_3(tag + "_3", [sum_chips_to_half(tag + "_sum", slots, ids)])[0]
    return full.reshape(buf.shape)


def reduce_3(name, fulls):
    def plan(in_refs, out_refs):
        x, y, c, chip, others = _coords()
        sends, lands = [], []
        for o_ref in out_refs:
            sends.append((o_ref.at[c], o_ref.at[c], (x, y, 1 - c)))
            lands.append(o_ref.at[1 - c])
        return [], sends, lands

    n = len(fulls)
    return _comm(name, fulls, [(f.shape, f.dtype) for f in fulls], {i: i for i in range(n)}, 0, n, plan)


_WEIGHTS = ['c_ctx', 'w_mod', 'b_mod', 'g_mix', 'g_ffn', 'w_in', 'ssm_lam_re', 'ssm_lam_im', 'ssm_log_dt', 'ssm_b_re',
            'ssm_b_im', 'ssm_c_re', 'ssm_c_im', 'ssm_d', 'ssm_w_glu', 'na_rpb', 'w_out', 'cv_w_pw1', 'cv_dw_w', 'cv_dw_b',
            'cv_ln_g', 'cv_ln_b', 'cv_w_pw2', 'ffn_w_up', 'ffn_conv_w', 'ffn_conv_b', 'ffn_w_down', 'g_out']
_INPUTS = ['x', 'c', 'ctx'] + _WEIGHTS + ['loss_target'] + ['m_' + w for w in _WEIGHTS] + ['v_' + w for w in _WEIGHTS]
_GATHERED_SMALL = ['ffn_conv_w', 'cv_dw_w', 'cv_dw_b', 'cv_ln_g', 'cv_ln_b']


def _silu(v):
    return v * jax.nn.sigmoid(v)


def _pick_index(t, idx, axis):
    shape = [1] * t.ndim
    shape[axis] = t.shape[axis]
    mask = (jnp.arange(t.shape[axis]) == idx).reshape(shape)
    return jnp.sum(jnp.where(mask, t, jnp.zeros((), t.dtype)), axis=axis)


def _pack(arrs, cols, row_mult=SUB):
    flat = jnp.concatenate([a.reshape(-1).astype(F32) for a in arrs])
    n = flat.shape[0]
    unit = row_mult * cols
    flat = jnp.pad(flat, (0, (-n) % unit))
    return flat.reshape(-1, cols)


def _unpack(buf, shapes):
    flat = buf.reshape(-1)
    out, o = [], 0
    for s in shapes:
        n = int(np.prod(s))
        out.append(flat[o:o + n].reshape(s))
        o += n
    return out


def _carried(res, comm):
    return res if comm is not None else (res, [])


def _ffn_fwd(tag, xin, sh, sc, gt, g, wup, cw3, cb3, wdn, comm_up=None, comm_mid=None, comm_down=None):
    hf = norm_mod_fwd(tag + "_norm", xin, g * (1.0 + sc), sh)
    up3, got_up = _carried(mm_nn_pieces(tag + "_up", hf, wup, 0, N_CHIP, BF16, halves=2, comm=comm_up), comm_up)
    comm_mid = comm_mid(got_up) if callable(comm_mid) else comm_mid
    act, got_mid = _carried(ffn_mid_fwd(tag + "_mid", up3, cw3, cb3, comm=comm_mid), comm_mid)
    comm_down = comm_down(got_mid) if callable(comm_down) else comm_down
    yf, got_down = _carried(mm_nn(tag + "_down", act, wdn, BF16, comm=comm_down), comm_down)
    return gate_res_fwd(tag + "_res", xin, yf, gt), (xin, hf, up3, act, yf), got_up, got_mid, got_down


def _ffn_bwd(tag, dxo, saved, sc, gt, g, wup, cw3, cb3, wdn, comm_down=None, comm_mid=None):
    xin, hf, up3, act, yf = saved
    dyf, dgt = gate_res_bwd(tag + "_res_b", dxo, yf, gt)
    dact, got_down = _carried(mm_nt(tag + "_down_bx", dyf, wdn, BF16, comm=comm_down), comm_down)
    dwdn = mm_tn(tag + "_down_bw", act, dyf, BF16)
    comm_mid = comm_mid(got_down) if callable(comm_mid) else comm_mid
    (dup3, dcw3, dcb3), got_mid = _carried(ffn_mid_bwd(tag + "_mid_b", up3, dact, cw3, cb3, comm=comm_mid), comm_mid)
    dhf = mm_nt_pieces(tag + "_up_bx", dup3, wup, BF16, halves=2)
    dwup = mm_tn_pieces(tag + "_up_bw", hf, dup3, N_CHIP, BF16, halves=2)
    dxi, cs1, cs2 = norm_mod_bwd(tag + "_norm_b", xin, dhf, g * (1.0 + sc), dxo)
    return dxi, dict(dsh=cs1[0], dsc=cs2[0] * g, dgt=dgt[0], dg=cs2[0] * (1.0 + sc), dwup=dwup, dwdn=dwdn,
                     dcw=dcw3.transpose(1, 0, 2).reshape(3, -1), dcb=dcb3.reshape(-1)), got_down, got_mid


def kernel(x, c, ctx, c_ctx, w_mod, b_mod, g_mix, g_ffn, w_in, ssm_lam_re, ssm_lam_im, ssm_log_dt, ssm_b_re, ssm_b_im, ssm_c_re, ssm_c_im, ssm_d, ssm_w_glu, na_rpb, w_out, cv_w_pw1, cv_dw_w, cv_dw_b, cv_ln_g, cv_ln_b, cv_w_pw2, ffn_w_up, ffn_conv_w, ffn_conv_b, ffn_w_down, g_out, loss_target, m_c_ctx, m_w_mod, m_b_mod, m_g_mix, m_g_ffn, m_w_in, m_ssm_lam_re, m_ssm_lam_im, m_ssm_log_dt, m_ssm_b_re, m_ssm_b_im, m_ssm_c_re, m_ssm_c_im, m_ssm_d, m_ssm_w_glu, m_na_rpb, m_w_out, m_cv_w_pw1, m_cv_dw_w, m_cv_dw_b, m_cv_ln_g, m_cv_ln_b, m_cv_w_pw2, m_ffn_w_up, m_ffn_conv_w, m_ffn_conv_b, m_ffn_w_down, m_g_out, v_c_ctx, v_w_mod, v_b_mod, v_g_mix, v_g_ffn, v_w_in, v_ssm_lam_re, v_ssm_lam_im, v_ssm_log_dt, v_ssm_b_re, v_ssm_b_im, v_ssm_c_re, v_ssm_c_im, v_ssm_d, v_ssm_w_glu, v_na_rpb, v_w_out, v_cv_w_pw1, v_cv_dw_w, v_cv_dw_b, v_cv_ln_g, v_cv_ln_b, v_cv_w_pw2, v_ffn_w_up, v_ffn_conv_w, v_ffn_conv_b, v_ffn_w_down, v_g_out):
    p = dict(locals())
    xi, yi, ci = lax.axis_index("x"), lax.axis_index("y"), lax.axis_index("c")
    me, chip = 4 * xi + 2 * yi + ci, 2 * xi + yi
    xs, cx, tgt = x[0], ctx[0], loss_target[0]
    L, D = xs.shape
    Lc = cx.shape[0]
    T = L + Lc
    W = D // 2
    Cq = w_mod.shape[2]

    s_mix = [t.astype(BF16) for t in (w_in[0], ssm_w_glu[0], w_out[0])]
    s_ffn0 = [t.astype(BF16) for t in (ffn_w_up[0], ffn_w_down[0])]
    s_conv = [t.astype(BF16) for t in (cv_w_pw1[0], cv_w_pw2[0])]
    s_ffn1 = [t.astype(BF16) for t in (ffn_w_up[1], ffn_w_down[1])]
    (Win,) = allgather_chips_2("gather_win_2", allgather_chips_1("gather_win_1", s_mix[:1]), s_mix[:1])
    Fd = ffn_w_down.shape[1] * N_CHIP
    c_idx = jnp.reshape(ci, (1,)).astype(jnp.int32)
    ids = jnp.stack([chip, ci]).astype(jnp.int32)

    def added(tag, grads, got):
        return [add_half("reduce_%s_add%d" % (tag, i), g, r, c_idx) for i, (g, r) in enumerate(zip(grads, got))]

    small_shapes = [p[n].shape for n in _GATHERED_SMALL]
    sm = allgather_dev("gather_small", _pack([p[n] for n in _GATHERED_SMALL], 1024))
    sm = sm.reshape(N_DEV, -1)[0::2]
    per_chip = [_unpack(sm[q], small_shapes) for q in range(N_CHIP)]
    conv_w_f, dw_w_f, dw_b_f, ln_g_f, ln_b_f = (jnp.concatenate([pc[i] for pc in per_chip], axis=-1)
                                                for i in range(len(_GATHERED_SMALL)))
    cw3 = [conv_w_f[l].reshape(3, 2, Fd).transpose(1, 0, 2) for l in range(2)]
    cb3 = [ffn_conv_b[l].reshape(2, 1, Fd) for l in range(2)]
    dw_w_f, dw_b_f, ln_g_f, ln_b_f = dw_w_f[0], dw_b_f[0], ln_g_f[0], ln_b_f[0]

    c_all = allgather_dev("gather_c", jnp.zeros((SUB, D), F32).at[0].set(c[0])).reshape(N_DEV, SUB, D)[:, 0]
    S16 = jnp.concatenate([_silu(c_all), _silu(c_ctx)[None], jnp.zeros((2 * SUB - N_DEV - 1, D), F32)])
    modp = mm_nn_pieces("mod_fwd", S16, w_mod, 0, 2, F32)
    modg = allgather_dev("gather_mod", modp).reshape(N_DEV, 2 * SUB, 2, Cq)[0::2]

    def mod_row(r):
        return r.transpose(1, 0, 2).reshape(2, N_CHIP * Cq) + b_mod

    mod_me = mod_row(_pick_index(modg, me, 1))
    mod_c = mod_row(modg[:, N_DEV])
    mods = [[mod_me[l, i * D:(i + 1) * D] for i in range(N_MOD)] for l in range(2)]
    shc, scc = mod_c[0, :D], mod_c[0, D:2 * D]

    sh_m, sc_m, gt_m, sh_f, sc_f, gt_f = mods[0]
    h0 = norm_mod_fwd("l0_norm", xs, g_mix[0] * (1.0 + sc_m), sh_m)
    hc0 = norm_mod_fwd("l0_norm_c", cx, g_mix[0] * (1.0 + scc), shc)
    u = mm_nn_pieces("l0_in_u", h0, Win, 0, 1, F32)
    qkv, g_mix1 = mm_nn_pieces("l0_in_qkv", h0, Win, 1, 3, BF16, comm=allgather_chips_1(None, s_mix[1:]))
    uc = mm_nn_pieces("l0_in_uc", hc0, Win, 0, 1, F32)
    kvc = mm_nn_pieces("l0_in_kvc", hc0, Win, 2, 2, BF16)

    lam_re, lam_im, log_dt = ssm_lam_re[0], ssm_lam_im[0], ssm_log_dt[0]
    b_re, b_im, c_re, c_im = ssm_b_re[0], ssm_b_im[0], ssm_c_re[0], ssm_c_im[0]
    (a_re, a_im, bb_re, bb_im), disc_vjp = jax.vjp(_s5_discretize, lam_re, lam_im, log_dt, b_re, b_im)
    G, P, Cg = bb_re.shape[1:]
    N = G * P
    a_re, a_im = a_re.reshape(2, 1, N), a_im.reshape(2, 1, N)
    a_f, a_b = jnp.stack([a_re, a_im], axis=1), jnp.stack([a_re, -a_im], axis=1)
    Bblk = jnp.stack([_blockdiag(bb_re), _blockdiag(bb_im)], axis=1)
    Cblk = jnp.stack([_blockdiag(c_re.swapaxes(-1, -2)), -_blockdiag(c_im.swapaxes(-1, -2))], axis=1)
    apow = _cpow((a_re, a_im), T // S5_SEG)

    useq = _interleave(jnp.stack([jnp.concatenate([uc, u]), jnp.concatenate([u, uc])]).astype(BF16))
    (hloc, fin), (Wglu, Wout) = s5_scan("s5_scan", useq, Bblk.astype(BF16), a_f, rev=False,
                                        comm=allgather_chips_2(None, g_mix1, s_mix[1:]))
    Wglu, Wout = Wglu.reshape(-1, Wglu.shape[-1]), Wout.reshape(-1, D)
    (hst, yseq), g_dn0 = s5_fix("s5_fix", hloc, _segment_carry(fin, apow, False), a_f, Cblk.swapaxes(-1, -2).astype(BF16),
                                rev=False, comm=allgather_chips_1(None, s_ffn0[1:]))
    ys = _deinterleave(yseq)
    y0, y1 = ys[0, Lc:], ys[1, :L]
    s5o = glu_fwd("s5_glu", u, y0, y1, ssm_d[0], Wglu)

    bias = na_bias(na_rpb[0])
    (o_na, lse), (g_up0, Wdn0) = natten_fwd(
        "na_fwd", qkv, kvc, bias,
        comm=_merge_comm(allgather_chips_1(None, s_ffn0[:1]), allgather_chips_2(None, g_dn0, s_ffn0[1:])))
    mixcat = jnp.concatenate([s5o, o_na], axis=1)
    ymix, (Wup0,) = mm_nn("l0_out", mixcat, Wout, BF16, comm=allgather_chips_2(None, [g_up0], s_ffn0[:1]))
    x1 = gate_res_fwd("l0_res", xs, ymix, gt_m)
    Wdn0 = Wdn0.reshape(-1, D)
    x2, ffn0, (g_up1,), (g_dn1, Wup1), (g_pw1, g_pw2, Wdn1) = _ffn_fwd(
        "f0", x1, sh_f, sc_f, gt_f, g_ffn[0], Wup0, cw3[0], cb3[0], Wdn0,
        comm_up=allgather_chips_1(None, s_ffn1[:1]),
        comm_mid=lambda got_up: _merge_comm(allgather_chips_1(None, s_ffn1[1:]), allgather_chips_2(None, got_up, s_ffn1[:1])),
        comm_down=lambda got_mid: _merge_comm(allgather_chips_1(None, s_conv), allgather_chips_2(None, got_mid[:1], s_ffn1[1:])))
    Wpw1, Wpw2 = allgather_chips_2("gather_conv_2", [g_pw1, g_pw2], s_conv)
    Wpw2 = Wpw2.reshape(-1, D)
    Wup, Wdn = [Wup0, Wup1], [Wdn0.reshape(-1, D), Wdn1.reshape(-1, D)]

    sh_v, sc_v, gt_v, sh_g, sc_g, gt_g = mods[1]
    hcv = norm_mod_fwd("l1_norm", x2, g_mix[1] * (1.0 + sc_v), sh_v)
    ag3 = mm_nn_pieces("l1_pw1", hcv, Wpw1, 0, N_CHIP, BF16, halves=2)
    z1, z3 = conf_mid_fwd("l1_mid", ag3, dw_w_f, dw_b_f, ln_g_f, ln_b_f)
    ycv = mm_nn("l1_pw2", z3, Wpw2, BF16)
    x3 = gate_res_fwd("l1_res", x2, ycv, gt_v)
    x4, ffn1, _, _, _ = _ffn_fwd("f1", x3, sh_g, sc_g, gt_g, g_ffn[1], Wup[1], cw3[1], cb3[1], Wdn[1])

    dx4, dg_out, loss_part = loss_head("loss", x4, g_out, tgt)
    loss = lax.psum(loss_part[0, 0], ("x", "y", "c"))

    dx3, gf1, _, _ = _ffn_bwd("f1", dx4, ffn1, sc_g, gt_g, g_ffn[1], Wup[1], cw3[1], cb3[1], Wdn[1])
    g_up1, g_dn1 = [gf1["dwup"]], [gf1["dwdn"].reshape(N_CHIP, -1, D)]
    dycv, dgt_v = gate_res_bwd("l1_res_b", dx3, ycv, gt_v)
    dz3, got = mm_nt("l1_pw2_bx", dycv, Wpw2, BF16, comm=reduce_1(None, g_up1))
    parts_up1 = added("up1", g_up1, got)
    dWpw2, got = mm_tn("l1_pw2_bw", z3, dycv, BF16, comm=reduce_1(None, g_dn1))
    parts_dn1 = added("dn1", g_dn1, got)
    dz1, dln_g, dln_b = conf_ln_bwd("l1_ln_b", z1, dz3, ln_g_f, ln_b_f)
    (dag3, ddw_w, ddw_b), slots_up1 = conf_conv_bwd("l1_conv_b", ag3, dz1, dw_w_f, comm=reduce_2(None, parts_up1))
    dhcv, slots_dn1 = mm_nt_pieces("l1_pw1_bx", dag3, Wpw1, BF16, halves=2, comm=reduce_2(None, parts_dn1))
    dWpw1 = mm_tn_pieces("l1_pw1_bw", hcv, dag3, N_CHIP, BF16, halves=2)
    dx2, cs1_v, cs2_v = norm_mod_bwd("l1_norm_b", x2, dhcv, g_mix[1] * (1.0 + sc_v), dx3)
    g_conv = [dWpw1, dWpw2.reshape(N_CHIP, -1, D)]

    held = {}

    def conv_stage_2(got_down):
        held["parts_conv"] = added("conv", g_conv, got_down)
        return reduce_2(None, held["parts_conv"])

    dx1, gf0, _, slots_conv = _ffn_bwd("f0", dx2, ffn0, sc_f, gt_f, g_ffn[0], Wup[0], cw3[0], cb3[0], Wdn[0],
                                       comm_down=reduce_1(None, g_conv), comm_mid=conv_stage_2)
    parts_conv = held["parts_conv"]
    g_up0, g_dn0 = [gf0["dwup"]], [gf0["dwdn"].reshape(N_CHIP, -1, D)]
    dymix, dgt_m = gate_res_bwd("l0_res_b", dx1, ymix, gt_m)
    dmix, got = mm_nt("l0_out_bx", dymix, Wout, BF16, comm=reduce_1(None, g_up0))
    parts_up0 = added("up0", g_up0, got)
    dWout, got = mm_tn("l0_out_bw", mixcat, dymix, BF16, comm=reduce_1(None, g_dn0))
    parts_dn0 = added("dn0", g_dn0, got)
    (dq, dk, dv, dkc, dvc, dbias), slots_up0 = natten_bwd("na_bwd", qkv, kvc, bias, o_na, lse, dmix,
                                                          comm=reduce_2(None, parts_up0))
    dy, zg, dzz, dd_skip = glu_bwd("s5_glu_b", u, y0, y1, ssm_d[0], Wglu, dmix)
    dWglu = mm_tn("s5_glu_bw", zg, dzz, BF16)
    g_mix2 = [dWglu.reshape(N_CHIP, -1, W), dWout.reshape(N_CHIP, -1, D)]

    zc = jnp.zeros((Lc, W), F32)
    dyseq = _interleave(jnp.stack([jnp.concatenate([zc, dy]), jnp.concatenate([dy, zc])]).astype(BF16))
    (gloc, gfin), slots_dn0 = s5_scan("s5_scan_b", dyseq, Cblk.astype(BF16), a_b, rev=True, comm=reduce_2(None, parts_dn0))
    apow_b = (apow[0], -apow[1])
    (gst, duseq), got = s5_fix("s5_fix_b", gloc, _segment_carry(gfin, apow_b, True), a_b, Bblk.swapaxes(-1, -2).astype(BF16),
                               rev=True, comm=reduce_1(None, g_mix2))
    parts_mix2 = added("mix2", g_mix2, got)
    (dBm, dCm, da8), slots_mix2 = s5_grads("s5_grads", gst, hst, useq, dyseq, comm=reduce_2(None, parts_mix2))
    dus = _deinterleave(duseq)
    du = fma3("s5_du", dy, dus[0, Lc:], dus[1, :L], ssm_d[0], BF16)
    duc = dus[0, :Lc] + dus[1, L:]

    d_in = [du, dq, dk, dv]
    d_in_c = [duc, jnp.zeros((Lc, W), BF16), dkc, dvc]
    dh0 = mm_nt_list("l0_in_bx", d_in, Win, BF16)
    dhc0 = mm_nt_list("l0_in_bxc", d_in_c, Win, BF16)
    h_all = jnp.concatenate([hc0, h0])
    dWin = jnp.stack([mm_tn("l0_in_bw%d" % q, h_all, jnp.concatenate([dc.astype(BF16), dl.astype(BF16)]), BF16)
                      for q, (dc, dl) in enumerate(zip(d_in_c, d_in))])
    dx0, cs1_m, cs2_m = norm_mod_bwd("l0_norm_b", xs, dh0, g_mix[0] * (1.0 + sc_m), dx1)
    _, cs1_c, cs2_c = norm_mod_bwd("l0_norm_bc", cx, dhc0, g_mix[0] * (1.0 + scc), jnp.zeros_like(cx))

    dmod0 = jnp.concatenate([cs1_m[0], cs2_m[0] * g_mix[0], dgt_m[0], gf0["dsh"], gf0["dsc"], gf0["dgt"]])
    dmod1 = jnp.concatenate([cs1_v[0], cs2_v[0] * g_mix[1], dgt_v[0], gf1["dsh"], gf1["dsc"], gf1["dgt"]])
    dmodc = jnp.concatenate([cs1_c[0], cs2_c[0] * g_mix[0], jnp.zeros((4 * D,), F32)])
    dm_rows = jnp.concatenate([jnp.stack([dmod0, dmod1, dmodc]), jnp.zeros((SUB - 3, N_MOD * D), F32)])
    dm_all = allgather_dev("gather_dmod", dm_rows).reshape(N_DEV, SUB, N_MOD * D)
    dm_sum = sum_lead("sum_dmod", dm_all, F32)
    pad7 = jnp.zeros((2 * SUB - N_DEV - 1, N_MOD * D), F32)
    dMod = [jnp.concatenate([dm_all[:, 0], dm_sum[2:3], pad7]), jnp.concatenate([dm_all[:, 1], jnp.zeros_like(dm_sum[2:3]), pad7])]
    dMod_cols = [_pick_index(m.reshape(m.shape[0], N_CHIP, Cq), chip, 1) for m in dMod]
    g_w_mod = jnp.stack([mm_tn("mod_bw%d" % l, S16, dMod_cols[l], F32) for l in range(2)])
    g_b_mod = jnp.stack([dm_sum[0] + dm_sum[2], dm_sum[1]])
    ds_part = mm_nt("mod_bx", dMod_cols[0], w_mod[0], F32)
    ds_all = allgather_dev("gather_dsc", jnp.zeros((SUB, D), F32).at[0].set(ds_part[N_DEV]))
    ds_c = sum_lead("sum_dsc", ds_all.reshape(N_DEV, SUB, D)[0::2], F32)[0]
    sg_c = jax.nn.sigmoid(c_ctx)
    g_c_ctx = ds_c * sg_c * (1.0 + c_ctx * (1.0 - sg_c))

    g_rpb_loc = na_bias_grad(dbias)

    dbb = [_blockdiag_extract(dBm[:, z], Cg, P) for z in range(2)]
    dcc = [_blockdiag_extract(dCm[:, z], Cg, P).swapaxes(-1, -2) for z in range(2)]
    da = jnp.sum(da8, axis=2).reshape(2, 2, G, P)
    small = {
        "g_mix": jnp.stack([cs2_m[0] * (1.0 + sc_m) + cs2_c[0] * (1.0 + scc), cs2_v[0] * (1.0 + sc_v)]),
        "g_ffn": jnp.stack([gf0["dg"], gf1["dg"]]),
        "a_re": da[:, 0], "a_im": da[:, 1], "bb_re": dbb[0], "bb_im": dbb[1], "c_re": dcc[0], "c_im": -dcc[1],
        "ssm_d": dd_skip, "na_rpb": g_rpb_loc, "cv_dw_w": ddw_w, "cv_dw_b": ddw_b, "cv_ln_g": dln_g, "cv_ln_b": dln_b,
        "ffn_conv_w": jnp.stack([gf0["dcw"], gf1["dcw"]]), "ffn_conv_b": jnp.stack([gf0["dcb"], gf1["dcb"]]),
        "g_out": dg_out,
    }
    skeys = list(small)
    sbuf = _pack([small[k] for k in skeys], 1024, 4 * SUB)
    ssum = dict(zip(skeys, _unpack(allreduce_small("reduce_small", sbuf, ids), [small[k].shape for k in skeys])))
    g_lam_re, g_lam_im, g_log_dt, g_b_re, g_b_im = disc_vjp((ssum["a_re"], ssum["a_im"], ssum["bb_re"], ssum["bb_im"]))

    def my_cols(t):
        n = t.shape[-1] // N_CHIP
        return _pick_index(t.reshape(t.shape[:-1] + (N_CHIP, n)), chip, t.ndim - 1)

    delta, new_m, new_v = {}, {}, {}
    parts_win = added("win", [dWin], reduce_1("reduce_win_1", [dWin]))
    (delta["w_mod"], new_m["w_mod"], new_v["w_mod"]), slots_win = adamw(
        "adamw_w_mod", w_mod, g_w_mod, m_w_mod, v_w_mod, comm=reduce_2(None, parts_win))
    parts = parts_win + parts_mix2 + parts_conv + parts_up0 + parts_dn0 + parts_up1 + parts_dn1
    slots = [*slots_win, *slots_mix2, *slots_conv, *slots_up0, *slots_dn0, *slots_up1, *slots_dn1]
    fulls = [sum_slots("reduce_sum_%d" % i, s, t, ids) for i, (s, t) in enumerate(zip(slots, parts))]
    full = [f.reshape(-1, f.shape[-1]) for f in reduce_3("reduce_g_3", fulls)]
    gWin, gWglu, gWout, gWpw1, gWpw2, gWup0, gWdn0, gWup1, gWdn1 = full

    grads = {
        "c_ctx": g_c_ctx, "w_mod": g_w_mod, "b_mod": g_b_mod, "g_mix": ssum["g_mix"], "g_ffn": ssum["g_ffn"],
        "w_in": gWin[None], "ssm_lam_re": g_lam_re[None], "ssm_lam_im": g_lam_im[None], "ssm_log_dt": g_log_dt[None],
        "ssm_b_re": g_b_re[None], "ssm_b_im": g_b_im[None], "ssm_c_re": ssum["c_re"][None], "ssm_c_im": ssum["c_im"][None],
        "ssm_d": ssum["ssm_d"], "ssm_w_glu": gWglu[None], "na_rpb": ssum["na_rpb"][None], "w_out": gWout[None],
        "cv_w_pw1": gWpw1[None], "cv_dw_w": my_cols(ssum["cv_dw_w"])[None], "cv_dw_b": my_cols(ssum["cv_dw_b"]),
        "cv_ln_g": my_cols(ssum["cv_ln_g"]), "cv_ln_b": my_cols(ssum["cv_ln_b"]), "cv_w_pw2": gWpw2[None],
        "ffn_w_up": jnp.stack([gWup0, gWup1]), "ffn_conv_w": my_cols(ssum["ffn_conv_w"]), "ffn_conv_b": ssum["ffn_conv_b"],
        "ffn_w_down": jnp.stack([gWdn0, gWdn1]), "g_out": ssum["g_out"][0],
    }
    grads = {k: grads[k].reshape(p[k].shape) for k in _WEIGHTS}

    large = [k for k in _WEIGHTS if p[k].size >= (1 << 18) or k == "w_mod"]
    tiny = [k for k in _WEIGHTS if k not in large]
    for k in large:
        if k != "w_mod":
            delta[k], new_m[k], new_v[k] = adamw("adamw_" + k, p[k], grads[k], p["m_" + k], p["v_" + k])
    packs = [_pack([src[pre + k] for k in tiny], 1024) for src, pre in ((p, ""), (grads, ""), (p, "m_"), (p, "v_"))]
    outs = adamw("adamw_small", *packs)
    shapes = [p[k].shape for k in tiny]
    for dst, buf in zip((delta, new_m, new_v), outs):
        dst.update(zip(tiny, _unpack(buf, shapes)))

    return (loss, dx0[None], *[grads[k] for k in _WEIGHTS], *[delta[k] for k in _WEIGHTS],
            *[new_m[k] for k in _WEIGHTS], *[new_v[k] for k in _WEIGHTS])
```

```python
import functools
import math

import numpy as np
import jax
import jax.numpy as jnp
from jax import lax
from jax.experimental import pallas as pl
from jax.experimental.pallas import tpu as pltpu

F32, BF16 = jnp.float32, jnp.bfloat16
MESH = pl.DeviceIdType.MESH
V7X_VMEM_LIMIT = 56 << 20
LANE, SUB = 128, 8
N_CHIP, N_DEV = 4, 8

GRID_W = 64
N_MOD = 6
SSM_GROUP, SSM_STATE = 16, 64
NA_HEAD_DIM, NA_WIN_R, NA_WIN_C = 128, 8, 16
EPS = 1e-6
NEG = -1e30
ADAM_LR, ADAM_B1, ADAM_B2, ADAM_EPS, ADAM_WD, ADAM_STEP = 0.001, 0.9, 0.999, 1e-08, 0.01, 10
S5_STRIP = 512
S5_SEG = 8

NN = (((1,), (0,)), ((), ()))
NT = (((1,), (1,)), ((), ()))
TN = (((0,), (0,)), ((), ()))


def _params(*sem, side_effects=False):
    return pltpu.CompilerParams(dimension_semantics=sem if sem else None, vmem_limit_bytes=V7X_VMEM_LIMIT,
                                has_side_effects=side_effects)


def _call(body, args, *, name, grid, in_specs, out_specs, out_shape, sem, scratch_shapes=(), comm=None):
    out_specs, out_shape, scratch_shapes = list(out_specs), list(out_shape), list(scratch_shapes)
    if comm is None:
        outs = pl.pallas_call(body, name=name, grid=grid, in_specs=list(in_specs), out_specs=out_specs, out_shape=out_shape,
                              scratch_shapes=scratch_shapes, compiler_params=_params(*sem))(*args)
        return list(outs), []
    c_args, c_shapes, c_alias, n_remote, plan = comm
    n_in, n_out, n_ci, n_co, n_sc = len(args), len(out_shape), len(c_args), len(c_shapes), len(scratch_shapes)

    def wrapped(*refs):
        ins, cins = refs[:n_in], refs[n_in:n_in + n_ci]
        o0 = n_in + n_ci
        outs, couts = refs[o0:o0 + n_out], refs[o0 + n_out:o0 + n_out + n_co]
        s0 = o0 + n_out + n_co
        scr, (send_sems, recv_sems) = refs[s0:s0 + n_sc], refs[s0 + n_sc:]
        pids = [pl.program_id(a) for a in range(len(grid))]
        first = functools.reduce(jnp.logical_and, [q == 0 for q in pids])
        last = functools.reduce(jnp.logical_and, [q == g - 1 for q, g in zip(pids, grid)])
        me = (lax.axis_index("x"), lax.axis_index("y"), lax.axis_index("c"))

        def copies():
            _, sends, lands = plan(cins, couts)
            assert len(sends) == n_remote and len(lands) == n_remote
            out = [pltpu.make_async_remote_copy(src_ref=s, dst_ref=d, send_sem=send_sems.at[i], recv_sem=recv_sems.at[i],
                                                device_id=peer, device_id_type=MESH) for i, (s, d, peer) in enumerate(sends)]
            arrivals = [pltpu.make_async_remote_copy(src_ref=d, dst_ref=d, send_sem=send_sems.at[i], recv_sem=recv_sems.at[i],
                                                     device_id=me, device_id_type=MESH) for i, d in enumerate(lands)]
            return out, arrivals

        @pl.when(first)
        def _():
            for cp in copies()[0]:
                cp.start()

        body(*ins, *outs, *scr)

        @pl.when(last)
        def _():
            out, arrivals = copies()
            for cp in arrivals:
                cp.wait_recv()
            for cp in out:
                cp.wait_send()

    any_spec = pl.BlockSpec(memory_space=pl.ANY)
    res = pl.pallas_call(
        wrapped, name=name, grid=grid,
        in_specs=[*in_specs, *[any_spec] * n_ci], out_specs=[*out_specs, *[any_spec] * n_co],
        out_shape=[*out_shape, *[jax.ShapeDtypeStruct(s, d) for s, d in c_shapes]],
        input_output_aliases={n_in + i: n_out + j for i, j in c_alias.items()},
        scratch_shapes=[*scratch_shapes, pltpu.SemaphoreType.DMA((n_remote,)), pltpu.SemaphoreType.DMA((n_remote,))],
        compiler_params=_params(*["arbitrary"] * len(grid), side_effects=True),
    )(*args, *c_args)
    return list(res[:n_out]), list(res[n_out:])


def _pick(n, pref, mult=LANE):
    if n <= pref:
        return n
    best = None
    for t in range(mult, pref + 1, mult):
        if n % t == 0:
            best = t
    assert best is not None, (n, pref, mult)
    return best


def _sigmoid(x):
    return 1.0 / (1.0 + jnp.exp(-x))


def _mm(name, a, b, *, dims, grid, a_spec, b_spec, o_spec, out_shape, out_dtype, acc_shape, exact=False, comm=None):
    nk = grid[2]

    def body(a_ref, b_ref, o_ref, *scratch):
        if exact:
            part = lax.dot_general(a_ref[...], b_ref[...], dims, preferred_element_type=F32,
                                   precision=lax.Precision.HIGHEST)
        else:
            part = lax.dot_general(a_ref[...].astype(BF16), b_ref[...].astype(BF16), dims,
                                   preferred_element_type=F32)
        if nk == 1:
            o_ref[...] = part.astype(o_ref.dtype)
        else:
            acc = scratch[0]
            kk = pl.program_id(2)

            @pl.when(kk == 0)
            def _():
                acc[...] = part

            @pl.when(kk > 0)
            def _():
                acc[...] += part

            @pl.when(kk == nk - 1)
            def _():
                o_ref[...] = acc[...].astype(o_ref.dtype)

    outs, couts = _call(body, [a, b], name=name, grid=grid, in_specs=[a_spec, b_spec], out_specs=[o_spec],
                        out_shape=[jax.ShapeDtypeStruct(out_shape, out_dtype)],
                        scratch_shapes=[] if nk == 1 else [pltpu.VMEM(acc_shape, F32)],
                        sem=("parallel", "parallel", "arbitrary"), comm=comm)
    return outs[0] if comm is None else (outs[0], couts)


MM_VMEM_BUDGET = 36 << 20


def _fit(M, N, cost, m_mult=SUB):
    best = None
    for tm in sorted({_pick(M, p, m_mult) for p in (2048, 1024, 512, 256, 128)}):
        for tn in sorted({_pick(N, p) for p in (1408, 1024, 512, 256, 128)}):
            if best is None or (cost(tm, tn) <= MM_VMEM_BUDGET and tm * tn > best[0] * best[1]):
                best = (tm, tn)
    return best


def _sz(t):
    return jnp.dtype(t).itemsize


def mm_nn_pieces(name, a, w, p0, n_p, out_dtype, halves=1, comm=None):
    M, K = a.shape
    Nq = w.shape[2]
    tm, tn = _fit(M, Nq, lambda m, n: 2 * (m * K * _sz(a.dtype) + K * n * _sz(w.dtype) + m * n * _sz(out_dtype)))
    tpp = Nq // tn
    pph = n_p // halves
    if halves == 1:
        o_spec = pl.BlockSpec((tm, tn), lambda i, j, k: (i, j))
        oshape = (M, n_p * Nq)
    else:
        o_spec = pl.BlockSpec((None, tm, tn), lambda i, j, k: ((j // tpp) // pph, i, ((j // tpp) % pph) * tpp + j % tpp))
        oshape = (halves, M, pph * Nq)
    return _mm(name, a, w, dims=NN, grid=(M // tm, n_p * tpp, 1),
               a_spec=pl.BlockSpec((tm, K), lambda i, j, k: (i, 0)),
               b_spec=pl.BlockSpec((None, K, tn), lambda i, j, k: (p0 + j // tpp, 0, j % tpp)),
               o_spec=o_spec, out_shape=oshape, out_dtype=out_dtype, acc_shape=(tm, tn), comm=comm)


def mm_nn(name, a, w, out_dtype, exact=False, comm=None):
    M, K = a.shape
    N = w.shape[1]
    tm, tn = _fit(M, N, lambda m, n: 2 * (m * K * _sz(a.dtype) + K * n * _sz(w.dtype) + m * n * _sz(out_dtype)))
    return _mm(name, a, w, dims=NN, grid=(M // tm, N // tn, 1),
               a_spec=pl.BlockSpec((tm, K), lambda i, j, k: (i, 0)),
               b_spec=pl.BlockSpec((K, tn), lambda i, j, k: (0, j)),
               o_spec=pl.BlockSpec((tm, tn), lambda i, j, k: (i, j)),
               out_shape=(M, N), out_dtype=out_dtype, acc_shape=(tm, tn), exact=exact, comm=comm)


def mm_nt(name, dy, w, out_dtype, exact=False, comm=None):
    M, N = dy.shape
    K = w.shape[0]
    tm, tn = _fit(M, K, lambda m, n: 2 * (m * N * _sz(dy.dtype) + n * N * _sz(w.dtype) + m * n * _sz(out_dtype)))
    return _mm(name, dy, w, dims=NT, grid=(M // tm, K // tn, 1),
               a_spec=pl.BlockSpec((tm, N), lambda i, j, k: (i, 0)),
               b_spec=pl.BlockSpec((tn, N), lambda i, j, k: (j, 0)),
               o_spec=pl.BlockSpec((tm, tn), lambda i, j, k: (i, j)),
               out_shape=(M, K), out_dtype=out_dtype, acc_shape=(tm, tn), exact=exact, comm=comm)


def mm_nt_pieces(name, dy, w, out_dtype, halves=1, comm=None):
    P, K, Nq = w.shape
    M = dy.shape[-2]
    tm, tn = _fit(M, K, lambda m, n: 2 * (m * Nq * _sz(dy.dtype) + n * Nq * _sz(w.dtype) + m * n * _sz(out_dtype)) + 4 * m * n)
    pph = P // halves
    if halves == 1:
        a_spec = pl.BlockSpec((tm, Nq), lambda i, j, k: (i, k))
    else:
        a_spec = pl.BlockSpec((None, tm, Nq), lambda i, j, k: (k // pph, i, k % pph))
    return _mm(name, dy, w, dims=NT, grid=(M // tm, K // tn, P),
               a_spec=a_spec,
               b_spec=pl.BlockSpec((None, tn, Nq), lambda i, j, k: (k, j, 0)),
               o_spec=pl.BlockSpec((tm, tn), lambda i, j, k: (i, j)),
               out_shape=(M, K), out_dtype=out_dtype, acc_shape=(tm, tn), comm=comm)


def mm_nt_list(name, dys, w, out_dtype):
    P, K, Nq = w.shape
    M = dys[0].shape[0]
    assert len(dys) == P
    tm, tn = _fit(M, K, lambda m, n: 2 * (sum(m * Nq * _sz(d.dtype) for d in dys) + n * Nq * _sz(w.dtype)
                                          + m * n * _sz(out_dtype)) + 4 * m * n)

    def body(*refs):
        d_refs, w_ref, o_ref, acc = refs[:P], refs[P], refs[P + 1], refs[P + 2]
        kk = pl.program_id(2)
        for q in range(P):
            @pl.when(kk == q)
            def _(q=q):
                part = lax.dot_general(d_refs[q][...].astype(BF16), w_ref[...], NT, preferred_element_type=F32)
                acc[...] = part if q == 0 else acc[...] + part

        @pl.when(kk == P - 1)
        def _():
            o_ref[...] = acc[...].astype(o_ref.dtype)

    return pl.pallas_call(
        body, name=name, grid=(M // tm, K // tn, P),
        in_specs=[*[pl.BlockSpec((tm, Nq), lambda i, j, k: (i, 0)) for _ in range(P)],
                  pl.BlockSpec((None, tn, Nq), lambda i, j, k: (k, j, 0))],
        out_specs=pl.BlockSpec((tm, tn), lambda i, j, k: (i, j)),
        out_shape=jax.ShapeDtypeStruct((M, K), out_dtype),
        scratch_shapes=[pltpu.VMEM((tm, tn), F32)],
        compiler_params=_params("parallel", "arbitrary", "arbitrary"),
    )(*dys, w)


def mm_tn(name, a, dy, out_dtype, comm=None):
    M, K = a.shape
    N = dy.shape[1]
    tm, tn = _fit(K, N, lambda m, n: 2 * (M * m * _sz(a.dtype) + M * n * _sz(dy.dtype) + m * n * _sz(out_dtype)), LANE)
    return _mm(name, a, dy, dims=TN, grid=(K // tm, N // tn, 1),
               a_spec=pl.BlockSpec((M, tm), lambda i, j, k: (0, i)),
               b_spec=pl.BlockSpec((M, tn), lambda i, j, k: (0, j)),
               o_spec=pl.BlockSpec((tm, tn), lambda i, j, k: (i, j)),
               out_shape=(K, N), out_dtype=out_dtype, acc_shape=(tm, tn), comm=comm)


def mm_tn_pieces(name, a, dy, n_p, out_dtype, halves=1, comm=None):
    M, K = a.shape
    Nq = (dy.shape[-1] * halves) // n_p
    tm, tn = _fit(K, Nq, lambda m, n: 2 * (M * m * _sz(a.dtype) + M * n * _sz(dy.dtype) + m * n * _sz(out_dtype)), LANE)
    tpp = Nq // tn
    pph = n_p // halves
    if halves == 1:
        b_spec = pl.BlockSpec((M, tn), lambda i, j, k: (0, j))
    else:
        b_spec = pl.BlockSpec((None, M, tn), lambda i, j, k: ((j // tpp) // pph, 0, ((j // tpp) % pph) * tpp + j % tpp))
    return _mm(name, a, dy, dims=TN, grid=(K // tm, n_p * tpp, 1),
               a_spec=pl.BlockSpec((M, tm), lambda i, j, k: (0, i)),
               b_spec=b_spec,
               o_spec=pl.BlockSpec((None, tm, tn), lambda i, j, k: (j // tpp, i, j % tpp)),
               out_shape=(n_p, K, Nq), out_dtype=out_dtype, acc_shape=(tm, tn), comm=comm)


def _row_call(name, body, ins, in_kinds, outs, rows, tr, scratch=()):
    def spec(kind, shape):
        if isinstance(kind, pl.BlockSpec):
            return kind
        if kind == "row":
            return pl.BlockSpec((tr,) + tuple(shape[1:]), lambda i: (i,) + (0,) * (len(shape) - 1))
        return pl.BlockSpec(tuple(shape), lambda i: (0,) * len(shape))

    return pl.pallas_call(
        body, name=name, grid=(rows // tr,),
        in_specs=[spec(k, a.shape) for k, a in zip(in_kinds, ins)],
        out_specs=[spec(k, s) for k, s, _ in outs],
        out_shape=[jax.ShapeDtypeStruct(s, d) for _, s, d in outs],
        scratch_shapes=list(scratch),
        compiler_params=_params("arbitrary"),
    )(*ins)


def _acc(ref, val):
    @pl.when(pl.program_id(0) == 0)
    def _():
        ref[...] = val

    @pl.when(pl.program_id(0) > 0)
    def _():
        ref[...] += val


def norm_mod_fwd(name, x, w, b, tr=256):
    rows, d = x.shape
    tr = _pick(rows, tr, SUB)

    def body(x_ref, w_ref, b_ref, h_ref):
        xv = x_ref[...]
        r = lax.rsqrt(jnp.mean(xv * xv, axis=-1, keepdims=True) + EPS)
        h_ref[...] = (xv * r * w_ref[...] + b_ref[...]).astype(BF16)

    return _row_call(name, body, [x, w.reshape(1, d), b.reshape(1, d)], ["row", "vec", "vec"],
                     [("row", (rows, d), BF16)], rows, tr)[0]


def norm_mod_bwd(name, x, dh, w, dx_in, tr=256):
    rows, d = x.shape
    tr = _pick(rows, tr, SUB)

    def body(x_ref, dh_ref, w_ref, dxi_ref, dx_ref, cs1_ref, cs2_ref):
        xv = x_ref[...]
        r = lax.rsqrt(jnp.mean(xv * xv, axis=-1, keepdims=True) + EPS)
        xn = xv * r
        dhv = dh_ref[...].astype(F32)
        dxn = dhv * w_ref[...]
        dx_ref[...] = dxi_ref[...] + r * (dxn - xn * jnp.mean(dxn * xn, axis=-1, keepdims=True))
        _acc(cs1_ref, jnp.sum(dhv, axis=0, keepdims=True))
        _acc(cs2_ref, jnp.sum(dhv * xn, axis=0, keepdims=True))

    return _row_call(name, body, [x, dh, w.reshape(1, d), dx_in], ["row", "row", "vec", "row"],
                     [("row", (rows, d), F32), ("acc", (1, d), F32), ("acc", (1, d), F32)], rows, tr)


def gate_res_fwd(name, x, y, gate, tr=256):
    rows, d = x.shape
    tr = _pick(rows, tr, SUB)

    def body(x_ref, y_ref, g_ref, o_ref):
        o_ref[...] = x_ref[...] + g_ref[...] * y_ref[...].astype(F32)

    return _row_call(name, body, [x, y, gate.reshape(1, d)], ["row", "row", "vec"],
                     [("row", (rows, d), F32)], rows, tr)[0]


def gate_res_bwd(name, dx, y, gate, tr=256):
    rows, d = dx.shape
    tr = _pick(rows, tr, SUB)

    def body(dx_ref, y_ref, g_ref, dy_ref, dg_ref):
        dxv = dx_ref[...]
        dy_ref[...] = (g_ref[...] * dxv).astype(BF16)
        _acc(dg_ref, jnp.sum(dxv * y_ref[...].astype(F32), axis=0, keepdims=True))

    return _row_call(name, body, [dx, y, gate.reshape(1, d)], ["row", "row", "vec"],
                     [("row", (rows, d), BF16), ("acc", (1, d), F32)], rows, tr)


def loss_head(name, x, g, target, tr=256):
    rows, d = x.shape
    tr = _pick(rows, tr, SUB)

    def body(x_ref, g_ref, t_ref, dx_ref, dg_ref, loss_ref):
        xv = x_ref[...]
        r = lax.rsqrt(jnp.mean(xv * xv, axis=-1, keepdims=True) + EPS)
        xn = xv * r
        err = xn * g_ref[...] - t_ref[...]
        dy = err * (1.0 / d)
        dxn = dy * g_ref[...]
        dx_ref[...] = r * (dxn - xn * jnp.mean(dxn * xn, axis=-1, keepdims=True))
        _acc(dg_ref, jnp.sum(dy * xn, axis=0, keepdims=True))
        part = 0.5 * jnp.sum(jnp.sum(err * err, axis=-1, keepdims=True) * (1.0 / d), axis=0, keepdims=True)
        _acc(loss_ref, jnp.broadcast_to(part, (1, LANE)))

    return _row_call(name, body, [x, g.reshape(1, d), target], ["row", "vec", "row"],
                     [("row", (rows, d), F32), ("acc", (1, d), F32), ("acc", (1, LANE), F32)], rows, tr)


def fma3(name, a, b, c, dvec, out_dtype, tr=256):
    rows, d = a.shape
    tr = _pick(rows, tr, SUB)

    def body(a_ref, b_ref, c_ref, d_ref, o_ref):
        o_ref[...] = (d_ref[...] * a_ref[...] + b_ref[...] + c_ref[...]).astype(o_ref.dtype)

    return _row_call(name, body, [a, b, c, dvec.reshape(1, d)], ["row", "row", "row", "vec"],
                     [("row", (rows, d), out_dtype)], rows, tr)[0]


def sum_lead(name, a, out_dtype, tr=512):
    n, rows, cols = a.shape
    tr = _pick(rows, tr, 16)

    def body(a_ref, o_ref):
        acc = a_ref[0].astype(F32)
        for s in range(1, n):
            acc = acc + a_ref[s].astype(F32)
        o_ref[...] = acc.astype(o_ref.dtype)

    return pl.pallas_call(
        body, name=name, grid=(rows // tr,),
        in_specs=[pl.BlockSpec((n, tr, cols), lambda i: (0, i, 0))],
        out_specs=pl.BlockSpec((tr, cols), lambda i: (i, 0)),
        out_shape=jax.ShapeDtypeStruct((rows, cols), out_dtype),
        compiler_params=_params("parallel"),
    )(a)


def adamw(name, w, g, m, v, comm=None):
    shape = w.shape
    cols = shape[-1]
    w2, g2, m2, v2 = (t.reshape(-1, cols) for t in (w, g, m, v))
    rows = w2.shape[0]
    tr, tc = _pick(rows, 256, SUB), _pick(cols, 1536)
    c1 = 1.0 - ADAM_B1 ** ADAM_STEP
    c2 = 1.0 - ADAM_B2 ** ADAM_STEP

    def body(w_ref, g_ref, m_ref, v_ref, d_ref, mo_ref, vo_ref):
        gv = g_ref[...]
        mn = ADAM_B1 * m_ref[...] + (1.0 - ADAM_B1) * gv
        vn = ADAM_B2 * v_ref[...] + (1.0 - ADAM_B2) * (gv * gv)
        mo_ref[...] = mn
        vo_ref[...] = vn
        d_ref[...] = -ADAM_LR * ((mn / c1) / (jnp.sqrt(vn / c2) + ADAM_EPS) + ADAM_WD * w_ref[...])

    blk = pl.BlockSpec((tr, tc), lambda i, j: (i, j))
    outs, couts = _call(body, [w2, g2, m2, v2], name=name, grid=(rows // tr, cols // tc), in_specs=[blk] * 4,
                        out_specs=[blk] * 3, out_shape=[jax.ShapeDtypeStruct(w2.shape, F32)] * 3,
                        sem=("parallel", "parallel"), comm=comm)
    outs = tuple(o.reshape(shape) for o in outs)
    return outs if comm is None else (outs, couts)


def add_half(name, grad, got, c_idx, tr=256):
    Pn, R, C = grad.shape
    hr = R // 2
    tr = _pick(hr, tr, HALO)
    nb = hr // tr

    def body(c_ref, a_ref, b_ref, o_ref):
        o_ref[...] = (a_ref[...].astype(F32) + b_ref[...].astype(F32)).astype(o_ref.dtype)

    return pl.pallas_call(
        body, name=name,
        grid_spec=pltpu.PrefetchScalarGridSpec(
            num_scalar_prefetch=1, grid=(Pn, nb),
            in_specs=[pl.BlockSpec((None, tr, C), lambda q, i, c: (q, c[0] * nb + i, 0)),
                      pl.BlockSpec((None, tr, C), lambda q, i, c: (q, i, 0))],
            out_specs=pl.BlockSpec((None, tr, C), lambda q, i, c: (q, i, 0))),
        out_shape=jax.ShapeDtypeStruct((Pn, hr, C), BF16),
        compiler_params=_params("parallel", "parallel"),
    )(c_idx, grad, got)


def pair_sum_to_slot(name, buf, got, ids, tr=256):
    R, C = buf.shape
    hr = R // 2
    tr = _pick(hr, tr, SUB)
    nb = hr // tr

    def body(ids_ref, a_ref, b_ref, o_ref):
        o_ref[...] = a_ref[...] + b_ref[...]

    return pl.pallas_call(
        body, name=name,
        grid_spec=pltpu.PrefetchScalarGridSpec(
            num_scalar_prefetch=1, grid=(nb,),
            in_specs=[pl.BlockSpec((tr, C), lambda i, ids: (ids[1] * nb + i, 0)),
                      pl.BlockSpec((tr, C), lambda i, ids: (i, 0))],
            out_specs=pl.BlockSpec((None, tr, C), lambda i, ids: (ids[0], i, 0))),
        out_shape=jax.ShapeDtypeStruct((N_CHIP, hr, C), F32),
        compiler_params=_params("parallel"),
    )(ids, buf, got)


def sum_chips_to_half(name, slots, ids, tr=256):
    n, hr, C = slots.shape
    tr = _pick(hr, tr, SUB)

    def body(ids_ref, s_ref, o_ref):
        acc = s_ref[0]
        for q in range(1, n):
            acc = acc + s_ref[q]
        o_ref[...] = acc

    return pl.pallas_call(
        body, name=name,
        grid_spec=pltpu.PrefetchScalarGridSpec(
            num_scalar_prefetch=1, grid=(hr // tr,),
            in_specs=[pl.BlockSpec((n, tr, C), lambda i, ids: (0, i, 0))],
            out_specs=pl.BlockSpec((None, tr, C), lambda i, ids: (ids[1], i, 0))),
        out_shape=jax.ShapeDtypeStruct((2, hr, C), F32),
        compiler_params=_params("parallel"),
    )(ids, slots)


def sum_slots(name, slots, mine, ids, tr=256):
    Pn, hr, C = slots.shape
    tr = _pick(hr, tr, HALO)

    def body(ids_ref, m_ref, s1_ref, s2_ref, s3_ref, o_ref):
        o_ref[...] = (m_ref[...].astype(F32) + s1_ref[...].astype(F32)) + (s2_ref[...].astype(F32) + s3_ref[...].astype(F32))

    def other(k):
        return pl.BlockSpec((None, tr, C), lambda i, ids: ((ids[0] + k) % Pn, i, 0))

    return pl.pallas_call(
        body, name=name,
        grid_spec=pltpu.PrefetchScalarGridSpec(
            num_scalar_prefetch=1, grid=(hr // tr,),
            in_specs=[pl.BlockSpec((None, tr, C), lambda i, ids: (ids[0], i, 0)), other(1), other(2), other(3)],
            out_specs=pl.BlockSpec((None, tr, C), lambda i, ids: (ids[1], i, 0))),
        out_shape=jax.ShapeDtypeStruct((2, hr, C), F32),
        compiler_params=_params("parallel"),
    )(ids, mine, slots, slots, slots)


HALO = 16


def _halo_specs(lead, R, tn, n_rows, col_of):
    nb, nblk = R // HALO, n_rows // HALO

    def mk(rows, row_of):
        return pl.BlockSpec((lead, rows, tn), lambda *g: (0, row_of(g[-1]), col_of(g)))

    return (mk(HALO, lambda i: jnp.maximum(i * nb - 1, 0)), mk(R, lambda i: i),
            mk(HALO, lambda i: jnp.minimum((i + 1) * nb, nblk - 1)))


def _fill_halo(dst, i, last, R, prev, cur, nxt):
    nd = len(dst.shape)
    lead = (slice(None),) * (nd - 2)
    dst[lead + (slice(0, HALO), slice(None))] = jnp.where(i == 0, 0.0, prev)
    dst[lead + (slice(HALO, HALO + R), slice(None))] = cur
    dst[lead + (slice(HALO + R, HALO + R + HALO), slice(None))] = jnp.where(i == last, 0.0, nxt)


def _shift_mats(n):
    i = np.arange(n)
    return jnp.asarray(np.stack([i[:, None] - 1 == i[None, :], i[:, None] + 1 == i[None, :]]), BF16)


def _shifted(s_ref, xb):
    return (jnp.dot(s_ref[0], xb, preferred_element_type=F32), jnp.dot(s_ref[1], xb, preferred_element_type=F32))


def ffn_mid_fwd(name, up3, cw, cb, R=256, tn=512, comm=None):
    _, L, Fd = up3.shape
    R, tn = _pick(L, R, HALO), _pick(Fd, tn)
    nrow = L // R

    def body(p_ref, c_ref, n_ref, w_ref, b_ref, s_ref, act_ref):
        i = pl.program_id(1)
        row = lax.broadcasted_iota(jnp.int32, (R, tn), 0)
        cv = []
        for z in range(2):
            xb = c_ref[z]
            before = jnp.where(i == 0, 0.0, p_ref[z].astype(F32)[HALO - 1:HALO])
            after = jnp.where(i == nrow - 1, 0.0, n_ref[z].astype(F32)[0:1])
            dn, up = _shifted(s_ref, xb)
            dn = jnp.where(row == 0, before, dn)
            up = jnp.where(row == R - 1, after, up)
            cv.append(b_ref[z] + w_ref[z, 0:1, :] * dn + w_ref[z, 1:2, :] * xb.astype(F32) + w_ref[z, 2:3, :] * up)
        u, g = cv
        act_ref[...] = (u * g * _sigmoid(g)).astype(BF16)

    hs = _halo_specs(2, R, tn, L, lambda g: g[0])
    outs, couts = _call(
        body, [up3, up3, up3, cw, cb, _shift_mats(R)], name=name, grid=(Fd // tn, nrow),
        in_specs=[*hs, pl.BlockSpec((2, 3, tn), lambda j, i: (0, 0, j)), pl.BlockSpec((2, 1, tn), lambda j, i: (0, 0, j)),
                  pl.BlockSpec((2, R, R), lambda j, i: (0, 0, 0))],
        out_specs=[pl.BlockSpec((R, tn), lambda j, i: (i, j))],
        out_shape=[jax.ShapeDtypeStruct((L, Fd), BF16)], sem=("parallel", "arbitrary"), comm=comm)
    return outs[0] if comm is None else (outs[0], couts)


def ffn_mid_bwd(name, up3, dact, cw, cb, R=256, tn=512, comm=None):
    _, L, Fd = up3.shape
    R, tn = _pick(L, R, HALO), _pick(Fd, tn)
    nrow = L // R

    def gate_grads(u, g, d):
        sg = _sigmoid(g)
        return d * g * sg, d * u * sg * (1.0 + g * (1.0 - sg))

    def body(pu, cu, nu, pd, cd, nd, w_ref, b_ref, s_ref, dup_ref, dcw_ref, dcb_ref):
        i = pl.program_id(1)
        first, last = i == 0, i == nrow - 1
        row = lax.broadcasted_iota(jnp.int32, (R, tn), 0)
        cv, cv_b, cv_a, taps = [], [], [], []
        for z in range(2):
            xb = cu[z]
            xf = xb.astype(F32)
            pf = jnp.where(first, 0.0, pu[z].astype(F32))
            nf = jnp.where(last, 0.0, nu[z].astype(F32))
            xm2, xm1, xp0, xp1 = pf[HALO - 2:HALO - 1], pf[HALO - 1:HALO], nf[0:1], nf[1:2]
            dn, up = _shifted(s_ref, xb)
            dn = jnp.where(row == 0, xm1, dn)
            up = jnp.where(row == R - 1, xp0, up)
            w0, w1, w2, b = w_ref[z, 0:1, :], w_ref[z, 1:2, :], w_ref[z, 2:3, :], b_ref[z]
            cv.append(b + w0 * dn + w1 * xf + w2 * up)
            cv_b.append(b + w0 * xm2 + w1 * xm1 + w2 * xf[0:1])
            cv_a.append(b + w0 * xf[R - 1:R] + w1 * xp0 + w2 * xp1)
            taps.append((dn, xf, up))
        dcs = gate_grads(cv[0], cv[1], cd[0].astype(F32))
        dcs_b = gate_grads(cv_b[0], cv_b[1], jnp.where(first, 0.0, pd[0].astype(F32)[HALO - 1:HALO]))
        dcs_a = gate_grads(cv_a[0], cv_a[1], jnp.where(last, 0.0, nd[0].astype(F32)[0:1]))

        @pl.when(first)
        def _():
            dcw_ref[...] = jnp.zeros_like(dcw_ref)
            dcb_ref[...] = jnp.zeros_like(dcb_ref)

        for z in range(2):
            dc = dcs[z]
            dc_dn, dc_up = _shifted(s_ref, dc.astype(BF16))
            dc_dn = jnp.where(row == 0, dcs_b[z], dc_dn)
            dc_up = jnp.where(row == R - 1, dcs_a[z], dc_up)
            dup_ref[z] = (w_ref[z, 0:1, :] * dc_up + w_ref[z, 1:2, :] * dc + w_ref[z, 2:3, :] * dc_dn).astype(BF16)
            dcb_ref[z] += jnp.sum(dc, axis=0, keepdims=True)
            for k in range(3):
                dcw_ref[z, k:k + 1, :] += jnp.sum(dc * taps[z][k], axis=0, keepdims=True)

    hu = _halo_specs(2, R, tn, L, lambda g: g[0])
    hd = _halo_specs(1, R, tn, L, lambda g: g[0])
    outs, couts = _call(
        body, [up3, up3, up3, dact[None], dact[None], dact[None], cw, cb, _shift_mats(R)], name=name, grid=(Fd // tn, nrow),
        in_specs=[*hu, *hd, pl.BlockSpec((2, 3, tn), lambda j, i: (0, 0, j)), pl.BlockSpec((2, 1, tn), lambda j, i: (0, 0, j)),
                  pl.BlockSpec((2, R, R), lambda j, i: (0, 0, 0))],
        out_specs=[pl.BlockSpec((2, R, tn), lambda j, i: (0, i, j)), pl.BlockSpec((2, 3, tn), lambda j, i: (0, 0, j)),
                   pl.BlockSpec((2, 1, tn), lambda j, i: (0, 0, j))],
        out_shape=[jax.ShapeDtypeStruct((2, L, Fd), BF16), jax.ShapeDtypeStruct((2, 3, Fd), F32),
                   jax.ShapeDtypeStruct((2, 1, Fd), F32)],
        sem=("parallel", "arbitrary"), comm=comm)
    return outs if comm is None else (outs, couts)


def _glu_z0(blk):
    return blk[0].astype(F32) * _sigmoid(blk[1].astype(F32))


def _sublane_copies(ref, cs):
    n = ref.shape[1]
    blk = ref[0, :, cs]
    for b in range(1, SUB):
        ref[b, :, cs] = pltpu.roll(blk, n - b, 0)


def _tap(ref, offset, rows, cs):
    return ref[offset % SUB, pl.ds(offset - offset % SUB, rows), cs]


def conf_mid_fwd(name, ag3, dw_w, dw_b, ln_g, ln_b, R=128, cb=256):
    _, L, C = ag3.shape
    K = dw_w.shape[0]
    pad = (K - 1) // 2
    assert pad <= HALO
    R, cb = _pick(L, R, HALO), _pick(C, cb)
    nrow = L // R

    def body(p_ref, c_ref, n_ref, w_ref, b_ref, g_ref, bb_ref, z1_ref, z3_ref, s_ref):
        i = pl.program_id(0)
        _fill_halo(s_ref.at[0], i, nrow - 1, R, _glu_z0(p_ref), _glu_z0(c_ref), _glu_z0(n_ref))
        for c0 in range(0, C, cb):
            cs = slice(c0, c0 + cb)
            _sublane_copies(s_ref, cs)
            acc = jnp.broadcast_to(b_ref[:, cs], (R, cb))
            for k in range(K):
                acc = acc + w_ref[k:k + 1, cs] * _tap(s_ref, HALO - pad + k, R, cs)
            z1_ref[:, cs] = acc
        z1 = z1_ref[...]
        zc = z1 - jnp.mean(z1, axis=-1, keepdims=True)
        zn = zc * lax.rsqrt(jnp.mean(zc * zc, axis=-1, keepdims=True) + EPS)
        z2 = zn * g_ref[...] + bb_ref[...]
        z3_ref[...] = (z2 * _sigmoid(z2)).astype(BF16)

    hs = _halo_specs(2, R, C, L, lambda g: 0)
    vec = pl.BlockSpec((1, C), lambda i: (0, 0))
    return pl.pallas_call(
        body, name=name, grid=(nrow,),
        in_specs=[*hs, pl.BlockSpec((K, C), lambda i: (0, 0)), vec, vec, vec],
        out_specs=[pl.BlockSpec((R, C), lambda i: (i, 0)), pl.BlockSpec((R, C), lambda i: (i, 0))],
        out_shape=[jax.ShapeDtypeStruct((L, C), F32), jax.ShapeDtypeStruct((L, C), BF16)],
        scratch_shapes=[pltpu.VMEM((SUB, R + 2 * HALO, C), F32)],
        compiler_params=_params("parallel"),
    )(ag3, ag3, ag3, dw_w, dw_b.reshape(1, C), ln_g.reshape(1, C), ln_b.reshape(1, C))


def conf_ln_bwd(name, z1, dz3, ln_g, ln_b, tr=256):
    rows, C = z1.shape
    tr = _pick(rows, tr, HALO)

    def body(z_ref, d_ref, g_ref, b_ref, dz_ref, dg_ref, db_ref):
        z1v = z_ref[...]
        zc = z1v - jnp.mean(z1v, axis=-1, keepdims=True)
        rs = lax.rsqrt(jnp.mean(zc * zc, axis=-1, keepdims=True) + EPS)
        zn = zc * rs
        z2 = zn * g_ref[...] + b_ref[...]
        sg = _sigmoid(z2)
        dz2 = d_ref[...].astype(F32) * sg * (1.0 + z2 * (1.0 - sg))
        _acc(dg_ref, jnp.sum(dz2 * zn, axis=0, keepdims=True))
        _acc(db_ref, jnp.sum(dz2, axis=0, keepdims=True))
        dzn = dz2 * g_ref[...]
        dz1 = rs * (dzn - jnp.mean(dzn, axis=-1, keepdims=True) - zn * jnp.mean(dzn * zn, axis=-1, keepdims=True))
        dz_ref[...] = dz1.astype(BF16)

    return _row_call(name, body, [z1, dz3, ln_g.reshape(1, C), ln_b.reshape(1, C)], ["row", "row", "vec", "vec"],
                     [("row", (rows, C), BF16), ("acc", (1, C), F32), ("acc", (1, C), F32)], rows, tr)


def conf_conv_bwd(name, ag3, dz1, dw_w, R=128, cb=256, comm=None):
    _, L, C = ag3.shape
    K = dw_w.shape[0]
    pad = (K - 1) // 2
    R, cb = _pick(L, R, HALO), _pick(C, cb)
    nrow = L // R

    def body(pa, ca, na, pd, cd, nd, w_ref, dag_ref, dw_ref, db_ref, s_ref, d_ref, z_ref):
        i = pl.program_id(0)
        _fill_halo(s_ref.at[0], i, nrow - 1, R, _glu_z0(pa), _glu_z0(ca), _glu_z0(na))
        _fill_halo(d_ref.at[0], i, nrow - 1, R, pd[0].astype(F32), cd[0].astype(F32), nd[0].astype(F32))

        @pl.when(i == 0)
        def _():
            dw_ref[...] = jnp.zeros_like(dw_ref)
            db_ref[...] = jnp.zeros_like(db_ref)

        for c0 in range(0, C, cb):
            cs = slice(c0, c0 + cb)
            _sublane_copies(s_ref, cs)
            _sublane_copies(d_ref, cs)
            dcur = d_ref[0, pl.ds(HALO, R), cs]
            acc = jnp.zeros((R, cb), F32)
            for k in range(K):
                acc = acc + w_ref[k:k + 1, cs] * _tap(d_ref, HALO + pad - k, R, cs)
                dw_ref[k:k + 1, cs] += jnp.sum(dcur * _tap(s_ref, HALO - pad + k, R, cs), axis=0, keepdims=True)
            z_ref[:, cs] = acc
            db_ref[:, cs] += jnp.sum(dcur, axis=0, keepdims=True)
        dz0 = z_ref[...]
        a = ca[0].astype(F32)
        sg = _sigmoid(ca[1].astype(F32))
        dag_ref[0] = (dz0 * sg).astype(BF16)
        dag_ref[1] = (dz0 * a * sg * (1.0 - sg)).astype(BF16)

    ha = _halo_specs(2, R, C, L, lambda g: 0)
    hd = _halo_specs(1, R, C, L, lambda g: 0)
    outs, couts = _call(
        body, [ag3, ag3, ag3, dz1[None], dz1[None], dz1[None], dw_w], name=name, grid=(nrow,),
        in_specs=[*ha, *hd, pl.BlockSpec((K, C), lambda i: (0, 0))],
        out_specs=[pl.BlockSpec((2, R, C), lambda i: (0, i, 0)), pl.BlockSpec((K, C), lambda i: (0, 0)),
                   pl.BlockSpec((1, C), lambda i: (0, 0))],
        out_shape=[jax.ShapeDtypeStruct((2, L, C), BF16), jax.ShapeDtypeStruct((K, C), F32),
                   jax.ShapeDtypeStruct((1, C), F32)],
        scratch_shapes=[pltpu.VMEM((SUB, R + 2 * HALO, C), F32), pltpu.VMEM((SUB, R + 2 * HALO, C), F32),
                        pltpu.VMEM((R, C), F32)],
        sem=("arbitrary",), comm=comm)
    return outs if comm is None else (outs, couts)


_GELU_C = math.sqrt(2.0 / math.pi)


def _gelu(x):
    return 0.5 * x * (1.0 + jnp.tanh(_GELU_C * (x + 0.044715 * x * x * x)))


def _gelu_grad(x):
    t = jnp.tanh(_GELU_C * (x + 0.044715 * x * x * x))
    return 0.5 * (1.0 + t) + 0.5 * x * (1.0 - t * t) * _GELU_C * (1.0 + 3.0 * 0.044715 * x * x)


def glu_fwd(name, u, y0, y1, d, wg, tr=512):
    rows, W = u.shape
    tr = _pick(rows, tr, HALO)

    def body(u_ref, y0_ref, y1_ref, d_ref, w_ref, o_ref):
        z = _gelu(d_ref[...] * u_ref[...] + y0_ref[...] + y1_ref[...])
        zz = jnp.dot(z.astype(BF16), w_ref[...], preferred_element_type=F32)
        o_ref[...] = (z * _sigmoid(zz)).astype(BF16)

    return _row_call(name, body, [u, y0, y1, d.reshape(1, W), wg], ["row", "row", "row", "vec", "vec"],
                     [("row", (rows, W), BF16)], rows, tr)[0]


def glu_bwd(name, u, y0, y1, d, wg, dmix, tr=512):
    rows, W = u.shape
    tr = _pick(rows, tr, HALO)

    def body(u_ref, y0_ref, y1_ref, d_ref, w_ref, do_ref, dy_ref, z_ref, dzz_ref, dd_ref):
        uv = u_ref[...]
        y = d_ref[...] * uv + y0_ref[...] + y1_ref[...]
        z = _gelu(y)
        zz = jnp.dot(z.astype(BF16), w_ref[...], preferred_element_type=F32)
        sg = _sigmoid(zz)
        do = do_ref[...].astype(F32)
        dzz = (do * z * sg * (1.0 - sg)).astype(BF16)
        dz = do * sg + lax.dot_general(dzz, w_ref[...], NT, preferred_element_type=F32)
        dy = dz * _gelu_grad(y)
        dy_ref[...] = dy
        z_ref[...] = z.astype(BF16)
        dzz_ref[...] = dzz
        _acc(dd_ref, jnp.sum(dy * uv, axis=0, keepdims=True))

    do_spec = pl.BlockSpec((tr, W), lambda i: (i, 0))
    return _row_call(name, body, [u, y0, y1, d.reshape(1, W), wg, dmix], ["row", "row", "row", "vec", "vec", do_spec],
                     [("row", (rows, W), F32), ("row", (rows, W), BF16), ("row", (rows, W), BF16), ("acc", (1, W), F32)],
                     rows, tr)


NA_KEYS = NA_WIN_R * GRID_W


NA_PAIRS = NA_WIN_R // 2


def na_bias(rpb):
    H, nr, nc = rpb.shape
    e, ok = _na_col_select()
    rp = jnp.pad(rpb.reshape(H * nr, nc), ((0, (-H * nr) % SUB), (0, LANE - nc)))
    cols = mm_nn("na_bias_mm", rp, jnp.asarray(e, F32), F32, exact=True)[:H * nr]
    tiles = (cols + jnp.asarray(np.where(ok, 0.0, NEG), F32)).reshape(H, nr, GRID_W, GRID_W)
    return jnp.concatenate([tiles[:, :-1], tiles[:, 1:]], axis=-1)


def na_bias_grad(db2):
    H, n2 = db2.shape[:2]
    left, right = db2[..., :GRID_W], db2[..., GRID_W:]
    tiles = jnp.pad(left, ((0, 0), (0, 1), (0, 0), (0, 0))) + jnp.pad(right, ((0, 0), (1, 0), (0, 0), (0, 0)))
    flat = tiles.reshape(H * (n2 + 1), GRID_W * GRID_W)
    flat = jnp.pad(flat, ((0, (-flat.shape[0]) % SUB), (0, 0)))
    dcol = mm_nt("na_bias_fold", flat, na_bias_fold_matrix(), F32, exact=True)
    return dcol[:H * (n2 + 1), :2 * NA_WIN_C - 1].reshape(H, n2 + 1, 2 * NA_WIN_C - 1)


def _na_col_select():
    q = np.arange(GRID_W)
    cs = np.clip(q - NA_WIN_C // 2, 0, GRID_W - NA_WIN_C)
    ok = ((q[None, :] >= cs[:, None]) & (q[None, :] < cs[:, None] + NA_WIN_C)).reshape(-1)
    cidx = np.clip(q[None, :] - q[:, None] + (NA_WIN_C - 1), 0, 2 * NA_WIN_C - 2).reshape(-1)
    return (cidx[None, :] == np.arange(LANE)[:, None]) & ok[None, :], ok


def na_bias_fold_matrix():
    return jnp.asarray(_na_col_select()[0], F32)


def _na_window(r, rows):
    kr0 = jnp.clip(r - NA_WIN_R // 2, 0, rows - NA_WIN_R)
    return pl.multiple_of(kr0 * GRID_W, GRID_W), r - kr0


def _na_dims(qkv, kvc):
    L = qkv.shape[0]
    NA = qkv.shape[1] // 3
    H = NA // NA_HEAD_DIM
    hp = 2 if H % 2 == 0 else 1
    return L, NA, H, hp, H // hp, L // GRID_W, kvc.shape[0]


def _na_bias_tile(b_ref, hh, off):
    return jnp.concatenate([b_ref[hh, NA_WIN_R - 1 - off + 2 * j] for j in range(NA_PAIRS)], axis=-1)


def natten_fwd(name, qkv, kvc, bias, comm=None):
    L, NA, H, hp, G, rows, Lc = _na_dims(qkv, kvc)
    scale = NA_HEAD_DIM ** -0.5
    wd = hp * NA_HEAD_DIM

    def body(q_ref, k_ref, v_ref, kc_ref, vc_ref, b_ref, o_ref, lse_ref):
        st, off = _na_window(pl.program_id(1), rows)
        for hh in range(hp):
            sl = slice(hh * NA_HEAD_DIM, (hh + 1) * NA_HEAD_DIM)
            q = q_ref[:, sl]
            s_loc = (lax.dot_general(q, k_ref[pl.ds(st, NA_KEYS), sl], NT, preferred_element_type=F32) * scale
                     + _na_bias_tile(b_ref, hh, off))
            s_ctx = lax.dot_general(q, kc_ref[:, sl], NT, preferred_element_type=F32) * scale
            m = jnp.maximum(jnp.max(s_loc, axis=-1, keepdims=True), jnp.max(s_ctx, axis=-1, keepdims=True))
            p_loc, p_ctx = jnp.exp(s_loc - m), jnp.exp(s_ctx - m)
            l = jnp.sum(p_loc, axis=-1, keepdims=True) + jnp.sum(p_ctx, axis=-1, keepdims=True)
            o = (jnp.dot(p_loc.astype(BF16), v_ref[pl.ds(st, NA_KEYS), sl], preferred_element_type=F32)
                 + jnp.dot(p_ctx.astype(BF16), vc_ref[:, sl], preferred_element_type=F32))
            o_ref[:, sl] = (o / l).astype(BF16)
            lse_ref[hh] = m + jnp.log(l)

    outs, couts = _call(
        body, [qkv, qkv, qkv, kvc, kvc, bias], name=name, grid=(G, rows),
        in_specs=[pl.BlockSpec((GRID_W, wd), lambda h, r: (r, h)),
                  pl.BlockSpec((L, wd), lambda h, r: (0, G + h)),
                  pl.BlockSpec((L, wd), lambda h, r: (0, 2 * G + h)),
                  pl.BlockSpec((Lc, wd), lambda h, r: (0, h)),
                  pl.BlockSpec((Lc, wd), lambda h, r: (0, G + h)),
                  pl.BlockSpec((hp,) + bias.shape[1:], lambda h, r: (h, 0, 0, 0))],
        out_specs=[pl.BlockSpec((GRID_W, wd), lambda h, r: (r, h)),
                   pl.BlockSpec((hp, GRID_W, 1), lambda h, r: (h, r, 0))],
        out_shape=[jax.ShapeDtypeStruct((L, NA), BF16), jax.ShapeDtypeStruct((H, L, 1), F32)],
        sem=("parallel", "arbitrary"), comm=comm)
    return outs if comm is None else (outs, couts)


def natten_bwd(name, qkv, kvc, bias, o, lse, dmix, comm=None):
    L, NA, H, hp, G, rows, Lc = _na_dims(qkv, kvc)
    scale = NA_HEAD_DIM ** -0.5
    wd = hp * NA_HEAD_DIM

    def body(q_ref, k_ref, v_ref, kc_ref, vc_ref, b_ref, o_ref, lse_ref, do_ref,
             dq_ref, dk_ref, dv_ref, dkc_ref, dvc_ref, db_ref):
        r = pl.program_id(1)
        st, off = _na_window(r, rows)

        @pl.when(r == 0)
        def _():
            for ref in (dk_ref, dv_ref, dkc_ref, dvc_ref, db_ref):
                ref[...] = jnp.zeros_like(ref)

        for hh in range(hp):
            sl = slice(hh * NA_HEAD_DIM, (hh + 1) * NA_HEAD_DIM)
            q, kl, vl, kc, vc = q_ref[:, sl], k_ref[pl.ds(st, NA_KEYS), sl], v_ref[pl.ds(st, NA_KEYS), sl], kc_ref[:, sl], vc_ref[:, sl]
            do = do_ref[:, sl]
            lse_v = lse_ref[hh]
            p_loc = jnp.exp(lax.dot_general(q, kl, NT, preferred_element_type=F32) * scale + _na_bias_tile(b_ref, hh, off) - lse_v)
            p_ctx = jnp.exp(lax.dot_general(q, kc, NT, preferred_element_type=F32) * scale - lse_v)
            delta = jnp.sum(do.astype(F32) * o_ref[:, sl].astype(F32), axis=-1, keepdims=True)
            ds_loc = p_loc * (lax.dot_general(do, vl, NT, preferred_element_type=F32) - delta)
            ds_ctx = p_ctx * (lax.dot_general(do, vc, NT, preferred_element_type=F32) - delta)
            dsl, dsc = ds_loc.astype(BF16), ds_ctx.astype(BF16)
            dq = jnp.dot(dsl, kl, preferred_element_type=F32) + jnp.dot(dsc, kc, preferred_element_type=F32)
            dq_ref[:, sl] = (dq * scale).astype(BF16)
            dk_ref[pl.ds(st, NA_KEYS), sl] += lax.dot_general(dsl, q, TN, preferred_element_type=F32) * scale
            dv_ref[pl.ds(st, NA_KEYS), sl] += lax.dot_general(p_loc.astype(BF16), do, TN, preferred_element_type=F32)
            dkc_ref[:, sl] += lax.dot_general(dsc, q, TN, preferred_element_type=F32) * scale
            dvc_ref[:, sl] += lax.dot_general(p_ctx.astype(BF16), do, TN, preferred_element_type=F32)
            for j in range(NA_PAIRS):
                db_ref[hh, NA_WIN_R - 1 - off + 2 * j] += ds_loc[:, 2 * j * GRID_W:(2 * j + 2) * GRID_W]

    tok = pl.BlockSpec((GRID_W, wd), lambda h, r: (r, h))
    bia = pl.BlockSpec((hp,) + bias.shape[1:], lambda h, r: (h, 0, 0, 0))
    outs, couts = _call(
        body, [qkv, qkv, qkv, kvc, kvc, bias, o, lse, dmix], name=name, grid=(G, rows),
        in_specs=[tok,
                  pl.BlockSpec((L, wd), lambda h, r: (0, G + h)),
                  pl.BlockSpec((L, wd), lambda h, r: (0, 2 * G + h)),
                  pl.BlockSpec((Lc, wd), lambda h, r: (0, h)),
                  pl.BlockSpec((Lc, wd), lambda h, r: (0, G + h)),
                  bia,
                  tok,
                  pl.BlockSpec((hp, GRID_W, 1), lambda h, r: (h, r, 0)),
                  pl.BlockSpec((GRID_W, wd), lambda h, r: (r, G + h))],
        out_specs=[tok,
                   pl.BlockSpec((L, wd), lambda h, r: (0, h)),
                   pl.BlockSpec((L, wd), lambda h, r: (0, h)),
                   pl.BlockSpec((Lc, wd), lambda h, r: (0, h)),
                   pl.BlockSpec((Lc, wd), lambda h, r: (0, h)),
                   bia],
        out_shape=[jax.ShapeDtypeStruct((L, NA), BF16), jax.ShapeDtypeStruct((L, NA), F32), jax.ShapeDtypeStruct((L, NA), F32),
                   jax.ShapeDtypeStruct((Lc, NA), F32), jax.ShapeDtypeStruct((Lc, NA), F32),
                   jax.ShapeDtypeStruct(bias.shape, F32)],
        sem=("parallel", "arbitrary"), comm=comm)
    return outs if comm is None else (outs, couts)


def _s5_dims(T, N):
    TC = T // S5_SEG
    assert T % (S5_SEG * SUB * 2) == 0 and N % S5_STRIP == 0
    return TC, TC // SUB, S5_SEG, N // S5_STRIP


def _s5_backward(d, rev):
    return (d == 1) != rev


def s5_scan(name, xin, mats, a, rev, comm=None):
    _, T, W = xin.shape
    N = a.shape[-1]
    TC, NG, NCH, NS = _s5_dims(T, N)
    CW, SL = W // NS, S5_STRIP

    def ck(d, k):
        return jnp.where(_s5_backward(d, rev), NCH - 1 - k, k)

    def body(x_ref, m_ref, a_ref, h_ref, f_ref, carry, hs):
        @pl.when(pl.program_id(2) == 0)
        def _():
            carry[...] = jnp.zeros_like(carry)

        xb = x_ref[...].astype(BF16)
        hs[0] = jnp.dot(xb, m_ref[0], preferred_element_type=F32)
        hs[1] = jnp.dot(xb, m_ref[1], preferred_element_type=F32)
        ar, ai = jnp.broadcast_to(a_ref[0], (SUB, SL)), jnp.broadcast_to(a_ref[1], (SUB, SL))
        bw = _s5_backward(pl.program_id(0), rev)

        def step(t, c):
            hr, hi = c
            row = pl.multiple_of(jnp.where(bw, NG - 1 - t, t) * SUB, SUB)
            nr = ar * hr - ai * hi + hs[0, pl.ds(row, SUB), :]
            ni = ar * hi + ai * hr + hs[1, pl.ds(row, SUB), :]
            hs[0, pl.ds(row, SUB), :] = nr
            hs[1, pl.ds(row, SUB), :] = ni
            return nr, ni

        hr, hi = lax.fori_loop(0, NG, step, (carry[0], carry[1]))
        carry[0], carry[1] = hr, hi
        f_ref[0], f_ref[1] = hr, hi
        h_ref[...] = hs[...].astype(BF16)

    outs, couts = _call(
        body, [xin, mats, a], name=name, grid=(2, NS, NCH),
        in_specs=[pl.BlockSpec((None, TC, CW), lambda d, j, k: (d, ck(d, k), j)),
                  pl.BlockSpec((None, 2, None, CW, SL), lambda d, j, k: (d, 0, j, 0, 0)),
                  pl.BlockSpec((None, 2, 1, SL), lambda d, j, k: (d, 0, 0, j))],
        out_specs=[pl.BlockSpec((None, 2, TC, SL), lambda d, j, k: (d, 0, ck(d, k), j)),
                   pl.BlockSpec((None, 2, SUB, SL), lambda d, j, k: (d, 0, 0, j))],
        out_shape=[jax.ShapeDtypeStruct((2, 2, T, N), BF16), jax.ShapeDtypeStruct((2, 2, SUB, N), F32)],
        scratch_shapes=[pltpu.VMEM((2, SUB, SL), F32), pltpu.VMEM((2, TC, SL), F32)],
        sem=("parallel", "parallel", "arbitrary"), comm=comm)
    return outs if comm is None else (outs, couts)


def s5_fix(name, hloc, hin, a, mats, rev, comm=None):
    _, _, T, N = hloc.shape
    TC, NG, NCH, NS = _s5_dims(T, N)
    SL = S5_STRIP
    CW = mats.shape[-1]

    def ck(d, k):
        return jnp.where(_s5_backward(d, rev), NCH - 1 - k, k)

    def body(h_ref, hin_ref, a_ref, m_ref, ho_ref, y_ref, g, hs):
        @pl.when(pl.program_id(2) == 0)
        def _():
            g[...] = hin_ref[...]

        hs[...] = h_ref[...].astype(F32)
        ar, ai = jnp.broadcast_to(a_ref[0], (SUB, SL)), jnp.broadcast_to(a_ref[1], (SUB, SL))
        bw = _s5_backward(pl.program_id(0), rev)

        def step(t, c):
            gr, gi = c
            row = pl.multiple_of(jnp.where(bw, NG - 1 - t, t) * SUB, SUB)
            nr = ar * gr - ai * gi
            ni = ar * gi + ai * gr
            hs[0, pl.ds(row, SUB), :] += nr
            hs[1, pl.ds(row, SUB), :] += ni
            return nr, ni

        gr, gi = lax.fori_loop(0, NG, step, (g[0], g[1]))
        g[0], g[1] = gr, gi
        hb = hs[...].astype(BF16)
        ho_ref[...] = hb
        y_ref[...] = (jnp.dot(hb[0], m_ref[0], preferred_element_type=F32)
                      + jnp.dot(hb[1], m_ref[1], preferred_element_type=F32))

    outs, couts = _call(
        body, [hloc, hin, a, mats], name=name, grid=(2, NS, NCH),
        in_specs=[pl.BlockSpec((None, 2, TC, SL), lambda d, j, k: (d, 0, ck(d, k), j)),
                  pl.BlockSpec((None, 2, SUB, SL), lambda d, j, k: (d, 0, 0, j)),
                  pl.BlockSpec((None, 2, 1, SL), lambda d, j, k: (d, 0, 0, j)),
                  pl.BlockSpec((None, 2, None, SL, CW), lambda d, j, k: (d, 0, j, 0, 0))],
        out_specs=[pl.BlockSpec((None, 2, TC, SL), lambda d, j, k: (d, 0, ck(d, k), j)),
                   pl.BlockSpec((None, TC, CW), lambda d, j, k: (d, ck(d, k), j))],
        out_shape=[jax.ShapeDtypeStruct((2, 2, T, N), BF16), jax.ShapeDtypeStruct((2, T, NS * CW), F32)],
        scratch_shapes=[pltpu.VMEM((2, SUB, SL), F32), pltpu.VMEM((2, TC, SL), F32)],
        sem=("parallel", "parallel", "arbitrary"), comm=comm)
    return outs if comm is None else (outs, couts)


def s5_grads(name, g, h, u, dy, comm=None):
    _, _, T, N = g.shape
    W = u.shape[-1]
    TC, NG, NCH, NS = _s5_dims(T, N)
    CW, SL = W // NS, S5_STRIP

    def body(g_ref, h_ref, hp_ref, hl_ref, u_ref, dy_ref, dm_ref, dc_ref, da_ref, hs):
        k = pl.program_id(2)
        sub = lax.broadcasted_iota(jnp.int32, (SUB, SL), 0)

        hf = h_ref[...].astype(F32)

        @pl.when(pl.program_id(0) == 0)
        def _():
            for z in range(2):
                wrapped = jnp.where(sub == 0, 0.0, pltpu.roll(hl_ref[z].astype(F32)[SUB:], 1, 0))
                hs[z, 0:SUB, :] = jnp.where(k == 0, wrapped, hp_ref[z].astype(F32)[SUB:])
                hs[z, SUB:TC, :] = hf[z, 0:TC - SUB]

        @pl.when(pl.program_id(0) == 1)
        def _():
            for z in range(2):
                wrapped = jnp.where(sub == SUB - 1, 0.0, pltpu.roll(hl_ref[z].astype(F32)[:SUB], SUB - 1, 0))
                hs[z, TC - SUB:TC, :] = jnp.where(k == NCH - 1, wrapped, hp_ref[z].astype(F32)[:SUB])
                hs[z, 0:TC - SUB, :] = hf[z, SUB:TC]

        gr, gi, pr, pi = g_ref[0].astype(F32), g_ref[1].astype(F32), hs[0], hs[1]
        dar = jnp.sum((gr * pr + gi * pi).reshape(NG, SUB, SL), axis=0)
        dai = jnp.sum((gi * pr - gr * pi).reshape(NG, SUB, SL), axis=0)
        ub, dyb = u_ref[...].astype(BF16), dy_ref[...].astype(BF16)
        dm = [lax.dot_general(ub, g_ref[z], TN, preferred_element_type=F32) for z in range(2)]
        dc = [lax.dot_general(dyb, h_ref[z], TN, preferred_element_type=F32) for z in range(2)]

        @pl.when(k == 0)
        def _():
            da_ref[0], da_ref[1] = dar, dai
            for z in range(2):
                dm_ref[z], dc_ref[z] = dm[z], dc[z]

        @pl.when(k > 0)
        def _():
            da_ref[0] += dar
            da_ref[1] += dai
            for z in range(2):
                dm_ref[z] += dm[z]
                dc_ref[z] += dc[z]

    big = pl.BlockSpec((None, 2, TC, SL), lambda d, j, k: (d, 0, k, j))
    tok = pl.BlockSpec((None, TC, CW), lambda d, j, k: (d, k, j))
    mat = pl.BlockSpec((None, 2, None, CW, SL), lambda d, j, k: (d, 0, j, 0, 0))
    outs, couts = _call(
        body, [g, h, h, h, u, dy], name=name, grid=(2, NS, NCH),
        in_specs=[big, big,
                  pl.BlockSpec((None, 2, 2 * SUB, SL), lambda d, j, k: (
                      d, 0, jnp.where(d == 0, jnp.maximum(k * NG - 1, 0), jnp.minimum((k + 1) * NG, T // SUB - 1)) // 2, j)),
                  pl.BlockSpec((None, 2, 2 * SUB, SL), lambda d, j, k: (d, 0, jnp.where(d == 0, T // SUB - 1, 0) // 2, j)),
                  tok, tok],
        out_specs=[mat, mat, pl.BlockSpec((None, 2, SUB, SL), lambda d, j, k: (d, 0, 0, j))],
        out_shape=[jax.ShapeDtypeStruct((2, 2, NS, CW, SL), F32), jax.ShapeDtypeStruct((2, 2, NS, CW, SL), F32),
                   jax.ShapeDtypeStruct((2, 2, SUB, N), F32)],
        scratch_shapes=[pltpu.VMEM((2, TC, SL), F32)],
        sem=("parallel", "parallel", "arbitrary"), comm=comm)
    return outs if comm is None else (outs, couts)


def _interleave(seq):
    *lead, T, W = seq.shape
    n = len(lead)
    return seq.reshape(*lead, S5_SEG, T // S5_SEG, W).swapaxes(n, n + 1).reshape(*lead, T, W)


def _deinterleave(seq):
    *lead, T, W = seq.shape
    n = len(lead)
    return seq.reshape(*lead, T // S5_SEG, S5_SEG, W).swapaxes(n, n + 1).reshape(*lead, T, W)


def _s5_discretize(lam_re, lam_im, log_dt, b_re, b_im):
    dt = jnp.exp(log_dt)[..., None]
    mag = jnp.exp(lam_re * dt)
    a_re = mag * jnp.cos(lam_im * dt)
    a_im = mag * jnp.sin(lam_im * dt)
    den = jnp.square(lam_re) + jnp.square(lam_im)
    f_re = ((a_re - 1.0) * lam_re + a_im * lam_im) / den
    f_im = (a_im * lam_re - (a_re - 1.0) * lam_im) / den
    bb_re = f_re[..., None] * b_re - f_im[..., None] * b_im
    bb_im = f_re[..., None] * b_im + f_im[..., None] * b_re
    return a_re, a_im, bb_re, bb_im


_GPS = S5_STRIP // SSM_STATE


def _blockdiag(t):
    d2, G, P, Cg = t.shape
    t5 = t.reshape(d2, G // _GPS, _GPS, P, Cg).transpose(0, 1, 2, 4, 3)
    m = t5[:, :, :, :, None, :] * jnp.eye(_GPS, dtype=t.dtype)[None, None, :, None, :, None]
    return m.reshape(d2, G // _GPS, _GPS * Cg, _GPS * P)


def _blockdiag_extract(m, Cg, P):
    d2, NS = m.shape[:2]
    m6 = m.reshape(d2, NS, _GPS, Cg, _GPS, P)
    diag = jnp.stack([m6[:, :, i, :, i, :] for i in range(_GPS)], axis=2)
    return diag.transpose(0, 1, 2, 4, 3).reshape(d2, NS * _GPS, P, Cg)


def _cmul(a, b):
    return a[0] * b[0] - a[1] * b[1], a[0] * b[1] + a[1] * b[0]


def _cpow(a, n):
    out, base = None, a
    while n:
        if n & 1:
            out = base if out is None else _cmul(out, base)
        base = _cmul(base, base)
        n >>= 1
    return out


def _segment_carry(fin, apow, rev):
    per_dir = []
    for d in range(2):
        fr, fi = fin[d, 0], fin[d, 1]
        ap = (apow[0][d], apow[1][d])
        cr = ci = jnp.zeros_like(fr[0:1])
        outs = [None] * S5_SEG
        backward = (d == 1) != rev
        for s in (range(S5_SEG - 1, -1, -1) if backward else range(S5_SEG)):
            outs[s] = (cr, ci)
            pr, pi = _cmul(ap, (cr, ci))
            cr, ci = pr + fr[s:s + 1], pi + fi[s:s + 1]
        per_dir.append(jnp.stack([jnp.concatenate([o[0] for o in outs]), jnp.concatenate([o[1] for o in outs])]))
    return jnp.stack(per_dir)


def _coords():
    x, y, c = lax.axis_index("x"), lax.axis_index("y"), lax.axis_index("c")
    others = [(1 - x, y), (x, 1 - y), (1 - x, 1 - y)]
    return x, y, c, 2 * x + y, others


def _comm(name, ins, out_shapes, aliases, n_local, n_remote, plan):
    n_in, n_out = len(ins), len(out_shapes)

    def body(*refs):
        in_refs, out_refs = refs[:n_in], refs[n_in:n_in + n_out]
        send_sems, recv_sems, local_sems = refs[n_in + n_out:]
        x, y, c = lax.axis_index("x"), lax.axis_index("y"), lax.axis_index("c")
        locs, sends, lands = plan(in_refs, out_refs)
        assert len(locs) == n_local and len(sends) == n_remote and len(lands) == n_remote
        local = [pltpu.make_async_copy(s, d, local_sems.at[i]) for i, (s, d) in enumerate(locs)]
        for cp in local:
            cp.start()
        remote = [pltpu.make_async_remote_copy(src_ref=s, dst_ref=d, send_sem=send_sems.at[i], recv_sem=recv_sems.at[i],
                                               device_id=peer, device_id_type=MESH)
                  for i, (s, d, peer) in enumerate(sends)]
        for cp in remote:
            cp.start()
        for i, d in enumerate(lands):
            pltpu.make_async_remote_copy(src_ref=d, dst_ref=d, send_sem=send_sems.at[i], recv_sem=recv_sems.at[i],
                                         device_id=(x, y, c), device_id_type=MESH).wait_recv()
        for cp in remote:
            cp.wait_send()
        for cp in local:
            cp.wait()

    any_spec = pl.BlockSpec(memory_space=pl.ANY)
    return pl.pallas_call(
        body, name=name,
        in_specs=[any_spec] * n_in, out_specs=[any_spec] * n_out,
        out_shape=[jax.ShapeDtypeStruct(s, d) for s, d in out_shapes],
        input_output_aliases=aliases,
        scratch_shapes=[pltpu.SemaphoreType.DMA((n_remote,)), pltpu.SemaphoreType.DMA((n_remote,)),
                        pltpu.SemaphoreType.DMA((max(n_local, 1),))],
        compiler_params=pltpu.CompilerParams(has_side_effects=True),
    )(*ins)


def allgather_dev(name, v):
    M, Nc = v.shape

    def plan(in_refs, out_refs):
        (v_ref,), (o_ref,) = in_refs, out_refs
        x, y, c = lax.axis_index("x"), lax.axis_index("y"), lax.axis_index("c")

        def rows(px, py, pc):
            return o_ref.at[pl.ds((4 * px + 2 * py + pc) * M, M), :]

        peers = [(x ^ fx, y ^ fy, c ^ fc) for fx in (0, 1) for fy in (0, 1) for fc in (0, 1) if fx or fy or fc]
        return ([(v_ref, rows(x, y, c))],
                [(v_ref, rows(x, y, c), p) for p in peers],
                [rows(*p) for p in peers])

    return _comm(name, [v], [((N_DEV * M, Nc), v.dtype)], {}, 1, N_DEV - 1, plan)[0]


def allgather_chips_1(name, shards):
    def plan(in_refs, out_refs):
        x, y, c, chip, others = _coords()
        sends, lands = [], []
        for s_ref, g_ref in zip(in_refs, out_refs):
            hr = s_ref.shape[0] // 2
            mine = pl.ds(c * hr, hr)
            for qx, qy in others:
                sends.append((s_ref.at[mine], g_ref.at[chip, mine], (qx, qy, c)))
                lands.append(g_ref.at[2 * qx + qy, mine])
        return [], sends, lands

    n = len(shards)
    comm = (list(shards), [((N_CHIP,) + s.shape, s.dtype) for s in shards], {}, 3 * n, plan)
    return comm if name is None else _comm(name, comm[0], comm[1], comm[2], 0, comm[3], comm[4])


def allgather_chips_2(name, gathered, shards):
    n = len(gathered)

    def plan(in_refs, out_refs):
        x, y, c, chip, others = _coords()
        sends, lands = [], []
        for s_ref, g_ref in zip(in_refs[n:], out_refs):
            hr = g_ref.shape[1] // 2
            for qx, qy in others:
                q = 2 * qx + qy
                sends.append((g_ref.at[q, pl.ds(c * hr, hr)], g_ref.at[q, pl.ds(c * hr, hr)], (x, y, 1 - c)))
                lands.append(g_ref.at[q, pl.ds((1 - c) * hr, hr)])
            sends.append((s_ref, g_ref.at[chip], (x, y, 1 - c)))
            lands.append(g_ref.at[chip])
        return [], sends, lands

    comm = (list(gathered) + list(shards), [(g.shape, g.dtype) for g in gathered], {i: i for i in range(n)}, 4 * n, plan)
    return comm if name is None else _comm(name, comm[0], comm[1], comm[2], 0, comm[3], comm[4])


def reduce_1(name, grads):
    def plan(in_refs, out_refs):
        x, y, c, chip, others = _coords()
        sends, lands = [], []
        for g_ref, got_ref in zip(in_refs, out_refs):
            hr = g_ref.shape[1] // 2
            sends.append((g_ref.at[:, pl.ds((1 - c) * hr, hr), :], got_ref, (x, y, 1 - c)))
            lands.append(got_ref)
        return [], sends, lands

    n = len(grads)
    comm = (list(grads), [((g.shape[0], g.shape[1] // 2, g.shape[2]), g.dtype) for g in grads], {}, n, plan)
    return comm if name is None else _comm(name, comm[0], comm[1], comm[2], 0, comm[3], comm[4])


def _merge_comm(a, b):
    if a is None or b is None:
        return a if b is None else b
    na_in, na_out = len(a[0]), len(a[1])

    def plan(in_refs, out_refs):
        _, s1, l1 = a[4](in_refs[:na_in], out_refs[:na_out])
        _, s2, l2 = b[4](in_refs[na_in:], out_refs[na_out:])
        return [], s1 + s2, l1 + l2

    alias = dict(a[2])
    alias.update({na_in + i: na_out + j for i, j in b[2].items()})
    return (a[0] + b[0], a[1] + b[1], alias, a[3] + b[3], plan)


def reduce_2(name, parts):
    def plan(in_refs, out_refs):
        x, y, c, chip, others = _coords()
        sends, lands = [], []
        for t_ref, q_ref in zip(in_refs, out_refs):
            for qx, qy in others:
                sends.append((t_ref.at[2 * qx + qy], q_ref.at[chip], (qx, qy, c)))
                lands.append(q_ref.at[2 * qx + qy])
        return [], sends, lands

    n = len(parts)
    comm = (list(parts), [(p.shape, p.dtype) for p in parts], {}, 3 * n, plan)
    return comm if name is None else _comm(name, comm[0], comm[1], comm[2], 0, comm[3], comm[4])


def share_slots(name, slots):
    def plan(in_refs, out_refs):
        x, y, c, chip, others = _coords()
        (q_ref,) = out_refs
        return ([], [(q_ref.at[chip], q_ref.at[chip], (qx, qy, c)) for qx, qy in others],
                [q_ref.at[2 * qx + qy] for qx, qy in others])

    return _comm(name, [slots], [(slots.shape, slots.dtype)], {0: 0}, 0, N_CHIP - 1, plan)[0]


def allreduce_small(tag, buf, ids):
    got = reduce_1(tag + "_1", [buf[None]])[0][0]
    slots = share_slots(tag + "_2", pair_sum_to_slot(tag + "_add", buf, got, ids))
    full = reduce_3(tag + "_3", [sum_chips_to_half(tag + "_sum", slots, ids)])[0]
    return full.reshape(buf.shape)


def reduce_3(name, fulls):
    def plan(in_refs, out_refs):
        x, y, c, chip, others = _coords()
        sends, lands = [], []
        for o_ref in out_refs:
            sends.append((o_ref.at[c], o_ref.at[c], (x, y, 1 - c)))
            lands.append(o_ref.at[1 - c])
        return [], sends, lands

    n = len(fulls)
    return _comm(name, fulls, [(f.shape, f.dtype) for f in fulls], {i: i for i in range(n)}, 0, n, plan)


_WEIGHTS = ['c_ctx', 'w_mod', 'b_mod', 'g_mix', 'g_ffn', 'w_in', 'ssm_lam_re', 'ssm_lam_im', 'ssm_log_dt', 'ssm_b_re',
            'ssm_b_im', 'ssm_c_re', 'ssm_c_im', 'ssm_d', 'ssm_w_glu', 'na_rpb', 'w_out', 'cv_w_pw1', 'cv_dw_w', 'cv_dw_b',
            'cv_ln_g', 'cv_ln_b', 'cv_w_pw2', 'ffn_w_up', 'ffn_conv_w', 'ffn_conv_b', 'ffn_w_down', 'g_out']
_INPUTS = ['x', 'c', 'ctx'] + _WEIGHTS + ['loss_target'] + ['m_' + w for w in _WEIGHTS] + ['v_' + w for w in _WEIGHTS]
_GATHERED_SMALL = ['ffn_conv_w', 'cv_dw_w', 'cv_dw_b', 'cv_ln_g', 'cv_ln_b']


def _silu(v):
    return v * jax.nn.sigmoid(v)


def _pick_index(t, idx, axis):
    shape = [1] * t.ndim
    shape[axis] = t.shape[axis]
    mask = (jnp.arange(t.shape[axis]) == idx).reshape(shape)
    return jnp.sum(jnp.where(mask, t, jnp.zeros((), t.dtype)), axis=axis)


def _pack(arrs, cols, row_mult=SUB):
    flat = jnp.concatenate([a.reshape(-1).astype(F32) for a in arrs])
    n = flat.shape[0]
    unit = row_mult * cols
    flat = jnp.pad(flat, (0, (-n) % unit))
    return flat.reshape(-1, cols)


def _unpack(buf, shapes):
    flat = buf.reshape(-1)
    out, o = [], 0
    for s in shapes:
        n = int(np.prod(s))
        out.append(flat[o:o + n].reshape(s))
        o += n
    return out


def _carried(res, comm):
    return res if comm is not None else (res, [])


def _ffn_fwd(tag, xin, sh, sc, gt, g, wup, cw3, cb3, wdn, comm_up=None, comm_mid=None, comm_down=None):
    hf = norm_mod_fwd(tag + "_norm", xin, g * (1.0 + sc), sh)
    up3, got_up = _carried(mm_nn_pieces(tag + "_up", hf, wup, 0, N_CHIP, BF16, halves=2, comm=comm_up), comm_up)
    comm_mid = comm_mid(got_up) if callable(comm_mid) else comm_mid
    act, got_mid = _carried(ffn_mid_fwd(tag + "_mid", up3, cw3, cb3, comm=comm_mid), comm_mid)
    comm_down = comm_down(got_mid) if callable(comm_down) else comm_down
    yf, got_down = _carried(mm_nn(tag + "_down", act, wdn, BF16, comm=comm_down), comm_down)
    return gate_res_fwd(tag + "_res", xin, yf, gt), (xin, hf, up3, act, yf), got_up, got_mid, got_down


def _ffn_bwd(tag, dxo, saved, sc, gt, g, wup, cw3, cb3, wdn, comm_down=None, comm_mid=None, comm_up=None):
    xin, hf, up3, act, yf = saved
    dyf, dgt = gate_res_bwd(tag + "_res_b", dxo, yf, gt)
    dact, got_down = _carried(mm_nt(tag + "_down_bx", dyf, wdn, BF16, comm=comm_down), comm_down)
    dwdn = mm_tn(tag + "_down_bw", act, dyf, BF16)
    comm_mid = comm_mid(got_down) if callable(comm_mid) else comm_mid
    (dup3, dcw3, dcb3), got_mid = _carried(ffn_mid_bwd(tag + "_mid_b", up3, dact, cw3, cb3, comm=comm_mid), comm_mid)
    dhf, got_up = _carried(mm_nt_pieces(tag + "_up_bx", dup3, wup, BF16, halves=2, comm=comm_up), comm_up)
    dwup = mm_tn_pieces(tag + "_up_bw", hf, dup3, N_CHIP, BF16, halves=2)
    dxi, cs1, cs2 = norm_mod_bwd(tag + "_norm_b", xin, dhf, g * (1.0 + sc), dxo)
    return dxi, dict(dsh=cs1[0], dsc=cs2[0] * g, dgt=dgt[0], dg=cs2[0] * (1.0 + sc), dwup=dwup, dwdn=dwdn,
                     dcw=dcw3.transpose(1, 0, 2).reshape(3, -1), dcb=dcb3.reshape(-1)), got_down, got_mid, got_up


def kernel(x, c, ctx, c_ctx, w_mod, b_mod, g_mix, g_ffn, w_in, ssm_lam_re, ssm_lam_im, ssm_log_dt, ssm_b_re, ssm_b_im, ssm_c_re, ssm_c_im, ssm_d, ssm_w_glu, na_rpb, w_out, cv_w_pw1, cv_dw_w, cv_dw_b, cv_ln_g, cv_ln_b, cv_w_pw2, ffn_w_up, ffn_conv_w, ffn_conv_b, ffn_w_down, g_out, loss_target, m_c_ctx, m_w_mod, m_b_mod, m_g_mix, m_g_ffn, m_w_in, m_ssm_lam_re, m_ssm_lam_im, m_ssm_log_dt, m_ssm_b_re, m_ssm_b_im, m_ssm_c_re, m_ssm_c_im, m_ssm_d, m_ssm_w_glu, m_na_rpb, m_w_out, m_cv_w_pw1, m_cv_dw_w, m_cv_dw_b, m_cv_ln_g, m_cv_ln_b, m_cv_w_pw2, m_ffn_w_up, m_ffn_conv_w, m_ffn_conv_b, m_ffn_w_down, m_g_out, v_c_ctx, v_w_mod, v_b_mod, v_g_mix, v_g_ffn, v_w_in, v_ssm_lam_re, v_ssm_lam_im, v_ssm_log_dt, v_ssm_b_re, v_ssm_b_im, v_ssm_c_re, v_ssm_c_im, v_ssm_d, v_ssm_w_glu, v_na_rpb, v_w_out, v_cv_w_pw1, v_cv_dw_w, v_cv_dw_b, v_cv_ln_g, v_cv_ln_b, v_cv_w_pw2, v_ffn_w_up, v_ffn_conv_w, v_ffn_conv_b, v_ffn_w_down, v_g_out):
    p = dict(locals())
    xi, yi, ci = lax.axis_index("x"), lax.axis_index("y"), lax.axis_index("c")
    me, chip = 4 * xi + 2 * yi + ci, 2 * xi + yi
    xs, cx, tgt = x[0], ctx[0], loss_target[0]
    L, D = xs.shape
    Lc = cx.shape[0]
    T = L + Lc
    W = D // 2
    Cq = w_mod.shape[2]

    s_mix = [t.astype(BF16) for t in (w_in[0], ssm_w_glu[0], w_out[0])]
    s_ffn0 = [t.astype(BF16) for t in (ffn_w_up[0], ffn_w_down[0])]
    s_conv = [t.astype(BF16) for t in (cv_w_pw1[0], cv_w_pw2[0])]
    s_ffn1 = [t.astype(BF16) for t in (ffn_w_up[1], ffn_w_down[1])]
    (Win,) = allgather_chips_2("gather_win_2", allgather_chips_1("gather_win_1", s_mix[:1]), s_mix[:1])
    Fd = ffn_w_down.shape[1] * N_CHIP
    c_idx = jnp.reshape(ci, (1,)).astype(jnp.int32)
    ids = jnp.stack([chip, ci]).astype(jnp.int32)

    def added(tag, grads, got):
        return [add_half("reduce_%s_add%d" % (tag, i), g, r, c_idx) for i, (g, r) in enumerate(zip(grads, got))]

    small_shapes = [p[n].shape for n in _GATHERED_SMALL]
    sm = allgather_dev("gather_small", _pack([p[n] for n in _GATHERED_SMALL], 1024))
    sm = sm.reshape(N_DEV, -1)[0::2]
    per_chip = [_unpack(sm[q], small_shapes) for q in range(N_CHIP)]
    conv_w_f, dw_w_f, dw_b_f, ln_g_f, ln_b_f = (jnp.concatenate([pc[i] for pc in per_chip], axis=-1)
                                                for i in range(len(_GATHERED_SMALL)))
    cw3 = [conv_w_f[l].reshape(3, 2, Fd).transpose(1, 0, 2) for l in range(2)]
    cb3 = [ffn_conv_b[l].reshape(2, 1, Fd) for l in range(2)]
    dw_w_f, dw_b_f, ln_g_f, ln_b_f = dw_w_f[0], dw_b_f[0], ln_g_f[0], ln_b_f[0]

    c_all = allgather_dev("gather_c", jnp.zeros((SUB, D), F32).at[0].set(c[0])).reshape(N_DEV, SUB, D)[:, 0]
    S16 = jnp.concatenate([_silu(c_all), _silu(c_ctx)[None], jnp.zeros((2 * SUB - N_DEV - 1, D), F32)])
    modp = mm_nn_pieces("mod_fwd", S16, w_mod, 0, 2, F32)
    modg = allgather_dev("gather_mod", modp).reshape(N_DEV, 2 * SUB, 2, Cq)[0::2]

    def mod_row(r):
        return r.transpose(1, 0, 2).reshape(2, N_CHIP * Cq) + b_mod

    mod_me = mod_row(_pick_index(modg, me, 1))
    mod_c = mod_row(modg[:, N_DEV])
    mods = [[mod_me[l, i * D:(i + 1) * D] for i in range(N_MOD)] for l in range(2)]
    shc, scc = mod_c[0, :D], mod_c[0, D:2 * D]

    sh_m, sc_m, gt_m, sh_f, sc_f, gt_f = mods[0]
    h0 = norm_mod_fwd("l0_norm", xs, g_mix[0] * (1.0 + sc_m), sh_m)
    hc0 = norm_mod_fwd("l0_norm_c", cx, g_mix[0] * (1.0 + scc), shc)
    u = mm_nn_pieces("l0_in_u", h0, Win, 0, 1, F32)
    qkv, g_mix1 = mm_nn_pieces("l0_in_qkv", h0, Win, 1, 3, BF16, comm=allgather_chips_1(None, s_mix[1:]))
    uc = mm_nn_pieces("l0_in_uc", hc0, Win, 0, 1, F32)
    kvc = mm_nn_pieces("l0_in_kvc", hc0, Win, 2, 2, BF16)

    lam_re, lam_im, log_dt = ssm_lam_re[0], ssm_lam_im[0], ssm_log_dt[0]
    b_re, b_im, c_re, c_im = ssm_b_re[0], ssm_b_im[0], ssm_c_re[0], ssm_c_im[0]
    (a_re, a_im, bb_re, bb_im), disc_vjp = jax.vjp(_s5_discretize, lam_re, lam_im, log_dt, b_re, b_im)
    G, P, Cg = bb_re.shape[1:]
    N = G * P
    a_re, a_im = a_re.reshape(2, 1, N), a_im.reshape(2, 1, N)
    a_f, a_b = jnp.stack([a_re, a_im], axis=1), jnp.stack([a_re, -a_im], axis=1)
    Bblk = jnp.stack([_blockdiag(bb_re), _blockdiag(bb_im)], axis=1)
    Cblk = jnp.stack([_blockdiag(c_re.swapaxes(-1, -2)), -_blockdiag(c_im.swapaxes(-1, -2))], axis=1)
    apow = _cpow((a_re, a_im), T // S5_SEG)

    useq = _interleave(jnp.stack([jnp.concatenate([uc, u]), jnp.concatenate([u, uc])]).astype(BF16))
    (hloc, fin), (Wglu, Wout) = s5_scan("s5_scan", useq, Bblk.astype(BF16), a_f, rev=False,
                                        comm=allgather_chips_2(None, g_mix1, s_mix[1:]))
    Wglu, Wout = Wglu.reshape(-1, Wglu.shape[-1]), Wout.reshape(-1, D)
    (hst, yseq), g_dn0 = s5_fix("s5_fix", hloc, _segment_carry(fin, apow, False), a_f, Cblk.swapaxes(-1, -2).astype(BF16),
                                rev=False, comm=allgather_chips_1(None, s_ffn0[1:]))
    ys = _deinterleave(yseq)
    y0, y1 = ys[0, Lc:], ys[1, :L]
    s5o = glu_fwd("s5_glu", u, y0, y1, ssm_d[0], Wglu)

    bias = na_bias(na_rpb[0])
    (o_na, lse), (g_up0, Wdn0) = natten_fwd(
        "na_fwd", qkv, kvc, bias,
        comm=_merge_comm(allgather_chips_1(None, s_ffn0[:1]), allgather_chips_2(None, g_dn0, s_ffn0[1:])))
    mixcat = jnp.concatenate([s5o, o_na], axis=1)
    ymix, (Wup0,) = mm_nn("l0_out", mixcat, Wout, BF16, comm=allgather_chips_2(None, [g_up0], s_ffn0[:1]))
    x1 = gate_res_fwd("l0_res", xs, ymix, gt_m)
    Wdn0 = Wdn0.reshape(-1, D)
    x2, ffn0, (g_up1,), (g_dn1, Wup1), (g_pw1, g_pw2, Wdn1) = _ffn_fwd(
        "f0", x1, sh_f, sc_f, gt_f, g_ffn[0], Wup0, cw3[0], cb3[0], Wdn0,
        comm_up=allgather_chips_1(None, s_ffn1[:1]),
        comm_mid=lambda got_up: _merge_comm(allgather_chips_1(None, s_ffn1[1:]), allgather_chips_2(None, got_up, s_ffn1[:1])),
        comm_down=lambda got_mid: _merge_comm(allgather_chips_1(None, s_conv), allgather_chips_2(None, got_mid[:1], s_ffn1[1:])))
    Wpw1, Wpw2 = allgather_chips_2("gather_conv_2", [g_pw1, g_pw2], s_conv)
    Wpw2 = Wpw2.reshape(-1, D)
    Wup, Wdn = [Wup0, Wup1], [Wdn0.reshape(-1, D), Wdn1.reshape(-1, D)]

    sh_v, sc_v, gt_v, sh_g, sc_g, gt_g = mods[1]
    hcv = norm_mod_fwd("l1_norm", x2, g_mix[1] * (1.0 + sc_v), sh_v)
    ag3 = mm_nn_pieces("l1_pw1", hcv, Wpw1, 0, N_CHIP, BF16, halves=2)
    z1, z3 = conf_mid_fwd("l1_mid", ag3, dw_w_f, dw_b_f, ln_g_f, ln_b_f)
    ycv = mm_nn("l1_pw2", z3, Wpw2, BF16)
    x3 = gate_res_fwd("l1_res", x2, ycv, gt_v)
    x4, ffn1, _, _, _ = _ffn_fwd("f1", x3, sh_g, sc_g, gt_g, g_ffn[1], Wup[1], cw3[1], cb3[1], Wdn[1])

    dx4, dg_out, loss_part = loss_head("loss", x4, g_out, tgt)
    loss = lax.psum(loss_part[0, 0], ("x", "y", "c"))

    dx3, gf1, _, _, _ = _ffn_bwd("f1", dx4, ffn1, sc_g, gt_g, g_ffn[1], Wup[1], cw3[1], cb3[1], Wdn[1])
    g_up1, g_dn1 = [gf1["dwup"]], [gf1["dwdn"].reshape(N_CHIP, -1, D)]
    dycv, dgt_v = gate_res_bwd("l1_res_b", dx3, ycv, gt_v)
    dz3, got = mm_nt("l1_pw2_bx", dycv, Wpw2, BF16, comm=reduce_1(None, g_up1))
    parts_up1 = added("up1", g_up1, got)
    dWpw2, got = mm_tn("l1_pw2_bw", z3, dycv, BF16, comm=reduce_1(None, g_dn1))
    parts_dn1 = added("dn1", g_dn1, got)
    dz1, dln_g, dln_b = conf_ln_bwd("l1_ln_b", z1, dz3, ln_g_f, ln_b_f)
    (dag3, ddw_w, ddw_b), slots_dn1 = conf_conv_bwd("l1_conv_b", ag3, dz1, dw_w_f, comm=reduce_2(None, parts_dn1))
    dhcv = mm_nt_pieces("l1_pw1_bx", dag3, Wpw1, BF16, halves=2)
    dWpw1 = mm_tn_pieces("l1_pw1_bw", hcv, dag3, N_CHIP, BF16, halves=2)
    dx2, cs1_v, cs2_v = norm_mod_bwd("l1_norm_b", x2, dhcv, g_mix[1] * (1.0 + sc_v), dx3)
    g_conv = [dWpw1, dWpw2.reshape(N_CHIP, -1, D)]

    held = {}

    def conv_stage_2(got_down):
        held["parts_conv"] = added("conv", g_conv, got_down)
        return reduce_2(None, held["parts_conv"])

    dx1, gf0, _, slots_conv, slots_up1 = _ffn_bwd(
        "f0", dx2, ffn0, sc_f, gt_f, g_ffn[0], Wup[0], cw3[0], cb3[0], Wdn[0],
        comm_down=reduce_1(None, g_conv), comm_mid=conv_stage_2, comm_up=reduce_2(None, parts_up1))
    parts_conv = held["parts_conv"]
    g_up0, g_dn0 = [gf0["dwup"]], [gf0["dwdn"].reshape(N_CHIP, -1, D)]
    dymix, dgt_m = gate_res_bwd("l0_res_b", dx1, ymix, gt_m)
    dmix, got = mm_nt("l0_out_bx", dymix, Wout, BF16, comm=reduce_1(None, g_up0))
    parts_up0 = added("up0", g_up0, got)
    dWout, got = mm_tn("l0_out_bw", mixcat, dymix, BF16, comm=reduce_1(None, g_dn0))
    parts_dn0 = added("dn0", g_dn0, got)
    (dq, dk, dv, dkc, dvc, dbias), slots_up0 = natten_bwd("na_bwd", qkv, kvc, bias, o_na, lse, dmix,
                                                          comm=reduce_2(None, parts_up0))
    dy, zg, dzz, dd_skip = glu_bwd("s5_glu_b", u, y0, y1, ssm_d[0], Wglu, dmix)
    dWglu = mm_tn("s5_glu_bw", zg, dzz, BF16)
    g_mix2 = [dWglu.reshape(N_CHIP, -1, W), dWout.reshape(N_CHIP, -1, D)]

    zc = jnp.zeros((Lc, W), F32)
    dyseq = _interleave(jnp.stack([jnp.concatenate([zc, dy]), jnp.concatenate([dy, zc])]).astype(BF16))
    (gloc, gfin), slots_dn0 = s5_scan("s5_scan_b", dyseq, Cblk.astype(BF16), a_b, rev=True, comm=reduce_2(None, parts_dn0))
    apow_b = (apow[0], -apow[1])
    (gst, duseq), got = s5_fix("s5_fix_b", gloc, _segment_carry(gfin, apow_b, True), a_b, Bblk.swapaxes(-1, -2).astype(BF16),
                               rev=True, comm=reduce_1(None, g_mix2))
    parts_mix2 = added("mix2", g_mix2, got)
    (dBm, dCm, da8), slots_mix2 = s5_grads("s5_grads", gst, hst, useq, dyseq, comm=reduce_2(None, parts_mix2))
    dus = _deinterleave(duseq)
    du = fma3("s5_du", dy, dus[0, Lc:], dus[1, :L], ssm_d[0], BF16)
    duc = dus[0, :Lc] + dus[1, L:]

    d_in = [du, dq, dk, dv]
    d_in_c = [duc, jnp.zeros((Lc, W), BF16), dkc, dvc]
    dh0 = mm_nt_list("l0_in_bx", d_in, Win, BF16)
    dhc0 = mm_nt_list("l0_in_bxc", d_in_c, Win, BF16)
    h_all = jnp.concatenate([hc0, h0])
    dWin = jnp.stack([mm_tn("l0_in_bw%d" % q, h_all, jnp.concatenate([dc.astype(BF16), dl.astype(BF16)]), BF16)
                      for q, (dc, dl) in enumerate(zip(d_in_c, d_in))])
    dx0, cs1_m, cs2_m = norm_mod_bwd("l0_norm_b", xs, dh0, g_mix[0] * (1.0 + sc_m), dx1)
    _, cs1_c, cs2_c = norm_mod_bwd("l0_norm_bc", cx, dhc0, g_mix[0] * (1.0 + scc), jnp.zeros_like(cx))

    dmod0 = jnp.concatenate([cs1_m[0], cs2_m[0] * g_mix[0], dgt_m[0], gf0["dsh"], gf0["dsc"], gf0["dgt"]])
    dmod1 = jnp.concatenate([cs1_v[0], cs2_v[0] * g_mix[1], dgt_v[0], gf1["dsh"], gf1["dsc"], gf1["dgt"]])
    dmodc = jnp.concatenate([cs1_c[0], cs2_c[0] * g_mix[0], jnp.zeros((4 * D,), F32)])
    dm_rows = jnp.concatenate([jnp.stack([dmod0, dmod1, dmodc]), jnp.zeros((SUB - 3, N_MOD * D), F32)])
    dm_all = allgather_dev("gather_dmod", dm_rows).reshape(N_DEV, SUB, N_MOD * D)
    dm_sum = sum_lead("sum_dmod", dm_all, F32)
    pad7 = jnp.zeros((2 * SUB - N_DEV - 1, N_MOD * D), F32)
    dMod = [jnp.concatenate([dm_all[:, 0], dm_sum[2:3], pad7]), jnp.concatenate([dm_all[:, 1], jnp.zeros_like(dm_sum[2:3]), pad7])]
    dMod_cols = [_pick_index(m.reshape(m.shape[0], N_CHIP, Cq), chip, 1) for m in dMod]
    g_w_mod = jnp.stack([mm_tn("mod_bw%d" % l, S16, dMod_cols[l], F32) for l in range(2)])
    g_b_mod = jnp.stack([dm_sum[0] + dm_sum[2], dm_sum[1]])
    ds_part = mm_nt("mod_bx", dMod_cols[0], w_mod[0], F32)
    ds_all = allgather_dev("gather_dsc", jnp.zeros((SUB, D), F32).at[0].set(ds_part[N_DEV]))
    ds_c = sum_lead("sum_dsc", ds_all.reshape(N_DEV, SUB, D)[0::2], F32)[0]
    sg_c = jax.nn.sigmoid(c_ctx)
    g_c_ctx = ds_c * sg_c * (1.0 + c_ctx * (1.0 - sg_c))

    g_rpb_loc = na_bias_grad(dbias)

    dbb = [_blockdiag_extract(dBm[:, z], Cg, P) for z in range(2)]
    dcc = [_blockdiag_extract(dCm[:, z], Cg, P).swapaxes(-1, -2) for z in range(2)]
    da = jnp.sum(da8, axis=2).reshape(2, 2, G, P)
    small = {
        "g_mix": jnp.stack([cs2_m[0] * (1.0 + sc_m) + cs2_c[0] * (1.0 + scc), cs2_v[0] * (1.0 + sc_v)]),
        "g_ffn": jnp.stack([gf0["dg"], gf1["dg"]]),
        "a_re": da[:, 0], "a_im": da[:, 1], "bb_re": dbb[0], "bb_im": dbb[1], "c_re": dcc[0], "c_im": -dcc[1],
        "ssm_d": dd_skip, "na_rpb": g_rpb_loc, "cv_dw_w": ddw_w, "cv_dw_b": ddw_b, "cv_ln_g": dln_g, "cv_ln_b": dln_b,
        "ffn_conv_w": jnp.stack([gf0["dcw"], gf1["dcw"]]), "ffn_conv_b": jnp.stack([gf0["dcb"], gf1["dcb"]]),
        "g_out": dg_out,
    }
    skeys = list(small)
    sbuf = _pack([small[k] for k in skeys], 1024, 4 * SUB)
    ssum = dict(zip(skeys, _unpack(allreduce_small("reduce_small", sbuf, ids), [small[k].shape for k in skeys])))
    g_lam_re, g_lam_im, g_log_dt, g_b_re, g_b_im = disc_vjp((ssum["a_re"], ssum["a_im"], ssum["bb_re"], ssum["bb_im"]))

    def my_cols(t):
        n = t.shape[-1] // N_CHIP
        return _pick_index(t.reshape(t.shape[:-1] + (N_CHIP, n)), chip, t.ndim - 1)

    delta, new_m, new_v = {}, {}, {}
    parts_win = added("win", [dWin], reduce_1("reduce_win_1", [dWin]))
    (delta["w_mod"], new_m["w_mod"], new_v["w_mod"]), slots_win = adamw(
        "adamw_w_mod", w_mod, g_w_mod, m_w_mod, v_w_mod, comm=reduce_2(None, parts_win))
    parts = parts_win + parts_mix2 + parts_conv + parts_up0 + parts_dn0 + parts_up1 + parts_dn1
    slots = [*slots_win, *slots_mix2, *slots_conv, *slots_up0, *slots_dn0, *slots_up1, *slots_dn1]
    fulls = [sum_slots("reduce_sum_%d" % i, s, t, ids) for i, (s, t) in enumerate(zip(slots, parts))]
    full = [f.reshape(-1, f.shape[-1]) for f in reduce_3("reduce_g_3", fulls)]
    gWin, gWglu, gWout, gWpw1, gWpw2, gWup0, gWdn0, gWup1, gWdn1 = full

    grads = {
        "c_ctx": g_c_ctx, "w_mod": g_w_mod, "b_mod": g_b_mod, "g_mix": ssum["g_mix"], "g_ffn": ssum["g_ffn"],
        "w_in": gWin[None], "ssm_lam_re": g_lam_re[None], "ssm_lam_im": g_lam_im[None], "ssm_log_dt": g_log_dt[None],
        "ssm_b_re": g_b_re[None], "ssm_b_im": g_b_im[None], "ssm_c_re": ssum["c_re"][None], "ssm_c_im": ssum["c_im"][None],
        "ssm_d": ssum["ssm_d"], "ssm_w_glu": gWglu[None], "na_rpb": ssum["na_rpb"][None], "w_out": gWout[None],
        "cv_w_pw1": gWpw1[None], "cv_dw_w": my_cols(ssum["cv_dw_w"])[None], "cv_dw_b": my_cols(ssum["cv_dw_b"]),
        "cv_ln_g": my_cols(ssum["cv_ln_g"]), "cv_ln_b": my_cols(ssum["cv_ln_b"]), "cv_w_pw2": gWpw2[None],
        "ffn_w_up": jnp.stack([gWup0, gWup1]), "ffn_conv_w": my_cols(ssum["ffn_conv_w"]), "ffn_conv_b": ssum["ffn_conv_b"],
        "ffn_w_down": jnp.stack([gWdn0, gWdn1]), "g_out": ssum["g_out"][0],
    }
    grads = {k: grads[k].reshape(p[k].shape) for k in _WEIGHTS}

    large = [k for k in _WEIGHTS if p[k].size >= (1 << 18) or k == "w_mod"]
    tiny = [k for k in _WEIGHTS if k not in large]
    for k in large:
        if k != "w_mod":
            delta[k], new_m[k], new_v[k] = adamw("adamw_" + k, p[k], grads[k], p["m_" + k], p["v_" + k])
    packs = [_pack([src[pre + k] for k in tiny], 1024) for src, pre in ((p, ""), (grads, ""), (p, "m_"), (p, "v_"))]
    outs = adamw("adamw_small", *packs)
    shapes = [p[k].shape for k in tiny]
    for dst, buf in zip((delta, new_m, new_v), outs):
        dst.update(zip(tiny, _unpack(buf, shapes)))

    return (loss, dx0[None], *[grads[k] for k in _WEIGHTS], *[delta[k] for k in _WEIGHTS],
            *[new_m[k] for k in _WEIGHTS], *[new_v[k] for k in _WEIGHTS])
```

```python
import functools
import math

import numpy as np
import jax
import jax.numpy as jnp
from jax import lax
from jax.experimental import pallas as pl
from jax.experimental.pallas import tpu as pltpu

F32, BF16 = jnp.float32, jnp.bfloat16
MESH = pl.DeviceIdType.MESH
V7X_VMEM_LIMIT = 56 << 20
LANE, SUB = 128, 8
N_CHIP, N_DEV = 4, 8

GRID_W = 64
N_MOD = 6
SSM_GROUP, SSM_STATE = 16, 64
NA_HEAD_DIM, NA_WIN_R, NA_WIN_C = 128, 8, 16
EPS = 1e-6
NEG = -1e30
ADAM_LR, ADAM_B1, ADAM_B2, ADAM_EPS, ADAM_WD, ADAM_STEP = 0.001, 0.9, 0.999, 1e-08, 0.01, 10
S5_STRIP = 512
S5_SEG = 8

NN = (((1,), (0,)), ((), ()))
NT = (((1,), (1,)), ((), ()))
TN = (((0,), (0,)), ((), ()))


def _params(*sem, side_effects=False):
    return pltpu.CompilerParams(dimension_semantics=sem if sem else None, vmem_limit_bytes=V7X_VMEM_LIMIT,
                                has_side_effects=side_effects)


def _call(body, args, *, name, grid, in_specs, out_specs, out_shape, sem, scratch_shapes=(), comm=None):
    out_specs, out_shape, scratch_shapes = list(out_specs), list(out_shape), list(scratch_shapes)
    if comm is None:
        outs = pl.pallas_call(body, name=name, grid=grid, in_specs=list(in_specs), out_specs=out_specs, out_shape=out_shape,
                              scratch_shapes=scratch_shapes, compiler_params=_params(*sem))(*args)
        return list(outs), []
    c_args, c_shapes, c_alias, n_remote, plan = comm
    n_in, n_out, n_ci, n_co, n_sc = len(args), len(out_shape), len(c_args), len(c_shapes), len(scratch_shapes)

    def wrapped(*refs):
        ins, cins = refs[:n_in], refs[n_in:n_in + n_ci]
        o0 = n_in + n_ci
        outs, couts = refs[o0:o0 + n_out], refs[o0 + n_out:o0 + n_out + n_co]
        s0 = o0 + n_out + n_co
        scr, (send_sems, recv_sems) = refs[s0:s0 + n_sc], refs[s0 + n_sc:]
        pids = [pl.program_id(a) for a in range(len(grid))]
        first = functools.reduce(jnp.logical_and, [q == 0 for q in pids])
        last = functools.reduce(jnp.logical_and, [q == g - 1 for q, g in zip(pids, grid)])
        me = (lax.axis_index("x"), lax.axis_index("y"), lax.axis_index("c"))

        def copies():
            _, sends, lands = plan(cins, couts)
            assert len(sends) == n_remote and len(lands) == n_remote
            out = [pltpu.make_async_remote_copy(src_ref=s, dst_ref=d, send_sem=send_sems.at[i], recv_sem=recv_sems.at[i],
                                                device_id=peer, device_id_type=MESH) for i, (s, d, peer) in enumerate(sends)]
            arrivals = [pltpu.make_async_remote_copy(src_ref=d, dst_ref=d, send_sem=send_sems.at[i], recv_sem=recv_sems.at[i],
                                                     device_id=me, device_id_type=MESH) for i, d in enumerate(lands)]
            return out, arrivals

        @pl.when(first)
        def _():
            for cp in copies()[0]:
                cp.start()

        body(*ins, *outs, *scr)

        @pl.when(last)
        def _():
            out, arrivals = copies()
            for cp in arrivals:
                cp.wait_recv()
            for cp in out:
                cp.wait_send()

    any_spec = pl.BlockSpec(memory_space=pl.ANY)
    res = pl.pallas_call(
        wrapped, name=name, grid=grid,
        in_specs=[*in_specs, *[any_spec] * n_ci], out_specs=[*out_specs, *[any_spec] * n_co],
        out_shape=[*out_shape, *[jax.ShapeDtypeStruct(s, d) for s, d in c_shapes]],
        input_output_aliases={n_in + i: n_out + j for i, j in c_alias.items()},
        scratch_shapes=[*scratch_shapes, pltpu.SemaphoreType.DMA((n_remote,)), pltpu.SemaphoreType.DMA((n_remote,))],
        compiler_params=_params(*["arbitrary"] * len(grid), side_effects=True),
    )(*args, *c_args)
    return list(res[:n_out]), list(res[n_out:])


def _pick(n, pref, mult=LANE):
    if n <= pref:
        return n
    best = None
    for t in range(mult, pref + 1, mult):
        if n % t == 0:
            best = t
    assert best is not None, (n, pref, mult)
    return best


def _sigmoid(x):
    return 1.0 / (1.0 + jnp.exp(-x))


def _mm(name, a, b, *, dims, grid, a_spec, b_spec, o_spec, out_shape, out_dtype, acc_shape, exact=False, comm=None):
    nk = grid[2]

    def body(a_ref, b_ref, o_ref, *scratch):
        if exact:
            part = lax.dot_general(a_ref[...], b_ref[...], dims, preferred_element_type=F32,
                                   precision=lax.Precision.HIGHEST)
        else:
            part = lax.dot_general(a_ref[...].astype(BF16), b_ref[...].astype(BF16), dims,
                                   preferred_element_type=F32)
        if nk == 1:
            o_ref[...] = part.astype(o_ref.dtype)
        else:
            acc = scratch[0]
            kk = pl.program_id(2)

            @pl.when(kk == 0)
            def _():
                acc[...] = part

            @pl.when(kk > 0)
            def _():
                acc[...] += part

            @pl.when(kk == nk - 1)
            def _():
                o_ref[...] = acc[...].astype(o_ref.dtype)

    outs, couts = _call(body, [a, b], name=name, grid=grid, in_specs=[a_spec, b_spec], out_specs=[o_spec],
                        out_shape=[jax.ShapeDtypeStruct(out_shape, out_dtype)],
                        scratch_shapes=[] if nk == 1 else [pltpu.VMEM(acc_shape, F32)],
                        sem=("parallel", "parallel", "arbitrary"), comm=comm)
    return outs[0] if comm is None else (outs[0], couts)


MM_VMEM_BUDGET = 36 << 20


def _fit(M, N, cost, m_mult=SUB):
    best = None
    for tm in sorted({_pick(M, p, m_mult) for p in (2048, 1024, 512, 256, 128)}):
        for tn in sorted({_pick(N, p) for p in (1408, 1024, 512, 256, 128)}):
            if best is None or (cost(tm, tn) <= MM_VMEM_BUDGET and tm * tn > best[0] * best[1]):
                best = (tm, tn)
    return best


def _sz(t):
    return jnp.dtype(t).itemsize


def mm_nn_pieces(name, a, w, p0, n_p, out_dtype, halves=1, comm=None):
    M, K = a.shape
    Nq = w.shape[2]
    tm, tn = _fit(M, Nq, lambda m, n: 2 * (m * K * _sz(a.dtype) + K * n * _sz(w.dtype) + m * n * _sz(out_dtype)))
    tpp = Nq // tn
    pph = n_p // halves
    if halves == 1:
        o_spec = pl.BlockSpec((tm, tn), lambda i, j, k: (i, j))
        oshape = (M, n_p * Nq)
    else:
        o_spec = pl.BlockSpec((None, tm, tn), lambda i, j, k: ((j // tpp) // pph, i, ((j // tpp) % pph) * tpp + j % tpp))
        oshape = (halves, M, pph * Nq)
    return _mm(name, a, w, dims=NN, grid=(M // tm, n_p * tpp, 1),
               a_spec=pl.BlockSpec((tm, K), lambda i, j, k: (i, 0)),
               b_spec=pl.BlockSpec((None, K, tn), lambda i, j, k: (p0 + j // tpp, 0, j % tpp)),
               o_spec=o_spec, out_shape=oshape, out_dtype=out_dtype, acc_shape=(tm, tn), comm=comm)


def mm_nn(name, a, w, out_dtype, exact=False, comm=None):
    M, K = a.shape
    N = w.shape[1]
    tm, tn = _fit(M, N, lambda m, n: 2 * (m * K * _sz(a.dtype) + K * n * _sz(w.dtype) + m * n * _sz(out_dtype)))
    return _mm(name, a, w, dims=NN, grid=(M // tm, N // tn, 1),
               a_spec=pl.BlockSpec((tm, K), lambda i, j, k: (i, 0)),
               b_spec=pl.BlockSpec((K, tn), lambda i, j, k: (0, j)),
               o_spec=pl.BlockSpec((tm, tn), lambda i, j, k: (i, j)),
               out_shape=(M, N), out_dtype=out_dtype, acc_shape=(tm, tn), exact=exact, comm=comm)


def mm_nt(name, dy, w, out_dtype, exact=False, comm=None):
    M, N = dy.shape
    K = w.shape[0]
    tm, tn = _fit(M, K, lambda m, n: 2 * (m * N * _sz(dy.dtype) + n * N * _sz(w.dtype) + m * n * _sz(out_dtype)))
    return _mm(name, dy, w, dims=NT, grid=(M // tm, K // tn, 1),
               a_spec=pl.BlockSpec((tm, N), lambda i, j, k: (i, 0)),
               b_spec=pl.BlockSpec((tn, N), lambda i, j, k: (j, 0)),
               o_spec=pl.BlockSpec((tm, tn), lambda i, j, k: (i, j)),
               out_shape=(M, K), out_dtype=out_dtype, acc_shape=(tm, tn), exact=exact, comm=comm)


def mm_nt_pieces(name, dy, w, out_dtype, halves=1, comm=None):
    P, K, Nq = w.shape
    M = dy.shape[-2]
    tm, tn = _fit(M, K, lambda m, n: 2 * (m * Nq * _sz(dy.dtype) + n * Nq * _sz(w.dtype) + m * n * _sz(out_dtype)) + 4 * m * n)
    pph = P // halves
    if halves == 1:
        a_spec = pl.BlockSpec((tm, Nq), lambda i, j, k: (i, k))
    else:
        a_spec = pl.BlockSpec((None, tm, Nq), lambda i, j, k: (k // pph, i, k % pph))
    return _mm(name, dy, w, dims=NT, grid=(M // tm, K // tn, P),
               a_spec=a_spec,
               b_spec=pl.BlockSpec((None, tn, Nq), lambda i, j, k: (k, j, 0)),
               o_spec=pl.BlockSpec((tm, tn), lambda i, j, k: (i, j)),
               out_shape=(M, K), out_dtype=out_dtype, acc_shape=(tm, tn), comm=comm)


def mm_nt_list(name, dys, w, out_dtype):
    P, K, Nq = w.shape
    M = dys[0].shape[0]
    assert len(dys) == P
    tm, tn = _fit(M, K, lambda m, n: 2 * (sum(m * Nq * _sz(d.dtype) for d in dys) + n * Nq * _sz(w.dtype)
                                          + m * n * _sz(out_dtype)) + 4 * m * n)

    def body(*refs):
        d_refs, w_ref, o_ref, acc = refs[:P], refs[P], refs[P + 1], refs[P + 2]
        kk = pl.program_id(2)
        for q in range(P):
            @pl.when(kk == q)
            def _(q=q):
                part = lax.dot_general(d_refs[q][...].astype(BF16), w_ref[...], NT, preferred_element_type=F32)
                acc[...] = part if q == 0 else acc[...] + part

        @pl.when(kk == P - 1)
        def _():
            o_ref[...] = acc[...].astype(o_ref.dtype)

    return pl.pallas_call(
        body, name=name, grid=(M // tm, K // tn, P),
        in_specs=[*[pl.BlockSpec((tm, Nq), lambda i, j, k: (i, 0)) for _ in range(P)],
                  pl.BlockSpec((None, tn, Nq), lambda i, j, k: (k, j, 0))],
        out_specs=pl.BlockSpec((tm, tn), lambda i, j, k: (i, j)),
        out_shape=jax.ShapeDtypeStruct((M, K), out_dtype),
        scratch_shapes=[pltpu.VMEM((tm, tn), F32)],
        compiler_params=_params("parallel", "arbitrary", "arbitrary"),
    )(*dys, w)


def mm_tn(name, a, dy, out_dtype, comm=None):
    M, K = a.shape
    N = dy.shape[1]
    tm, tn = _fit(K, N, lambda m, n: 2 * (M * m * _sz(a.dtype) + M * n * _sz(dy.dtype) + m * n * _sz(out_dtype)), LANE)
    return _mm(name, a, dy, dims=TN, grid=(K // tm, N // tn, 1),
               a_spec=pl.BlockSpec((M, tm), lambda i, j, k: (0, i)),
               b_spec=pl.BlockSpec((M, tn), lambda i, j, k: (0, j)),
               o_spec=pl.BlockSpec((tm, tn), lambda i, j, k: (i, j)),
               out_shape=(K, N), out_dtype=out_dtype, acc_shape=(tm, tn), comm=comm)


def mm_tn_pieces(name, a, dy, n_p, out_dtype, halves=1, comm=None):
    M, K = a.shape
    Nq = (dy.shape[-1] * halves) // n_p
    tm, tn = _fit(K, Nq, lambda m, n: 2 * (M * m * _sz(a.dtype) + M * n * _sz(dy.dtype) + m * n * _sz(out_dtype)), LANE)
    tpp = Nq // tn
    pph = n_p // halves
    if halves == 1:
        b_spec = pl.BlockSpec((M, tn), lambda i, j, k: (0, j))
    else:
        b_spec = pl.BlockSpec((None, M, tn), lambda i, j, k: ((j // tpp) // pph, 0, ((j // tpp) % pph) * tpp + j % tpp))
    return _mm(name, a, dy, dims=TN, grid=(K // tm, n_p * tpp, 1),
               a_spec=pl.BlockSpec((M, tm), lambda i, j, k: (0, i)),
               b_spec=b_spec,
               o_spec=pl.BlockSpec((None, tm, tn), lambda i, j, k: (j // tpp, i, j % tpp)),
               out_shape=(n_p, K, Nq), out_dtype=out_dtype, acc_shape=(tm, tn), comm=comm)


def _row_call(name, body, ins, in_kinds, outs, rows, tr, scratch=()):
    def spec(kind, shape):
        if isinstance(kind, pl.BlockSpec):
            return kind
        if kind == "row":
            return pl.BlockSpec((tr,) + tuple(shape[1:]), lambda i: (i,) + (0,) * (len(shape) - 1))
        return pl.BlockSpec(tuple(shape), lambda i: (0,) * len(shape))

    return pl.pallas_call(
        body, name=name, grid=(rows // tr,),
        in_specs=[spec(k, a.shape) for k, a in zip(in_kinds, ins)],
        out_specs=[spec(k, s) for k, s, _ in outs],
        out_shape=[jax.ShapeDtypeStruct(s, d) for _, s, d in outs],
        scratch_shapes=list(scratch),
        compiler_params=_params("arbitrary"),
    )(*ins)


def _acc(ref, val):
    @pl.when(pl.program_id(0) == 0)
    def _():
        ref[...] = val

    @pl.when(pl.program_id(0) > 0)
    def _():
        ref[...] += val


def norm_mod_fwd(name, x, w, b, tr=256):
    rows, d = x.shape
    tr = _pick(rows, tr, SUB)

    def body(x_ref, w_ref, b_ref, h_ref):
        xv = x_ref[...]
        r = lax.rsqrt(jnp.mean(xv * xv, axis=-1, keepdims=True) + EPS)
        h_ref[...] = (xv * r * w_ref[...] + b_ref[...]).astype(BF16)

    return _row_call(name, body, [x, w.reshape(1, d), b.reshape(1, d)], ["row", "vec", "vec"],
                     [("row", (rows, d), BF16)], rows, tr)[0]


def norm_mod_bwd(name, x, dh, w, dx_in, tr=256):
    rows, d = x.shape
    tr = _pick(rows, tr, SUB)

    def body(x_ref, dh_ref, w_ref, dxi_ref, dx_ref, cs1_ref, cs2_ref):
        xv = x_ref[...]
        r = lax.rsqrt(jnp.mean(xv * xv, axis=-1, keepdims=True) + EPS)
        xn = xv * r
        dhv = dh_ref[...].astype(F32)
        dxn = dhv * w_ref[...]
        dx_ref[...] = dxi_ref[...] + r * (dxn - xn * jnp.mean(dxn * xn, axis=-1, keepdims=True))
        _acc(cs1_ref, jnp.sum(dhv, axis=0, keepdims=True))
        _acc(cs2_ref, jnp.sum(dhv * xn, axis=0, keepdims=True))

    return _row_call(name, body, [x, dh, w.reshape(1, d), dx_in], ["row", "row", "vec", "row"],
                     [("row", (rows, d), F32), ("acc", (1, d), F32), ("acc", (1, d), F32)], rows, tr)


def gate_res_fwd(name, x, y, gate, tr=256):
    rows, d = x.shape
    tr = _pick(rows, tr, SUB)

    def body(x_ref, y_ref, g_ref, o_ref):
        o_ref[...] = x_ref[...] + g_ref[...] * y_ref[...].astype(F32)

    return _row_call(name, body, [x, y, gate.reshape(1, d)], ["row", "row", "vec"],
                     [("row", (rows, d), F32)], rows, tr)[0]


def gate_res_bwd(name, dx, y, gate, tr=256):
    rows, d = dx.shape
    tr = _pick(rows, tr, SUB)

    def body(dx_ref, y_ref, g_ref, dy_ref, dg_ref):
        dxv = dx_ref[...]
        dy_ref[...] = (g_ref[...] * dxv).astype(BF16)
        _acc(dg_ref, jnp.sum(dxv * y_ref[...].astype(F32), axis=0, keepdims=True))

    return _row_call(name, body, [dx, y, gate.reshape(1, d)], ["row", "row", "vec"],
                     [("row", (rows, d), BF16), ("acc", (1, d), F32)], rows, tr)


def loss_head(name, x, g, target, tr=256):
    rows, d = x.shape
    tr = _pick(rows, tr, SUB)

    def body(x_ref, g_ref, t_ref, dx_ref, dg_ref, loss_ref):
        xv = x_ref[...]
        r = lax.rsqrt(jnp.mean(xv * xv, axis=-1, keepdims=True) + EPS)
        xn = xv * r
        err = xn * g_ref[...] - t_ref[...]
        dy = err * (1.0 / d)
        dxn = dy * g_ref[...]
        dx_ref[...] = r * (dxn - xn * jnp.mean(dxn * xn, axis=-1, keepdims=True))
        _acc(dg_ref, jnp.sum(dy * xn, axis=0, keepdims=True))
        part = 0.5 * jnp.sum(jnp.sum(err * err, axis=-1, keepdims=True) * (1.0 / d), axis=0, keepdims=True)
        _acc(loss_ref, jnp.broadcast_to(part, (1, LANE)))

    return _row_call(name, body, [x, g.reshape(1, d), target], ["row", "vec", "row"],
                     [("row", (rows, d), F32), ("acc", (1, d), F32), ("acc", (1, LANE), F32)], rows, tr)


def fma3(name, a, b, c, dvec, out_dtype, tr=256):
    rows, d = a.shape
    tr = _pick(rows, tr, SUB)

    def body(a_ref, b_ref, c_ref, d_ref, o_ref):
        o_ref[...] = (d_ref[...] * a_ref[...] + b_ref[...] + c_ref[...]).astype(o_ref.dtype)

    return _row_call(name, body, [a, b, c, dvec.reshape(1, d)], ["row", "row", "row", "vec"],
                     [("row", (rows, d), out_dtype)], rows, tr)[0]


def sum_lead(name, a, out_dtype, tr=512):
    n, rows, cols = a.shape
    tr = _pick(rows, tr, 16)

    def body(a_ref, o_ref):
        acc = a_ref[0].astype(F32)
        for s in range(1, n):
            acc = acc + a_ref[s].astype(F32)
        o_ref[...] = acc.astype(o_ref.dtype)

    return pl.pallas_call(
        body, name=name, grid=(rows // tr,),
        in_specs=[pl.BlockSpec((n, tr, cols), lambda i: (0, i, 0))],
        out_specs=pl.BlockSpec((tr, cols), lambda i: (i, 0)),
        out_shape=jax.ShapeDtypeStruct((rows, cols), out_dtype),
        compiler_params=_params("parallel"),
    )(a)


def adamw(name, w, g, m, v, comm=None):
    shape = w.shape
    cols = shape[-1]
    w2, g2, m2, v2 = (t.reshape(-1, cols) for t in (w, g, m, v))
    rows = w2.shape[0]
    tr, tc = _pick(rows, 256, SUB), _pick(cols, 1536)
    c1 = 1.0 - ADAM_B1 ** ADAM_STEP
    c2 = 1.0 - ADAM_B2 ** ADAM_STEP

    def body(w_ref, g_ref, m_ref, v_ref, d_ref, mo_ref, vo_ref):
        gv = g_ref[...]
        mn = ADAM_B1 * m_ref[...] + (1.0 - ADAM_B1) * gv
        vn = ADAM_B2 * v_ref[...] + (1.0 - ADAM_B2) * (gv * gv)
        mo_ref[...] = mn
        vo_ref[...] = vn
        d_ref[...] = -ADAM_LR * ((mn / c1) / (jnp.sqrt(vn / c2) + ADAM_EPS) + ADAM_WD * w_ref[...])

    blk = pl.BlockSpec((tr, tc), lambda i, j: (i, j))
    outs, couts = _call(body, [w2, g2, m2, v2], name=name, grid=(rows // tr, cols // tc), in_specs=[blk] * 4,
                        out_specs=[blk] * 3, out_shape=[jax.ShapeDtypeStruct(w2.shape, F32)] * 3,
                        sem=("parallel", "parallel"), comm=comm)
    outs = tuple(o.reshape(shape) for o in outs)
    return outs if comm is None else (outs, couts)


def add_half(name, grad, got, c_idx, tr=256):
    Pn, R, C = grad.shape
    hr = R // 2
    tr = _pick(hr, tr, HALO)
    nb = hr // tr

    def body(c_ref, a_ref, b_ref, o_ref):
        o_ref[...] = (a_ref[...].astype(F32) + b_ref[...].astype(F32)).astype(o_ref.dtype)

    return pl.pallas_call(
        body, name=name,
        grid_spec=pltpu.PrefetchScalarGridSpec(
            num_scalar_prefetch=1, grid=(Pn, nb),
            in_specs=[pl.BlockSpec((None, tr, C), lambda q, i, c: (q, c[0] * nb + i, 0)),
                      pl.BlockSpec((None, tr, C), lambda q, i, c: (q, i, 0))],
            out_specs=pl.BlockSpec((None, tr, C), lambda q, i, c: (q, i, 0))),
        out_shape=jax.ShapeDtypeStruct((Pn, hr, C), BF16),
        compiler_params=_params("parallel", "parallel"),
    )(c_idx, grad, got)


def pair_sum_to_slot(name, buf, got, ids, tr=256):
    R, C = buf.shape
    hr = R // 2
    tr = _pick(hr, tr, SUB)
    nb = hr // tr

    def body(ids_ref, a_ref, b_ref, o_ref):
        o_ref[...] = a_ref[...] + b_ref[...]

    return pl.pallas_call(
        body, name=name,
        grid_spec=pltpu.PrefetchScalarGridSpec(
            num_scalar_prefetch=1, grid=(nb,),
            in_specs=[pl.BlockSpec((tr, C), lambda i, ids: (ids[1] * nb + i, 0)),
                      pl.BlockSpec((tr, C), lambda i, ids: (i, 0))],
            out_specs=pl.BlockSpec((None, tr, C), lambda i, ids: (ids[0], i, 0))),
        out_shape=jax.ShapeDtypeStruct((N_CHIP, hr, C), F32),
        compiler_params=_params("parallel"),
    )(ids, buf, got)


def sum_chips_to_half(name, slots, ids, tr=256):
    n, hr, C = slots.shape
    tr = _pick(hr, tr, SUB)

    def body(ids_ref, s_ref, o_ref):
        acc = s_ref[0]
        for q in range(1, n):
            acc = acc + s_ref[q]
        o_ref[...] = acc

    return pl.pallas_call(
        body, name=name,
        grid_spec=pltpu.PrefetchScalarGridSpec(
            num_scalar_prefetch=1, grid=(hr // tr,),
            in_specs=[pl.BlockSpec((n, tr, C), lambda i, ids: (0, i, 0))],
            out_specs=pl.BlockSpec((None, tr, C), lambda i, ids: (ids[1], i, 0))),
        out_shape=jax.ShapeDtypeStruct((2, hr, C), F32),
        compiler_params=_params("parallel"),
    )(ids, slots)


def sum_slots(name, slots, mine, ids, tr=256):
    Pn, hr, C = slots.shape
    tr = _pick(hr, tr, HALO)

    def body(ids_ref, m_ref, s1_ref, s2_ref, s3_ref, o_ref):
        o_ref[...] = (m_ref[...].astype(F32) + s1_ref[...].astype(F32)) + (s2_ref[...].astype(F32) + s3_ref[...].astype(F32))

    def other(k):
        return pl.BlockSpec((None, tr, C), lambda i, ids: ((ids[0] + k) % Pn, i, 0))

    return pl.pallas_call(
        body, name=name,
        grid_spec=pltpu.PrefetchScalarGridSpec(
            num_scalar_prefetch=1, grid=(hr // tr,),
            in_specs=[pl.BlockSpec((None, tr, C), lambda i, ids: (ids[0], i, 0)), other(1), other(2), other(3)],
            out_specs=pl.BlockSpec((None, tr, C), lambda i, ids: (ids[1], i, 0))),
        out_shape=jax.ShapeDtypeStruct((2, hr, C), F32),
        compiler_params=_params("parallel"),
    )(ids, mine, slots, slots, slots)


HALO = 16


def _halo_specs(lead, R, tn, n_rows, col_of):
    nb, nblk = R // HALO, n_rows // HALO

    def mk(rows, row_of):
        return pl.BlockSpec((lead, rows, tn), lambda *g: (0, row_of(g[-1]), col_of(g)))

    return (mk(HALO, lambda i: jnp.maximum(i * nb - 1, 0)), mk(R, lambda i: i),
            mk(HALO, lambda i: jnp.minimum((i + 1) * nb, nblk - 1)))


def _fill_halo(dst, i, last, R, prev, cur, nxt):
    nd = len(dst.shape)
    lead = (slice(None),) * (nd - 2)
    dst[lead + (slice(0, HALO), slice(None))] = jnp.where(i == 0, 0.0, prev)
    dst[lead + (slice(HALO, HALO + R), slice(None))] = cur
    dst[lead + (slice(HALO + R, HALO + R + HALO), slice(None))] = jnp.where(i == last, 0.0, nxt)


def _shift_mats(n):
    i = np.arange(n)
    return jnp.asarray(np.stack([i[:, None] - 1 == i[None, :], i[:, None] + 1 == i[None, :]]), BF16)


def _shifted(s_ref, xb):
    return (jnp.dot(s_ref[0], xb, preferred_element_type=F32), jnp.dot(s_ref[1], xb, preferred_element_type=F32))


def ffn_mid_fwd(name, up3, cw, cb, R=256, tn=512, comm=None):
    _, L, Fd = up3.shape
    R, tn = _pick(L, R, HALO), _pick(Fd, tn)
    nrow = L // R

    def body(p_ref, c_ref, n_ref, w_ref, b_ref, s_ref, act_ref):
        i = pl.program_id(1)
        row = lax.broadcasted_iota(jnp.int32, (R, tn), 0)
        cv = []
        for z in range(2):
            xb = c_ref[z]
            before = jnp.where(i == 0, 0.0, p_ref[z].astype(F32)[HALO - 1:HALO])
            after = jnp.where(i == nrow - 1, 0.0, n_ref[z].astype(F32)[0:1])
            dn, up = _shifted(s_ref, xb)
            dn = jnp.where(row == 0, before, dn)
            up = jnp.where(row == R - 1, after, up)
            cv.append(b_ref[z] + w_ref[z, 0:1, :] * dn + w_ref[z, 1:2, :] * xb.astype(F32) + w_ref[z, 2:3, :] * up)
        u, g = cv
        act_ref[...] = (u * g * _sigmoid(g)).astype(BF16)

    hs = _halo_specs(2, R, tn, L, lambda g: g[0])
    outs, couts = _call(
        body, [up3, up3, up3, cw, cb, _shift_mats(R)], name=name, grid=(Fd // tn, nrow),
        in_specs=[*hs, pl.BlockSpec((2, 3, tn), lambda j, i: (0, 0, j)), pl.BlockSpec((2, 1, tn), lambda j, i: (0, 0, j)),
                  pl.BlockSpec((2, R, R), lambda j, i: (0, 0, 0))],
        out_specs=[pl.BlockSpec((R, tn), lambda j, i: (i, j))],
        out_shape=[jax.ShapeDtypeStruct((L, Fd), BF16)], sem=("parallel", "arbitrary"), comm=comm)
    return outs[0] if comm is None else (outs[0], couts)


def ffn_mid_bwd(name, up3, dact, cw, cb, R=256, tn=512, comm=None):
    _, L, Fd = up3.shape
    R, tn = _pick(L, R, HALO), _pick(Fd, tn)
    nrow = L // R

    def gate_grads(u, g, d):
        sg = _sigmoid(g)
        return d * g * sg, d * u * sg * (1.0 + g * (1.0 - sg))

    def body(pu, cu, nu, pd, cd, nd, w_ref, b_ref, s_ref, dup_ref, dcw_ref, dcb_ref):
        i = pl.program_id(1)
        first, last = i == 0, i == nrow - 1
        row = lax.broadcasted_iota(jnp.int32, (R, tn), 0)
        cv, cv_b, cv_a, taps = [], [], [], []
        for z in range(2):
            xb = cu[z]
            xf = xb.astype(F32)
            pf = jnp.where(first, 0.0, pu[z].astype(F32))
            nf = jnp.where(last, 0.0, nu[z].astype(F32))
            xm2, xm1, xp0, xp1 = pf[HALO - 2:HALO - 1], pf[HALO - 1:HALO], nf[0:1], nf[1:2]
            dn, up = _shifted(s_ref, xb)
            dn = jnp.where(row == 0, xm1, dn)
            up = jnp.where(row == R - 1, xp0, up)
            w0, w1, w2, b = w_ref[z, 0:1, :], w_ref[z, 1:2, :], w_ref[z, 2:3, :], b_ref[z]
            cv.append(b + w0 * dn + w1 * xf + w2 * up)
            cv_b.append(b + w0 * xm2 + w1 * xm1 + w2 * xf[0:1])
            cv_a.append(b + w0 * xf[R - 1:R] + w1 * xp0 + w2 * xp1)
            taps.append((dn, xf, up))
        dcs = gate_grads(cv[0], cv[1], cd[0].astype(F32))
        dcs_b = gate_grads(cv_b[0], cv_b[1], jnp.where(first, 0.0, pd[0].astype(F32)[HALO - 1:HALO]))
        dcs_a = gate_grads(cv_a[0], cv_a[1], jnp.where(last, 0.0, nd[0].astype(F32)[0:1]))

        @pl.when(first)
        def _():
            dcw_ref[...] = jnp.zeros_like(dcw_ref)
            dcb_ref[...] = jnp.zeros_like(dcb_ref)

        for z in range(2):
            dc = dcs[z]
            dc_dn, dc_up = _shifted(s_ref, dc.astype(BF16))
            dc_dn = jnp.where(row == 0, dcs_b[z], dc_dn)
            dc_up = jnp.where(row == R - 1, dcs_a[z], dc_up)
            dup_ref[z] = (w_ref[z, 0:1, :] * dc_up + w_ref[z, 1:2, :] * dc + w_ref[z, 2:3, :] * dc_dn).astype(BF16)
            dcb_ref[z] += jnp.sum(dc, axis=0, keepdims=True)
            for k in range(3):
                dcw_ref[z, k:k + 1, :] += jnp.sum(dc * taps[z][k], axis=0, keepdims=True)

    hu = _halo_specs(2, R, tn, L, lambda g: g[0])
    hd = _halo_specs(1, R, tn, L, lambda g: g[0])
    outs, couts = _call(
        body, [up3, up3, up3, dact[None], dact[None], dact[None], cw, cb, _shift_mats(R)], name=name, grid=(Fd // tn, nrow),
        in_specs=[*hu, *hd, pl.BlockSpec((2, 3, tn), lambda j, i: (0, 0, j)), pl.BlockSpec((2, 1, tn), lambda j, i: (0, 0, j)),
                  pl.BlockSpec((2, R, R), lambda j, i: (0, 0, 0))],
        out_specs=[pl.BlockSpec((2, R, tn), lambda j, i: (0, i, j)), pl.BlockSpec((2, 3, tn), lambda j, i: (0, 0, j)),
                   pl.BlockSpec((2, 1, tn), lambda j, i: (0, 0, j))],
        out_shape=[jax.ShapeDtypeStruct((2, L, Fd), BF16), jax.ShapeDtypeStruct((2, 3, Fd), F32),
                   jax.ShapeDtypeStruct((2, 1, Fd), F32)],
        sem=("parallel", "arbitrary"), comm=comm)
    return outs if comm is None else (outs, couts)


def _glu_z0(blk):
    return blk[0].astype(F32) * _sigmoid(blk[1].astype(F32))


def _sublane_copies(ref, cs):
    n = ref.shape[1]
    blk = ref[0, :, cs]
    for b in range(1, SUB):
        ref[b, :, cs] = pltpu.roll(blk, n - b, 0)


def _tap(ref, offset, rows, cs):
    return ref[offset % SUB, pl.ds(offset - offset % SUB, rows), cs]


def conf_mid_fwd(name, ag3, dw_w, dw_b, ln_g, ln_b, R=128, cb=256):
    _, L, C = ag3.shape
    K = dw_w.shape[0]
    pad = (K - 1) // 2
    assert pad <= HALO
    R, cb = _pick(L, R, HALO), _pick(C, cb)
    nrow = L // R

    def body(p_ref, c_ref, n_ref, w_ref, b_ref, g_ref, bb_ref, z1_ref, z3_ref, s_ref):
        i = pl.program_id(0)
        _fill_halo(s_ref.at[0], i, nrow - 1, R, _glu_z0(p_ref), _glu_z0(c_ref), _glu_z0(n_ref))
        for c0 in range(0, C, cb):
            cs = slice(c0, c0 + cb)
            _sublane_copies(s_ref, cs)
            acc = jnp.broadcast_to(b_ref[:, cs], (R, cb))
            for k in range(K):
                acc = acc + w_ref[k:k + 1, cs] * _tap(s_ref, HALO - pad + k, R, cs)
            z1_ref[:, cs] = acc
        z1 = z1_ref[...]
        zc = z1 - jnp.mean(z1, axis=-1, keepdims=True)
        zn = zc * lax.rsqrt(jnp.mean(zc * zc, axis=-1, keepdims=True) + EPS)
        z2 = zn * g_ref[...] + bb_ref[...]
        z3_ref[...] = (z2 * _sigmoid(z2)).astype(BF16)

    hs = _halo_specs(2, R, C, L, lambda g: 0)
    vec = pl.BlockSpec((1, C), lambda i: (0, 0))
    return pl.pallas_call(
        body, name=name, grid=(nrow,),
        in_specs=[*hs, pl.BlockSpec((K, C), lambda i: (0, 0)), vec, vec, vec],
        out_specs=[pl.BlockSpec((R, C), lambda i: (i, 0)), pl.BlockSpec((R, C), lambda i: (i, 0))],
        out_shape=[jax.ShapeDtypeStruct((L, C), F32), jax.ShapeDtypeStruct((L, C), BF16)],
        scratch_shapes=[pltpu.VMEM((SUB, R + 2 * HALO, C), F32)],
        compiler_params=_params("parallel"),
    )(ag3, ag3, ag3, dw_w, dw_b.reshape(1, C), ln_g.reshape(1, C), ln_b.reshape(1, C))


def conf_ln_bwd(name, z1, dz3, ln_g, ln_b, tr=256):
    rows, C = z1.shape
    tr = _pick(rows, tr, HALO)

    def body(z_ref, d_ref, g_ref, b_ref, dz_ref, dg_ref, db_ref):
        z1v = z_ref[...]
        zc = z1v - jnp.mean(z1v, axis=-1, keepdims=True)
        rs = lax.rsqrt(jnp.mean(zc * zc, axis=-1, keepdims=True) + EPS)
        zn = zc * rs
        z2 = zn * g_ref[...] + b_ref[...]
        sg = _sigmoid(z2)
        dz2 = d_ref[...].astype(F32) * sg * (1.0 + z2 * (1.0 - sg))
        _acc(dg_ref, jnp.sum(dz2 * zn, axis=0, keepdims=True))
        _acc(db_ref, jnp.sum(dz2, axis=0, keepdims=True))
        dzn = dz2 * g_ref[...]
        dz1 = rs * (dzn - jnp.mean(dzn, axis=-1, keepdims=True) - zn * jnp.mean(dzn * zn, axis=-1, keepdims=True))
        dz_ref[...] = dz1.astype(BF16)

    return _row_call(name, body, [z1, dz3, ln_g.reshape(1, C), ln_b.reshape(1, C)], ["row", "row", "vec", "vec"],
                     [("row", (rows, C), BF16), ("acc", (1, C), F32), ("acc", (1, C), F32)], rows, tr)


def conf_conv_bwd(name, ag3, dz1, dw_w, R=128, cb=256, comm=None):
    _, L, C = ag3.shape
    K = dw_w.shape[0]
    pad = (K - 1) // 2
    R, cb = _pick(L, R, HALO), _pick(C, cb)
    nrow = L // R

    def body(pa, ca, na, pd, cd, nd, w_ref, dag_ref, dw_ref, db_ref, s_ref, d_ref, z_ref):
        i = pl.program_id(0)
        _fill_halo(s_ref.at[0], i, nrow - 1, R, _glu_z0(pa), _glu_z0(ca), _glu_z0(na))
        _fill_halo(d_ref.at[0], i, nrow - 1, R, pd[0].astype(F32), cd[0].astype(F32), nd[0].astype(F32))

        @pl.when(i == 0)
        def _():
            dw_ref[...] = jnp.zeros_like(dw_ref)
            db_ref[...] = jnp.zeros_like(db_ref)

        for c0 in range(0, C, cb):
            cs = slice(c0, c0 + cb)
            _sublane_copies(s_ref, cs)
            _sublane_copies(d_ref, cs)
            dcur = d_ref[0, pl.ds(HALO, R), cs]
            acc = jnp.zeros((R, cb), F32)
            for k in range(K):
                acc = acc + w_ref[k:k + 1, cs] * _tap(d_ref, HALO + pad - k, R, cs)
                dw_ref[k:k + 1, cs] += jnp.sum(dcur * _tap(s_ref, HALO - pad + k, R, cs), axis=0, keepdims=True)
            z_ref[:, cs] = acc
            db_ref[:, cs] += jnp.sum(dcur, axis=0, keepdims=True)
        dz0 = z_ref[...]
        a = ca[0].astype(F32)
        sg = _sigmoid(ca[1].astype(F32))
        dag_ref[0] = (dz0 * sg).astype(BF16)
        dag_ref[1] = (dz0 * a * sg * (1.0 - sg)).astype(BF16)

    ha = _halo_specs(2, R, C, L, lambda g: 0)
    hd = _halo_specs(1, R, C, L, lambda g: 0)
    outs, couts = _call(
        body, [ag3, ag3, ag3, dz1[None], dz1[None], dz1[None], dw_w], name=name, grid=(nrow,),
        in_specs=[*ha, *hd, pl.BlockSpec((K, C), lambda i: (0, 0))],
        out_specs=[pl.BlockSpec((2, R, C), lambda i: (0, i, 0)), pl.BlockSpec((K, C), lambda i: (0, 0)),
                   pl.BlockSpec((1, C), lambda i: (0, 0))],
        out_shape=[jax.ShapeDtypeStruct((2, L, C), BF16), jax.ShapeDtypeStruct((K, C), F32),
                   jax.ShapeDtypeStruct((1, C), F32)],
        scratch_shapes=[pltpu.VMEM((SUB, R + 2 * HALO, C), F32), pltpu.VMEM((SUB, R + 2 * HALO, C), F32),
                        pltpu.VMEM((R, C), F32)],
        sem=("arbitrary",), comm=comm)
    return outs if comm is None else (outs, couts)


_GELU_C = math.sqrt(2.0 / math.pi)


def _gelu(x):
    return 0.5 * x * (1.0 + jnp.tanh(_GELU_C * (x + 0.044715 * x * x * x)))


def _gelu_grad(x):
    t = jnp.tanh(_GELU_C * (x + 0.044715 * x * x * x))
    return 0.5 * (1.0 + t) + 0.5 * x * (1.0 - t * t) * _GELU_C * (1.0 + 3.0 * 0.044715 * x * x)


def glu_fwd(name, u, y0, y1, d, wg, tr=512):
    rows, W = u.shape
    tr = _pick(rows, tr, HALO)

    def body(u_ref, y0_ref, y1_ref, d_ref, w_ref, o_ref):
        z = _gelu(d_ref[...] * u_ref[...] + y0_ref[...] + y1_ref[...])
        zz = jnp.dot(z.astype(BF16), w_ref[...], preferred_element_type=F32)
        o_ref[...] = (z * _sigmoid(zz)).astype(BF16)

    return _row_call(name, body, [u, y0, y1, d.reshape(1, W), wg], ["row", "row", "row", "vec", "vec"],
                     [("row", (rows, W), BF16)], rows, tr)[0]


def glu_bwd(name, u, y0, y1, d, wg, dmix, tr=512):
    rows, W = u.shape
    tr = _pick(rows, tr, HALO)

    def body(u_ref, y0_ref, y1_ref, d_ref, w_ref, do_ref, dy_ref, z_ref, dzz_ref, dd_ref):
        uv = u_ref[...]
        y = d_ref[...] * uv + y0_ref[...] + y1_ref[...]
        z = _gelu(y)
        zz = jnp.dot(z.astype(BF16), w_ref[...], preferred_element_type=F32)
        sg = _sigmoid(zz)
        do = do_ref[...].astype(F32)
        dzz = (do * z * sg * (1.0 - sg)).astype(BF16)
        dz = do * sg + lax.dot_general(dzz, w_ref[...], NT, preferred_element_type=F32)
        dy = dz * _gelu_grad(y)
        dy_ref[...] = dy
        z_ref[...] = z.astype(BF16)
        dzz_ref[...] = dzz
        _acc(dd_ref, jnp.sum(dy * uv, axis=0, keepdims=True))

    do_spec = pl.BlockSpec((tr, W), lambda i: (i, 0))
    return _row_call(name, body, [u, y0, y1, d.reshape(1, W), wg, dmix], ["row", "row", "row", "vec", "vec", do_spec],
                     [("row", (rows, W), F32), ("row", (rows, W), BF16), ("row", (rows, W), BF16), ("acc", (1, W), F32)],
                     rows, tr)


NA_KEYS = NA_WIN_R * GRID_W


NA_PAIRS = NA_WIN_R // 2


def na_bias(rpb):
    H, nr, nc = rpb.shape
    e, ok = _na_col_select()
    rp = jnp.pad(rpb.reshape(H * nr, nc), ((0, (-H * nr) % SUB), (0, LANE - nc)))
    cols = mm_nn("na_bias_mm", rp, jnp.asarray(e, F32), F32, exact=True)[:H * nr]
    tiles = (cols + jnp.asarray(np.where(ok, 0.0, NEG), F32)).reshape(H, nr, GRID_W, GRID_W)
    return jnp.concatenate([tiles[:, :-1], tiles[:, 1:]], axis=-1)


def na_bias_grad(db2):
    H, n2 = db2.shape[:2]
    left, right = db2[..., :GRID_W], db2[..., GRID_W:]
    tiles = jnp.pad(left, ((0, 0), (0, 1), (0, 0), (0, 0))) + jnp.pad(right, ((0, 0), (1, 0), (0, 0), (0, 0)))
    flat = tiles.reshape(H * (n2 + 1), GRID_W * GRID_W)
    flat = jnp.pad(flat, ((0, (-flat.shape[0]) % SUB), (0, 0)))
    dcol = mm_nt("na_bias_fold", flat, na_bias_fold_matrix(), F32, exact=True)
    return dcol[:H * (n2 + 1), :2 * NA_WIN_C - 1].reshape(H, n2 + 1, 2 * NA_WIN_C - 1)


def _na_col_select():
    q = np.arange(GRID_W)
    cs = np.clip(q - NA_WIN_C // 2, 0, GRID_W - NA_WIN_C)
    ok = ((q[None, :] >= cs[:, None]) & (q[None, :] < cs[:, None] + NA_WIN_C)).reshape(-1)
    cidx = np.clip(q[None, :] - q[:, None] + (NA_WIN_C - 1), 0, 2 * NA_WIN_C - 2).reshape(-1)
    return (cidx[None, :] == np.arange(LANE)[:, None]) & ok[None, :], ok


def na_bias_fold_matrix():
    return jnp.asarray(_na_col_select()[0], F32)


def _na_window(r, rows):
    kr0 = jnp.clip(r - NA_WIN_R // 2, 0, rows - NA_WIN_R)
    return pl.multiple_of(kr0 * GRID_W, GRID_W), r - kr0


def _na_dims(qkv, kvc):
    L = qkv.shape[0]
    NA = qkv.shape[1] // 3
    H = NA // NA_HEAD_DIM
    hp = 2 if H % 2 == 0 else 1
    return L, NA, H, hp, H // hp, L // GRID_W, kvc.shape[0]


def _na_bias_tile(b_ref, hh, off):
    return jnp.concatenate([b_ref[hh, NA_WIN_R - 1 - off + 2 * j] for j in range(NA_PAIRS)], axis=-1)


def natten_fwd(name, qkv, kvc, bias, comm=None):
    L, NA, H, hp, G, rows, Lc = _na_dims(qkv, kvc)
    scale = NA_HEAD_DIM ** -0.5
    wd = hp * NA_HEAD_DIM

    def body(q_ref, k_ref, v_ref, kc_ref, vc_ref, b_ref, o_ref, lse_ref):
        st, off = _na_window(pl.program_id(1), rows)
        for hh in range(hp):
            sl = slice(hh * NA_HEAD_DIM, (hh + 1) * NA_HEAD_DIM)
            q = q_ref[:, sl]
            s_loc = (lax.dot_general(q, k_ref[pl.ds(st, NA_KEYS), sl], NT, preferred_element_type=F32) * scale
                     + _na_bias_tile(b_ref, hh, off))
            s_ctx = lax.dot_general(q, kc_ref[:, sl], NT, preferred_element_type=F32) * scale
            m = jnp.maximum(jnp.max(s_loc, axis=-1, keepdims=True), jnp.max(s_ctx, axis=-1, keepdims=True))
            p_loc, p_ctx = jnp.exp(s_loc - m), jnp.exp(s_ctx - m)
            l = jnp.sum(p_loc, axis=-1, keepdims=True) + jnp.sum(p_ctx, axis=-1, keepdims=True)
            o = (jnp.dot(p_loc.astype(BF16), v_ref[pl.ds(st, NA_KEYS), sl], preferred_element_type=F32)
                 + jnp.dot(p_ctx.astype(BF16), vc_ref[:, sl], preferred_element_type=F32))
            o_ref[:, sl] = (o / l).astype(BF16)
            lse_ref[hh] = m + jnp.log(l)

    outs, couts = _call(
        body, [qkv, qkv, qkv, kvc, kvc, bias], name=name, grid=(G, rows),
        in_specs=[pl.BlockSpec((GRID_W, wd), lambda h, r: (r, h)),
                  pl.BlockSpec((L, wd), lambda h, r: (0, G + h)),
                  pl.BlockSpec((L, wd), lambda h, r: (0, 2 * G + h)),
                  pl.BlockSpec((Lc, wd), lambda h, r: (0, h)),
                  pl.BlockSpec((Lc, wd), lambda h, r: (0, G + h)),
                  pl.BlockSpec((hp,) + bias.shape[1:], lambda h, r: (h, 0, 0, 0))],
        out_specs=[pl.BlockSpec((GRID_W, wd), lambda h, r: (r, h)),
                   pl.BlockSpec((hp, GRID_W, 1), lambda h, r: (h, r, 0))],
        out_shape=[jax.ShapeDtypeStruct((L, NA), BF16), jax.ShapeDtypeStruct((H, L, 1), F32)],
        sem=("parallel", "arbitrary"), comm=comm)
    return outs if comm is None else (outs, couts)


def natten_bwd(name, qkv, kvc, bias, o, lse, dmix, comm=None):
    L, NA, H, hp, G, rows, Lc = _na_dims(qkv, kvc)
    scale = NA_HEAD_DIM ** -0.5
    wd = hp * NA_HEAD_DIM

    def body(q_ref, k_ref, v_ref, kc_ref, vc_ref, b_ref, o_ref, lse_ref, do_ref,
             dq_ref, dk_ref, dv_ref, dkc_ref, dvc_ref, db_ref):
        r = pl.program_id(1)
        st, off = _na_window(r, rows)

        @pl.when(r == 0)
        def _():
            for ref in (dk_ref, dv_ref, dkc_ref, dvc_ref, db_ref):
                ref[...] = jnp.zeros_like(ref)

        for hh in range(hp):
            sl = slice(hh * NA_HEAD_DIM, (hh + 1) * NA_HEAD_DIM)
            q, kl, vl, kc, vc = q_ref[:, sl], k_ref[pl.ds(st, NA_KEYS), sl], v_ref[pl.ds(st, NA_KEYS), sl], kc_ref[:, sl], vc_ref[:, sl]
            do = do_ref[:, sl]
            lse_v = lse_ref[hh]
            p_loc = jnp.exp(lax.dot_general(q, kl, NT, preferred_element_type=F32) * scale + _na_bias_tile(b_ref, hh, off) - lse_v)
            p_ctx = jnp.exp(lax.dot_general(q, kc, NT, preferred_element_type=F32) * scale - lse_v)
            delta = jnp.sum(do.astype(F32) * o_ref[:, sl].astype(F32), axis=-1, keepdims=True)
            ds_loc = p_loc * (lax.dot_general(do, vl, NT, preferred_element_type=F32) - delta)
            ds_ctx = p_ctx * (lax.dot_general(do, vc, NT, preferred_element_type=F32) - delta)
            dsl, dsc = ds_loc.astype(BF16), ds_ctx.astype(BF16)
            dq = jnp.dot(dsl, kl, preferred_element_type=F32) + jnp.dot(dsc, kc, preferred_element_type=F32)
            dq_ref[:, sl] = (dq * scale).astype(BF16)
            dk_ref[pl.ds(st, NA_KEYS), sl] += lax.dot_general(dsl, q, TN, preferred_element_type=F32) * scale
            dv_ref[pl.ds(st, NA_KEYS), sl] += lax.dot_general(p_loc.astype(BF16), do, TN, preferred_element_type=F32)
            dkc_ref[:, sl] += lax.dot_general(dsc, q, TN, preferred_element_type=F32) * scale
            dvc_ref[:, sl] += lax.dot_general(p_ctx.astype(BF16), do, TN, preferred_element_type=F32)
            for j in range(NA_PAIRS):
                db_ref[hh, NA_WIN_R - 1 - off + 2 * j] += ds_loc[:, 2 * j * GRID_W:(2 * j + 2) * GRID_W]

    tok = pl.BlockSpec((GRID_W, wd), lambda h, r: (r, h))
    bia = pl.BlockSpec((hp,) + bias.shape[1:], lambda h, r: (h, 0, 0, 0))
    outs, couts = _call(
        body, [qkv, qkv, qkv, kvc, kvc, bias, o, lse, dmix], name=name, grid=(G, rows),
        in_specs=[tok,
                  pl.BlockSpec((L, wd), lambda h, r: (0, G + h)),
                  pl.BlockSpec((L, wd), lambda h, r: (0, 2 * G + h)),
                  pl.BlockSpec((Lc, wd), lambda h, r: (0, h)),
                  pl.BlockSpec((Lc, wd), lambda h, r: (0, G + h)),
                  bia,
                  tok,
                  pl.BlockSpec((hp, GRID_W, 1), lambda h, r: (h, r, 0)),
                  pl.BlockSpec((GRID_W, wd), lambda h, r: (r, G + h))],
        out_specs=[tok,
                   pl.BlockSpec((L, wd), lambda h, r: (0, h)),
                   pl.BlockSpec((L, wd), lambda h, r: (0, h)),
                   pl.BlockSpec((Lc, wd), lambda h, r: (0, h)),
                   pl.BlockSpec((Lc, wd), lambda h, r: (0, h)),
                   bia],
        out_shape=[jax.ShapeDtypeStruct((L, NA), BF16), jax.ShapeDtypeStruct((L, NA), F32), jax.ShapeDtypeStruct((L, NA), F32),
                   jax.ShapeDtypeStruct((Lc, NA), F32), jax.ShapeDtypeStruct((Lc, NA), F32),
                   jax.ShapeDtypeStruct(bias.shape, F32)],
        sem=("parallel", "arbitrary"), comm=comm)
    return outs if comm is None else (outs, couts)


def _s5_dims(T, N):
    TC = T // S5_SEG
    assert T % (S5_SEG * SUB * 2) == 0 and N % S5_STRIP == 0
    return TC, TC // SUB, S5_SEG, N // S5_STRIP


def _s5_backward(d, rev):
    return (d == 1) != rev


def s5_scan(name, xin, mats, a, rev, comm=None):
    _, T, W = xin.shape
    N = a.shape[-1]
    TC, NG, NCH, NS = _s5_dims(T, N)
    CW, SL = W // NS, S5_STRIP

    def ck(d, k):
        return jnp.where(_s5_backward(d, rev), NCH - 1 - k, k)

    def body(x_ref, m_ref, a_ref, h_ref, f_ref, carry, hs):
        @pl.when(pl.program_id(2) == 0)
        def _():
            carry[...] = jnp.zeros_like(carry)

        xb = x_ref[...].astype(BF16)
        hs[0] = jnp.dot(xb, m_ref[0], preferred_element_type=F32)
        hs[1] = jnp.dot(xb, m_ref[1], preferred_element_type=F32)
        ar, ai = jnp.broadcast_to(a_ref[0], (SUB, SL)), jnp.broadcast_to(a_ref[1], (SUB, SL))
        bw = _s5_backward(pl.program_id(0), rev)

        def step(t, c):
            hr, hi = c
            row = pl.multiple_of(jnp.where(bw, NG - 1 - t, t) * SUB, SUB)
            nr = ar * hr - ai * hi + hs[0, pl.ds(row, SUB), :]
            ni = ar * hi + ai * hr + hs[1, pl.ds(row, SUB), :]
            hs[0, pl.ds(row, SUB), :] = nr
            hs[1, pl.ds(row, SUB), :] = ni
            return nr, ni

        hr, hi = lax.fori_loop(0, NG, step, (carry[0], carry[1]))
        carry[0], carry[1] = hr, hi
        f_ref[0], f_ref[1] = hr, hi
        h_ref[...] = hs[...].astype(BF16)

    outs, couts = _call(
        body, [xin, mats, a], name=name, grid=(2, NS, NCH),
        in_specs=[pl.BlockSpec((None, TC, CW), lambda d, j, k: (d, ck(d, k), j)),
                  pl.BlockSpec((None, 2, None, CW, SL), lambda d, j, k: (d, 0, j, 0, 0)),
                  pl.BlockSpec((None, 2, 1, SL), lambda d, j, k: (d, 0, 0, j))],
        out_specs=[pl.BlockSpec((None, 2, TC, SL), lambda d, j, k: (d, 0, ck(d, k), j)),
                   pl.BlockSpec((None, 2, SUB, SL), lambda d, j, k: (d, 0, 0, j))],
        out_shape=[jax.ShapeDtypeStruct((2, 2, T, N), BF16), jax.ShapeDtypeStruct((2, 2, SUB, N), F32)],
        scratch_shapes=[pltpu.VMEM((2, SUB, SL), F32), pltpu.VMEM((2, TC, SL), F32)],
        sem=("parallel", "parallel", "arbitrary"), comm=comm)
    return outs if comm is None else (outs, couts)


def s5_fix(name, hloc, hin, a, mats, rev, comm=None):
    _, _, T, N = hloc.shape
    TC, NG, NCH, NS = _s5_dims(T, N)
    SL = S5_STRIP
    CW = mats.shape[-1]

    def ck(d, k):
        return jnp.where(_s5_backward(d, rev), NCH - 1 - k, k)

    def body(h_ref, hin_ref, a_ref, m_ref, ho_ref, y_ref, g, hs):
        @pl.when(pl.program_id(2) == 0)
        def _():
            g[...] = hin_ref[...]

        hs[...] = h_ref[...].astype(F32)
        ar, ai = jnp.broadcast_to(a_ref[0], (SUB, SL)), jnp.broadcast_to(a_ref[1], (SUB, SL))
        bw = _s5_backward(pl.program_id(0), rev)

        def step(t, c):
            gr, gi = c
            row = pl.multiple_of(jnp.where(bw, NG - 1 - t, t) * SUB, SUB)
            nr = ar * gr - ai * gi
            ni = ar * gi + ai * gr
            hs[0, pl.ds(row, SUB), :] += nr
            hs[1, pl.ds(row, SUB), :] += ni
            return nr, ni

        gr, gi = lax.fori_loop(0, NG, step, (g[0], g[1]))
        g[0], g[1] = gr, gi
        hb = hs[...].astype(BF16)
        ho_ref[...] = hb
        y_ref[...] = (jnp.dot(hb[0], m_ref[0], preferred_element_type=F32)
                      + jnp.dot(hb[1], m_ref[1], preferred_element_type=F32))

    outs, couts = _call(
        body, [hloc, hin, a, mats], name=name, grid=(2, NS, NCH),
        in_specs=[pl.BlockSpec((None, 2, TC, SL), lambda d, j, k: (d, 0, ck(d, k), j)),
                  pl.BlockSpec((None, 2, SUB, SL), lambda d, j, k: (d, 0, 0, j)),
                  pl.BlockSpec((None, 2, 1, SL), lambda d, j, k: (d, 0, 0, j)),
                  pl.BlockSpec((None, 2, None, SL, CW), lambda d, j, k: (d, 0, j, 0, 0))],
        out_specs=[pl.BlockSpec((None, 2, TC, SL), lambda d, j, k: (d, 0, ck(d, k), j)),
                   pl.BlockSpec((None, TC, CW), lambda d, j, k: (d, ck(d, k), j))],
        out_shape=[jax.ShapeDtypeStruct((2, 2, T, N), BF16), jax.ShapeDtypeStruct((2, T, NS * CW), F32)],
        scratch_shapes=[pltpu.VMEM((2, SUB, SL), F32), pltpu.VMEM((2, TC, SL), F32)],
        sem=("parallel", "parallel", "arbitrary"), comm=comm)
    return outs if comm is None else (outs, couts)


def s5_grads(name, g, h, u, dy, comm=None):
    _, _, T, N = g.shape
    W = u.shape[-1]
    TC, NG, NCH, NS = _s5_dims(T, N)
    CW, SL = W // NS, S5_STRIP

    def body(g_ref, h_ref, hp_ref, hl_ref, u_ref, dy_ref, dm_ref, dc_ref, da_ref, hs):
        k = pl.program_id(2)
        sub = lax.broadcasted_iota(jnp.int32, (SUB, SL), 0)

        hf = h_ref[...].astype(F32)

        @pl.when(pl.program_id(0) == 0)
        def _():
            for z in range(2):
                wrapped = jnp.where(sub == 0, 0.0, pltpu.roll(hl_ref[z].astype(F32)[SUB:], 1, 0))
                hs[z, 0:SUB, :] = jnp.where(k == 0, wrapped, hp_ref[z].astype(F32)[SUB:])
                hs[z, SUB:TC, :] = hf[z, 0:TC - SUB]

        @pl.when(pl.program_id(0) == 1)
        def _():
            for z in range(2):
                wrapped = jnp.where(sub == SUB - 1, 0.0, pltpu.roll(hl_ref[z].astype(F32)[:SUB], SUB - 1, 0))
                hs[z, TC - SUB:TC, :] = jnp.where(k == NCH - 1, wrapped, hp_ref[z].astype(F32)[:SUB])
                hs[z, 0:TC - SUB, :] = hf[z, SUB:TC]

        gr, gi, pr, pi = g_ref[0].astype(F32), g_ref[1].astype(F32), hs[0], hs[1]
        dar = jnp.sum((gr * pr + gi * pi).reshape(NG, SUB, SL), axis=0)
        dai = jnp.sum((gi * pr - gr * pi).reshape(NG, SUB, SL), axis=0)
        ub, dyb = u_ref[...].astype(BF16), dy_ref[...].astype(BF16)
        dm = [lax.dot_general(ub, g_ref[z], TN, preferred_element_type=F32) for z in range(2)]
        dc = [lax.dot_general(dyb, h_ref[z], TN, preferred_element_type=F32) for z in range(2)]

        @pl.when(k == 0)
        def _():
            da_ref[0], da_ref[1] = dar, dai
            for z in range(2):
                dm_ref[z], dc_ref[z] = dm[z], dc[z]

        @pl.when(k > 0)
        def _():
            da_ref[0] += dar
            da_ref[1] += dai
            for z in range(2):
                dm_ref[z] += dm[z]
                dc_ref[z] += dc[z]

    big = pl.BlockSpec((None, 2, TC, SL), lambda d, j, k: (d, 0, k, j))
    tok = pl.BlockSpec((None, TC, CW), lambda d, j, k: (d, k, j))
    mat = pl.BlockSpec((None, 2, None, CW, SL), lambda d, j, k: (d, 0, j, 0, 0))
    outs, couts = _call(
        body, [g, h, h, h, u, dy], name=name, grid=(2, NS, NCH),
        in_specs=[big, big,
                  pl.BlockSpec((None, 2, 2 * SUB, SL), lambda d, j, k: (
                      d, 0, jnp.where(d == 0, jnp.maximum(k * NG - 1, 0), jnp.minimum((k + 1) * NG, T // SUB - 1)) // 2, j)),
                  pl.BlockSpec((None, 2, 2 * SUB, SL), lambda d, j, k: (d, 0, jnp.where(d == 0, T // SUB - 1, 0) // 2, j)),
                  tok, tok],
        out_specs=[mat, mat, pl.BlockSpec((None, 2, SUB, SL), lambda d, j, k: (d, 0, 0, j))],
        out_shape=[jax.ShapeDtypeStruct((2, 2, NS, CW, SL), F32), jax.ShapeDtypeStruct((2, 2, NS, CW, SL), F32),
                   jax.ShapeDtypeStruct((2, 2, SUB, N), F32)],
        scratch_shapes=[pltpu.VMEM((2, TC, SL), F32)],
        sem=("parallel", "parallel", "arbitrary"), comm=comm)
    return outs if comm is None else (outs, couts)


def _interleave(seq):
    *lead, T, W = seq.shape
    n = len(lead)
    return seq.reshape(*lead, S5_SEG, T // S5_SEG, W).swapaxes(n, n + 1).reshape(*lead, T, W)


def _deinterleave(seq):
    *lead, T, W = seq.shape
    n = len(lead)
    return seq.reshape(*lead, T // S5_SEG, S5_SEG, W).swapaxes(n, n + 1).reshape(*lead, T, W)


def _s5_discretize(lam_re, lam_im, log_dt, b_re, b_im):
    dt = jnp.exp(log_dt)[..., None]
    mag = jnp.exp(lam_re * dt)
    a_re = mag * jnp.cos(lam_im * dt)
    a_im = mag * jnp.sin(lam_im * dt)
    den = jnp.square(lam_re) + jnp.square(lam_im)
    f_re = ((a_re - 1.0) * lam_re + a_im * lam_im) / den
    f_im = (a_im * lam_re - (a_re - 1.0) * lam_im) / den
    bb_re = f_re[..., None] * b_re - f_im[..., None] * b_im
    bb_im = f_re[..., None] * b_im + f_im[..., None] * b_re
    return a_re, a_im, bb_re, bb_im


_GPS = S5_STRIP // SSM_STATE


def _blockdiag(t):
    d2, G, P, Cg = t.shape
    t5 = t.reshape(d2, G // _GPS, _GPS, P, Cg).transpose(0, 1, 2, 4, 3)
    m = t5[:, :, :, :, None, :] * jnp.eye(_GPS, dtype=t.dtype)[None, None, :, None, :, None]
    return m.reshape(d2, G // _GPS, _GPS * Cg, _GPS * P)


def _blockdiag_extract(m, Cg, P):
    d2, NS = m.shape[:2]
    m6 = m.reshape(d2, NS, _GPS, Cg, _GPS, P)
    diag = jnp.stack([m6[:, :, i, :, i, :] for i in range(_GPS)], axis=2)
    return diag.transpose(0, 1, 2, 4, 3).reshape(d2, NS * _GPS, P, Cg)


def _cmul(a, b):
    return a[0] * b[0] - a[1] * b[1], a[0] * b[1] + a[1] * b[0]


def _cpow(a, n):
    out, base = None, a
    while n:
        if n & 1:
            out = base if out is None else _cmul(out, base)
        base = _cmul(base, base)
        n >>= 1
    return out


def _segment_carry(fin, apow, rev):
    per_dir = []
    for d in range(2):
        fr, fi = fin[d, 0], fin[d, 1]
        ap = (apow[0][d], apow[1][d])
        cr = ci = jnp.zeros_like(fr[0:1])
        outs = [None] * S5_SEG
        backward = (d == 1) != rev
        for s in (range(S5_SEG - 1, -1, -1) if backward else range(S5_SEG)):
            outs[s] = (cr, ci)
            pr, pi = _cmul(ap, (cr, ci))
            cr, ci = pr + fr[s:s + 1], pi + fi[s:s + 1]
        per_dir.append(jnp.stack([jnp.concatenate([o[0] for o in outs]), jnp.concatenate([o[1] for o in outs])]))
    return jnp.stack(per_dir)


def _coords():
    x, y, c = lax.axis_index("x"), lax.axis_index("y"), lax.axis_index("c")
    others = [(1 - x, y), (x, 1 - y), (1 - x, 1 - y)]
    return x, y, c, 2 * x + y, others


def _comm(name, ins, out_shapes, aliases, n_local, n_remote, plan):
    n_in, n_out = len(ins), len(out_shapes)

    def body(*refs):
        in_refs, out_refs = refs[:n_in], refs[n_in:n_in + n_out]
        send_sems, recv_sems, local_sems = refs[n_in + n_out:]
        x, y, c = lax.axis_index("x"), lax.axis_index("y"), lax.axis_index("c")
        locs, sends, lands = plan(in_refs, out_refs)
        assert len(locs) == n_local and len(sends) == n_remote and len(lands) == n_remote
        local = [pltpu.make_async_copy(s, d, local_sems.at[i]) for i, (s, d) in enumerate(locs)]
        for cp in local:
            cp.start()
        remote = [pltpu.make_async_remote_copy(src_ref=s, dst_ref=d, send_sem=send_sems.at[i], recv_sem=recv_sems.at[i],
                                               device_id=peer, device_id_type=MESH)
                  for i, (s, d, peer) in enumerate(sends)]
        for cp in remote:
            cp.start()
        for i, d in enumerate(lands):
            pltpu.make_async_remote_copy(src_ref=d, dst_ref=d, send_sem=send_sems.at[i], recv_sem=recv_sems.at[i],
                                         device_id=(x, y, c), device_id_type=MESH).wait_recv()
        for cp in remote:
            cp.wait_send()
        for cp in local:
            cp.wait()

    any_spec = pl.BlockSpec(memory_space=pl.ANY)
    return pl.pallas_call(
        body, name=name,
        in_specs=[any_spec] * n_in, out_specs=[any_spec] * n_out,
        out_shape=[jax.ShapeDtypeStruct(s, d) for s, d in out_shapes],
        input_output_aliases=aliases,
        scratch_shapes=[pltpu.SemaphoreType.DMA((n_remote,)), pltpu.SemaphoreType.DMA((n_remote,)),
                        pltpu.SemaphoreType.DMA((max(n_local, 1),))],
        compiler_params=pltpu.CompilerParams(has_side_effects=True),
    )(*ins)


def allgather_dev(name, v):
    M, Nc = v.shape

    def plan(in_refs, out_refs):
        (v_ref,), (o_ref,) = in_refs, out_refs
        x, y, c = lax.axis_index("x"), lax.axis_index("y"), lax.axis_index("c")

        def rows(px, py, pc):
            return o_ref.at[pl.ds((4 * px + 2 * py + pc) * M, M), :]

        peers = [(x ^ fx, y ^ fy, c ^ fc) for fx in (0, 1) for fy in (0, 1) for fc in (0, 1) if fx or fy or fc]
        return ([(v_ref, rows(x, y, c))],
                [(v_ref, rows(x, y, c), p) for p in peers],
                [rows(*p) for p in peers])

    return _comm(name, [v], [((N_DEV * M, Nc), v.dtype)], {}, 1, N_DEV - 1, plan)[0]


def allgather_chips_1(name, shards):
    def plan(in_refs, out_refs):
        x, y, c, chip, others = _coords()
        sends, lands = [], []
        for s_ref, g_ref in zip(in_refs, out_refs):
            hr = s_ref.shape[0] // 2
            mine = pl.ds(c * hr, hr)
            for qx, qy in others:
                sends.append((s_ref.at[mine], g_ref.at[chip, mine], (qx, qy, c)))
                lands.append(g_ref.at[2 * qx + qy, mine])
        return [], sends, lands

    n = len(shards)
    comm = (list(shards), [((N_CHIP,) + s.shape, s.dtype) for s in shards], {}, 3 * n, plan)
    return comm if name is None else _comm(name, comm[0], comm[1], comm[2], 0, comm[3], comm[4])


def allgather_chips_2(name, gathered, shards):
    n = len(gathered)

    def plan(in_refs, out_refs):
        x, y, c, chip, others = _coords()
        sends, lands = [], []
        for s_ref, g_ref in zip(in_refs[n:], out_refs):
            hr = g_ref.shape[1] // 2
            for qx, qy in others:
                q = 2 * qx + qy
                sends.append((g_ref.at[q, pl.ds(c * hr, hr)], g_ref.at[q, pl.ds(c * hr, hr)], (x, y, 1 - c)))
                lands.append(g_ref.at[q, pl.ds((1 - c) * hr, hr)])
            sends.append((s_ref, g_ref.at[chip], (x, y, 1 - c)))
            lands.append(g_ref.at[chip])
        return [], sends, lands

    comm = (list(gathered) + list(shards), [(g.shape, g.dtype) for g in gathered], {i: i for i in range(n)}, 4 * n, plan)
    return comm if name is None else _comm(name, comm[0], comm[1], comm[2], 0, comm[3], comm[4])


def reduce_1(name, grads):
    def plan(in_refs, out_refs):
        x, y, c, chip, others = _coords()
        sends, lands = [], []
        for g_ref, got_ref in zip(in_refs, out_refs):
            hr = g_ref.shape[1] // 2
            sends.append((g_ref.at[:, pl.ds((1 - c) * hr, hr), :], got_ref, (x, y, 1 - c)))
            lands.append(got_ref)
        return [], sends, lands

    n = len(grads)
    comm = (list(grads), [((g.shape[0], g.shape[1] // 2, g.shape[2]), g.dtype) for g in grads], {}, n, plan)
    return comm if name is None else _comm(name, comm[0], comm[1], comm[2], 0, comm[3], comm[4])


def _merge_comm(a, b):
    if a is None or b is None:
        return a if b is None else b
    na_in, na_out = len(a[0]), len(a[1])

    def plan(in_refs, out_refs):
        _, s1, l1 = a[4](in_refs[:na_in], out_refs[:na_out])
        _, s2, l2 = b[4](in_refs[na_in:], out_refs[na_out:])
        return [], s1 + s2, l1 + l2

    alias = dict(a[2])
    alias.update({na_in + i: na_out + j for i, j in b[2].items()})
    return (a[0] + b[0], a[1] + b[1], alias, a[3] + b[3], plan)


def reduce_2(name, parts):
    def plan(in_refs, out_refs):
        x, y, c, chip, others = _coords()
        sends, lands = [], []
        for t_ref, q_ref in zip(in_refs, out_refs):
            for qx, qy in others:
                sends.append((t_ref.at[2 * qx + qy], q_ref.at[chip], (qx, qy, c)))
                lands.append(q_ref.at[2 * qx + qy])
        return [], sends, lands

    n = len(parts)
    comm = (list(parts), [(p.shape, p.dtype) for p in parts], {}, 3 * n, plan)
    return comm if name is None else _comm(name, comm[0], comm[1], comm[2], 0, comm[3], comm[4])


def share_slots(name, slots):
    def plan(in_refs, out_refs):
        x, y, c, chip, others = _coords()
        (q_ref,) = out_refs
        return ([], [(q_ref.at[chip], q_ref.at[chip], (qx, qy, c)) for qx, qy in others],
                [q_ref.at[2 * qx + qy] for qx, qy in others])

    return _comm(name, [slots], [(slots.shape, slots.dtype)], {0: 0}, 0, N_CHIP - 1, plan)[0]


def allreduce_small(tag, buf, ids):
    got = reduce_1(tag + "_1", [buf[None]])[0][0]
    slots = share_slots(tag + "_2", pair_sum_to_slot(tag + "_add", buf, got, ids))
    full = reduce_3(tag + "_3", [sum_chips_to_half(tag + "_sum", slots, ids)])[0]
    return full.reshape(buf.shape)


def reduce_3(name, fulls):
    def plan(in_refs, out_refs):
        x, y, c, chip, others = _coords()
        sends, lands = [], []
        for o_ref in out_refs:
            sends.append((o_ref.at[c], o_ref.at[c], (x, y, 1 - c)))
            lands.append(o_ref.at[1 - c])
        return [], sends, lands

    n = len(fulls)
    return _comm(name, fulls, [(f.shape, f.dtype) for f in fulls], {i: i for i in range(n)}, 0, n, plan)


_WEIGHTS = ['c_ctx', 'w_mod', 'b_mod', 'g_mix', 'g_ffn', 'w_in', 'ssm_lam_re', 'ssm_lam_im', 'ssm_log_dt', 'ssm_b_re',
            'ssm_b_im', 'ssm_c_re', 'ssm_c_im', 'ssm_d', 'ssm_w_glu', 'na_rpb', 'w_out', 'cv_w_pw1', 'cv_dw_w', 'cv_dw_b',
            'cv_ln_g', 'cv_ln_b', 'cv_w_pw2', 'ffn_w_up', 'ffn_conv_w', 'ffn_conv_b', 'ffn_w_down', 'g_out']
_INPUTS = ['x', 'c', 'ctx'] + _WEIGHTS + ['loss_target'] + ['m_' + w for w in _WEIGHTS] + ['v_' + w for w in _WEIGHTS]
_GATHERED_SMALL = ['ffn_conv_w', 'cv_dw_w', 'cv_dw_b', 'cv_ln_g', 'cv_ln_b']


def _silu(v):
    return v * jax.nn.sigmoid(v)


def _pick_index(t, idx, axis):
    shape = [1] * t.ndim
    shape[axis] = t.shape[axis]
    mask = (jnp.arange(t.shape[axis]) == idx).reshape(shape)
    return jnp.sum(jnp.where(mask, t, jnp.zeros((), t.dtype)), axis=axis)


def _pack(arrs, cols, row_mult=SUB):
    flat = jnp.concatenate([a.reshape(-1).astype(F32) for a in arrs])
    n = flat.shape[0]
    unit = row_mult * cols
    flat = jnp.pad(flat, (0, (-n) % unit))
    return flat.reshape(-1, cols)


def _unpack(buf, shapes):
    flat = buf.reshape(-1)
    out, o = [], 0
    for s in shapes:
        n = int(np.prod(s))
        out.append(flat[o:o + n].reshape(s))
        o += n
    return out


def _carried(res, comm):
    return res if comm is not None else (res, [])


def _ffn_fwd(tag, xin, sh, sc, gt, g, wup, cw3, cb3, wdn, comm_up=None, comm_mid=None, comm_down=None):
    hf = norm_mod_fwd(tag + "_norm", xin, g * (1.0 + sc), sh)
    up3, got_up = _carried(mm_nn_pieces(tag + "_up", hf, wup, 0, N_CHIP, BF16, halves=2, comm=comm_up), comm_up)
    comm_mid = comm_mid(got_up) if callable(comm_mid) else comm_mid
    act, got_mid = _carried(ffn_mid_fwd(tag + "_mid", up3, cw3, cb3, comm=comm_mid), comm_mid)
    comm_down = comm_down(got_mid) if callable(comm_down) else comm_down
    yf, got_down = _carried(mm_nn(tag + "_down", act, wdn, BF16, comm=comm_down), comm_down)
    return gate_res_fwd(tag + "_res", xin, yf, gt), (xin, hf, up3, act, yf), got_up, got_mid, got_down


def _ffn_bwd(tag, dxo, saved, sc, gt, g, wup, cw3, cb3, wdn, comm_down=None, comm_mid=None, comm_up=None):
    xin, hf, up3, act, yf = saved
    dyf, dgt = gate_res_bwd(tag + "_res_b", dxo, yf, gt)
    dact, got_down = _carried(mm_nt(tag + "_down_bx", dyf, wdn, BF16, comm=comm_down), comm_down)
    dwdn = mm_tn(tag + "_down_bw", act, dyf, BF16)
    comm_mid = comm_mid(got_down) if callable(comm_mid) else comm_mid
    (dup3, dcw3, dcb3), got_mid = _carried(ffn_mid_bwd(tag + "_mid_b", up3, dact, cw3, cb3, comm=comm_mid), comm_mid)
    dhf, got_up = _carried(mm_nt_pieces(tag + "_up_bx", dup3, wup, BF16, halves=2, comm=comm_up), comm_up)
    dwup = mm_tn_pieces(tag + "_up_bw", hf, dup3, N_CHIP, BF16, halves=2)
    dxi, cs1, cs2 = norm_mod_bwd(tag + "_norm_b", xin, dhf, g * (1.0 + sc), dxo)
    return dxi, dict(dsh=cs1[0], dsc=cs2[0] * g, dgt=dgt[0], dg=cs2[0] * (1.0 + sc), dwup=dwup, dwdn=dwdn,
                     dcw=dcw3.transpose(1, 0, 2).reshape(3, -1), dcb=dcb3.reshape(-1)), got_down, got_mid, got_up


def kernel(x, c, ctx, c_ctx, w_mod, b_mod, g_mix, g_ffn, w_in, ssm_lam_re, ssm_lam_im, ssm_log_dt, ssm_b_re, ssm_b_im, ssm_c_re, ssm_c_im, ssm_d, ssm_w_glu, na_rpb, w_out, cv_w_pw1, cv_dw_w, cv_dw_b, cv_ln_g, cv_ln_b, cv_w_pw2, ffn_w_up, ffn_conv_w, ffn_conv_b, ffn_w_down, g_out, loss_target, m_c_ctx, m_w_mod, m_b_mod, m_g_mix, m_g_ffn, m_w_in, m_ssm_lam_re, m_ssm_lam_im, m_ssm_log_dt, m_ssm_b_re, m_ssm_b_im, m_ssm_c_re, m_ssm_c_im, m_ssm_d, m_ssm_w_glu, m_na_rpb, m_w_out, m_cv_w_pw1, m_cv_dw_w, m_cv_dw_b, m_cv_ln_g, m_cv_ln_b, m_cv_w_pw2, m_ffn_w_up, m_ffn_conv_w, m_ffn_conv_b, m_ffn_w_down, m_g_out, v_c_ctx, v_w_mod, v_b_mod, v_g_mix, v_g_ffn, v_w_in, v_ssm_lam_re, v_ssm_lam_im, v_ssm_log_dt, v_ssm_b_re, v_ssm_b_im, v_ssm_c_re, v_ssm_c_im, v_ssm_d, v_ssm_w_glu, v_na_rpb, v_w_out, v_cv_w_pw1, v_cv_dw_w, v_cv_dw_b, v_cv_ln_g, v_cv_ln_b, v_cv_w_pw2, v_ffn_w_up, v_ffn_conv_w, v_ffn_conv_b, v_ffn_w_down, v_g_out):
    p = dict(locals())
    xi, yi, ci = lax.axis_index("x"), lax.axis_index("y"), lax.axis_index("c")
    me, chip = 4 * xi + 2 * yi + ci, 2 * xi + yi
    xs, cx, tgt = x[0], ctx[0], loss_target[0]
    L, D = xs.shape
    Lc = cx.shape[0]
    T = L + Lc
    W = D // 2
    Cq = w_mod.shape[2]

    s_mix = [t.astype(BF16) for t in (w_in[0], ssm_w_glu[0], w_out[0])]
    s_ffn0 = [t.astype(BF16) for t in (ffn_w_up[0], ffn_w_down[0])]
    s_conv = [t.astype(BF16) for t in (cv_w_pw1[0], cv_w_pw2[0])]
    s_ffn1 = [t.astype(BF16) for t in (ffn_w_up[1], ffn_w_down[1])]
    (Win,) = allgather_chips_2("gather_win_2", allgather_chips_1("gather_win_1", s_mix[:1]), s_mix[:1])
    Fd = ffn_w_down.shape[1] * N_CHIP
    c_idx = jnp.reshape(ci, (1,)).astype(jnp.int32)
    ids = jnp.stack([chip, ci]).astype(jnp.int32)

    def added(tag, grads, got):
        return [add_half("reduce_%s_add%d" % (tag, i), g, r, c_idx) for i, (g, r) in enumerate(zip(grads, got))]

    small_shapes = [p[n].shape for n in _GATHERED_SMALL]
    sm = allgather_dev("gather_small", _pack([p[n] for n in _GATHERED_SMALL], 1024))
    sm = sm.reshape(N_DEV, -1)[0::2]
    per_chip = [_unpack(sm[q], small_shapes) for q in range(N_CHIP)]
    conv_w_f, dw_w_f, dw_b_f, ln_g_f, ln_b_f = (jnp.concatenate([pc[i] for pc in per_chip], axis=-1)
                                                for i in range(len(_GATHERED_SMALL)))
    cw3 = [conv_w_f[l].reshape(3, 2, Fd).transpose(1, 0, 2) for l in range(2)]
    cb3 = [ffn_conv_b[l].reshape(2, 1, Fd) for l in range(2)]
    dw_w_f, dw_b_f, ln_g_f, ln_b_f = dw_w_f[0], dw_b_f[0], ln_g_f[0], ln_b_f[0]

    c_all = allgather_dev("gather_c", jnp.zeros((SUB, D), F32).at[0].set(c[0])).reshape(N_DEV, SUB, D)[:, 0]
    S16 = jnp.concatenate([_silu(c_all), _silu(c_ctx)[None], jnp.zeros((2 * SUB - N_DEV - 1, D), F32)])
    modp = mm_nn_pieces("mod_fwd", S16, w_mod, 0, 2, F32)
    modg = allgather_dev("gather_mod", modp).reshape(N_DEV, 2 * SUB, 2, Cq)[0::2]

    def mod_row(r):
        return r.transpose(1, 0, 2).reshape(2, N_CHIP * Cq) + b_mod

    mod_me = mod_row(_pick_index(modg, me, 1))
    mod_c = mod_row(modg[:, N_DEV])
    mods = [[mod_me[l, i * D:(i + 1) * D] for i in range(N_MOD)] for l in range(2)]
    shc, scc = mod_c[0, :D], mod_c[0, D:2 * D]

    sh_m, sc_m, gt_m, sh_f, sc_f, gt_f = mods[0]
    h0 = norm_mod_fwd("l0_norm", xs, g_mix[0] * (1.0 + sc_m), sh_m)
    hc0 = norm_mod_fwd("l0_norm_c", cx, g_mix[0] * (1.0 + scc), shc)
    u = mm_nn_pieces("l0_in_u", h0, Win, 0, 1, F32)
    qkv, g_mix1 = mm_nn_pieces("l0_in_qkv", h0, Win, 1, 3, BF16, comm=allgather_chips_1(None, s_mix[1:]))
    uc = mm_nn_pieces("l0_in_uc", hc0, Win, 0, 1, F32)
    kvc = mm_nn_pieces("l0_in_kvc", hc0, Win, 2, 2, BF16)

    lam_re, lam_im, log_dt = ssm_lam_re[0], ssm_lam_im[0], ssm_log_dt[0]
    b_re, b_im, c_re, c_im = ssm_b_re[0], ssm_b_im[0], ssm_c_re[0], ssm_c_im[0]
    (a_re, a_im, bb_re, bb_im), disc_vjp = jax.vjp(_s5_discretize, lam_re, lam_im, log_dt, b_re, b_im)
    G, P, Cg = bb_re.shape[1:]
    N = G * P
    a_re, a_im = a_re.reshape(2, 1, N), a_im.reshape(2, 1, N)
    a_f, a_b = jnp.stack([a_re, a_im], axis=1), jnp.stack([a_re, -a_im], axis=1)
    Bblk = jnp.stack([_blockdiag(bb_re), _blockdiag(bb_im)], axis=1)
    Cblk = jnp.stack([_blockdiag(c_re.swapaxes(-1, -2)), -_blockdiag(c_im.swapaxes(-1, -2))], axis=1)
    apow = _cpow((a_re, a_im), T // S5_SEG)

    useq = _interleave(jnp.stack([jnp.concatenate([uc, u]), jnp.concatenate([u, uc])]).astype(BF16))
    (hloc, fin), (Wglu, Wout) = s5_scan("s5_scan", useq, Bblk.astype(BF16), a_f, rev=False,
                                        comm=allgather_chips_2(None, g_mix1, s_mix[1:]))
    Wglu, Wout = Wglu.reshape(-1, Wglu.shape[-1]), Wout.reshape(-1, D)
    (hst, yseq), g_dn0 = s5_fix("s5_fix", hloc, _segment_carry(fin, apow, False), a_f, Cblk.swapaxes(-1, -2).astype(BF16),
                                rev=False, comm=allgather_chips_1(None, s_ffn0[1:]))
    ys = _deinterleave(yseq)
    y0, y1 = ys[0, Lc:], ys[1, :L]
    s5o = glu_fwd("s5_glu", u, y0, y1, ssm_d[0], Wglu)

    bias = na_bias(na_rpb[0])
    (o_na, lse), (g_up0, Wdn0) = natten_fwd(
        "na_fwd", qkv, kvc, bias,
        comm=_merge_comm(allgather_chips_1(None, s_ffn0[:1]), allgather_chips_2(None, g_dn0, s_ffn0[1:])))
    mixcat = jnp.concatenate([s5o, o_na], axis=1)
    ymix, (Wup0,) = mm_nn("l0_out", mixcat, Wout, BF16, comm=allgather_chips_2(None, [g_up0], s_ffn0[:1]))
    x1 = gate_res_fwd("l0_res", xs, ymix, gt_m)
    Wdn0 = Wdn0.reshape(-1, D)
    x2, ffn0, (g_up1,), (g_dn1, Wup1), (g_pw1, g_pw2, Wdn1) = _ffn_fwd(
        "f0", x1, sh_f, sc_f, gt_f, g_ffn[0], Wup0, cw3[0], cb3[0], Wdn0,
        comm_up=allgather_chips_1(None, s_ffn1[:1]),
        comm_mid=lambda got_up: _merge_comm(allgather_chips_1(None, s_ffn1[1:]), allgather_chips_2(None, got_up, s_ffn1[:1])),
        comm_down=lambda got_mid: _merge_comm(allgather_chips_1(None, s_conv), allgather_chips_2(None, got_mid[:1], s_ffn1[1:])))
    Wpw1, Wpw2 = allgather_chips_2("gather_conv_2", [g_pw1, g_pw2], s_conv)
    Wpw2 = Wpw2.reshape(-1, D)
    Wup, Wdn = [Wup0, Wup1], [Wdn0.reshape(-1, D), Wdn1.reshape(-1, D)]

    sh_v, sc_v, gt_v, sh_g, sc_g, gt_g = mods[1]
    hcv = norm_mod_fwd("l1_norm", x2, g_mix[1] * (1.0 + sc_v), sh_v)
    ag3 = mm_nn_pieces("l1_pw1", hcv, Wpw1, 0, N_CHIP, BF16, halves=2)
    z1, z3 = conf_mid_fwd("l1_mid", ag3, dw_w_f, dw_b_f, ln_g_f, ln_b_f)
    ycv = mm_nn("l1_pw2", z3, Wpw2, BF16)
    x3 = gate_res_fwd("l1_res", x2, ycv, gt_v)
    x4, ffn1, _, _, _ = _ffn_fwd("f1", x3, sh_g, sc_g, gt_g, g_ffn[1], Wup[1], cw3[1], cb3[1], Wdn[1])

    dx4, dg_out, loss_part = loss_head("loss", x4, g_out, tgt)
    loss = lax.psum(loss_part[0, 0], ("x", "y", "c"))

    dx3, gf1, _, _, _ = _ffn_bwd("f1", dx4, ffn1, sc_g, gt_g, g_ffn[1], Wup[1], cw3[1], cb3[1], Wdn[1])
    g_up1, g_dn1 = [gf1["dwup"]], [gf1["dwdn"].reshape(N_CHIP, -1, D)]
    dycv, dgt_v = gate_res_bwd("l1_res_b", dx3, ycv, gt_v)
    dz3, got = mm_nt("l1_pw2_bx", dycv, Wpw2, BF16, comm=reduce_1(None, g_up1))
    parts_up1 = added("up1", g_up1, got)
    dWpw2, got = mm_tn("l1_pw2_bw", z3, dycv, BF16, comm=reduce_1(None, g_dn1))
    parts_dn1 = added("dn1", g_dn1, got)
    dz1, dln_g, dln_b = conf_ln_bwd("l1_ln_b", z1, dz3, ln_g_f, ln_b_f)
    (dag3, ddw_w, ddw_b), slots_up1 = conf_conv_bwd("l1_conv_b", ag3, dz1, dw_w_f, comm=reduce_2(None, parts_up1))
    dhcv = mm_nt_pieces("l1_pw1_bx", dag3, Wpw1, BF16, halves=2)
    dWpw1 = mm_tn_pieces("l1_pw1_bw", hcv, dag3, N_CHIP, BF16, halves=2)
    dx2, cs1_v, cs2_v = norm_mod_bwd("l1_norm_b", x2, dhcv, g_mix[1] * (1.0 + sc_v), dx3)
    g_conv = [dWpw1, dWpw2.reshape(N_CHIP, -1, D)]

    held = {}

    def conv_stage_2(got_down):
        held["parts_conv"] = added("conv", g_conv, got_down)
        return _merge_comm(reduce_2(None, held["parts_conv"]), reduce_2(None, parts_dn1))

    dx1, gf0, _, slots_mid, _ = _ffn_bwd("f0", dx2, ffn0, sc_f, gt_f, g_ffn[0], Wup[0], cw3[0], cb3[0], Wdn[0],
                                          comm_down=reduce_1(None, g_conv), comm_mid=conv_stage_2)
    slots_conv, slots_dn1 = slots_mid[:2], slots_mid[2:]
    parts_conv = held["parts_conv"]
    g_up0, g_dn0 = [gf0["dwup"]], [gf0["dwdn"].reshape(N_CHIP, -1, D)]
    dymix, dgt_m = gate_res_bwd("l0_res_b", dx1, ymix, gt_m)
    dmix, got = mm_nt("l0_out_bx", dymix, Wout, BF16, comm=reduce_1(None, g_up0))
    parts_up0 = added("up0", g_up0, got)
    dWout, got = mm_tn("l0_out_bw", mixcat, dymix, BF16, comm=reduce_1(None, g_dn0))
    parts_dn0 = added("dn0", g_dn0, got)
    (dq, dk, dv, dkc, dvc, dbias), slots_up0 = natten_bwd("na_bwd", qkv, kvc, bias, o_na, lse, dmix,
                                                          comm=reduce_2(None, parts_up0))
    dy, zg, dzz, dd_skip = glu_bwd("s5_glu_b", u, y0, y1, ssm_d[0], Wglu, dmix)
    dWglu = mm_tn("s5_glu_bw", zg, dzz, BF16)
    g_mix2 = [dWglu.reshape(N_CHIP, -1, W), dWout.reshape(N_CHIP, -1, D)]

    zc = jnp.zeros((Lc, W), F32)
    dyseq = _interleave(jnp.stack([jnp.concatenate([zc, dy]), jnp.concatenate([dy, zc])]).astype(BF16))
    (gloc, gfin), slots_dn0 = s5_scan("s5_scan_b", dyseq, Cblk.astype(BF16), a_b, rev=True, comm=reduce_2(None, parts_dn0))
    apow_b = (apow[0], -apow[1])
    (gst, duseq), got = s5_fix("s5_fix_b", gloc, _segment_carry(gfin, apow_b, True), a_b, Bblk.swapaxes(-1, -2).astype(BF16),
                               rev=True, comm=reduce_1(None, g_mix2))
    parts_mix2 = added("mix2", g_mix2, got)
    (dBm, dCm, da8), slots_mix2 = s5_grads("s5_grads", gst, hst, useq, dyseq, comm=reduce_2(None, parts_mix2))
    dus = _deinterleave(duseq)
    du = fma3("s5_du", dy, dus[0, Lc:], dus[1, :L], ssm_d[0], BF16)
    duc = dus[0, :Lc] + dus[1, L:]

    d_in = [du, dq, dk, dv]
    d_in_c = [duc, jnp.zeros((Lc, W), BF16), dkc, dvc]
    dh0 = mm_nt_list("l0_in_bx", d_in, Win, BF16)
    dhc0 = mm_nt_list("l0_in_bxc", d_in_c, Win, BF16)
    h_all = jnp.concatenate([hc0, h0])
    dWin = jnp.stack([mm_tn("l0_in_bw%d" % q, h_all, jnp.concatenate([dc.astype(BF16), dl.astype(BF16)]), BF16)
                      for q, (dc, dl) in enumerate(zip(d_in_c, d_in))])
    dx0, cs1_m, cs2_m = norm_mod_bwd("l0_norm_b", xs, dh0, g_mix[0] * (1.0 + sc_m), dx1)
    _, cs1_c, cs2_c = norm_mod_bwd("l0_norm_bc", cx, dhc0, g_mix[0] * (1.0 + scc), jnp.zeros_like(cx))

    dmod0 = jnp.concatenate([cs1_m[0], cs2_m[0] * g_mix[0], dgt_m[0], gf0["dsh"], gf0["dsc"], gf0["dgt"]])
    dmod1 = jnp.concatenate([cs1_v[0], cs2_v[0] * g_mix[1], dgt_v[0], gf1["dsh"], gf1["dsc"], gf1["dgt"]])
    dmodc = jnp.concatenate([cs1_c[0], cs2_c[0] * g_mix[0], jnp.zeros((4 * D,), F32)])
    dm_rows = jnp.concatenate([jnp.stack([dmod0, dmod1, dmodc]), jnp.zeros((SUB - 3, N_MOD * D), F32)])
    dm_all = allgather_dev("gather_dmod", dm_rows).reshape(N_DEV, SUB, N_MOD * D)
    dm_sum = sum_lead("sum_dmod", dm_all, F32)
    pad7 = jnp.zeros((2 * SUB - N_DEV - 1, N_MOD * D), F32)
    dMod = [jnp.concatenate([dm_all[:, 0], dm_sum[2:3], pad7]), jnp.concatenate([dm_all[:, 1], jnp.zeros_like(dm_sum[2:3]), pad7])]
    dMod_cols = [_pick_index(m.reshape(m.shape[0], N_CHIP, Cq), chip, 1) for m in dMod]
    g_w_mod = jnp.stack([mm_tn("mod_bw%d" % l, S16, dMod_cols[l], F32) for l in range(2)])
    g_b_mod = jnp.stack([dm_sum[0] + dm_sum[2], dm_sum[1]])
    ds_part = mm_nt("mod_bx", dMod_cols[0], w_mod[0], F32)
    ds_all = allgather_dev("gather_dsc", jnp.zeros((SUB, D), F32).at[0].set(ds_part[N_DEV]))
    ds_c = sum_lead("sum_dsc", ds_all.reshape(N_DEV, SUB, D)[0::2], F32)[0]
    sg_c = jax.nn.sigmoid(c_ctx)
    g_c_ctx = ds_c * sg_c * (1.0 + c_ctx * (1.0 - sg_c))

    g_rpb_loc = na_bias_grad(dbias)

    dbb = [_blockdiag_extract(dBm[:, z], Cg, P) for z in range(2)]
    dcc = [_blockdiag_extract(dCm[:, z], Cg, P).swapaxes(-1, -2) for z in range(2)]
    da = jnp.sum(da8, axis=2).reshape(2, 2, G, P)
    small = {
        "g_mix": jnp.stack([cs2_m[0] * (1.0 + sc_m) + cs2_c[0] * (1.0 + scc), cs2_v[0] * (1.0 + sc_v)]),
        "g_ffn": jnp.stack([gf0["dg"], gf1["dg"]]),
        "a_re": da[:, 0], "a_im": da[:, 1], "bb_re": dbb[0], "bb_im": dbb[1], "c_re": dcc[0], "c_im": -dcc[1],
        "ssm_d": dd_skip, "na_rpb": g_rpb_loc, "cv_dw_w": ddw_w, "cv_dw_b": ddw_b, "cv_ln_g": dln_g, "cv_ln_b": dln_b,
        "ffn_conv_w": jnp.stack([gf0["dcw"], gf1["dcw"]]), "ffn_conv_b": jnp.stack([gf0["dcb"], gf1["dcb"]]),
        "g_out": dg_out,
    }
    skeys = list(small)
    sbuf = _pack([small[k] for k in skeys], 1024, 4 * SUB)
    ssum = dict(zip(skeys, _unpack(allreduce_small("reduce_small", sbuf, ids), [small[k].shape for k in skeys])))
    g_lam_re, g_lam_im, g_log_dt, g_b_re, g_b_im = disc_vjp((ssum["a_re"], ssum["a_im"], ssum["bb_re"], ssum["bb_im"]))

    def my_cols(t):
        n = t.shape[-1] // N_CHIP
        return _pick_index(t.reshape(t.shape[:-1] + (N_CHIP, n)), chip, t.ndim - 1)

    delta, new_m, new_v = {}, {}, {}
    parts_win = added("win", [dWin], reduce_1("reduce_win_1", [dWin]))
    (delta["w_mod"], new_m["w_mod"], new_v["w_mod"]), slots_win = adamw(
        "adamw_w_mod", w_mod, g_w_mod, m_w_mod, v_w_mod, comm=reduce_2(None, parts_win))
    parts = parts_win + parts_mix2 + parts_conv + parts_up0 + parts_dn0 + parts_up1 + parts_dn1
    slots = [*slots_win, *slots_mix2, *slots_conv, *slots_up0, *slots_dn0, *slots_up1, *slots_dn1]
    fulls = [sum_slots("reduce_sum_%d" % i, s, t, ids) for i, (s, t) in enumerate(zip(slots, parts))]
    full = [f.reshape(-1, f.shape[-1]) for f in reduce_3("reduce_g_3", fulls)]
    gWin, gWglu, gWout, gWpw1, gWpw2, gWup0, gWdn0, gWup1, gWdn1 = full

    grads = {
        "c_ctx": g_c_ctx, "w_mod": g_w_mod, "b_mod": g_b_mod, "g_mix": ssum["g_mix"], "g_ffn": ssum["g_ffn"],
        "w_in": gWin[None], "ssm_lam_re": g_lam_re[None], "ssm_lam_im": g_lam_im[None], "ssm_log_dt": g_log_dt[None],
        "ssm_b_re": g_b_re[None], "ssm_b_im": g_b_im[None], "ssm_c_re": ssum["c_re"][None], "ssm_c_im": ssum["c_im"][None],
        "ssm_d": ssum["ssm_d"], "ssm_w_glu": gWglu[None], "na_rpb": ssum["na_rpb"][None], "w_out": gWout[None],
        "cv_w_pw1": gWpw1[None], "cv_dw_w": my_cols(ssum["cv_dw_w"])[None], "cv_dw_b": my_cols(ssum["cv_dw_b"]),
        "cv_ln_g": my_cols(ssum["cv_ln_g"]), "cv_ln_b": my_cols(ssum["cv_ln_b"]), "cv_w_pw2": gWpw2[None],
        "ffn_w_up": jnp.stack([gWup0, gWup1]), "ffn_conv_w": my_cols(ssum["ffn_conv_w"]), "ffn_conv_b": ssum["ffn_conv_b"],
        "ffn_w_down": jnp.stack([gWdn0, gWdn1]), "g_out": ssum["g_out"][0],
    }
    grads = {k: grads[k].reshape(p[k].shape) for k in _WEIGHTS}

    large = [k for k in _WEIGHTS if p[k].size >= (1 << 18) or k == "w_mod"]
    tiny = [k for k in _WEIGHTS if k not in large]
    for k in large:
        if k != "w_mod":
            delta[k], new_m[k], new_v[k] = adamw("adamw_" + k, p[k], grads[k], p["m_" + k], p["v_" + k])
    packs = [_pack([src[pre + k] for k in tiny], 1024) for src, pre in ((p, ""), (grads, ""), (p, "m_"), (p, "v_"))]
    outs = adamw("adamw_small", *packs)
    shapes = [p[k].shape for k in tiny]
    for dst, buf in zip((delta, new_m, new_v), outs):
        dst.update(zip(tiny, _unpack(buf, shapes)))

    return (loss, dx0[None], *[grads[k] for k in _WEIGHTS], *[delta[k] for k in _WEIGHTS],
            *[new_m[k] for k in _WEIGHTS], *[new_v[k] for k in _WEIGHTS])
```

```python
import functools
import math

import numpy as np
import jax
import jax.numpy as jnp
from jax import lax
from jax.experimental import pallas as pl
from jax.experimental.pallas import tpu as pltpu

F32, BF16 = jnp.float32, jnp.bfloat16
MESH = pl.DeviceIdType.MESH
V7X_VMEM_LIMIT = 56 << 20
LANE, SUB = 128, 8
N_CHIP, N_DEV = 4, 8

GRID_W = 64
N_MOD = 6
SSM_GROUP, SSM_STATE = 16, 64
NA_HEAD_DIM, NA_WIN_R, NA_WIN_C = 128, 8, 16
EPS = 1e-6
NEG = -1e30
ADAM_LR, ADAM_B1, ADAM_B2, ADAM_EPS, ADAM_WD, ADAM_STEP = 0.001, 0.9, 0.999, 1e-08, 0.01, 10
S5_STRIP = 512
S5_SEG = 8

NN = (((1,), (0,)), ((), ()))
NT = (((1,), (1,)), ((), ()))
TN = (((0,), (0,)), ((), ()))


def _params(*sem, side_effects=False):
    return pltpu.CompilerParams(dimension_semantics=sem if sem else None, vmem_limit_bytes=V7X_VMEM_LIMIT,
                                has_side_effects=side_effects)


def _call(body, args, *, name, grid, in_specs, out_specs, out_shape, sem, scratch_shapes=(), comm=None):
    out_specs, out_shape, scratch_shapes = list(out_specs), list(out_shape), list(scratch_shapes)
    if comm is None:
        outs = pl.pallas_call(body, name=name, grid=grid, in_specs=list(in_specs), out_specs=out_specs, out_shape=out_shape,
                              scratch_shapes=scratch_shapes, compiler_params=_params(*sem))(*args)
        return list(outs), []
    c_args, c_shapes, c_alias, n_remote, plan = comm
    n_in, n_out, n_ci, n_co, n_sc = len(args), len(out_shape), len(c_args), len(c_shapes), len(scratch_shapes)

    def wrapped(*refs):
        ins, cins = refs[:n_in], refs[n_in:n_in + n_ci]
        o0 = n_in + n_ci
        outs, couts = refs[o0:o0 + n_out], refs[o0 + n_out:o0 + n_out + n_co]
        s0 = o0 + n_out + n_co
        scr, (send_sems, recv_sems) = refs[s0:s0 + n_sc], refs[s0 + n_sc:]
        pids = [pl.program_id(a) for a in range(len(grid))]
        first = functools.reduce(jnp.logical_and, [q == 0 for q in pids])
        last = functools.reduce(jnp.logical_and, [q == g - 1 for q, g in zip(pids, grid)])
        me = (lax.axis_index("x"), lax.axis_index("y"), lax.axis_index("c"))

        def copies():
            _, sends, lands = plan(cins, couts)
            assert len(sends) == n_remote and len(lands) == n_remote
            out = [pltpu.make_async_remote_copy(src_ref=s, dst_ref=d, send_sem=send_sems.at[i], recv_sem=recv_sems.at[i],
                                                device_id=peer, device_id_type=MESH) for i, (s, d, peer) in enumerate(sends)]
            arrivals = [pltpu.make_async_remote_copy(src_ref=d, dst_ref=d, send_sem=send_sems.at[i], recv_sem=recv_sems.at[i],
                                                     device_id=me, device_id_type=MESH) for i, d in enumerate(lands)]
            return out, arrivals

        @pl.when(first)
        def _():
            for cp in copies()[0]:
                cp.start()

        body(*ins, *outs, *scr)

        @pl.when(last)
        def _():
            out, arrivals = copies()
            for cp in arrivals:
                cp.wait_recv()
            for cp in out:
                cp.wait_send()

    any_spec = pl.BlockSpec(memory_space=pl.ANY)
    res = pl.pallas_call(
        wrapped, name=name, grid=grid,
        in_specs=[*in_specs, *[any_spec] * n_ci], out_specs=[*out_specs, *[any_spec] * n_co],
        out_shape=[*out_shape, *[jax.ShapeDtypeStruct(s, d) for s, d in c_shapes]],
        input_output_aliases={n_in + i: n_out + j for i, j in c_alias.items()},
        scratch_shapes=[*scratch_shapes, pltpu.SemaphoreType.DMA((n_remote,)), pltpu.SemaphoreType.DMA((n_remote,))],
        compiler_params=_params(*["arbitrary"] * len(grid), side_effects=True),
    )(*args, *c_args)
    return list(res[:n_out]), list(res[n_out:])


def _pick(n, pref, mult=LANE):
    if n <= pref:
        return n
    best = None
    for t in range(mult, pref + 1, mult):
        if n % t == 0:
            best = t
    assert best is not None, (n, pref, mult)
    return best


def _sigmoid(x):
    return 1.0 / (1.0 + jnp.exp(-x))


def _mm(name, a, b, *, dims, grid, a_spec, b_spec, o_spec, out_shape, out_dtype, acc_shape, exact=False, comm=None):
    nk = grid[2]

    def body(a_ref, b_ref, o_ref, *scratch):
        if exact:
            part = lax.dot_general(a_ref[...], b_ref[...], dims, preferred_element_type=F32,
                                   precision=lax.Precision.HIGHEST)
        else:
            part = lax.dot_general(a_ref[...].astype(BF16), b_ref[...].astype(BF16), dims,
                                   preferred_element_type=F32)
        if nk == 1:
            o_ref[...] = part.astype(o_ref.dtype)
        else:
            acc = scratch[0]
            kk = pl.program_id(2)

            @pl.when(kk == 0)
            def _():
                acc[...] = part

            @pl.when(kk > 0)
            def _():
                acc[...] += part

            @pl.when(kk == nk - 1)
            def _():
                o_ref[...] = acc[...].astype(o_ref.dtype)

    outs, couts = _call(body, [a, b], name=name, grid=grid, in_specs=[a_spec, b_spec], out_specs=[o_spec],
                        out_shape=[jax.ShapeDtypeStruct(out_shape, out_dtype)],
                        scratch_shapes=[] if nk == 1 else [pltpu.VMEM(acc_shape, F32)],
                        sem=("parallel", "parallel", "arbitrary"), comm=comm)
    return outs[0] if comm is None else (outs[0], couts)


MM_VMEM_BUDGET = 36 << 20


def _fit(M, N, cost, m_mult=SUB):
    best = None
    for tm in sorted({_pick(M, p, m_mult) for p in (2048, 1024, 512, 256, 128)}):
        for tn in sorted({_pick(N, p) for p in (1408, 1024, 512, 256, 128)}):
            if best is None or (cost(tm, tn) <= MM_VMEM_BUDGET and tm * tn > best[0] * best[1]):
                best = (tm, tn)
    return best


def _sz(t):
    return jnp.dtype(t).itemsize


def mm_nn_pieces(name, a, w, p0, n_p, out_dtype, halves=1, comm=None):
    M, K = a.shape
    Nq = w.shape[2]
    tm, tn = _fit(M, Nq, lambda m, n: 2 * (m * K * _sz(a.dtype) + K * n * _sz(w.dtype) + m * n * _sz(out_dtype)))
    tpp = Nq // tn
    pph = n_p // halves
    if halves == 1:
        o_spec = pl.BlockSpec((tm, tn), lambda i, j, k: (i, j))
        oshape = (M, n_p * Nq)
    else:
        o_spec = pl.BlockSpec((None, tm, tn), lambda i, j, k: ((j // tpp) // pph, i, ((j // tpp) % pph) * tpp + j % tpp))
        oshape = (halves, M, pph * Nq)
    return _mm(name, a, w, dims=NN, grid=(M // tm, n_p * tpp, 1),
               a_spec=pl.BlockSpec((tm, K), lambda i, j, k: (i, 0)),
               b_spec=pl.BlockSpec((None, K, tn), lambda i, j, k: (p0 + j // tpp, 0, j % tpp)),
               o_spec=o_spec, out_shape=oshape, out_dtype=out_dtype, acc_shape=(tm, tn), comm=comm)


def mm_nn(name, a, w, out_dtype, exact=False, comm=None):
    M, K = a.shape
    N = w.shape[1]
    tm, tn = _fit(M, N, lambda m, n: 2 * (m * K * _sz(a.dtype) + K * n * _sz(w.dtype) + m * n * _sz(out_dtype)))
    return _mm(name, a, w, dims=NN, grid=(M // tm, N // tn, 1),
               a_spec=pl.BlockSpec((tm, K), lambda i, j, k: (i, 0)),
               b_spec=pl.BlockSpec((K, tn), lambda i, j, k: (0, j)),
               o_spec=pl.BlockSpec((tm, tn), lambda i, j, k: (i, j)),
               out_shape=(M, N), out_dtype=out_dtype, acc_shape=(tm, tn), exact=exact, comm=comm)


def mm_nt(name, dy, w, out_dtype, exact=False, comm=None):
    M, N = dy.shape
    K = w.shape[0]
    tm, tn = _fit(M, K, lambda m, n: 2 * (m * N * _sz(dy.dtype) + n * N * _sz(w.dtype) + m * n * _sz(out_dtype)))
    return _mm(name, dy, w, dims=NT, grid=(M // tm, K // tn, 1),
               a_spec=pl.BlockSpec((tm, N), lambda i, j, k: (i, 0)),
               b_spec=pl.BlockSpec((tn, N), lambda i, j, k: (j, 0)),
               o_spec=pl.BlockSpec((tm, tn), lambda i, j, k: (i, j)),
               out_shape=(M, K), out_dtype=out_dtype, acc_shape=(tm, tn), exact=exact, comm=comm)


def mm_nt_pieces(name, dy, w, out_dtype, halves=1, comm=None):
    P, K, Nq = w.shape
    M = dy.shape[-2]
    tm, tn = _fit(M, K, lambda m, n: 2 * (m * Nq * _sz(dy.dtype) + n * Nq * _sz(w.dtype) + m * n * _sz(out_dtype)) + 4 * m * n)
    pph = P // halves
    if halves == 1:
        a_spec = pl.BlockSpec((tm, Nq), lambda i, j, k: (i, k))
    else:
        a_spec = pl.BlockSpec((None, tm, Nq), lambda i, j, k: (k // pph, i, k % pph))
    return _mm(name, dy, w, dims=NT, grid=(M // tm, K // tn, P),
               a_spec=a_spec,
               b_spec=pl.BlockSpec((None, tn, Nq), lambda i, j, k: (k, j, 0)),
               o_spec=pl.BlockSpec((tm, tn), lambda i, j, k: (i, j)),
               out_shape=(M, K), out_dtype=out_dtype, acc_shape=(tm, tn), comm=comm)


def mm_nt_list(name, dys, w, out_dtype):
    P, K, Nq = w.shape
    M = dys[0].shape[0]
    assert len(dys) == P
    tm, tn = _fit(M, K, lambda m, n: 2 * (sum(m * Nq * _sz(d.dtype) for d in dys) + n * Nq * _sz(w.dtype)
                                          + m * n * _sz(out_dtype)) + 4 * m * n)

    def body(*refs):
        d_refs, w_ref, o_ref, acc = refs[:P], refs[P], refs[P + 1], refs[P + 2]
        kk = pl.program_id(2)
        for q in range(P):
            @pl.when(kk == q)
            def _(q=q):
                part = lax.dot_general(d_refs[q][...].astype(BF16), w_ref[...], NT, preferred_element_type=F32)
                acc[...] = part if q == 0 else acc[...] + part

        @pl.when(kk == P - 1)
        def _():
            o_ref[...] = acc[...].astype(o_ref.dtype)

    return pl.pallas_call(
        body, name=name, grid=(M // tm, K // tn, P),
        in_specs=[*[pl.BlockSpec((tm, Nq), lambda i, j, k: (i, 0)) for _ in range(P)],
                  pl.BlockSpec((None, tn, Nq), lambda i, j, k: (k, j, 0))],
        out_specs=pl.BlockSpec((tm, tn), lambda i, j, k: (i, j)),
        out_shape=jax.ShapeDtypeStruct((M, K), out_dtype),
        scratch_shapes=[pltpu.VMEM((tm, tn), F32)],
        compiler_params=_params("parallel", "arbitrary", "arbitrary"),
    )(*dys, w)


def mm_tn(name, a, dy, out_dtype, comm=None):
    M, K = a.shape
    N = dy.shape[1]
    tm, tn = _fit(K, N, lambda m, n: 2 * (M * m * _sz(a.dtype) + M * n * _sz(dy.dtype) + m * n * _sz(out_dtype)), LANE)
    return _mm(name, a, dy, dims=TN, grid=(K // tm, N // tn, 1),
               a_spec=pl.BlockSpec((M, tm), lambda i, j, k: (0, i)),
               b_spec=pl.BlockSpec((M, tn), lambda i, j, k: (0, j)),
               o_spec=pl.BlockSpec((tm, tn), lambda i, j, k: (i, j)),
               out_shape=(K, N), out_dtype=out_dtype, acc_shape=(tm, tn), comm=comm)


def mm_tn_pieces(name, a, dy, n_p, out_dtype, halves=1, comm=None):
    M, K = a.shape
    Nq = (dy.shape[-1] * halves) // n_p
    tm, tn = _fit(K, Nq, lambda m, n: 2 * (M * m * _sz(a.dtype) + M * n * _sz(dy.dtype) + m * n * _sz(out_dtype)), LANE)
    tpp = Nq // tn
    pph = n_p // halves
    if halves == 1:
        b_spec = pl.BlockSpec((M, tn), lambda i, j, k: (0, j))
    else:
        b_spec = pl.BlockSpec((None, M, tn), lambda i, j, k: ((j // tpp) // pph, 0, ((j // tpp) % pph) * tpp + j % tpp))
    return _mm(name, a, dy, dims=TN, grid=(K // tm, n_p * tpp, 1),
               a_spec=pl.BlockSpec((M, tm), lambda i, j, k: (0, i)),
               b_spec=b_spec,
               o_spec=pl.BlockSpec((None, tm, tn), lambda i, j, k: (j // tpp, i, j % tpp)),
               out_shape=(n_p, K, Nq), out_dtype=out_dtype, acc_shape=(tm, tn), comm=comm)


def _row_call(name, body, ins, in_kinds, outs, rows, tr, scratch=()):
    def spec(kind, shape):
        if isinstance(kind, pl.BlockSpec):
            return kind
        if kind == "row":
            return pl.BlockSpec((tr,) + tuple(shape[1:]), lambda i: (i,) + (0,) * (len(shape) - 1))
        return pl.BlockSpec(tuple(shape), lambda i: (0,) * len(shape))

    return pl.pallas_call(
        body, name=name, grid=(rows // tr,),
        in_specs=[spec(k, a.shape) for k, a in zip(in_kinds, ins)],
        out_specs=[spec(k, s) for k, s, _ in outs],
        out_shape=[jax.ShapeDtypeStruct(s, d) for _, s, d in outs],
        scratch_shapes=list(scratch),
        compiler_params=_params("arbitrary"),
    )(*ins)


def _acc(ref, val):
    @pl.when(pl.program_id(0) == 0)
    def _():
        ref[...] = val

    @pl.when(pl.program_id(0) > 0)
    def _():
        ref[...] += val


def norm_mod_fwd(name, x, w, b, tr=256):
    rows, d = x.shape
    tr = _pick(rows, tr, SUB)

    def body(x_ref, w_ref, b_ref, h_ref):
        xv = x_ref[...]
        r = lax.rsqrt(jnp.mean(xv * xv, axis=-1, keepdims=True) + EPS)
        h_ref[...] = (xv * r * w_ref[...] + b_ref[...]).astype(BF16)

    return _row_call(name, body, [x, w.reshape(1, d), b.reshape(1, d)], ["row", "vec", "vec"],
                     [("row", (rows, d), BF16)], rows, tr)[0]


def norm_mod_bwd(name, x, dh, w, dx_in, tr=256):
    rows, d = x.shape
    tr = _pick(rows, tr, SUB)

    def body(x_ref, dh_ref, w_ref, dxi_ref, dx_ref, cs1_ref, cs2_ref):
        xv = x_ref[...]
        r = lax.rsqrt(jnp.mean(xv * xv, axis=-1, keepdims=True) + EPS)
        xn = xv * r
        dhv = dh_ref[...].astype(F32)
        dxn = dhv * w_ref[...]
        dx_ref[...] = dxi_ref[...] + r * (dxn - xn * jnp.mean(dxn * xn, axis=-1, keepdims=True))
        _acc(cs1_ref, jnp.sum(dhv, axis=0, keepdims=True))
        _acc(cs2_ref, jnp.sum(dhv * xn, axis=0, keepdims=True))

    return _row_call(name, body, [x, dh, w.reshape(1, d), dx_in], ["row", "row", "vec", "row"],
                     [("row", (rows, d), F32), ("acc", (1, d), F32), ("acc", (1, d), F32)], rows, tr)


def gate_res_fwd(name, x, y, gate, tr=256):
    rows, d = x.shape
    tr = _pick(rows, tr, SUB)

    def body(x_ref, y_ref, g_ref, o_ref):
        o_ref[...] = x_ref[...] + g_ref[...] * y_ref[...].astype(F32)

    return _row_call(name, body, [x, y, gate.reshape(1, d)], ["row", "row", "vec"],
                     [("row", (rows, d), F32)], rows, tr)[0]


def gate_res_bwd(name, dx, y, gate, tr=256):
    rows, d = dx.shape
    tr = _pick(rows, tr, SUB)

    def body(dx_ref, y_ref, g_ref, dy_ref, dg_ref):
        dxv = dx_ref[...]
        dy_ref[...] = (g_ref[...] * dxv).astype(BF16)
        _acc(dg_ref, jnp.sum(dxv * y_ref[...].astype(F32), axis=0, keepdims=True))

    return _row_call(name, body, [dx, y, gate.reshape(1, d)], ["row", "row", "vec"],
                     [("row", (rows, d), BF16), ("acc", (1, d), F32)], rows, tr)


def loss_head(name, x, g, target, tr=256):
    rows, d = x.shape
    tr = _pick(rows, tr, SUB)

    def body(x_ref, g_ref, t_ref, dx_ref, dg_ref, loss_ref):
        xv = x_ref[...]
        r = lax.rsqrt(jnp.mean(xv * xv, axis=-1, keepdims=True) + EPS)
        xn = xv * r
        err = xn * g_ref[...] - t_ref[...]
        dy = err * (1.0 / d)
        dxn = dy * g_ref[...]
        dx_ref[...] = r * (dxn - xn * jnp.mean(dxn * xn, axis=-1, keepdims=True))
        _acc(dg_ref, jnp.sum(dy * xn, axis=0, keepdims=True))
        part = 0.5 * jnp.sum(jnp.sum(err * err, axis=-1, keepdims=True) * (1.0 / d), axis=0, keepdims=True)
        _acc(loss_ref, jnp.broadcast_to(part, (1, LANE)))

    return _row_call(name, body, [x, g.reshape(1, d), target], ["row", "vec", "row"],
                     [("row", (rows, d), F32), ("acc", (1, d), F32), ("acc", (1, LANE), F32)], rows, tr)


def fma3(name, a, b, c, dvec, out_dtype, tr=256):
    rows, d = a.shape
    tr = _pick(rows, tr, SUB)

    def body(a_ref, b_ref, c_ref, d_ref, o_ref):
        o_ref[...] = (d_ref[...] * a_ref[...] + b_ref[...] + c_ref[...]).astype(o_ref.dtype)

    return _row_call(name, body, [a, b, c, dvec.reshape(1, d)], ["row", "row", "row", "vec"],
                     [("row", (rows, d), out_dtype)], rows, tr)[0]


def sum_lead(name, a, out_dtype, tr=512):
    n, rows, cols = a.shape
    tr = _pick(rows, tr, 16)

    def body(a_ref, o_ref):
        acc = a_ref[0].astype(F32)
        for s in range(1, n):
            acc = acc + a_ref[s].astype(F32)
        o_ref[...] = acc.astype(o_ref.dtype)

    return pl.pallas_call(
        body, name=name, grid=(rows // tr,),
        in_specs=[pl.BlockSpec((n, tr, cols), lambda i: (0, i, 0))],
        out_specs=pl.BlockSpec((tr, cols), lambda i: (i, 0)),
        out_shape=jax.ShapeDtypeStruct((rows, cols), out_dtype),
        compiler_params=_params("parallel"),
    )(a)


def adamw(name, w, g, m, v, comm=None):
    shape = w.shape
    cols = shape[-1]
    w2, g2, m2, v2 = (t.reshape(-1, cols) for t in (w, g, m, v))
    rows = w2.shape[0]
    tr, tc = _pick(rows, 256, SUB), _pick(cols, 1536)
    c1 = 1.0 - ADAM_B1 ** ADAM_STEP
    c2 = 1.0 - ADAM_B2 ** ADAM_STEP

    def body(w_ref, g_ref, m_ref, v_ref, d_ref, mo_ref, vo_ref):
        gv = g_ref[...]
        mn = ADAM_B1 * m_ref[...] + (1.0 - ADAM_B1) * gv
        vn = ADAM_B2 * v_ref[...] + (1.0 - ADAM_B2) * (gv * gv)
        mo_ref[...] = mn
        vo_ref[...] = vn
        d_ref[...] = -ADAM_LR * ((mn / c1) / (jnp.sqrt(vn / c2) + ADAM_EPS) + ADAM_WD * w_ref[...])

    blk = pl.BlockSpec((tr, tc), lambda i, j: (i, j))
    outs, couts = _call(body, [w2, g2, m2, v2], name=name, grid=(rows // tr, cols // tc), in_specs=[blk] * 4,
                        out_specs=[blk] * 3, out_shape=[jax.ShapeDtypeStruct(w2.shape, F32)] * 3,
                        sem=("parallel", "parallel"), comm=comm)
    outs = tuple(o.reshape(shape) for o in outs)
    return outs if comm is None else (outs, couts)


def add_half(name, grad, got, c_idx, tr=256):
    Pn, R, C = grad.shape
    hr = R // 2
    tr = _pick(hr, tr, HALO)
    nb = hr // tr

    def body(c_ref, a_ref, b_ref, o_ref):
        o_ref[...] = (a_ref[...].astype(F32) + b_ref[...].astype(F32)).astype(o_ref.dtype)

    return pl.pallas_call(
        body, name=name,
        grid_spec=pltpu.PrefetchScalarGridSpec(
            num_scalar_prefetch=1, grid=(Pn, nb),
            in_specs=[pl.BlockSpec((None, tr, C), lambda q, i, c: (q, c[0] * nb + i, 0)),
                      pl.BlockSpec((None, tr, C), lambda q, i, c: (q, i, 0))],
            out_specs=pl.BlockSpec((None, tr, C), lambda q, i, c: (q, i, 0))),
        out_shape=jax.ShapeDtypeStruct((Pn, hr, C), BF16),
        compiler_params=_params("parallel", "parallel"),
    )(c_idx, grad, got)


def pair_sum_to_slot(name, buf, got, ids, tr=256):
    R, C = buf.shape
    hr = R // 2
    tr = _pick(hr, tr, SUB)
    nb = hr // tr

    def body(ids_ref, a_ref, b_ref, o_ref):
        o_ref[...] = a_ref[...] + b_ref[...]

    return pl.pallas_call(
        body, name=name,
        grid_spec=pltpu.PrefetchScalarGridSpec(
            num_scalar_prefetch=1, grid=(nb,),
            in_specs=[pl.BlockSpec((tr, C), lambda i, ids: (ids[1] * nb + i, 0)),
                      pl.BlockSpec((tr, C), lambda i, ids: (i, 0))],
            out_specs=pl.BlockSpec((None, tr, C), lambda i, ids: (ids[0], i, 0))),
        out_shape=jax.ShapeDtypeStruct((N_CHIP, hr, C), F32),
        compiler_params=_params("parallel"),
    )(ids, buf, got)


def sum_chips_to_half(name, slots, ids, tr=256):
    n, hr, C = slots.shape
    tr = _pick(hr, tr, SUB)

    def body(ids_ref, s_ref, o_ref):
        acc = s_ref[0]
        for q in range(1, n):
            acc = acc + s_ref[q]
        o_ref[...] = acc

    return pl.pallas_call(
        body, name=name,
        grid_spec=pltpu.PrefetchScalarGridSpec(
            num_scalar_prefetch=1, grid=(hr // tr,),
            in_specs=[pl.BlockSpec((n, tr, C), lambda i, ids: (0, i, 0))],
            out_specs=pl.BlockSpec((None, tr, C), lambda i, ids: (ids[1], i, 0))),
        out_shape=jax.ShapeDtypeStruct((2, hr, C), F32),
        compiler_params=_params("parallel"),
    )(ids, slots)


def sum_slots(name, slots, mine, ids, tr=256):
    Pn, hr, C = slots.shape
    tr = _pick(hr, tr, HALO)

    def body(ids_ref, m_ref, s1_ref, s2_ref, s3_ref, o_ref):
        o_ref[...] = (m_ref[...].astype(F32) + s1_ref[...].astype(F32)) + (s2_ref[...].astype(F32) + s3_ref[...].astype(F32))

    def other(k):
        return pl.BlockSpec((None, tr, C), lambda i, ids: ((ids[0] + k) % Pn, i, 0))

    return pl.pallas_call(
        body, name=name,
        grid_spec=pltpu.PrefetchScalarGridSpec(
            num_scalar_prefetch=1, grid=(hr // tr,),
            in_specs=[pl.BlockSpec((None, tr, C), lambda i, ids: (ids[0], i, 0)), other(1), other(2), other(3)],
            out_specs=pl.BlockSpec((None, tr, C), lambda i, ids: (ids[1], i, 0))),
        out_shape=jax.ShapeDtypeStruct((2, hr, C), F32),
        compiler_params=_params("parallel"),
    )(ids, mine, slots, slots, slots)


HALO = 16


def _halo_specs(lead, R, tn, n_rows, col_of):
    nb, nblk = R // HALO, n_rows // HALO

    def mk(rows, row_of):
        return pl.BlockSpec((lead, rows, tn), lambda *g: (0, row_of(g[-1]), col_of(g)))

    return (mk(HALO, lambda i: jnp.maximum(i * nb - 1, 0)), mk(R, lambda i: i),
            mk(HALO, lambda i: jnp.minimum((i + 1) * nb, nblk - 1)))


def _fill_halo(dst, i, last, R, prev, cur, nxt):
    nd = len(dst.shape)
    lead = (slice(None),) * (nd - 2)
    dst[lead + (slice(0, HALO), slice(None))] = jnp.where(i == 0, 0.0, prev)
    dst[lead + (slice(HALO, HALO + R), slice(None))] = cur
    dst[lead + (slice(HALO + R, HALO + R + HALO), slice(None))] = jnp.where(i == last, 0.0, nxt)


def _shift_mats(n):
    i = np.arange(n)
    return jnp.asarray(np.stack([i[:, None] - 1 == i[None, :], i[:, None] + 1 == i[None, :]]), BF16)


def _shifted(s_ref, xb):
    return (jnp.dot(s_ref[0], xb, preferred_element_type=F32), jnp.dot(s_ref[1], xb, preferred_element_type=F32))


def ffn_mid_fwd(name, up3, cw, cb, R=256, tn=512, comm=None):
    _, L, Fd = up3.shape
    R, tn = _pick(L, R, HALO), _pick(Fd, tn)
    nrow = L // R

    def body(p_ref, c_ref, n_ref, w_ref, b_ref, s_ref, act_ref):
        i = pl.program_id(1)
        row = lax.broadcasted_iota(jnp.int32, (R, tn), 0)
        cv = []
        for z in range(2):
            xb = c_ref[z]
            before = jnp.where(i == 0, 0.0, p_ref[z].astype(F32)[HALO - 1:HALO])
            after = jnp.where(i == nrow - 1, 0.0, n_ref[z].astype(F32)[0:1])
            dn, up = _shifted(s_ref, xb)
            dn = jnp.where(row == 0, before, dn)
            up = jnp.where(row == R - 1, after, up)
            cv.append(b_ref[z] + w_ref[z, 0:1, :] * dn + w_ref[z, 1:2, :] * xb.astype(F32) + w_ref[z, 2:3, :] * up)
        u, g = cv
        act_ref[...] = (u * g * _sigmoid(g)).astype(BF16)

    hs = _halo_specs(2, R, tn, L, lambda g: g[0])
    outs, couts = _call(
        body, [up3, up3, up3, cw, cb, _shift_mats(R)], name=name, grid=(Fd // tn, nrow),
        in_specs=[*hs, pl.BlockSpec((2, 3, tn), lambda j, i: (0, 0, j)), pl.BlockSpec((2, 1, tn), lambda j, i: (0, 0, j)),
                  pl.BlockSpec((2, R, R), lambda j, i: (0, 0, 0))],
        out_specs=[pl.BlockSpec((R, tn), lambda j, i: (i, j))],
        out_shape=[jax.ShapeDtypeStruct((L, Fd), BF16)], sem=("parallel", "arbitrary"), comm=comm)
    return outs[0] if comm is None else (outs[0], couts)


def ffn_mid_bwd(name, up3, dact, cw, cb, R=256, tn=512, comm=None):
    _, L, Fd = up3.shape
    R, tn = _pick(L, R, HALO), _pick(Fd, tn)
    nrow = L // R

    def gate_grads(u, g, d):
        sg = _sigmoid(g)
        return d * g * sg, d * u * sg * (1.0 + g * (1.0 - sg))

    def body(pu, cu, nu, pd, cd, nd, w_ref, b_ref, s_ref, dup_ref, dcw_ref, dcb_ref):
        i = pl.program_id(1)
        first, last = i == 0, i == nrow - 1
        row = lax.broadcasted_iota(jnp.int32, (R, tn), 0)
        cv, cv_b, cv_a, taps = [], [], [], []
        for z in range(2):
            xb = cu[z]
            xf = xb.astype(F32)
            pf = jnp.where(first, 0.0, pu[z].astype(F32))
            nf = jnp.where(last, 0.0, nu[z].astype(F32))
            xm2, xm1, xp0, xp1 = pf[HALO - 2:HALO - 1], pf[HALO - 1:HALO], nf[0:1], nf[1:2]
            dn, up = _shifted(s_ref, xb)
            dn = jnp.where(row == 0, xm1, dn)
            up = jnp.where(row == R - 1, xp0, up)
            w0, w1, w2, b = w_ref[z, 0:1, :], w_ref[z, 1:2, :], w_ref[z, 2:3, :], b_ref[z]
            cv.append(b + w0 * dn + w1 * xf + w2 * up)
            cv_b.append(b + w0 * xm2 + w1 * xm1 + w2 * xf[0:1])
            cv_a.append(b + w0 * xf[R - 1:R] + w1 * xp0 + w2 * xp1)
            taps.append((dn, xf, up))
        dcs = gate_grads(cv[0], cv[1], cd[0].astype(F32))
        dcs_b = gate_grads(cv_b[0], cv_b[1], jnp.where(first, 0.0, pd[0].astype(F32)[HALO - 1:HALO]))
        dcs_a = gate_grads(cv_a[0], cv_a[1], jnp.where(last, 0.0, nd[0].astype(F32)[0:1]))

        @pl.when(first)
        def _():
            dcw_ref[...] = jnp.zeros_like(dcw_ref)
            dcb_ref[...] = jnp.zeros_like(dcb_ref)

        for z in range(2):
            dc = dcs[z]
            dc_dn, dc_up = _shifted(s_ref, dc.astype(BF16))
            dc_dn = jnp.where(row == 0, dcs_b[z], dc_dn)
            dc_up = jnp.where(row == R - 1, dcs_a[z], dc_up)
            dup_ref[z] = (w_ref[z, 0:1, :] * dc_up + w_ref[z, 1:2, :] * dc + w_ref[z, 2:3, :] * dc_dn).astype(BF16)
            dcb_ref[z] += jnp.sum(dc, axis=0, keepdims=True)
            for k in range(3):
                dcw_ref[z, k:k + 1, :] += jnp.sum(dc * taps[z][k], axis=0, keepdims=True)

    hu = _halo_specs(2, R, tn, L, lambda g: g[0])
    hd = _halo_specs(1, R, tn, L, lambda g: g[0])
    outs, couts = _call(
        body, [up3, up3, up3, dact[None], dact[None], dact[None], cw, cb, _shift_mats(R)], name=name, grid=(Fd // tn, nrow),
        in_specs=[*hu, *hd, pl.BlockSpec((2, 3, tn), lambda j, i: (0, 0, j)), pl.BlockSpec((2, 1, tn), lambda j, i: (0, 0, j)),
                  pl.BlockSpec((2, R, R), lambda j, i: (0, 0, 0))],
        out_specs=[pl.BlockSpec((2, R, tn), lambda j, i: (0, i, j)), pl.BlockSpec((2, 3, tn), lambda j, i: (0, 0, j)),
                   pl.BlockSpec((2, 1, tn), lambda j, i: (0, 0, j))],
        out_shape=[jax.ShapeDtypeStruct((2, L, Fd), BF16), jax.ShapeDtypeStruct((2, 3, Fd), F32),
                   jax.ShapeDtypeStruct((2, 1, Fd), F32)],
        sem=("parallel", "arbitrary"), comm=comm)
    return outs if comm is None else (outs, couts)


def _glu_z0(blk):
    return blk[0].astype(F32) * _sigmoid(blk[1].astype(F32))


def _sublane_copies(ref, cs):
    n = ref.shape[1]
    blk = ref[0, :, cs]
    for b in range(1, SUB):
        ref[b, :, cs] = pltpu.roll(blk, n - b, 0)


def _tap(ref, offset, rows, cs):
    return ref[offset % SUB, pl.ds(offset - offset % SUB, rows), cs]


def conf_mid_fwd(name, ag3, dw_w, dw_b, ln_g, ln_b, R=128, cb=256):
    _, L, C = ag3.shape
    K = dw_w.shape[0]
    pad = (K - 1) // 2
    assert pad <= HALO
    R, cb = _pick(L, R, HALO), _pick(C, cb)
    nrow = L // R

    def body(p_ref, c_ref, n_ref, w_ref, b_ref, g_ref, bb_ref, z1_ref, z3_ref, s_ref):
        i = pl.program_id(0)
        _fill_halo(s_ref.at[0], i, nrow - 1, R, _glu_z0(p_ref), _glu_z0(c_ref), _glu_z0(n_ref))
        for c0 in range(0, C, cb):
            cs = slice(c0, c0 + cb)
            _sublane_copies(s_ref, cs)
            acc = jnp.broadcast_to(b_ref[:, cs], (R, cb))
            for k in range(K):
                acc = acc + w_ref[k:k + 1, cs] * _tap(s_ref, HALO - pad + k, R, cs)
            z1_ref[:, cs] = acc
        z1 = z1_ref[...]
        zc = z1 - jnp.mean(z1, axis=-1, keepdims=True)
        zn = zc * lax.rsqrt(jnp.mean(zc * zc, axis=-1, keepdims=True) + EPS)
        z2 = zn * g_ref[...] + bb_ref[...]
        z3_ref[...] = (z2 * _sigmoid(z2)).astype(BF16)

    hs = _halo_specs(2, R, C, L, lambda g: 0)
    vec = pl.BlockSpec((1, C), lambda i: (0, 0))
    return pl.pallas_call(
        body, name=name, grid=(nrow,),
        in_specs=[*hs, pl.BlockSpec((K, C), lambda i: (0, 0)), vec, vec, vec],
        out_specs=[pl.BlockSpec((R, C), lambda i: (i, 0)), pl.BlockSpec((R, C), lambda i: (i, 0))],
        out_shape=[jax.ShapeDtypeStruct((L, C), F32), jax.ShapeDtypeStruct((L, C), BF16)],
        scratch_shapes=[pltpu.VMEM((SUB, R + 2 * HALO, C), F32)],
        compiler_params=_params("parallel"),
    )(ag3, ag3, ag3, dw_w, dw_b.reshape(1, C), ln_g.reshape(1, C), ln_b.reshape(1, C))


def conf_ln_bwd(name, z1, dz3, ln_g, ln_b, tr=256):
    rows, C = z1.shape
    tr = _pick(rows, tr, HALO)

    def body(z_ref, d_ref, g_ref, b_ref, dz_ref, dg_ref, db_ref):
        z1v = z_ref[...]
        zc = z1v - jnp.mean(z1v, axis=-1, keepdims=True)
        rs = lax.rsqrt(jnp.mean(zc * zc, axis=-1, keepdims=True) + EPS)
        zn = zc * rs
        z2 = zn * g_ref[...] + b_ref[...]
        sg = _sigmoid(z2)
        dz2 = d_ref[...].astype(F32) * sg * (1.0 + z2 * (1.0 - sg))
        _acc(dg_ref, jnp.sum(dz2 * zn, axis=0, keepdims=True))
        _acc(db_ref, jnp.sum(dz2, axis=0, keepdims=True))
        dzn = dz2 * g_ref[...]
        dz1 = rs * (dzn - jnp.mean(dzn, axis=-1, keepdims=True) - zn * jnp.mean(dzn * zn, axis=-1, keepdims=True))
        dz_ref[...] = dz1.astype(BF16)

    return _row_call(name, body, [z1, dz3, ln_g.reshape(1, C), ln_b.reshape(1, C)], ["row", "row", "vec", "vec"],
                     [("row", (rows, C), BF16), ("acc", (1, C), F32), ("acc", (1, C), F32)], rows, tr)


def conf_conv_bwd(name, ag3, dz1, dw_w, R=128, cb=256, comm=None):
    _, L, C = ag3.shape
    K = dw_w.shape[0]
    pad = (K - 1) // 2
    R, cb = _pick(L, R, HALO), _pick(C, cb)
    nrow = L // R

    def body(pa, ca, na, pd, cd, nd, w_ref, dag_ref, dw_ref, db_ref, s_ref, d_ref, z_ref):
        i = pl.program_id(0)
        _fill_halo(s_ref.at[0], i, nrow - 1, R, _glu_z0(pa), _glu_z0(ca), _glu_z0(na))
        _fill_halo(d_ref.at[0], i, nrow - 1, R, pd[0].astype(F32), cd[0].astype(F32), nd[0].astype(F32))

        @pl.when(i == 0)
        def _():
            dw_ref[...] = jnp.zeros_like(dw_ref)
            db_ref[...] = jnp.zeros_like(db_ref)

        for c0 in range(0, C, cb):
            cs = slice(c0, c0 + cb)
            _sublane_copies(s_ref, cs)
            _sublane_copies(d_ref, cs)
            dcur = d_ref[0, pl.ds(HALO, R), cs]
            acc = jnp.zeros((R, cb), F32)
            for k in range(K):
                acc = acc + w_ref[k:k + 1, cs] * _tap(d_ref, HALO + pad - k, R, cs)
                dw_ref[k:k + 1, cs] += jnp.sum(dcur * _tap(s_ref, HALO - pad + k, R, cs), axis=0, keepdims=True)
            z_ref[:, cs] = acc
            db_ref[:, cs] += jnp.sum(dcur, axis=0, keepdims=True)
        dz0 = z_ref[...]
        a = ca[0].astype(F32)
        sg = _sigmoid(ca[1].astype(F32))
        dag_ref[0] = (dz0 * sg).astype(BF16)
        dag_ref[1] = (dz0 * a * sg * (1.0 - sg)).astype(BF16)

    ha = _halo_specs(2, R, C, L, lambda g: 0)
    hd = _halo_specs(1, R, C, L, lambda g: 0)
    outs, couts = _call(
        body, [ag3, ag3, ag3, dz1[None], dz1[None], dz1[None], dw_w], name=name, grid=(nrow,),
        in_specs=[*ha, *hd, pl.BlockSpec((K, C), lambda i: (0, 0))],
        out_specs=[pl.BlockSpec((2, R, C), lambda i: (0, i, 0)), pl.BlockSpec((K, C), lambda i: (0, 0)),
                   pl.BlockSpec((1, C), lambda i: (0, 0))],
        out_shape=[jax.ShapeDtypeStruct((2, L, C), BF16), jax.ShapeDtypeStruct((K, C), F32),
                   jax.ShapeDtypeStruct((1, C), F32)],
        scratch_shapes=[pltpu.VMEM((SUB, R + 2 * HALO, C), F32), pltpu.VMEM((SUB, R + 2 * HALO, C), F32),
                        pltpu.VMEM((R, C), F32)],
        sem=("arbitrary",), comm=comm)
    return outs if comm is None else (outs, couts)


_GELU_C = math.sqrt(2.0 / math.pi)


def _gelu(x):
    return 0.5 * x * (1.0 + jnp.tanh(_GELU_C * (x + 0.044715 * x * x * x)))


def _gelu_grad(x):
    t = jnp.tanh(_GELU_C * (x + 0.044715 * x * x * x))
    return 0.5 * (1.0 + t) + 0.5 * x * (1.0 - t * t) * _GELU_C * (1.0 + 3.0 * 0.044715 * x * x)


def glu_fwd(name, u, y0, y1, d, wg, tr=512):
    rows, W = u.shape
    tr = _pick(rows, tr, HALO)

    def body(u_ref, y0_ref, y1_ref, d_ref, w_ref, o_ref):
        z = _gelu(d_ref[...] * u_ref[...] + y0_ref[...] + y1_ref[...])
        zz = jnp.dot(z.astype(BF16), w_ref[...], preferred_element_type=F32)
        o_ref[...] = (z * _sigmoid(zz)).astype(BF16)

    return _row_call(name, body, [u, y0, y1, d.reshape(1, W), wg], ["row", "row", "row", "vec", "vec"],
                     [("row", (rows, W), BF16)], rows, tr)[0]


def glu_bwd(name, u, y0, y1, d, wg, dmix, tr=512):
    rows, W = u.shape
    tr = _pick(rows, tr, HALO)

    def body(u_ref, y0_ref, y1_ref, d_ref, w_ref, do_ref, dy_ref, z_ref, dzz_ref, dd_ref):
        uv = u_ref[...]
        y = d_ref[...] * uv + y0_ref[...] + y1_ref[...]
        z = _gelu(y)
        zz = jnp.dot(z.astype(BF16), w_ref[...], preferred_element_type=F32)
        sg = _sigmoid(zz)
        do = do_ref[...].astype(F32)
        dzz = (do * z * sg * (1.0 - sg)).astype(BF16)
        dz = do * sg + lax.dot_general(dzz, w_ref[...], NT, preferred_element_type=F32)
        dy = dz * _gelu_grad(y)
        dy_ref[...] = dy
        z_ref[...] = z.astype(BF16)
        dzz_ref[...] = dzz
        _acc(dd_ref, jnp.sum(dy * uv, axis=0, keepdims=True))

    do_spec = pl.BlockSpec((tr, W), lambda i: (i, 0))
    return _row_call(name, body, [u, y0, y1, d.reshape(1, W), wg, dmix], ["row", "row", "row", "vec", "vec", do_spec],
                     [("row", (rows, W), F32), ("row", (rows, W), BF16), ("row", (rows, W), BF16), ("acc", (1, W), F32)],
                     rows, tr)


NA_KEYS = NA_WIN_R * GRID_W


NA_PAIRS = NA_WIN_R // 2


def na_bias(rpb):
    H, nr, nc = rpb.shape
    e, ok = _na_col_select()
    rp = jnp.pad(rpb.reshape(H * nr, nc), ((0, (-H * nr) % SUB), (0, LANE - nc)))
    cols = mm_nn("na_bias_mm", rp, jnp.asarray(e, F32), F32, exact=True)[:H * nr]
    tiles = (cols + jnp.asarray(np.where(ok, 0.0, NEG), F32)).reshape(H, nr, GRID_W, GRID_W)
    return jnp.concatenate([tiles[:, :-1], tiles[:, 1:]], axis=-1)


def na_bias_grad(db2):
    H, n2 = db2.shape[:2]
    left, right = db2[..., :GRID_W], db2[..., GRID_W:]
    tiles = jnp.pad(left, ((0, 0), (0, 1), (0, 0), (0, 0))) + jnp.pad(right, ((0, 0), (1, 0), (0, 0), (0, 0)))
    flat = tiles.reshape(H * (n2 + 1), GRID_W * GRID_W)
    flat = jnp.pad(flat, ((0, (-flat.shape[0]) % SUB), (0, 0)))
    dcol = mm_nt("na_bias_fold", flat, na_bias_fold_matrix(), F32, exact=True)
    return dcol[:H * (n2 + 1), :2 * NA_WIN_C - 1].reshape(H, n2 + 1, 2 * NA_WIN_C - 1)


def _na_col_select():
    q = np.arange(GRID_W)
    cs = np.clip(q - NA_WIN_C // 2, 0, GRID_W - NA_WIN_C)
    ok = ((q[None, :] >= cs[:, None]) & (q[None, :] < cs[:, None] + NA_WIN_C)).reshape(-1)
    cidx = np.clip(q[None, :] - q[:, None] + (NA_WIN_C - 1), 0, 2 * NA_WIN_C - 2).reshape(-1)
    return (cidx[None, :] == np.arange(LANE)[:, None]) & ok[None, :], ok


def na_bias_fold_matrix():
    return jnp.asarray(_na_col_select()[0], F32)


def _na_window(r, rows):
    kr0 = jnp.clip(r - NA_WIN_R // 2, 0, rows - NA_WIN_R)
    return pl.multiple_of(kr0 * GRID_W, GRID_W), r - kr0


def _na_dims(qkv, kvc):
    L = qkv.shape[0]
    NA = qkv.shape[1] // 3
    H = NA // NA_HEAD_DIM
    hp = 2 if H % 2 == 0 else 1
    return L, NA, H, hp, H // hp, L // GRID_W, kvc.shape[0]


def _na_bias_tile(b_ref, hh, off):
    return jnp.concatenate([b_ref[hh, NA_WIN_R - 1 - off + 2 * j] for j in range(NA_PAIRS)], axis=-1)


def natten_fwd(name, qkv, kvc, bias, comm=None):
    L, NA, H, hp, G, rows, Lc = _na_dims(qkv, kvc)
    scale = NA_HEAD_DIM ** -0.5
    wd = hp * NA_HEAD_DIM

    def body(q_ref, k_ref, v_ref, kc_ref, vc_ref, b_ref, o_ref, lse_ref):
        st, off = _na_window(pl.program_id(1), rows)
        for hh in range(hp):
            sl = slice(hh * NA_HEAD_DIM, (hh + 1) * NA_HEAD_DIM)
            q = q_ref[:, sl]
            s_loc = (lax.dot_general(q, k_ref[pl.ds(st, NA_KEYS), sl], NT, preferred_element_type=F32) * scale
                     + _na_bias_tile(b_ref, hh, off))
            s_ctx = lax.dot_general(q, kc_ref[:, sl], NT, preferred_element_type=F32) * scale
            m = jnp.maximum(jnp.max(s_loc, axis=-1, keepdims=True), jnp.max(s_ctx, axis=-1, keepdims=True))
            p_loc, p_ctx = jnp.exp(s_loc - m), jnp.exp(s_ctx - m)
            l = jnp.sum(p_loc, axis=-1, keepdims=True) + jnp.sum(p_ctx, axis=-1, keepdims=True)
            o = (jnp.dot(p_loc.astype(BF16), v_ref[pl.ds(st, NA_KEYS), sl], preferred_element_type=F32)
                 + jnp.dot(p_ctx.astype(BF16), vc_ref[:, sl], preferred_element_type=F32))
            o_ref[:, sl] = (o / l).astype(BF16)
            lse_ref[hh] = m + jnp.log(l)

    outs, couts = _call(
        body, [qkv, qkv, qkv, kvc, kvc, bias], name=name, grid=(G, rows),
        in_specs=[pl.BlockSpec((GRID_W, wd), lambda h, r: (r, h)),
                  pl.BlockSpec((L, wd), lambda h, r: (0, G + h)),
                  pl.BlockSpec((L, wd), lambda h, r: (0, 2 * G + h)),
                  pl.BlockSpec((Lc, wd), lambda h, r: (0, h)),
                  pl.BlockSpec((Lc, wd), lambda h, r: (0, G + h)),
                  pl.BlockSpec((hp,) + bias.shape[1:], lambda h, r: (h, 0, 0, 0))],
        out_specs=[pl.BlockSpec((GRID_W, wd), lambda h, r: (r, h)),
                   pl.BlockSpec((hp, GRID_W, 1), lambda h, r: (h, r, 0))],
        out_shape=[jax.ShapeDtypeStruct((L, NA), BF16), jax.ShapeDtypeStruct((H, L, 1), F32)],
        sem=("parallel", "arbitrary"), comm=comm)
    return outs if comm is None else (outs, couts)


def natten_bwd(name, qkv, kvc, bias, o, lse, dmix, comm=None):
    L, NA, H, hp, G, rows, Lc = _na_dims(qkv, kvc)
    scale = NA_HEAD_DIM ** -0.5
    wd = hp * NA_HEAD_DIM

    def body(q_ref, k_ref, v_ref, kc_ref, vc_ref, b_ref, o_ref, lse_ref, do_ref,
             dq_ref, dk_ref, dv_ref, dkc_ref, dvc_ref, db_ref):
        r = pl.program_id(1)
        st, off = _na_window(r, rows)

        @pl.when(r == 0)
        def _():
            for ref in (dk_ref, dv_ref, dkc_ref, dvc_ref, db_ref):
                ref[...] = jnp.zeros_like(ref)

        for hh in range(hp):
            sl = slice(hh * NA_HEAD_DIM, (hh + 1) * NA_HEAD_DIM)
            q, kl, vl, kc, vc = q_ref[:, sl], k_ref[pl.ds(st, NA_KEYS), sl], v_ref[pl.ds(st, NA_KEYS), sl], kc_ref[:, sl], vc_ref[:, sl]
            do = do_ref[:, sl]
            lse_v = lse_ref[hh]
            p_loc = jnp.exp(lax.dot_general(q, kl, NT, preferred_element_type=F32) * scale + _na_bias_tile(b_ref, hh, off) - lse_v)
            p_ctx = jnp.exp(lax.dot_general(q, kc, NT, preferred_element_type=F32) * scale - lse_v)
            delta = jnp.sum(do.astype(F32) * o_ref[:, sl].astype(F32), axis=-1, keepdims=True)
            ds_loc = p_loc * (lax.dot_general(do, vl, NT, preferred_element_type=F32) - delta)
            ds_ctx = p_ctx * (lax.dot_general(do, vc, NT, preferred_element_type=F32) - delta)
            dsl, dsc = ds_loc.astype(BF16), ds_ctx.astype(BF16)
            dq = jnp.dot(dsl, kl, preferred_element_type=F32) + jnp.dot(dsc, kc, preferred_element_type=F32)
            dq_ref[:, sl] = (dq * scale).astype(BF16)
            dk_ref[pl.ds(st, NA_KEYS), sl] += lax.dot_general(dsl, q, TN, preferred_element_type=F32) * scale
            dv_ref[pl.ds(st, NA_KEYS), sl] += lax.dot_general(p_loc.astype(BF16), do, TN, preferred_element_type=F32)
            dkc_ref[:, sl] += lax.dot_general(dsc, q, TN, preferred_element_type=F32) * scale
            dvc_ref[:, sl] += lax.dot_general(p_ctx.astype(BF16), do, TN, preferred_element_type=F32)
            for j in range(NA_PAIRS):
                db_ref[hh, NA_WIN_R - 1 - off + 2 * j] += ds_loc[:, 2 * j * GRID_W:(2 * j + 2) * GRID_W]

    tok = pl.BlockSpec((GRID_W, wd), lambda h, r: (r, h))
    bia = pl.BlockSpec((hp,) + bias.shape[1:], lambda h, r: (h, 0, 0, 0))
    outs, couts = _call(
        body, [qkv, qkv, qkv, kvc, kvc, bias, o, lse, dmix], name=name, grid=(G, rows),
        in_specs=[tok,
                  pl.BlockSpec((L, wd), lambda h, r: (0, G + h)),
                  pl.BlockSpec((L, wd), lambda h, r: (0, 2 * G + h)),
                  pl.BlockSpec((Lc, wd), lambda h, r: (0, h)),
                  pl.BlockSpec((Lc, wd), lambda h, r: (0, G + h)),
                  bia,
                  tok,
                  pl.BlockSpec((hp, GRID_W, 1), lambda h, r: (h, r, 0)),
                  pl.BlockSpec((GRID_W, wd), lambda h, r: (r, G + h))],
        out_specs=[tok,
                   pl.BlockSpec((L, wd), lambda h, r: (0, h)),
                   pl.BlockSpec((L, wd), lambda h, r: (0, h)),
                   pl.BlockSpec((Lc, wd), lambda h, r: (0, h)),
                   pl.BlockSpec((Lc, wd), lambda h, r: (0, h)),
                   bia],
        out_shape=[jax.ShapeDtypeStruct((L, NA), BF16), jax.ShapeDtypeStruct((L, NA), F32), jax.ShapeDtypeStruct((L, NA), F32),
                   jax.ShapeDtypeStruct((Lc, NA), F32), jax.ShapeDtypeStruct((Lc, NA), F32),
                   jax.ShapeDtypeStruct(bias.shape, F32)],
        sem=("parallel", "arbitrary"), comm=comm)
    return outs if comm is None else (outs, couts)


def _s5_dims(T, N):
    TC = T // S5_SEG
    assert T % (S5_SEG * SUB * 2) == 0 and N % S5_STRIP == 0
    return TC, TC // SUB, S5_SEG, N // S5_STRIP


def _s5_backward(d, rev):
    return (d == 1) != rev


def s5_scan(name, xin, mats, a, rev, comm=None):
    _, T, W = xin.shape
    N = a.shape[-1]
    TC, NG, NCH, NS = _s5_dims(T, N)
    CW, SL = W // NS, S5_STRIP

    def ck(d, k):
        return jnp.where(_s5_backward(d, rev), NCH - 1 - k, k)

    def body(x_ref, m_ref, a_ref, h_ref, f_ref, carry, hs):
        @pl.when(pl.program_id(2) == 0)
        def _():
            carry[...] = jnp.zeros_like(carry)

        xb = x_ref[...].astype(BF16)
        hs[0] = jnp.dot(xb, m_ref[0], preferred_element_type=F32)
        hs[1] = jnp.dot(xb, m_ref[1], preferred_element_type=F32)
        ar, ai = jnp.broadcast_to(a_ref[0], (SUB, SL)), jnp.broadcast_to(a_ref[1], (SUB, SL))
        bw = _s5_backward(pl.program_id(0), rev)

        def step(t, c):
            hr, hi = c
            row = pl.multiple_of(jnp.where(bw, NG - 1 - t, t) * SUB, SUB)
            nr = ar * hr - ai * hi + hs[0, pl.ds(row, SUB), :]
            ni = ar * hi + ai * hr + hs[1, pl.ds(row, SUB), :]
            hs[0, pl.ds(row, SUB), :] = nr
            hs[1, pl.ds(row, SUB), :] = ni
            return nr, ni

        hr, hi = lax.fori_loop(0, NG, step, (carry[0], carry[1]))
        carry[0], carry[1] = hr, hi
        f_ref[0], f_ref[1] = hr, hi
        h_ref[...] = hs[...].astype(BF16)

    outs, couts = _call(
        body, [xin, mats, a], name=name, grid=(2, NS, NCH),
        in_specs=[pl.BlockSpec((None, TC, CW), lambda d, j, k: (d, ck(d, k), j)),
                  pl.BlockSpec((None, 2, None, CW, SL), lambda d, j, k: (d, 0, j, 0, 0)),
                  pl.BlockSpec((None, 2, 1, SL), lambda d, j, k: (d, 0, 0, j))],
        out_specs=[pl.BlockSpec((None, 2, TC, SL), lambda d, j, k: (d, 0, ck(d, k), j)),
                   pl.BlockSpec((None, 2, SUB, SL), lambda d, j, k: (d, 0, 0, j))],
        out_shape=[jax.ShapeDtypeStruct((2, 2, T, N), BF16), jax.ShapeDtypeStruct((2, 2, SUB, N), F32)],
        scratch_shapes=[pltpu.VMEM((2, SUB, SL), F32), pltpu.VMEM((2, TC, SL), F32)],
        sem=("parallel", "parallel", "arbitrary"), comm=comm)
    return outs if comm is None else (outs, couts)


def s5_fix(name, hloc, hin, a, mats, rev, comm=None):
    _, _, T, N = hloc.shape
    TC, NG, NCH, NS = _s5_dims(T, N)
    SL = S5_STRIP
    CW = mats.shape[-1]

    def ck(d, k):
        return jnp.where(_s5_backward(d, rev), NCH - 1 - k, k)

    def body(h_ref, hin_ref, a_ref, m_ref, ho_ref, y_ref, g, hs):
        @pl.when(pl.program_id(2) == 0)
        def _():
            g[...] = hin_ref[...]

        hs[...] = h_ref[...].astype(F32)
        ar, ai = jnp.broadcast_to(a_ref[0], (SUB, SL)), jnp.broadcast_to(a_ref[1], (SUB, SL))
        bw = _s5_backward(pl.program_id(0), rev)

        def step(t, c):
            gr, gi = c
            row = pl.multiple_of(jnp.where(bw, NG - 1 - t, t) * SUB, SUB)
            nr = ar * gr - ai * gi
            ni = ar * gi + ai * gr
            hs[0, pl.ds(row, SUB), :] += nr
            hs[1, pl.ds(row, SUB), :] += ni
            return nr, ni

        gr, gi = lax.fori_loop(0, NG, step, (g[0], g[1]))
        g[0], g[1] = gr, gi
        hb = hs[...].astype(BF16)
        ho_ref[...] = hb
        y_ref[...] = (jnp.dot(hb[0], m_ref[0], preferred_element_type=F32)
                      + jnp.dot(hb[1], m_ref[1], preferred_element_type=F32))

    outs, couts = _call(
        body, [hloc, hin, a, mats], name=name, grid=(2, NS, NCH),
        in_specs=[pl.BlockSpec((None, 2, TC, SL), lambda d, j, k: (d, 0, ck(d, k), j)),
                  pl.BlockSpec((None, 2, SUB, SL), lambda d, j, k: (d, 0, 0, j)),
                  pl.BlockSpec((None, 2, 1, SL), lambda d, j, k: (d, 0, 0, j)),
                  pl.BlockSpec((None, 2, None, SL, CW), lambda d, j, k: (d, 0, j, 0, 0))],
        out_specs=[pl.BlockSpec((None, 2, TC, SL), lambda d, j, k: (d, 0, ck(d, k), j)),
                   pl.BlockSpec((None, TC, CW), lambda d, j, k: (d, ck(d, k), j))],
        out_shape=[jax.ShapeDtypeStruct((2, 2, T, N), BF16), jax.ShapeDtypeStruct((2, T, NS * CW), F32)],
        scratch_shapes=[pltpu.VMEM((2, SUB, SL), F32), pltpu.VMEM((2, TC, SL), F32)],
        sem=("parallel", "parallel", "arbitrary"), comm=comm)
    return outs if comm is None else (outs, couts)


def s5_grads(name, g, h, u, dy, comm=None):
    _, _, T, N = g.shape
    W = u.shape[-1]
    TC, NG, NCH, NS = _s5_dims(T, N)
    CW, SL = W // NS, S5_STRIP

    def body(g_ref, h_ref, hp_ref, hl_ref, u_ref, dy_ref, dm_ref, dc_ref, da_ref, hs):
        k = pl.program_id(2)
        sub = lax.broadcasted_iota(jnp.int32, (SUB, SL), 0)

        hf = h_ref[...].astype(F32)

        @pl.when(pl.program_id(0) == 0)
        def _():
            for z in range(2):
                wrapped = jnp.where(sub == 0, 0.0, pltpu.roll(hl_ref[z].astype(F32)[SUB:], 1, 0))
                hs[z, 0:SUB, :] = jnp.where(k == 0, wrapped, hp_ref[z].astype(F32)[SUB:])
                hs[z, SUB:TC, :] = hf[z, 0:TC - SUB]

        @pl.when(pl.program_id(0) == 1)
        def _():
            for z in range(2):
                wrapped = jnp.where(sub == SUB - 1, 0.0, pltpu.roll(hl_ref[z].astype(F32)[:SUB], SUB - 1, 0))
                hs[z, TC - SUB:TC, :] = jnp.where(k == NCH - 1, wrapped, hp_ref[z].astype(F32)[:SUB])
                hs[z, 0:TC - SUB, :] = hf[z, SUB:TC]

        gr, gi, pr, pi = g_ref[0].astype(F32), g_ref[1].astype(F32), hs[0], hs[1]
        dar = jnp.sum((gr * pr + gi * pi).reshape(NG, SUB, SL), axis=0)
        dai = jnp.sum((gi * pr - gr * pi).reshape(NG, SUB, SL), axis=0)
        ub, dyb = u_ref[...].astype(BF16), dy_ref[...].astype(BF16)
        dm = [lax.dot_general(ub, g_ref[z], TN, preferred_element_type=F32) for z in range(2)]
        dc = [lax.dot_general(dyb, h_ref[z], TN, preferred_element_type=F32) for z in range(2)]

        @pl.when(k == 0)
        def _():
            da_ref[0], da_ref[1] = dar, dai
            for z in range(2):
                dm_ref[z], dc_ref[z] = dm[z], dc[z]

        @pl.when(k > 0)
        def _():
            da_ref[0] += dar
            da_ref[1] += dai
            for z in range(2):
                dm_ref[z] += dm[z]
                dc_ref[z] += dc[z]

    big = pl.BlockSpec((None, 2, TC, SL), lambda d, j, k: (d, 0, k, j))
    tok = pl.BlockSpec((None, TC, CW), lambda d, j, k: (d, k, j))
    mat = pl.BlockSpec((None, 2, None, CW, SL), lambda d, j, k: (d, 0, j, 0, 0))
    outs, couts = _call(
        body, [g, h, h, h, u, dy], name=name, grid=(2, NS, NCH),
        in_specs=[big, big,
                  pl.BlockSpec((None, 2, 2 * SUB, SL), lambda d, j, k: (
                      d, 0, jnp.where(d == 0, jnp.maximum(k * NG - 1, 0), jnp.minimum((k + 1) * NG, T // SUB - 1)) // 2, j)),
                  pl.BlockSpec((None, 2, 2 * SUB, SL), lambda d, j, k: (d, 0, jnp.where(d == 0, T // SUB - 1, 0) // 2, j)),
                  tok, tok],
        out_specs=[mat, mat, pl.BlockSpec((None, 2, SUB, SL), lambda d, j, k: (d, 0, 0, j))],
        out_shape=[jax.ShapeDtypeStruct((2, 2, NS, CW, SL), F32), jax.ShapeDtypeStruct((2, 2, NS, CW, SL), F32),
                   jax.ShapeDtypeStruct((2, 2, SUB, N), F32)],
        scratch_shapes=[pltpu.VMEM((2, TC, SL), F32)],
        sem=("parallel", "parallel", "arbitrary"), comm=comm)
    return outs if comm is None else (outs, couts)


def _interleave(seq):
    *lead, T, W = seq.shape
    n = len(lead)
    return seq.reshape(*lead, S5_SEG, T // S5_SEG, W).swapaxes(n, n + 1).reshape(*lead, T, W)


def _deinterleave(seq):
    *lead, T, W = seq.shape
    n = len(lead)
    return seq.reshape(*lead, T // S5_SEG, S5_SEG, W).swapaxes(n, n + 1).reshape(*lead, T, W)


def _s5_discretize(lam_re, lam_im, log_dt, b_re, b_im):
    dt = jnp.exp(log_dt)[..., None]
    mag = jnp.exp(lam_re * dt)
    a_re = mag * jnp.cos(lam_im * dt)
    a_im = mag * jnp.sin(lam_im * dt)
    den = jnp.square(lam_re) + jnp.square(lam_im)
    f_re = ((a_re - 1.0) * lam_re + a_im * lam_im) / den
    f_im = (a_im * lam_re - (a_re - 1.0) * lam_im) / den
    bb_re = f_re[..., None] * b_re - f_im[..., None] * b_im
    bb_im = f_re[..., None] * b_im + f_im[..., None] * b_re
    return a_re, a_im, bb_re, bb_im


_GPS = S5_STRIP // SSM_STATE


def _blockdiag(t):
    d2, G, P, Cg = t.shape
    t5 = t.reshape(d2, G // _GPS, _GPS, P, Cg).transpose(0, 1, 2, 4, 3)
    m = t5[:, :, :, :, None, :] * jnp.eye(_GPS, dtype=t.dtype)[None, None, :, None, :, None]
    return m.reshape(d2, G // _GPS, _GPS * Cg, _GPS * P)


def _blockdiag_extract(m, Cg, P):
    d2, NS = m.shape[:2]
    m6 = m.reshape(d2, NS, _GPS, Cg, _GPS, P)
    diag = jnp.stack([m6[:, :, i, :, i, :] for i in range(_GPS)], axis=2)
    return diag.transpose(0, 1, 2, 4, 3).reshape(d2, NS * _GPS, P, Cg)


def _cmul(a, b):
    return a[0] * b[0] - a[1] * b[1], a[0] * b[1] + a[1] * b[0]


def _cpow(a, n):
    out, base = None, a
    while n:
        if n & 1:
            out = base if out is None else _cmul(out, base)
        base = _cmul(base, base)
        n >>= 1
    return out


def _segment_carry(fin, apow, rev):
    per_dir = []
    for d in range(2):
        fr, fi = fin[d, 0], fin[d, 1]
        ap = (apow[0][d], apow[1][d])
        cr = ci = jnp.zeros_like(fr[0:1])
        outs = [None] * S5_SEG
        backward = (d == 1) != rev
        for s in (range(S5_SEG - 1, -1, -1) if backward else range(S5_SEG)):
            outs[s] = (cr, ci)
            pr, pi = _cmul(ap, (cr, ci))
            cr, ci = pr + fr[s:s + 1], pi + fi[s:s + 1]
        per_dir.append(jnp.stack([jnp.concatenate([o[0] for o in outs]), jnp.concatenate([o[1] for o in outs])]))
    return jnp.stack(per_dir)


def _coords():
    x, y, c = lax.axis_index("x"), lax.axis_index("y"), lax.axis_index("c")
    others = [(1 - x, y), (x, 1 - y), (1 - x, 1 - y)]
    return x, y, c, 2 * x + y, others


def _comm(name, ins, out_shapes, aliases, n_local, n_remote, plan):
    n_in, n_out = len(ins), len(out_shapes)

    def body(*refs):
        in_refs, out_refs = refs[:n_in], refs[n_in:n_in + n_out]
        send_sems, recv_sems, local_sems = refs[n_in + n_out:]
        x, y, c = lax.axis_index("x"), lax.axis_index("y"), lax.axis_index("c")
        locs, sends, lands = plan(in_refs, out_refs)
        assert len(locs) == n_local and len(sends) == n_remote and len(lands) == n_remote
        local = [pltpu.make_async_copy(s, d, local_sems.at[i]) for i, (s, d) in enumerate(locs)]
        for cp in local:
            cp.start()
        remote = [pltpu.make_async_remote_copy(src_ref=s, dst_ref=d, send_sem=send_sems.at[i], recv_sem=recv_sems.at[i],
                                               device_id=peer, device_id_type=MESH)
                  for i, (s, d, peer) in enumerate(sends)]
        for cp in remote:
            cp.start()
        for i, d in enumerate(lands):
            pltpu.make_async_remote_copy(src_ref=d, dst_ref=d, send_sem=send_sems.at[i], recv_sem=recv_sems.at[i],
                                         device_id=(x, y, c), device_id_type=MESH).wait_recv()
        for cp in remote:
            cp.wait_send()
        for cp in local:
            cp.wait()

    any_spec = pl.BlockSpec(memory_space=pl.ANY)
    return pl.pallas_call(
        body, name=name,
        in_specs=[any_spec] * n_in, out_specs=[any_spec] * n_out,
        out_shape=[jax.ShapeDtypeStruct(s, d) for s, d in out_shapes],
        input_output_aliases=aliases,
        scratch_shapes=[pltpu.SemaphoreType.DMA((n_remote,)), pltpu.SemaphoreType.DMA((n_remote,)),
                        pltpu.SemaphoreType.DMA((max(n_local, 1),))],
        compiler_params=pltpu.CompilerParams(has_side_effects=True),
    )(*ins)


def allgather_dev(name, v):
    M, Nc = v.shape

    def plan(in_refs, out_refs):
        (v_ref,), (o_ref,) = in_refs, out_refs
        x, y, c = lax.axis_index("x"), lax.axis_index("y"), lax.axis_index("c")

        def rows(px, py, pc):
            return o_ref.at[pl.ds((4 * px + 2 * py + pc) * M, M), :]

        peers = [(x ^ fx, y ^ fy, c ^ fc) for fx in (0, 1) for fy in (0, 1) for fc in (0, 1) if fx or fy or fc]
        return ([(v_ref, rows(x, y, c))],
                [(v_ref, rows(x, y, c), p) for p in peers],
                [rows(*p) for p in peers])

    return _comm(name, [v], [((N_DEV * M, Nc), v.dtype)], {}, 1, N_DEV - 1, plan)[0]


def allgather_chips_1(name, shards):
    def plan(in_refs, out_refs):
        x, y, c, chip, others = _coords()
        sends, lands = [], []
        for s_ref, g_ref in zip(in_refs, out_refs):
            hr = s_ref.shape[0] // 2
            mine = pl.ds(c * hr, hr)
            for qx, qy in others:
                sends.append((s_ref.at[mine], g_ref.at[chip, mine], (qx, qy, c)))
                lands.append(g_ref.at[2 * qx + qy, mine])
        return [], sends, lands

    n = len(shards)
    comm = (list(shards), [((N_CHIP,) + s.shape, s.dtype) for s in shards], {}, 3 * n, plan)
    return comm if name is None else _comm(name, comm[0], comm[1], comm[2], 0, comm[3], comm[4])


def allgather_chips_2(name, gathered, shards):
    n = len(gathered)

    def plan(in_refs, out_refs):
        x, y, c, chip, others = _coords()
        sends, lands = [], []
        for s_ref, g_ref in zip(in_refs[n:], out_refs):
            hr = g_ref.shape[1] // 2
            for qx, qy in others:
                q = 2 * qx + qy
                sends.append((g_ref.at[q, pl.ds(c * hr, hr)], g_ref.at[q, pl.ds(c * hr, hr)], (x, y, 1 - c)))
                lands.append(g_ref.at[q, pl.ds((1 - c) * hr, hr)])
            sends.append((s_ref, g_ref.at[chip], (x, y, 1 - c)))
            lands.append(g_ref.at[chip])
        return [], sends, lands

    comm = (list(gathered) + list(shards), [(g.shape, g.dtype) for g in gathered], {i: i for i in range(n)}, 4 * n, plan)
    return comm if name is None else _comm(name, comm[0], comm[1], comm[2], 0, comm[3], comm[4])


def reduce_1(name, grads):
    def plan(in_refs, out_refs):
        x, y, c, chip, others = _coords()
        sends, lands = [], []
        for g_ref, got_ref in zip(in_refs, out_refs):
            hr = g_ref.shape[1] // 2
            sends.append((g_ref.at[:, pl.ds((1 - c) * hr, hr), :], got_ref, (x, y, 1 - c)))
            lands.append(got_ref)
        return [], sends, lands

    n = len(grads)
    comm = (list(grads), [((g.shape[0], g.shape[1] // 2, g.shape[2]), g.dtype) for g in grads], {}, n, plan)
    return comm if name is None else _comm(name, comm[0], comm[1], comm[2], 0, comm[3], comm[4])


def _merge_comm(a, b):
    if a is None or b is None:
        return a if b is None else b
    na_in, na_out = len(a[0]), len(a[1])

    def plan(in_refs, out_refs):
        _, s1, l1 = a[4](in_refs[:na_in], out_refs[:na_out])
        _, s2, l2 = b[4](in_refs[na_in:], out_refs[na_out:])
        return [], s1 + s2, l1 + l2

    alias = dict(a[2])
    alias.update({na_in + i: na_out + j for i, j in b[2].items()})
    return (a[0] + b[0], a[1] + b[1], alias, a[3] + b[3], plan)


def reduce_2(name, parts):
    def plan(in_refs, out_refs):
        x, y, c, chip, others = _coords()
        sends, lands = [], []
        for t_ref, q_ref in zip(in_refs, out_refs):
            for qx, qy in others:
                sends.append((t_ref.at[2 * qx + qy], q_ref.at[chip], (qx, qy, c)))
                lands.append(q_ref.at[2 * qx + qy])
        return [], sends, lands

    n = len(parts)
    comm = (list(parts), [(p.shape, p.dtype) for p in parts], {}, 3 * n, plan)
    return comm if name is None else _comm(name, comm[0], comm[1], comm[2], 0, comm[3], comm[4])


def share_slots(name, slots):
    def plan(in_refs, out_refs):
        x, y, c, chip, others = _coords()
        (q_ref,) = out_refs
        return ([], [(q_ref.at[chip], q_ref.at[chip], (qx, qy, c)) for qx, qy in others],
                [q_ref.at[2 * qx + qy] for qx, qy in others])

    return _comm(name, [slots], [(slots.shape, slots.dtype)], {0: 0}, 0, N_CHIP - 1, plan)[0]


def allreduce_small(tag, buf, ids):
    got = reduce_1(tag + "_1", [buf[None]])[0][0]
    slots = share_slots(tag + "_2", pair_sum_to_slot(tag + "_add", buf, got, ids))
    full = reduce_3(tag + "_3", [sum_chips_to_half(tag + "_sum", slots, ids)])[0]
    return full.reshape(buf.shape)


def reduce_3(name, fulls):
    def plan(in_refs, out_refs):
        x, y, c, chip, others = _coords()
        sends, lands = [], []
        for o_ref in out_refs:
            sends.append((o_ref.at[c], o_ref.at[c], (x, y, 1 - c)))
            lands.append(o_ref.at[1 - c])
        return [], sends, lands

    n = len(fulls)
    return _comm(name, fulls, [(f.shape, f.dtype) for f in fulls], {i: i for i in range(n)}, 0, n, plan)


_WEIGHTS = ['c_ctx', 'w_mod', 'b_mod', 'g_mix', 'g_ffn', 'w_in', 'ssm_lam_re', 'ssm_lam_im', 'ssm_log_dt', 'ssm_b_re',
            'ssm_b_im', 'ssm_c_re', 'ssm_c_im', 'ssm_d', 'ssm_w_glu', 'na_rpb', 'w_out', 'cv_w_pw1', 'cv_dw_w', 'cv_dw_b',
            'cv_ln_g', 'cv_ln_b', 'cv_w_pw2', 'ffn_w_up', 'ffn_conv_w', 'ffn_conv_b', 'ffn_w_down', 'g_out']
_INPUTS = ['x', 'c', 'ctx'] + _WEIGHTS + ['loss_target'] + ['m_' + w for w in _WEIGHTS] + ['v_' + w for w in _WEIGHTS]
_GATHERED_SMALL = ['ffn_conv_w', 'cv_dw_w', 'cv_dw_b', 'cv_ln_g', 'cv_ln_b']


def _silu(v):
    return v * jax.nn.sigmoid(v)


def _pick_index(t, idx, axis):
    shape = [1] * t.ndim
    shape[axis] = t.shape[axis]
    mask = (jnp.arange(t.shape[axis]) == idx).reshape(shape)
    return jnp.sum(jnp.where(mask, t, jnp.zeros((), t.dtype)), axis=axis)


def _pack(arrs, cols, row_mult=SUB):
    flat = jnp.concatenate([a.reshape(-1).astype(F32) for a in arrs])
    n = flat.shape[0]
    unit = row_mult * cols
    flat = jnp.pad(flat, (0, (-n) % unit))
    return flat.reshape(-1, cols)


def _unpack(buf, shapes):
    flat = buf.reshape(-1)
    out, o = [], 0
    for s in shapes:
        n = int(np.prod(s))
        out.append(flat[o:o + n].reshape(s))
        o += n
    return out


def _carried(res, comm):
    return res if comm is not None else (res, [])


def _ffn_fwd(tag, xin, sh, sc, gt, g, wup, cw3, cb3, wdn, comm_up=None, comm_mid=None, comm_down=None):
    hf = norm_mod_fwd(tag + "_norm", xin, g * (1.0 + sc), sh)
    up3, got_up = _carried(mm_nn_pieces(tag + "_up", hf, wup, 0, N_CHIP, BF16, halves=2, comm=comm_up), comm_up)
    comm_mid = comm_mid(got_up) if callable(comm_mid) else comm_mid
    act, got_mid = _carried(ffn_mid_fwd(tag + "_mid", up3, cw3, cb3, comm=comm_mid), comm_mid)
    comm_down = comm_down(got_mid) if callable(comm_down) else comm_down
    yf, got_down = _carried(mm_nn(tag + "_down", act, wdn, BF16, comm=comm_down), comm_down)
    return gate_res_fwd(tag + "_res", xin, yf, gt), (xin, hf, up3, act, yf), got_up, got_mid, got_down


def _ffn_bwd(tag, dxo, saved, sc, gt, g, wup, cw3, cb3, wdn, comm_down=None, comm_mid=None, comm_up=None):
    xin, hf, up3, act, yf = saved
    dyf, dgt = gate_res_bwd(tag + "_res_b", dxo, yf, gt)
    dact, got_down = _carried(mm_nt(tag + "_down_bx", dyf, wdn, BF16, comm=comm_down), comm_down)
    dwdn = mm_tn(tag + "_down_bw", act, dyf, BF16)
    comm_mid = comm_mid(got_down) if callable(comm_mid) else comm_mid
    (dup3, dcw3, dcb3), got_mid = _carried(ffn_mid_bwd(tag + "_mid_b", up3, dact, cw3, cb3, comm=comm_mid), comm_mid)
    dhf, got_up = _carried(mm_nt_pieces(tag + "_up_bx", dup3, wup, BF16, halves=2, comm=comm_up), comm_up)
    dwup = mm_tn_pieces(tag + "_up_bw", hf, dup3, N_CHIP, BF16, halves=2)
    dxi, cs1, cs2 = norm_mod_bwd(tag + "_norm_b", xin, dhf, g * (1.0 + sc), dxo)
    return dxi, dict(dsh=cs1[0], dsc=cs2[0] * g, dgt=dgt[0], dg=cs2[0] * (1.0 + sc), dwup=dwup, dwdn=dwdn,
                     dcw=dcw3.transpose(1, 0, 2).reshape(3, -1), dcb=dcb3.reshape(-1)), got_down, got_mid, got_up


def kernel(x, c, ctx, c_ctx, w_mod, b_mod, g_mix, g_ffn, w_in, ssm_lam_re, ssm_lam_im, ssm_log_dt, ssm_b_re, ssm_b_im, ssm_c_re, ssm_c_im, ssm_d, ssm_w_glu, na_rpb, w_out, cv_w_pw1, cv_dw_w, cv_dw_b, cv_ln_g, cv_ln_b, cv_w_pw2, ffn_w_up, ffn_conv_w, ffn_conv_b, ffn_w_down, g_out, loss_target, m_c_ctx, m_w_mod, m_b_mod, m_g_mix, m_g_ffn, m_w_in, m_ssm_lam_re, m_ssm_lam_im, m_ssm_log_dt, m_ssm_b_re, m_ssm_b_im, m_ssm_c_re, m_ssm_c_im, m_ssm_d, m_ssm_w_glu, m_na_rpb, m_w_out, m_cv_w_pw1, m_cv_dw_w, m_cv_dw_b, m_cv_ln_g, m_cv_ln_b, m_cv_w_pw2, m_ffn_w_up, m_ffn_conv_w, m_ffn_conv_b, m_ffn_w_down, m_g_out, v_c_ctx, v_w_mod, v_b_mod, v_g_mix, v_g_ffn, v_w_in, v_ssm_lam_re, v_ssm_lam_im, v_ssm_log_dt, v_ssm_b_re, v_ssm_b_im, v_ssm_c_re, v_ssm_c_im, v_ssm_d, v_ssm_w_glu, v_na_rpb, v_w_out, v_cv_w_pw1, v_cv_dw_w, v_cv_dw_b, v_cv_ln_g, v_cv_ln_b, v_cv_w_pw2, v_ffn_w_up, v_ffn_conv_w, v_ffn_conv_b, v_ffn_w_down, v_g_out):
    p = dict(locals())
    xi, yi, ci = lax.axis_index("x"), lax.axis_index("y"), lax.axis_index("c")
    me, chip = 4 * xi + 2 * yi + ci, 2 * xi + yi
    xs, cx, tgt = x[0], ctx[0], loss_target[0]
    L, D = xs.shape
    Lc = cx.shape[0]
    T = L + Lc
    W = D // 2
    Cq = w_mod.shape[2]

    s_mix = [t.astype(BF16) for t in (w_in[0], ssm_w_glu[0], w_out[0])]
    s_ffn0 = [t.astype(BF16) for t in (ffn_w_up[0], ffn_w_down[0])]
    s_conv = [t.astype(BF16) for t in (cv_w_pw1[0], cv_w_pw2[0])]
    s_ffn1 = [t.astype(BF16) for t in (ffn_w_up[1], ffn_w_down[1])]
    (Win,) = allgather_chips_2("gather_win_2", allgather_chips_1("gather_win_1", s_mix[:1]), s_mix[:1])
    Fd = ffn_w_down.shape[1] * N_CHIP
    c_idx = jnp.reshape(ci, (1,)).astype(jnp.int32)
    ids = jnp.stack([chip, ci]).astype(jnp.int32)

    def added(tag, grads, got):
        return [add_half("reduce_%s_add%d" % (tag, i), g, r, c_idx) for i, (g, r) in enumerate(zip(grads, got))]

    small_shapes = [p[n].shape for n in _GATHERED_SMALL]
    sm = allgather_dev("gather_small", _pack([p[n] for n in _GATHERED_SMALL], 1024))
    sm = sm.reshape(N_DEV, -1)[0::2]
    per_chip = [_unpack(sm[q], small_shapes) for q in range(N_CHIP)]
    conv_w_f, dw_w_f, dw_b_f, ln_g_f, ln_b_f = (jnp.concatenate([pc[i] for pc in per_chip], axis=-1)
                                                for i in range(len(_GATHERED_SMALL)))
    cw3 = [conv_w_f[l].reshape(3, 2, Fd).transpose(1, 0, 2) for l in range(2)]
    cb3 = [ffn_conv_b[l].reshape(2, 1, Fd) for l in range(2)]
    dw_w_f, dw_b_f, ln_g_f, ln_b_f = dw_w_f[0], dw_b_f[0], ln_g_f[0], ln_b_f[0]

    c_all = allgather_dev("gather_c", jnp.zeros((SUB, D), F32).at[0].set(c[0])).reshape(N_DEV, SUB, D)[:, 0]
    S16 = jnp.concatenate([_silu(c_all), _silu(c_ctx)[None], jnp.zeros((2 * SUB - N_DEV - 1, D), F32)])
    modp = mm_nn_pieces("mod_fwd", S16, w_mod, 0, 2, F32)
    modg = allgather_dev("gather_mod", modp).reshape(N_DEV, 2 * SUB, 2, Cq)[0::2]

    def mod_row(r):
        return r.transpose(1, 0, 2).reshape(2, N_CHIP * Cq) + b_mod

    mod_me = mod_row(_pick_index(modg, me, 1))
    mod_c = mod_row(modg[:, N_DEV])
    mods = [[mod_me[l, i * D:(i + 1) * D] for i in range(N_MOD)] for l in range(2)]
    shc, scc = mod_c[0, :D], mod_c[0, D:2 * D]

    sh_m, sc_m, gt_m, sh_f, sc_f, gt_f = mods[0]
    h0 = norm_mod_fwd("l0_norm", xs, g_mix[0] * (1.0 + sc_m), sh_m)
    hc0 = norm_mod_fwd("l0_norm_c", cx, g_mix[0] * (1.0 + scc), shc)
    u = mm_nn_pieces("l0_in_u", h0, Win, 0, 1, F32)
    qkv, g_mix1 = mm_nn_pieces("l0_in_qkv", h0, Win, 1, 3, BF16, comm=allgather_chips_1(None, s_mix[1:]))
    uc = mm_nn_pieces("l0_in_uc", hc0, Win, 0, 1, F32)
    kvc = mm_nn_pieces("l0_in_kvc", hc0, Win, 2, 2, BF16)

    lam_re, lam_im, log_dt = ssm_lam_re[0], ssm_lam_im[0], ssm_log_dt[0]
    b_re, b_im, c_re, c_im = ssm_b_re[0], ssm_b_im[0], ssm_c_re[0], ssm_c_im[0]
    (a_re, a_im, bb_re, bb_im), disc_vjp = jax.vjp(_s5_discretize, lam_re, lam_im, log_dt, b_re, b_im)
    G, P, Cg = bb_re.shape[1:]
    N = G * P
    a_re, a_im = a_re.reshape(2, 1, N), a_im.reshape(2, 1, N)
    a_f, a_b = jnp.stack([a_re, a_im], axis=1), jnp.stack([a_re, -a_im], axis=1)
    Bblk = jnp.stack([_blockdiag(bb_re), _blockdiag(bb_im)], axis=1)
    Cblk = jnp.stack([_blockdiag(c_re.swapaxes(-1, -2)), -_blockdiag(c_im.swapaxes(-1, -2))], axis=1)
    apow = _cpow((a_re, a_im), T // S5_SEG)

    useq = _interleave(jnp.stack([jnp.concatenate([uc, u]), jnp.concatenate([u, uc])]).astype(BF16))
    (hloc, fin), (Wglu, Wout, g_dn0) = s5_scan(
        "s5_scan", useq, Bblk.astype(BF16), a_f, rev=False,
        comm=_merge_comm(allgather_chips_2(None, g_mix1, s_mix[1:]), allgather_chips_1(None, s_ffn0[1:])))
    Wglu, Wout = Wglu.reshape(-1, Wglu.shape[-1]), Wout.reshape(-1, D)
    (hst, yseq), (g_pw1, g_pw2) = s5_fix("s5_fix", hloc, _segment_carry(fin, apow, False), a_f,
                                         Cblk.swapaxes(-1, -2).astype(BF16), rev=False, comm=allgather_chips_1(None, s_conv))
    g_dn0 = [g_dn0]
    ys = _deinterleave(yseq)
    y0, y1 = ys[0, Lc:], ys[1, :L]
    s5o = glu_fwd("s5_glu", u, y0, y1, ssm_d[0], Wglu)

    bias = na_bias(na_rpb[0])
    (o_na, lse), (g_up0, Wdn0) = natten_fwd(
        "na_fwd", qkv, kvc, bias,
        comm=_merge_comm(allgather_chips_1(None, s_ffn0[:1]), allgather_chips_2(None, g_dn0, s_ffn0[1:])))
    mixcat = jnp.concatenate([s5o, o_na], axis=1)
    ymix, (Wup0,) = mm_nn("l0_out", mixcat, Wout, BF16, comm=allgather_chips_2(None, [g_up0], s_ffn0[:1]))
    x1 = gate_res_fwd("l0_res", xs, ymix, gt_m)
    Wdn0 = Wdn0.reshape(-1, D)
    x2, ffn0, (g_up1,), (g_dn1, Wup1), (Wdn1,) = _ffn_fwd(
        "f0", x1, sh_f, sc_f, gt_f, g_ffn[0], Wup0, cw3[0], cb3[0], Wdn0,
        comm_up=allgather_chips_1(None, s_ffn1[:1]),
        comm_mid=lambda got_up: _merge_comm(allgather_chips_1(None, s_ffn1[1:]), allgather_chips_2(None, got_up, s_ffn1[:1])),
        comm_down=lambda got_mid: allgather_chips_2(None, got_mid[:1], s_ffn1[1:]))
    Wpw1, Wpw2 = allgather_chips_2("gather_conv_2", [g_pw1, g_pw2], s_conv)
    Wpw2 = Wpw2.reshape(-1, D)
    Wup, Wdn = [Wup0, Wup1], [Wdn0.reshape(-1, D), Wdn1.reshape(-1, D)]

    sh_v, sc_v, gt_v, sh_g, sc_g, gt_g = mods[1]
    hcv = norm_mod_fwd("l1_norm", x2, g_mix[1] * (1.0 + sc_v), sh_v)
    ag3 = mm_nn_pieces("l1_pw1", hcv, Wpw1, 0, N_CHIP, BF16, halves=2)
    z1, z3 = conf_mid_fwd("l1_mid", ag3, dw_w_f, dw_b_f, ln_g_f, ln_b_f)
    ycv = mm_nn("l1_pw2", z3, Wpw2, BF16)
    x3 = gate_res_fwd("l1_res", x2, ycv, gt_v)
    x4, ffn1, _, _, _ = _ffn_fwd("f1", x3, sh_g, sc_g, gt_g, g_ffn[1], Wup[1], cw3[1], cb3[1], Wdn[1])

    dx4, dg_out, loss_part = loss_head("loss", x4, g_out, tgt)
    loss = lax.psum(loss_part[0, 0], ("x", "y", "c"))

    dx3, gf1, _, _, _ = _ffn_bwd("f1", dx4, ffn1, sc_g, gt_g, g_ffn[1], Wup[1], cw3[1], cb3[1], Wdn[1])
    g_up1, g_dn1 = [gf1["dwup"]], [gf1["dwdn"].reshape(N_CHIP, -1, D)]
    dycv, dgt_v = gate_res_bwd("l1_res_b", dx3, ycv, gt_v)
    dz3, got = mm_nt("l1_pw2_bx", dycv, Wpw2, BF16, comm=reduce_1(None, g_up1))
    parts_up1 = added("up1", g_up1, got)
    dWpw2, got = mm_tn("l1_pw2_bw", z3, dycv, BF16, comm=reduce_1(None, g_dn1))
    parts_dn1 = added("dn1", g_dn1, got)
    dz1, dln_g, dln_b = conf_ln_bwd("l1_ln_b", z1, dz3, ln_g_f, ln_b_f)
    (dag3, ddw_w, ddw_b), slots_up1 = conf_conv_bwd("l1_conv_b", ag3, dz1, dw_w_f, comm=reduce_2(None, parts_up1))
    dhcv = mm_nt_pieces("l1_pw1_bx", dag3, Wpw1, BF16, halves=2)
    dWpw1 = mm_tn_pieces("l1_pw1_bw", hcv, dag3, N_CHIP, BF16, halves=2)
    dx2, cs1_v, cs2_v = norm_mod_bwd("l1_norm_b", x2, dhcv, g_mix[1] * (1.0 + sc_v), dx3)
    g_conv = [dWpw1, dWpw2.reshape(N_CHIP, -1, D)]

    held = {}

    def conv_stage_2(got_down):
        held["parts_conv"] = added("conv", g_conv, got_down)
        return _merge_comm(reduce_2(None, held["parts_conv"]), reduce_2(None, parts_dn1))

    dx1, gf0, _, slots_mid, _ = _ffn_bwd("f0", dx2, ffn0, sc_f, gt_f, g_ffn[0], Wup[0], cw3[0], cb3[0], Wdn[0],
                                          comm_down=reduce_1(None, g_conv), comm_mid=conv_stage_2)
    slots_conv, slots_dn1 = slots_mid[:2], slots_mid[2:]
    parts_conv = held["parts_conv"]
    g_up0, g_dn0 = [gf0["dwup"]], [gf0["dwdn"].reshape(N_CHIP, -1, D)]
    dymix, dgt_m = gate_res_bwd("l0_res_b", dx1, ymix, gt_m)
    dmix, got = mm_nt("l0_out_bx", dymix, Wout, BF16, comm=reduce_1(None, g_up0))
    parts_up0 = added("up0", g_up0, got)
    dWout, got = mm_tn("l0_out_bw", mixcat, dymix, BF16, comm=reduce_1(None, g_dn0))
    parts_dn0 = added("dn0", g_dn0, got)
    (dq, dk, dv, dkc, dvc, dbias), slots_up0 = natten_bwd("na_bwd", qkv, kvc, bias, o_na, lse, dmix,
                                                          comm=reduce_2(None, parts_up0))
    dy, zg, dzz, dd_skip = glu_bwd("s5_glu_b", u, y0, y1, ssm_d[0], Wglu, dmix)
    dWglu = mm_tn("s5_glu_bw", zg, dzz, BF16)
    g_mix2 = [dWglu.reshape(N_CHIP, -1, W), dWout.reshape(N_CHIP, -1, D)]

    zc = jnp.zeros((Lc, W), F32)
    dyseq = _interleave(jnp.stack([jnp.concatenate([zc, dy]), jnp.concatenate([dy, zc])]).astype(BF16))
    (gloc, gfin), slots_dn0 = s5_scan("s5_scan_b", dyseq, Cblk.astype(BF16), a_b, rev=True, comm=reduce_2(None, parts_dn0))
    apow_b = (apow[0], -apow[1])
    (gst, duseq), got = s5_fix("s5_fix_b", gloc, _segment_carry(gfin, apow_b, True), a_b, Bblk.swapaxes(-1, -2).astype(BF16),
                               rev=True, comm=reduce_1(None, g_mix2))
    parts_mix2 = added("mix2", g_mix2, got)
    (dBm, dCm, da8), slots_mix2 = s5_grads("s5_grads", gst, hst, useq, dyseq, comm=reduce_2(None, parts_mix2))
    dus = _deinterleave(duseq)
    du = fma3("s5_du", dy, dus[0, Lc:], dus[1, :L], ssm_d[0], BF16)
    duc = dus[0, :Lc] + dus[1, L:]

    d_in = [du, dq, dk, dv]
    d_in_c = [duc, jnp.zeros((Lc, W), BF16), dkc, dvc]
    dh0 = mm_nt_list("l0_in_bx", d_in, Win, BF16)
    dhc0 = mm_nt_list("l0_in_bxc", d_in_c, Win, BF16)
    h_all = jnp.concatenate([hc0, h0])
    dWin = jnp.stack([mm_tn("l0_in_bw%d" % q, h_all, jnp.concatenate([dc.astype(BF16), dl.astype(BF16)]), BF16)
                      for q, (dc, dl) in enumerate(zip(d_in_c, d_in))])
    dx0, cs1_m, cs2_m = norm_mod_bwd("l0_norm_b", xs, dh0, g_mix[0] * (1.0 + sc_m), dx1)
    _, cs1_c, cs2_c = norm_mod_bwd("l0_norm_bc", cx, dhc0, g_mix[0] * (1.0 + scc), jnp.zeros_like(cx))

    dmod0 = jnp.concatenate([cs1_m[0], cs2_m[0] * g_mix[0], dgt_m[0], gf0["dsh"], gf0["dsc"], gf0["dgt"]])
    dmod1 = jnp.concatenate([cs1_v[0], cs2_v[0] * g_mix[1], dgt_v[0], gf1["dsh"], gf1["dsc"], gf1["dgt"]])
    dmodc = jnp.concatenate([cs1_c[0], cs2_c[0] * g_mix[0], jnp.zeros((4 * D,), F32)])
    dm_rows = jnp.concatenate([jnp.stack([dmod0, dmod1, dmodc]), jnp.zeros((SUB - 3, N_MOD * D), F32)])
    dm_all = allgather_dev("gather_dmod", dm_rows).reshape(N_DEV, SUB, N_MOD * D)
    dm_sum = sum_lead("sum_dmod", dm_all, F32)
    pad7 = jnp.zeros((2 * SUB - N_DEV - 1, N_MOD * D), F32)
    dMod = [jnp.concatenate([dm_all[:, 0], dm_sum[2:3], pad7]), jnp.concatenate([dm_all[:, 1], jnp.zeros_like(dm_sum[2:3]), pad7])]
    dMod_cols = [_pick_index(m.reshape(m.shape[0], N_CHIP, Cq), chip, 1) for m in dMod]
    g_w_mod = jnp.stack([mm_tn("mod_bw%d" % l, S16, dMod_cols[l], F32) for l in range(2)])
    g_b_mod = jnp.stack([dm_sum[0] + dm_sum[2], dm_sum[1]])
    ds_part = mm_nt("mod_bx", dMod_cols[0], w_mod[0], F32)
    ds_all = allgather_dev("gather_dsc", jnp.zeros((SUB, D), F32).at[0].set(ds_part[N_DEV]))
    ds_c = sum_lead("sum_dsc", ds_all.reshape(N_DEV, SUB, D)[0::2], F32)[0]
    sg_c = jax.nn.sigmoid(c_ctx)
    g_c_ctx = ds_c * sg_c * (1.0 + c_ctx * (1.0 - sg_c))

    g_rpb_loc = na_bias_grad(dbias)

    dbb = [_blockdiag_extract(dBm[:, z], Cg, P) for z in range(2)]
    dcc = [_blockdiag_extract(dCm[:, z], Cg, P).swapaxes(-1, -2) for z in range(2)]
    da = jnp.sum(da8, axis=2).reshape(2, 2, G, P)
    small = {
        "g_mix": jnp.stack([cs2_m[0] * (1.0 + sc_m) + cs2_c[0] * (1.0 + scc), cs2_v[0] * (1.0 + sc_v)]),
        "g_ffn": jnp.stack([gf0["dg"], gf1["dg"]]),
        "a_re": da[:, 0], "a_im": da[:, 1], "bb_re": dbb[0], "bb_im": dbb[1], "c_re": dcc[0], "c_im": -dcc[1],
        "ssm_d": dd_skip, "na_rpb": g_rpb_loc, "cv_dw_w": ddw_w, "cv_dw_b": ddw_b, "cv_ln_g": dln_g, "cv_ln_b": dln_b,
        "ffn_conv_w": jnp.stack([gf0["dcw"], gf1["dcw"]]), "ffn_conv_b": jnp.stack([gf0["dcb"], gf1["dcb"]]),
        "g_out": dg_out,
    }
    skeys = list(small)
    sbuf = _pack([small[k] for k in skeys], 1024, 4 * SUB)
    ssum = dict(zip(skeys, _unpack(allreduce_small("reduce_small", sbuf, ids), [small[k].shape for k in skeys])))
    g_lam_re, g_lam_im, g_log_dt, g_b_re, g_b_im = disc_vjp((ssum["a_re"], ssum["a_im"], ssum["bb_re"], ssum["bb_im"]))

    def my_cols(t):
        n = t.shape[-1] // N_CHIP
        return _pick_index(t.reshape(t.shape[:-1] + (N_CHIP, n)), chip, t.ndim - 1)

    delta, new_m, new_v = {}, {}, {}
    parts_win = added("win", [dWin], reduce_1("reduce_win_1", [dWin]))
    (delta["w_mod"], new_m["w_mod"], new_v["w_mod"]), slots_win = adamw(
        "adamw_w_mod", w_mod, g_w_mod, m_w_mod, v_w_mod, comm=reduce_2(None, parts_win))
    parts = parts_win + parts_mix2 + parts_conv + parts_up0 + parts_dn0 + parts_up1 + parts_dn1
    slots = [*slots_win, *slots_mix2, *slots_conv, *slots_up0, *slots_dn0, *slots_up1, *slots_dn1]
    fulls = [sum_slots("reduce_sum_%d" % i, s, t, ids) for i, (s, t) in enumerate(zip(slots, parts))]
    full = [f.reshape(-1, f.shape[-1]) for f in reduce_3("reduce_g_3", fulls)]
    gWin, gWglu, gWout, gWpw1, gWpw2, gWup0, gWdn0, gWup1, gWdn1 = full

    grads = {
        "c_ctx": g_c_ctx, "w_mod": g_w_mod, "b_mod": g_b_mod, "g_mix": ssum["g_mix"], "g_ffn": ssum["g_ffn"],
        "w_in": gWin[None], "ssm_lam_re": g_lam_re[None], "ssm_lam_im": g_lam_im[None], "ssm_log_dt": g_log_dt[None],
        "ssm_b_re": g_b_re[None], "ssm_b_im": g_b_im[None], "ssm_c_re": ssum["c_re"][None], "ssm_c_im": ssum["c_im"][None],
        "ssm_d": ssum["ssm_d"], "ssm_w_glu": gWglu[None], "na_rpb": ssum["na_rpb"][None], "w_out": gWout[None],
        "cv_w_pw1": gWpw1[None], "cv_dw_w": my_cols(ssum["cv_dw_w"])[None], "cv_dw_b": my_cols(ssum["cv_dw_b"]),
        "cv_ln_g": my_cols(ssum["cv_ln_g"]), "cv_ln_b": my_cols(ssum["cv_ln_b"]), "cv_w_pw2": gWpw2[None],
        "ffn_w_up": jnp.stack([gWup0, gWup1]), "ffn_conv_w": my_cols(ssum["ffn_conv_w"]), "ffn_conv_b": ssum["ffn_conv_b"],
        "ffn_w_down": jnp.stack([gWdn0, gWdn1]), "g_out": ssum["g_out"][0],
    }
    grads = {k: grads[k].reshape(p[k].shape) for k in _WEIGHTS}

    large = [k for k in _WEIGHTS if p[k].size >= (1 << 18) or k == "w_mod"]
    tiny = [k for k in _WEIGHTS if k not in large]
    for k in large:
        if k != "w_mod":
            delta[k], new_m[k], new_v[k] = adamw("adamw_" + k, p[k], grads[k], p["m_" + k], p["v_" + k])
    packs = [_pack([src[pre + k] for k in tiny], 1024) for src, pre in ((p, ""), (grads, ""), (p, "m_"), (p, "v_"))]
    outs = adamw("adamw_small", *packs)
    shapes = [p[k].shape for k in tiny]
    for dst, buf in zip((delta, new_m, new_v), outs):
        dst.update(zip(tiny, _unpack(buf, shapes)))

    return (loss, dx0[None], *[grads[k] for k in _WEIGHTS], *[delta[k] for k in _WEIGHTS],
            *[new_m[k] for k in _WEIGHTS], *[new_v[k] for k in _WEIGHTS])
```

```python
import functools
import math

import numpy as np
import jax
import jax.numpy as jnp
from jax import lax
from jax.experimental import pallas as pl
from jax.experimental.pallas import tpu as pltpu

F32, BF16 = jnp.float32, jnp.bfloat16
MESH = pl.DeviceIdType.MESH
V7X_VMEM_LIMIT = 56 << 20
LANE, SUB = 128, 8
N_CHIP, N_DEV = 4, 8

GRID_W = 64
N_MOD = 6
SSM_GROUP, SSM_STATE = 16, 64
NA_HEAD_DIM, NA_WIN_R, NA_WIN_C = 128, 8, 16
EPS = 1e-6
NEG = -1e30
ADAM_LR, ADAM_B1, ADAM_B2, ADAM_EPS, ADAM_WD, ADAM_STEP = 0.001, 0.9, 0.999, 1e-08, 0.01, 10
S5_STRIP = 512
S5_SEG = 8

NN = (((1,), (0,)), ((), ()))
NT = (((1,), (1,)), ((), ()))
TN = (((0,), (0,)), ((), ()))


def _params(*sem, side_effects=False):
    return pltpu.CompilerParams(dimension_semantics=sem if sem else None, vmem_limit_bytes=V7X_VMEM_LIMIT,
                                has_side_effects=side_effects)


def _call(body, args, *, name, grid, in_specs, out_specs, out_shape, sem, scratch_shapes=(), comm=None):
    out_specs, out_shape, scratch_shapes = list(out_specs), list(out_shape), list(scratch_shapes)
    if comm is None:
        outs = pl.pallas_call(body, name=name, grid=grid, in_specs=list(in_specs), out_specs=out_specs, out_shape=out_shape,
                              scratch_shapes=scratch_shapes, compiler_params=_params(*sem))(*args)
        return list(outs), []
    c_args, c_shapes, c_alias, n_remote, plan = comm
    n_in, n_out, n_ci, n_co, n_sc = len(args), len(out_shape), len(c_args), len(c_shapes), len(scratch_shapes)

    def wrapped(*refs):
        ins, cins = refs[:n_in], refs[n_in:n_in + n_ci]
        o0 = n_in + n_ci
        outs, couts = refs[o0:o0 + n_out], refs[o0 + n_out:o0 + n_out + n_co]
        s0 = o0 + n_out + n_co
        scr, (send_sems, recv_sems) = refs[s0:s0 + n_sc], refs[s0 + n_sc:]
        pids = [pl.program_id(a) for a in range(len(grid))]
        first = functools.reduce(jnp.logical_and, [q == 0 for q in pids])
        last = functools.reduce(jnp.logical_and, [q == g - 1 for q, g in zip(pids, grid)])
        me = (lax.axis_index("x"), lax.axis_index("y"), lax.axis_index("c"))

        def copies():
            _, sends, lands = plan(cins, couts)
            assert len(sends) == n_remote and len(lands) == n_remote
            out = [pltpu.make_async_remote_copy(src_ref=s, dst_ref=d, send_sem=send_sems.at[i], recv_sem=recv_sems.at[i],
                                                device_id=peer, device_id_type=MESH) for i, (s, d, peer) in enumerate(sends)]
            arrivals = [pltpu.make_async_remote_copy(src_ref=d, dst_ref=d, send_sem=send_sems.at[i], recv_sem=recv_sems.at[i],
                                                     device_id=me, device_id_type=MESH) for i, d in enumerate(lands)]
            return out, arrivals

        @pl.when(first)
        def _():
            for cp in copies()[0]:
                cp.start()

        body(*ins, *outs, *scr)

        @pl.when(last)
        def _():
            out, arrivals = copies()
            for cp in arrivals:
                cp.wait_recv()
            for cp in out:
                cp.wait_send()

    any_spec = pl.BlockSpec(memory_space=pl.ANY)
    res = pl.pallas_call(
        wrapped, name=name, grid=grid,
        in_specs=[*in_specs, *[any_spec] * n_ci], out_specs=[*out_specs, *[any_spec] * n_co],
        out_shape=[*out_shape, *[jax.ShapeDtypeStruct(s, d) for s, d in c_shapes]],
        input_output_aliases={n_in + i: n_out + j for i, j in c_alias.items()},
        scratch_shapes=[*scratch_shapes, pltpu.SemaphoreType.DMA((n_remote,)), pltpu.SemaphoreType.DMA((n_remote,))],
        compiler_params=_params(*["arbitrary"] * len(grid), side_effects=True),
    )(*args, *c_args)
    return list(res[:n_out]), list(res[n_out:])


def _pick(n, pref, mult=LANE):
    if n <= pref:
        return n
    best = None
    for t in range(mult, pref + 1, mult):
        if n % t == 0:
            best = t
    assert best is not None, (n, pref, mult)
    return best


def _sigmoid(x):
    return 1.0 / (1.0 + jnp.exp(-x))


def _mm(name, a, b, *, dims, grid, a_spec, b_spec, o_spec, out_shape, out_dtype, acc_shape, exact=False, comm=None):
    nk = grid[2]

    def body(a_ref, b_ref, o_ref, *scratch):
        if exact:
            part = lax.dot_general(a_ref[...], b_ref[...], dims, preferred_element_type=F32,
                                   precision=lax.Precision.HIGHEST)
        else:
            part = lax.dot_general(a_ref[...].astype(BF16), b_ref[...].astype(BF16), dims,
                                   preferred_element_type=F32)
        if nk == 1:
            o_ref[...] = part.astype(o_ref.dtype)
        else:
            acc = scratch[0]
            kk = pl.program_id(2)

            @pl.when(kk == 0)
            def _():
                acc[...] = part

            @pl.when(kk > 0)
            def _():
                acc[...] += part

            @pl.when(kk == nk - 1)
            def _():
                o_ref[...] = acc[...].astype(o_ref.dtype)

    outs, couts = _call(body, [a, b], name=name, grid=grid, in_specs=[a_spec, b_spec], out_specs=[o_spec],
                        out_shape=[jax.ShapeDtypeStruct(out_shape, out_dtype)],
                        scratch_shapes=[] if nk == 1 else [pltpu.VMEM(acc_shape, F32)],
                        sem=("parallel", "parallel", "arbitrary"), comm=comm)
    return outs[0] if comm is None else (outs[0], couts)


MM_VMEM_BUDGET = 36 << 20


def _fit(M, N, cost, m_mult=SUB):
    best = None
    for tm in sorted({_pick(M, p, m_mult) for p in (2048, 1024, 512, 256, 128)}):
        for tn in sorted({_pick(N, p) for p in (1408, 1024, 512, 256, 128)}):
            if best is None or (cost(tm, tn) <= MM_VMEM_BUDGET and tm * tn > best[0] * best[1]):
                best = (tm, tn)
    return best


def _sz(t):
    return jnp.dtype(t).itemsize


def mm_nn_pieces(name, a, w, p0, n_p, out_dtype, halves=1, comm=None):
    M, K = a.shape
    Nq = w.shape[2]
    tm, tn = _fit(M, Nq, lambda m, n: 2 * (m * K * _sz(a.dtype) + K * n * _sz(w.dtype) + m * n * _sz(out_dtype)))
    tpp = Nq // tn
    pph = n_p // halves
    if halves == 1:
        o_spec = pl.BlockSpec((tm, tn), lambda i, j, k: (i, j))
        oshape = (M, n_p * Nq)
    else:
        o_spec = pl.BlockSpec((None, tm, tn), lambda i, j, k: ((j // tpp) // pph, i, ((j // tpp) % pph) * tpp + j % tpp))
        oshape = (halves, M, pph * Nq)
    return _mm(name, a, w, dims=NN, grid=(M // tm, n_p * tpp, 1),
               a_spec=pl.BlockSpec((tm, K), lambda i, j, k: (i, 0)),
               b_spec=pl.BlockSpec((None, K, tn), lambda i, j, k: (p0 + j // tpp, 0, j % tpp)),
               o_spec=o_spec, out_shape=oshape, out_dtype=out_dtype, acc_shape=(tm, tn), comm=comm)


def mm_nn(name, a, w, out_dtype, exact=False, comm=None):
    M, K = a.shape
    N = w.shape[1]
    tm, tn = _fit(M, N, lambda m, n: 2 * (m * K * _sz(a.dtype) + K * n * _sz(w.dtype) + m * n * _sz(out_dtype)))
    return _mm(name, a, w, dims=NN, grid=(M // tm, N // tn, 1),
               a_spec=pl.BlockSpec((tm, K), lambda i, j, k: (i, 0)),
               b_spec=pl.BlockSpec((K, tn), lambda i, j, k: (0, j)),
               o_spec=pl.BlockSpec((tm, tn), lambda i, j, k: (i, j)),
               out_shape=(M, N), out_dtype=out_dtype, acc_shape=(tm, tn), exact=exact, comm=comm)


def mm_nt(name, dy, w, out_dtype, exact=False, comm=None):
    M, N = dy.shape
    K = w.shape[0]
    tm, tn = _fit(M, K, lambda m, n: 2 * (m * N * _sz(dy.dtype) + n * N * _sz(w.dtype) + m * n * _sz(out_dtype)))
    return _mm(name, dy, w, dims=NT, grid=(M // tm, K // tn, 1),
               a_spec=pl.BlockSpec((tm, N), lambda i, j, k: (i, 0)),
               b_spec=pl.BlockSpec((tn, N), lambda i, j, k: (j, 0)),
               o_spec=pl.BlockSpec((tm, tn), lambda i, j, k: (i, j)),
               out_shape=(M, K), out_dtype=out_dtype, acc_shape=(tm, tn), exact=exact, comm=comm)


def mm_nt_pieces(name, dy, w, out_dtype, halves=1, comm=None):
    P, K, Nq = w.shape
    M = dy.shape[-2]
    tm, tn = _fit(M, K, lambda m, n: 2 * (m * Nq * _sz(dy.dtype) + n * Nq * _sz(w.dtype) + m * n * _sz(out_dtype)) + 4 * m * n)
    pph = P // halves
    if halves == 1:
        a_spec = pl.BlockSpec((tm, Nq), lambda i, j, k: (i, k))
    else:
        a_spec = pl.BlockSpec((None, tm, Nq), lambda i, j, k: (k // pph, i, k % pph))
    return _mm(name, dy, w, dims=NT, grid=(M // tm, K // tn, P),
               a_spec=a_spec,
               b_spec=pl.BlockSpec((None, tn, Nq), lambda i, j, k: (k, j, 0)),
               o_spec=pl.BlockSpec((tm, tn), lambda i, j, k: (i, j)),
               out_shape=(M, K), out_dtype=out_dtype, acc_shape=(tm, tn), comm=comm)


def mm_nt_list(name, dys, w, out_dtype):
    P, K, Nq = w.shape
    M = dys[0].shape[0]
    assert len(dys) == P
    tm, tn = _fit(M, K, lambda m, n: 2 * (sum(m * Nq * _sz(d.dtype) for d in dys) + n * Nq * _sz(w.dtype)
                                          + m * n * _sz(out_dtype)) + 4 * m * n)

    def body(*refs):
        d_refs, w_ref, o_ref, acc = refs[:P], refs[P], refs[P + 1], refs[P + 2]
        kk = pl.program_id(2)
        for q in range(P):
            @pl.when(kk == q)
            def _(q=q):
                part = lax.dot_general(d_refs[q][...].astype(BF16), w_ref[...], NT, preferred_element_type=F32)
                acc[...] = part if q == 0 else acc[...] + part

        @pl.when(kk == P - 1)
        def _():
            o_ref[...] = acc[...].astype(o_ref.dtype)

    return pl.pallas_call(
        body, name=name, grid=(M // tm, K // tn, P),
        in_specs=[*[pl.BlockSpec((tm, Nq), lambda i, j, k: (i, 0)) for _ in range(P)],
                  pl.BlockSpec((None, tn, Nq), lambda i, j, k: (k, j, 0))],
        out_specs=pl.BlockSpec((tm, tn), lambda i, j, k: (i, j)),
        out_shape=jax.ShapeDtypeStruct((M, K), out_dtype),
        scratch_shapes=[pltpu.VMEM((tm, tn), F32)],
        compiler_params=_params("parallel", "arbitrary", "arbitrary"),
    )(*dys, w)


def mm_tn(name, a, dy, out_dtype, comm=None):
    M, K = a.shape
    N = dy.shape[1]
    tm, tn = _fit(K, N, lambda m, n: 2 * (M * m * _sz(a.dtype) + M * n * _sz(dy.dtype) + m * n * _sz(out_dtype)), LANE)
    return _mm(name, a, dy, dims=TN, grid=(K // tm, N // tn, 1),
               a_spec=pl.BlockSpec((M, tm), lambda i, j, k: (0, i)),
               b_spec=pl.BlockSpec((M, tn), lambda i, j, k: (0, j)),
               o_spec=pl.BlockSpec((tm, tn), lambda i, j, k: (i, j)),
               out_shape=(K, N), out_dtype=out_dtype, acc_shape=(tm, tn), comm=comm)


def mm_tn_pieces(name, a, dy, n_p, out_dtype, halves=1, comm=None):
    M, K = a.shape
    Nq = (dy.shape[-1] * halves) // n_p
    tm, tn = _fit(K, Nq, lambda m, n: 2 * (M * m * _sz(a.dtype) + M * n * _sz(dy.dtype) + m * n * _sz(out_dtype)), LANE)
    tpp = Nq // tn
    pph = n_p // halves
    if halves == 1:
        b_spec = pl.BlockSpec((M, tn), lambda i, j, k: (0, j))
    else:
        b_spec = pl.BlockSpec((None, M, tn), lambda i, j, k: ((j // tpp) // pph, 0, ((j // tpp) % pph) * tpp + j % tpp))
    return _mm(name, a, dy, dims=TN, grid=(K // tm, n_p * tpp, 1),
               a_spec=pl.BlockSpec((M, tm), lambda i, j, k: (0, i)),
               b_spec=b_spec,
               o_spec=pl.BlockSpec((None, tm, tn), lambda i, j, k: (j // tpp, i, j % tpp)),
               out_shape=(n_p, K, Nq), out_dtype=out_dtype, acc_shape=(tm, tn), comm=comm)


def _row_call(name, body, ins, in_kinds, outs, rows, tr, scratch=()):
    def spec(kind, shape):
        if isinstance(kind, pl.BlockSpec):
            return kind
        if kind == "row":
            return pl.BlockSpec((tr,) + tuple(shape[1:]), lambda i: (i,) + (0,) * (len(shape) - 1))
        return pl.BlockSpec(tuple(shape), lambda i: (0,) * len(shape))

    return pl.pallas_call(
        body, name=name, grid=(rows // tr,),
        in_specs=[spec(k, a.shape) for k, a in zip(in_kinds, ins)],
        out_specs=[spec(k, s) for k, s, _ in outs],
        out_shape=[jax.ShapeDtypeStruct(s, d) for _, s, d in outs],
        scratch_shapes=list(scratch),
        compiler_params=_params("arbitrary"),
    )(*ins)


def _acc(ref, val):
    @pl.when(pl.program_id(0) == 0)
    def _():
        ref[...] = val

    @pl.when(pl.program_id(0) > 0)
    def _():
        ref[...] += val


def norm_mod_fwd(name, x, w, b, tr=256):
    rows, d = x.shape
    tr = _pick(rows, tr, SUB)

    def body(x_ref, w_ref, b_ref, h_ref):
        xv = x_ref[...]
        r = lax.rsqrt(jnp.mean(xv * xv, axis=-1, keepdims=True) + EPS)
        h_ref[...] = (xv * r * w_ref[...] + b_ref[...]).astype(BF16)

    return _row_call(name, body, [x, w.reshape(1, d), b.reshape(1, d)], ["row", "vec", "vec"],
                     [("row", (rows, d), BF16)], rows, tr)[0]


def norm_mod_bwd(name, x, dh, w, dx_in, tr=256):
    rows, d = x.shape
    tr = _pick(rows, tr, SUB)

    def body(x_ref, dh_ref, w_ref, dxi_ref, dx_ref, cs1_ref, cs2_ref):
        xv = x_ref[...]
        r = lax.rsqrt(jnp.mean(xv * xv, axis=-1, keepdims=True) + EPS)
        xn = xv * r
        dhv = dh_ref[...].astype(F32)
        dxn = dhv * w_ref[...]
        dx_ref[...] = dxi_ref[...] + r * (dxn - xn * jnp.mean(dxn * xn, axis=-1, keepdims=True))
        _acc(cs1_ref, jnp.sum(dhv, axis=0, keepdims=True))
        _acc(cs2_ref, jnp.sum(dhv * xn, axis=0, keepdims=True))

    return _row_call(name, body, [x, dh, w.reshape(1, d), dx_in], ["row", "row", "vec", "row"],
                     [("row", (rows, d), F32), ("acc", (1, d), F32), ("acc", (1, d), F32)], rows, tr)


def gate_res_fwd(name, x, y, gate, tr=256):
    rows, d = x.shape
    tr = _pick(rows, tr, SUB)

    def body(x_ref, y_ref, g_ref, o_ref):
        o_ref[...] = x_ref[...] + g_ref[...] * y_ref[...].astype(F32)

    return _row_call(name, body, [x, y, gate.reshape(1, d)], ["row", "row", "vec"],
                     [("row", (rows, d), F32)], rows, tr)[0]


def res_norm_fwd(name, x, y, gate, w, b, tr=256):
    rows, d = x.shape
    tr = _pick(rows, tr, SUB)

    def body(x_ref, y_ref, g_ref, w_ref, b_ref, o_ref, h_ref):
        xv = x_ref[...] + g_ref[...] * y_ref[...].astype(F32)
        o_ref[...] = xv
        r = lax.rsqrt(jnp.mean(xv * xv, axis=-1, keepdims=True) + EPS)
        h_ref[...] = (xv * r * w_ref[...] + b_ref[...]).astype(BF16)

    return _row_call(name, body, [x, y, gate.reshape(1, d), w.reshape(1, d), b.reshape(1, d)],
                     ["row", "row", "vec", "vec", "vec"], [("row", (rows, d), F32), ("row", (rows, d), BF16)], rows, tr)


def gate_res_bwd(name, dx, y, gate, tr=256):
    rows, d = dx.shape
    tr = _pick(rows, tr, SUB)

    def body(dx_ref, y_ref, g_ref, dy_ref, dg_ref):
        dxv = dx_ref[...]
        dy_ref[...] = (g_ref[...] * dxv).astype(BF16)
        _acc(dg_ref, jnp.sum(dxv * y_ref[...].astype(F32), axis=0, keepdims=True))

    return _row_call(name, body, [dx, y, gate.reshape(1, d)], ["row", "row", "vec"],
                     [("row", (rows, d), BF16), ("acc", (1, d), F32)], rows, tr)


def loss_head(name, x, g, target, tr=256):
    rows, d = x.shape
    tr = _pick(rows, tr, SUB)

    def body(x_ref, g_ref, t_ref, dx_ref, dg_ref, loss_ref):
        xv = x_ref[...]
        r = lax.rsqrt(jnp.mean(xv * xv, axis=-1, keepdims=True) + EPS)
        xn = xv * r
        err = xn * g_ref[...] - t_ref[...]
        dy = err * (1.0 / d)
        dxn = dy * g_ref[...]
        dx_ref[...] = r * (dxn - xn * jnp.mean(dxn * xn, axis=-1, keepdims=True))
        _acc(dg_ref, jnp.sum(dy * xn, axis=0, keepdims=True))
        part = 0.5 * jnp.sum(jnp.sum(err * err, axis=-1, keepdims=True) * (1.0 / d), axis=0, keepdims=True)
        _acc(loss_ref, jnp.broadcast_to(part, (1, LANE)))

    return _row_call(name, body, [x, g.reshape(1, d), target], ["row", "vec", "row"],
                     [("row", (rows, d), F32), ("acc", (1, d), F32), ("acc", (1, LANE), F32)], rows, tr)


def fma3(name, a, b, c, dvec, out_dtype, tr=256):
    rows, d = a.shape
    tr = _pick(rows, tr, SUB)

    def body(a_ref, b_ref, c_ref, d_ref, o_ref):
        o_ref[...] = (d_ref[...] * a_ref[...] + b_ref[...] + c_ref[...]).astype(o_ref.dtype)

    return _row_call(name, body, [a, b, c, dvec.reshape(1, d)], ["row", "row", "row", "vec"],
                     [("row", (rows, d), out_dtype)], rows, tr)[0]


def sum_lead(name, a, out_dtype, tr=512):
    n, rows, cols = a.shape
    tr = _pick(rows, tr, 16)

    def body(a_ref, o_ref):
        acc = a_ref[0].astype(F32)
        for s in range(1, n):
            acc = acc + a_ref[s].astype(F32)
        o_ref[...] = acc.astype(o_ref.dtype)

    return pl.pallas_call(
        body, name=name, grid=(rows // tr,),
        in_specs=[pl.BlockSpec((n, tr, cols), lambda i: (0, i, 0))],
        out_specs=pl.BlockSpec((tr, cols), lambda i: (i, 0)),
        out_shape=jax.ShapeDtypeStruct((rows, cols), out_dtype),
        compiler_params=_params("parallel"),
    )(a)


def adamw(name, w, g, m, v, comm=None):
    shape = w.shape
    cols = shape[-1]
    w2, g2, m2, v2 = (t.reshape(-1, cols) for t in (w, g, m, v))
    rows = w2.shape[0]
    tr, tc = _pick(rows, 256, SUB), _pick(cols, 1536)
    c1 = 1.0 - ADAM_B1 ** ADAM_STEP
    c2 = 1.0 - ADAM_B2 ** ADAM_STEP

    def body(w_ref, g_ref, m_ref, v_ref, d_ref, mo_ref, vo_ref):
        gv = g_ref[...]
        mn = ADAM_B1 * m_ref[...] + (1.0 - ADAM_B1) * gv
        vn = ADAM_B2 * v_ref[...] + (1.0 - ADAM_B2) * (gv * gv)
        mo_ref[...] = mn
        vo_ref[...] = vn
        d_ref[...] = -ADAM_LR * ((mn / c1) / (jnp.sqrt(vn / c2) + ADAM_EPS) + ADAM_WD * w_ref[...])

    blk = pl.BlockSpec((tr, tc), lambda i, j: (i, j))
    outs, couts = _call(body, [w2, g2, m2, v2], name=name, grid=(rows // tr, cols // tc), in_specs=[blk] * 4,
                        out_specs=[blk] * 3, out_shape=[jax.ShapeDtypeStruct(w2.shape, F32)] * 3,
                        sem=("parallel", "parallel"), comm=comm)
    outs = tuple(o.reshape(shape) for o in outs)
    return outs if comm is None else (outs, couts)


def add_half(name, grad, got, c_idx, tr=256):
    Pn, R, C = grad.shape
    hr = R // 2
    tr = _pick(hr, tr, HALO)
    nb = hr // tr

    def body(c_ref, a_ref, b_ref, o_ref):
        o_ref[...] = (a_ref[...].astype(F32) + b_ref[...].astype(F32)).astype(o_ref.dtype)

    return pl.pallas_call(
        body, name=name,
        grid_spec=pltpu.PrefetchScalarGridSpec(
            num_scalar_prefetch=1, grid=(Pn, nb),
            in_specs=[pl.BlockSpec((None, tr, C), lambda q, i, c: (q, c[0] * nb + i, 0)),
                      pl.BlockSpec((None, tr, C), lambda q, i, c: (q, i, 0))],
            out_specs=pl.BlockSpec((None, tr, C), lambda q, i, c: (q, i, 0))),
        out_shape=jax.ShapeDtypeStruct((Pn, hr, C), BF16),
        compiler_params=_params("parallel", "parallel"),
    )(c_idx, grad, got)


def pair_sum_to_slot(name, buf, got, ids, tr=256):
    R, C = buf.shape
    hr = R // 2
    tr = _pick(hr, tr, SUB)
    nb = hr // tr

    def body(ids_ref, a_ref, b_ref, o_ref):
        o_ref[...] = a_ref[...] + b_ref[...]

    return pl.pallas_call(
        body, name=name,
        grid_spec=pltpu.PrefetchScalarGridSpec(
            num_scalar_prefetch=1, grid=(nb,),
            in_specs=[pl.BlockSpec((tr, C), lambda i, ids: (ids[1] * nb + i, 0)),
                      pl.BlockSpec((tr, C), lambda i, ids: (i, 0))],
            out_specs=pl.BlockSpec((None, tr, C), lambda i, ids: (ids[0], i, 0))),
        out_shape=jax.ShapeDtypeStruct((N_CHIP, hr, C), F32),
        compiler_params=_params("parallel"),
    )(ids, buf, got)


def sum_chips_to_half(name, slots, ids, tr=256):
    n, hr, C = slots.shape
    tr = _pick(hr, tr, SUB)

    def body(ids_ref, s_ref, o_ref):
        acc = s_ref[0]
        for q in range(1, n):
            acc = acc + s_ref[q]
        o_ref[...] = acc

    return pl.pallas_call(
        body, name=name,
        grid_spec=pltpu.PrefetchScalarGridSpec(
            num_scalar_prefetch=1, grid=(hr // tr,),
            in_specs=[pl.BlockSpec((n, tr, C), lambda i, ids: (0, i, 0))],
            out_specs=pl.BlockSpec((None, tr, C), lambda i, ids: (ids[1], i, 0))),
        out_shape=jax.ShapeDtypeStruct((2, hr, C), F32),
        compiler_params=_params("parallel"),
    )(ids, slots)


def sum_slots(name, slots, mine, ids, tr=256):
    Pn, hr, C = slots.shape
    tr = _pick(hr, tr, HALO)

    def body(ids_ref, m_ref, s1_ref, s2_ref, s3_ref, o_ref):
        o_ref[...] = (m_ref[...].astype(F32) + s1_ref[...].astype(F32)) + (s2_ref[...].astype(F32) + s3_ref[...].astype(F32))

    def other(k):
        return pl.BlockSpec((None, tr, C), lambda i, ids: ((ids[0] + k) % Pn, i, 0))

    return pl.pallas_call(
        body, name=name,
        grid_spec=pltpu.PrefetchScalarGridSpec(
            num_scalar_prefetch=1, grid=(hr // tr,),
            in_specs=[pl.BlockSpec((None, tr, C), lambda i, ids: (ids[0], i, 0)), other(1), other(2), other(3)],
            out_specs=pl.BlockSpec((None, tr, C), lambda i, ids: (ids[1], i, 0))),
        out_shape=jax.ShapeDtypeStruct((2, hr, C), F32),
        compiler_params=_params("parallel"),
    )(ids, mine, slots, slots, slots)


HALO = 16


def _halo_specs(lead, R, tn, n_rows, col_of):
    nb, nblk = R // HALO, n_rows // HALO

    def mk(rows, row_of):
        return pl.BlockSpec((lead, rows, tn), lambda *g: (0, row_of(g[-1]), col_of(g)))

    return (mk(HALO, lambda i: jnp.maximum(i * nb - 1, 0)), mk(R, lambda i: i),
            mk(HALO, lambda i: jnp.minimum((i + 1) * nb, nblk - 1)))


def _fill_halo(dst, i, last, R, prev, cur, nxt):
    nd = len(dst.shape)
    lead = (slice(None),) * (nd - 2)
    dst[lead + (slice(0, HALO), slice(None))] = jnp.where(i == 0, 0.0, prev)
    dst[lead + (slice(HALO, HALO + R), slice(None))] = cur
    dst[lead + (slice(HALO + R, HALO + R + HALO), slice(None))] = jnp.where(i == last, 0.0, nxt)


def _shift_mats(n):
    i = np.arange(n)
    return jnp.asarray(np.stack([i[:, None] - 1 == i[None, :], i[:, None] + 1 == i[None, :]]), BF16)


def _shifted(s_ref, xb):
    return (jnp.dot(s_ref[0], xb, preferred_element_type=F32), jnp.dot(s_ref[1], xb, preferred_element_type=F32))


def ffn_mid_fwd(name, up3, cw, cb, R=256, tn=512, comm=None):
    _, L, Fd = up3.shape
    R, tn = _pick(L, R, HALO), _pick(Fd, tn)
    nrow = L // R

    def body(p_ref, c_ref, n_ref, w_ref, b_ref, s_ref, act_ref):
        i = pl.program_id(1)
        row = lax.broadcasted_iota(jnp.int32, (R, tn), 0)
        cv = []
        for z in range(2):
            xb = c_ref[z]
            before = jnp.where(i == 0, 0.0, p_ref[z].astype(F32)[HALO - 1:HALO])
            after = jnp.where(i == nrow - 1, 0.0, n_ref[z].astype(F32)[0:1])
            dn, up = _shifted(s_ref, xb)
            dn = jnp.where(row == 0, before, dn)
            up = jnp.where(row == R - 1, after, up)
            cv.append(b_ref[z] + w_ref[z, 0:1, :] * dn + w_ref[z, 1:2, :] * xb.astype(F32) + w_ref[z, 2:3, :] * up)
        u, g = cv
        act_ref[...] = (u * g * _sigmoid(g)).astype(BF16)

    hs = _halo_specs(2, R, tn, L, lambda g: g[0])
    outs, couts = _call(
        body, [up3, up3, up3, cw, cb, _shift_mats(R)], name=name, grid=(Fd // tn, nrow),
        in_specs=[*hs, pl.BlockSpec((2, 3, tn), lambda j, i: (0, 0, j)), pl.BlockSpec((2, 1, tn), lambda j, i: (0, 0, j)),
                  pl.BlockSpec((2, R, R), lambda j, i: (0, 0, 0))],
        out_specs=[pl.BlockSpec((R, tn), lambda j, i: (i, j))],
        out_shape=[jax.ShapeDtypeStruct((L, Fd), BF16)], sem=("parallel", "arbitrary"), comm=comm)
    return outs[0] if comm is None else (outs[0], couts)


def ffn_mid_bwd(name, up3, dact, cw, cb, R=256, tn=512, comm=None):
    _, L, Fd = up3.shape
    R, tn = _pick(L, R, HALO), _pick(Fd, tn)
    nrow = L // R

    def gate_grads(u, g, d):
        sg = _sigmoid(g)
        return d * g * sg, d * u * sg * (1.0 + g * (1.0 - sg))

    def body(pu, cu, nu, pd, cd, nd, w_ref, b_ref, s_ref, dup_ref, dcw_ref, dcb_ref):
        i = pl.program_id(1)
        first, last = i == 0, i == nrow - 1
        row = lax.broadcasted_iota(jnp.int32, (R, tn), 0)
        cv, cv_b, cv_a, taps = [], [], [], []
        for z in range(2):
            xb = cu[z]
            xf = xb.astype(F32)
            pf = jnp.where(first, 0.0, pu[z].astype(F32))
            nf = jnp.where(last, 0.0, nu[z].astype(F32))
            xm2, xm1, xp0, xp1 = pf[HALO - 2:HALO - 1], pf[HALO - 1:HALO], nf[0:1], nf[1:2]
            dn, up = _shifted(s_ref, xb)
            dn = jnp.where(row == 0, xm1, dn)
            up = jnp.where(row == R - 1, xp0, up)
            w0, w1, w2, b = w_ref[z, 0:1, :], w_ref[z, 1:2, :], w_ref[z, 2:3, :], b_ref[z]
            cv.append(b + w0 * dn + w1 * xf + w2 * up)
            cv_b.append(b + w0 * xm2 + w1 * xm1 + w2 * xf[0:1])
            cv_a.append(b + w0 * xf[R - 1:R] + w1 * xp0 + w2 * xp1)
            taps.append((dn, xf, up))
        dcs = gate_grads(cv[0], cv[1], cd[0].astype(F32))
        dcs_b = gate_grads(cv_b[0], cv_b[1], jnp.where(first, 0.0, pd[0].astype(F32)[HALO - 1:HALO]))
        dcs_a = gate_grads(cv_a[0], cv_a[1], jnp.where(last, 0.0, nd[0].astype(F32)[0:1]))

        @pl.when(first)
        def _():
            dcw_ref[...] = jnp.zeros_like(dcw_ref)
            dcb_ref[...] = jnp.zeros_like(dcb_ref)

        for z in range(2):
            dc = dcs[z]
            dc_dn, dc_up = _shifted(s_ref, dc.astype(BF16))
            dc_dn = jnp.where(row == 0, dcs_b[z], dc_dn)
            dc_up = jnp.where(row == R - 1, dcs_a[z], dc_up)
            dup_ref[z] = (w_ref[z, 0:1, :] * dc_up + w_ref[z, 1:2, :] * dc + w_ref[z, 2:3, :] * dc_dn).astype(BF16)
            dcb_ref[z] += jnp.sum(dc, axis=0, keepdims=True)
            for k in range(3):
                dcw_ref[z, k:k + 1, :] += jnp.sum(dc * taps[z][k], axis=0, keepdims=True)

    hu = _halo_specs(2, R, tn, L, lambda g: g[0])
    hd = _halo_specs(1, R, tn, L, lambda g: g[0])
    outs, couts = _call(
        body, [up3, up3, up3, dact[None], dact[None], dact[None], cw, cb, _shift_mats(R)], name=name, grid=(Fd // tn, nrow),
        in_specs=[*hu, *hd, pl.BlockSpec((2, 3, tn), lambda j, i: (0, 0, j)), pl.BlockSpec((2, 1, tn), lambda j, i: (0, 0, j)),
                  pl.BlockSpec((2, R, R), lambda j, i: (0, 0, 0))],
        out_specs=[pl.BlockSpec((2, R, tn), lambda j, i: (0, i, j)), pl.BlockSpec((2, 3, tn), lambda j, i: (0, 0, j)),
                   pl.BlockSpec((2, 1, tn), lambda j, i: (0, 0, j))],
        out_shape=[jax.ShapeDtypeStruct((2, L, Fd), BF16), jax.ShapeDtypeStruct((2, 3, Fd), F32),
                   jax.ShapeDtypeStruct((2, 1, Fd), F32)],
        sem=("parallel", "arbitrary"), comm=comm)
    return outs if comm is None else (outs, couts)


def _glu_z0(blk):
    return blk[0].astype(F32) * _sigmoid(blk[1].astype(F32))


def _sublane_copies(ref, cs):
    n = ref.shape[1]
    blk = ref[0, :, cs]
    for b in range(1, SUB):
        ref[b, :, cs] = pltpu.roll(blk, n - b, 0)


def _tap(ref, offset, rows, cs):
    return ref[offset % SUB, pl.ds(offset - offset % SUB, rows), cs]


def conf_mid_fwd(name, ag3, dw_w, dw_b, ln_g, ln_b, R=128, cb=256):
    _, L, C = ag3.shape
    K = dw_w.shape[0]
    pad = (K - 1) // 2
    assert pad <= HALO
    R, cb = _pick(L, R, HALO), _pick(C, cb)
    nrow = L // R

    def body(p_ref, c_ref, n_ref, w_ref, b_ref, g_ref, bb_ref, z1_ref, z3_ref, s_ref):
        i = pl.program_id(0)
        _fill_halo(s_ref.at[0], i, nrow - 1, R, _glu_z0(p_ref), _glu_z0(c_ref), _glu_z0(n_ref))
        for c0 in range(0, C, cb):
            cs = slice(c0, c0 + cb)
            _sublane_copies(s_ref, cs)
            acc = jnp.broadcast_to(b_ref[:, cs], (R, cb))
            for k in range(K):
                acc = acc + w_ref[k:k + 1, cs] * _tap(s_ref, HALO - pad + k, R, cs)
            z1_ref[:, cs] = acc
        z1 = z1_ref[...]
        zc = z1 - jnp.mean(z1, axis=-1, keepdims=True)
        zn = zc * lax.rsqrt(jnp.mean(zc * zc, axis=-1, keepdims=True) + EPS)
        z2 = zn * g_ref[...] + bb_ref[...]
        z3_ref[...] = (z2 * _sigmoid(z2)).astype(BF16)

    hs = _halo_specs(2, R, C, L, lambda g: 0)
    vec = pl.BlockSpec((1, C), lambda i: (0, 0))
    return pl.pallas_call(
        body, name=name, grid=(nrow,),
        in_specs=[*hs, pl.BlockSpec((K, C), lambda i: (0, 0)), vec, vec, vec],
        out_specs=[pl.BlockSpec((R, C), lambda i: (i, 0)), pl.BlockSpec((R, C), lambda i: (i, 0))],
        out_shape=[jax.ShapeDtypeStruct((L, C), F32), jax.ShapeDtypeStruct((L, C), BF16)],
        scratch_shapes=[pltpu.VMEM((SUB, R + 2 * HALO, C), F32)],
        compiler_params=_params("parallel"),
    )(ag3, ag3, ag3, dw_w, dw_b.reshape(1, C), ln_g.reshape(1, C), ln_b.reshape(1, C))


def conf_ln_bwd(name, z1, dz3, ln_g, ln_b, tr=256):
    rows, C = z1.shape
    tr = _pick(rows, tr, HALO)

    def body(z_ref, d_ref, g_ref, b_ref, dz_ref, dg_ref, db_ref):
        z1v = z_ref[...]
        zc = z1v - jnp.mean(z1v, axis=-1, keepdims=True)
        rs = lax.rsqrt(jnp.mean(zc * zc, axis=-1, keepdims=True) + EPS)
        zn = zc * rs
        z2 = zn * g_ref[...] + b_ref[...]
        sg = _sigmoid(z2)
        dz2 = d_ref[...].astype(F32) * sg * (1.0 + z2 * (1.0 - sg))
        _acc(dg_ref, jnp.sum(dz2 * zn, axis=0, keepdims=True))
        _acc(db_ref, jnp.sum(dz2, axis=0, keepdims=True))
        dzn = dz2 * g_ref[...]
        dz1 = rs * (dzn - jnp.mean(dzn, axis=-1, keepdims=True) - zn * jnp.mean(dzn * zn, axis=-1, keepdims=True))
        dz_ref[...] = dz1.astype(BF16)

    return _row_call(name, body, [z1, dz3, ln_g.reshape(1, C), ln_b.reshape(1, C)], ["row", "row", "vec", "vec"],
                     [("row", (rows, C), BF16), ("acc", (1, C), F32), ("acc", (1, C), F32)], rows, tr)


def conf_conv_bwd(name, ag3, dz1, dw_w, R=128, cb=256, comm=None):
    _, L, C = ag3.shape
    K = dw_w.shape[0]
    pad = (K - 1) // 2
    R, cb = _pick(L, R, HALO), _pick(C, cb)
    nrow = L // R

    def body(pa, ca, na, pd, cd, nd, w_ref, dag_ref, dw_ref, db_ref, s_ref, d_ref, z_ref):
        i = pl.program_id(0)
        _fill_halo(s_ref.at[0], i, nrow - 1, R, _glu_z0(pa), _glu_z0(ca), _glu_z0(na))
        _fill_halo(d_ref.at[0], i, nrow - 1, R, pd[0].astype(F32), cd[0].astype(F32), nd[0].astype(F32))

        @pl.when(i == 0)
        def _():
            dw_ref[...] = jnp.zeros_like(dw_ref)
            db_ref[...] = jnp.zeros_like(db_ref)

        for c0 in range(0, C, cb):
            cs = slice(c0, c0 + cb)
            _sublane_copies(s_ref, cs)
            _sublane_copies(d_ref, cs)
            dcur = d_ref[0, pl.ds(HALO, R), cs]
            acc = jnp.zeros((R, cb), F32)
            for k in range(K):
                acc = acc + w_ref[k:k + 1, cs] * _tap(d_ref, HALO + pad - k, R, cs)
                dw_ref[k:k + 1, cs] += jnp.sum(dcur * _tap(s_ref, HALO - pad + k, R, cs), axis=0, keepdims=True)
            z_ref[:, cs] = acc
            db_ref[:, cs] += jnp.sum(dcur, axis=0, keepdims=True)
        dz0 = z_ref[...]
        a = ca[0].astype(F32)
        sg = _sigmoid(ca[1].astype(F32))
        dag_ref[0] = (dz0 * sg).astype(BF16)
        dag_ref[1] = (dz0 * a * sg * (1.0 - sg)).astype(BF16)

    ha = _halo_specs(2, R, C, L, lambda g: 0)
    hd = _halo_specs(1, R, C, L, lambda g: 0)
    outs, couts = _call(
        body, [ag3, ag3, ag3, dz1[None], dz1[None], dz1[None], dw_w], name=name, grid=(nrow,),
        in_specs=[*ha, *hd, pl.BlockSpec((K, C), lambda i: (0, 0))],
        out_specs=[pl.BlockSpec((2, R, C), lambda i: (0, i, 0)), pl.BlockSpec((K, C), lambda i: (0, 0)),
                   pl.BlockSpec((1, C), lambda i: (0, 0))],
        out_shape=[jax.ShapeDtypeStruct((2, L, C), BF16), jax.ShapeDtypeStruct((K, C), F32),
                   jax.ShapeDtypeStruct((1, C), F32)],
        scratch_shapes=[pltpu.VMEM((SUB, R + 2 * HALO, C), F32), pltpu.VMEM((SUB, R + 2 * HALO, C), F32),
                        pltpu.VMEM((R, C), F32)],
        sem=("arbitrary",), comm=comm)
    return outs if comm is None else (outs, couts)


_GELU_C = math.sqrt(2.0 / math.pi)


def _gelu(x):
    return 0.5 * x * (1.0 + jnp.tanh(_GELU_C * (x + 0.044715 * x * x * x)))


def _gelu_grad(x):
    t = jnp.tanh(_GELU_C * (x + 0.044715 * x * x * x))
    return 0.5 * (1.0 + t) + 0.5 * x * (1.0 - t * t) * _GELU_C * (1.0 + 3.0 * 0.044715 * x * x)


def glu_fwd(name, u, y0, y1, d, wg, tr=512):
    rows, W = u.shape
    tr = _pick(rows, tr, HALO)

    def body(u_ref, y0_ref, y1_ref, d_ref, w_ref, o_ref):
        z = _gelu(d_ref[...] * u_ref[...] + y0_ref[...] + y1_ref[...])
        zz = jnp.dot(z.astype(BF16), w_ref[...], preferred_element_type=F32)
        o_ref[...] = (z * _sigmoid(zz)).astype(BF16)

    return _row_call(name, body, [u, y0, y1, d.reshape(1, W), wg], ["row", "row", "row", "vec", "vec"],
                     [("row", (rows, W), BF16)], rows, tr)[0]


def glu_bwd(name, u, y0, y1, d, wg, dmix, tr=512):
    rows, W = u.shape
    tr = _pick(rows, tr, HALO)

    def body(u_ref, y0_ref, y1_ref, d_ref, w_ref, do_ref, dy_ref, z_ref, dzz_ref, dd_ref):
        uv = u_ref[...]
        y = d_ref[...] * uv + y0_ref[...] + y1_ref[...]
        z = _gelu(y)
        zz = jnp.dot(z.astype(BF16), w_ref[...], preferred_element_type=F32)
        sg = _sigmoid(zz)
        do = do_ref[...].astype(F32)
        dzz = (do * z * sg * (1.0 - sg)).astype(BF16)
        dz = do * sg + lax.dot_general(dzz, w_ref[...], NT, preferred_element_type=F32)
        dy = dz * _gelu_grad(y)
        dy_ref[...] = dy
        z_ref[...] = z.astype(BF16)
        dzz_ref[...] = dzz
        _acc(dd_ref, jnp.sum(dy * uv, axis=0, keepdims=True))

    do_spec = pl.BlockSpec((tr, W), lambda i: (i, 0))
    return _row_call(name, body, [u, y0, y1, d.reshape(1, W), wg, dmix], ["row", "row", "row", "vec", "vec", do_spec],
                     [("row", (rows, W), F32), ("row", (rows, W), BF16), ("row", (rows, W), BF16), ("acc", (1, W), F32)],
                     rows, tr)


NA_KEYS = NA_WIN_R * GRID_W


NA_PAIRS = NA_WIN_R // 2


def na_bias(rpb):
    H, nr, nc = rpb.shape
    e, ok = _na_col_select()
    rp = jnp.pad(rpb.reshape(H * nr, nc), ((0, (-H * nr) % SUB), (0, LANE - nc)))
    cols = mm_nn("na_bias_mm", rp, jnp.asarray(e, F32), F32, exact=True)[:H * nr]
    tiles = (cols + jnp.asarray(np.where(ok, 0.0, NEG), F32)).reshape(H, nr, GRID_W, GRID_W)
    return jnp.concatenate([tiles[:, :-1], tiles[:, 1:]], axis=-1)


def na_bias_grad(db2):
    H, n2 = db2.shape[:2]
    left, right = db2[..., :GRID_W], db2[..., GRID_W:]
    tiles = jnp.pad(left, ((0, 0), (0, 1), (0, 0), (0, 0))) + jnp.pad(right, ((0, 0), (1, 0), (0, 0), (0, 0)))
    flat = tiles.reshape(H * (n2 + 1), GRID_W * GRID_W)
    flat = jnp.pad(flat, ((0, (-flat.shape[0]) % SUB), (0, 0)))
    dcol = mm_nt("na_bias_fold", flat, na_bias_fold_matrix(), F32, exact=True)
    return dcol[:H * (n2 + 1), :2 * NA_WIN_C - 1].reshape(H, n2 + 1, 2 * NA_WIN_C - 1)


def _na_col_select():
    q = np.arange(GRID_W)
    cs = np.clip(q - NA_WIN_C // 2, 0, GRID_W - NA_WIN_C)
    ok = ((q[None, :] >= cs[:, None]) & (q[None, :] < cs[:, None] + NA_WIN_C)).reshape(-1)
    cidx = np.clip(q[None, :] - q[:, None] + (NA_WIN_C - 1), 0, 2 * NA_WIN_C - 2).reshape(-1)
    return (cidx[None, :] == np.arange(LANE)[:, None]) & ok[None, :], ok


def na_bias_fold_matrix():
    return jnp.asarray(_na_col_select()[0], F32)


def _na_window(r, rows):
    kr0 = jnp.clip(r - NA_WIN_R // 2, 0, rows - NA_WIN_R)
    return pl.multiple_of(kr0 * GRID_W, GRID_W), r - kr0


def _na_dims(qkv, kvc):
    L = qkv.shape[0]
    NA = qkv.shape[1] // 3
    H = NA // NA_HEAD_DIM
    hp = 2 if H % 2 == 0 else 1
    return L, NA, H, hp, H // hp, L // GRID_W, kvc.shape[0]


def _na_bias_tile(b_ref, hh, off):
    return jnp.concatenate([b_ref[hh, NA_WIN_R - 1 - off + 2 * j] for j in range(NA_PAIRS)], axis=-1)


def natten_fwd(name, qkv, kvc, bias, comm=None):
    L, NA, H, hp, G, rows, Lc = _na_dims(qkv, kvc)
    scale = NA_HEAD_DIM ** -0.5
    wd = hp * NA_HEAD_DIM

    def body(q_ref, k_ref, v_ref, kc_ref, vc_ref, b_ref, o_ref, lse_ref):
        st, off = _na_window(pl.program_id(1), rows)
        for hh in range(hp):
            sl = slice(hh * NA_HEAD_DIM, (hh + 1) * NA_HEAD_DIM)
            q = q_ref[:, sl]
            s_loc = (lax.dot_general(q, k_ref[pl.ds(st, NA_KEYS), sl], NT, preferred_element_type=F32) * scale
                     + _na_bias_tile(b_ref, hh, off))
            s_ctx = lax.dot_general(q, kc_ref[:, sl], NT, preferred_element_type=F32) * scale
            m = jnp.maximum(jnp.max(s_loc, axis=-1, keepdims=True), jnp.max(s_ctx, axis=-1, keepdims=True))
            p_loc, p_ctx = jnp.exp(s_loc - m), jnp.exp(s_ctx - m)
            l = jnp.sum(p_loc, axis=-1, keepdims=True) + jnp.sum(p_ctx, axis=-1, keepdims=True)
            o = (jnp.dot(p_loc.astype(BF16), v_ref[pl.ds(st, NA_KEYS), sl], preferred_element_type=F32)
                 + jnp.dot(p_ctx.astype(BF16), vc_ref[:, sl], preferred_element_type=F32))
            o_ref[:, sl] = (o / l).astype(BF16)
            lse_ref[hh] = m + jnp.log(l)

    outs, couts = _call(
        body, [qkv, qkv, qkv, kvc, kvc, bias], name=name, grid=(G, rows),
        in_specs=[pl.BlockSpec((GRID_W, wd), lambda h, r: (r, h)),
                  pl.BlockSpec((L, wd), lambda h, r: (0, G + h)),
                  pl.BlockSpec((L, wd), lambda h, r: (0, 2 * G + h)),
                  pl.BlockSpec((Lc, wd), lambda h, r: (0, h)),
                  pl.BlockSpec((Lc, wd), lambda h, r: (0, G + h)),
                  pl.BlockSpec((hp,) + bias.shape[1:], lambda h, r: (h, 0, 0, 0))],
        out_specs=[pl.BlockSpec((GRID_W, wd), lambda h, r: (r, h)),
                   pl.BlockSpec((hp, GRID_W, 1), lambda h, r: (h, r, 0))],
        out_shape=[jax.ShapeDtypeStruct((L, NA), BF16), jax.ShapeDtypeStruct((H, L, 1), F32)],
        sem=("parallel", "arbitrary"), comm=comm)
    return outs if comm is None else (outs, couts)


def natten_bwd(name, qkv, kvc, bias, o, lse, dmix, comm=None):
    L, NA, H, hp, G, rows, Lc = _na_dims(qkv, kvc)
    scale = NA_HEAD_DIM ** -0.5
    wd = hp * NA_HEAD_DIM

    def body(q_ref, k_ref, v_ref, kc_ref, vc_ref, b_ref, o_ref, lse_ref, do_ref,
             dq_ref, dk_ref, dv_ref, dkc_ref, dvc_ref, db_ref):
        r = pl.program_id(1)
        st, off = _na_window(r, rows)

        @pl.when(r == 0)
        def _():
            for ref in (dk_ref, dv_ref, dkc_ref, dvc_ref, db_ref):
                ref[...] = jnp.zeros_like(ref)

        for hh in range(hp):
            sl = slice(hh * NA_HEAD_DIM, (hh + 1) * NA_HEAD_DIM)
            q, kl, vl, kc, vc = q_ref[:, sl], k_ref[pl.ds(st, NA_KEYS), sl], v_ref[pl.ds(st, NA_KEYS), sl], kc_ref[:, sl], vc_ref[:, sl]
            do = do_ref[:, sl]
            lse_v = lse_ref[hh]
            p_loc = jnp.exp(lax.dot_general(q, kl, NT, preferred_element_type=F32) * scale + _na_bias_tile(b_ref, hh, off) - lse_v)
            p_ctx = jnp.exp(lax.dot_general(q, kc, NT, preferred_element_type=F32) * scale - lse_v)
            delta = jnp.sum(do.astype(F32) * o_ref[:, sl].astype(F32), axis=-1, keepdims=True)
            ds_loc = p_loc * (lax.dot_general(do, vl, NT, preferred_element_type=F32) - delta)
            ds_ctx = p_ctx * (lax.dot_general(do, vc, NT, preferred_element_type=F32) - delta)
            dsl, dsc = ds_loc.astype(BF16), ds_ctx.astype(BF16)
            dq = jnp.dot(dsl, kl, preferred_element_type=F32) + jnp.dot(dsc, kc, preferred_element_type=F32)
            dq_ref[:, sl] = (dq * scale).astype(BF16)
            dk_ref[pl.ds(st, NA_KEYS), sl] += lax.dot_general(dsl, q, TN, preferred_element_type=F32) * scale
            dv_ref[pl.ds(st, NA_KEYS), sl] += lax.dot_general(p_loc.astype(BF16), do, TN, preferred_element_type=F32)
            dkc_ref[:, sl] += lax.dot_general(dsc, q, TN, preferred_element_type=F32) * scale
            dvc_ref[:, sl] += lax.dot_general(p_ctx.astype(BF16), do, TN, preferred_element_type=F32)
            for j in range(NA_PAIRS):
                db_ref[hh, NA_WIN_R - 1 - off + 2 * j] += ds_loc[:, 2 * j * GRID_W:(2 * j + 2) * GRID_W]

    tok = pl.BlockSpec((GRID_W, wd), lambda h, r: (r, h))
    bia = pl.BlockSpec((hp,) + bias.shape[1:], lambda h, r: (h, 0, 0, 0))
    outs, couts = _call(
        body, [qkv, qkv, qkv, kvc, kvc, bias, o, lse, dmix], name=name, grid=(G, rows),
        in_specs=[tok,
                  pl.BlockSpec((L, wd), lambda h, r: (0, G + h)),
                  pl.BlockSpec((L, wd), lambda h, r: (0, 2 * G + h)),
                  pl.BlockSpec((Lc, wd), lambda h, r: (0, h)),
                  pl.BlockSpec((Lc, wd), lambda h, r: (0, G + h)),
                  bia,
                  tok,
                  pl.BlockSpec((hp, GRID_W, 1), lambda h, r: (h, r, 0)),
                  pl.BlockSpec((GRID_W, wd), lambda h, r: (r, G + h))],
        out_specs=[tok,
                   pl.BlockSpec((L, wd), lambda h, r: (0, h)),
                   pl.BlockSpec((L, wd), lambda h, r: (0, h)),
                   pl.BlockSpec((Lc, wd), lambda h, r: (0, h)),
                   pl.BlockSpec((Lc, wd), lambda h, r: (0, h)),
                   bia],
        out_shape=[jax.ShapeDtypeStruct((L, NA), BF16), jax.ShapeDtypeStruct((L, NA), F32), jax.ShapeDtypeStruct((L, NA), F32),
                   jax.ShapeDtypeStruct((Lc, NA), F32), jax.ShapeDtypeStruct((Lc, NA), F32),
                   jax.ShapeDtypeStruct(bias.shape, F32)],
        sem=("parallel", "arbitrary"), comm=comm)
    return outs if comm is None else (outs, couts)


def _s5_dims(T, N):
    TC = T // S5_SEG
    assert T % (S5_SEG * SUB * 2) == 0 and N % S5_STRIP == 0
    return TC, TC // SUB, S5_SEG, N // S5_STRIP


def _s5_backward(d, rev):
    return (d == 1) != rev


def s5_scan(name, xin, mats, a, rev, comm=None):
    _, T, W = xin.shape
    N = a.shape[-1]
    TC, NG, NCH, NS = _s5_dims(T, N)
    CW, SL = W // NS, S5_STRIP

    def ck(d, k):
        return jnp.where(_s5_backward(d, rev), NCH - 1 - k, k)

    def body(x_ref, m_ref, a_ref, h_ref, f_ref, carry, hs):
        @pl.when(pl.program_id(2) == 0)
        def _():
            carry[...] = jnp.zeros_like(carry)

        xb = x_ref[...].astype(BF16)
        hs[0] = jnp.dot(xb, m_ref[0], preferred_element_type=F32)
        hs[1] = jnp.dot(xb, m_ref[1], preferred_element_type=F32)
        ar, ai = jnp.broadcast_to(a_ref[0], (SUB, SL)), jnp.broadcast_to(a_ref[1], (SUB, SL))
        bw = _s5_backward(pl.program_id(0), rev)

        def step(t, c):
            hr, hi = c
            row = pl.multiple_of(jnp.where(bw, NG - 1 - t, t) * SUB, SUB)
            nr = ar * hr - ai * hi + hs[0, pl.ds(row, SUB), :]
            ni = ar * hi + ai * hr + hs[1, pl.ds(row, SUB), :]
            hs[0, pl.ds(row, SUB), :] = nr
            hs[1, pl.ds(row, SUB), :] = ni
            return nr, ni

        hr, hi = lax.fori_loop(0, NG, step, (carry[0], carry[1]))
        carry[0], carry[1] = hr, hi
        f_ref[0], f_ref[1] = hr, hi
        h_ref[...] = hs[...].astype(BF16)

    outs, couts = _call(
        body, [xin, mats, a], name=name, grid=(2, NS, NCH),
        in_specs=[pl.BlockSpec((None, TC, CW), lambda d, j, k: (d, ck(d, k), j)),
                  pl.BlockSpec((None, 2, None, CW, SL), lambda d, j, k: (d, 0, j, 0, 0)),
                  pl.BlockSpec((None, 2, 1, SL), lambda d, j, k: (d, 0, 0, j))],
        out_specs=[pl.BlockSpec((None, 2, TC, SL), lambda d, j, k: (d, 0, ck(d, k), j)),
                   pl.BlockSpec((None, 2, SUB, SL), lambda d, j, k: (d, 0, 0, j))],
        out_shape=[jax.ShapeDtypeStruct((2, 2, T, N), BF16), jax.ShapeDtypeStruct((2, 2, SUB, N), F32)],
        scratch_shapes=[pltpu.VMEM((2, SUB, SL), F32), pltpu.VMEM((2, TC, SL), F32)],
        sem=("parallel", "parallel", "arbitrary"), comm=comm)
    return outs if comm is None else (outs, couts)


def s5_fix(name, hloc, hin, a, mats, rev, comm=None):
    _, _, T, N = hloc.shape
    TC, NG, NCH, NS = _s5_dims(T, N)
    SL = S5_STRIP
    CW = mats.shape[-1]

    def ck(d, k):
        return jnp.where(_s5_backward(d, rev), NCH - 1 - k, k)

    def body(h_ref, hin_ref, a_ref, m_ref, ho_ref, y_ref, g, hs):
        @pl.when(pl.program_id(2) == 0)
        def _():
            g[...] = hin_ref[...]

        hs[...] = h_ref[...].astype(F32)
        ar, ai = jnp.broadcast_to(a_ref[0], (SUB, SL)), jnp.broadcast_to(a_ref[1], (SUB, SL))
        bw = _s5_backward(pl.program_id(0), rev)

        def step(t, c):
            gr, gi = c
            row = pl.multiple_of(jnp.where(bw, NG - 1 - t, t) * SUB, SUB)
            nr = ar * gr - ai * gi
            ni = ar * gi + ai * gr
            hs[0, pl.ds(row, SUB), :] += nr
            hs[1, pl.ds(row, SUB), :] += ni
            return nr, ni

        gr, gi = lax.fori_loop(0, NG, step, (g[0], g[1]))
        g[0], g[1] = gr, gi
        hb = hs[...].astype(BF16)
        ho_ref[...] = hb
        y_ref[...] = (jnp.dot(hb[0], m_ref[0], preferred_element_type=F32)
                      + jnp.dot(hb[1], m_ref[1], preferred_element_type=F32))

    outs, couts = _call(
        body, [hloc, hin, a, mats], name=name, grid=(2, NS, NCH),
        in_specs=[pl.BlockSpec((None, 2, TC, SL), lambda d, j, k: (d, 0, ck(d, k), j)),
                  pl.BlockSpec((None, 2, SUB, SL), lambda d, j, k: (d, 0, 0, j)),
                  pl.BlockSpec((None, 2, 1, SL), lambda d, j, k: (d, 0, 0, j)),
                  pl.BlockSpec((None, 2, None, SL, CW), lambda d, j, k: (d, 0, j, 0, 0))],
        out_specs=[pl.BlockSpec((None, 2, TC, SL), lambda d, j, k: (d, 0, ck(d, k), j)),
                   pl.BlockSpec((None, TC, CW), lambda d, j, k: (d, ck(d, k), j))],
        out_shape=[jax.ShapeDtypeStruct((2, 2, T, N), BF16), jax.ShapeDtypeStruct((2, T, NS * CW), F32)],
        scratch_shapes=[pltpu.VMEM((2, SUB, SL), F32), pltpu.VMEM((2, TC, SL), F32)],
        sem=("parallel", "parallel", "arbitrary"), comm=comm)
    return outs if comm is None else (outs, couts)


def s5_grads(name, g, h, u, dy, comm=None):
    _, _, T, N = g.shape
    W = u.shape[-1]
    TC, NG, NCH, NS = _s5_dims(T, N)
    CW, SL = W // NS, S5_STRIP

    def body(g_ref, h_ref, hp_ref, hl_ref, u_ref, dy_ref, dm_ref, dc_ref, da_ref, hs):
        k = pl.program_id(2)
        sub = lax.broadcasted_iota(jnp.int32, (SUB, SL), 0)

        hf = h_ref[...].astype(F32)

        @pl.when(pl.program_id(0) == 0)
        def _():
            for z in range(2):
                wrapped = jnp.where(sub == 0, 0.0, pltpu.roll(hl_ref[z].astype(F32)[SUB:], 1, 0))
                hs[z, 0:SUB, :] = jnp.where(k == 0, wrapped, hp_ref[z].astype(F32)[SUB:])
                hs[z, SUB:TC, :] = hf[z, 0:TC - SUB]

        @pl.when(pl.program_id(0) == 1)
        def _():
            for z in range(2):
                wrapped = jnp.where(sub == SUB - 1, 0.0, pltpu.roll(hl_ref[z].astype(F32)[:SUB], SUB - 1, 0))
                hs[z, TC - SUB:TC, :] = jnp.where(k == NCH - 1, wrapped, hp_ref[z].astype(F32)[:SUB])
                hs[z, 0:TC - SUB, :] = hf[z, SUB:TC]

        gr, gi, pr, pi = g_ref[0].astype(F32), g_ref[1].astype(F32), hs[0], hs[1]
        dar = jnp.sum((gr * pr + gi * pi).reshape(NG, SUB, SL), axis=0)
        dai = jnp.sum((gi * pr - gr * pi).reshape(NG, SUB, SL), axis=0)
        ub, dyb = u_ref[...].astype(BF16), dy_ref[...].astype(BF16)
        dm = [lax.dot_general(ub, g_ref[z], TN, preferred_element_type=F32) for z in range(2)]
        dc = [lax.dot_general(dyb, h_ref[z], TN, preferred_element_type=F32) for z in range(2)]

        @pl.when(k == 0)
        def _():
            da_ref[0], da_ref[1] = dar, dai
            for z in range(2):
                dm_ref[z], dc_ref[z] = dm[z], dc[z]

        @pl.when(k > 0)
        def _():
            da_ref[0] += dar
            da_ref[1] += dai
            for z in range(2):
                dm_ref[z] += dm[z]
                dc_ref[z] += dc[z]

    big = pl.BlockSpec((None, 2, TC, SL), lambda d, j, k: (d, 0, k, j))
    tok = pl.BlockSpec((None, TC, CW), lambda d, j, k: (d, k, j))
    mat = pl.BlockSpec((None, 2, None, CW, SL), lambda d, j, k: (d, 0, j, 0, 0))
    outs, couts = _call(
        body, [g, h, h, h, u, dy], name=name, grid=(2, NS, NCH),
        in_specs=[big, big,
                  pl.BlockSpec((None, 2, 2 * SUB, SL), lambda d, j, k: (
                      d, 0, jnp.where(d == 0, jnp.maximum(k * NG - 1, 0), jnp.minimum((k + 1) * NG, T // SUB - 1)) // 2, j)),
                  pl.BlockSpec((None, 2, 2 * SUB, SL), lambda d, j, k: (d, 0, jnp.where(d == 0, T // SUB - 1, 0) // 2, j)),
                  tok, tok],
        out_specs=[mat, mat, pl.BlockSpec((None, 2, SUB, SL), lambda d, j, k: (d, 0, 0, j))],
        out_shape=[jax.ShapeDtypeStruct((2, 2, NS, CW, SL), F32), jax.ShapeDtypeStruct((2, 2, NS, CW, SL), F32),
                   jax.ShapeDtypeStruct((2, 2, SUB, N), F32)],
        scratch_shapes=[pltpu.VMEM((2, TC, SL), F32)],
        sem=("parallel", "parallel", "arbitrary"), comm=comm)
    return outs if comm is None else (outs, couts)


def _interleave(seq):
    *lead, T, W = seq.shape
    n = len(lead)
    return seq.reshape(*lead, S5_SEG, T // S5_SEG, W).swapaxes(n, n + 1).reshape(*lead, T, W)


def _deinterleave(seq):
    *lead, T, W = seq.shape
    n = len(lead)
    return seq.reshape(*lead, T // S5_SEG, S5_SEG, W).swapaxes(n, n + 1).reshape(*lead, T, W)


def _s5_discretize(lam_re, lam_im, log_dt, b_re, b_im):
    dt = jnp.exp(log_dt)[..., None]
    mag = jnp.exp(lam_re * dt)
    a_re = mag * jnp.cos(lam_im * dt)
    a_im = mag * jnp.sin(lam_im * dt)
    den = jnp.square(lam_re) + jnp.square(lam_im)
    f_re = ((a_re - 1.0) * lam_re + a_im * lam_im) / den
    f_im = (a_im * lam_re - (a_re - 1.0) * lam_im) / den
    bb_re = f_re[..., None] * b_re - f_im[..., None] * b_im
    bb_im = f_re[..., None] * b_im + f_im[..., None] * b_re
    return a_re, a_im, bb_re, bb_im


_GPS = S5_STRIP // SSM_STATE


def _blockdiag(t):
    d2, G, P, Cg = t.shape
    t5 = t.reshape(d2, G // _GPS, _GPS, P, Cg).transpose(0, 1, 2, 4, 3)
    m = t5[:, :, :, :, None, :] * jnp.eye(_GPS, dtype=t.dtype)[None, None, :, None, :, None]
    return m.reshape(d2, G // _GPS, _GPS * Cg, _GPS * P)


def _blockdiag_extract(m, Cg, P):
    d2, NS = m.shape[:2]
    m6 = m.reshape(d2, NS, _GPS, Cg, _GPS, P)
    diag = jnp.stack([m6[:, :, i, :, i, :] for i in range(_GPS)], axis=2)
    return diag.transpose(0, 1, 2, 4, 3).reshape(d2, NS * _GPS, P, Cg)


def _cmul(a, b):
    return a[0] * b[0] - a[1] * b[1], a[0] * b[1] + a[1] * b[0]


def _cpow(a, n):
    out, base = None, a
    while n:
        if n & 1:
            out = base if out is None else _cmul(out, base)
        base = _cmul(base, base)
        n >>= 1
    return out


def _segment_carry(fin, apow, rev):
    per_dir = []
    for d in range(2):
        fr, fi = fin[d, 0], fin[d, 1]
        ap = (apow[0][d], apow[1][d])
        cr = ci = jnp.zeros_like(fr[0:1])
        outs = [None] * S5_SEG
        backward = (d == 1) != rev
        for s in (range(S5_SEG - 1, -1, -1) if backward else range(S5_SEG)):
            outs[s] = (cr, ci)
            pr, pi = _cmul(ap, (cr, ci))
            cr, ci = pr + fr[s:s + 1], pi + fi[s:s + 1]
        per_dir.append(jnp.stack([jnp.concatenate([o[0] for o in outs]), jnp.concatenate([o[1] for o in outs])]))
    return jnp.stack(per_dir)


def _coords():
    x, y, c = lax.axis_index("x"), lax.axis_index("y"), lax.axis_index("c")
    others = [(1 - x, y), (x, 1 - y), (1 - x, 1 - y)]
    return x, y, c, 2 * x + y, others


def _comm(name, ins, out_shapes, aliases, n_local, n_remote, plan):
    n_in, n_out = len(ins), len(out_shapes)

    def body(*refs):
        in_refs, out_refs = refs[:n_in], refs[n_in:n_in + n_out]
        send_sems, recv_sems, local_sems = refs[n_in + n_out:]
        x, y, c = lax.axis_index("x"), lax.axis_index("y"), lax.axis_index("c")
        locs, sends, lands = plan(in_refs, out_refs)
        assert len(locs) == n_local and len(sends) == n_remote and len(lands) == n_remote
        local = [pltpu.make_async_copy(s, d, local_sems.at[i]) for i, (s, d) in enumerate(locs)]
        for cp in local:
            cp.start()
        remote = [pltpu.make_async_remote_copy(src_ref=s, dst_ref=d, send_sem=send_sems.at[i], recv_sem=recv_sems.at[i],
                                               device_id=peer, device_id_type=MESH)
                  for i, (s, d, peer) in enumerate(sends)]
        for cp in remote:
            cp.start()
        for i, d in enumerate(lands):
            pltpu.make_async_remote_copy(src_ref=d, dst_ref=d, send_sem=send_sems.at[i], recv_sem=recv_sems.at[i],
                                         device_id=(x, y, c), device_id_type=MESH).wait_recv()
        for cp in remote:
            cp.wait_send()
        for cp in local:
            cp.wait()

    any_spec = pl.BlockSpec(memory_space=pl.ANY)
    return pl.pallas_call(
        body, name=name,
        in_specs=[any_spec] * n_in, out_specs=[any_spec] * n_out,
        out_shape=[jax.ShapeDtypeStruct(s, d) for s, d in out_shapes],
        input_output_aliases=aliases,
        scratch_shapes=[pltpu.SemaphoreType.DMA((n_remote,)), pltpu.SemaphoreType.DMA((n_remote,)),
                        pltpu.SemaphoreType.DMA((max(n_local, 1),))],
        compiler_params=pltpu.CompilerParams(has_side_effects=True),
    )(*ins)


def allgather_dev(name, v):
    M, Nc = v.shape

    def plan(in_refs, out_refs):
        (v_ref,), (o_ref,) = in_refs, out_refs
        x, y, c = lax.axis_index("x"), lax.axis_index("y"), lax.axis_index("c")

        def rows(px, py, pc):
            return o_ref.at[pl.ds((4 * px + 2 * py + pc) * M, M), :]

        peers = [(x ^ fx, y ^ fy, c ^ fc) for fx in (0, 1) for fy in (0, 1) for fc in (0, 1) if fx or fy or fc]
        return ([(v_ref, rows(x, y, c))],
                [(v_ref, rows(x, y, c), p) for p in peers],
                [rows(*p) for p in peers])

    return _comm(name, [v], [((N_DEV * M, Nc), v.dtype)], {}, 1, N_DEV - 1, plan)[0]


def allgather_chips_1(name, shards):
    def plan(in_refs, out_refs):
        x, y, c, chip, others = _coords()
        sends, lands = [], []
        for s_ref, g_ref in zip(in_refs, out_refs):
            hr = s_ref.shape[0] // 2
            mine = pl.ds(c * hr, hr)
            for qx, qy in others:
                sends.append((s_ref.at[mine], g_ref.at[chip, mine], (qx, qy, c)))
                lands.append(g_ref.at[2 * qx + qy, mine])
        return [], sends, lands

    n = len(shards)
    comm = (list(shards), [((N_CHIP,) + s.shape, s.dtype) for s in shards], {}, 3 * n, plan)
    return comm if name is None else _comm(name, comm[0], comm[1], comm[2], 0, comm[3], comm[4])


def allgather_chips_2(name, gathered, shards):
    n = len(gathered)

    def plan(in_refs, out_refs):
        x, y, c, chip, others = _coords()
        sends, lands = [], []
        for s_ref, g_ref in zip(in_refs[n:], out_refs):
            hr = g_ref.shape[1] // 2
            for qx, qy in others:
                q = 2 * qx + qy
                sends.append((g_ref.at[q, pl.ds(c * hr, hr)], g_ref.at[q, pl.ds(c * hr, hr)], (x, y, 1 - c)))
                lands.append(g_ref.at[q, pl.ds((1 - c) * hr, hr)])
            sends.append((s_ref, g_ref.at[chip], (x, y, 1 - c)))
            lands.append(g_ref.at[chip])
        return [], sends, lands

    comm = (list(gathered) + list(shards), [(g.shape, g.dtype) for g in gathered], {i: i for i in range(n)}, 4 * n, plan)
    return comm if name is None else _comm(name, comm[0], comm[1], comm[2], 0, comm[3], comm[4])


def reduce_1(name, grads):
    def plan(in_refs, out_refs):
        x, y, c, chip, others = _coords()
        sends, lands = [], []
        for g_ref, got_ref in zip(in_refs, out_refs):
            hr = g_ref.shape[1] // 2
            sends.append((g_ref.at[:, pl.ds((1 - c) * hr, hr), :], got_ref, (x, y, 1 - c)))
            lands.append(got_ref)
        return [], sends, lands

    n = len(grads)
    comm = (list(grads), [((g.shape[0], g.shape[1] // 2, g.shape[2]), g.dtype) for g in grads], {}, n, plan)
    return comm if name is None else _comm(name, comm[0], comm[1], comm[2], 0, comm[3], comm[4])


def _merge_comm(a, b):
    if a is None or b is None:
        return a if b is None else b
    na_in, na_out = len(a[0]), len(a[1])

    def plan(in_refs, out_refs):
        _, s1, l1 = a[4](in_refs[:na_in], out_refs[:na_out])
        _, s2, l2 = b[4](in_refs[na_in:], out_refs[na_out:])
        return [], s1 + s2, l1 + l2

    alias = dict(a[2])
    alias.update({na_in + i: na_out + j for i, j in b[2].items()})
    return (a[0] + b[0], a[1] + b[1], alias, a[3] + b[3], plan)


def reduce_2(name, parts):
    def plan(in_refs, out_refs):
        x, y, c, chip, others = _coords()
        sends, lands = [], []
        for t_ref, q_ref in zip(in_refs, out_refs):
            for qx, qy in others:
                sends.append((t_ref.at[2 * qx + qy], q_ref.at[chip], (qx, qy, c)))
                lands.append(q_ref.at[2 * qx + qy])
        return [], sends, lands

    n = len(parts)
    comm = (list(parts), [(p.shape, p.dtype) for p in parts], {}, 3 * n, plan)
    return comm if name is None else _comm(name, comm[0], comm[1], comm[2], 0, comm[3], comm[4])


def share_slots(name, slots):
    def plan(in_refs, out_refs):
        x, y, c, chip, others = _coords()
        (q_ref,) = out_refs
        return ([], [(q_ref.at[chip], q_ref.at[chip], (qx, qy, c)) for qx, qy in others],
                [q_ref.at[2 * qx + qy] for qx, qy in others])

    return _comm(name, [slots], [(slots.shape, slots.dtype)], {0: 0}, 0, N_CHIP - 1, plan)[0]


def allreduce_small(tag, buf, ids):
    got = reduce_1(tag + "_1", [buf[None]])[0][0]
    slots = share_slots(tag + "_2", pair_sum_to_slot(tag + "_add", buf, got, ids))
    full = reduce_3(tag + "_3", [sum_chips_to_half(tag + "_sum", slots, ids)])[0]
    return full.reshape(buf.shape)


def reduce_3(name, fulls):
    def plan(in_refs, out_refs):
        x, y, c, chip, others = _coords()
        sends, lands = [], []
        for o_ref in out_refs:
            sends.append((o_ref.at[c], o_ref.at[c], (x, y, 1 - c)))
            lands.append(o_ref.at[1 - c])
        return [], sends, lands

    n = len(fulls)
    return _comm(name, fulls, [(f.shape, f.dtype) for f in fulls], {i: i for i in range(n)}, 0, n, plan)


_WEIGHTS = ['c_ctx', 'w_mod', 'b_mod', 'g_mix', 'g_ffn', 'w_in', 'ssm_lam_re', 'ssm_lam_im', 'ssm_log_dt', 'ssm_b_re',
            'ssm_b_im', 'ssm_c_re', 'ssm_c_im', 'ssm_d', 'ssm_w_glu', 'na_rpb', 'w_out', 'cv_w_pw1', 'cv_dw_w', 'cv_dw_b',
            'cv_ln_g', 'cv_ln_b', 'cv_w_pw2', 'ffn_w_up', 'ffn_conv_w', 'ffn_conv_b', 'ffn_w_down', 'g_out']
_INPUTS = ['x', 'c', 'ctx'] + _WEIGHTS + ['loss_target'] + ['m_' + w for w in _WEIGHTS] + ['v_' + w for w in _WEIGHTS]
_GATHERED_SMALL = ['ffn_conv_w', 'cv_dw_w', 'cv_dw_b', 'cv_ln_g', 'cv_ln_b']


def _silu(v):
    return v * jax.nn.sigmoid(v)


def _pick_index(t, idx, axis):
    shape = [1] * t.ndim
    shape[axis] = t.shape[axis]
    mask = (jnp.arange(t.shape[axis]) == idx).reshape(shape)
    return jnp.sum(jnp.where(mask, t, jnp.zeros((), t.dtype)), axis=axis)


def _pack(arrs, cols, row_mult=SUB):
    flat = jnp.concatenate([a.reshape(-1).astype(F32) for a in arrs])
    n = flat.shape[0]
    unit = row_mult * cols
    flat = jnp.pad(flat, (0, (-n) % unit))
    return flat.reshape(-1, cols)


def _unpack(buf, shapes):
    flat = buf.reshape(-1)
    out, o = [], 0
    for s in shapes:
        n = int(np.prod(s))
        out.append(flat[o:o + n].reshape(s))
        o += n
    return out


def _carried(res, comm):
    return res if comm is not None else (res, [])


def _ffn_fwd(tag, xin, sh, sc, gt, g, wup, cw3, cb3, wdn, comm_up=None, comm_mid=None, comm_down=None, hf=None, nxt=None):
    if hf is None:
        hf = norm_mod_fwd(tag + "_norm", xin, g * (1.0 + sc), sh)
    up3, got_up = _carried(mm_nn_pieces(tag + "_up", hf, wup, 0, N_CHIP, BF16, halves=2, comm=comm_up), comm_up)
    comm_mid = comm_mid(got_up) if callable(comm_mid) else comm_mid
    act, got_mid = _carried(ffn_mid_fwd(tag + "_mid", up3, cw3, cb3, comm=comm_mid), comm_mid)
    comm_down = comm_down(got_mid) if callable(comm_down) else comm_down
    yf, got_down = _carried(mm_nn(tag + "_down", act, wdn, BF16, comm=comm_down), comm_down)
    if nxt is None:
        xo, hn = gate_res_fwd(tag + "_res", xin, yf, gt), None
    else:
        xo, hn = res_norm_fwd(tag + "_res", xin, yf, gt, *nxt)
    return xo, (xin, hf, up3, act, yf), got_up, got_mid, got_down, hn


def _ffn_bwd(tag, dxo, saved, sc, gt, g, wup, cw3, cb3, wdn, comm_down=None, comm_mid=None, comm_up=None):
    xin, hf, up3, act, yf = saved
    dyf, dgt = gate_res_bwd(tag + "_res_b", dxo, yf, gt)
    dact, got_down = _carried(mm_nt(tag + "_down_bx", dyf, wdn, BF16, comm=comm_down), comm_down)
    dwdn = mm_tn(tag + "_down_bw", act, dyf, BF16)
    comm_mid = comm_mid(got_down) if callable(comm_mid) else comm_mid
    (dup3, dcw3, dcb3), got_mid = _carried(ffn_mid_bwd(tag + "_mid_b", up3, dact, cw3, cb3, comm=comm_mid), comm_mid)
    dhf, got_up = _carried(mm_nt_pieces(tag + "_up_bx", dup3, wup, BF16, halves=2, comm=comm_up), comm_up)
    dwup = mm_tn_pieces(tag + "_up_bw", hf, dup3, N_CHIP, BF16, halves=2)
    dxi, cs1, cs2 = norm_mod_bwd(tag + "_norm_b", xin, dhf, g * (1.0 + sc), dxo)
    return dxi, dict(dsh=cs1[0], dsc=cs2[0] * g, dgt=dgt[0], dg=cs2[0] * (1.0 + sc), dwup=dwup, dwdn=dwdn,
                     dcw=dcw3.transpose(1, 0, 2).reshape(3, -1), dcb=dcb3.reshape(-1)), got_down, got_mid, got_up


def kernel(x, c, ctx, c_ctx, w_mod, b_mod, g_mix, g_ffn, w_in, ssm_lam_re, ssm_lam_im, ssm_log_dt, ssm_b_re, ssm_b_im, ssm_c_re, ssm_c_im, ssm_d, ssm_w_glu, na_rpb, w_out, cv_w_pw1, cv_dw_w, cv_dw_b, cv_ln_g, cv_ln_b, cv_w_pw2, ffn_w_up, ffn_conv_w, ffn_conv_b, ffn_w_down, g_out, loss_target, m_c_ctx, m_w_mod, m_b_mod, m_g_mix, m_g_ffn, m_w_in, m_ssm_lam_re, m_ssm_lam_im, m_ssm_log_dt, m_ssm_b_re, m_ssm_b_im, m_ssm_c_re, m_ssm_c_im, m_ssm_d, m_ssm_w_glu, m_na_rpb, m_w_out, m_cv_w_pw1, m_cv_dw_w, m_cv_dw_b, m_cv_ln_g, m_cv_ln_b, m_cv_w_pw2, m_ffn_w_up, m_ffn_conv_w, m_ffn_conv_b, m_ffn_w_down, m_g_out, v_c_ctx, v_w_mod, v_b_mod, v_g_mix, v_g_ffn, v_w_in, v_ssm_lam_re, v_ssm_lam_im, v_ssm_log_dt, v_ssm_b_re, v_ssm_b_im, v_ssm_c_re, v_ssm_c_im, v_ssm_d, v_ssm_w_glu, v_na_rpb, v_w_out, v_cv_w_pw1, v_cv_dw_w, v_cv_dw_b, v_cv_ln_g, v_cv_ln_b, v_cv_w_pw2, v_ffn_w_up, v_ffn_conv_w, v_ffn_conv_b, v_ffn_w_down, v_g_out):
    p = dict(locals())
    xi, yi, ci = lax.axis_index("x"), lax.axis_index("y"), lax.axis_index("c")
    me, chip = 4 * xi + 2 * yi + ci, 2 * xi + yi
    xs, cx, tgt = x[0], ctx[0], loss_target[0]
    L, D = xs.shape
    Lc = cx.shape[0]
    T = L + Lc
    W = D // 2
    Cq = w_mod.shape[2]

    s_mix = [t.astype(BF16) for t in (w_in[0], ssm_w_glu[0], w_out[0])]
    s_ffn0 = [t.astype(BF16) for t in (ffn_w_up[0], ffn_w_down[0])]
    s_conv = [t.astype(BF16) for t in (cv_w_pw1[0], cv_w_pw2[0])]
    s_ffn1 = [t.astype(BF16) for t in (ffn_w_up[1], ffn_w_down[1])]
    (Win,) = allgather_chips_2("gather_win_2", allgather_chips_1("gather_win_1", s_mix[:1]), s_mix[:1])
    Fd = ffn_w_down.shape[1] * N_CHIP
    c_idx = jnp.reshape(ci, (1,)).astype(jnp.int32)
    ids = jnp.stack([chip, ci]).astype(jnp.int32)

    def added(tag, grads, got):
        return [add_half("reduce_%s_add%d" % (tag, i), g, r, c_idx) for i, (g, r) in enumerate(zip(grads, got))]

    small_shapes = [p[n].shape for n in _GATHERED_SMALL]
    sm = allgather_dev("gather_small", _pack([p[n] for n in _GATHERED_SMALL], 1024))
    sm = sm.reshape(N_DEV, -1)[0::2]
    per_chip = [_unpack(sm[q], small_shapes) for q in range(N_CHIP)]
    conv_w_f, dw_w_f, dw_b_f, ln_g_f, ln_b_f = (jnp.concatenate([pc[i] for pc in per_chip], axis=-1)
                                                for i in range(len(_GATHERED_SMALL)))
    cw3 = [conv_w_f[l].reshape(3, 2, Fd).transpose(1, 0, 2) for l in range(2)]
    cb3 = [ffn_conv_b[l].reshape(2, 1, Fd) for l in range(2)]
    dw_w_f, dw_b_f, ln_g_f, ln_b_f = dw_w_f[0], dw_b_f[0], ln_g_f[0], ln_b_f[0]

    c_all = allgather_dev("gather_c", jnp.zeros((SUB, D), F32).at[0].set(c[0])).reshape(N_DEV, SUB, D)[:, 0]
    S16 = jnp.concatenate([_silu(c_all), _silu(c_ctx)[None], jnp.zeros((2 * SUB - N_DEV - 1, D), F32)])
    modp = mm_nn_pieces("mod_fwd", S16, w_mod, 0, 2, F32)
    modg = allgather_dev("gather_mod", modp).reshape(N_DEV, 2 * SUB, 2, Cq)[0::2]

    def mod_row(r):
        return r.transpose(1, 0, 2).reshape(2, N_CHIP * Cq) + b_mod

    mod_me = mod_row(_pick_index(modg, me, 1))
    mod_c = mod_row(modg[:, N_DEV])
    mods = [[mod_me[l, i * D:(i + 1) * D] for i in range(N_MOD)] for l in range(2)]
    shc, scc = mod_c[0, :D], mod_c[0, D:2 * D]

    sh_m, sc_m, gt_m, sh_f, sc_f, gt_f = mods[0]
    h0 = norm_mod_fwd("l0_norm", xs, g_mix[0] * (1.0 + sc_m), sh_m)
    hc0 = norm_mod_fwd("l0_norm_c", cx, g_mix[0] * (1.0 + scc), shc)
    u = mm_nn_pieces("l0_in_u", h0, Win, 0, 1, F32)
    qkv, g_mix1 = mm_nn_pieces("l0_in_qkv", h0, Win, 1, 3, BF16, comm=allgather_chips_1(None, s_mix[1:]))
    uc = mm_nn_pieces("l0_in_uc", hc0, Win, 0, 1, F32)
    kvc = mm_nn_pieces("l0_in_kvc", hc0, Win, 2, 2, BF16)

    lam_re, lam_im, log_dt = ssm_lam_re[0], ssm_lam_im[0], ssm_log_dt[0]
    b_re, b_im, c_re, c_im = ssm_b_re[0], ssm_b_im[0], ssm_c_re[0], ssm_c_im[0]
    (a_re, a_im, bb_re, bb_im), disc_vjp = jax.vjp(_s5_discretize, lam_re, lam_im, log_dt, b_re, b_im)
    G, P, Cg = bb_re.shape[1:]
    N = G * P
    a_re, a_im = a_re.reshape(2, 1, N), a_im.reshape(2, 1, N)
    a_f, a_b = jnp.stack([a_re, a_im], axis=1), jnp.stack([a_re, -a_im], axis=1)
    Bblk = jnp.stack([_blockdiag(bb_re), _blockdiag(bb_im)], axis=1)
    Cblk = jnp.stack([_blockdiag(c_re.swapaxes(-1, -2)), -_blockdiag(c_im.swapaxes(-1, -2))], axis=1)
    apow = _cpow((a_re, a_im), T // S5_SEG)

    useq = _interleave(jnp.stack([jnp.concatenate([uc, u]), jnp.concatenate([u, uc])]).astype(BF16))
    (hloc, fin), (Wglu, Wout, g_dn0) = s5_scan(
        "s5_scan", useq, Bblk.astype(BF16), a_f, rev=False,
        comm=_merge_comm(allgather_chips_2(None, g_mix1, s_mix[1:]), allgather_chips_1(None, s_ffn0[1:])))
    Wglu, Wout = Wglu.reshape(-1, Wglu.shape[-1]), Wout.reshape(-1, D)
    (hst, yseq), (g_pw1, g_pw2) = s5_fix("s5_fix", hloc, _segment_carry(fin, apow, False), a_f,
                                         Cblk.swapaxes(-1, -2).astype(BF16), rev=False, comm=allgather_chips_1(None, s_conv))
    g_dn0 = [g_dn0]
    ys = _deinterleave(yseq)
    y0, y1 = ys[0, Lc:], ys[1, :L]
    s5o = glu_fwd("s5_glu", u, y0, y1, ssm_d[0], Wglu)

    bias = na_bias(na_rpb[0])
    (o_na, lse), (g_up0, Wdn0) = natten_fwd(
        "na_fwd", qkv, kvc, bias,
        comm=_merge_comm(allgather_chips_1(None, s_ffn0[:1]), allgather_chips_2(None, g_dn0, s_ffn0[1:])))
    mixcat = jnp.concatenate([s5o, o_na], axis=1)
    ymix, (Wup0,) = mm_nn("l0_out", mixcat, Wout, BF16, comm=allgather_chips_2(None, [g_up0], s_ffn0[:1]))
    sh_v, sc_v, gt_v, sh_g, sc_g, gt_g = mods[1]
    x1, hf0 = res_norm_fwd("l0_res", xs, ymix, gt_m, g_ffn[0] * (1.0 + sc_f), sh_f)
    Wdn0 = Wdn0.reshape(-1, D)
    x2, ffn0, (g_up1,), (g_dn1, Wup1), (Wdn1,), hcv = _ffn_fwd(
        "f0", x1, sh_f, sc_f, gt_f, g_ffn[0], Wup0, cw3[0], cb3[0], Wdn0,
        comm_up=allgather_chips_1(None, s_ffn1[:1]),
        comm_mid=lambda got_up: _merge_comm(allgather_chips_1(None, s_ffn1[1:]), allgather_chips_2(None, got_up, s_ffn1[:1])),
        comm_down=lambda got_mid: allgather_chips_2(None, got_mid[:1], s_ffn1[1:]),
        hf=hf0, nxt=(g_mix[1] * (1.0 + sc_v), sh_v))
    Wpw1, Wpw2 = allgather_chips_2("gather_conv_2", [g_pw1, g_pw2], s_conv)
    Wpw2 = Wpw2.reshape(-1, D)
    Wup, Wdn = [Wup0, Wup1], [Wdn0.reshape(-1, D), Wdn1.reshape(-1, D)]

    ag3 = mm_nn_pieces("l1_pw1", hcv, Wpw1, 0, N_CHIP, BF16, halves=2)
    z1, z3 = conf_mid_fwd("l1_mid", ag3, dw_w_f, dw_b_f, ln_g_f, ln_b_f)
    ycv = mm_nn("l1_pw2", z3, Wpw2, BF16)
    x3, hf1 = res_norm_fwd("l1_res", x2, ycv, gt_v, g_ffn[1] * (1.0 + sc_g), sh_g)
    x4, ffn1, _, _, _, _ = _ffn_fwd("f1", x3, sh_g, sc_g, gt_g, g_ffn[1], Wup[1], cw3[1], cb3[1], Wdn[1], hf=hf1)

    dx4, dg_out, loss_part = loss_head("loss", x4, g_out, tgt)
    loss = lax.psum(loss_part[0, 0], ("x", "y", "c"))

    dx3, gf1, _, _, _ = _ffn_bwd("f1", dx4, ffn1, sc_g, gt_g, g_ffn[1], Wup[1], cw3[1], cb3[1], Wdn[1])
    g_up1, g_dn1 = [gf1["dwup"]], [gf1["dwdn"].reshape(N_CHIP, -1, D)]
    dycv, dgt_v = gate_res_bwd("l1_res_b", dx3, ycv, gt_v)
    dz3, got = mm_nt("l1_pw2_bx", dycv, Wpw2, BF16, comm=reduce_1(None, g_up1))
    parts_up1 = added("up1", g_up1, got)
    dWpw2, got = mm_tn("l1_pw2_bw", z3, dycv, BF16, comm=reduce_1(None, g_dn1))
    parts_dn1 = added("dn1", g_dn1, got)
    dz1, dln_g, dln_b = conf_ln_bwd("l1_ln_b", z1, dz3, ln_g_f, ln_b_f)
    (dag3, ddw_w, ddw_b), slots_up1 = conf_conv_bwd("l1_conv_b", ag3, dz1, dw_w_f, comm=reduce_2(None, parts_up1))
    dhcv = mm_nt_pieces("l1_pw1_bx", dag3, Wpw1, BF16, halves=2)
    dWpw1 = mm_tn_pieces("l1_pw1_bw", hcv, dag3, N_CHIP, BF16, halves=2)
    dx2, cs1_v, cs2_v = norm_mod_bwd("l1_norm_b", x2, dhcv, g_mix[1] * (1.0 + sc_v), dx3)
    g_conv = [dWpw1, dWpw2.reshape(N_CHIP, -1, D)]

    held = {}

    def conv_stage_2(got_down):
        held["parts_conv"] = added("conv", g_conv, got_down)
        return _merge_comm(reduce_2(None, held["parts_conv"]), reduce_2(None, parts_dn1))

    dx1, gf0, _, slots_mid, _ = _ffn_bwd("f0", dx2, ffn0, sc_f, gt_f, g_ffn[0], Wup[0], cw3[0], cb3[0], Wdn[0],
                                          comm_down=reduce_1(None, g_conv), comm_mid=conv_stage_2)
    slots_conv, slots_dn1 = slots_mid[:2], slots_mid[2:]
    parts_conv = held["parts_conv"]
    g_up0, g_dn0 = [gf0["dwup"]], [gf0["dwdn"].reshape(N_CHIP, -1, D)]
    dymix, dgt_m = gate_res_bwd("l0_res_b", dx1, ymix, gt_m)
    dmix, got = mm_nt("l0_out_bx", dymix, Wout, BF16, comm=reduce_1(None, g_up0))
    parts_up0 = added("up0", g_up0, got)
    dWout, got = mm_tn("l0_out_bw", mixcat, dymix, BF16, comm=reduce_1(None, g_dn0))
    parts_dn0 = added("dn0", g_dn0, got)
    (dq, dk, dv, dkc, dvc, dbias), slots_up0 = natten_bwd("na_bwd", qkv, kvc, bias, o_na, lse, dmix,
                                                          comm=reduce_2(None, parts_up0))
    dy, zg, dzz, dd_skip = glu_bwd("s5_glu_b", u, y0, y1, ssm_d[0], Wglu, dmix)
    dWglu = mm_tn("s5_glu_bw", zg, dzz, BF16)
    g_mix2 = [dWglu.reshape(N_CHIP, -1, W), dWout.reshape(N_CHIP, -1, D)]

    zc = jnp.zeros((Lc, W), F32)
    dyseq = _interleave(jnp.stack([jnp.concatenate([zc, dy]), jnp.concatenate([dy, zc])]).astype(BF16))
    (gloc, gfin), slots_dn0 = s5_scan("s5_scan_b", dyseq, Cblk.astype(BF16), a_b, rev=True, comm=reduce_2(None, parts_dn0))
    apow_b = (apow[0], -apow[1])
    (gst, duseq), got = s5_fix("s5_fix_b", gloc, _segment_carry(gfin, apow_b, True), a_b, Bblk.swapaxes(-1, -2).astype(BF16),
                               rev=True, comm=reduce_1(None, g_mix2))
    parts_mix2 = added("mix2", g_mix2, got)
    (dBm, dCm, da8), slots_mix2 = s5_grads("s5_grads", gst, hst, useq, dyseq, comm=reduce_2(None, parts_mix2))
    dus = _deinterleave(duseq)
    du = fma3("s5_du", dy, dus[0, Lc:], dus[1, :L], ssm_d[0], BF16)
    duc = dus[0, :Lc] + dus[1, L:]

    d_in = [du, dq, dk, dv]
    d_in_c = [duc, jnp.zeros((Lc, W), BF16), dkc, dvc]
    dh0 = mm_nt_list("l0_in_bx", d_in, Win, BF16)
    dhc0 = mm_nt_list("l0_in_bxc", d_in_c, Win, BF16)
    h_all = jnp.concatenate([hc0, h0])
    dWin = jnp.stack([mm_tn("l0_in_bw%d" % q, h_all, jnp.concatenate([dc.astype(BF16), dl.astype(BF16)]), BF16)
                      for q, (dc, dl) in enumerate(zip(d_in_c, d_in))])
    dx0, cs1_m, cs2_m = norm_mod_bwd("l0_norm_b", xs, dh0, g_mix[0] * (1.0 + sc_m), dx1)
    _, cs1_c, cs2_c = norm_mod_bwd("l0_norm_bc", cx, dhc0, g_mix[0] * (1.0 + scc), jnp.zeros_like(cx))

    dmod0 = jnp.concatenate([cs1_m[0], cs2_m[0] * g_mix[0], dgt_m[0], gf0["dsh"], gf0["dsc"], gf0["dgt"]])
    dmod1 = jnp.concatenate([cs1_v[0], cs2_v[0] * g_mix[1], dgt_v[0], gf1["dsh"], gf1["dsc"], gf1["dgt"]])
    dmodc = jnp.concatenate([cs1_c[0], cs2_c[0] * g_mix[0], jnp.zeros((4 * D,), F32)])
    dm_rows = jnp.concatenate([jnp.stack([dmod0, dmod1, dmodc]), jnp.zeros((SUB - 3, N_MOD * D), F32)])
    dm_all = allgather_dev("gather_dmod", dm_rows).reshape(N_DEV, SUB, N_MOD * D)
    dm_sum = sum_lead("sum_dmod", dm_all, F32)
    pad7 = jnp.zeros((2 * SUB - N_DEV - 1, N_MOD * D), F32)
    dMod = [jnp.concatenate([dm_all[:, 0], dm_sum[2:3], pad7]), jnp.concatenate([dm_all[:, 1], jnp.zeros_like(dm_sum[2:3]), pad7])]
    dMod_cols = [_pick_index(m.reshape(m.shape[0], N_CHIP, Cq), chip, 1) for m in dMod]
    g_w_mod = jnp.stack([mm_tn("mod_bw%d" % l, S16, dMod_cols[l], F32) for l in range(2)])
    g_b_mod = jnp.stack([dm_sum[0] + dm_sum[2], dm_sum[1]])
    ds_part = mm_nt("mod_bx", dMod_cols[0], w_mod[0], F32)
    ds_all = allgather_dev("gather_dsc", jnp.zeros((SUB, D), F32).at[0].set(ds_part[N_DEV]))
    ds_c = sum_lead("sum_dsc", ds_all.reshape(N_DEV, SUB, D)[0::2], F32)[0]
    sg_c = jax.nn.sigmoid(c_ctx)
    g_c_ctx = ds_c * sg_c * (1.0 + c_ctx * (1.0 - sg_c))

    g_rpb_loc = na_bias_grad(dbias)

    dbb = [_blockdiag_extract(dBm[:, z], Cg, P) for z in range(2)]
    dcc = [_blockdiag_extract(dCm[:, z], Cg, P).swapaxes(-1, -2) for z in range(2)]
    da = jnp.sum(da8, axis=2).reshape(2, 2, G, P)
    small = {
        "g_mix": jnp.stack([cs2_m[0] * (1.0 + sc_m) + cs2_c[0] * (1.0 + scc), cs2_v[0] * (1.0 + sc_v)]),
        "g_ffn": jnp.stack([gf0["dg"], gf1["dg"]]),
        "a_re": da[:, 0], "a_im": da[:, 1], "bb_re": dbb[0], "bb_im": dbb[1], "c_re": dcc[0], "c_im": -dcc[1],
        "ssm_d": dd_skip, "na_rpb": g_rpb_loc, "cv_dw_w": ddw_w, "cv_dw_b": ddw_b, "cv_ln_g": dln_g, "cv_ln_b": dln_b,
        "ffn_conv_w": jnp.stack([gf0["dcw"], gf1["dcw"]]), "ffn_conv_b": jnp.stack([gf0["dcb"], gf1["dcb"]]),
        "g_out": dg_out,
    }
    skeys = list(small)
    sbuf = _pack([small[k] for k in skeys], 1024, 4 * SUB)
    ssum = dict(zip(skeys, _unpack(allreduce_small("reduce_small", sbuf, ids), [small[k].shape for k in skeys])))
    g_lam_re, g_lam_im, g_log_dt, g_b_re, g_b_im = disc_vjp((ssum["a_re"], ssum["a_im"], ssum["bb_re"], ssum["bb_im"]))

    def my_cols(t):
        n = t.shape[-1] // N_CHIP
        return _pick_index(t.reshape(t.shape[:-1] + (N_CHIP, n)), chip, t.ndim - 1)

    delta, new_m, new_v = {}, {}, {}
    parts_win = added("win", [dWin], reduce_1("reduce_win_1", [dWin]))
    (delta["w_mod"], new_m["w_mod"], new_v["w_mod"]), slots_win = adamw(
        "adamw_w_mod", w_mod, g_w_mod, m_w_mod, v_w_mod, comm=reduce_2(None, parts_win))
    parts = parts_win + parts_mix2 + parts_conv + parts_up0 + parts_dn0 + parts_up1 + parts_dn1
    slots = [*slots_win, *slots_mix2, *slots_conv, *slots_up0, *slots_dn0, *slots_up1, *slots_dn1]
    fulls = [sum_slots("reduce_sum_%d" % i, s, t, ids) for i, (s, t) in enumerate(zip(slots, parts))]
    full = [f.reshape(-1, f.shape[-1]) for f in reduce_3("reduce_g_3", fulls)]
    gWin, gWglu, gWout, gWpw1, gWpw2, gWup0, gWdn0, gWup1, gWdn1 = full

    grads = {
        "c_ctx": g_c_ctx, "w_mod": g_w_mod, "b_mod": g_b_mod, "g_mix": ssum["g_mix"], "g_ffn": ssum["g_ffn"],
        "w_in": gWin[None], "ssm_lam_re": g_lam_re[None], "ssm_lam_im": g_lam_im[None], "ssm_log_dt": g_log_dt[None],
        "ssm_b_re": g_b_re[None], "ssm_b_im": g_b_im[None], "ssm_c_re": ssum["c_re"][None], "ssm_c_im": ssum["c_im"][None],
        "ssm_d": ssum["ssm_d"], "ssm_w_glu": gWglu[None], "na_rpb": ssum["na_rpb"][None], "w_out": gWout[None],
        "cv_w_pw1": gWpw1[None], "cv_dw_w": my_cols(ssum["cv_dw_w"])[None], "cv_dw_b": my_cols(ssum["cv_dw_b"]),
        "cv_ln_g": my_cols(ssum["cv_ln_g"]), "cv_ln_b": my_cols(ssum["cv_ln_b"]), "cv_w_pw2": gWpw2[None],
        "ffn_w_up": jnp.stack([gWup0, gWup1]), "ffn_conv_w": my_cols(ssum["ffn_conv_w"]), "ffn_conv_b": ssum["ffn_conv_b"],
        "ffn_w_down": jnp.stack([gWdn0, gWdn1]), "g_out": ssum["g_out"][0],
    }
    grads = {k: grads[k].reshape(p[k].shape) for k in _WEIGHTS}

    large = [k for k in _WEIGHTS if p[k].size >= (1 << 18) or k == "w_mod"]
    tiny = [k for k in _WEIGHTS if k not in large]
    for k in large:
        if k != "w_mod":
            delta[k], new_m[k], new_v[k] = adamw("adamw_" + k, p[k], grads[k], p["m_" + k], p["v_" + k])
    packs = [_pack([src[pre + k] for k in tiny], 1024) for src, pre in ((p, ""), (grads, ""), (p, "m_"), (p, "v_"))]
    outs = adamw("adamw_small", *packs)
    shapes = [p[k].shape for k in tiny]
    for dst, buf in zip((delta, new_m, new_v), outs):
        dst.update(zip(tiny, _unpack(buf, shapes)))

    return (loss, dx0[None], *[grads[k] for k in _WEIGHTS], *[delta[k] for k in _WEIGHTS],
            *[new_m[k] for k in _WEIGHTS], *[new_v[k] for k in _WEIGHTS])
```

```python
import functools
import math

import numpy as np
import jax
import jax.numpy as jnp
from jax import lax
from jax.experimental import pallas as pl
from jax.experimental.pallas import tpu as pltpu

F32, BF16 = jnp.float32, jnp.bfloat16
MESH = pl.DeviceIdType.MESH
V7X_VMEM_LIMIT = 56 << 20
LANE, SUB = 128, 8
N_CHIP, N_DEV = 4, 8

GRID_W = 64
N_MOD = 6
SSM_GROUP, SSM_STATE = 16, 64
NA_HEAD_DIM, NA_WIN_R, NA_WIN_C = 128, 8, 16
EPS = 1e-6
NEG = -1e30
ADAM_LR, ADAM_B1, ADAM_B2, ADAM_EPS, ADAM_WD, ADAM_STEP = 0.001, 0.9, 0.999, 1e-08, 0.01, 10
S5_STRIP = 512
S5_SEG = 8

NN = (((1,), (0,)), ((), ()))
NT = (((1,), (1,)), ((), ()))
TN = (((0,), (0,)), ((), ()))


def _params(*sem, side_effects=False):
    return pltpu.CompilerParams(dimension_semantics=sem if sem else None, vmem_limit_bytes=V7X_VMEM_LIMIT,
                                has_side_effects=side_effects)


def _call(body, args, *, name, grid, in_specs, out_specs, out_shape, sem, scratch_shapes=(), comm=None):
    out_specs, out_shape, scratch_shapes = list(out_specs), list(out_shape), list(scratch_shapes)
    if comm is None:
        outs = pl.pallas_call(body, name=name, grid=grid, in_specs=list(in_specs), out_specs=out_specs, out_shape=out_shape,
                              scratch_shapes=scratch_shapes, compiler_params=_params(*sem))(*args)
        return list(outs), []
    c_args, c_shapes, c_alias, n_remote, plan = comm
    n_in, n_out, n_ci, n_co, n_sc = len(args), len(out_shape), len(c_args), len(c_shapes), len(scratch_shapes)

    def wrapped(*refs):
        ins, cins = refs[:n_in], refs[n_in:n_in + n_ci]
        o0 = n_in + n_ci
        outs, couts = refs[o0:o0 + n_out], refs[o0 + n_out:o0 + n_out + n_co]
        s0 = o0 + n_out + n_co
        scr, (send_sems, recv_sems) = refs[s0:s0 + n_sc], refs[s0 + n_sc:]
        pids = [pl.program_id(a) for a in range(len(grid))]
        first = functools.reduce(jnp.logical_and, [q == 0 for q in pids])
        last = functools.reduce(jnp.logical_and, [q == g - 1 for q, g in zip(pids, grid)])
        me = (lax.axis_index("x"), lax.axis_index("y"), lax.axis_index("c"))

        def copies():
            _, sends, lands = plan(cins, couts)
            assert len(sends) == n_remote and len(lands) == n_remote
            out = [pltpu.make_async_remote_copy(src_ref=s, dst_ref=d, send_sem=send_sems.at[i], recv_sem=recv_sems.at[i],
                                                device_id=peer, device_id_type=MESH) for i, (s, d, peer) in enumerate(sends)]
            arrivals = [pltpu.make_async_remote_copy(src_ref=d, dst_ref=d, send_sem=send_sems.at[i], recv_sem=recv_sems.at[i],
                                                     device_id=me, device_id_type=MESH) for i, d in enumerate(lands)]
            return out, arrivals

        @pl.when(first)
        def _():
            for cp in copies()[0]:
                cp.start()

        body(*ins, *outs, *scr)

        @pl.when(last)
        def _():
            out, arrivals = copies()
            for cp in arrivals:
                cp.wait_recv()
            for cp in out:
                cp.wait_send()

    any_spec = pl.BlockSpec(memory_space=pl.ANY)
    res = pl.pallas_call(
        wrapped, name=name, grid=grid,
        in_specs=[*in_specs, *[any_spec] * n_ci], out_specs=[*out_specs, *[any_spec] * n_co],
        out_shape=[*out_shape, *[jax.ShapeDtypeStruct(s, d) for s, d in c_shapes]],
        input_output_aliases={n_in + i: n_out + j for i, j in c_alias.items()},
        scratch_shapes=[*scratch_shapes, pltpu.SemaphoreType.DMA((n_remote,)), pltpu.SemaphoreType.DMA((n_remote,))],
        compiler_params=_params(*["arbitrary"] * len(grid), side_effects=True),
    )(*args, *c_args)
    return list(res[:n_out]), list(res[n_out:])


def _pick(n, pref, mult=LANE):
    if n <= pref:
        return n
    best = None
    for t in range(mult, pref + 1, mult):
        if n % t == 0:
            best = t
    assert best is not None, (n, pref, mult)
    return best


def _sigmoid(x):
    return 1.0 / (1.0 + jnp.exp(-x))


def _mm(name, a, b, *, dims, grid, a_spec, b_spec, o_spec, out_shape, out_dtype, acc_shape, exact=False, comm=None):
    nk = grid[2]

    def body(a_ref, b_ref, o_ref, *scratch):
        if exact:
            part = lax.dot_general(a_ref[...], b_ref[...], dims, preferred_element_type=F32,
                                   precision=lax.Precision.HIGHEST)
        else:
            part = lax.dot_general(a_ref[...].astype(BF16), b_ref[...].astype(BF16), dims,
                                   preferred_element_type=F32)
        if nk == 1:
            o_ref[...] = part.astype(o_ref.dtype)
        else:
            acc = scratch[0]
            kk = pl.program_id(2)

            @pl.when(kk == 0)
            def _():
                acc[...] = part

            @pl.when(kk > 0)
            def _():
                acc[...] += part

            @pl.when(kk == nk - 1)
            def _():
                o_ref[...] = acc[...].astype(o_ref.dtype)

    outs, couts = _call(body, [a, b], name=name, grid=grid, in_specs=[a_spec, b_spec], out_specs=[o_spec],
                        out_shape=[jax.ShapeDtypeStruct(out_shape, out_dtype)],
                        scratch_shapes=[] if nk == 1 else [pltpu.VMEM(acc_shape, F32)],
                        sem=("parallel", "parallel", "arbitrary"), comm=comm)
    return outs[0] if comm is None else (outs[0], couts)


MM_VMEM_BUDGET = 36 << 20


def _fit(M, N, cost, m_mult=SUB):
    best = None
    for tm in sorted({_pick(M, p, m_mult) for p in (2048, 1024, 512, 256, 128)}):
        for tn in sorted({_pick(N, p) for p in (1408, 1024, 512, 256, 128)}):
            if best is None or (cost(tm, tn) <= MM_VMEM_BUDGET and tm * tn > best[0] * best[1]):
                best = (tm, tn)
    return best


def _sz(t):
    return jnp.dtype(t).itemsize


def mm_nn_pieces(name, a, w, p0, n_p, out_dtype, halves=1, comm=None):
    M, K = a.shape
    Nq = w.shape[2]
    tm, tn = _fit(M, Nq, lambda m, n: 2 * (m * K * _sz(a.dtype) + K * n * _sz(w.dtype) + m * n * _sz(out_dtype)))
    tpp = Nq // tn
    pph = n_p // halves
    if halves == 1:
        o_spec = pl.BlockSpec((tm, tn), lambda i, j, k: (i, j))
        oshape = (M, n_p * Nq)
    else:
        o_spec = pl.BlockSpec((None, tm, tn), lambda i, j, k: ((j // tpp) // pph, i, ((j // tpp) % pph) * tpp + j % tpp))
        oshape = (halves, M, pph * Nq)
    return _mm(name, a, w, dims=NN, grid=(M // tm, n_p * tpp, 1),
               a_spec=pl.BlockSpec((tm, K), lambda i, j, k: (i, 0)),
               b_spec=pl.BlockSpec((None, K, tn), lambda i, j, k: (p0 + j // tpp, 0, j % tpp)),
               o_spec=o_spec, out_shape=oshape, out_dtype=out_dtype, acc_shape=(tm, tn), comm=comm)


def mm_nn(name, a, w, out_dtype, exact=False, comm=None):
    M, K = a.shape
    N = w.shape[1]
    tm, tn = _fit(M, N, lambda m, n: 2 * (m * K * _sz(a.dtype) + K * n * _sz(w.dtype) + m * n * _sz(out_dtype)))
    return _mm(name, a, w, dims=NN, grid=(M // tm, N // tn, 1),
               a_spec=pl.BlockSpec((tm, K), lambda i, j, k: (i, 0)),
               b_spec=pl.BlockSpec((K, tn), lambda i, j, k: (0, j)),
               o_spec=pl.BlockSpec((tm, tn), lambda i, j, k: (i, j)),
               out_shape=(M, N), out_dtype=out_dtype, acc_shape=(tm, tn), exact=exact, comm=comm)


def mm_nt(name, dy, w, out_dtype, exact=False, comm=None):
    M, N = dy.shape
    K = w.shape[0]
    tm, tn = _fit(M, K, lambda m, n: 2 * (m * N * _sz(dy.dtype) + n * N * _sz(w.dtype) + m * n * _sz(out_dtype)))
    return _mm(name, dy, w, dims=NT, grid=(M // tm, K // tn, 1),
               a_spec=pl.BlockSpec((tm, N), lambda i, j, k: (i, 0)),
               b_spec=pl.BlockSpec((tn, N), lambda i, j, k: (j, 0)),
               o_spec=pl.BlockSpec((tm, tn), lambda i, j, k: (i, j)),
               out_shape=(M, K), out_dtype=out_dtype, acc_shape=(tm, tn), exact=exact, comm=comm)


def mm_nt_pieces(name, dy, w, out_dtype, halves=1, comm=None):
    P, K, Nq = w.shape
    M = dy.shape[-2]
    tm, tn = _fit(M, K, lambda m, n: 2 * (m * Nq * _sz(dy.dtype) + n * Nq * _sz(w.dtype) + m * n * _sz(out_dtype)) + 4 * m * n)
    pph = P // halves
    if halves == 1:
        a_spec = pl.BlockSpec((tm, Nq), lambda i, j, k: (i, k))
    else:
        a_spec = pl.BlockSpec((None, tm, Nq), lambda i, j, k: (k // pph, i, k % pph))
    return _mm(name, dy, w, dims=NT, grid=(M // tm, K // tn, P),
               a_spec=a_spec,
               b_spec=pl.BlockSpec((None, tn, Nq), lambda i, j, k: (k, j, 0)),
               o_spec=pl.BlockSpec((tm, tn), lambda i, j, k: (i, j)),
               out_shape=(M, K), out_dtype=out_dtype, acc_shape=(tm, tn), comm=comm)


def mm_nt_list(name, dys, w, out_dtype):
    P, K, Nq = w.shape
    M = dys[0].shape[0]
    assert len(dys) == P
    tm, tn = _fit(M, K, lambda m, n: 2 * (sum(m * Nq * _sz(d.dtype) for d in dys) + n * Nq * _sz(w.dtype)
                                          + m * n * _sz(out_dtype)) + 4 * m * n)

    def body(*refs):
        d_refs, w_ref, o_ref, acc = refs[:P], refs[P], refs[P + 1], refs[P + 2]
        kk = pl.program_id(2)
        for q in range(P):
            @pl.when(kk == q)
            def _(q=q):
                part = lax.dot_general(d_refs[q][...].astype(BF16), w_ref[...], NT, preferred_element_type=F32)
                acc[...] = part if q == 0 else acc[...] + part

        @pl.when(kk == P - 1)
        def _():
            o_ref[...] = acc[...].astype(o_ref.dtype)

    return pl.pallas_call(
        body, name=name, grid=(M // tm, K // tn, P),
        in_specs=[*[pl.BlockSpec((tm, Nq), lambda i, j, k: (i, 0)) for _ in range(P)],
                  pl.BlockSpec((None, tn, Nq), lambda i, j, k: (k, j, 0))],
        out_specs=pl.BlockSpec((tm, tn), lambda i, j, k: (i, j)),
        out_shape=jax.ShapeDtypeStruct((M, K), out_dtype),
        scratch_shapes=[pltpu.VMEM((tm, tn), F32)],
        compiler_params=_params("parallel", "arbitrary", "arbitrary"),
    )(*dys, w)


def mm_tn(name, a, dy, out_dtype, comm=None):
    M, K = a.shape
    N = dy.shape[1]
    tm, tn = _fit(K, N, lambda m, n: 2 * (M * m * _sz(a.dtype) + M * n * _sz(dy.dtype) + m * n * _sz(out_dtype)), LANE)
    return _mm(name, a, dy, dims=TN, grid=(K // tm, N // tn, 1),
               a_spec=pl.BlockSpec((M, tm), lambda i, j, k: (0, i)),
               b_spec=pl.BlockSpec((M, tn), lambda i, j, k: (0, j)),
               o_spec=pl.BlockSpec((tm, tn), lambda i, j, k: (i, j)),
               out_shape=(K, N), out_dtype=out_dtype, acc_shape=(tm, tn), comm=comm)


def mm_tn_pieces(name, a, dy, n_p, out_dtype, halves=1, comm=None):
    M, K = a.shape
    Nq = (dy.shape[-1] * halves) // n_p
    tm, tn = _fit(K, Nq, lambda m, n: 2 * (M * m * _sz(a.dtype) + M * n * _sz(dy.dtype) + m * n * _sz(out_dtype)), LANE)
    tpp = Nq // tn
    pph = n_p // halves
    if halves == 1:
        b_spec = pl.BlockSpec((M, tn), lambda i, j, k: (0, j))
    else:
        b_spec = pl.BlockSpec((None, M, tn), lambda i, j, k: ((j // tpp) // pph, 0, ((j // tpp) % pph) * tpp + j % tpp))
    return _mm(name, a, dy, dims=TN, grid=(K // tm, n_p * tpp, 1),
               a_spec=pl.BlockSpec((M, tm), lambda i, j, k: (0, i)),
               b_spec=b_spec,
               o_spec=pl.BlockSpec((None, tm, tn), lambda i, j, k: (j // tpp, i, j % tpp)),
               out_shape=(n_p, K, Nq), out_dtype=out_dtype, acc_shape=(tm, tn), comm=comm)


def _row_call(name, body, ins, in_kinds, outs, rows, tr, scratch=()):
    def spec(kind, shape):
        if isinstance(kind, pl.BlockSpec):
            return kind
        if kind == "row":
            return pl.BlockSpec((tr,) + tuple(shape[1:]), lambda i: (i,) + (0,) * (len(shape) - 1))
        return pl.BlockSpec(tuple(shape), lambda i: (0,) * len(shape))

    return pl.pallas_call(
        body, name=name, grid=(rows // tr,),
        in_specs=[spec(k, a.shape) for k, a in zip(in_kinds, ins)],
        out_specs=[spec(k, s) for k, s, _ in outs],
        out_shape=[jax.ShapeDtypeStruct(s, d) for _, s, d in outs],
        scratch_shapes=list(scratch),
        compiler_params=_params("arbitrary"),
    )(*ins)


def _acc(ref, val):
    @pl.when(pl.program_id(0) == 0)
    def _():
        ref[...] = val

    @pl.when(pl.program_id(0) > 0)
    def _():
        ref[...] += val


def norm_mod_fwd(name, x, w, b, tr=256):
    rows, d = x.shape
    tr = _pick(rows, tr, SUB)

    def body(x_ref, w_ref, b_ref, h_ref):
        xv = x_ref[...]
        r = lax.rsqrt(jnp.mean(xv * xv, axis=-1, keepdims=True) + EPS)
        h_ref[...] = (xv * r * w_ref[...] + b_ref[...]).astype(BF16)

    return _row_call(name, body, [x, w.reshape(1, d), b.reshape(1, d)], ["row", "vec", "vec"],
                     [("row", (rows, d), BF16)], rows, tr)[0]


def norm_mod_bwd(name, x, dh, w, dx_in, tr=256):
    rows, d = x.shape
    tr = _pick(rows, tr, SUB)

    def body(x_ref, dh_ref, w_ref, dxi_ref, dx_ref, cs1_ref, cs2_ref):
        xv = x_ref[...]
        r = lax.rsqrt(jnp.mean(xv * xv, axis=-1, keepdims=True) + EPS)
        xn = xv * r
        dhv = dh_ref[...].astype(F32)
        dxn = dhv * w_ref[...]
        dx_ref[...] = dxi_ref[...] + r * (dxn - xn * jnp.mean(dxn * xn, axis=-1, keepdims=True))
        _acc(cs1_ref, jnp.sum(dhv, axis=0, keepdims=True))
        _acc(cs2_ref, jnp.sum(dhv * xn, axis=0, keepdims=True))

    return _row_call(name, body, [x, dh, w.reshape(1, d), dx_in], ["row", "row", "vec", "row"],
                     [("row", (rows, d), F32), ("acc", (1, d), F32), ("acc", (1, d), F32)], rows, tr)


def gate_res_fwd(name, x, y, gate, tr=256):
    rows, d = x.shape
    tr = _pick(rows, tr, SUB)

    def body(x_ref, y_ref, g_ref, o_ref):
        o_ref[...] = x_ref[...] + g_ref[...] * y_ref[...].astype(F32)

    return _row_call(name, body, [x, y, gate.reshape(1, d)], ["row", "row", "vec"],
                     [("row", (rows, d), F32)], rows, tr)[0]


def res_norm_fwd(name, x, y, gate, w, b, tr=256):
    rows, d = x.shape
    tr = _pick(rows, tr, SUB)

    def body(x_ref, y_ref, g_ref, w_ref, b_ref, o_ref, h_ref):
        xv = x_ref[...] + g_ref[...] * y_ref[...].astype(F32)
        o_ref[...] = xv
        r = lax.rsqrt(jnp.mean(xv * xv, axis=-1, keepdims=True) + EPS)
        h_ref[...] = (xv * r * w_ref[...] + b_ref[...]).astype(BF16)

    return _row_call(name, body, [x, y, gate.reshape(1, d), w.reshape(1, d), b.reshape(1, d)],
                     ["row", "row", "vec", "vec", "vec"], [("row", (rows, d), F32), ("row", (rows, d), BF16)], rows, tr)


def gate_res_bwd(name, dx, y, gate, tr=256):
    rows, d = dx.shape
    tr = _pick(rows, tr, SUB)

    def body(dx_ref, y_ref, g_ref, dy_ref, dg_ref):
        dxv = dx_ref[...]
        dy_ref[...] = (g_ref[...] * dxv).astype(BF16)
        _acc(dg_ref, jnp.sum(dxv * y_ref[...].astype(F32), axis=0, keepdims=True))

    return _row_call(name, body, [dx, y, gate.reshape(1, d)], ["row", "row", "vec"],
                     [("row", (rows, d), BF16), ("acc", (1, d), F32)], rows, tr)


def loss_head(name, x, y, gate, g, target, tr=256):
    rows, d = x.shape
    tr = _pick(rows, tr, SUB)

    def body(x_ref, y_ref, gt_ref, g_ref, t_ref, dx_ref, dg_ref, loss_ref):
        xv = x_ref[...] + gt_ref[...] * y_ref[...].astype(F32)
        r = lax.rsqrt(jnp.mean(xv * xv, axis=-1, keepdims=True) + EPS)
        xn = xv * r
        err = xn * g_ref[...] - t_ref[...]
        dy = err * (1.0 / d)
        dxn = dy * g_ref[...]
        dx_ref[...] = r * (dxn - xn * jnp.mean(dxn * xn, axis=-1, keepdims=True))
        _acc(dg_ref, jnp.sum(dy * xn, axis=0, keepdims=True))
        part = 0.5 * jnp.sum(jnp.sum(err * err, axis=-1, keepdims=True) * (1.0 / d), axis=0, keepdims=True)
        _acc(loss_ref, jnp.broadcast_to(part, (1, LANE)))

    return _row_call(name, body, [x, y, gate.reshape(1, d), g.reshape(1, d), target], ["row", "row", "vec", "vec", "row"],
                     [("row", (rows, d), F32), ("acc", (1, d), F32), ("acc", (1, LANE), F32)], rows, tr)


def fma3(name, a, b, c, dvec, out_dtype, tr=256):
    rows, d = a.shape
    tr = _pick(rows, tr, SUB)

    def body(a_ref, b_ref, c_ref, d_ref, o_ref):
        o_ref[...] = (d_ref[...] * a_ref[...] + b_ref[...] + c_ref[...]).astype(o_ref.dtype)

    return _row_call(name, body, [a, b, c, dvec.reshape(1, d)], ["row", "row", "row", "vec"],
                     [("row", (rows, d), out_dtype)], rows, tr)[0]


def sum_lead(name, a, out_dtype, tr=512):
    n, rows, cols = a.shape
    tr = _pick(rows, tr, 16)

    def body(a_ref, o_ref):
        acc = a_ref[0].astype(F32)
        for s in range(1, n):
            acc = acc + a_ref[s].astype(F32)
        o_ref[...] = acc.astype(o_ref.dtype)

    return pl.pallas_call(
        body, name=name, grid=(rows // tr,),
        in_specs=[pl.BlockSpec((n, tr, cols), lambda i: (0, i, 0))],
        out_specs=pl.BlockSpec((tr, cols), lambda i: (i, 0)),
        out_shape=jax.ShapeDtypeStruct((rows, cols), out_dtype),
        compiler_params=_params("parallel"),
    )(a)


def adamw(name, w, g, m, v, comm=None):
    shape = w.shape
    cols = shape[-1]
    w2, g2, m2, v2 = (t.reshape(-1, cols) for t in (w, g, m, v))
    rows = w2.shape[0]
    tr, tc = _pick(rows, 256, SUB), _pick(cols, 1536)
    c1 = 1.0 - ADAM_B1 ** ADAM_STEP
    c2 = 1.0 - ADAM_B2 ** ADAM_STEP

    def body(w_ref, g_ref, m_ref, v_ref, d_ref, mo_ref, vo_ref):
        gv = g_ref[...]
        mn = ADAM_B1 * m_ref[...] + (1.0 - ADAM_B1) * gv
        vn = ADAM_B2 * v_ref[...] + (1.0 - ADAM_B2) * (gv * gv)
        mo_ref[...] = mn
        vo_ref[...] = vn
        d_ref[...] = -ADAM_LR * ((mn / c1) / (jnp.sqrt(vn / c2) + ADAM_EPS) + ADAM_WD * w_ref[...])

    blk = pl.BlockSpec((tr, tc), lambda i, j: (i, j))
    outs, couts = _call(body, [w2, g2, m2, v2], name=name, grid=(rows // tr, cols // tc), in_specs=[blk] * 4,
                        out_specs=[blk] * 3, out_shape=[jax.ShapeDtypeStruct(w2.shape, F32)] * 3,
                        sem=("parallel", "parallel"), comm=comm)
    outs = tuple(o.reshape(shape) for o in outs)
    return outs if comm is None else (outs, couts)


def add_half(name, grad, got, c_idx, tr=256):
    Pn, R, C = grad.shape
    hr = R // 2
    tr = _pick(hr, tr, HALO)
    nb = hr // tr

    def body(c_ref, a_ref, b_ref, o_ref):
        o_ref[...] = (a_ref[...].astype(F32) + b_ref[...].astype(F32)).astype(o_ref.dtype)

    return pl.pallas_call(
        body, name=name,
        grid_spec=pltpu.PrefetchScalarGridSpec(
            num_scalar_prefetch=1, grid=(Pn, nb),
            in_specs=[pl.BlockSpec((None, tr, C), lambda q, i, c: (q, c[0] * nb + i, 0)),
                      pl.BlockSpec((None, tr, C), lambda q, i, c: (q, i, 0))],
            out_specs=pl.BlockSpec((None, tr, C), lambda q, i, c: (q, i, 0))),
        out_shape=jax.ShapeDtypeStruct((Pn, hr, C), BF16),
        compiler_params=_params("parallel", "parallel"),
    )(c_idx, grad, got)


def pair_sum_to_slot(name, buf, got, ids, tr=256):
    R, C = buf.shape
    hr = R // 2
    tr = _pick(hr, tr, SUB)
    nb = hr // tr

    def body(ids_ref, a_ref, b_ref, o_ref):
        o_ref[...] = a_ref[...] + b_ref[...]

    return pl.pallas_call(
        body, name=name,
        grid_spec=pltpu.PrefetchScalarGridSpec(
            num_scalar_prefetch=1, grid=(nb,),
            in_specs=[pl.BlockSpec((tr, C), lambda i, ids: (ids[1] * nb + i, 0)),
                      pl.BlockSpec((tr, C), lambda i, ids: (i, 0))],
            out_specs=pl.BlockSpec((None, tr, C), lambda i, ids: (ids[0], i, 0))),
        out_shape=jax.ShapeDtypeStruct((N_CHIP, hr, C), F32),
        compiler_params=_params("parallel"),
    )(ids, buf, got)


def sum_chips_to_half(name, slots, ids, tr=256):
    n, hr, C = slots.shape
    tr = _pick(hr, tr, SUB)

    def body(ids_ref, s_ref, o_ref):
        acc = s_ref[0]
        for q in range(1, n):
            acc = acc + s_ref[q]
        o_ref[...] = acc

    return pl.pallas_call(
        body, name=name,
        grid_spec=pltpu.PrefetchScalarGridSpec(
            num_scalar_prefetch=1, grid=(hr // tr,),
            in_specs=[pl.BlockSpec((n, tr, C), lambda i, ids: (0, i, 0))],
            out_specs=pl.BlockSpec((None, tr, C), lambda i, ids: (ids[1], i, 0))),
        out_shape=jax.ShapeDtypeStruct((2, hr, C), F32),
        compiler_params=_params("parallel"),
    )(ids, slots)


def sum_slots(name, slots, mine, ids, tr=256):
    Pn, hr, C = slots.shape
    tr = _pick(hr, tr, HALO)

    def body(ids_ref, m_ref, s1_ref, s2_ref, s3_ref, o_ref):
        o_ref[...] = (m_ref[...].astype(F32) + s1_ref[...].astype(F32)) + (s2_ref[...].astype(F32) + s3_ref[...].astype(F32))

    def other(k):
        return pl.BlockSpec((None, tr, C), lambda i, ids: ((ids[0] + k) % Pn, i, 0))

    return pl.pallas_call(
        body, name=name,
        grid_spec=pltpu.PrefetchScalarGridSpec(
            num_scalar_prefetch=1, grid=(hr // tr,),
            in_specs=[pl.BlockSpec((None, tr, C), lambda i, ids: (ids[0], i, 0)), other(1), other(2), other(3)],
            out_specs=pl.BlockSpec((None, tr, C), lambda i, ids: (ids[1], i, 0))),
        out_shape=jax.ShapeDtypeStruct((2, hr, C), F32),
        compiler_params=_params("parallel"),
    )(ids, mine, slots, slots, slots)


HALO = 16


def _halo_specs(lead, R, tn, n_rows, col_of):
    nb, nblk = R // HALO, n_rows // HALO

    def mk(rows, row_of):
        return pl.BlockSpec((lead, rows, tn), lambda *g: (0, row_of(g[-1]), col_of(g)))

    return (mk(HALO, lambda i: jnp.maximum(i * nb - 1, 0)), mk(R, lambda i: i),
            mk(HALO, lambda i: jnp.minimum((i + 1) * nb, nblk - 1)))


def _fill_halo(dst, i, last, R, prev, cur, nxt):
    nd = len(dst.shape)
    lead = (slice(None),) * (nd - 2)
    dst[lead + (slice(0, HALO), slice(None))] = jnp.where(i == 0, 0.0, prev)
    dst[lead + (slice(HALO, HALO + R), slice(None))] = cur
    dst[lead + (slice(HALO + R, HALO + R + HALO), slice(None))] = jnp.where(i == last, 0.0, nxt)


def _shift_mats(n):
    i = np.arange(n)
    return jnp.asarray(np.stack([i[:, None] - 1 == i[None, :], i[:, None] + 1 == i[None, :]]), BF16)


def _shifted(s_ref, xb):
    return (jnp.dot(s_ref[0], xb, preferred_element_type=F32), jnp.dot(s_ref[1], xb, preferred_element_type=F32))


def ffn_mid_fwd(name, up3, cw, cb, R=256, tn=512, comm=None):
    _, L, Fd = up3.shape
    R, tn = _pick(L, R, HALO), _pick(Fd, tn)
    nrow = L // R

    def body(p_ref, c_ref, n_ref, w_ref, b_ref, s_ref, act_ref):
        i = pl.program_id(1)
        row = lax.broadcasted_iota(jnp.int32, (R, tn), 0)
        cv = []
        for z in range(2):
            xb = c_ref[z]
            before = jnp.where(i == 0, 0.0, p_ref[z].astype(F32)[HALO - 1:HALO])
            after = jnp.where(i == nrow - 1, 0.0, n_ref[z].astype(F32)[0:1])
            dn, up = _shifted(s_ref, xb)
            dn = jnp.where(row == 0, before, dn)
            up = jnp.where(row == R - 1, after, up)
            cv.append(b_ref[z] + w_ref[z, 0:1, :] * dn + w_ref[z, 1:2, :] * xb.astype(F32) + w_ref[z, 2:3, :] * up)
        u, g = cv
        act_ref[...] = (u * g * _sigmoid(g)).astype(BF16)

    hs = _halo_specs(2, R, tn, L, lambda g: g[0])
    outs, couts = _call(
        body, [up3, up3, up3, cw, cb, _shift_mats(R)], name=name, grid=(Fd // tn, nrow),
        in_specs=[*hs, pl.BlockSpec((2, 3, tn), lambda j, i: (0, 0, j)), pl.BlockSpec((2, 1, tn), lambda j, i: (0, 0, j)),
                  pl.BlockSpec((2, R, R), lambda j, i: (0, 0, 0))],
        out_specs=[pl.BlockSpec((R, tn), lambda j, i: (i, j))],
        out_shape=[jax.ShapeDtypeStruct((L, Fd), BF16)], sem=("parallel", "arbitrary"), comm=comm)
    return outs[0] if comm is None else (outs[0], couts)


def ffn_mid_bwd(name, up3, dact, cw, cb, R=256, tn=512, comm=None):
    _, L, Fd = up3.shape
    R, tn = _pick(L, R, HALO), _pick(Fd, tn)
    nrow = L // R

    def gate_grads(u, g, d):
        sg = _sigmoid(g)
        return d * g * sg, d * u * sg * (1.0 + g * (1.0 - sg))

    def body(pu, cu, nu, pd, cd, nd, w_ref, b_ref, s_ref, dup_ref, dcw_ref, dcb_ref):
        i = pl.program_id(1)
        first, last = i == 0, i == nrow - 1
        row = lax.broadcasted_iota(jnp.int32, (R, tn), 0)
        cv, cv_b, cv_a, taps = [], [], [], []
        for z in range(2):
            xb = cu[z]
            xf = xb.astype(F32)
            pf = jnp.where(first, 0.0, pu[z].astype(F32))
            nf = jnp.where(last, 0.0, nu[z].astype(F32))
            xm2, xm1, xp0, xp1 = pf[HALO - 2:HALO - 1], pf[HALO - 1:HALO], nf[0:1], nf[1:2]
            dn, up = _shifted(s_ref, xb)
            dn = jnp.where(row == 0, xm1, dn)
            up = jnp.where(row == R - 1, xp0, up)
            w0, w1, w2, b = w_ref[z, 0:1, :], w_ref[z, 1:2, :], w_ref[z, 2:3, :], b_ref[z]
            cv.append(b + w0 * dn + w1 * xf + w2 * up)
            cv_b.append(b + w0 * xm2 + w1 * xm1 + w2 * xf[0:1])
            cv_a.append(b + w0 * xf[R - 1:R] + w1 * xp0 + w2 * xp1)
            taps.append((dn, xf, up))
        dcs = gate_grads(cv[0], cv[1], cd[0].astype(F32))
        dcs_b = gate_grads(cv_b[0], cv_b[1], jnp.where(first, 0.0, pd[0].astype(F32)[HALO - 1:HALO]))
        dcs_a = gate_grads(cv_a[0], cv_a[1], jnp.where(last, 0.0, nd[0].astype(F32)[0:1]))

        @pl.when(first)
        def _():
            dcw_ref[...] = jnp.zeros_like(dcw_ref)
            dcb_ref[...] = jnp.zeros_like(dcb_ref)

        for z in range(2):
            dc = dcs[z]
            dc_dn, dc_up = _shifted(s_ref, dc.astype(BF16))
            dc_dn = jnp.where(row == 0, dcs_b[z], dc_dn)
            dc_up = jnp.where(row == R - 1, dcs_a[z], dc_up)
            dup_ref[z] = (w_ref[z, 0:1, :] * dc_up + w_ref[z, 1:2, :] * dc + w_ref[z, 2:3, :] * dc_dn).astype(BF16)
            dcb_ref[z] += jnp.sum(dc, axis=0, keepdims=True)
            for k in range(3):
                dcw_ref[z, k:k + 1, :] += jnp.sum(dc * taps[z][k], axis=0, keepdims=True)

    hu = _halo_specs(2, R, tn, L, lambda g: g[0])
    hd = _halo_specs(1, R, tn, L, lambda g: g[0])
    outs, couts = _call(
        body, [up3, up3, up3, dact[None], dact[None], dact[None], cw, cb, _shift_mats(R)], name=name, grid=(Fd // tn, nrow),
        in_specs=[*hu, *hd, pl.BlockSpec((2, 3, tn), lambda j, i: (0, 0, j)), pl.BlockSpec((2, 1, tn), lambda j, i: (0, 0, j)),
                  pl.BlockSpec((2, R, R), lambda j, i: (0, 0, 0))],
        out_specs=[pl.BlockSpec((2, R, tn), lambda j, i: (0, i, j)), pl.BlockSpec((2, 3, tn), lambda j, i: (0, 0, j)),
                   pl.BlockSpec((2, 1, tn), lambda j, i: (0, 0, j))],
        out_shape=[jax.ShapeDtypeStruct((2, L, Fd), BF16), jax.ShapeDtypeStruct((2, 3, Fd), F32),
                   jax.ShapeDtypeStruct((2, 1, Fd), F32)],
        sem=("parallel", "arbitrary"), comm=comm)
    return outs if comm is None else (outs, couts)


def _glu_z0(blk):
    return blk[0].astype(F32) * _sigmoid(blk[1].astype(F32))


def _sublane_copies(ref, cs):
    n = ref.shape[1]
    blk = ref[0, :, cs]
    for b in range(1, SUB):
        ref[b, :, cs] = pltpu.roll(blk, n - b, 0)


def _tap(ref, offset, rows, cs):
    return ref[offset % SUB, pl.ds(offset - offset % SUB, rows), cs]


def conf_mid_fwd(name, ag3, dw_w, dw_b, ln_g, ln_b, R=128, cb=256):
    _, L, C = ag3.shape
    K = dw_w.shape[0]
    pad = (K - 1) // 2
    assert pad <= HALO
    R, cb = _pick(L, R, HALO), _pick(C, cb)
    nrow = L // R

    def body(p_ref, c_ref, n_ref, w_ref, b_ref, g_ref, bb_ref, z1_ref, z3_ref, s_ref):
        i = pl.program_id(0)
        _fill_halo(s_ref.at[0], i, nrow - 1, R, _glu_z0(p_ref), _glu_z0(c_ref), _glu_z0(n_ref))
        for c0 in range(0, C, cb):
            cs = slice(c0, c0 + cb)
            _sublane_copies(s_ref, cs)
            acc = jnp.broadcast_to(b_ref[:, cs], (R, cb))
            for k in range(K):
                acc = acc + w_ref[k:k + 1, cs] * _tap(s_ref, HALO - pad + k, R, cs)
            z1_ref[:, cs] = acc
        z1 = z1_ref[...]
        zc = z1 - jnp.mean(z1, axis=-1, keepdims=True)
        zn = zc * lax.rsqrt(jnp.mean(zc * zc, axis=-1, keepdims=True) + EPS)
        z2 = zn * g_ref[...] + bb_ref[...]
        z3_ref[...] = (z2 * _sigmoid(z2)).astype(BF16)

    hs = _halo_specs(2, R, C, L, lambda g: 0)
    vec = pl.BlockSpec((1, C), lambda i: (0, 0))
    return pl.pallas_call(
        body, name=name, grid=(nrow,),
        in_specs=[*hs, pl.BlockSpec((K, C), lambda i: (0, 0)), vec, vec, vec],
        out_specs=[pl.BlockSpec((R, C), lambda i: (i, 0)), pl.BlockSpec((R, C), lambda i: (i, 0))],
        out_shape=[jax.ShapeDtypeStruct((L, C), F32), jax.ShapeDtypeStruct((L, C), BF16)],
        scratch_shapes=[pltpu.VMEM((SUB, R + 2 * HALO, C), F32)],
        compiler_params=_params("parallel"),
    )(ag3, ag3, ag3, dw_w, dw_b.reshape(1, C), ln_g.reshape(1, C), ln_b.reshape(1, C))


def conf_ln_bwd(name, z1, dz3, ln_g, ln_b, tr=256):
    rows, C = z1.shape
    tr = _pick(rows, tr, HALO)

    def body(z_ref, d_ref, g_ref, b_ref, dz_ref, dg_ref, db_ref):
        z1v = z_ref[...]
        zc = z1v - jnp.mean(z1v, axis=-1, keepdims=True)
        rs = lax.rsqrt(jnp.mean(zc * zc, axis=-1, keepdims=True) + EPS)
        zn = zc * rs
        z2 = zn * g_ref[...] + b_ref[...]
        sg = _sigmoid(z2)
        dz2 = d_ref[...].astype(F32) * sg * (1.0 + z2 * (1.0 - sg))
        _acc(dg_ref, jnp.sum(dz2 * zn, axis=0, keepdims=True))
        _acc(db_ref, jnp.sum(dz2, axis=0, keepdims=True))
        dzn = dz2 * g_ref[...]
        dz1 = rs * (dzn - jnp.mean(dzn, axis=-1, keepdims=True) - zn * jnp.mean(dzn * zn, axis=-1, keepdims=True))
        dz_ref[...] = dz1.astype(BF16)

    return _row_call(name, body, [z1, dz3, ln_g.reshape(1, C), ln_b.reshape(1, C)], ["row", "row", "vec", "vec"],
                     [("row", (rows, C), BF16), ("acc", (1, C), F32), ("acc", (1, C), F32)], rows, tr)


def conf_conv_bwd(name, ag3, dz1, dw_w, R=128, cb=256, comm=None):
    _, L, C = ag3.shape
    K = dw_w.shape[0]
    pad = (K - 1) // 2
    R, cb = _pick(L, R, HALO), _pick(C, cb)
    nrow = L // R

    def body(pa, ca, na, pd, cd, nd, w_ref, dag_ref, dw_ref, db_ref, s_ref, d_ref, z_ref):
        i = pl.program_id(0)
        _fill_halo(s_ref.at[0], i, nrow - 1, R, _glu_z0(pa), _glu_z0(ca), _glu_z0(na))
        _fill_halo(d_ref.at[0], i, nrow - 1, R, pd[0].astype(F32), cd[0].astype(F32), nd[0].astype(F32))

        @pl.when(i == 0)
        def _():
            dw_ref[...] = jnp.zeros_like(dw_ref)
            db_ref[...] = jnp.zeros_like(db_ref)

        for c0 in range(0, C, cb):
            cs = slice(c0, c0 + cb)
            _sublane_copies(s_ref, cs)
            _sublane_copies(d_ref, cs)
            dcur = d_ref[0, pl.ds(HALO, R), cs]
            acc = jnp.zeros((R, cb), F32)
            for k in range(K):
                acc = acc + w_ref[k:k + 1, cs] * _tap(d_ref, HALO + pad - k, R, cs)
                dw_ref[k:k + 1, cs] += jnp.sum(dcur * _tap(s_ref, HALO - pad + k, R, cs), axis=0, keepdims=True)
            z_ref[:, cs] = acc
            db_ref[:, cs] += jnp.sum(dcur, axis=0, keepdims=True)
        dz0 = z_ref[...]
        a = ca[0].astype(F32)
        sg = _sigmoid(ca[1].astype(F32))
        dag_ref[0] = (dz0 * sg).astype(BF16)
        dag_ref[1] = (dz0 * a * sg * (1.0 - sg)).astype(BF16)

    ha = _halo_specs(2, R, C, L, lambda g: 0)
    hd = _halo_specs(1, R, C, L, lambda g: 0)
    outs, couts = _call(
        body, [ag3, ag3, ag3, dz1[None], dz1[None], dz1[None], dw_w], name=name, grid=(nrow,),
        in_specs=[*ha, *hd, pl.BlockSpec((K, C), lambda i: (0, 0))],
        out_specs=[pl.BlockSpec((2, R, C), lambda i: (0, i, 0)), pl.BlockSpec((K, C), lambda i: (0, 0)),
                   pl.BlockSpec((1, C), lambda i: (0, 0))],
        out_shape=[jax.ShapeDtypeStruct((2, L, C), BF16), jax.ShapeDtypeStruct((K, C), F32),
                   jax.ShapeDtypeStruct((1, C), F32)],
        scratch_shapes=[pltpu.VMEM((SUB, R + 2 * HALO, C), F32), pltpu.VMEM((SUB, R + 2 * HALO, C), F32),
                        pltpu.VMEM((R, C), F32)],
        sem=("arbitrary",), comm=comm)
    return outs if comm is None else (outs, couts)


_GELU_C = math.sqrt(2.0 / math.pi)


def _gelu(x):
    return 0.5 * x * (1.0 + jnp.tanh(_GELU_C * (x + 0.044715 * x * x * x)))


def _gelu_grad(x):
    t = jnp.tanh(_GELU_C * (x + 0.044715 * x * x * x))
    return 0.5 * (1.0 + t) + 0.5 * x * (1.0 - t * t) * _GELU_C * (1.0 + 3.0 * 0.044715 * x * x)


def glu_fwd(name, u, y0, y1, d, wg, tr=512):
    rows, W = u.shape
    tr = _pick(rows, tr, HALO)

    def body(u_ref, y0_ref, y1_ref, d_ref, w_ref, o_ref):
        z = _gelu(d_ref[...] * u_ref[...] + y0_ref[...] + y1_ref[...])
        zz = jnp.dot(z.astype(BF16), w_ref[...], preferred_element_type=F32)
        o_ref[...] = (z * _sigmoid(zz)).astype(BF16)

    return _row_call(name, body, [u, y0, y1, d.reshape(1, W), wg], ["row", "row", "row", "vec", "vec"],
                     [("row", (rows, W), BF16)], rows, tr)[0]


def glu_bwd(name, u, y0, y1, d, wg, dmix, tr=512):
    rows, W = u.shape
    tr = _pick(rows, tr, HALO)

    def body(u_ref, y0_ref, y1_ref, d_ref, w_ref, do_ref, dy_ref, z_ref, dzz_ref, dd_ref):
        uv = u_ref[...]
        y = d_ref[...] * uv + y0_ref[...] + y1_ref[...]
        z = _gelu(y)
        zz = jnp.dot(z.astype(BF16), w_ref[...], preferred_element_type=F32)
        sg = _sigmoid(zz)
        do = do_ref[...].astype(F32)
        dzz = (do * z * sg * (1.0 - sg)).astype(BF16)
        dz = do * sg + lax.dot_general(dzz, w_ref[...], NT, preferred_element_type=F32)
        dy = dz * _gelu_grad(y)
        dy_ref[...] = dy
        z_ref[...] = z.astype(BF16)
        dzz_ref[...] = dzz
        _acc(dd_ref, jnp.sum(dy * uv, axis=0, keepdims=True))

    do_spec = pl.BlockSpec((tr, W), lambda i: (i, 0))
    return _row_call(name, body, [u, y0, y1, d.reshape(1, W), wg, dmix], ["row", "row", "row", "vec", "vec", do_spec],
                     [("row", (rows, W), F32), ("row", (rows, W), BF16), ("row", (rows, W), BF16), ("acc", (1, W), F32)],
                     rows, tr)


NA_KEYS = NA_WIN_R * GRID_W


NA_PAIRS = NA_WIN_R // 2


def na_bias(rpb):
    H, nr, nc = rpb.shape
    e, ok = _na_col_select()
    rp = jnp.pad(rpb.reshape(H * nr, nc), ((0, (-H * nr) % SUB), (0, LANE - nc)))
    cols = mm_nn("na_bias_mm", rp, jnp.asarray(e, F32), F32, exact=True)[:H * nr]
    tiles = (cols + jnp.asarray(np.where(ok, 0.0, NEG), F32)).reshape(H, nr, GRID_W, GRID_W)
    return jnp.concatenate([tiles[:, :-1], tiles[:, 1:]], axis=-1)


def na_bias_grad(db2):
    H, n2 = db2.shape[:2]
    left, right = db2[..., :GRID_W], db2[..., GRID_W:]
    tiles = jnp.pad(left, ((0, 0), (0, 1), (0, 0), (0, 0))) + jnp.pad(right, ((0, 0), (1, 0), (0, 0), (0, 0)))
    flat = tiles.reshape(H * (n2 + 1), GRID_W * GRID_W)
    flat = jnp.pad(flat, ((0, (-flat.shape[0]) % SUB), (0, 0)))
    dcol = mm_nt("na_bias_fold", flat, na_bias_fold_matrix(), F32, exact=True)
    return dcol[:H * (n2 + 1), :2 * NA_WIN_C - 1].reshape(H, n2 + 1, 2 * NA_WIN_C - 1)


def _na_col_select():
    q = np.arange(GRID_W)
    cs = np.clip(q - NA_WIN_C // 2, 0, GRID_W - NA_WIN_C)
    ok = ((q[None, :] >= cs[:, None]) & (q[None, :] < cs[:, None] + NA_WIN_C)).reshape(-1)
    cidx = np.clip(q[None, :] - q[:, None] + (NA_WIN_C - 1), 0, 2 * NA_WIN_C - 2).reshape(-1)
    return (cidx[None, :] == np.arange(LANE)[:, None]) & ok[None, :], ok


def na_bias_fold_matrix():
    return jnp.asarray(_na_col_select()[0], F32)


def _na_window(r, rows):
    kr0 = jnp.clip(r - NA_WIN_R // 2, 0, rows - NA_WIN_R)
    return pl.multiple_of(kr0 * GRID_W, GRID_W), r - kr0


def _na_dims(qkv, kvc):
    L = qkv.shape[0]
    NA = qkv.shape[1] // 3
    H = NA // NA_HEAD_DIM
    hp = 2 if H % 2 == 0 else 1
    return L, NA, H, hp, H // hp, L // GRID_W, kvc.shape[0]


def _na_bias_tile(b_ref, hh, off):
    return jnp.concatenate([b_ref[hh, NA_WIN_R - 1 - off + 2 * j] for j in range(NA_PAIRS)], axis=-1)


def natten_fwd(name, qkv, kvc, bias, comm=None):
    L, NA, H, hp, G, rows, Lc = _na_dims(qkv, kvc)
    scale = NA_HEAD_DIM ** -0.5
    wd = hp * NA_HEAD_DIM

    def body(q_ref, k_ref, v_ref, kc_ref, vc_ref, b_ref, o_ref, lse_ref):
        st, off = _na_window(pl.program_id(1), rows)
        for hh in range(hp):
            sl = slice(hh * NA_HEAD_DIM, (hh + 1) * NA_HEAD_DIM)
            q = q_ref[:, sl]
            s_loc = (lax.dot_general(q, k_ref[pl.ds(st, NA_KEYS), sl], NT, preferred_element_type=F32) * scale
                     + _na_bias_tile(b_ref, hh, off))
            s_ctx = lax.dot_general(q, kc_ref[:, sl], NT, preferred_element_type=F32) * scale
            m = jnp.maximum(jnp.max(s_loc, axis=-1, keepdims=True), jnp.max(s_ctx, axis=-1, keepdims=True))
            p_loc, p_ctx = jnp.exp(s_loc - m), jnp.exp(s_ctx - m)
            l = jnp.sum(p_loc, axis=-1, keepdims=True) + jnp.sum(p_ctx, axis=-1, keepdims=True)
            o = (jnp.dot(p_loc.astype(BF16), v_ref[pl.ds(st, NA_KEYS), sl], preferred_element_type=F32)
                 + jnp.dot(p_ctx.astype(BF16), vc_ref[:, sl], preferred_element_type=F32))
            o_ref[:, sl] = (o / l).astype(BF16)
            lse_ref[hh] = m + jnp.log(l)

    outs, couts = _call(
        body, [qkv, qkv, qkv, kvc, kvc, bias], name=name, grid=(G, rows),
        in_specs=[pl.BlockSpec((GRID_W, wd), lambda h, r: (r, h)),
                  pl.BlockSpec((L, wd), lambda h, r: (0, G + h)),
                  pl.BlockSpec((L, wd), lambda h, r: (0, 2 * G + h)),
                  pl.BlockSpec((Lc, wd), lambda h, r: (0, h)),
                  pl.BlockSpec((Lc, wd), lambda h, r: (0, G + h)),
                  pl.BlockSpec((hp,) + bias.shape[1:], lambda h, r: (h, 0, 0, 0))],
        out_specs=[pl.BlockSpec((GRID_W, wd), lambda h, r: (r, h)),
                   pl.BlockSpec((hp, GRID_W, 1), lambda h, r: (h, r, 0))],
        out_shape=[jax.ShapeDtypeStruct((L, NA), BF16), jax.ShapeDtypeStruct((H, L, 1), F32)],
        sem=("parallel", "arbitrary"), comm=comm)
    return outs if comm is None else (outs, couts)


def natten_bwd(name, qkv, kvc, bias, o, lse, dmix, comm=None):
    L, NA, H, hp, G, rows, Lc = _na_dims(qkv, kvc)
    scale = NA_HEAD_DIM ** -0.5
    wd = hp * NA_HEAD_DIM

    def body(q_ref, k_ref, v_ref, kc_ref, vc_ref, b_ref, o_ref, lse_ref, do_ref,
             dq_ref, dk_ref, dv_ref, dkc_ref, dvc_ref, db_ref):
        r = pl.program_id(1)
        st, off = _na_window(r, rows)

        @pl.when(r == 0)
        def _():
            for ref in (dk_ref, dv_ref, dkc_ref, dvc_ref, db_ref):
                ref[...] = jnp.zeros_like(ref)

        for hh in range(hp):
            sl = slice(hh * NA_HEAD_DIM, (hh + 1) * NA_HEAD_DIM)
            q, kl, vl, kc, vc = q_ref[:, sl], k_ref[pl.ds(st, NA_KEYS), sl], v_ref[pl.ds(st, NA_KEYS), sl], kc_ref[:, sl], vc_ref[:, sl]
            do = do_ref[:, sl]
            lse_v = lse_ref[hh]
            p_loc = jnp.exp(lax.dot_general(q, kl, NT, preferred_element_type=F32) * scale + _na_bias_tile(b_ref, hh, off) - lse_v)
            p_ctx = jnp.exp(lax.dot_general(q, kc, NT, preferred_element_type=F32) * scale - lse_v)
            delta = jnp.sum(do.astype(F32) * o_ref[:, sl].astype(F32), axis=-1, keepdims=True)
            ds_loc = p_loc * (lax.dot_general(do, vl, NT, preferred_element_type=F32) - delta)
            ds_ctx = p_ctx * (lax.dot_general(do, vc, NT, preferred_element_type=F32) - delta)
            dsl, dsc = ds_loc.astype(BF16), ds_ctx.astype(BF16)
            dq = jnp.dot(dsl, kl, preferred_element_type=F32) + jnp.dot(dsc, kc, preferred_element_type=F32)
            dq_ref[:, sl] = (dq * scale).astype(BF16)
            dk_ref[pl.ds(st, NA_KEYS), sl] += lax.dot_general(dsl, q, TN, preferred_element_type=F32) * scale
            dv_ref[pl.ds(st, NA_KEYS), sl] += lax.dot_general(p_loc.astype(BF16), do, TN, preferred_element_type=F32)
            dkc_ref[:, sl] += lax.dot_general(dsc, q, TN, preferred_element_type=F32) * scale
            dvc_ref[:, sl] += lax.dot_general(p_ctx.astype(BF16), do, TN, preferred_element_type=F32)
            for j in range(NA_PAIRS):
                db_ref[hh, NA_WIN_R - 1 - off + 2 * j] += ds_loc[:, 2 * j * GRID_W:(2 * j + 2) * GRID_W]

    tok = pl.BlockSpec((GRID_W, wd), lambda h, r: (r, h))
    bia = pl.BlockSpec((hp,) + bias.shape[1:], lambda h, r: (h, 0, 0, 0))
    outs, couts = _call(
        body, [qkv, qkv, qkv, kvc, kvc, bias, o, lse, dmix], name=name, grid=(G, rows),
        in_specs=[tok,
                  pl.BlockSpec((L, wd), lambda h, r: (0, G + h)),
                  pl.BlockSpec((L, wd), lambda h, r: (0, 2 * G + h)),
                  pl.BlockSpec((Lc, wd), lambda h, r: (0, h)),
                  pl.BlockSpec((Lc, wd), lambda h, r: (0, G + h)),
                  bia,
                  tok,
                  pl.BlockSpec((hp, GRID_W, 1), lambda h, r: (h, r, 0)),
                  pl.BlockSpec((GRID_W, wd), lambda h, r: (r, G + h))],
        out_specs=[tok,
                   pl.BlockSpec((L, wd), lambda h, r: (0, h)),
                   pl.BlockSpec((L, wd), lambda h, r: (0, h)),
                   pl.BlockSpec((Lc, wd), lambda h, r: (0, h)),
                   pl.BlockSpec((Lc, wd), lambda h, r: (0, h)),
                   bia],
        out_shape=[jax.ShapeDtypeStruct((L, NA), BF16), jax.ShapeDtypeStruct((L, NA), F32), jax.ShapeDtypeStruct((L, NA), F32),
                   jax.ShapeDtypeStruct((Lc, NA), F32), jax.ShapeDtypeStruct((Lc, NA), F32),
                   jax.ShapeDtypeStruct(bias.shape, F32)],
        sem=("parallel", "arbitrary"), comm=comm)
    return outs if comm is None else (outs, couts)


def _s5_dims(T, N):
    TC = T // S5_SEG
    assert T % (S5_SEG * SUB * 2) == 0 and N % S5_STRIP == 0
    return TC, TC // SUB, S5_SEG, N // S5_STRIP


def _s5_backward(d, rev):
    return (d == 1) != rev


def s5_scan(name, xin, mats, a, rev, comm=None):
    _, T, W = xin.shape
    N = a.shape[-1]
    TC, NG, NCH, NS = _s5_dims(T, N)
    CW, SL = W // NS, S5_STRIP

    def ck(d, k):
        return jnp.where(_s5_backward(d, rev), NCH - 1 - k, k)

    def body(x_ref, m_ref, a_ref, h_ref, f_ref, carry, hs):
        @pl.when(pl.program_id(2) == 0)
        def _():
            carry[...] = jnp.zeros_like(carry)

        xb = x_ref[...].astype(BF16)
        hs[0] = jnp.dot(xb, m_ref[0], preferred_element_type=F32)
        hs[1] = jnp.dot(xb, m_ref[1], preferred_element_type=F32)
        ar, ai = jnp.broadcast_to(a_ref[0], (SUB, SL)), jnp.broadcast_to(a_ref[1], (SUB, SL))
        bw = _s5_backward(pl.program_id(0), rev)

        def step(t, c):
            hr, hi = c
            row = pl.multiple_of(jnp.where(bw, NG - 1 - t, t) * SUB, SUB)
            nr = ar * hr - ai * hi + hs[0, pl.ds(row, SUB), :]
            ni = ar * hi + ai * hr + hs[1, pl.ds(row, SUB), :]
            hs[0, pl.ds(row, SUB), :] = nr
            hs[1, pl.ds(row, SUB), :] = ni
            return nr, ni

        hr, hi = lax.fori_loop(0, NG, step, (carry[0], carry[1]))
        carry[0], carry[1] = hr, hi
        f_ref[0], f_ref[1] = hr, hi
        h_ref[...] = hs[...].astype(BF16)

    outs, couts = _call(
        body, [xin, mats, a], name=name, grid=(2, NS, NCH),
        in_specs=[pl.BlockSpec((None, TC, CW), lambda d, j, k: (d, ck(d, k), j)),
                  pl.BlockSpec((None, 2, None, CW, SL), lambda d, j, k: (d, 0, j, 0, 0)),
                  pl.BlockSpec((None, 2, 1, SL), lambda d, j, k: (d, 0, 0, j))],
        out_specs=[pl.BlockSpec((None, 2, TC, SL), lambda d, j, k: (d, 0, ck(d, k), j)),
                   pl.BlockSpec((None, 2, SUB, SL), lambda d, j, k: (d, 0, 0, j))],
        out_shape=[jax.ShapeDtypeStruct((2, 2, T, N), BF16), jax.ShapeDtypeStruct((2, 2, SUB, N), F32)],
        scratch_shapes=[pltpu.VMEM((2, SUB, SL), F32), pltpu.VMEM((2, TC, SL), F32)],
        sem=("parallel", "parallel", "arbitrary"), comm=comm)
    return outs if comm is None else (outs, couts)


def s5_fix(name, hloc, hin, a, mats, rev, comm=None):
    _, _, T, N = hloc.shape
    TC, NG, NCH, NS = _s5_dims(T, N)
    SL = S5_STRIP
    CW = mats.shape[-1]

    def ck(d, k):
        return jnp.where(_s5_backward(d, rev), NCH - 1 - k, k)

    def body(h_ref, hin_ref, a_ref, m_ref, ho_ref, y_ref, g, hs):
        @pl.when(pl.program_id(2) == 0)
        def _():
            g[...] = hin_ref[...]

        hs[...] = h_ref[...].astype(F32)
        ar, ai = jnp.broadcast_to(a_ref[0], (SUB, SL)), jnp.broadcast_to(a_ref[1], (SUB, SL))
        bw = _s5_backward(pl.program_id(0), rev)

        def step(t, c):
            gr, gi = c
            row = pl.multiple_of(jnp.where(bw, NG - 1 - t, t) * SUB, SUB)
            nr = ar * gr - ai * gi
            ni = ar * gi + ai * gr
            hs[0, pl.ds(row, SUB), :] += nr
            hs[1, pl.ds(row, SUB), :] += ni
            return nr, ni

        gr, gi = lax.fori_loop(0, NG, step, (g[0], g[1]))
        g[0], g[1] = gr, gi
        hb = hs[...].astype(BF16)
        ho_ref[...] = hb
        y_ref[...] = (jnp.dot(hb[0], m_ref[0], preferred_element_type=F32)
                      + jnp.dot(hb[1], m_ref[1], preferred_element_type=F32))

    outs, couts = _call(
        body, [hloc, hin, a, mats], name=name, grid=(2, NS, NCH),
        in_specs=[pl.BlockSpec((None, 2, TC, SL), lambda d, j, k: (d, 0, ck(d, k), j)),
                  pl.BlockSpec((None, 2, SUB, SL), lambda d, j, k: (d, 0, 0, j)),
                  pl.BlockSpec((None, 2, 1, SL), lambda d, j, k: (d, 0, 0, j)),
                  pl.BlockSpec((None, 2, None, SL, CW), lambda d, j, k: (d, 0, j, 0, 0))],
        out_specs=[pl.BlockSpec((None, 2, TC, SL), lambda d, j, k: (d, 0, ck(d, k), j)),
                   pl.BlockSpec((None, TC, CW), lambda d, j, k: (d, ck(d, k), j))],
        out_shape=[jax.ShapeDtypeStruct((2, 2, T, N), BF16), jax.ShapeDtypeStruct((2, T, NS * CW), F32)],
        scratch_shapes=[pltpu.VMEM((2, SUB, SL), F32), pltpu.VMEM((2, TC, SL), F32)],
        sem=("parallel", "parallel", "arbitrary"), comm=comm)
    return outs if comm is None else (outs, couts)


def s5_grads(name, g, h, u, dy, comm=None):
    _, _, T, N = g.shape
    W = u.shape[-1]
    TC, NG, NCH, NS = _s5_dims(T, N)
    CW, SL = W // NS, S5_STRIP

    def body(g_ref, h_ref, hp_ref, hl_ref, u_ref, dy_ref, dm_ref, dc_ref, da_ref, hs):
        k = pl.program_id(2)
        sub = lax.broadcasted_iota(jnp.int32, (SUB, SL), 0)

        hf = h_ref[...].astype(F32)

        @pl.when(pl.program_id(0) == 0)
        def _():
            for z in range(2):
                wrapped = jnp.where(sub == 0, 0.0, pltpu.roll(hl_ref[z].astype(F32)[SUB:], 1, 0))
                hs[z, 0:SUB, :] = jnp.where(k == 0, wrapped, hp_ref[z].astype(F32)[SUB:])
                hs[z, SUB:TC, :] = hf[z, 0:TC - SUB]

        @pl.when(pl.program_id(0) == 1)
        def _():
            for z in range(2):
                wrapped = jnp.where(sub == SUB - 1, 0.0, pltpu.roll(hl_ref[z].astype(F32)[:SUB], SUB - 1, 0))
                hs[z, TC - SUB:TC, :] = jnp.where(k == NCH - 1, wrapped, hp_ref[z].astype(F32)[:SUB])
                hs[z, 0:TC - SUB, :] = hf[z, SUB:TC]

        gr, gi, pr, pi = g_ref[0].astype(F32), g_ref[1].astype(F32), hs[0], hs[1]
        dar = jnp.sum((gr * pr + gi * pi).reshape(NG, SUB, SL), axis=0)
        dai = jnp.sum((gi * pr - gr * pi).reshape(NG, SUB, SL), axis=0)
        ub, dyb = u_ref[...].astype(BF16), dy_ref[...].astype(BF16)
        dm = [lax.dot_general(ub, g_ref[z], TN, preferred_element_type=F32) for z in range(2)]
        dc = [lax.dot_general(dyb, h_ref[z], TN, preferred_element_type=F32) for z in range(2)]

        @pl.when(k == 0)
        def _():
            da_ref[0], da_ref[1] = dar, dai
            for z in range(2):
                dm_ref[z], dc_ref[z] = dm[z], dc[z]

        @pl.when(k > 0)
        def _():
            da_ref[0] += dar
            da_ref[1] += dai
            for z in range(2):
                dm_ref[z] += dm[z]
                dc_ref[z] += dc[z]

    big = pl.BlockSpec((None, 2, TC, SL), lambda d, j, k: (d, 0, k, j))
    tok = pl.BlockSpec((None, TC, CW), lambda d, j, k: (d, k, j))
    mat = pl.BlockSpec((None, 2, None, CW, SL), lambda d, j, k: (d, 0, j, 0, 0))
    outs, couts = _call(
        body, [g, h, h, h, u, dy], name=name, grid=(2, NS, NCH),
        in_specs=[big, big,
                  pl.BlockSpec((None, 2, 2 * SUB, SL), lambda d, j, k: (
                      d, 0, jnp.where(d == 0, jnp.maximum(k * NG - 1, 0), jnp.minimum((k + 1) * NG, T // SUB - 1)) // 2, j)),
                  pl.BlockSpec((None, 2, 2 * SUB, SL), lambda d, j, k: (d, 0, jnp.where(d == 0, T // SUB - 1, 0) // 2, j)),
                  tok, tok],
        out_specs=[mat, mat, pl.BlockSpec((None, 2, SUB, SL), lambda d, j, k: (d, 0, 0, j))],
        out_shape=[jax.ShapeDtypeStruct((2, 2, NS, CW, SL), F32), jax.ShapeDtypeStruct((2, 2, NS, CW, SL), F32),
                   jax.ShapeDtypeStruct((2, 2, SUB, N), F32)],
        scratch_shapes=[pltpu.VMEM((2, TC, SL), F32)],
        sem=("parallel", "parallel", "arbitrary"), comm=comm)
    return outs if comm is None else (outs, couts)


def _interleave(seq):
    *lead, T, W = seq.shape
    n = len(lead)
    return seq.reshape(*lead, S5_SEG, T // S5_SEG, W).swapaxes(n, n + 1).reshape(*lead, T, W)


def _deinterleave(seq):
    *lead, T, W = seq.shape
    n = len(lead)
    return seq.reshape(*lead, T // S5_SEG, S5_SEG, W).swapaxes(n, n + 1).reshape(*lead, T, W)


def _s5_discretize(lam_re, lam_im, log_dt, b_re, b_im):
    dt = jnp.exp(log_dt)[..., None]
    mag = jnp.exp(lam_re * dt)
    a_re = mag * jnp.cos(lam_im * dt)
    a_im = mag * jnp.sin(lam_im * dt)
    den = jnp.square(lam_re) + jnp.square(lam_im)
    f_re = ((a_re - 1.0) * lam_re + a_im * lam_im) / den
    f_im = (a_im * lam_re - (a_re - 1.0) * lam_im) / den
    bb_re = f_re[..., None] * b_re - f_im[..., None] * b_im
    bb_im = f_re[..., None] * b_im + f_im[..., None] * b_re
    return a_re, a_im, bb_re, bb_im


_GPS = S5_STRIP // SSM_STATE


def _blockdiag(t):
    d2, G, P, Cg = t.shape
    t5 = t.reshape(d2, G // _GPS, _GPS, P, Cg).transpose(0, 1, 2, 4, 3)
    m = t5[:, :, :, :, None, :] * jnp.eye(_GPS, dtype=t.dtype)[None, None, :, None, :, None]
    return m.reshape(d2, G // _GPS, _GPS * Cg, _GPS * P)


def _blockdiag_extract(m, Cg, P):
    d2, NS = m.shape[:2]
    m6 = m.reshape(d2, NS, _GPS, Cg, _GPS, P)
    diag = jnp.stack([m6[:, :, i, :, i, :] for i in range(_GPS)], axis=2)
    return diag.transpose(0, 1, 2, 4, 3).reshape(d2, NS * _GPS, P, Cg)


def _cmul(a, b):
    return a[0] * b[0] - a[1] * b[1], a[0] * b[1] + a[1] * b[0]


def _cpow(a, n):
    out, base = None, a
    while n:
        if n & 1:
            out = base if out is None else _cmul(out, base)
        base = _cmul(base, base)
        n >>= 1
    return out


def _segment_carry(fin, apow, rev):
    per_dir = []
    for d in range(2):
        fr, fi = fin[d, 0], fin[d, 1]
        ap = (apow[0][d], apow[1][d])
        cr = ci = jnp.zeros_like(fr[0:1])
        outs = [None] * S5_SEG
        backward = (d == 1) != rev
        for s in (range(S5_SEG - 1, -1, -1) if backward else range(S5_SEG)):
            outs[s] = (cr, ci)
            pr, pi = _cmul(ap, (cr, ci))
            cr, ci = pr + fr[s:s + 1], pi + fi[s:s + 1]
        per_dir.append(jnp.stack([jnp.concatenate([o[0] for o in outs]), jnp.concatenate([o[1] for o in outs])]))
    return jnp.stack(per_dir)


def _coords():
    x, y, c = lax.axis_index("x"), lax.axis_index("y"), lax.axis_index("c")
    others = [(1 - x, y), (x, 1 - y), (1 - x, 1 - y)]
    return x, y, c, 2 * x + y, others


def _comm(name, ins, out_shapes, aliases, n_local, n_remote, plan):
    n_in, n_out = len(ins), len(out_shapes)

    def body(*refs):
        in_refs, out_refs = refs[:n_in], refs[n_in:n_in + n_out]
        send_sems, recv_sems, local_sems = refs[n_in + n_out:]
        x, y, c = lax.axis_index("x"), lax.axis_index("y"), lax.axis_index("c")
        locs, sends, lands = plan(in_refs, out_refs)
        assert len(locs) == n_local and len(sends) == n_remote and len(lands) == n_remote
        local = [pltpu.make_async_copy(s, d, local_sems.at[i]) for i, (s, d) in enumerate(locs)]
        for cp in local:
            cp.start()
        remote = [pltpu.make_async_remote_copy(src_ref=s, dst_ref=d, send_sem=send_sems.at[i], recv_sem=recv_sems.at[i],
                                               device_id=peer, device_id_type=MESH)
                  for i, (s, d, peer) in enumerate(sends)]
        for cp in remote:
            cp.start()
        for i, d in enumerate(lands):
            pltpu.make_async_remote_copy(src_ref=d, dst_ref=d, send_sem=send_sems.at[i], recv_sem=recv_sems.at[i],
                                         device_id=(x, y, c), device_id_type=MESH).wait_recv()
        for cp in remote:
            cp.wait_send()
        for cp in local:
            cp.wait()

    any_spec = pl.BlockSpec(memory_space=pl.ANY)
    return pl.pallas_call(
        body, name=name,
        in_specs=[any_spec] * n_in, out_specs=[any_spec] * n_out,
        out_shape=[jax.ShapeDtypeStruct(s, d) for s, d in out_shapes],
        input_output_aliases=aliases,
        scratch_shapes=[pltpu.SemaphoreType.DMA((n_remote,)), pltpu.SemaphoreType.DMA((n_remote,)),
                        pltpu.SemaphoreType.DMA((max(n_local, 1),))],
        compiler_params=pltpu.CompilerParams(has_side_effects=True),
    )(*ins)


def allgather_dev(name, v):
    M, Nc = v.shape

    def plan(in_refs, out_refs):
        (v_ref,), (o_ref,) = in_refs, out_refs
        x, y, c = lax.axis_index("x"), lax.axis_index("y"), lax.axis_index("c")

        def rows(px, py, pc):
            return o_ref.at[pl.ds((4 * px + 2 * py + pc) * M, M), :]

        peers = [(x ^ fx, y ^ fy, c ^ fc) for fx in (0, 1) for fy in (0, 1) for fc in (0, 1) if fx or fy or fc]
        return ([(v_ref, rows(x, y, c))],
                [(v_ref, rows(x, y, c), p) for p in peers],
                [rows(*p) for p in peers])

    return _comm(name, [v], [((N_DEV * M, Nc), v.dtype)], {}, 1, N_DEV - 1, plan)[0]


def allgather_chips_1(name, shards):
    def plan(in_refs, out_refs):
        x, y, c, chip, others = _coords()
        sends, lands = [], []
        for s_ref, g_ref in zip(in_refs, out_refs):
            hr = s_ref.shape[0] // 2
            mine = pl.ds(c * hr, hr)
            for qx, qy in others:
                sends.append((s_ref.at[mine], g_ref.at[chip, mine], (qx, qy, c)))
                lands.append(g_ref.at[2 * qx + qy, mine])
        return [], sends, lands

    n = len(shards)
    comm = (list(shards), [((N_CHIP,) + s.shape, s.dtype) for s in shards], {}, 3 * n, plan)
    return comm if name is None else _comm(name, comm[0], comm[1], comm[2], 0, comm[3], comm[4])


def allgather_chips_2(name, gathered, shards):
    n = len(gathered)

    def plan(in_refs, out_refs):
        x, y, c, chip, others = _coords()
        sends, lands = [], []
        for s_ref, g_ref in zip(in_refs[n:], out_refs):
            hr = g_ref.shape[1] // 2
            for qx, qy in others:
                q = 2 * qx + qy
                sends.append((g_ref.at[q, pl.ds(c * hr, hr)], g_ref.at[q, pl.ds(c * hr, hr)], (x, y, 1 - c)))
                lands.append(g_ref.at[q, pl.ds((1 - c) * hr, hr)])
            sends.append((s_ref, g_ref.at[chip], (x, y, 1 - c)))
            lands.append(g_ref.at[chip])
        return [], sends, lands

    comm = (list(gathered) + list(shards), [(g.shape, g.dtype) for g in gathered], {i: i for i in range(n)}, 4 * n, plan)
    return comm if name is None else _comm(name, comm[0], comm[1], comm[2], 0, comm[3], comm[4])


def reduce_1(name, grads):
    def plan(in_refs, out_refs):
        x, y, c, chip, others = _coords()
        sends, lands = [], []
        for g_ref, got_ref in zip(in_refs, out_refs):
            hr = g_ref.shape[1] // 2
            sends.append((g_ref.at[:, pl.ds((1 - c) * hr, hr), :], got_ref, (x, y, 1 - c)))
            lands.append(got_ref)
        return [], sends, lands

    n = len(grads)
    comm = (list(grads), [((g.shape[0], g.shape[1] // 2, g.shape[2]), g.dtype) for g in grads], {}, n, plan)
    return comm if name is None else _comm(name, comm[0], comm[1], comm[2], 0, comm[3], comm[4])


def _merge_comm(a, b):
    if a is None or b is None:
        return a if b is None else b
    na_in, na_out = len(a[0]), len(a[1])

    def plan(in_refs, out_refs):
        _, s1, l1 = a[4](in_refs[:na_in], out_refs[:na_out])
        _, s2, l2 = b[4](in_refs[na_in:], out_refs[na_out:])
        return [], s1 + s2, l1 + l2

    alias = dict(a[2])
    alias.update({na_in + i: na_out + j for i, j in b[2].items()})
    return (a[0] + b[0], a[1] + b[1], alias, a[3] + b[3], plan)


def reduce_2(name, parts):
    def plan(in_refs, out_refs):
        x, y, c, chip, others = _coords()
        sends, lands = [], []
        for t_ref, q_ref in zip(in_refs, out_refs):
            for qx, qy in others:
                sends.append((t_ref.at[2 * qx + qy], q_ref.at[chip], (qx, qy, c)))
                lands.append(q_ref.at[2 * qx + qy])
        return [], sends, lands

    n = len(parts)
    comm = (list(parts), [(p.shape, p.dtype) for p in parts], {}, 3 * n, plan)
    return comm if name is None else _comm(name, comm[0], comm[1], comm[2], 0, comm[3], comm[4])


def share_slots(name, slots):
    def plan(in_refs, out_refs):
        x, y, c, chip, others = _coords()
        (q_ref,) = out_refs
        return ([], [(q_ref.at[chip], q_ref.at[chip], (qx, qy, c)) for qx, qy in others],
                [q_ref.at[2 * qx + qy] for qx, qy in others])

    return _comm(name, [slots], [(slots.shape, slots.dtype)], {0: 0}, 0, N_CHIP - 1, plan)[0]


def allreduce_small(tag, buf, ids):
    got = reduce_1(tag + "_1", [buf[None]])[0][0]
    slots = share_slots(tag + "_2", pair_sum_to_slot(tag + "_add", buf, got, ids))
    full = reduce_3(tag + "_3", [sum_chips_to_half(tag + "_sum", slots, ids)])[0]
    return full.reshape(buf.shape)


def reduce_3(name, fulls):
    def plan(in_refs, out_refs):
        x, y, c, chip, others = _coords()
        sends, lands = [], []
        for o_ref in out_refs:
            sends.append((o_ref.at[c], o_ref.at[c], (x, y, 1 - c)))
            lands.append(o_ref.at[1 - c])
        return [], sends, lands

    n = len(fulls)
    return _comm(name, fulls, [(f.shape, f.dtype) for f in fulls], {i: i for i in range(n)}, 0, n, plan)


_WEIGHTS = ['c_ctx', 'w_mod', 'b_mod', 'g_mix', 'g_ffn', 'w_in', 'ssm_lam_re', 'ssm_lam_im', 'ssm_log_dt', 'ssm_b_re',
            'ssm_b_im', 'ssm_c_re', 'ssm_c_im', 'ssm_d', 'ssm_w_glu', 'na_rpb', 'w_out', 'cv_w_pw1', 'cv_dw_w', 'cv_dw_b',
            'cv_ln_g', 'cv_ln_b', 'cv_w_pw2', 'ffn_w_up', 'ffn_conv_w', 'ffn_conv_b', 'ffn_w_down', 'g_out']
_INPUTS = ['x', 'c', 'ctx'] + _WEIGHTS + ['loss_target'] + ['m_' + w for w in _WEIGHTS] + ['v_' + w for w in _WEIGHTS]
_GATHERED_SMALL = ['ffn_conv_w', 'cv_dw_w', 'cv_dw_b', 'cv_ln_g', 'cv_ln_b']


def _silu(v):
    return v * jax.nn.sigmoid(v)


def _pick_index(t, idx, axis):
    shape = [1] * t.ndim
    shape[axis] = t.shape[axis]
    mask = (jnp.arange(t.shape[axis]) == idx).reshape(shape)
    return jnp.sum(jnp.where(mask, t, jnp.zeros((), t.dtype)), axis=axis)


def _pack(arrs, cols, row_mult=SUB):
    flat = jnp.concatenate([a.reshape(-1).astype(F32) for a in arrs])
    n = flat.shape[0]
    unit = row_mult * cols
    flat = jnp.pad(flat, (0, (-n) % unit))
    return flat.reshape(-1, cols)


def _unpack(buf, shapes):
    flat = buf.reshape(-1)
    out, o = [], 0
    for s in shapes:
        n = int(np.prod(s))
        out.append(flat[o:o + n].reshape(s))
        o += n
    return out


def _carried(res, comm):
    return res if comm is not None else (res, [])


def _ffn_fwd(tag, xin, sh, sc, gt, g, wup, cw3, cb3, wdn, comm_up=None, comm_mid=None, comm_down=None, hf=None, nxt=None):
    if hf is None:
        hf = norm_mod_fwd(tag + "_norm", xin, g * (1.0 + sc), sh)
    up3, got_up = _carried(mm_nn_pieces(tag + "_up", hf, wup, 0, N_CHIP, BF16, halves=2, comm=comm_up), comm_up)
    comm_mid = comm_mid(got_up) if callable(comm_mid) else comm_mid
    act, got_mid = _carried(ffn_mid_fwd(tag + "_mid", up3, cw3, cb3, comm=comm_mid), comm_mid)
    comm_down = comm_down(got_mid) if callable(comm_down) else comm_down
    yf, got_down = _carried(mm_nn(tag + "_down", act, wdn, BF16, comm=comm_down), comm_down)
    if nxt == "loss":
        xo, hn = None, None
    elif nxt is None:
        xo, hn = gate_res_fwd(tag + "_res", xin, yf, gt), None
    else:
        xo, hn = res_norm_fwd(tag + "_res", xin, yf, gt, *nxt)
    return xo, (xin, hf, up3, act, yf), got_up, got_mid, got_down, hn


def _ffn_bwd(tag, dxo, saved, sc, gt, g, wup, cw3, cb3, wdn, comm_down=None, comm_mid=None, comm_up=None):
    xin, hf, up3, act, yf = saved
    dyf, dgt = gate_res_bwd(tag + "_res_b", dxo, yf, gt)
    dact, got_down = _carried(mm_nt(tag + "_down_bx", dyf, wdn, BF16, comm=comm_down), comm_down)
    dwdn = mm_tn(tag + "_down_bw", act, dyf, BF16)
    comm_mid = comm_mid(got_down) if callable(comm_mid) else comm_mid
    (dup3, dcw3, dcb3), got_mid = _carried(ffn_mid_bwd(tag + "_mid_b", up3, dact, cw3, cb3, comm=comm_mid), comm_mid)
    dhf, got_up = _carried(mm_nt_pieces(tag + "_up_bx", dup3, wup, BF16, halves=2, comm=comm_up), comm_up)
    dwup = mm_tn_pieces(tag + "_up_bw", hf, dup3, N_CHIP, BF16, halves=2)
    dxi, cs1, cs2 = norm_mod_bwd(tag + "_norm_b", xin, dhf, g * (1.0 + sc), dxo)
    return dxi, dict(dsh=cs1[0], dsc=cs2[0] * g, dgt=dgt[0], dg=cs2[0] * (1.0 + sc), dwup=dwup, dwdn=dwdn,
                     dcw=dcw3.transpose(1, 0, 2).reshape(3, -1), dcb=dcb3.reshape(-1)), got_down, got_mid, got_up


def kernel(x, c, ctx, c_ctx, w_mod, b_mod, g_mix, g_ffn, w_in, ssm_lam_re, ssm_lam_im, ssm_log_dt, ssm_b_re, ssm_b_im, ssm_c_re, ssm_c_im, ssm_d, ssm_w_glu, na_rpb, w_out, cv_w_pw1, cv_dw_w, cv_dw_b, cv_ln_g, cv_ln_b, cv_w_pw2, ffn_w_up, ffn_conv_w, ffn_conv_b, ffn_w_down, g_out, loss_target, m_c_ctx, m_w_mod, m_b_mod, m_g_mix, m_g_ffn, m_w_in, m_ssm_lam_re, m_ssm_lam_im, m_ssm_log_dt, m_ssm_b_re, m_ssm_b_im, m_ssm_c_re, m_ssm_c_im, m_ssm_d, m_ssm_w_glu, m_na_rpb, m_w_out, m_cv_w_pw1, m_cv_dw_w, m_cv_dw_b, m_cv_ln_g, m_cv_ln_b, m_cv_w_pw2, m_ffn_w_up, m_ffn_conv_w, m_ffn_conv_b, m_ffn_w_down, m_g_out, v_c_ctx, v_w_mod, v_b_mod, v_g_mix, v_g_ffn, v_w_in, v_ssm_lam_re, v_ssm_lam_im, v_ssm_log_dt, v_ssm_b_re, v_ssm_b_im, v_ssm_c_re, v_ssm_c_im, v_ssm_d, v_ssm_w_glu, v_na_rpb, v_w_out, v_cv_w_pw1, v_cv_dw_w, v_cv_dw_b, v_cv_ln_g, v_cv_ln_b, v_cv_w_pw2, v_ffn_w_up, v_ffn_conv_w, v_ffn_conv_b, v_ffn_w_down, v_g_out):
    p = dict(locals())
    xi, yi, ci = lax.axis_index("x"), lax.axis_index("y"), lax.axis_index("c")
    me, chip = 4 * xi + 2 * yi + ci, 2 * xi + yi
    xs, cx, tgt = x[0], ctx[0], loss_target[0]
    L, D = xs.shape
    Lc = cx.shape[0]
    T = L + Lc
    W = D // 2
    Cq = w_mod.shape[2]

    s_mix = [t.astype(BF16) for t in (w_in[0], ssm_w_glu[0], w_out[0])]
    s_ffn0 = [t.astype(BF16) for t in (ffn_w_up[0], ffn_w_down[0])]
    s_conv = [t.astype(BF16) for t in (cv_w_pw1[0], cv_w_pw2[0])]
    s_ffn1 = [t.astype(BF16) for t in (ffn_w_up[1], ffn_w_down[1])]
    (Win,) = allgather_chips_2("gather_win_2", allgather_chips_1("gather_win_1", s_mix[:1]), s_mix[:1])
    Fd = ffn_w_down.shape[1] * N_CHIP
    c_idx = jnp.reshape(ci, (1,)).astype(jnp.int32)
    ids = jnp.stack([chip, ci]).astype(jnp.int32)

    def added(tag, grads, got):
        return [add_half("reduce_%s_add%d" % (tag, i), g, r, c_idx) for i, (g, r) in enumerate(zip(grads, got))]

    small_shapes = [p[n].shape for n in _GATHERED_SMALL]
    sm = allgather_dev("gather_small", _pack([p[n] for n in _GATHERED_SMALL], 1024))
    sm = sm.reshape(N_DEV, -1)[0::2]
    per_chip = [_unpack(sm[q], small_shapes) for q in range(N_CHIP)]
    conv_w_f, dw_w_f, dw_b_f, ln_g_f, ln_b_f = (jnp.concatenate([pc[i] for pc in per_chip], axis=-1)
                                                for i in range(len(_GATHERED_SMALL)))
    cw3 = [conv_w_f[l].reshape(3, 2, Fd).transpose(1, 0, 2) for l in range(2)]
    cb3 = [ffn_conv_b[l].reshape(2, 1, Fd) for l in range(2)]
    dw_w_f, dw_b_f, ln_g_f, ln_b_f = dw_w_f[0], dw_b_f[0], ln_g_f[0], ln_b_f[0]

    c_all = allgather_dev("gather_c", jnp.zeros((SUB, D), F32).at[0].set(c[0])).reshape(N_DEV, SUB, D)[:, 0]
    S16 = jnp.concatenate([_silu(c_all), _silu(c_ctx)[None], jnp.zeros((2 * SUB - N_DEV - 1, D), F32)])
    modp = mm_nn_pieces("mod_fwd", S16, w_mod, 0, 2, F32)
    modg = allgather_dev("gather_mod", modp).reshape(N_DEV, 2 * SUB, 2, Cq)[0::2]

    def mod_row(r):
        return r.transpose(1, 0, 2).reshape(2, N_CHIP * Cq) + b_mod

    mod_me = mod_row(_pick_index(modg, me, 1))
    mod_c = mod_row(modg[:, N_DEV])
    mods = [[mod_me[l, i * D:(i + 1) * D] for i in range(N_MOD)] for l in range(2)]
    shc, scc = mod_c[0, :D], mod_c[0, D:2 * D]

    sh_m, sc_m, gt_m, sh_f, sc_f, gt_f = mods[0]
    h0 = norm_mod_fwd("l0_norm", xs, g_mix[0] * (1.0 + sc_m), sh_m)
    hc0 = norm_mod_fwd("l0_norm_c", cx, g_mix[0] * (1.0 + scc), shc)
    u = mm_nn_pieces("l0_in_u", h0, Win, 0, 1, F32)
    qkv, g_mix1 = mm_nn_pieces("l0_in_qkv", h0, Win, 1, 3, BF16, comm=allgather_chips_1(None, s_mix[1:]))
    uc = mm_nn_pieces("l0_in_uc", hc0, Win, 0, 1, F32)
    kvc = mm_nn_pieces("l0_in_kvc", hc0, Win, 2, 2, BF16)

    lam_re, lam_im, log_dt = ssm_lam_re[0], ssm_lam_im[0], ssm_log_dt[0]
    b_re, b_im, c_re, c_im = ssm_b_re[0], ssm_b_im[0], ssm_c_re[0], ssm_c_im[0]
    (a_re, a_im, bb_re, bb_im), disc_vjp = jax.vjp(_s5_discretize, lam_re, lam_im, log_dt, b_re, b_im)
    G, P, Cg = bb_re.shape[1:]
    N = G * P
    a_re, a_im = a_re.reshape(2, 1, N), a_im.reshape(2, 1, N)
    a_f, a_b = jnp.stack([a_re, a_im], axis=1), jnp.stack([a_re, -a_im], axis=1)
    Bblk = jnp.stack([_blockdiag(bb_re), _blockdiag(bb_im)], axis=1)
    Cblk = jnp.stack([_blockdiag(c_re.swapaxes(-1, -2)), -_blockdiag(c_im.swapaxes(-1, -2))], axis=1)
    apow = _cpow((a_re, a_im), T // S5_SEG)

    useq = _interleave(jnp.stack([jnp.concatenate([uc, u]), jnp.concatenate([u, uc])]).astype(BF16))
    (hloc, fin), (Wglu, Wout, g_dn0) = s5_scan(
        "s5_scan", useq, Bblk.astype(BF16), a_f, rev=False,
        comm=_merge_comm(allgather_chips_2(None, g_mix1, s_mix[1:]), allgather_chips_1(None, s_ffn0[1:])))
    Wglu, Wout = Wglu.reshape(-1, Wglu.shape[-1]), Wout.reshape(-1, D)
    (hst, yseq), (g_pw1, g_pw2) = s5_fix("s5_fix", hloc, _segment_carry(fin, apow, False), a_f,
                                         Cblk.swapaxes(-1, -2).astype(BF16), rev=False, comm=allgather_chips_1(None, s_conv))
    g_dn0 = [g_dn0]
    ys = _deinterleave(yseq)
    y0, y1 = ys[0, Lc:], ys[1, :L]
    s5o = glu_fwd("s5_glu", u, y0, y1, ssm_d[0], Wglu)

    bias = na_bias(na_rpb[0])
    (o_na, lse), (g_up0, Wdn0) = natten_fwd(
        "na_fwd", qkv, kvc, bias,
        comm=_merge_comm(allgather_chips_1(None, s_ffn0[:1]), allgather_chips_2(None, g_dn0, s_ffn0[1:])))
    mixcat = jnp.concatenate([s5o, o_na], axis=1)
    ymix, (Wup0,) = mm_nn("l0_out", mixcat, Wout, BF16, comm=allgather_chips_2(None, [g_up0], s_ffn0[:1]))
    sh_v, sc_v, gt_v, sh_g, sc_g, gt_g = mods[1]
    x1, hf0 = res_norm_fwd("l0_res", xs, ymix, gt_m, g_ffn[0] * (1.0 + sc_f), sh_f)
    Wdn0 = Wdn0.reshape(-1, D)
    x2, ffn0, (g_up1,), (g_dn1, Wup1), (Wdn1,), hcv = _ffn_fwd(
        "f0", x1, sh_f, sc_f, gt_f, g_ffn[0], Wup0, cw3[0], cb3[0], Wdn0,
        comm_up=allgather_chips_1(None, s_ffn1[:1]),
        comm_mid=lambda got_up: _merge_comm(allgather_chips_1(None, s_ffn1[1:]), allgather_chips_2(None, got_up, s_ffn1[:1])),
        comm_down=lambda got_mid: allgather_chips_2(None, got_mid[:1], s_ffn1[1:]),
        hf=hf0, nxt=(g_mix[1] * (1.0 + sc_v), sh_v))
    Wpw1, Wpw2 = allgather_chips_2("gather_conv_2", [g_pw1, g_pw2], s_conv)
    Wpw2 = Wpw2.reshape(-1, D)
    Wup, Wdn = [Wup0, Wup1], [Wdn0.reshape(-1, D), Wdn1.reshape(-1, D)]

    ag3 = mm_nn_pieces("l1_pw1", hcv, Wpw1, 0, N_CHIP, BF16, halves=2)
    z1, z3 = conf_mid_fwd("l1_mid", ag3, dw_w_f, dw_b_f, ln_g_f, ln_b_f)
    ycv = mm_nn("l1_pw2", z3, Wpw2, BF16)
    x3, hf1 = res_norm_fwd("l1_res", x2, ycv, gt_v, g_ffn[1] * (1.0 + sc_g), sh_g)
    _, ffn1, _, _, _, _ = _ffn_fwd("f1", x3, sh_g, sc_g, gt_g, g_ffn[1], Wup[1], cw3[1], cb3[1], Wdn[1], hf=hf1, nxt="loss")

    dx4, dg_out, loss_part = loss_head("loss", x3, ffn1[4], gt_g, g_out, tgt)
    loss = lax.psum(loss_part[0, 0], ("x", "y", "c"))

    dx3, gf1, _, _, _ = _ffn_bwd("f1", dx4, ffn1, sc_g, gt_g, g_ffn[1], Wup[1], cw3[1], cb3[1], Wdn[1])
    g_up1, g_dn1 = [gf1["dwup"]], [gf1["dwdn"].reshape(N_CHIP, -1, D)]
    dycv, dgt_v = gate_res_bwd("l1_res_b", dx3, ycv, gt_v)
    dz3, got = mm_nt("l1_pw2_bx", dycv, Wpw2, BF16, comm=reduce_1(None, g_up1))
    parts_up1 = added("up1", g_up1, got)
    dWpw2, got = mm_tn("l1_pw2_bw", z3, dycv, BF16, comm=reduce_1(None, g_dn1))
    parts_dn1 = added("dn1", g_dn1, got)
    dz1, dln_g, dln_b = conf_ln_bwd("l1_ln_b", z1, dz3, ln_g_f, ln_b_f)
    (dag3, ddw_w, ddw_b), slots_up1 = conf_conv_bwd("l1_conv_b", ag3, dz1, dw_w_f, comm=reduce_2(None, parts_up1))
    dhcv = mm_nt_pieces("l1_pw1_bx", dag3, Wpw1, BF16, halves=2)
    dWpw1 = mm_tn_pieces("l1_pw1_bw", hcv, dag3, N_CHIP, BF16, halves=2)
    dx2, cs1_v, cs2_v = norm_mod_bwd("l1_norm_b", x2, dhcv, g_mix[1] * (1.0 + sc_v), dx3)
    g_conv = [dWpw1, dWpw2.reshape(N_CHIP, -1, D)]

    held = {}

    def conv_stage_2(got_down):
        held["parts_conv"] = added("conv", g_conv, got_down)
        return _merge_comm(reduce_2(None, held["parts_conv"]), reduce_2(None, parts_dn1))

    dx1, gf0, _, slots_mid, _ = _ffn_bwd("f0", dx2, ffn0, sc_f, gt_f, g_ffn[0], Wup[0], cw3[0], cb3[0], Wdn[0],
                                          comm_down=reduce_1(None, g_conv), comm_mid=conv_stage_2)
    slots_conv, slots_dn1 = slots_mid[:2], slots_mid[2:]
    parts_conv = held["parts_conv"]
    g_up0, g_dn0 = [gf0["dwup"]], [gf0["dwdn"].reshape(N_CHIP, -1, D)]
    dymix, dgt_m = gate_res_bwd("l0_res_b", dx1, ymix, gt_m)
    dmix, got = mm_nt("l0_out_bx", dymix, Wout, BF16, comm=reduce_1(None, g_up0))
    parts_up0 = added("up0", g_up0, got)
    dWout, got = mm_tn("l0_out_bw", mixcat, dymix, BF16, comm=reduce_1(None, g_dn0))
    parts_dn0 = added("dn0", g_dn0, got)
    (dq, dk, dv, dkc, dvc, dbias), slots_up0 = natten_bwd("na_bwd", qkv, kvc, bias, o_na, lse, dmix,
                                                          comm=reduce_2(None, parts_up0))
    dy, zg, dzz, dd_skip = glu_bwd("s5_glu_b", u, y0, y1, ssm_d[0], Wglu, dmix)
    dWglu = mm_tn("s5_glu_bw", zg, dzz, BF16)
    g_mix2 = [dWglu.reshape(N_CHIP, -1, W), dWout.reshape(N_CHIP, -1, D)]

    zc = jnp.zeros((Lc, W), F32)
    dyseq = _interleave(jnp.stack([jnp.concatenate([zc, dy]), jnp.concatenate([dy, zc])]).astype(BF16))
    (gloc, gfin), slots_dn0 = s5_scan("s5_scan_b", dyseq, Cblk.astype(BF16), a_b, rev=True, comm=reduce_2(None, parts_dn0))
    apow_b = (apow[0], -apow[1])
    (gst, duseq), got = s5_fix("s5_fix_b", gloc, _segment_carry(gfin, apow_b, True), a_b, Bblk.swapaxes(-1, -2).astype(BF16),
                               rev=True, comm=reduce_1(None, g_mix2))
    parts_mix2 = added("mix2", g_mix2, got)
    (dBm, dCm, da8), slots_mix2 = s5_grads("s5_grads", gst, hst, useq, dyseq, comm=reduce_2(None, parts_mix2))
    dus = _deinterleave(duseq)
    du = fma3("s5_du", dy, dus[0, Lc:], dus[1, :L], ssm_d[0], BF16)
    duc = dus[0, :Lc] + dus[1, L:]

    d_in = [du, dq, dk, dv]
    d_in_c = [duc, jnp.zeros((Lc, W), BF16), dkc, dvc]
    dh0 = mm_nt_list("l0_in_bx", d_in, Win, BF16)
    dhc0 = mm_nt_list("l0_in_bxc", d_in_c, Win, BF16)
    h_all = jnp.concatenate([hc0, h0])
    dWin = jnp.stack([mm_tn("l0_in_bw%d" % q, h_all, jnp.concatenate([dc.astype(BF16), dl.astype(BF16)]), BF16)
                      for q, (dc, dl) in enumerate(zip(d_in_c, d_in))])
    dx0, cs1_m, cs2_m = norm_mod_bwd("l0_norm_b", xs, dh0, g_mix[0] * (1.0 + sc_m), dx1)
    _, cs1_c, cs2_c = norm_mod_bwd("l0_norm_bc", cx, dhc0, g_mix[0] * (1.0 + scc), jnp.zeros_like(cx))

    dmod0 = jnp.concatenate([cs1_m[0], cs2_m[0] * g_mix[0], dgt_m[0], gf0["dsh"], gf0["dsc"], gf0["dgt"]])
    dmod1 = jnp.concatenate([cs1_v[0], cs2_v[0] * g_mix[1], dgt_v[0], gf1["dsh"], gf1["dsc"], gf1["dgt"]])
    dmodc = jnp.concatenate([cs1_c[0], cs2_c[0] * g_mix[0], jnp.zeros((4 * D,), F32)])
    dm_rows = jnp.concatenate([jnp.stack([dmod0, dmod1, dmodc]), jnp.zeros((SUB - 3, N_MOD * D), F32)])
    dm_all = allgather_dev("gather_dmod", dm_rows).reshape(N_DEV, SUB, N_MOD * D)
    dm_sum = sum_lead("sum_dmod", dm_all, F32)
    pad7 = jnp.zeros((2 * SUB - N_DEV - 1, N_MOD * D), F32)
    dMod = [jnp.concatenate([dm_all[:, 0], dm_sum[2:3], pad7]), jnp.concatenate([dm_all[:, 1], jnp.zeros_like(dm_sum[2:3]), pad7])]
    dMod_cols = [_pick_index(m.reshape(m.shape[0], N_CHIP, Cq), chip, 1) for m in dMod]
    g_w_mod = jnp.stack([mm_tn("mod_bw%d" % l, S16, dMod_cols[l], F32) for l in range(2)])
    g_b_mod = jnp.stack([dm_sum[0] + dm_sum[2], dm_sum[1]])
    ds_part = mm_nt("mod_bx", dMod_cols[0], w_mod[0], F32)
    ds_all = allgather_dev("gather_dsc", jnp.zeros((SUB, D), F32).at[0].set(ds_part[N_DEV]))
    ds_c = sum_lead("sum_dsc", ds_all.reshape(N_DEV, SUB, D)[0::2], F32)[0]
    sg_c = jax.nn.sigmoid(c_ctx)
    g_c_ctx = ds_c * sg_c * (1.0 + c_ctx * (1.0 - sg_c))

    g_rpb_loc = na_bias_grad(dbias)

    dbb = [_blockdiag_extract(dBm[:, z], Cg, P) for z in range(2)]
    dcc = [_blockdiag_extract(dCm[:, z], Cg, P).swapaxes(-1, -2) for z in range(2)]
    da = jnp.sum(da8, axis=2).reshape(2, 2, G, P)
    small = {
        "g_mix": jnp.stack([cs2_m[0] * (1.0 + sc_m) + cs2_c[0] * (1.0 + scc), cs2_v[0] * (1.0 + sc_v)]),
        "g_ffn": jnp.stack([gf0["dg"], gf1["dg"]]),
        "a_re": da[:, 0], "a_im": da[:, 1], "bb_re": dbb[0], "bb_im": dbb[1], "c_re": dcc[0], "c_im": -dcc[1],
        "ssm_d": dd_skip, "na_rpb": g_rpb_loc, "cv_dw_w": ddw_w, "cv_dw_b": ddw_b, "cv_ln_g": dln_g, "cv_ln_b": dln_b,
        "ffn_conv_w": jnp.stack([gf0["dcw"], gf1["dcw"]]), "ffn_conv_b": jnp.stack([gf0["dcb"], gf1["dcb"]]),
        "g_out": dg_out,
    }
    skeys = list(small)
    sbuf = _pack([small[k] for k in skeys], 1024, 4 * SUB)
    ssum = dict(zip(skeys, _unpack(allreduce_small("reduce_small", sbuf, ids), [small[k].shape for k in skeys])))
    g_lam_re, g_lam_im, g_log_dt, g_b_re, g_b_im = disc_vjp((ssum["a_re"], ssum["a_im"], ssum["bb_re"], ssum["bb_im"]))

    def my_cols(t):
        n = t.shape[-1] // N_CHIP
        return _pick_index(t.reshape(t.shape[:-1] + (N_CHIP, n)), chip, t.ndim - 1)

    delta, new_m, new_v = {}, {}, {}
    parts_win = added("win", [dWin], reduce_1("reduce_win_1", [dWin]))
    (delta["w_mod"], new_m["w_mod"], new_v["w_mod"]), slots_win = adamw(
        "adamw_w_mod", w_mod, g_w_mod, m_w_mod, v_w_mod, comm=reduce_2(None, parts_win))
    parts = parts_win + parts_mix2 + parts_conv + parts_up0 + parts_dn0 + parts_up1 + parts_dn1
    slots = [*slots_win, *slots_mix2, *slots_conv, *slots_up0, *slots_dn0, *slots_up1, *slots_dn1]
    fulls = [sum_slots("reduce_sum_%d" % i, s, t, ids) for i, (s, t) in enumerate(zip(slots, parts))]
    full = [f.reshape(-1, f.shape[-1]) for f in reduce_3("reduce_g_3", fulls)]
    gWin, gWglu, gWout, gWpw1, gWpw2, gWup0, gWdn0, gWup1, gWdn1 = full

    grads = {
        "c_ctx": g_c_ctx, "w_mod": g_w_mod, "b_mod": g_b_mod, "g_mix": ssum["g_mix"], "g_ffn": ssum["g_ffn"],
        "w_in": gWin[None], "ssm_lam_re": g_lam_re[None], "ssm_lam_im": g_lam_im[None], "ssm_log_dt": g_log_dt[None],
        "ssm_b_re": g_b_re[None], "ssm_b_im": g_b_im[None], "ssm_c_re": ssum["c_re"][None], "ssm_c_im": ssum["c_im"][None],
        "ssm_d": ssum["ssm_d"], "ssm_w_glu": gWglu[None], "na_rpb": ssum["na_rpb"][None], "w_out": gWout[None],
        "cv_w_pw1": gWpw1[None], "cv_dw_w": my_cols(ssum["cv_dw_w"])[None], "cv_dw_b": my_cols(ssum["cv_dw_b"]),
        "cv_ln_g": my_cols(ssum["cv_ln_g"]), "cv_ln_b": my_cols(ssum["cv_ln_b"]), "cv_w_pw2": gWpw2[None],
        "ffn_w_up": jnp.stack([gWup0, gWup1]), "ffn_conv_w": my_cols(ssum["ffn_conv_w"]), "ffn_conv_b": ssum["ffn_conv_b"],
        "ffn_w_down": jnp.stack([gWdn0, gWdn1]), "g_out": ssum["g_out"][0],
    }
    grads = {k: grads[k].reshape(p[k].shape) for k in _WEIGHTS}

    large = [k for k in _WEIGHTS if p[k].size >= (1 << 18) or k == "w_mod"]
    tiny = [k for k in _WEIGHTS if k not in large]
    for k in large:
        if k != "w_mod":
            delta[k], new_m[k], new_v[k] = adamw("adamw_" + k, p[k], grads[k], p["m_" + k], p["v_" + k])
    packs = [_pack([src[pre + k] for k in tiny], 1024) for src, pre in ((p, ""), (grads, ""), (p, "m_"), (p, "v_"))]
    outs = adamw("adamw_small", *packs)
    shapes = [p[k].shape for k in tiny]
    for dst, buf in zip((delta, new_m, new_v), outs):
        dst.update(zip(tiny, _unpack(buf, shapes)))

    return (loss, dx0[None], *[grads[k] for k in _WEIGHTS], *[delta[k] for k in _WEIGHTS],
            *[new_m[k] for k in _WEIGHTS], *[new_v[k] for k in _WEIGHTS])
```
